```python
import math
import jax, jax.numpy as jnp
from jax import lax
import numpy as np

D_MODEL = 1024
BATCH = 16
SEQ = 256
DEPTH = 1
DEC_BATCH = 4
DEC_SEQ = 1024
PAST_LEN = 512

GRID_W = 64
D_MIX = D_MODEL
ML_WIDTH = D_MIX // 2
ML_HEADS = 4
ML_HEAD_DIM = ML_WIDTH // ML_HEADS
ML_CHUNK = 64
HY_WIDTH = D_MIX - ML_WIDTH
HY_ORDER = 2
HY_SHORT_K = 3
HY_EMB = 33
HY_BANDS = (HY_EMB - 1) // 2
HY_FILTER_HIDDEN = 64
HY_MOD_SHIFT = 0.05
N_GROUPS = 4
EXPERTS_PER_GROUP = 4
N_EXPERTS = N_GROUPS * EXPERTS_PER_GROUP
EXPERT_FF = 512
TOP_K_FINE = 2
N_MOD = 6
EPS = 1e-6
ML_QKVO_COLS = 4 * ML_WIDTH
ML_GATE_COLS = 4 * ML_HEADS
HY_COLS = (HY_ORDER + 1) * HY_WIDTH
IN_COLS = ML_QKVO_COLS + ML_GATE_COLS + HY_COLS

kernel_name = "hymba_mlstm_hyena_hmoe_diffusion_step"


def rms_norm(x, g):
    xf = x.astype(jnp.float32)
    y = xf * lax.rsqrt(jnp.mean(xf * xf, axis=-1, keepdims=True) + EPS)
    return (y * g.astype(jnp.float32)).astype(x.dtype)


def ada_mod(cvec, w_ada, b_ada):
    m = jax.nn.silu(cvec) @ w_ada + b_ada
    return [a[:, None, :] for a in jnp.split(m, N_MOD, axis=-1)]


def _mlstm_chunk(carry, xs):
    m_mat0, n0, m0 = carry
    q, k, v, ig, lf = xs
    ch = q.shape[2]
    b = jnp.cumsum(lf, axis=-1)
    lower = jnp.tril(jnp.ones((ch, ch), dtype=bool))
    logd = jnp.where(lower, b[..., :, None] - b[..., None, :] + ig[..., None, :], -jnp.inf)
    inter = b + m0[..., None]
    m = jnp.maximum(inter, jnp.max(logd, axis=-1))
    s = jnp.einsum('bhjd,bhsd->bhjs', q, k) * jnp.exp(logd - m[..., None])
    sc_inter = jnp.exp(inter - m)
    num = sc_inter[..., None] * jnp.einsum('bhjd,bhde->bhje', q, m_mat0) + jnp.einsum('bhjs,bhse->bhje', s, v)
    den = sc_inter * jnp.einsum('bhjd,bhd->bhj', q, n0) + jnp.sum(s, axis=-1)
    h = num / jnp.maximum(jnp.abs(den), jnp.exp(-m))[..., None]
    b_last = b[..., -1]
    logw = b_last[..., None] - b + ig
    m_new = jnp.maximum(b_last + m0, jnp.max(logw, axis=-1))
    w = jnp.exp(logw - m_new[..., None])
    decay = jnp.exp(b_last + m0 - m_new)
    m_mat = decay[..., None, None] * m_mat0 + jnp.einsum('bhs,bhsd,bhse->bhde', w, k, v)
    n_new = decay[..., None] * n0 + jnp.einsum('bhs,bhsd->bhd', w, k)
    return (m_mat, n_new, m_new), h


def mlstm_scan(q, k, v, ig, lf, state):
    bsz, nh, seq_len, _ = q.shape
    nc = seq_len // ML_CHUNK

    def chunks(a):
        a = a.reshape(a.shape[:2] + (nc, ML_CHUNK) + a.shape[3:])
        return jnp.moveaxis(a, 2, 0)

    st, h = lax.scan(_mlstm_chunk, state, (chunks(q), chunks(k), chunks(v), chunks(ig), chunks(lf)))
    h = jnp.moveaxis(h, 0, 2).reshape(bsz, nh, seq_len, -1)
    return h, st


def short_conv(u, w, grid):
    bsz, seq_len, ch = u.shape
    pad = HY_SHORT_K // 2
    if grid:
        rows = seq_len // GRID_W
        p = jnp.pad(u.reshape(bsz, rows, GRID_W, ch), ((0, 0), (0, 0), (pad, pad), (0, 0)))
        out = sum(w[j] * p[:, :, j:j + GRID_W] for j in range(HY_SHORT_K))
        return out.reshape(bsz, seq_len, ch)
    p = jnp.pad(u, ((0, 0), (pad, pad), (0, 0)))
    return sum(w[j] * p[:, j:j + seq_len] for j in range(HY_SHORT_K))


def hyena_filters(seq_len, p):
    f32 = jnp.float32
    t = jnp.linspace(0.0, 1.0, seq_len, dtype=f32)[:, None]
    wpos = 2.0 * math.pi * jnp.arange(seq_len, dtype=f32)[:, None] / seq_len
    bands = jnp.linspace(1e-4, HY_BANDS - 1, HY_BANDS, dtype=f32)[None, :]
    z = jnp.concatenate([t, jnp.cos(bands * wpos), -jnp.sin(bands * wpos)], axis=-1)
    h = jnp.sin(z @ p['hy_f_w1'].astype(f32) + p['hy_f_b1'].astype(f32))
    h = jnp.sin(h @ p['hy_f_w2'].astype(f32) + p['hy_f_b2'].astype(f32))
    h = (h @ p['hy_f_w3'].astype(f32) + p['hy_f_b3'].astype(f32)).reshape(seq_len, HY_ORDER, 2, HY_WIDTH)
    window = jnp.exp(-t[:, :, None, None] * jnp.abs(p['hy_decay'].astype(f32))) + HY_MOD_SHIFT
    h = h * window
    h = h * lax.rsqrt(jnp.sum(h * h, axis=(0, 2), keepdims=True) + EPS)
    return jnp.transpose(h, (1, 2, 0, 3))


def long_conv(u, h_pos, h_neg, bias):
    seq_len, ch = h_pos.shape
    filt_2l = jnp.concatenate([h_pos[:1] + h_neg[:1], h_pos[1:], jnp.zeros((1, ch), h_pos.dtype), h_neg[:0:-1]], axis=0)
    uf = u.astype(jnp.float32)
    y = jnp.fft.irfft(jnp.fft.rfft(uf, n=2 * seq_len, axis=1) * jnp.fft.rfft(filt_2l, axis=0)[None],
                      n=2 * seq_len, axis=1)[:, :seq_len]
    return (y + bias.astype(jnp.float32) * uf).astype(u.dtype)


def mixer(xm, p, state, grid):
    bsz, seq_len, _ = xm.shape
    f32 = jnp.float32
    proj = xm @ p['w_in']
    q, k, v, o = [proj[..., i * ML_WIDTH:(i + 1) * ML_WIDTH] for i in range(4)]
    gate_pre = proj[..., ML_QKVO_COLS:ML_QKVO_COLS + ML_GATE_COLS]
    hy_in = proj[..., ML_QKVO_COLS + ML_GATE_COLS:]

    def heads(a):
        return a.reshape(bsz, seq_len, ML_HEADS, ML_HEAD_DIM).transpose(0, 2, 1, 3).astype(f32)
    qh, kh, vh = heads(q), heads(k) * (ML_HEAD_DIM ** -0.5), heads(v)
    g = (gate_pre.reshape(bsz, seq_len, 4, ML_HEADS).astype(f32) + p['ml_gate_bias'].astype(f32)).transpose(0, 2, 3, 1)
    c_st, n_st, m_st = [a.astype(f32) for a in state]
    h_f, (cf, nf, mf) = mlstm_scan(qh, kh, vh, g[:, 0], jax.nn.log_sigmoid(g[:, 1]), (c_st[:, 0], n_st[:, 0], m_st[:, 0]))
    rev = lambda a: jnp.flip(a, axis=2)
    h_b, (cb, nb, mb) = mlstm_scan(rev(qh), rev(kh), rev(vh), rev(g[:, 2]), rev(jax.nn.log_sigmoid(g[:, 3])),
                                   (c_st[:, 1], n_st[:, 1], m_st[:, 1]))
    h = h_f + rev(h_b)
    h = h * lax.rsqrt(jnp.mean(h * h, axis=-1, keepdims=True) + EPS)
    y_ml = (h.transpose(0, 2, 1, 3).reshape(bsz, seq_len, ML_WIDTH) * p['ml_head_gain'].astype(f32)
            * jax.nn.sigmoid(o.astype(f32))).astype(xm.dtype)
    new_state = (jnp.stack([cf, cb], axis=1), jnp.stack([nf, nb], axis=1), jnp.stack([mf, mb], axis=1))

    u = short_conv(hy_in, p['hy_conv_w'], grid)
    parts = jnp.split(u, HY_ORDER + 1, axis=-1)
    filt = hyena_filters(seq_len, p)
    z = parts[-1]
    for n in range(HY_ORDER):
        z = parts[n] * long_conv(z, filt[n, 0], filt[n, 1], p['hy_bias'][n])

    y = jnp.concatenate([y_ml, z], axis=-1) @ p['w_out']
    return y, new_state


def hier_moe(x, p):
    bsz, seq_len, d = x.shape
    t = bsz * seq_len
    xt = x.reshape(t, d)
    p_coarse = jax.nn.softmax((xt @ p['w_rc'] + p['b_rc']).astype(jnp.float32), axis=-1)
    p_grp, g_idx = lax.top_k(p_coarse, 1)
    l_fine = (xt @ p['w_rf'] + p['b_rf']).astype(jnp.float32).reshape(t, N_GROUPS, EXPERTS_PER_GROUP)
    l_sel = jnp.take_along_axis(l_fine, g_idx[:, :, None], axis=1)[:, 0]
    top_l, e_idx = lax.top_k(l_sel, TOP_K_FINE)
    w_sel = jax.nn.softmax(top_l, axis=-1) * p_grp
    expert_ids = g_idx * EXPERTS_PER_GROUP + e_idx
    gates = jnp.sum(jax.nn.one_hot(expert_ids, N_EXPERTS, dtype=jnp.float32) * w_sel[..., None], axis=1)
    hg = jnp.einsum('td,edf->tef', xt, p['w_gate'])
    hu = jnp.einsum('td,edf->tef', xt, p['w_up'])
    act = jax.nn.silu(hg) * hu * gates[:, :, None].astype(x.dtype)
    return jnp.einsum('tef,efd->td', act, p['w_down']).reshape(bsz, seq_len, d)


def trunk_layer(x, cvec, p, state, grid):
    sh1, sc1, g1, sh2, sc2, g2 = ada_mod(cvec, p['w_ada'], p['b_ada'])
    h = rms_norm(x, p['g_norm'][0]) * (1.0 + sc1) + sh1
    y, new_state = mixer(h, p, state, grid)
    x = x + g1 * rms_norm(y, p['g_norm'][1])
    h = rms_norm(x, p['g_norm'][2]) * (1.0 + sc2) + sh2
    x = x + g2 * rms_norm(hier_moe(h, p), p['g_norm'][3])
    return x, new_state


def setup_inputs(seed: int = 0) -> dict:
    key = jax.random.key(seed)
    ks = jax.random.split(key, 32)
    nrm = lambda k, s, sc: jax.random.normal(k, s, jnp.float32) * sc
    fg = jnp.linspace(3.0, 6.0, ML_HEADS, dtype=jnp.float32)
    gate_bias = jnp.stack([jnp.zeros((ML_HEADS,)), fg, jnp.zeros((ML_HEADS,)), fg], axis=0)[None]
    decay_base = jnp.linspace(-math.log(1e-2) / 1.5, -math.log(1e-2) / 0.3, HY_WIDTH, dtype=jnp.float32)
    return {
        'x_prompt': nrm(ks[0], (BATCH, SEQ, D_MODEL), 1.0),
        'x_sample': nrm(ks[1], (DEC_BATCH, DEC_SEQ, D_MODEL), 1.0),
        'state_C': nrm(ks[2], (DEC_BATCH, DEPTH, 2, ML_HEADS, ML_HEAD_DIM, ML_HEAD_DIM), 0.05),
        'state_n': nrm(ks[3], (DEC_BATCH, DEPTH, 2, ML_HEADS, ML_HEAD_DIM), 0.1),
        'state_m': nrm(ks[4], (DEC_BATCH, DEPTH, 2, ML_HEADS), 0.5),
        'c': nrm(ks[5], (DEC_BATCH, D_MODEL), 1.0),
        'c_ctx': nrm(ks[6], (D_MODEL,), 1.0),
        'w_ada': nrm(ks[7], (DEPTH, D_MODEL, N_MOD * D_MODEL), 0.5 * D_MODEL ** -0.5),
        'b_ada': nrm(ks[8], (DEPTH, N_MOD * D_MODEL), 0.01),
        'g_norm': 1.0 + nrm(ks[9], (DEPTH, 4, D_MODEL), 0.05),
        'w_in': nrm(ks[10], (DEPTH, D_MODEL, IN_COLS), D_MODEL ** -0.5),
        'ml_gate_bias': gate_bias + nrm(ks[11], (DEPTH, 4, ML_HEADS), 0.1),
        'ml_head_gain': 1.0 + nrm(ks[12], (DEPTH, ML_WIDTH), 0.05),
        'hy_conv_w': nrm(ks[13], (DEPTH, HY_SHORT_K, HY_COLS), 0.5),
        'hy_f_w1': nrm(ks[14], (DEPTH, HY_EMB, HY_FILTER_HIDDEN), HY_EMB ** -0.5),
        'hy_f_b1': nrm(ks[15], (DEPTH, HY_FILTER_HIDDEN), 0.1),
        'hy_f_w2': nrm(ks[16], (DEPTH, HY_FILTER_HIDDEN, HY_FILTER_HIDDEN), HY_FILTER_HIDDEN ** -0.5),
        'hy_f_b2': nrm(ks[17], (DEPTH, HY_FILTER_HIDDEN), 0.1),
        'hy_f_w3': nrm(ks[18], (DEPTH, HY_FILTER_HIDDEN, HY_ORDER * 2 * HY_WIDTH), HY_FILTER_HIDDEN ** -0.5),
        'hy_f_b3': nrm(ks[19], (DEPTH, HY_ORDER * 2 * HY_WIDTH), 0.1),
        'hy_decay': decay_base * (1.0 + nrm(ks[20], (DEPTH, HY_ORDER, 2, HY_WIDTH), 0.05)),
        'hy_bias': nrm(ks[21], (DEPTH, HY_ORDER, HY_WIDTH), 0.5),
        'w_out': nrm(ks[22], (DEPTH, D_MIX, D_MODEL), D_MIX ** -0.5),
        'w_rc': nrm(ks[23], (DEPTH, D_MODEL, N_GROUPS), D_MODEL ** -0.5),
        'b_rc': nrm(ks[24], (DEPTH, N_GROUPS), 0.01),
        'w_rf': nrm(ks[25], (DEPTH, D_MODEL, N_EXPERTS), D_MODEL ** -0.5),
        'b_rf': nrm(ks[26], (DEPTH, N_EXPERTS), 0.01),
        'w_gate': nrm(ks[27], (DEPTH, N_EXPERTS, D_MODEL, EXPERT_FF), D_MODEL ** -0.5),
        'w_up': nrm(ks[28], (DEPTH, N_EXPERTS, D_MODEL, EXPERT_FF), D_MODEL ** -0.5),
        'w_down': nrm(ks[29], (DEPTH, N_EXPERTS, EXPERT_FF, D_MODEL), EXPERT_FF ** -0.5),
    }


def reference(x_prompt, x_sample, state_C, state_n, state_m, c, c_ctx, w_ada, b_ada, g_norm, w_in,
              ml_gate_bias, ml_head_gain, hy_conv_w, hy_f_w1, hy_f_b1, hy_f_w2, hy_f_b2, hy_f_w3, hy_f_b3,
              hy_decay, hy_bias, w_out, w_rc, b_rc, w_rf, b_rf, w_gate, w_up, w_down):
    bsz = x_prompt.shape[0]
    y_p, y_s = x_prompt, x_sample
    new_c, new_n, new_m = [], [], []
    for l in range(DEPTH):
        p = {'w_ada': w_ada[l], 'b_ada': b_ada[l], 'g_norm': g_norm[l], 'w_in': w_in[l],
             'ml_gate_bias': ml_gate_bias[l], 'ml_head_gain': ml_head_gain[l], 'hy_conv_w': hy_conv_w[l],
             'hy_f_w1': hy_f_w1[l], 'hy_f_b1': hy_f_b1[l], 'hy_f_w2': hy_f_w2[l], 'hy_f_b2': hy_f_b2[l],
             'hy_f_w3': hy_f_w3[l], 'hy_f_b3': hy_f_b3[l], 'hy_decay': hy_decay[l], 'hy_bias': hy_bias[l],
             'w_out': w_out[l], 'w_rc': w_rc[l], 'b_rc': b_rc[l], 'w_rf': w_rf[l], 'b_rf': b_rf[l],
             'w_gate': w_gate[l], 'w_up': w_up[l], 'w_down': w_down[l]}
        st0 = (jnp.zeros((bsz, 2, ML_HEADS, ML_HEAD_DIM, ML_HEAD_DIM), jnp.float32),
               jnp.zeros((bsz, 2, ML_HEADS, ML_HEAD_DIM), jnp.float32),
               jnp.zeros((bsz, 2, ML_HEADS), jnp.float32))
        y_p, st = trunk_layer(y_p, c_ctx[None, :], p, st0, grid=False)
        new_c.append(st[0]); new_n.append(st[1]); new_m.append(st[2])
        y_s, _ = trunk_layer(y_s, c, p, (state_C[:, l], state_n[:, l], state_m[:, l]), grid=True)
    new_state_C = jnp.stack(new_c, axis=1)
    new_state_n = jnp.stack(new_n, axis=1)
    new_state_m = jnp.stack(new_m, axis=1)
    return (y_p, y_s, new_state_C, new_state_n, new_state_m)
```

```python
import functools
import math

import jax
import jax.numpy as jnp
from jax import lax
from jax.experimental import pallas as pl
from jax.experimental.pallas import tpu as pltpu

F32 = jnp.float32
BF16 = jnp.bfloat16

D_MODEL = 1024
BATCH = 16
SEQ = 256
DEC_BATCH = 4
DEC_SEQ = 1024
GRID_W = 64
ML_WIDTH = 512
ML_HEADS = 4
ML_HEAD_DIM = 128
HY_WIDTH = 512
HY_ORDER = 2
HY_EMB = 33
HY_BANDS = 16
HY_FILTER_HIDDEN = 64
HY_MOD_SHIFT = 0.05
N_GROUPS = 4
EXPERTS_PER_GROUP = 4
N_EXPERTS = 16
EXPERT_FF = 512
N_MOD = 6
EPS = 1e-6
ML_QKVO_COLS = 4 * ML_WIDTH
ML_GATE_COLS = 4 * ML_HEADS
HY_COLS = 3 * HY_WIDTH
MAIN_COLS = ML_QKVO_COLS + HY_COLS

T_PROMPT = BATCH * SEQ
T_SAMPLE = DEC_BATCH * DEC_SEQ
T_ALL = T_PROMPT + T_SAMPLE
TILE = 256
N_TILES_P = T_PROMPT // TILE
N_TILES = T_ALL // TILE
MOD_ROWS = 8
K_SCALE = ML_HEAD_DIM ** -0.5
VMEM_LIMIT = 56 * 1024 * 1024


def _cparams(sem):
    return pltpu.CompilerParams(dimension_semantics=sem, vmem_limit_bytes=VMEM_LIMIT)


def _split2(x):
    hi = x.astype(BF16)
    lo = (x - hi.astype(F32)).astype(BF16)
    return hi, lo


def _dot(a, b):
    return jnp.dot(a, b, preferred_element_type=F32)


def _dot_nt(a, b):
    return lax.dot_general(a, b, (((1,), (1,)), ((), ())), preferred_element_type=F32)


def _dot_tn(a, b):
    return lax.dot_general(a, b, (((0,), (0,)), ((), ())), preferred_element_type=F32)


def _dot3(a, b):
    ah, al = _split2(a)
    bh, bl = _split2(b)
    return _dot(ah, bh) + _dot(al, bh) + _dot(ah, bl)


def _dot3_nt(a, b):
    ah, al = _split2(a)
    bh, bl = _split2(b)
    return _dot_nt(ah, bh) + _dot_nt(al, bh) + _dot_nt(ah, bl)


def _dot_exact_lhs(t, x):
    x1 = x.astype(BF16)
    r1 = x - x1.astype(F32)
    x2 = r1.astype(BF16)
    x3 = (r1 - x2.astype(F32)).astype(BF16)
    return _dot(t, x1) + _dot(t, x2) + _dot(t, x3)


def _dot_exact_rhs(x, t):
    x1 = x.astype(BF16)
    r1 = x - x1.astype(F32)
    x2 = r1.astype(BF16)
    x3 = (r1 - x2.astype(F32)).astype(BF16)
    return _dot(x1, t) + _dot(x2, t) + _dot(x3, t)


def _rms(x, g):
    return x * lax.rsqrt(jnp.mean(x * x, axis=-1, keepdims=True) + EPS) * g


def _mod_row_of_tile(i, tiles_per_sample_seq, n_prompt_tiles):
    return jnp.where(i < n_prompt_tiles, 0, 1 + (i - n_prompt_tiles) // tiles_per_sample_seq)


def _ada_kernel(cv_ref, w_ref, b_ref, o_ref):
    cv = cv_ref[...]
    s = cv * jax.nn.sigmoid(cv)
    o_ref[...] = _dot3(s, w_ref[...]) + b_ref[...]


def _ada(cv, w_ada, b_ada):
    n = N_MOD * D_MODEL
    return pl.pallas_call(
        _ada_kernel,
        out_shape=jax.ShapeDtypeStruct((MOD_ROWS, n), F32),
        grid=(N_MOD,),
        in_specs=[pl.BlockSpec((MOD_ROWS, D_MODEL), lambda j: (0, 0)),
                  pl.BlockSpec((D_MODEL, D_MODEL), lambda j: (0, j)),
                  pl.BlockSpec((1, D_MODEL), lambda j: (0, j))],
        out_specs=pl.BlockSpec((MOD_ROWS, D_MODEL), lambda j: (0, j)),
        compiler_params=_cparams(("arbitrary",)),
        name="ada_mod",
    )(cv, w_ada, b_ada.reshape(1, n))


def _log_sigmoid(x):
    return jnp.minimum(x, 0.0) - jnp.log1p(jnp.exp(-jnp.abs(x)))


def _inproj_kernel(xp_ref, xs_ref, m_ref, gn_ref, w_ref, wg_ref, wgt_ref, gb_ref, gbt_ref,
                   proj_ref, gate_ref, gatet_ref):
    i = pl.program_id(0)
    x = jnp.where(i < N_TILES_P, xp_ref[...], xs_ref[...])
    h = _rms(x, gn_ref[0:1, :]) * (1.0 + m_ref[0, 1:2, :]) + m_ref[0, 0:1, :]
    hb = h.astype(BF16)
    cb = 512
    for j in range(MAIN_COLS // cb):
        proj_ref[:, j * cb:(j + 1) * cb] = _dot(hb, w_ref[:, j * cb:(j + 1) * cb]).astype(BF16)
    hl = (h - hb.astype(F32)).astype(BF16)
    wg = wg_ref[...]
    wgh, wgl = _split2(wg)
    g = _dot(hb, wgh) + _dot(hl, wgh) + _dot(hb, wgl) + gb_ref[...]
    col = lax.broadcasted_iota(jnp.int32, g.shape, 1)
    gate_ref[...] = jnp.where((col % 8) >= 4, _log_sigmoid(g), g)
    wgt = wgt_ref[...]
    wth, wtl = _split2(wgt)
    gt = _dot_nt(wth, hb) + _dot_nt(wth, hl) + _dot_nt(wtl, hb) + gbt_ref[...]
    row = lax.broadcasted_iota(jnp.int32, gt.shape, 0)
    gatet_ref[0] = jnp.where((row % 8) >= 4, _log_sigmoid(gt), gt)


def _inproj(xp, xs, mods3, g_norm, w_main, wg, wgt, gb, gbt):
    tps = DEC_SEQ // TILE
    return pl.pallas_call(
        _inproj_kernel,
        out_shape=(jax.ShapeDtypeStruct((T_ALL, MAIN_COLS), BF16),
                   jax.ShapeDtypeStruct((T_ALL, ML_GATE_COLS), F32),
                   jax.ShapeDtypeStruct((N_TILES, ML_GATE_COLS, TILE), F32)),
        grid=(N_TILES,),
        in_specs=[pl.BlockSpec((TILE, D_MODEL), lambda i: (jnp.minimum(i, N_TILES_P - 1), 0)),
                  pl.BlockSpec((TILE, D_MODEL), lambda i: (jnp.maximum(i - N_TILES_P, 0), 0)),
                  pl.BlockSpec((1, N_MOD, D_MODEL), lambda i: (_mod_row_of_tile(i, tps, N_TILES_P), 0, 0)),
                  pl.BlockSpec((4, D_MODEL), lambda i: (0, 0)),
                  pl.BlockSpec((D_MODEL, MAIN_COLS), lambda i: (0, 0)),
                  pl.BlockSpec((D_MODEL, ML_GATE_COLS), lambda i: (0, 0)),
                  pl.BlockSpec((ML_GATE_COLS, D_MODEL), lambda i: (0, 0)),
                  pl.BlockSpec((1, ML_GATE_COLS), lambda i: (0, 0)),
                  pl.BlockSpec((ML_GATE_COLS, 1), lambda i: (0, 0))],
        out_specs=(pl.BlockSpec((TILE, MAIN_COLS), lambda i: (i, 0)),
                   pl.BlockSpec((TILE, ML_GATE_COLS), lambda i: (i, 0)),
                   pl.BlockSpec((1, ML_GATE_COLS, TILE), lambda i: (i, 0, 0))),
        compiler_params=_cparams(("arbitrary",)),
        name="in_proj",
    )(xp, xs, mods3, g_norm, w_main, wg, wgt, gb, gbt)


def _mlstm_kernel(*refs, seq_len, has_state):
    if has_state:
        (q_ref, k_ref, v_ref, o_ref, g_ref, gt_ref, gain_ref, c0_ref, n0_ref, m0_ref,
         y_ref, c_ref, n_ref, m_ref, hf_ref, hb_ref, cs_ref, ns_ref, ms_ref) = refs
    else:
        (q_ref, k_ref, v_ref, o_ref, g_ref, gt_ref, gain_ref,
         y_ref, c_ref, n_ref, m_ref, hf_ref, hb_ref, cs_ref, ns_ref, ms_ref) = refs
    ch = TILE
    nc = seq_len // ch
    hd = ML_HEAD_DIM
    ri = lax.broadcasted_iota(jnp.int32, (ch, ch), 0)
    ci = lax.broadcasted_iota(jnp.int32, (ch, ch), 1)
    lower = ci <= ri
    upper = ci >= ri
    t_low = jnp.where(lower, 1.0, 0.0).astype(BF16)
    t_up = jnp.where(upper, 1.0, 0.0).astype(BF16)

    for d in range(2):
        for h in range(ML_HEADS):
            r = d * ML_HEADS + h
            if has_state:
                cs_ref[r] = c0_ref[0, d, h]
                ns_ref[r] = n0_ref[0, d, h:h + 1, :]
                ms_ref[r] = jnp.broadcast_to(m0_ref[0, r:r + 1, :], (1, hd))
            else:
                cs_ref[r] = jnp.zeros((hd, hd), F32)
                ns_ref[r] = jnp.zeros((1, hd), F32)
                ms_ref[r] = jnp.zeros((1, hd), F32)

    def step(t, carry):
        for d in range(2):
            c = t if d == 0 else nc - 1 - t
            r0 = pl.multiple_of(c * ch, ch)
            rows = pl.ds(r0, ch)
            gcol = g_ref[rows, :]
            grow = gt_ref[c]
            tmat_c = t_low if d == 0 else t_up
            tmat_r = t_up if d == 0 else t_low
            bcol_all = _dot_exact_lhs(tmat_c, gcol)
            brow_all = _dot_exact_rhs(grow, tmat_r)
            mask = lower if d == 0 else upper
            hacc_ref = hf_ref if d == 0 else hb_ref
            for h in range(ML_HEADS):
                r = d * ML_HEADS + h
                fcol = (1 + 2 * d) * ML_HEADS + h
                icol = (2 * d) * ML_HEADS + h
                cols = slice(h * hd, (h + 1) * hd)
                q = q_ref[rows, cols]
                k = k_ref[rows, cols]
                v = v_ref[rows, cols]
                c_prev = cs_ref[r]
                n_prev = ns_ref[r]
                m_prev = ms_ref[r][:, 0:1]
                b_col = bcol_all[:, fcol:fcol + 1]
                b_row = brow_all[fcol:fcol + 1, :]
                ig_row = grow[icol:icol + 1, :]
                ig_col = gcol[:, icol:icol + 1]
                logd = jnp.where(mask, b_col - b_row + ig_row, -jnp.inf)
                inter = b_col + m_prev
                m_pos = jnp.maximum(inter, jnp.max(logd, axis=-1, keepdims=True))
                s = _dot_nt(q, k) * K_SCALE * jnp.exp(logd - m_pos)
                sc_inter = jnp.exp(inter - m_pos)
                qf = q.astype(F32)
                num = sc_inter * _dot(q, c_prev.astype(BF16)) + _dot(s.astype(BF16), v)
                den = (sc_inter * jnp.sum(qf * n_prev, axis=-1, keepdims=True)
                       + jnp.sum(s, axis=-1, keepdims=True))
                hh = num / jnp.maximum(jnp.abs(den), jnp.exp(-m_pos))
                hacc_ref[rows, cols] = hh
                b_last = b_col[ch - 1:ch, :] if d == 0 else b_col[0:1, :]
                logw = b_last - b_col + ig_col
                m_new = jnp.maximum(b_last + m_prev, jnp.max(logw, axis=0, keepdims=True))
                w = jnp.exp(logw - m_new)
                decay = jnp.exp(b_last + m_prev - m_new)
                kw = k.astype(F32) * (w * K_SCALE)
                cs_ref[r] = decay * c_prev + _dot_tn(kw.astype(BF16), v)
                ns_ref[r] = decay * n_prev + jnp.sum(kw, axis=0, keepdims=True)
                ms_ref[r] = jnp.broadcast_to(m_new, (1, hd))
        return carry

    lax.fori_loop(0, nc, step, 0)

    for d in range(2):
        for h in range(ML_HEADS):
            r = d * ML_HEADS + h
            c_ref[0, d, h] = cs_ref[r]
            n_ref[0, d, h:h + 1, :] = ns_ref[r]
            m_ref[0, r:r + 1, :] = ms_ref[r]
    for h in range(ML_HEADS):
        cols = slice(h * hd, (h + 1) * hd)
        hh = hf_ref[:, cols] + hb_ref[:, cols]
        hh = hh * lax.rsqrt(jnp.mean(hh * hh, axis=-1, keepdims=True) + EPS)
        y = hh * gain_ref[:, cols] * jax.nn.sigmoid(o_ref[:, cols].astype(F32))
        y_ref[:, cols] = y.astype(BF16)


def _mlstm(proj, gates, gates_t, gain, state, seq_len, n_seq, row_block_off):
    has_state = state is not None
    tiles = seq_len // TILE
    off = row_block_off
    qkvo_specs = [pl.BlockSpec((seq_len, ML_WIDTH), functools.partial(lambda b, j: (off + b, j), j=j))
                  for j in range(4)]
    in_specs = qkvo_specs + [
        pl.BlockSpec((seq_len, ML_GATE_COLS), lambda b: (off + b, 0)),
        pl.BlockSpec((tiles, ML_GATE_COLS, TILE), lambda b: (off + b, 0, 0)),
        pl.BlockSpec((1, ML_WIDTH), lambda b: (0, 0)),
    ]
    args = [proj, proj, proj, proj, gates, gates_t, gain]
    if has_state:
        c0, n0, m0 = state
        in_specs += [
            pl.BlockSpec((1, 2, ML_HEADS, ML_HEAD_DIM, ML_HEAD_DIM), lambda b: (b, 0, 0, 0, 0)),
            pl.BlockSpec((1, 2, ML_HEADS, ML_HEAD_DIM), lambda b: (b, 0, 0, 0)),
            pl.BlockSpec((1, 2 * ML_HEADS, 1), lambda b: (b, 0, 0)),
        ]
        args += [c0, n0, m0]
    out_shape = (jax.ShapeDtypeStruct((n_seq * seq_len, ML_WIDTH), BF16),
                 jax.ShapeDtypeStruct((n_seq, 2, ML_HEADS, ML_HEAD_DIM, ML_HEAD_DIM), F32),
                 jax.ShapeDtypeStruct((n_seq, 2, ML_HEADS, ML_HEAD_DIM), F32),
                 jax.ShapeDtypeStruct((n_seq, 2 * ML_HEADS, ML_HEAD_DIM), F32))
    out_specs = (pl.BlockSpec((seq_len, ML_WIDTH), lambda b: (b, 0)),
                 pl.BlockSpec((1, 2, ML_HEADS, ML_HEAD_DIM, ML_HEAD_DIM), lambda b: (b, 0, 0, 0, 0)),
                 pl.BlockSpec((1, 2, ML_HEADS, ML_HEAD_DIM), lambda b: (b, 0, 0, 0)),
                 pl.BlockSpec((1, 2 * ML_HEADS, ML_HEAD_DIM), lambda b: (b, 0, 0)))
    scratch = [pltpu.VMEM((seq_len, ML_WIDTH), F32), pltpu.VMEM((seq_len, ML_WIDTH), F32),
               pltpu.VMEM((2 * ML_HEADS, ML_HEAD_DIM, ML_HEAD_DIM), F32),
               pltpu.VMEM((2 * ML_HEADS, 1, ML_HEAD_DIM), F32),
               pltpu.VMEM((2 * ML_HEADS, 1, ML_HEAD_DIM), F32)]
    return pl.pallas_call(
        functools.partial(_mlstm_kernel, seq_len=seq_len, has_state=has_state),
        out_shape=out_shape, grid=(n_seq,), in_specs=in_specs, out_specs=out_specs,
        scratch_shapes=scratch, compiler_params=_cparams(("arbitrary",)),
        name=f"mlstm_{seq_len}",
    )(*args)


def _dft_mats(seq_len):
    lo = 32
    hi = seq_len // lo
    d = jnp.arange(seq_len, dtype=jnp.int32)
    a = jnp.arange(hi, dtype=jnp.int32) * lo
    b = jnp.arange(lo, dtype=jnp.int32)
    scale = math.pi / seq_len
    ang_a = ((a[:, None] * d[None, :]) % (2 * seq_len)).astype(F32) * scale
    ang_b = ((b[:, None] * d[None, :]) % (2 * seq_len)).astype(F32) * scale
    ca, sa, cb, sb = jnp.cos(ang_a), jnp.sin(ang_a), jnp.cos(ang_b), jnp.sin(ang_b)
    cosm = (ca[:, None, :] * cb[None, :, :] - sa[:, None, :] * sb[None, :, :]).reshape(seq_len, seq_len)
    sinm = (sa[:, None, :] * cb[None, :, :] + ca[:, None, :] * sb[None, :, :]).reshape(seq_len, seq_len)
    nyq = jnp.where(d % 2 == 0, 1.0, -1.0).astype(F32)
    krow = jnp.arange(seq_len, dtype=jnp.int32)[:, None]
    sinm = jnp.where(krow == 0, nyq[None, :], sinm)
    f = jnp.concatenate([cosm, sinm], axis=0).astype(BF16)
    cos_t = (ca.T[:, :, None] * cb.T[:, None, :] - sa.T[:, :, None] * sb.T[:, None, :]).reshape(seq_len, seq_len)
    sin_t = (sa.T[:, :, None] * cb.T[:, None, :] + ca.T[:, :, None] * sb.T[:, None, :]).reshape(seq_len, seq_len)
    kcol = jnp.arange(seq_len, dtype=jnp.int32)[None, :]
    sin_t = jnp.where(kcol == 0, nyq[:, None], sin_t)
    ft = jnp.concatenate([cos_t, sin_t], axis=1).astype(BF16)
    return f, ft


def _filter_feats(seq_len):
    t = jnp.linspace(0.0, 1.0, seq_len, dtype=F32)[:, None]
    wpos = 2.0 * math.pi * jnp.arange(seq_len, dtype=F32)[:, None] / seq_len
    bands = jnp.linspace(1e-4, HY_BANDS - 1, HY_BANDS, dtype=F32)[None, :]
    z = jnp.concatenate([t, jnp.cos(bands * wpos), -jnp.sin(bands * wpos)], axis=-1)
    return jnp.pad(z, ((0, 0), (0, 128 - HY_EMB)))


def _filter_kernel(z_ref, w1_ref, b1_ref, w2_ref, b2_ref, w3_ref, b3_ref, dec_ref, f_ref,
                   a_ref, b_ref, d_ref, *, seq_len):
    n = 2 * seq_len
    z = z_ref[...]
    h = jnp.sin(_dot3(z, w1_ref[...]) + b1_ref[...])
    h = jnp.sin(_dot3(h, w2_ref[...]) + b2_ref[...])
    h = _dot3(h, w3_ref[...]) + b3_ref[...]
    t = z[:, 0:1]
    h = h * (jnp.exp(-t * jnp.abs(dec_ref[...])) + HY_MOD_SHIFT)
    ss = jnp.sum(h * h, axis=0, keepdims=True)
    inv = lax.rsqrt(ss[:, :HY_WIDTH] + ss[:, HY_WIDTH:] + EPS)
    hp = h[:, :HY_WIDTH] * inv
    hn = h[:, HY_WIDTH:] * inv
    ssum = hp + hn
    sdif = hp - hn
    hc = _dot(f_ref[0:seq_len, :], ssum.astype(BF16))
    hs = _dot(f_ref[seq_len:n, :], sdif.astype(BF16))
    di = lax.broadcasted_iota(jnp.int32, (seq_len, 1), 0)
    sgn = jnp.where(di % 2 == 0, 1.0, -1.0)
    nyq = jnp.sum(ssum * sgn, axis=0, keepdims=True)
    first = di == 0
    a_ref[0] = hc * jnp.where(first, 1.0 / n, 2.0 / n)
    b_ref[0] = jnp.where(first, 0.0, hs * (2.0 / n))
    d_ref[0] = jnp.where(first, nyq * (1.0 / n), hc * (2.0 / n))


def _hyena_filters(seq_len, f, w1p, b1, w2, b2, w3, b3, dec):
    z = _filter_feats(seq_len)
    hid = HY_FILTER_HIDDEN
    oc = 2 * HY_WIDTH
    full = lambda shape: pl.BlockSpec(shape, lambda o: tuple(0 for _ in shape))
    out = jax.ShapeDtypeStruct((HY_ORDER, seq_len, HY_WIDTH), F32)
    return pl.pallas_call(
        functools.partial(_filter_kernel, seq_len=seq_len),
        out_shape=(out, out, out),
        grid=(HY_ORDER,),
        in_specs=[full((seq_len, 128)), full((128, hid)), full((1, hid)), full((hid, hid)), full((1, hid)),
                  pl.BlockSpec((hid, oc), lambda o: (0, o)),
                  pl.BlockSpec((1, oc), lambda o: (0, o)),
                  pl.BlockSpec((1, oc), lambda o: (0, o)),
                  full((2 * seq_len, seq_len))],
        out_specs=tuple(pl.BlockSpec((1, seq_len, HY_WIDTH), lambda o: (o, 0, 0)) for _ in range(3)),
        compiler_params=_cparams(("arbitrary",)),
        name=f"hyena_filter_{seq_len}",
    )(z, w1p, b1, w2, b2, w3, b3, dec, f)


def _hyena_kernel(x1_ref, x2_ref, v_ref, cw1_ref, cw2_ref, cwv_ref, a_ref, b_ref, d_ref, bias_ref,
                  f_ref, ft_ref, z_ref, *, seq_len, width):
    ti = lax.broadcasted_iota(jnp.int32, (seq_len, 1), 0)
    has_prev = (ti % width) != 0
    has_next = (ti % width) != (width - 1)

    def short_conv(x_ref, w_ref):
        x = x_ref[...].astype(F32)
        prev = jnp.where(has_prev, pltpu.roll(x, 1, axis=0), 0.0)
        nxt = jnp.where(has_next, pltpu.roll(x, seq_len - 1, axis=0), 0.0)
        return w_ref[0:1, :] * prev + w_ref[1:2, :] * x + w_ref[2:3, :] * nxt

    gates = (short_conv(x1_ref, cw1_ref), short_conv(x2_ref, cw2_ref))
    z = short_conv(v_ref, cwv_ref)
    for o in range(HY_ORDER):
        u = _dot(f_ref[...], z.astype(BF16))
        ut = u[:seq_len]
        ub = u[seq_len:]
        a, b, dd = a_ref[o], b_ref[o], d_ref[o]
        yt = ut * a - ub * b
        yb = ut * b + ub * dd
        y = _dot(ft_ref[:, :seq_len], yt.astype(BF16)) + _dot(ft_ref[:, seq_len:], yb.astype(BF16))
        z = gates[o] * (y + bias_ref[o:o + 1, :] * z)
    z_ref[...] = z.astype(BF16)


def _hyena(proj, conv_w, coefs, hy_bias, f, ft, seq_len, n_seq, row_block_off, width):
    cb = 256
    nblk = HY_WIDTH // cb
    base = ML_QKVO_COLS // cb
    off = row_block_off
    a, b, d = coefs

    def col_spec(part):
        return pl.BlockSpec((seq_len, cb), lambda s, j: (off + s, base + part * nblk + j))

    def w_spec(part):
        return pl.BlockSpec((3, cb), lambda s, j: (0, part * nblk + j))

    coef_spec = pl.BlockSpec((HY_ORDER, seq_len, cb), lambda s, j: (0, 0, j))
    return pl.pallas_call(
        functools.partial(_hyena_kernel, seq_len=seq_len, width=width),
        out_shape=jax.ShapeDtypeStruct((n_seq * seq_len, HY_WIDTH), BF16),
        grid=(n_seq, nblk),
        in_specs=[col_spec(0), col_spec(1), col_spec(2), w_spec(0), w_spec(1), w_spec(2),
                  coef_spec, coef_spec, coef_spec,
                  pl.BlockSpec((HY_ORDER, cb), lambda s, j: (0, j)),
                  pl.BlockSpec((2 * seq_len, seq_len), lambda s, j: (0, 0)),
                  pl.BlockSpec((seq_len, 2 * seq_len), lambda s, j: (0, 0))],
        out_specs=pl.BlockSpec((seq_len, cb), lambda s, j: (s, j)),
        compiler_params=_cparams(("arbitrary", "arbitrary")),
        name=f"hyena_conv_{seq_len}",
    )(proj, proj, proj, conv_w, conv_w, conv_w, a, b, d, hy_bias, f, ft)


def _first_max(x, n):
    mx = jnp.max(x, axis=-1, keepdims=True)
    lane = lax.broadcasted_iota(jnp.int32, x.shape, 1).astype(F32)
    idx = jnp.min(jnp.where(x == mx, lane, float(n)), axis=-1, keepdims=True)
    return mx, idx.astype(jnp.int32)


def _outproj_kernel(xp_ref, xs_ref, yp_ref, ys_ref, zp_ref, zs_ref, m_ref, gn_ref, wo_ref, wr_ref, br_ref,
                    x1_ref, h2_ref, gates_ref):
    i = pl.program_id(0)
    is_p = i < N_TILES_P
    x = jnp.where(is_p, xp_ref[...], xs_ref[...])
    yml = jnp.where(is_p, yp_ref[...], ys_ref[...])
    zz = jnp.where(is_p, zp_ref[...], zs_ref[...])
    y = _dot(yml, wo_ref[0:ML_WIDTH, :]) + _dot(zz, wo_ref[ML_WIDTH:, :])
    x1 = x + m_ref[0, 2:3, :] * _rms(y, gn_ref[1:2, :])
    x1_ref[...] = x1
    h2 = _rms(x1, gn_ref[2:3, :]) * (1.0 + m_ref[0, 4:5, :]) + m_ref[0, 3:4, :]
    h2_ref[...] = h2.astype(BF16)
    logits = _dot3(h2, wr_ref[...]) + br_ref[...]
    lc = logits[:, 0:N_GROUPS]
    mx, gi = _first_max(lc, N_GROUPS)
    p_grp = 1.0 / jnp.sum(jnp.exp(lc - mx), axis=-1, keepdims=True)
    lsel = jnp.zeros((TILE, EXPERTS_PER_GROUP), F32)
    for g in range(N_GROUPS):
        lo = N_GROUPS + g * EXPERTS_PER_GROUP
        lsel = jnp.where(gi == g, logits[:, lo:lo + EXPERTS_PER_GROUP], lsel)
    l1, i1 = _first_max(lsel, EXPERTS_PER_GROUP)
    lane4 = lax.broadcasted_iota(jnp.int32, lsel.shape, 1)
    l2, i2 = _first_max(jnp.where(lane4 == i1, -jnp.inf, lsel), EXPERTS_PER_GROUP)
    e2 = jnp.exp(l2 - l1)
    w1 = p_grp / (1.0 + e2)
    w2 = p_grp * e2 / (1.0 + e2)
    lane16 = lax.broadcasted_iota(jnp.int32, (TILE, N_EXPERTS), 1)
    id1 = gi * EXPERTS_PER_GROUP + i1
    id2 = gi * EXPERTS_PER_GROUP + i2
    gates_ref[...] = jnp.where(lane16 == id1, w1, 0.0) + jnp.where(lane16 == id2, w2, 0.0)


def _outproj(xp, xs, yp, ys, zp, zs, mods3, g_norm, w_out, w_r, b_r):
    tps = DEC_SEQ // TILE
    pidx = lambda i: (jnp.minimum(i, N_TILES_P - 1), 0)
    sidx = lambda i: (jnp.maximum(i - N_TILES_P, 0), 0)
    return pl.pallas_call(
        _outproj_kernel,
        out_shape=(jax.ShapeDtypeStruct((T_ALL, D_MODEL), F32),
                   jax.ShapeDtypeStruct((T_ALL, D_MODEL), BF16),
                   jax.ShapeDtypeStruct((T_ALL, N_EXPERTS), F32)),
        grid=(N_TILES,),
        in_specs=[pl.BlockSpec((TILE, D_MODEL), pidx), pl.BlockSpec((TILE, D_MODEL), sidx),
                  pl.BlockSpec((TILE, ML_WIDTH), pidx), pl.BlockSpec((TILE, ML_WIDTH), sidx),
                  pl.BlockSpec((TILE, HY_WIDTH), pidx), pl.BlockSpec((TILE, HY_WIDTH), sidx),
                  pl.BlockSpec((1, N_MOD, D_MODEL), lambda i: (_mod_row_of_tile(i, tps, N_TILES_P), 0, 0)),
                  pl.BlockSpec((4, D_MODEL), lambda i: (0, 0)),
                  pl.BlockSpec((D_MODEL, D_MODEL), lambda i: (0, 0)),
                  pl.BlockSpec((D_MODEL, 128), lambda i: (0, 0)),
                  pl.BlockSpec((1, 128), lambda i: (0, 0))],
        out_specs=(pl.BlockSpec((TILE, D_MODEL), lambda i: (i, 0)),
                   pl.BlockSpec((TILE, D_MODEL), lambda i: (i, 0)),
                   pl.BlockSpec((TILE, N_EXPERTS), lambda i: (i, 0))),
        compiler_params=_cparams(("arbitrary",)),
        name="out_proj_router",
    )(xp, xs, yp, ys, zp, zs, mods3, g_norm, w_out, w_r, b_r)


MOE_TILE = 1024
MOE_TILES_P = T_PROMPT // MOE_TILE
MOE_TILES = T_ALL // MOE_TILE


def _moe_kernel(h_ref, gates_ref, wg_ref, wu_ref, wd_ref, x1_ref, m_ref, gn_ref, op_ref, os_ref, acc_ref):
    i = pl.program_id(0)
    e = pl.program_id(1)

    @pl.when(e == 0)
    def _():
        acc_ref[...] = jnp.zeros_like(acc_ref)

    h = h_ref[...]
    hg = _dot(h, wg_ref[0].astype(BF16))
    hu = _dot(h, wu_ref[0].astype(BF16))
    lane = lax.broadcasted_iota(jnp.int32, (MOE_TILE, N_EXPERTS), 1)
    gate = jnp.sum(jnp.where(lane == e, gates_ref[...], 0.0), axis=-1, keepdims=True)
    act = hg * jax.nn.sigmoid(hg) * hu * gate
    acc_ref[...] += _dot(act.astype(BF16), wd_ref[0].astype(BF16))

    @pl.when(e == N_EXPERTS - 1)
    def _():
        out = x1_ref[...] + m_ref[0, 5:6, :] * _rms(acc_ref[...], gn_ref[3:4, :])

        @pl.when(i < MOE_TILES_P)
        def _():
            op_ref[...] = out

        @pl.when(i >= MOE_TILES_P)
        def _():
            os_ref[...] = out


def _moe(h2, gates, w_gate, w_up, w_down, x1, mods3, g_norm):
    tps = DEC_SEQ // MOE_TILE
    return pl.pallas_call(
        _moe_kernel,
        out_shape=(jax.ShapeDtypeStruct((T_PROMPT, D_MODEL), F32),
                   jax.ShapeDtypeStruct((T_SAMPLE, D_MODEL), F32)),
        grid=(MOE_TILES, N_EXPERTS),
        in_specs=[pl.BlockSpec((MOE_TILE, D_MODEL), lambda i, e: (i, 0)),
                  pl.BlockSpec((MOE_TILE, N_EXPERTS), lambda i, e: (i, 0)),
                  pl.BlockSpec((1, D_MODEL, EXPERT_FF), lambda i, e: (e, 0, 0)),
                  pl.BlockSpec((1, D_MODEL, EXPERT_FF), lambda i, e: (e, 0, 0)),
                  pl.BlockSpec((1, EXPERT_FF, D_MODEL), lambda i, e: (e, 0, 0)),
                  pl.BlockSpec((MOE_TILE, D_MODEL), lambda i, e: (i, 0)),
                  pl.BlockSpec((1, N_MOD, D_MODEL),
                               lambda i, e: (_mod_row_of_tile(i, tps, MOE_TILES_P), 0, 0)),
                  pl.BlockSpec((4, D_MODEL), lambda i, e: (0, 0))],
        out_specs=(pl.BlockSpec((MOE_TILE, D_MODEL), lambda i, e: (jnp.minimum(i, MOE_TILES_P - 1), 0)),
                   pl.BlockSpec((MOE_TILE, D_MODEL), lambda i, e: (jnp.maximum(i - MOE_TILES_P, 0), 0))),
        scratch_shapes=[pltpu.VMEM((MOE_TILE, D_MODEL), F32)],
        compiler_params=_cparams(("arbitrary", "arbitrary")),
        name="moe_dense",
    )(h2, gates, w_gate, w_up, w_down, x1, mods3, g_norm)


def kernel(x_prompt, x_sample, state_C, state_n, state_m, c, c_ctx, w_ada, b_ada, g_norm, w_in, ml_gate_bias, ml_head_gain, hy_conv_w, hy_f_w1, hy_f_b1, hy_f_w2, hy_f_b2, hy_f_w3, hy_f_b3, hy_decay, hy_bias, w_out, w_rc, b_rc, w_rf, b_rf, w_gate, w_up, w_down):
    xp = x_prompt.reshape(T_PROMPT, D_MODEL)
    xs = x_sample.reshape(T_SAMPLE, D_MODEL)
    gn = g_norm[0]

    cv = jnp.concatenate([c_ctx[None, :], c, jnp.zeros((MOD_ROWS - 1 - DEC_BATCH, D_MODEL), F32)], axis=0)
    mods3 = _ada(cv, w_ada[0], b_ada[0]).reshape(MOD_ROWS, N_MOD, D_MODEL)

    w_in0 = w_in[0]
    w_main = jnp.concatenate([w_in0[:, :ML_QKVO_COLS], w_in0[:, ML_QKVO_COLS + ML_GATE_COLS:]],
                             axis=1).astype(BF16)
    wg = w_in0[:, ML_QKVO_COLS:ML_QKVO_COLS + ML_GATE_COLS]
    gb = ml_gate_bias[0].reshape(1, ML_GATE_COLS)
    proj, gates, gates_t = _inproj(xp, xs, mods3, gn, w_main, wg, wg.T, gb, gb.reshape(ML_GATE_COLS, 1))

    gain = ml_head_gain[0].reshape(1, ML_WIDTH)
    y_ml_p, c_new, n_new, m_new = _mlstm(proj, gates, gates_t, gain, None, SEQ, BATCH, 0)
    state = (state_C[:, 0], state_n[:, 0], state_m[:, 0].reshape(DEC_BATCH, 2 * ML_HEADS, 1))
    y_ml_s, _, _, _ = _mlstm(proj, gates, gates_t, gain, state, DEC_SEQ, DEC_BATCH, T_PROMPT // DEC_SEQ)

    w1p = jnp.pad(hy_f_w1[0], ((0, 128 - HY_EMB), (0, 0)))
    b1 = hy_f_b1[0].reshape(1, -1)
    b2 = hy_f_b2[0].reshape(1, -1)
    b3 = hy_f_b3[0].reshape(1, -1)
    dec = hy_decay[0].reshape(1, -1)
    z_parts = []
    for seq_len, n_seq, off, width in ((SEQ, BATCH, 0, SEQ), (DEC_SEQ, DEC_BATCH, T_PROMPT // DEC_SEQ, GRID_W)):
        f, ft = _dft_mats(seq_len)
        coefs = _hyena_filters(seq_len, f, w1p, b1, hy_f_w2[0], b2, hy_f_w3[0], b3, dec)
        z_parts.append(_hyena(proj, hy_conv_w[0], coefs, hy_bias[0], f, ft, seq_len, n_seq, off, width))
    z_p, z_s = z_parts

    w_r = jnp.pad(jnp.concatenate([w_rc[0], w_rf[0]], axis=1), ((0, 0), (0, 128 - N_GROUPS - N_EXPERTS)))
    b_r = jnp.pad(jnp.concatenate([b_rc[0], b_rf[0]], axis=0), (0, 128 - N_GROUPS - N_EXPERTS)).reshape(1, 128)
    x1, h2, moe_gates = _outproj(xp, xs, y_ml_p, y_ml_s, z_p, z_s, mods3, gn, w_out[0].astype(BF16), w_r, b_r)

    y_p, y_s = _moe(h2, moe_gates, w_gate[0], w_up[0], w_down[0], x1, mods3, gn)

    new_c = c_new.reshape(BATCH, 1, 2, ML_HEADS, ML_HEAD_DIM, ML_HEAD_DIM)
    new_n = n_new.reshape(BATCH, 1, 2, ML_HEADS, ML_HEAD_DIM)
    new_m = m_new[:, :, 0].reshape(BATCH, 1, 2, ML_HEADS)
    return (y_p.reshape(BATCH, SEQ, D_MODEL), y_s.reshape(DEC_BATCH, DEC_SEQ, D_MODEL), new_c, new_n, new_m)
```

```python
import functools
import math

import jax
import jax.numpy as jnp
from jax import lax
from jax.experimental import pallas as pl
from jax.experimental.pallas import tpu as pltpu

F32 = jnp.float32
BF16 = jnp.bfloat16

D_MODEL = 1024
BATCH = 16
SEQ = 256
DEC_BATCH = 4
DEC_SEQ = 1024
GRID_W = 64
ML_WIDTH = 512
ML_HEADS = 4
ML_HEAD_DIM = 128
HY_WIDTH = 512
HY_ORDER = 2
HY_EMB = 33
HY_BANDS = 16
HY_FILTER_HIDDEN = 64
HY_MOD_SHIFT = 0.05
N_GROUPS = 4
EXPERTS_PER_GROUP = 4
N_EXPERTS = 16
EXPERT_FF = 512
N_MOD = 6
EPS = 1e-6
ML_QKVO_COLS = 4 * ML_WIDTH
ML_GATE_COLS = 4 * ML_HEADS
HY_COLS = 3 * HY_WIDTH
MAIN_COLS = ML_QKVO_COLS + HY_COLS

T_PROMPT = BATCH * SEQ
T_SAMPLE = DEC_BATCH * DEC_SEQ
T_ALL = T_PROMPT + T_SAMPLE
TILE = 256
N_TILES_P = T_PROMPT // TILE
N_TILES = T_ALL // TILE
MOD_ROWS = 8
K_SCALE = ML_HEAD_DIM ** -0.5
VMEM_LIMIT = 56 * 1024 * 1024


def _cparams(sem):
    return pltpu.CompilerParams(dimension_semantics=sem, vmem_limit_bytes=VMEM_LIMIT)


def _split2(x):
    hi = x.astype(BF16)
    lo = (x - hi.astype(F32)).astype(BF16)
    return hi, lo


def _dot(a, b):
    return jnp.dot(a, b, preferred_element_type=F32)


def _dot_nt(a, b):
    return lax.dot_general(a, b, (((1,), (1,)), ((), ())), preferred_element_type=F32)


def _dot_tn(a, b):
    return lax.dot_general(a, b, (((0,), (0,)), ((), ())), preferred_element_type=F32)


def _dot3(a, b):
    ah, al = _split2(a)
    bh, bl = _split2(b)
    return _dot(ah, bh) + _dot(al, bh) + _dot(ah, bl)


def _dot3_nt(a, b):
    ah, al = _split2(a)
    bh, bl = _split2(b)
    return _dot_nt(ah, bh) + _dot_nt(al, bh) + _dot_nt(ah, bl)


def _dot_exact_lhs(t, x):
    x1 = x.astype(BF16)
    r1 = x - x1.astype(F32)
    x2 = r1.astype(BF16)
    x3 = (r1 - x2.astype(F32)).astype(BF16)
    return _dot(t, x1) + _dot(t, x2) + _dot(t, x3)


def _dot_exact_rhs(x, t):
    x1 = x.astype(BF16)
    r1 = x - x1.astype(F32)
    x2 = r1.astype(BF16)
    x3 = (r1 - x2.astype(F32)).astype(BF16)
    return _dot(x1, t) + _dot(x2, t) + _dot(x3, t)


def _rms(x, g):
    return x * lax.rsqrt(jnp.mean(x * x, axis=-1, keepdims=True) + EPS) * g


def _mod_row_of_tile(i, tiles_per_sample_seq, n_prompt_tiles):
    return jnp.where(i < n_prompt_tiles, 0, 1 + (i - n_prompt_tiles) // tiles_per_sample_seq)


def _ada_kernel(cv_ref, w_ref, b_ref, o_ref):
    cv = cv_ref[...]
    s = cv * jax.nn.sigmoid(cv)
    o_ref[...] = _dot3(s, w_ref[...]) + b_ref[...]


def _ada(cv, w_ada, b_ada):
    n = N_MOD * D_MODEL
    return pl.pallas_call(
        _ada_kernel,
        out_shape=jax.ShapeDtypeStruct((MOD_ROWS, n), F32),
        grid=(N_MOD,),
        in_specs=[pl.BlockSpec((MOD_ROWS, D_MODEL), lambda j: (0, 0)),
                  pl.BlockSpec((D_MODEL, D_MODEL), lambda j: (0, j)),
                  pl.BlockSpec((1, D_MODEL), lambda j: (0, j))],
        out_specs=pl.BlockSpec((MOD_ROWS, D_MODEL), lambda j: (0, j)),
        compiler_params=_cparams(("arbitrary",)),
        name="ada_mod",
    )(cv, w_ada, b_ada.reshape(1, n))


def _log_sigmoid(x):
    return jnp.minimum(x, 0.0) - jnp.log1p(jnp.exp(-jnp.abs(x)))


def _inproj_kernel(xp_ref, xs_ref, m_ref, gn_ref, w_ref, wg_ref, wgt_ref, gb_ref, gbt_ref,
                   proj_ref, gate_ref, gatet_ref):
    i = pl.program_id(0)
    x = jnp.where(i < N_TILES_P, xp_ref[...], xs_ref[...])
    h = _rms(x, gn_ref[0:1, :]) * (1.0 + m_ref[0, 1:2, :]) + m_ref[0, 0:1, :]
    hb = h.astype(BF16)
    cb = 512
    for j in range(MAIN_COLS // cb):
        proj_ref[:, j * cb:(j + 1) * cb] = _dot(hb, w_ref[:, j * cb:(j + 1) * cb]).astype(BF16)
    hl = (h - hb.astype(F32)).astype(BF16)
    wg = wg_ref[...]
    wgh, wgl = _split2(wg)
    g = _dot(hb, wgh) + _dot(hl, wgh) + _dot(hb, wgl) + gb_ref[...]
    col = lax.broadcasted_iota(jnp.int32, g.shape, 1)
    gate_ref[...] = jnp.where((col % 8) >= 4, _log_sigmoid(g), g)
    wgt = wgt_ref[...]
    wth, wtl = _split2(wgt)
    gt = _dot_nt(wth, hb) + _dot_nt(wth, hl) + _dot_nt(wtl, hb) + gbt_ref[...]
    row = lax.broadcasted_iota(jnp.int32, gt.shape, 0)
    gatet_ref[0] = jnp.where((row % 8) >= 4, _log_sigmoid(gt), gt)


def _inproj(xp, xs, mods3, g_norm, w_main, wg, wgt, gb, gbt):
    tps = DEC_SEQ // TILE
    return pl.pallas_call(
        _inproj_kernel,
        out_shape=(jax.ShapeDtypeStruct((T_ALL, MAIN_COLS), BF16),
                   jax.ShapeDtypeStruct((T_ALL, ML_GATE_COLS), F32),
                   jax.ShapeDtypeStruct((N_TILES, ML_GATE_COLS, TILE), F32)),
        grid=(N_TILES,),
        in_specs=[pl.BlockSpec((TILE, D_MODEL), lambda i: (jnp.minimum(i, N_TILES_P - 1), 0)),
                  pl.BlockSpec((TILE, D_MODEL), lambda i: (jnp.maximum(i - N_TILES_P, 0), 0)),
                  pl.BlockSpec((1, N_MOD, D_MODEL), lambda i: (_mod_row_of_tile(i, tps, N_TILES_P), 0, 0)),
                  pl.BlockSpec((4, D_MODEL), lambda i: (0, 0)),
                  pl.BlockSpec((D_MODEL, MAIN_COLS), lambda i: (0, 0)),
                  pl.BlockSpec((D_MODEL, ML_GATE_COLS), lambda i: (0, 0)),
                  pl.BlockSpec((ML_GATE_COLS, D_MODEL), lambda i: (0, 0)),
                  pl.BlockSpec((1, ML_GATE_COLS), lambda i: (0, 0)),
                  pl.BlockSpec((ML_GATE_COLS, 1), lambda i: (0, 0))],
        out_specs=(pl.BlockSpec((TILE, MAIN_COLS), lambda i: (i, 0)),
                   pl.BlockSpec((TILE, ML_GATE_COLS), lambda i: (i, 0)),
                   pl.BlockSpec((1, ML_GATE_COLS, TILE), lambda i: (i, 0, 0))),
        compiler_params=_cparams(("arbitrary",)),
        name="in_proj",
    )(xp, xs, mods3, g_norm, w_main, wg, wgt, gb, gbt)


def _mlstm_kernel(*refs, seq_len, has_state):
    if has_state:
        (q_ref, k_ref, v_ref, o_ref, g_ref, gt_ref, gain_ref, c0_ref, n0_ref, m0_ref,
         y_ref, c_ref, n_ref, m_ref, hf_ref, hb_ref, cs_ref, ns_ref, ms_ref) = refs
    else:
        (q_ref, k_ref, v_ref, o_ref, g_ref, gt_ref, gain_ref,
         y_ref, c_ref, n_ref, m_ref, hf_ref, hb_ref, cs_ref, ns_ref, ms_ref) = refs
    ch = TILE
    nc = seq_len // ch
    hd = ML_HEAD_DIM
    ri = lax.broadcasted_iota(jnp.int32, (ch, ch), 0)
    ci = lax.broadcasted_iota(jnp.int32, (ch, ch), 1)
    lower = ci <= ri
    upper = ci >= ri
    t_low = jnp.where(lower, 1.0, 0.0).astype(BF16)
    t_up = jnp.where(upper, 1.0, 0.0).astype(BF16)

    for d in range(2):
        for h in range(ML_HEADS):
            r = d * ML_HEADS + h
            if has_state:
                cs_ref[r] = c0_ref[0, d, h]
                ns_ref[r] = n0_ref[0, d, h:h + 1, :]
                ms_ref[r] = jnp.broadcast_to(m0_ref[0, r:r + 1, :], (1, hd))
            else:
                cs_ref[r] = jnp.zeros((hd, hd), F32)
                ns_ref[r] = jnp.zeros((1, hd), F32)
                ms_ref[r] = jnp.zeros((1, hd), F32)

    def step(t, carry):
        for d in range(2):
            c = t if d == 0 else nc - 1 - t
            r0 = pl.multiple_of(c * ch, ch)
            rows = pl.ds(r0, ch)
            gcol = g_ref[rows, :]
            grow = gt_ref[c]
            tmat_c = t_low if d == 0 else t_up
            tmat_r = t_up if d == 0 else t_low
            bcol_all = _dot_exact_lhs(tmat_c, gcol)
            brow_all = _dot_exact_rhs(grow, tmat_r)
            mask = lower if d == 0 else upper
            hacc_ref = hf_ref if d == 0 else hb_ref
            for h in range(ML_HEADS):
                r = d * ML_HEADS + h
                fcol = (1 + 2 * d) * ML_HEADS + h
                icol = (2 * d) * ML_HEADS + h
                cols = slice(h * hd, (h + 1) * hd)
                q = q_ref[rows, cols]
                k = k_ref[rows, cols]
                v = v_ref[rows, cols]
                c_prev = cs_ref[r]
                n_prev = ns_ref[r]
                m_prev = ms_ref[r][:, 0:1]
                b_col = bcol_all[:, fcol:fcol + 1]
                b_row = brow_all[fcol:fcol + 1, :]
                ig_row = grow[icol:icol + 1, :]
                ig_col = gcol[:, icol:icol + 1]
                logd = jnp.where(mask, b_col - b_row + ig_row, -jnp.inf)
                inter = b_col + m_prev
                m_pos = jnp.maximum(inter, jnp.max(logd, axis=-1, keepdims=True))
                s = _dot_nt(q, k) * K_SCALE * jnp.exp(logd - m_pos)
                sc_inter = jnp.exp(inter - m_pos)
                qf = q.astype(F32)
                num = sc_inter * _dot(q, c_prev.astype(BF16)) + _dot(s.astype(BF16), v)
                den = (sc_inter * jnp.sum(qf * n_prev, axis=-1, keepdims=True)
                       + jnp.sum(s, axis=-1, keepdims=True))
                hh = num / jnp.maximum(jnp.abs(den), jnp.exp(-m_pos))
                hacc_ref[rows, cols] = hh
                b_last = b_col[ch - 1:ch, :] if d == 0 else b_col[0:1, :]
                logw = b_last - b_col + ig_col
                m_new = jnp.maximum(b_last + m_prev, jnp.max(logw, axis=0, keepdims=True))
                w = jnp.exp(logw - m_new)
                decay = jnp.exp(b_last + m_prev - m_new)
                kw = k.astype(F32) * (w * K_SCALE)
                cs_ref[r] = decay * c_prev + _dot_tn(kw.astype(BF16), v)
                ns_ref[r] = decay * n_prev + jnp.sum(kw, axis=0, keepdims=True)
                ms_ref[r] = jnp.broadcast_to(m_new, (1, hd))
        return carry

    lax.fori_loop(0, nc, step, 0)

    for d in range(2):
        for h in range(ML_HEADS):
            r = d * ML_HEADS + h
            c_ref[0, d, h] = cs_ref[r]
            n_ref[0, d, h:h + 1, :] = ns_ref[r]
            m_ref[0, r:r + 1, :] = ms_ref[r]
    for h in range(ML_HEADS):
        cols = slice(h * hd, (h + 1) * hd)
        hh = hf_ref[:, cols] + hb_ref[:, cols]
        hh = hh * lax.rsqrt(jnp.mean(hh * hh, axis=-1, keepdims=True) + EPS)
        y = hh * gain_ref[:, cols] * jax.nn.sigmoid(o_ref[:, cols].astype(F32))
        y_ref[:, cols] = y.astype(BF16)


def _mlstm(proj, gates, gates_t, gain, state, seq_len, n_seq, row_block_off):
    has_state = state is not None
    tiles = seq_len // TILE
    off = row_block_off
    qkvo_specs = [pl.BlockSpec((seq_len, ML_WIDTH), functools.partial(lambda b, j: (off + b, j), j=j))
                  for j in range(4)]
    in_specs = qkvo_specs + [
        pl.BlockSpec((seq_len, ML_GATE_COLS), lambda b: (off + b, 0)),
        pl.BlockSpec((tiles, ML_GATE_COLS, TILE), lambda b: (off + b, 0, 0)),
        pl.BlockSpec((1, ML_WIDTH), lambda b: (0, 0)),
    ]
    args = [proj, proj, proj, proj, gates, gates_t, gain]
    if has_state:
        c0, n0, m0 = state
        in_specs += [
            pl.BlockSpec((1, 2, ML_HEADS, ML_HEAD_DIM, ML_HEAD_DIM), lambda b: (b, 0, 0, 0, 0)),
            pl.BlockSpec((1, 2, ML_HEADS, ML_HEAD_DIM), lambda b: (b, 0, 0, 0)),
            pl.BlockSpec((1, 2 * ML_HEADS, 1), lambda b: (b, 0, 0)),
        ]
        args += [c0, n0, m0]
    out_shape = (jax.ShapeDtypeStruct((n_seq * seq_len, ML_WIDTH), BF16),
                 jax.ShapeDtypeStruct((n_seq, 2, ML_HEADS, ML_HEAD_DIM, ML_HEAD_DIM), F32),
                 jax.ShapeDtypeStruct((n_seq, 2, ML_HEADS, ML_HEAD_DIM), F32),
                 jax.ShapeDtypeStruct((n_seq, 2 * ML_HEADS, ML_HEAD_DIM), F32))
    out_specs = (pl.BlockSpec((seq_len, ML_WIDTH), lambda b: (b, 0)),
                 pl.BlockSpec((1, 2, ML_HEADS, ML_HEAD_DIM, ML_HEAD_DIM), lambda b: (b, 0, 0, 0, 0)),
                 pl.BlockSpec((1, 2, ML_HEADS, ML_HEAD_DIM), lambda b: (b, 0, 0, 0)),
                 pl.BlockSpec((1, 2 * ML_HEADS, ML_HEAD_DIM), lambda b: (b, 0, 0)))
    scratch = [pltpu.VMEM((seq_len, ML_WIDTH), F32), pltpu.VMEM((seq_len, ML_WIDTH), F32),
               pltpu.VMEM((2 * ML_HEADS, ML_HEAD_DIM, ML_HEAD_DIM), F32),
               pltpu.VMEM((2 * ML_HEADS, 1, ML_HEAD_DIM), F32),
               pltpu.VMEM((2 * ML_HEADS, 1, ML_HEAD_DIM), F32)]
    return pl.pallas_call(
        functools.partial(_mlstm_kernel, seq_len=seq_len, has_state=has_state),
        out_shape=out_shape, grid=(n_seq,), in_specs=in_specs, out_specs=out_specs,
        scratch_shapes=scratch, compiler_params=_cparams(("arbitrary",)),
        name=f"mlstm_{seq_len}",
    )(*args)


def _dft_mats(seq_len):
    lo = 32
    hi = seq_len // lo
    d = jnp.arange(seq_len, dtype=jnp.int32)
    a = jnp.arange(hi, dtype=jnp.int32) * lo
    b = jnp.arange(lo, dtype=jnp.int32)
    scale = math.pi / seq_len
    ang_a = ((a[:, None] * d[None, :]) % (2 * seq_len)).astype(F32) * scale
    ang_b = ((b[:, None] * d[None, :]) % (2 * seq_len)).astype(F32) * scale
    ca, sa, cb, sb = jnp.cos(ang_a), jnp.sin(ang_a), jnp.cos(ang_b), jnp.sin(ang_b)
    cosm = (ca[:, None, :] * cb[None, :, :] - sa[:, None, :] * sb[None, :, :]).reshape(seq_len, seq_len)
    sinm = (sa[:, None, :] * cb[None, :, :] + ca[:, None, :] * sb[None, :, :]).reshape(seq_len, seq_len)
    nyq = jnp.where(d % 2 == 0, 1.0, -1.0).astype(F32)
    krow = jnp.arange(seq_len, dtype=jnp.int32)[:, None]
    sinm = jnp.where(krow == 0, nyq[None, :], sinm)
    f = jnp.concatenate([cosm, sinm], axis=0).astype(BF16)
    cos_t = (ca.T[:, :, None] * cb.T[:, None, :] - sa.T[:, :, None] * sb.T[:, None, :]).reshape(seq_len, seq_len)
    sin_t = (sa.T[:, :, None] * cb.T[:, None, :] + ca.T[:, :, None] * sb.T[:, None, :]).reshape(seq_len, seq_len)
    kcol = jnp.arange(seq_len, dtype=jnp.int32)[None, :]
    sin_t = jnp.where(kcol == 0, nyq[:, None], sin_t)
    ft = jnp.concatenate([cos_t, sin_t], axis=1).astype(BF16)
    return f, ft


def _filter_feats(seq_len):
    t = jnp.linspace(0.0, 1.0, seq_len, dtype=F32)[:, None]
    wpos = 2.0 * math.pi * jnp.arange(seq_len, dtype=F32)[:, None] / seq_len
    bands = jnp.linspace(1e-4, HY_BANDS - 1, HY_BANDS, dtype=F32)[None, :]
    z = jnp.concatenate([t, jnp.cos(bands * wpos), -jnp.sin(bands * wpos)], axis=-1)
    return jnp.pad(z, ((0, 0), (0, 128 - HY_EMB)))


def _filter_kernel(z_ref, w1_ref, b1_ref, w2_ref, b2_ref, w3_ref, b3_ref, dec_ref, f_ref,
                   a_ref, b_ref, d_ref, *, seq_len):
    n = 2 * seq_len
    z = z_ref[...]
    h = jnp.sin(_dot3(z, w1_ref[...]) + b1_ref[...])
    h = jnp.sin(_dot3(h, w2_ref[...]) + b2_ref[...])
    h = _dot3(h, w3_ref[...]) + b3_ref[...]
    t = z[:, 0:1]
    h = h * (jnp.exp(-t * jnp.abs(dec_ref[...])) + HY_MOD_SHIFT)
    ss = jnp.sum(h * h, axis=0, keepdims=True)
    inv = lax.rsqrt(ss[:, :HY_WIDTH] + ss[:, HY_WIDTH:] + EPS)
    hp = h[:, :HY_WIDTH] * inv
    hn = h[:, HY_WIDTH:] * inv
    ssum = hp + hn
    sdif = hp - hn
    hc = _dot(f_ref[0:seq_len, :], ssum.astype(BF16))
    hs = _dot(f_ref[seq_len:n, :], sdif.astype(BF16))
    di = lax.broadcasted_iota(jnp.int32, (seq_len, 1), 0)
    sgn = jnp.where(di % 2 == 0, 1.0, -1.0)
    nyq = jnp.sum(ssum * sgn, axis=0, keepdims=True)
    first = di == 0
    a_ref[0] = hc * jnp.where(first, 1.0 / n, 2.0 / n)
    b_ref[0] = jnp.where(first, 0.0, hs * (2.0 / n))
    d_ref[0] = jnp.where(first, nyq * (1.0 / n), hc * (2.0 / n))


def _hyena_filters(seq_len, f, w1p, b1, w2, b2, w3, b3, dec):
    z = _filter_feats(seq_len)
    hid = HY_FILTER_HIDDEN
    oc = 2 * HY_WIDTH
    full = lambda shape: pl.BlockSpec(shape, lambda o: tuple(0 for _ in shape))
    out = jax.ShapeDtypeStruct((HY_ORDER, seq_len, HY_WIDTH), F32)
    return pl.pallas_call(
        functools.partial(_filter_kernel, seq_len=seq_len),
        out_shape=(out, out, out),
        grid=(HY_ORDER,),
        in_specs=[full((seq_len, 128)), full((128, hid)), full((1, hid)), full((hid, hid)), full((1, hid)),
                  pl.BlockSpec((hid, oc), lambda o: (0, o)),
                  pl.BlockSpec((1, oc), lambda o: (0, o)),
                  pl.BlockSpec((1, oc), lambda o: (0, o)),
                  full((2 * seq_len, seq_len))],
        out_specs=tuple(pl.BlockSpec((1, seq_len, HY_WIDTH), lambda o: (o, 0, 0)) for _ in range(3)),
        compiler_params=_cparams(("arbitrary",)),
        name=f"hyena_filter_{seq_len}",
    )(z, w1p, b1, w2, b2, w3, b3, dec, f)


def _hyena_kernel(x1_ref, x2_ref, v_ref, cw1_ref, cw2_ref, cwv_ref, a_ref, b_ref, d_ref, bias_ref,
                  f_ref, ft_ref, z_ref, *, seq_len, width):
    ti = lax.broadcasted_iota(jnp.int32, (seq_len, 1), 0)
    has_prev = (ti % width) != 0
    has_next = (ti % width) != (width - 1)

    def short_conv(x_ref, w_ref):
        x = x_ref[...].astype(F32)
        prev = jnp.where(has_prev, pltpu.roll(x, 1, axis=0), 0.0)
        nxt = jnp.where(has_next, pltpu.roll(x, seq_len - 1, axis=0), 0.0)
        return w_ref[0:1, :] * prev + w_ref[1:2, :] * x + w_ref[2:3, :] * nxt

    gates = (short_conv(x1_ref, cw1_ref), short_conv(x2_ref, cw2_ref))
    z = short_conv(v_ref, cwv_ref)
    for o in range(HY_ORDER):
        u = _dot(f_ref[...], z.astype(BF16))
        ut = u[:seq_len]
        ub = u[seq_len:]
        a, b, dd = a_ref[o], b_ref[o], d_ref[o]
        yt = ut * a - ub * b
        yb = ut * b + ub * dd
        y = _dot(ft_ref[:, :seq_len], yt.astype(BF16)) + _dot(ft_ref[:, seq_len:], yb.astype(BF16))
        z = gates[o] * (y + bias_ref[o:o + 1, :] * z)
    z_ref[...] = z.astype(BF16)


def _hyena(proj, conv_w, coefs, hy_bias, f, ft, seq_len, n_seq, row_block_off, width):
    cb = 256
    nblk = HY_WIDTH // cb
    base = ML_QKVO_COLS // cb
    off = row_block_off
    a, b, d = coefs

    def col_spec(part):
        return pl.BlockSpec((seq_len, cb), lambda s, j: (off + s, base + part * nblk + j))

    def w_spec(part):
        return pl.BlockSpec((3, cb), lambda s, j: (0, part * nblk + j))

    coef_spec = pl.BlockSpec((HY_ORDER, seq_len, cb), lambda s, j: (0, 0, j))
    return pl.pallas_call(
        functools.partial(_hyena_kernel, seq_len=seq_len, width=width),
        out_shape=jax.ShapeDtypeStruct((n_seq * seq_len, HY_WIDTH), BF16),
        grid=(n_seq, nblk),
        in_specs=[col_spec(0), col_spec(1), col_spec(2), w_spec(0), w_spec(1), w_spec(2),
                  coef_spec, coef_spec, coef_spec,
                  pl.BlockSpec((HY_ORDER, cb), lambda s, j: (0, j)),
                  pl.BlockSpec((2 * seq_len, seq_len), lambda s, j: (0, 0)),
                  pl.BlockSpec((seq_len, 2 * seq_len), lambda s, j: (0, 0))],
        out_specs=pl.BlockSpec((seq_len, cb), lambda s, j: (s, j)),
        compiler_params=_cparams(("arbitrary", "arbitrary")),
        name=f"hyena_conv_{seq_len}",
    )(proj, proj, proj, conv_w, conv_w, conv_w, a, b, d, hy_bias, f, ft)


def _first_max(x, n):
    mx = jnp.max(x, axis=-1, keepdims=True)
    lane = lax.broadcasted_iota(jnp.int32, x.shape, 1).astype(F32)
    idx = jnp.min(jnp.where(x == mx, lane, float(n)), axis=-1, keepdims=True)
    return mx, idx.astype(jnp.int32)


PAIRS_PER_GROUP = 6
N_BUCKETS = N_GROUPS * PAIRS_PER_GROUP
PAIR_SLOTS = ((0, 1), (0, 2), (0, 3), (1, 3), (1, 2), (3, 2))
H2_EXT = D_MODEL + 128
ROW_TILE = 128
ROW_CAP = T_ALL + N_BUCKETS * ROW_TILE
N_ROW_TILES = ROW_CAP // ROW_TILE


def _outproj_kernel(xp_ref, xs_ref, yp_ref, ys_ref, zp_ref, zs_ref, m_ref, gn_ref, wo_ref, wr_ref, br_ref,
                    x1_ref, h2_ref, bid_ref):
    i = pl.program_id(0)
    is_p = i < N_TILES_P
    x = jnp.where(is_p, xp_ref[...], xs_ref[...])
    yml = jnp.where(is_p, yp_ref[...], ys_ref[...])
    zz = jnp.where(is_p, zp_ref[...], zs_ref[...])
    y = _dot(yml, wo_ref[0:ML_WIDTH, :]) + _dot(zz, wo_ref[ML_WIDTH:, :])
    x1 = x + m_ref[0, 2:3, :] * _rms(y, gn_ref[1:2, :])
    x1_ref[...] = x1
    h2 = _rms(x1, gn_ref[2:3, :]) * (1.0 + m_ref[0, 4:5, :]) + m_ref[0, 3:4, :]
    h2_ref[:, :D_MODEL] = h2
    logits = _dot3(h2, wr_ref[...]) + br_ref[...]
    lc = logits[:, 0:N_GROUPS]
    mx, gi = _first_max(lc, N_GROUPS)
    p_grp = 1.0 / jnp.sum(jnp.exp(lc - mx), axis=-1, keepdims=True)
    lsel = jnp.zeros((TILE, EXPERTS_PER_GROUP), F32)
    for g in range(N_GROUPS):
        lo = N_GROUPS + g * EXPERTS_PER_GROUP
        lsel = jnp.where(gi == g, logits[:, lo:lo + EXPERTS_PER_GROUP], lsel)
    l1, i1 = _first_max(lsel, EXPERTS_PER_GROUP)
    lane4 = lax.broadcasted_iota(jnp.int32, lsel.shape, 1)
    l2, i2 = _first_max(jnp.where(lane4 == i1, -jnp.inf, lsel), EXPERTS_PER_GROUP)
    e2 = jnp.exp(l2 - l1)
    w1 = p_grp / (1.0 + e2)
    w2 = p_grp * e2 / (1.0 + e2)
    lo_e = jnp.minimum(i1, i2)
    hi_e = jnp.maximum(i1, i2)
    pair = jnp.where(lo_e == 0, hi_e - 1, jnp.where(lo_e == 1, jnp.where(hi_e == 3, 3, 4), 5))
    slot_a = jnp.where(pair == 5, hi_e, lo_e)
    first_in_a = i1 == slot_a
    w_a = jnp.where(first_in_a, w1, w2)
    w_b = jnp.where(first_in_a, w2, w1)
    lane = lax.broadcasted_iota(jnp.int32, (TILE, H2_EXT - D_MODEL), 1)
    h2_ref[:, D_MODEL:] = jnp.where(lane == 0, w_a, jnp.where(lane == 1, w_b, 0.0))
    bucket = (gi * PAIRS_PER_GROUP + pair).astype(F32)
    ri = lax.broadcasted_iota(jnp.int32, (TILE, TILE), 0)
    ci = lax.broadcasted_iota(jnp.int32, (TILE, TILE), 1)
    bid_ref[0] = jnp.sum(jnp.where(ri == ci, bucket, 0.0), axis=0, keepdims=True).astype(jnp.int32)


def _outproj(xp, xs, yp, ys, zp, zs, mods3, g_norm, w_out, w_r, b_r):
    tps = DEC_SEQ // TILE
    pidx = lambda i: (jnp.minimum(i, N_TILES_P - 1), 0)
    sidx = lambda i: (jnp.maximum(i - N_TILES_P, 0), 0)
    return pl.pallas_call(
        _outproj_kernel,
        out_shape=(jax.ShapeDtypeStruct((T_ALL, D_MODEL), F32),
                   jax.ShapeDtypeStruct((T_ALL, H2_EXT), F32),
                   jax.ShapeDtypeStruct((N_TILES, 1, TILE), jnp.int32)),
        grid=(N_TILES,),
        in_specs=[pl.BlockSpec((TILE, D_MODEL), pidx), pl.BlockSpec((TILE, D_MODEL), sidx),
                  pl.BlockSpec((TILE, ML_WIDTH), pidx), pl.BlockSpec((TILE, ML_WIDTH), sidx),
                  pl.BlockSpec((TILE, HY_WIDTH), pidx), pl.BlockSpec((TILE, HY_WIDTH), sidx),
                  pl.BlockSpec((1, N_MOD, D_MODEL), lambda i: (_mod_row_of_tile(i, tps, N_TILES_P), 0, 0)),
                  pl.BlockSpec((4, D_MODEL), lambda i: (0, 0)),
                  pl.BlockSpec((D_MODEL, D_MODEL), lambda i: (0, 0)),
                  pl.BlockSpec((D_MODEL, 128), lambda i: (0, 0)),
                  pl.BlockSpec((1, 128), lambda i: (0, 0))],
        out_specs=(pl.BlockSpec((TILE, D_MODEL), lambda i: (i, 0)),
                   pl.BlockSpec((TILE, H2_EXT), lambda i: (i, 0)),
                   pl.BlockSpec((1, 1, TILE), lambda i: (i, 0, 0))),
        compiler_params=_cparams(("arbitrary",)),
        name="out_proj_router",
    )(xp, xs, yp, ys, zp, zs, mods3, g_norm, w_out, w_r, b_r)


def _route_kernel(bid_ref, pos_ref, meta_ref):
    nb = 32
    tm = float(ROW_TILE)
    sub = lax.broadcasted_iota(jnp.int32, (nb, TILE), 0)
    ri = lax.broadcasted_iota(jnp.int32, (TILE, TILE), 0)
    ci = lax.broadcasted_iota(jnp.int32, (TILE, TILE), 1)
    before = jnp.where(ri < ci, 1.0, 0.0).astype(BF16)

    def onehot(blk):
        return jnp.where(sub == bid_ref[blk], 1.0, 0.0)

    zeros = jnp.zeros((nb, 1), F32)
    cnt = lax.fori_loop(0, N_TILES, lambda blk, c: c + jnp.sum(onehot(blk), axis=1, keepdims=True), zeros)
    padded = jnp.floor((cnt + (tm - 1.0)) * (1.0 / tm)) * tm
    r32 = lax.broadcasted_iota(jnp.int32, (nb, nb), 0)
    c32 = lax.broadcasted_iota(jnp.int32, (nb, nb), 1)
    padded_row = jnp.sum(jnp.where(r32 == c32, padded, 0.0), axis=0, keepdims=True)
    offs = jnp.sum(jnp.where(c32 < r32, padded_row, 0.0), axis=1, keepdims=True)
    ends = offs + padded

    def place(blk, seen):
        oh = onehot(blk)
        rank = _dot(oh.astype(BF16), before)
        pos = jnp.sum(oh * (rank + seen + offs), axis=0, keepdims=True)
        pos_ref[blk] = pos.astype(jnp.int32)
        return seen + jnp.sum(oh, axis=1, keepdims=True)

    lax.fori_loop(0, N_TILES, place, zeros)

    start = lax.broadcasted_iota(jnp.int32, (nb, 128), 1).astype(F32) * tm
    bsub = lax.broadcasted_iota(jnp.int32, (nb, 128), 0)
    done = jnp.where((bsub < N_BUCKETS) & (ends <= start), 1.0, 0.0)
    tb = jnp.sum(done, axis=0, keepdims=True)
    valid = jnp.where(tb < N_BUCKETS, 1.0, 0.0)
    tbc = jnp.minimum(tb, N_BUCKETS - 1.0)
    grp = jnp.floor((tbc + 0.5) * (1.0 / PAIRS_PER_GROUP))
    pair = tbc - PAIRS_PER_GROUP * grp
    loc_a = jnp.zeros_like(pair)
    loc_b = jnp.zeros_like(pair)
    for k, (sa, sb) in enumerate(PAIR_SLOTS):
        loc_a = jnp.where(pair == k, float(sa), loc_a)
        loc_b = jnp.where(pair == k, float(sb), loc_b)
    row8 = lax.broadcasted_iota(jnp.int32, (8, 128), 0)
    meta = jnp.where(row8 == 0, grp * EXPERTS_PER_GROUP + loc_a,
                     jnp.where(row8 == 1, grp * EXPERTS_PER_GROUP + loc_b,
                               jnp.where(row8 == 2, valid, 0.0)))
    meta_ref[...] = meta.astype(jnp.int32)


def _route(bid):
    return pl.pallas_call(
        _route_kernel,
        out_shape=(jax.ShapeDtypeStruct((N_TILES, 1, TILE), jnp.int32),
                   jax.ShapeDtypeStruct((8, 128), jnp.int32)),
        compiler_params=pltpu.CompilerParams(vmem_limit_bytes=VMEM_LIMIT),
        name="moe_route",
    )(bid)


def _moe_kernel(meta_ref, pos_ref, h2_hbm, wga_ref, wua_ref, wda_ref, wgb_ref, wub_ref, wdb_ref,
                y_ref, src_ref, xbuf, sem, wga_s, wua_s, wda_s, wgb_s, wub_s, wdb_s):
    j = pl.program_id(0)

    def row_copy(tile, r, slot):
        tok = src_ref[tile * ROW_TILE + r]
        return pltpu.make_async_copy(h2_hbm.at[pl.ds(tok, 1), :], xbuf.at[slot, pl.ds(r, 1), :], sem.at[slot])

    def issue(tile, slot):
        def body(r, c):
            row_copy(tile, r, slot).start()
            return c
        lax.fori_loop(0, ROW_TILE, body, 0, unroll=8)

    def wait(tile, slot):
        def body(r, c):
            row_copy(tile, r, slot).wait()
            return c
        lax.fori_loop(0, ROW_TILE, body, 0, unroll=8)

    @pl.when(j == 0)
    def _():
        def clear(p, c):
            src_ref[p] = 0
            return c
        lax.fori_loop(0, ROW_CAP, clear, 0, unroll=8)

        def invert(t, c):
            src_ref[pos_ref[t]] = t
            return c
        lax.fori_loop(0, T_ALL, invert, 0, unroll=8)

        @pl.when(meta_ref[2, 0] == 1)
        def _():
            issue(0, 0)

    nxt = jnp.minimum(j + 1, N_ROW_TILES - 1)

    @pl.when((j + 1 < N_ROW_TILES) & (meta_ref[2, nxt] == 1))
    def _():
        issue(nxt, nxt % 2)

    valid = meta_ref[2, j] == 1
    prev = jnp.maximum(j - 1, 0)

    @pl.when(valid & ((j == 0) | (meta_ref[0, j] != meta_ref[0, prev])))
    def _():
        wga_s[...] = wga_ref[0].astype(BF16)
        wua_s[...] = wua_ref[0].astype(BF16)
        wda_s[...] = wda_ref[0].astype(BF16)

    @pl.when(valid & ((j == 0) | (meta_ref[1, j] != meta_ref[1, prev])))
    def _():
        wgb_s[...] = wgb_ref[0].astype(BF16)
        wub_s[...] = wub_ref[0].astype(BF16)
        wdb_s[...] = wdb_ref[0].astype(BF16)

    @pl.when(valid)
    def _():
        slot = j % 2
        wait(j, slot)
        xe = xbuf[slot]
        x = xe[:, :D_MODEL].astype(BF16)

        def expert(wg, wu, gate):
            hg = _dot(x, wg[...])
            hu = _dot(x, wu[...])
            return (hg * jax.nn.sigmoid(hg) * hu * gate).astype(BF16)

        act_a = expert(wga_s, wua_s, xe[:, D_MODEL:D_MODEL + 1])
        act_b = expert(wgb_s, wub_s, xe[:, D_MODEL + 1:D_MODEL + 2])
        y_ref[...] = _dot(act_a, wda_s[...]) + _dot(act_b, wdb_s[...])

    @pl.when(jnp.logical_not(valid))
    def _():
        y_ref[...] = jnp.zeros_like(y_ref)


def _moe(meta, pos, h2ext, w_gate, w_up, w_down):
    up_spec = lambda slot: pl.BlockSpec((1, D_MODEL, EXPERT_FF), lambda j, meta, pos: (meta[slot, j], 0, 0))
    down_spec = lambda slot: pl.BlockSpec((1, EXPERT_FF, D_MODEL), lambda j, meta, pos: (meta[slot, j], 0, 0))
    grid_spec = pltpu.PrefetchScalarGridSpec(
        num_scalar_prefetch=2,
        grid=(N_ROW_TILES,),
        in_specs=[pl.BlockSpec(memory_space=pl.ANY),
                  up_spec(0), up_spec(0), down_spec(0), up_spec(1), up_spec(1), down_spec(1)],
        out_specs=pl.BlockSpec((ROW_TILE, D_MODEL), lambda j, meta, pos: (j, 0)),
        scratch_shapes=[pltpu.SMEM((ROW_CAP,), jnp.int32),
                        pltpu.VMEM((2, ROW_TILE, H2_EXT), F32),
                        pltpu.SemaphoreType.DMA((2,)),
                        pltpu.VMEM((D_MODEL, EXPERT_FF), BF16), pltpu.VMEM((D_MODEL, EXPERT_FF), BF16),
                        pltpu.VMEM((EXPERT_FF, D_MODEL), BF16),
                        pltpu.VMEM((D_MODEL, EXPERT_FF), BF16), pltpu.VMEM((D_MODEL, EXPERT_FF), BF16),
                        pltpu.VMEM((EXPERT_FF, D_MODEL), BF16)])
    return pl.pallas_call(
        _moe_kernel,
        out_shape=jax.ShapeDtypeStruct((ROW_CAP, D_MODEL), F32),
        grid_spec=grid_spec,
        compiler_params=_cparams(("arbitrary",)),
        name="moe_experts",
    )(meta, pos, h2ext, w_gate, w_up, w_down, w_gate, w_up, w_down)


def _final_kernel(pos_ref, y_hbm, x1_ref, m_ref, gn_ref, op_ref, os_ref, ybuf, sem):
    i = pl.program_id(0)

    def row_copy(tile, r, slot):
        p = pos_ref[tile * TILE + r]
        return pltpu.make_async_copy(y_hbm.at[pl.ds(p, 1), :], ybuf.at[slot, pl.ds(r, 1), :], sem.at[slot])

    def issue(tile, slot):
        def body(r, c):
            row_copy(tile, r, slot).start()
            return c
        lax.fori_loop(0, TILE, body, 0, unroll=8)

    def wait(tile, slot):
        def body(r, c):
            row_copy(tile, r, slot).wait()
            return c
        lax.fori_loop(0, TILE, body, 0, unroll=8)

    @pl.when(i == 0)
    def _():
        issue(0, 0)

    @pl.when(i + 1 < N_TILES)
    def _():
        issue(i + 1, (i + 1) % 2)

    slot = i % 2
    wait(i, slot)
    out = x1_ref[...] + m_ref[0, 5:6, :] * _rms(ybuf[slot], gn_ref[3:4, :])

    @pl.when(i < N_TILES_P)
    def _():
        op_ref[...] = out

    @pl.when(i >= N_TILES_P)
    def _():
        os_ref[...] = out


def _final(pos, y_sorted, x1, mods3, g_norm):
    tps = DEC_SEQ // TILE
    grid_spec = pltpu.PrefetchScalarGridSpec(
        num_scalar_prefetch=1,
        grid=(N_TILES,),
        in_specs=[pl.BlockSpec(memory_space=pl.ANY),
                  pl.BlockSpec((TILE, D_MODEL), lambda i, pos: (i, 0)),
                  pl.BlockSpec((1, N_MOD, D_MODEL), lambda i, pos: (_mod_row_of_tile(i, tps, N_TILES_P), 0, 0)),
                  pl.BlockSpec((4, D_MODEL), lambda i, pos: (0, 0))],
        out_specs=(pl.BlockSpec((TILE, D_MODEL), lambda i, pos: (jnp.minimum(i, N_TILES_P - 1), 0)),
                   pl.BlockSpec((TILE, D_MODEL), lambda i, pos: (jnp.maximum(i - N_TILES_P, 0), 0))),
        scratch_shapes=[pltpu.VMEM((2, TILE, D_MODEL), F32), pltpu.SemaphoreType.DMA((2,))])
    return pl.pallas_call(
        _final_kernel,
        out_shape=(jax.ShapeDtypeStruct((T_PROMPT, D_MODEL), F32),
                   jax.ShapeDtypeStruct((T_SAMPLE, D_MODEL), F32)),
        grid_spec=grid_spec,
        compiler_params=_cparams(("arbitrary",)),
        name="moe_combine_final",
    )(pos, y_sorted, x1, mods3, g_norm)


def kernel(x_prompt, x_sample, state_C, state_n, state_m, c, c_ctx, w_ada, b_ada, g_norm, w_in, ml_gate_bias, ml_head_gain, hy_conv_w, hy_f_w1, hy_f_b1, hy_f_w2, hy_f_b2, hy_f_w3, hy_f_b3, hy_decay, hy_bias, w_out, w_rc, b_rc, w_rf, b_rf, w_gate, w_up, w_down):
    xp = x_prompt.reshape(T_PROMPT, D_MODEL)
    xs = x_sample.reshape(T_SAMPLE, D_MODEL)
    gn = g_norm[0]

    cv = jnp.concatenate([c_ctx[None, :], c, jnp.zeros((MOD_ROWS - 1 - DEC_BATCH, D_MODEL), F32)], axis=0)
    mods3 = _ada(cv, w_ada[0], b_ada[0]).reshape(MOD_ROWS, N_MOD, D_MODEL)

    w_in0 = w_in[0]
    w_main = jnp.concatenate([w_in0[:, :ML_QKVO_COLS], w_in0[:, ML_QKVO_COLS + ML_GATE_COLS:]],
                             axis=1).astype(BF16)
    wg = w_in0[:, ML_QKVO_COLS:ML_QKVO_COLS + ML_GATE_COLS]
    gb = ml_gate_bias[0].reshape(1, ML_GATE_COLS)
    proj, gates, gates_t = _inproj(xp, xs, mods3, gn, w_main, wg, wg.T, gb, gb.reshape(ML_GATE_COLS, 1))

    gain = ml_head_gain[0].reshape(1, ML_WIDTH)
    y_ml_p, c_new, n_new, m_new = _mlstm(proj, gates, gates_t, gain, None, SEQ, BATCH, 0)
    state = (state_C[:, 0], state_n[:, 0], state_m[:, 0].reshape(DEC_BATCH, 2 * ML_HEADS, 1))
    y_ml_s, _, _, _ = _mlstm(proj, gates, gates_t, gain, state, DEC_SEQ, DEC_BATCH, T_PROMPT // DEC_SEQ)

    w1p = jnp.pad(hy_f_w1[0], ((0, 128 - HY_EMB), (0, 0)))
    b1 = hy_f_b1[0].reshape(1, -1)
    b2 = hy_f_b2[0].reshape(1, -1)
    b3 = hy_f_b3[0].reshape(1, -1)
    dec = hy_decay[0].reshape(1, -1)
    z_parts = []
    for seq_len, n_seq, off, width in ((SEQ, BATCH, 0, SEQ), (DEC_SEQ, DEC_BATCH, T_PROMPT // DEC_SEQ, GRID_W)):
        f, ft = _dft_mats(seq_len)
        coefs = _hyena_filters(seq_len, f, w1p, b1, hy_f_w2[0], b2, hy_f_w3[0], b3, dec)
        z_parts.append(_hyena(proj, hy_conv_w[0], coefs, hy_bias[0], f, ft, seq_len, n_seq, off, width))
    z_p, z_s = z_parts

    w_r = jnp.pad(jnp.concatenate([w_rc[0], w_rf[0]], axis=1), ((0, 0), (0, 128 - N_GROUPS - N_EXPERTS)))
    b_r = jnp.pad(jnp.concatenate([b_rc[0], b_rf[0]], axis=0), (0, 128 - N_GROUPS - N_EXPERTS)).reshape(1, 128)
    x1, h2ext, bid = _outproj(xp, xs, y_ml_p, y_ml_s, z_p, z_s, mods3, gn, w_out[0].astype(BF16), w_r, b_r)

    pos3, meta = _route(bid)
    pos = pos3.reshape(T_ALL)
    y_sorted = _moe(meta, pos, h2ext, w_gate[0], w_up[0], w_down[0])
    y_p, y_s = _final(pos, y_sorted, x1, mods3, gn)

    new_c = c_new.reshape(BATCH, 1, 2, ML_HEADS, ML_HEAD_DIM, ML_HEAD_DIM)
    new_n = n_new.reshape(BATCH, 1, 2, ML_HEADS, ML_HEAD_DIM)
    new_m = m_new[:, :, 0].reshape(BATCH, 1, 2, ML_HEADS)
    return (y_p.reshape(BATCH, SEQ, D_MODEL), y_s.reshape(DEC_BATCH, DEC_SEQ, D_MODEL), new_c, new_n, new_m)
```

```python
import functools
import math

import jax
import jax.numpy as jnp
from jax import lax
from jax.experimental import pallas as pl
from jax.experimental.pallas import tpu as pltpu

F32 = jnp.float32
BF16 = jnp.bfloat16

D_MODEL = 1024
BATCH = 16
SEQ = 256
DEC_BATCH = 4
DEC_SEQ = 1024
GRID_W = 64
ML_WIDTH = 512
ML_HEADS = 4
ML_HEAD_DIM = 128
HY_WIDTH = 512
HY_ORDER = 2
HY_EMB = 33
HY_BANDS = 16
HY_FILTER_HIDDEN = 64
HY_MOD_SHIFT = 0.05
N_GROUPS = 4
EXPERTS_PER_GROUP = 4
N_EXPERTS = 16
EXPERT_FF = 512
N_MOD = 6
EPS = 1e-6
ML_QKVO_COLS = 4 * ML_WIDTH
ML_GATE_COLS = 4 * ML_HEADS
HY_COLS = 3 * HY_WIDTH
MAIN_COLS = ML_QKVO_COLS + HY_COLS

T_PROMPT = BATCH * SEQ
T_SAMPLE = DEC_BATCH * DEC_SEQ
T_ALL = T_PROMPT + T_SAMPLE
TILE = 256
N_TILES_P = T_PROMPT // TILE
N_TILES = T_ALL // TILE
MOD_ROWS = 8
K_SCALE = ML_HEAD_DIM ** -0.5
VMEM_LIMIT = 56 * 1024 * 1024


def _cparams(sem):
    return pltpu.CompilerParams(dimension_semantics=sem, vmem_limit_bytes=VMEM_LIMIT)


def _split2(x):
    hi = x.astype(BF16)
    lo = (x - hi.astype(F32)).astype(BF16)
    return hi, lo


def _dot(a, b):
    return jnp.dot(a, b, preferred_element_type=F32)


def _dot_nt(a, b):
    return lax.dot_general(a, b, (((1,), (1,)), ((), ())), preferred_element_type=F32)


def _dot_tn(a, b):
    return lax.dot_general(a, b, (((0,), (0,)), ((), ())), preferred_element_type=F32)


def _dot3(a, b):
    ah, al = _split2(a)
    bh, bl = _split2(b)
    return _dot(ah, bh) + _dot(al, bh) + _dot(ah, bl)


def _dot3_nt(a, b):
    ah, al = _split2(a)
    bh, bl = _split2(b)
    return _dot_nt(ah, bh) + _dot_nt(al, bh) + _dot_nt(ah, bl)


def _dot_exact_lhs(t, x):
    x1 = x.astype(BF16)
    r1 = x - x1.astype(F32)
    x2 = r1.astype(BF16)
    x3 = (r1 - x2.astype(F32)).astype(BF16)
    return _dot(t, x1) + _dot(t, x2) + _dot(t, x3)


def _dot_exact_rhs(x, t):
    x1 = x.astype(BF16)
    r1 = x - x1.astype(F32)
    x2 = r1.astype(BF16)
    x3 = (r1 - x2.astype(F32)).astype(BF16)
    return _dot(x1, t) + _dot(x2, t) + _dot(x3, t)


def _rms(x, g):
    return x * lax.rsqrt(jnp.mean(x * x, axis=-1, keepdims=True) + EPS) * g


def _mod_row_of_tile(i, tiles_per_sample_seq, n_prompt_tiles):
    return jnp.where(i < n_prompt_tiles, 0, 1 + (i - n_prompt_tiles) // tiles_per_sample_seq)


def _ada_kernel(cv_ref, w_ref, b_ref, o_ref):
    cv = cv_ref[...]
    s = cv * jax.nn.sigmoid(cv)
    o_ref[...] = _dot3(s, w_ref[...]) + b_ref[...]


def _ada(cv, w_ada, b_ada):
    n = N_MOD * D_MODEL
    return pl.pallas_call(
        _ada_kernel,
        out_shape=jax.ShapeDtypeStruct((MOD_ROWS, n), F32),
        grid=(N_MOD,),
        in_specs=[pl.BlockSpec((MOD_ROWS, D_MODEL), lambda j: (0, 0)),
                  pl.BlockSpec((D_MODEL, D_MODEL), lambda j: (0, j)),
                  pl.BlockSpec((1, D_MODEL), lambda j: (0, j))],
        out_specs=pl.BlockSpec((MOD_ROWS, D_MODEL), lambda j: (0, j)),
        compiler_params=_cparams(("arbitrary",)),
        name="ada_mod",
    )(cv, w_ada, b_ada.reshape(1, n))


def _log_sigmoid(x):
    return jnp.minimum(x, 0.0) - jnp.log1p(jnp.exp(-jnp.abs(x)))


def _rows_to_cols(rows):
    ri = lax.broadcasted_iota(jnp.int32, (TILE, TILE), 0)
    ci = lax.broadcasted_iota(jnp.int32, (TILE, TILE), 1)
    eye = jnp.where(ri == ci, 1.0, 0.0).astype(BF16)
    p1 = rows.astype(BF16)
    r1 = rows - p1.astype(F32)
    p2 = r1.astype(BF16)
    p3 = (r1 - p2.astype(F32)).astype(BF16)
    return _dot_nt(eye, p1) + _dot_nt(eye, p2) + _dot_nt(eye, p3)


def _inproj_kernel(xp_ref, xs_ref, m_ref, gn_ref, w_ref, wgt_ref, gbt_ref, proj_ref, gate_ref, gatet_ref):
    i = pl.program_id(0)
    x = jnp.where(i < N_TILES_P, xp_ref[...], xs_ref[...])
    h = _rms(x, gn_ref[0:1, :]) * (1.0 + m_ref[0, 1:2, :]) + m_ref[0, 0:1, :]
    hb = h.astype(BF16)
    cb = 512
    for j in range(MAIN_COLS // cb):
        proj_ref[:, j * cb:(j + 1) * cb] = _dot(hb, w_ref[:, j * cb:(j + 1) * cb]).astype(BF16)
    hl = (h - hb.astype(F32)).astype(BF16)
    wth, wtl = _split2(wgt_ref[...])
    gt = _dot_nt(wth, hb) + _dot_nt(wth, hl) + _dot_nt(wtl, hb) + gbt_ref[...]
    row = lax.broadcasted_iota(jnp.int32, gt.shape, 0)
    gt = jnp.where((row % 8) >= 4, _log_sigmoid(gt), gt)
    gatet_ref[0] = gt
    gate_ref[...] = _rows_to_cols(gt)


def _inproj(xp, xs, mods3, g_norm, w_main, wgt, gbt):
    tps = DEC_SEQ // TILE
    return pl.pallas_call(
        _inproj_kernel,
        out_shape=(jax.ShapeDtypeStruct((T_ALL, MAIN_COLS), BF16),
                   jax.ShapeDtypeStruct((T_ALL, ML_GATE_COLS), F32),
                   jax.ShapeDtypeStruct((N_TILES, ML_GATE_COLS, TILE), F32)),
        grid=(N_TILES,),
        in_specs=[pl.BlockSpec((TILE, D_MODEL), lambda i: (jnp.minimum(i, N_TILES_P - 1), 0)),
                  pl.BlockSpec((TILE, D_MODEL), lambda i: (jnp.maximum(i - N_TILES_P, 0), 0)),
                  pl.BlockSpec((1, N_MOD, D_MODEL), lambda i: (_mod_row_of_tile(i, tps, N_TILES_P), 0, 0)),
                  pl.BlockSpec((4, D_MODEL), lambda i: (0, 0)),
                  pl.BlockSpec((D_MODEL, MAIN_COLS), lambda i: (0, 0)),
                  pl.BlockSpec((ML_GATE_COLS, D_MODEL), lambda i: (0, 0)),
                  pl.BlockSpec((ML_GATE_COLS, 1), lambda i: (0, 0))],
        out_specs=(pl.BlockSpec((TILE, MAIN_COLS), lambda i: (i, 0)),
                   pl.BlockSpec((TILE, ML_GATE_COLS), lambda i: (i, 0)),
                   pl.BlockSpec((1, ML_GATE_COLS, TILE), lambda i: (i, 0, 0))),
        compiler_params=_cparams(("arbitrary",)),
        name="in_proj",
    )(xp, xs, mods3, g_norm, w_main, wgt, gbt)


def _mlstm_kernel(*refs, seq_len, has_state):
    if has_state:
        (q_ref, k_ref, v_ref, o_ref, g_ref, gt_ref, gain_ref, c0_ref, n0_ref, m0_ref,
         y_ref, c_ref, n_ref, m_ref, hf_ref, hb_ref, cs_ref, ns_ref, ms_ref) = refs
    else:
        (q_ref, k_ref, v_ref, o_ref, g_ref, gt_ref, gain_ref,
         y_ref, c_ref, n_ref, m_ref, hf_ref, hb_ref, cs_ref, ns_ref, ms_ref) = refs
    ch = TILE
    nc = seq_len // ch
    hd = ML_HEAD_DIM
    ri = lax.broadcasted_iota(jnp.int32, (ch, ch), 0)
    ci = lax.broadcasted_iota(jnp.int32, (ch, ch), 1)
    lower = ci <= ri
    upper = ci >= ri
    t_low = jnp.where(lower, 1.0, 0.0).astype(BF16)
    t_up = jnp.where(upper, 1.0, 0.0).astype(BF16)

    for d in range(2):
        for h in range(ML_HEADS):
            r = d * ML_HEADS + h
            if has_state:
                cs_ref[r] = c0_ref[0, d, h]
                ns_ref[r] = n0_ref[0, d, h:h + 1, :]
                ms_ref[r] = jnp.broadcast_to(m0_ref[0, r:r + 1, :], (1, hd))
            else:
                cs_ref[r] = jnp.zeros((hd, hd), F32)
                ns_ref[r] = jnp.zeros((1, hd), F32)
                ms_ref[r] = jnp.zeros((1, hd), F32)

    def step(t, carry):
        for d in range(2):
            c = t if d == 0 else nc - 1 - t
            r0 = pl.multiple_of(c * ch, ch)
            rows = pl.ds(r0, ch)
            gcol = g_ref[rows, :]
            grow = gt_ref[c]
            tmat_c = t_low if d == 0 else t_up
            tmat_r = t_up if d == 0 else t_low
            bcol_all = _dot_exact_lhs(tmat_c, gcol)
            brow_all = _dot_exact_rhs(grow, tmat_r)
            mask = lower if d == 0 else upper
            hacc_ref = hf_ref if d == 0 else hb_ref
            for h in range(ML_HEADS):
                r = d * ML_HEADS + h
                fcol = (1 + 2 * d) * ML_HEADS + h
                icol = (2 * d) * ML_HEADS + h
                cols = slice(h * hd, (h + 1) * hd)
                q = q_ref[rows, cols]
                k = k_ref[rows, cols]
                v = v_ref[rows, cols]
                c_prev = cs_ref[r]
                n_prev = ns_ref[r]
                m_prev = ms_ref[r][:, 0:1]
                b_col = bcol_all[:, fcol:fcol + 1]
                b_row = brow_all[fcol:fcol + 1, :]
                ig_row = grow[icol:icol + 1, :]
                ig_col = gcol[:, icol:icol + 1]
                logd = jnp.where(mask, b_col - b_row + ig_row, -jnp.inf)
                inter = b_col + m_prev
                m_pos = jnp.maximum(inter, jnp.max(logd, axis=-1, keepdims=True))
                s = _dot_nt(q, k) * K_SCALE * jnp.exp(logd - m_pos)
                sc_inter = jnp.exp(inter - m_pos)
                qf = q.astype(F32)
                num = sc_inter * _dot(q, c_prev.astype(BF16)) + _dot(s.astype(BF16), v)
                den = (sc_inter * jnp.sum(qf * n_prev, axis=-1, keepdims=True)
                       + jnp.sum(s, axis=-1, keepdims=True))
                hh = num / jnp.maximum(jnp.abs(den), jnp.exp(-m_pos))
                hacc_ref[rows, cols] = hh
                b_last = b_col[ch - 1:ch, :] if d == 0 else b_col[0:1, :]
                logw = b_last - b_col + ig_col
                m_new = jnp.maximum(b_last + m_prev, jnp.max(logw, axis=0, keepdims=True))
                w = jnp.exp(logw - m_new)
                decay = jnp.exp(b_last + m_prev - m_new)
                kw = k.astype(F32) * (w * K_SCALE)
                cs_ref[r] = decay * c_prev + _dot_tn(kw.astype(BF16), v)
                ns_ref[r] = decay * n_prev + jnp.sum(kw, axis=0, keepdims=True)
                ms_ref[r] = jnp.broadcast_to(m_new, (1, hd))
        return carry

    lax.fori_loop(0, nc, step, 0)

    for d in range(2):
        for h in range(ML_HEADS):
            r = d * ML_HEADS + h
            c_ref[0, d, h] = cs_ref[r]
            n_ref[0, d, h:h + 1, :] = ns_ref[r]
            m_ref[0, r:r + 1, :] = ms_ref[r]
    for h in range(ML_HEADS):
        cols = slice(h * hd, (h + 1) * hd)
        hh = hf_ref[:, cols] + hb_ref[:, cols]
        hh = hh * lax.rsqrt(jnp.mean(hh * hh, axis=-1, keepdims=True) + EPS)
        y = hh * gain_ref[:, cols] * jax.nn.sigmoid(o_ref[:, cols].astype(F32))
        y_ref[:, cols] = y.astype(BF16)


def _mlstm(proj, gates, gates_t, gain, state, seq_len, n_seq, row_block_off):
    has_state = state is not None
    tiles = seq_len // TILE
    off = row_block_off
    qkvo_specs = [pl.BlockSpec((seq_len, ML_WIDTH), functools.partial(lambda b, j: (off + b, j), j=j))
                  for j in range(4)]
    in_specs = qkvo_specs + [
        pl.BlockSpec((seq_len, ML_GATE_COLS), lambda b: (off + b, 0)),
        pl.BlockSpec((tiles, ML_GATE_COLS, TILE), lambda b: (off + b, 0, 0)),
        pl.BlockSpec((1, ML_WIDTH), lambda b: (0, 0)),
    ]
    args = [proj, proj, proj, proj, gates, gates_t, gain]
    if has_state:
        c0, n0, m0 = state
        in_specs += [
            pl.BlockSpec((1, 2, ML_HEADS, ML_HEAD_DIM, ML_HEAD_DIM), lambda b: (b, 0, 0, 0, 0)),
            pl.BlockSpec((1, 2, ML_HEADS, ML_HEAD_DIM), lambda b: (b, 0, 0, 0)),
            pl.BlockSpec((1, 2 * ML_HEADS, 1), lambda b: (b, 0, 0)),
        ]
        args += [c0, n0, m0]
    out_shape = (jax.ShapeDtypeStruct((n_seq * seq_len, ML_WIDTH), BF16),
                 jax.ShapeDtypeStruct((n_seq, 2, ML_HEADS, ML_HEAD_DIM, ML_HEAD_DIM), F32),
                 jax.ShapeDtypeStruct((n_seq, 2, ML_HEADS, ML_HEAD_DIM), F32),
                 jax.ShapeDtypeStruct((n_seq, 2 * ML_HEADS, ML_HEAD_DIM), F32))
    out_specs = (pl.BlockSpec((seq_len, ML_WIDTH), lambda b: (b, 0)),
                 pl.BlockSpec((1, 2, ML_HEADS, ML_HEAD_DIM, ML_HEAD_DIM), lambda b: (b, 0, 0, 0, 0)),
                 pl.BlockSpec((1, 2, ML_HEADS, ML_HEAD_DIM), lambda b: (b, 0, 0, 0)),
                 pl.BlockSpec((1, 2 * ML_HEADS, ML_HEAD_DIM), lambda b: (b, 0, 0)))
    scratch = [pltpu.VMEM((seq_len, ML_WIDTH), F32), pltpu.VMEM((seq_len, ML_WIDTH), F32),
               pltpu.VMEM((2 * ML_HEADS, ML_HEAD_DIM, ML_HEAD_DIM), F32),
               pltpu.VMEM((2 * ML_HEADS, 1, ML_HEAD_DIM), F32),
               pltpu.VMEM((2 * ML_HEADS, 1, ML_HEAD_DIM), F32)]
    return pl.pallas_call(
        functools.partial(_mlstm_kernel, seq_len=seq_len, has_state=has_state),
        out_shape=out_shape, grid=(n_seq,), in_specs=in_specs, out_specs=out_specs,
        scratch_shapes=scratch, compiler_params=_cparams(("arbitrary",)),
        name=f"mlstm_{seq_len}",
    )(*args)


def _dft_mats(seq_len):
    lo = 32
    hi = seq_len // lo
    d = jnp.arange(seq_len, dtype=jnp.int32)
    a = jnp.arange(hi, dtype=jnp.int32) * lo
    b = jnp.arange(lo, dtype=jnp.int32)
    scale = math.pi / seq_len
    ang_a = ((a[:, None] * d[None, :]) % (2 * seq_len)).astype(F32) * scale
    ang_b = ((b[:, None] * d[None, :]) % (2 * seq_len)).astype(F32) * scale
    ca, sa, cb, sb = jnp.cos(ang_a), jnp.sin(ang_a), jnp.cos(ang_b), jnp.sin(ang_b)
    cosm = (ca[:, None, :] * cb[None, :, :] - sa[:, None, :] * sb[None, :, :]).reshape(seq_len, seq_len)
    sinm = (sa[:, None, :] * cb[None, :, :] + ca[:, None, :] * sb[None, :, :]).reshape(seq_len, seq_len)
    nyq = jnp.where(d % 2 == 0, 1.0, -1.0).astype(F32)
    krow = jnp.arange(seq_len, dtype=jnp.int32)[:, None]
    sinm = jnp.where(krow == 0, nyq[None, :], sinm)
    f = jnp.concatenate([cosm, sinm], axis=0).astype(BF16)
    cos_t = (ca.T[:, :, None] * cb.T[:, None, :] - sa.T[:, :, None] * sb.T[:, None, :]).reshape(seq_len, seq_len)
    sin_t = (sa.T[:, :, None] * cb.T[:, None, :] + ca.T[:, :, None] * sb.T[:, None, :]).reshape(seq_len, seq_len)
    kcol = jnp.arange(seq_len, dtype=jnp.int32)[None, :]
    sin_t = jnp.where(kcol == 0, nyq[:, None], sin_t)
    ft = jnp.concatenate([cos_t, sin_t], axis=1).astype(BF16)
    return f, ft


def _filter_feats(seq_len):
    t = jnp.linspace(0.0, 1.0, seq_len, dtype=F32)[:, None]
    wpos = 2.0 * math.pi * jnp.arange(seq_len, dtype=F32)[:, None] / seq_len
    bands = jnp.linspace(1e-4, HY_BANDS - 1, HY_BANDS, dtype=F32)[None, :]
    z = jnp.concatenate([t, jnp.cos(bands * wpos), -jnp.sin(bands * wpos)], axis=-1)
    return jnp.pad(z, ((0, 0), (0, 128 - HY_EMB)))


def _filter_kernel(z_ref, w1_ref, b1_ref, w2_ref, b2_ref, w3_ref, b3_ref, dec_ref, f_ref,
                   a_ref, b_ref, d_ref, *, seq_len):
    n = 2 * seq_len
    z = z_ref[...]
    h = jnp.sin(_dot3(z, w1_ref[...]) + b1_ref[...])
    h = jnp.sin(_dot3(h, w2_ref[...]) + b2_ref[...])
    h = _dot3(h, w3_ref[...]) + b3_ref[...]
    t = z[:, 0:1]
    h = h * (jnp.exp(-t * jnp.abs(dec_ref[...])) + HY_MOD_SHIFT)
    ss = jnp.sum(h * h, axis=0, keepdims=True)
    inv = lax.rsqrt(ss[:, :HY_WIDTH] + ss[:, HY_WIDTH:] + EPS)
    hp = h[:, :HY_WIDTH] * inv
    hn = h[:, HY_WIDTH:] * inv
    ssum = hp + hn
    sdif = hp - hn
    hc = _dot(f_ref[0:seq_len, :], ssum.astype(BF16))
    hs = _dot(f_ref[seq_len:n, :], sdif.astype(BF16))
    di = lax.broadcasted_iota(jnp.int32, (seq_len, 1), 0)
    sgn = jnp.where(di % 2 == 0, 1.0, -1.0)
    nyq = jnp.sum(ssum * sgn, axis=0, keepdims=True)
    first = di == 0
    a_ref[0] = hc * jnp.where(first, 1.0 / n, 2.0 / n)
    b_ref[0] = jnp.where(first, 0.0, hs * (2.0 / n))
    d_ref[0] = jnp.where(first, nyq * (1.0 / n), hc * (2.0 / n))


def _hyena_filters(seq_len, f, w1p, b1, w2, b2, w3, b3, dec):
    z = _filter_feats(seq_len)
    hid = HY_FILTER_HIDDEN
    oc = 2 * HY_WIDTH
    full = lambda shape: pl.BlockSpec(shape, lambda o: tuple(0 for _ in shape))
    out = jax.ShapeDtypeStruct((HY_ORDER, seq_len, HY_WIDTH), F32)
    return pl.pallas_call(
        functools.partial(_filter_kernel, seq_len=seq_len),
        out_shape=(out, out, out),
        grid=(HY_ORDER,),
        in_specs=[full((seq_len, 128)), full((128, hid)), full((1, hid)), full((hid, hid)), full((1, hid)),
                  pl.BlockSpec((hid, oc), lambda o: (0, o)),
                  pl.BlockSpec((1, oc), lambda o: (0, o)),
                  pl.BlockSpec((1, oc), lambda o: (0, o)),
                  full((2 * seq_len, seq_len))],
        out_specs=tuple(pl.BlockSpec((1, seq_len, HY_WIDTH), lambda o: (o, 0, 0)) for _ in range(3)),
        compiler_params=_cparams(("arbitrary",)),
        name=f"hyena_filter_{seq_len}",
    )(z, w1p, b1, w2, b2, w3, b3, dec, f)


def _hyena_kernel(x1_ref, x2_ref, v_ref, cw1_ref, cw2_ref, cwv_ref, a_ref, b_ref, d_ref, bias_ref,
                  f_ref, ft_ref, z_ref, *, seq_len, width):
    ti = lax.broadcasted_iota(jnp.int32, (seq_len, 1), 0)
    has_prev = (ti % width) != 0
    has_next = (ti % width) != (width - 1)

    def short_conv(x_ref, w_ref):
        x = x_ref[...].astype(F32)
        prev = jnp.where(has_prev, pltpu.roll(x, 1, axis=0), 0.0)
        nxt = jnp.where(has_next, pltpu.roll(x, seq_len - 1, axis=0), 0.0)
        return w_ref[0:1, :] * prev + w_ref[1:2, :] * x + w_ref[2:3, :] * nxt

    gates = (short_conv(x1_ref, cw1_ref), short_conv(x2_ref, cw2_ref))
    z = short_conv(v_ref, cwv_ref)
    for o in range(HY_ORDER):
        u = _dot(f_ref[...], z.astype(BF16))
        ut = u[:seq_len]
        ub = u[seq_len:]
        a, b, dd = a_ref[o], b_ref[o], d_ref[o]
        yt = ut * a - ub * b
        yb = ut * b + ub * dd
        y = _dot(ft_ref[:, :seq_len], yt.astype(BF16)) + _dot(ft_ref[:, seq_len:], yb.astype(BF16))
        z = gates[o] * (y + bias_ref[o:o + 1, :] * z)
    z_ref[...] = z.astype(BF16)


def _hyena(proj, conv_w, coefs, hy_bias, f, ft, seq_len, n_seq, row_block_off, width):
    cb = 256
    nblk = HY_WIDTH // cb
    base = ML_QKVO_COLS // cb
    off = row_block_off
    a, b, d = coefs

    def col_spec(part):
        return pl.BlockSpec((seq_len, cb), lambda s, j: (off + s, base + part * nblk + j))

    def w_spec(part):
        return pl.BlockSpec((3, cb), lambda s, j: (0, part * nblk + j))

    coef_spec = pl.BlockSpec((HY_ORDER, seq_len, cb), lambda s, j: (0, 0, j))
    return pl.pallas_call(
        functools.partial(_hyena_kernel, seq_len=seq_len, width=width),
        out_shape=jax.ShapeDtypeStruct((n_seq * seq_len, HY_WIDTH), BF16),
        grid=(n_seq, nblk),
        in_specs=[col_spec(0), col_spec(1), col_spec(2), w_spec(0), w_spec(1), w_spec(2),
                  coef_spec, coef_spec, coef_spec,
                  pl.BlockSpec((HY_ORDER, cb), lambda s, j: (0, j)),
                  pl.BlockSpec((2 * seq_len, seq_len), lambda s, j: (0, 0)),
                  pl.BlockSpec((seq_len, 2 * seq_len), lambda s, j: (0, 0))],
        out_specs=pl.BlockSpec((seq_len, cb), lambda s, j: (s, j)),
        compiler_params=_cparams(("arbitrary", "arbitrary")),
        name=f"hyena_conv_{seq_len}",
    )(proj, proj, proj, conv_w, conv_w, conv_w, a, b, d, hy_bias, f, ft)


def _first_max(x, n):
    mx = jnp.max(x, axis=0, keepdims=True)
    row = lax.broadcasted_iota(jnp.int32, x.shape, 0).astype(F32)
    idx = jnp.min(jnp.where(x == mx, row, float(n)), axis=0, keepdims=True)
    return mx, idx.astype(jnp.int32)


ROUTER_ROWS = 32
PAIRS_PER_GROUP = 6
N_BUCKETS = N_GROUPS * PAIRS_PER_GROUP
PAIR_SLOTS = ((0, 1), (0, 2), (0, 3), (1, 3), (1, 2), (3, 2))
LANES = 128
FEAT_SLABS = D_MODEL // LANES
H2_SLABS = FEAT_SLABS + 1
ROW_TILE = 256
ROW_CAP = T_ALL + N_BUCKETS * ROW_TILE
N_ROW_TILES = ROW_CAP // ROW_TILE


def _outproj_kernel(xp_ref, xs_ref, yp_ref, ys_ref, zp_ref, zs_ref, m_ref, gn_ref, wo_ref, wr_ref, br_ref,
                    x1_ref, h2_ref, bid_ref):
    i = pl.program_id(0)
    is_p = i < N_TILES_P
    x = jnp.where(is_p, xp_ref[...], xs_ref[...])
    yml = jnp.where(is_p, yp_ref[...], ys_ref[...])
    zz = jnp.where(is_p, zp_ref[...], zs_ref[...])
    y = _dot(yml, wo_ref[0:ML_WIDTH, :]) + _dot(zz, wo_ref[ML_WIDTH:, :])
    x1 = x + m_ref[0, 2:3, :] * _rms(y, gn_ref[1:2, :])
    x1_ref[...] = x1
    h2 = _rms(x1, gn_ref[2:3, :]) * (1.0 + m_ref[0, 4:5, :]) + m_ref[0, 3:4, :]
    for s in range(FEAT_SLABS):
        h2_ref[:, s, :] = h2[:, s * LANES:(s + 1) * LANES]
    h2h, h2l = _split2(h2)
    wrh, wrl = _split2(wr_ref[...])
    logits = _dot_nt(wrh, h2h) + _dot_nt(wrh, h2l) + _dot_nt(wrl, h2h) + br_ref[...]
    lc = logits[0:N_GROUPS]
    mx, gi = _first_max(lc, N_GROUPS)
    p_grp = 1.0 / jnp.sum(jnp.exp(lc - mx), axis=0, keepdims=True)
    lsel = jnp.zeros((EXPERTS_PER_GROUP, TILE), F32)
    for g in range(N_GROUPS):
        lo = N_GROUPS + g * EXPERTS_PER_GROUP
        lsel = jnp.where(gi == g, logits[lo:lo + EXPERTS_PER_GROUP], lsel)
    l1, i1 = _first_max(lsel, EXPERTS_PER_GROUP)
    sub4 = lax.broadcasted_iota(jnp.int32, lsel.shape, 0)
    l2, i2 = _first_max(jnp.where(sub4 == i1, -jnp.inf, lsel), EXPERTS_PER_GROUP)
    e2 = jnp.exp(l2 - l1)
    w1 = p_grp / (1.0 + e2)
    w2 = p_grp * e2 / (1.0 + e2)
    lo_e = jnp.minimum(i1, i2)
    hi_e = jnp.maximum(i1, i2)
    pair = jnp.where(lo_e == 0, hi_e - 1, jnp.where(lo_e == 1, jnp.where(hi_e == 3, 3, 4), 5))
    slot_a = jnp.where(pair == 5, hi_e, lo_e)
    first_in_a = i1 == slot_a
    w_a = jnp.where(first_in_a, w1, w2)
    w_b = jnp.where(first_in_a, w2, w1)
    sub = lax.broadcasted_iota(jnp.int32, (LANES, TILE), 0)
    gate_rows = jnp.where(sub == 0, w_a, jnp.where(sub == 1, w_b, 0.0))
    h2_ref[:, FEAT_SLABS, :] = _rows_to_cols(gate_rows)
    bid_ref[0] = gi * PAIRS_PER_GROUP + pair


def _outproj(xp, xs, yp, ys, zp, zs, mods3, g_norm, w_out, w_r, b_r):
    tps = DEC_SEQ // TILE
    pidx = lambda i: (jnp.minimum(i, N_TILES_P - 1), 0)
    sidx = lambda i: (jnp.maximum(i - N_TILES_P, 0), 0)
    return pl.pallas_call(
        _outproj_kernel,
        out_shape=(jax.ShapeDtypeStruct((T_ALL, D_MODEL), F32),
                   jax.ShapeDtypeStruct((T_ALL, H2_SLABS, LANES), F32),
                   jax.ShapeDtypeStruct((N_TILES, 1, TILE), jnp.int32)),
        grid=(N_TILES,),
        in_specs=[pl.BlockSpec((TILE, D_MODEL), pidx), pl.BlockSpec((TILE, D_MODEL), sidx),
                  pl.BlockSpec((TILE, ML_WIDTH), pidx), pl.BlockSpec((TILE, ML_WIDTH), sidx),
                  pl.BlockSpec((TILE, HY_WIDTH), pidx), pl.BlockSpec((TILE, HY_WIDTH), sidx),
                  pl.BlockSpec((1, N_MOD, D_MODEL), lambda i: (_mod_row_of_tile(i, tps, N_TILES_P), 0, 0)),
                  pl.BlockSpec((4, D_MODEL), lambda i: (0, 0)),
                  pl.BlockSpec((D_MODEL, D_MODEL), lambda i: (0, 0)),
                  pl.BlockSpec((ROUTER_ROWS, D_MODEL), lambda i: (0, 0)),
                  pl.BlockSpec((ROUTER_ROWS, 1), lambda i: (0, 0))],
        out_specs=(pl.BlockSpec((TILE, D_MODEL), lambda i: (i, 0)),
                   pl.BlockSpec((TILE, H2_SLABS, LANES), lambda i: (i, 0, 0)),
                   pl.BlockSpec((1, 1, TILE), lambda i: (i, 0, 0))),
        compiler_params=_cparams(("arbitrary",)),
        name="out_proj_router",
    )(xp, xs, yp, ys, zp, zs, mods3, g_norm, w_out, w_r, b_r)


def _route_kernel(bid_ref, pos_ref, meta_ref):
    nb = 32
    tm = float(ROW_TILE)
    sub = lax.broadcasted_iota(jnp.int32, (nb, TILE), 0)
    ri = lax.broadcasted_iota(jnp.int32, (TILE, TILE), 0)
    ci = lax.broadcasted_iota(jnp.int32, (TILE, TILE), 1)
    before = jnp.where(ri < ci, 1.0, 0.0).astype(BF16)

    def onehot(blk):
        return jnp.where(sub == bid_ref[blk], 1.0, 0.0)

    zeros = jnp.zeros((nb, 1), F32)
    cnt = lax.fori_loop(0, N_TILES, lambda blk, c: c + jnp.sum(onehot(blk), axis=1, keepdims=True), zeros)
    padded = jnp.floor((cnt + (tm - 1.0)) * (1.0 / tm)) * tm
    r32 = lax.broadcasted_iota(jnp.int32, (nb, nb), 0)
    c32 = lax.broadcasted_iota(jnp.int32, (nb, nb), 1)
    padded_row = jnp.sum(jnp.where(r32 == c32, padded, 0.0), axis=0, keepdims=True)
    offs = jnp.sum(jnp.where(c32 < r32, padded_row, 0.0), axis=1, keepdims=True)
    ends = offs + padded

    def place(blk, seen):
        oh = onehot(blk)
        rank = _dot(oh.astype(BF16), before)
        pos = jnp.sum(oh * (rank + seen + offs), axis=0, keepdims=True)
        pos_ref[blk] = pos.astype(jnp.int32)
        return seen + jnp.sum(oh, axis=1, keepdims=True)

    lax.fori_loop(0, N_TILES, place, zeros)

    start = lax.broadcasted_iota(jnp.int32, (nb, 128), 1).astype(F32) * tm
    bsub = lax.broadcasted_iota(jnp.int32, (nb, 128), 0)
    done = jnp.where((bsub < N_BUCKETS) & (ends <= start), 1.0, 0.0)
    tb = jnp.sum(done, axis=0, keepdims=True)
    valid = jnp.where(tb < N_BUCKETS, 1.0, 0.0)
    tbc = jnp.minimum(tb, N_BUCKETS - 1.0)
    grp = jnp.floor((tbc + 0.5) * (1.0 / PAIRS_PER_GROUP))
    pair = tbc - PAIRS_PER_GROUP * grp
    loc_a = jnp.zeros_like(pair)
    loc_b = jnp.zeros_like(pair)
    for k, (sa, sb) in enumerate(PAIR_SLOTS):
        loc_a = jnp.where(pair == k, float(sa), loc_a)
        loc_b = jnp.where(pair == k, float(sb), loc_b)
    row8 = lax.broadcasted_iota(jnp.int32, (8, 128), 0)
    meta = jnp.where(row8 == 0, grp * EXPERTS_PER_GROUP + loc_a,
                     jnp.where(row8 == 1, grp * EXPERTS_PER_GROUP + loc_b,
                               jnp.where(row8 == 2, valid, 0.0)))
    meta_ref[...] = meta.astype(jnp.int32)


def _route(bid):
    return pl.pallas_call(
        _route_kernel,
        out_shape=(jax.ShapeDtypeStruct((N_TILES, 1, TILE), jnp.int32),
                   jax.ShapeDtypeStruct((8, 128), jnp.int32)),
        compiler_params=pltpu.CompilerParams(vmem_limit_bytes=VMEM_LIMIT),
        name="moe_route",
    )(bid)


def _moe_kernel(meta_ref, pos_ref, h2_hbm, wga_ref, wua_ref, wda_ref, wgb_ref, wub_ref, wdb_ref,
                y_ref, src_ref, xbuf, sem, wga_s, wua_s, wda_s, wgb_s, wub_s, wdb_s):
    j = pl.program_id(0)

    def row_copy(tile, r, slot):
        tok = src_ref[tile * ROW_TILE + r]
        return pltpu.make_async_copy(h2_hbm.at[tok], xbuf.at[slot, r], sem.at[slot])

    def issue(tile, slot):
        def body(r, c):
            row_copy(tile, r, slot).start()
            return c
        lax.fori_loop(0, ROW_TILE, body, 0, unroll=8)

    def wait(tile, slot):
        def body(r, c):
            row_copy(tile, r, slot).wait()
            return c
        lax.fori_loop(0, ROW_TILE, body, 0, unroll=8)

    @pl.when(j == 0)
    def _():
        def clear(p, c):
            src_ref[p] = 0
            return c
        lax.fori_loop(0, ROW_CAP, clear, 0, unroll=8)

        def invert(t, c):
            src_ref[pos_ref[t]] = t
            return c
        lax.fori_loop(0, T_ALL, invert, 0, unroll=8)

        @pl.when(meta_ref[2, 0] == 1)
        def _():
            issue(0, 0)

    nxt = jnp.minimum(j + 1, N_ROW_TILES - 1)

    @pl.when((j + 1 < N_ROW_TILES) & (meta_ref[2, nxt] == 1))
    def _():
        issue(nxt, nxt % 2)

    valid = meta_ref[2, j] == 1
    prev = jnp.maximum(j - 1, 0)

    @pl.when(valid & ((j == 0) | (meta_ref[0, j] != meta_ref[0, prev])))
    def _():
        wga_s[...] = wga_ref[0].astype(BF16)
        wua_s[...] = wua_ref[0].astype(BF16)
        wda_s[...] = wda_ref[0].astype(BF16)

    @pl.when(valid & ((j == 0) | (meta_ref[1, j] != meta_ref[1, prev])))
    def _():
        wgb_s[...] = wgb_ref[0].astype(BF16)
        wub_s[...] = wub_ref[0].astype(BF16)
        wdb_s[...] = wdb_ref[0].astype(BF16)

    @pl.when(valid)
    def _():
        slot = j % 2
        wait(j, slot)
        x = jnp.concatenate([xbuf[slot, :, s, :] for s in range(FEAT_SLABS)], axis=1).astype(BF16)
        gates = xbuf[slot, :, FEAT_SLABS, :]

        def expert(wg, wu, gate):
            hg = _dot(x, wg[...])
            hu = _dot(x, wu[...])
            return (hg * jax.nn.sigmoid(hg) * hu * gate).astype(BF16)

        act_a = expert(wga_s, wua_s, gates[:, 0:1])
        act_b = expert(wgb_s, wub_s, gates[:, 1:2])
        y = _dot(act_a, wda_s[...]) + _dot(act_b, wdb_s[...])
        for s in range(FEAT_SLABS):
            y_ref[:, s, :] = y[:, s * LANES:(s + 1) * LANES]

    @pl.when(jnp.logical_not(valid))
    def _():
        y_ref[...] = jnp.zeros_like(y_ref)


def _moe(meta, pos, h2ext, w_gate, w_up, w_down):
    up_spec = lambda slot: pl.BlockSpec((1, D_MODEL, EXPERT_FF), lambda j, meta, pos: (meta[slot, j], 0, 0))
    down_spec = lambda slot: pl.BlockSpec((1, EXPERT_FF, D_MODEL), lambda j, meta, pos: (meta[slot, j], 0, 0))
    grid_spec = pltpu.PrefetchScalarGridSpec(
        num_scalar_prefetch=2,
        grid=(N_ROW_TILES,),
        in_specs=[pl.BlockSpec(memory_space=pl.ANY),
                  up_spec(0), up_spec(0), down_spec(0), up_spec(1), up_spec(1), down_spec(1)],
        out_specs=pl.BlockSpec((ROW_TILE, FEAT_SLABS, LANES), lambda j, meta, pos: (j, 0, 0)),
        scratch_shapes=[pltpu.SMEM((ROW_CAP,), jnp.int32),
                        pltpu.VMEM((2, ROW_TILE, H2_SLABS, LANES), F32),
                        pltpu.SemaphoreType.DMA((2,)),
                        pltpu.VMEM((D_MODEL, EXPERT_FF), BF16), pltpu.VMEM((D_MODEL, EXPERT_FF), BF16),
                        pltpu.VMEM((EXPERT_FF, D_MODEL), BF16),
                        pltpu.VMEM((D_MODEL, EXPERT_FF), BF16), pltpu.VMEM((D_MODEL, EXPERT_FF), BF16),
                        pltpu.VMEM((EXPERT_FF, D_MODEL), BF16)])
    return pl.pallas_call(
        _moe_kernel,
        out_shape=jax.ShapeDtypeStruct((ROW_CAP, FEAT_SLABS, LANES), F32),
        grid_spec=grid_spec,
        compiler_params=_cparams(("arbitrary",)),
        name="moe_experts",
    )(meta, pos, h2ext, w_gate, w_up, w_down, w_gate, w_up, w_down)


def _final_kernel(pos_ref, y_hbm, x1_ref, m_ref, gn_ref, op_ref, os_ref, ybuf, sem):
    i = pl.program_id(0)

    def row_copy(tile, r, slot):
        p = pos_ref[tile * TILE + r]
        return pltpu.make_async_copy(y_hbm.at[p], ybuf.at[slot, r], sem.at[slot])

    def issue(tile, slot):
        def body(r, c):
            row_copy(tile, r, slot).start()
            return c
        lax.fori_loop(0, TILE, body, 0, unroll=8)

    def wait(tile, slot):
        def body(r, c):
            row_copy(tile, r, slot).wait()
            return c
        lax.fori_loop(0, TILE, body, 0, unroll=8)

    @pl.when(i == 0)
    def _():
        issue(0, 0)

    @pl.when(i + 1 < N_TILES)
    def _():
        issue(i + 1, (i + 1) % 2)

    slot = i % 2
    wait(i, slot)
    y = jnp.concatenate([ybuf[slot, :, s, :] for s in range(FEAT_SLABS)], axis=1)
    out = x1_ref[...] + m_ref[0, 5:6, :] * _rms(y, gn_ref[3:4, :])

    @pl.when(i < N_TILES_P)
    def _():
        op_ref[...] = out

    @pl.when(i >= N_TILES_P)
    def _():
        os_ref[...] = out


def _final(pos, y_sorted, x1, mods3, g_norm):
    tps = DEC_SEQ // TILE
    grid_spec = pltpu.PrefetchScalarGridSpec(
        num_scalar_prefetch=1,
        grid=(N_TILES,),
        in_specs=[pl.BlockSpec(memory_space=pl.ANY),
                  pl.BlockSpec((TILE, D_MODEL), lambda i, pos: (i, 0)),
                  pl.BlockSpec((1, N_MOD, D_MODEL), lambda i, pos: (_mod_row_of_tile(i, tps, N_TILES_P), 0, 0)),
                  pl.BlockSpec((4, D_MODEL), lambda i, pos: (0, 0))],
        out_specs=(pl.BlockSpec((TILE, D_MODEL), lambda i, pos: (jnp.minimum(i, N_TILES_P - 1), 0)),
                   pl.BlockSpec((TILE, D_MODEL), lambda i, pos: (jnp.maximum(i - N_TILES_P, 0), 0))),
        scratch_shapes=[pltpu.VMEM((2, TILE, FEAT_SLABS, LANES), F32), pltpu.SemaphoreType.DMA((2,))])
    return pl.pallas_call(
        _final_kernel,
        out_shape=(jax.ShapeDtypeStruct((T_PROMPT, D_MODEL), F32),
                   jax.ShapeDtypeStruct((T_SAMPLE, D_MODEL), F32)),
        grid_spec=grid_spec,
        compiler_params=_cparams(("arbitrary",)),
        name="moe_combine_final",
    )(pos, y_sorted, x1, mods3, g_norm)


def kernel(x_prompt, x_sample, state_C, state_n, state_m, c, c_ctx, w_ada, b_ada, g_norm, w_in, ml_gate_bias, ml_head_gain, hy_conv_w, hy_f_w1, hy_f_b1, hy_f_w2, hy_f_b2, hy_f_w3, hy_f_b3, hy_decay, hy_bias, w_out, w_rc, b_rc, w_rf, b_rf, w_gate, w_up, w_down):
    xp = x_prompt.reshape(T_PROMPT, D_MODEL)
    xs = x_sample.reshape(T_SAMPLE, D_MODEL)
    gn = g_norm[0]

    cv = jnp.concatenate([c_ctx[None, :], c, jnp.zeros((MOD_ROWS - 1 - DEC_BATCH, D_MODEL), F32)], axis=0)
    mods3 = _ada(cv, w_ada[0], b_ada[0]).reshape(MOD_ROWS, N_MOD, D_MODEL)

    w_in0 = w_in[0]
    w_main = jnp.concatenate([w_in0[:, :ML_QKVO_COLS], w_in0[:, ML_QKVO_COLS + ML_GATE_COLS:]],
                             axis=1).astype(BF16)
    wg = w_in0[:, ML_QKVO_COLS:ML_QKVO_COLS + ML_GATE_COLS]
    gbt = ml_gate_bias[0].reshape(ML_GATE_COLS, 1)
    proj, gates, gates_t = _inproj(xp, xs, mods3, gn, w_main, wg.T, gbt)

    gain = ml_head_gain[0].reshape(1, ML_WIDTH)
    y_ml_p, c_new, n_new, m_new = _mlstm(proj, gates, gates_t, gain, None, SEQ, BATCH, 0)
    state = (state_C[:, 0], state_n[:, 0], state_m[:, 0].reshape(DEC_BATCH, 2 * ML_HEADS, 1))
    y_ml_s, _, _, _ = _mlstm(proj, gates, gates_t, gain, state, DEC_SEQ, DEC_BATCH, T_PROMPT // DEC_SEQ)

    w1p = jnp.pad(hy_f_w1[0], ((0, 128 - HY_EMB), (0, 0)))
    b1 = hy_f_b1[0].reshape(1, -1)
    b2 = hy_f_b2[0].reshape(1, -1)
    b3 = hy_f_b3[0].reshape(1, -1)
    dec = hy_decay[0].reshape(1, -1)
    z_parts = []
    for seq_len, n_seq, off, width in ((SEQ, BATCH, 0, SEQ), (DEC_SEQ, DEC_BATCH, T_PROMPT // DEC_SEQ, GRID_W)):
        f, ft = _dft_mats(seq_len)
        coefs = _hyena_filters(seq_len, f, w1p, b1, hy_f_w2[0], b2, hy_f_w3[0], b3, dec)
        z_parts.append(_hyena(proj, hy_conv_w[0], coefs, hy_bias[0], f, ft, seq_len, n_seq, off, width))
    z_p, z_s = z_parts

    pad_r = ROUTER_ROWS - N_GROUPS - N_EXPERTS
    w_r = jnp.pad(jnp.concatenate([w_rc[0], w_rf[0]], axis=1).T, ((0, pad_r), (0, 0)))
    b_r = jnp.pad(jnp.concatenate([b_rc[0], b_rf[0]], axis=0), (0, pad_r)).reshape(ROUTER_ROWS, 1)
    x1, h2ext, bid = _outproj(xp, xs, y_ml_p, y_ml_s, z_p, z_s, mods3, gn, w_out[0].astype(BF16), w_r, b_r)

    pos3, meta = _route(bid)
    pos = pos3.reshape(T_ALL)
    y_sorted = _moe(meta, pos, h2ext, w_gate[0], w_up[0], w_down[0])
    y_p, y_s = _final(pos, y_sorted, x1, mods3, gn)

    new_c = c_new.reshape(BATCH, 1, 2, ML_HEADS, ML_HEAD_DIM, ML_HEAD_DIM)
    new_n = n_new.reshape(BATCH, 1, 2, ML_HEADS, ML_HEAD_DIM)
    new_m = m_new[:, :, 0].reshape(BATCH, 1, 2, ML_HEADS)
    return (y_p.reshape(BATCH, SEQ, D_MODEL), y_s.reshape(DEC_BATCH, DEC_SEQ, D_MODEL), new_c, new_n, new_m)
```

```python
import functools
import math

import jax
import jax.numpy as jnp
from jax import lax
from jax.experimental import pallas as pl
from jax.experimental.pallas import tpu as pltpu

F32 = jnp.float32
BF16 = jnp.bfloat16

D_MODEL = 1024
BATCH = 16
SEQ = 256
DEC_BATCH = 4
DEC_SEQ = 1024
GRID_W = 64
ML_WIDTH = 512
ML_HEADS = 4
ML_HEAD_DIM = 128
HY_WIDTH = 512
HY_ORDER = 2
HY_EMB = 33
HY_BANDS = 16
HY_FILTER_HIDDEN = 64
HY_MOD_SHIFT = 0.05
N_GROUPS = 4
EXPERTS_PER_GROUP = 4
N_EXPERTS = 16
EXPERT_FF = 512
N_MOD = 6
EPS = 1e-6
ML_QKVO_COLS = 4 * ML_WIDTH
ML_GATE_COLS = 4 * ML_HEADS
HY_COLS = 3 * HY_WIDTH
MAIN_COLS = ML_QKVO_COLS + HY_COLS

T_PROMPT = BATCH * SEQ
T_SAMPLE = DEC_BATCH * DEC_SEQ
T_ALL = T_PROMPT + T_SAMPLE
TILE = 256
N_TILES_P = T_PROMPT // TILE
N_TILES = T_ALL // TILE
MOD_ROWS = 8
K_SCALE = ML_HEAD_DIM ** -0.5
VMEM_LIMIT = 56 * 1024 * 1024


def _cparams(sem):
    return pltpu.CompilerParams(dimension_semantics=sem, vmem_limit_bytes=VMEM_LIMIT)


def _split2(x):
    hi = x.astype(BF16)
    lo = (x - hi.astype(F32)).astype(BF16)
    return hi, lo


def _dot(a, b):
    return jnp.dot(a, b, preferred_element_type=F32)


def _dot_nt(a, b):
    return lax.dot_general(a, b, (((1,), (1,)), ((), ())), preferred_element_type=F32)


def _dot_tn(a, b):
    return lax.dot_general(a, b, (((0,), (0,)), ((), ())), preferred_element_type=F32)


def _dot3(a, b):
    ah, al = _split2(a)
    bh, bl = _split2(b)
    return _dot(ah, bh) + _dot(al, bh) + _dot(ah, bl)


def _dot3_nt(a, b):
    ah, al = _split2(a)
    bh, bl = _split2(b)
    return _dot_nt(ah, bh) + _dot_nt(al, bh) + _dot_nt(ah, bl)


def _dot_exact_lhs(t, x):
    x1 = x.astype(BF16)
    r1 = x - x1.astype(F32)
    x2 = r1.astype(BF16)
    x3 = (r1 - x2.astype(F32)).astype(BF16)
    return _dot(t, x1) + _dot(t, x2) + _dot(t, x3)


def _dot_exact_rhs(x, t):
    x1 = x.astype(BF16)
    r1 = x - x1.astype(F32)
    x2 = r1.astype(BF16)
    x3 = (r1 - x2.astype(F32)).astype(BF16)
    return _dot(x1, t) + _dot(x2, t) + _dot(x3, t)


def _rms(x, g):
    return x * lax.rsqrt(jnp.mean(x * x, axis=-1, keepdims=True) + EPS) * g


def _mod_row_of_tile(i, tiles_per_sample_seq, n_prompt_tiles):
    return jnp.where(i < n_prompt_tiles, 0, 1 + (i - n_prompt_tiles) // tiles_per_sample_seq)


def _ada_kernel(cv_ref, w_ref, b_ref, o_ref):
    cv = cv_ref[...]
    s = cv * jax.nn.sigmoid(cv)
    o_ref[...] = _dot3(s, w_ref[...]) + b_ref[...]


def _ada(cv, w_ada, b_ada):
    n = N_MOD * D_MODEL
    return pl.pallas_call(
        _ada_kernel,
        out_shape=jax.ShapeDtypeStruct((MOD_ROWS, n), F32),
        grid=(N_MOD,),
        in_specs=[pl.BlockSpec((MOD_ROWS, D_MODEL), lambda j: (0, 0)),
                  pl.BlockSpec((D_MODEL, D_MODEL), lambda j: (0, j)),
                  pl.BlockSpec((1, D_MODEL), lambda j: (0, j))],
        out_specs=pl.BlockSpec((MOD_ROWS, D_MODEL), lambda j: (0, j)),
        compiler_params=_cparams(("arbitrary",)),
        name="ada_mod",
    )(cv, w_ada, b_ada.reshape(1, n))


def _log_sigmoid(x):
    return jnp.minimum(x, 0.0) - jnp.log1p(jnp.exp(-jnp.abs(x)))


def _rows_to_cols(rows):
    ri = lax.broadcasted_iota(jnp.int32, (TILE, TILE), 0)
    ci = lax.broadcasted_iota(jnp.int32, (TILE, TILE), 1)
    eye = jnp.where(ri == ci, 1.0, 0.0).astype(BF16)
    p1 = rows.astype(BF16)
    r1 = rows - p1.astype(F32)
    p2 = r1.astype(BF16)
    p3 = (r1 - p2.astype(F32)).astype(BF16)
    return _dot_nt(eye, p1) + _dot_nt(eye, p2) + _dot_nt(eye, p3)


def _inproj_kernel(xp_ref, xs_ref, m_ref, gn_ref, w_ref, wgt_ref, gbt_ref, proj_ref, gate_ref, gatet_ref):
    i = pl.program_id(0)
    x = jnp.where(i < N_TILES_P, xp_ref[...], xs_ref[...])
    h = _rms(x, gn_ref[0:1, :]) * (1.0 + m_ref[0, 1:2, :]) + m_ref[0, 0:1, :]
    hb = h.astype(BF16)
    cb = 512
    for j in range(MAIN_COLS // cb):
        proj_ref[:, j * cb:(j + 1) * cb] = _dot(hb, w_ref[:, j * cb:(j + 1) * cb]).astype(BF16)
    hl = (h - hb.astype(F32)).astype(BF16)
    wth, wtl = _split2(wgt_ref[...])
    gt = _dot_nt(wth, hb) + _dot_nt(wth, hl) + _dot_nt(wtl, hb) + gbt_ref[...]
    row = lax.broadcasted_iota(jnp.int32, gt.shape, 0)
    gt = jnp.where((row % 8) >= 4, _log_sigmoid(gt), gt)
    gatet_ref[0] = gt
    gate_ref[...] = _rows_to_cols(gt)


def _inproj(xp, xs, mods3, g_norm, w_main, wgt, gbt):
    tps = DEC_SEQ // TILE
    return pl.pallas_call(
        _inproj_kernel,
        out_shape=(jax.ShapeDtypeStruct((T_ALL, MAIN_COLS), BF16),
                   jax.ShapeDtypeStruct((T_ALL, ML_GATE_COLS), F32),
                   jax.ShapeDtypeStruct((N_TILES, ML_GATE_COLS, TILE), F32)),
        grid=(N_TILES,),
        in_specs=[pl.BlockSpec((TILE, D_MODEL), lambda i: (jnp.minimum(i, N_TILES_P - 1), 0)),
                  pl.BlockSpec((TILE, D_MODEL), lambda i: (jnp.maximum(i - N_TILES_P, 0), 0)),
                  pl.BlockSpec((1, N_MOD, D_MODEL), lambda i: (_mod_row_of_tile(i, tps, N_TILES_P), 0, 0)),
                  pl.BlockSpec((4, D_MODEL), lambda i: (0, 0)),
                  pl.BlockSpec((D_MODEL, MAIN_COLS), lambda i: (0, 0)),
                  pl.BlockSpec((ML_GATE_COLS, D_MODEL), lambda i: (0, 0)),
                  pl.BlockSpec((ML_GATE_COLS, 1), lambda i: (0, 0))],
        out_specs=(pl.BlockSpec((TILE, MAIN_COLS), lambda i: (i, 0)),
                   pl.BlockSpec((TILE, ML_GATE_COLS), lambda i: (i, 0)),
                   pl.BlockSpec((1, ML_GATE_COLS, TILE), lambda i: (i, 0, 0))),
        compiler_params=_cparams(("arbitrary",)),
        name="in_proj",
    )(xp, xs, mods3, g_norm, w_main, wgt, gbt)


def _mlstm_kernel(*refs, seq_len, has_state):
    if has_state:
        (q_ref, k_ref, v_ref, o_ref, g_ref, gt_ref, gain_ref, c0_ref, n0_ref, m0_ref,
         y_ref, c_ref, n_ref, m_ref, hf_ref, hb_ref, cs_ref, ns_ref, ms_ref) = refs
    else:
        (q_ref, k_ref, v_ref, o_ref, g_ref, gt_ref, gain_ref,
         y_ref, c_ref, n_ref, m_ref, hf_ref, hb_ref, cs_ref, ns_ref, ms_ref) = refs
    ch = TILE
    nc = seq_len // ch
    hd = ML_HEAD_DIM
    ri = lax.broadcasted_iota(jnp.int32, (ch, ch), 0)
    ci = lax.broadcasted_iota(jnp.int32, (ch, ch), 1)
    lower = ci <= ri
    upper = ci >= ri
    t_low = jnp.where(lower, 1.0, 0.0).astype(BF16)
    t_up = jnp.where(upper, 1.0, 0.0).astype(BF16)

    for d in range(2):
        for h in range(ML_HEADS):
            r = d * ML_HEADS + h
            if has_state:
                cs_ref[r] = c0_ref[0, d, h]
                ns_ref[r] = n0_ref[0, d, h:h + 1, :]
                ms_ref[r] = jnp.broadcast_to(m0_ref[0, r:r + 1, :], (1, hd))
            else:
                cs_ref[r] = jnp.zeros((hd, hd), F32)
                ns_ref[r] = jnp.zeros((1, hd), F32)
                ms_ref[r] = jnp.zeros((1, hd), F32)

    def step(t, carry):
        for d in range(2):
            c = t if d == 0 else nc - 1 - t
            r0 = pl.multiple_of(c * ch, ch)
            rows = pl.ds(r0, ch)
            gcol = g_ref[rows, :]
            grow = gt_ref[c]
            tmat_c = t_low if d == 0 else t_up
            tmat_r = t_up if d == 0 else t_low
            bcol_all = _dot_exact_lhs(tmat_c, gcol)
            brow_all = _dot_exact_rhs(grow, tmat_r)
            mask = lower if d == 0 else upper
            hacc_ref = hf_ref if d == 0 else hb_ref
            for h in range(ML_HEADS):
                r = d * ML_HEADS + h
                fcol = (1 + 2 * d) * ML_HEADS + h
                icol = (2 * d) * ML_HEADS + h
                cols = slice(h * hd, (h + 1) * hd)
                q = q_ref[rows, cols]
                k = k_ref[rows, cols]
                v = v_ref[rows, cols]
                c_prev = cs_ref[r]
                n_prev = ns_ref[r]
                m_prev = ms_ref[r][:, 0:1]
                b_col = bcol_all[:, fcol:fcol + 1]
                b_row = brow_all[fcol:fcol + 1, :]
                ig_row = grow[icol:icol + 1, :]
                ig_col = gcol[:, icol:icol + 1]
                logd = jnp.where(mask, b_col - b_row + ig_row, -jnp.inf)
                inter = b_col + m_prev
                m_pos = jnp.maximum(inter, jnp.max(logd, axis=-1, keepdims=True))
                s = _dot_nt(q, k) * K_SCALE * jnp.exp(logd - m_pos)
                sc_inter = jnp.exp(inter - m_pos)
                qf = q.astype(F32)
                num = sc_inter * _dot(q, c_prev.astype(BF16)) + _dot(s.astype(BF16), v)
                den = (sc_inter * jnp.sum(qf * n_prev, axis=-1, keepdims=True)
                       + jnp.sum(s, axis=-1, keepdims=True))
                hh = num / jnp.maximum(jnp.abs(den), jnp.exp(-m_pos))
                hacc_ref[rows, cols] = hh
                b_last = b_col[ch - 1:ch, :] if d == 0 else b_col[0:1, :]
                logw = b_last - b_col + ig_col
                m_new = jnp.maximum(b_last + m_prev, jnp.max(logw, axis=0, keepdims=True))
                w = jnp.exp(logw - m_new)
                decay = jnp.exp(b_last + m_prev - m_new)
                kw = k.astype(F32) * (w * K_SCALE)
                cs_ref[r] = decay * c_prev + _dot_tn(kw.astype(BF16), v)
                ns_ref[r] = decay * n_prev + jnp.sum(kw, axis=0, keepdims=True)
                ms_ref[r] = jnp.broadcast_to(m_new, (1, hd))
        return carry

    lax.fori_loop(0, nc, step, 0)

    for d in range(2):
        for h in range(ML_HEADS):
            r = d * ML_HEADS + h
            c_ref[0, d, h] = cs_ref[r]
            n_ref[0, d, h:h + 1, :] = ns_ref[r]
            m_ref[0, r:r + 1, :] = ms_ref[r]
    for h in range(ML_HEADS):
        cols = slice(h * hd, (h + 1) * hd)
        hh = hf_ref[:, cols] + hb_ref[:, cols]
        hh = hh * lax.rsqrt(jnp.mean(hh * hh, axis=-1, keepdims=True) + EPS)
        y = hh * gain_ref[:, cols] * jax.nn.sigmoid(o_ref[:, cols].astype(F32))
        y_ref[:, cols] = y.astype(BF16)


def _mlstm(proj, gates, gates_t, gain, state, seq_len, n_seq, row_block_off):
    has_state = state is not None
    tiles = seq_len // TILE
    off = row_block_off
    qkvo_specs = [pl.BlockSpec((seq_len, ML_WIDTH), functools.partial(lambda b, j: (off + b, j), j=j))
                  for j in range(4)]
    in_specs = qkvo_specs + [
        pl.BlockSpec((seq_len, ML_GATE_COLS), lambda b: (off + b, 0)),
        pl.BlockSpec((tiles, ML_GATE_COLS, TILE), lambda b: (off + b, 0, 0)),
        pl.BlockSpec((1, ML_WIDTH), lambda b: (0, 0)),
    ]
    args = [proj, proj, proj, proj, gates, gates_t, gain]
    if has_state:
        c0, n0, m0 = state
        in_specs += [
            pl.BlockSpec((1, 2, ML_HEADS, ML_HEAD_DIM, ML_HEAD_DIM), lambda b: (b, 0, 0, 0, 0)),
            pl.BlockSpec((1, 2, ML_HEADS, ML_HEAD_DIM), lambda b: (b, 0, 0, 0)),
            pl.BlockSpec((1, 2 * ML_HEADS, 1), lambda b: (b, 0, 0)),
        ]
        args += [c0, n0, m0]
    out_shape = (jax.ShapeDtypeStruct((n_seq * seq_len, ML_WIDTH), BF16),
                 jax.ShapeDtypeStruct((n_seq, 2, ML_HEADS, ML_HEAD_DIM, ML_HEAD_DIM), F32),
                 jax.ShapeDtypeStruct((n_seq, 2, ML_HEADS, ML_HEAD_DIM), F32),
                 jax.ShapeDtypeStruct((n_seq, 2 * ML_HEADS, ML_HEAD_DIM), F32))
    out_specs = (pl.BlockSpec((seq_len, ML_WIDTH), lambda b: (b, 0)),
                 pl.BlockSpec((1, 2, ML_HEADS, ML_HEAD_DIM, ML_HEAD_DIM), lambda b: (b, 0, 0, 0, 0)),
                 pl.BlockSpec((1, 2, ML_HEADS, ML_HEAD_DIM), lambda b: (b, 0, 0, 0)),
                 pl.BlockSpec((1, 2 * ML_HEADS, ML_HEAD_DIM), lambda b: (b, 0, 0)))
    scratch = [pltpu.VMEM((seq_len, ML_WIDTH), F32), pltpu.VMEM((seq_len, ML_WIDTH), F32),
               pltpu.VMEM((2 * ML_HEADS, ML_HEAD_DIM, ML_HEAD_DIM), F32),
               pltpu.VMEM((2 * ML_HEADS, 1, ML_HEAD_DIM), F32),
               pltpu.VMEM((2 * ML_HEADS, 1, ML_HEAD_DIM), F32)]
    return pl.pallas_call(
        functools.partial(_mlstm_kernel, seq_len=seq_len, has_state=has_state),
        out_shape=out_shape, grid=(n_seq,), in_specs=in_specs, out_specs=out_specs,
        scratch_shapes=scratch, compiler_params=_cparams(("arbitrary",)),
        name=f"mlstm_{seq_len}",
    )(*args)


def _dft_mats(seq_len):
    lo = 32
    hi = seq_len // lo
    d = jnp.arange(seq_len, dtype=jnp.int32)
    a = jnp.arange(hi, dtype=jnp.int32) * lo
    b = jnp.arange(lo, dtype=jnp.int32)
    scale = math.pi / seq_len
    ang_a = ((a[:, None] * d[None, :]) % (2 * seq_len)).astype(F32) * scale
    ang_b = ((b[:, None] * d[None, :]) % (2 * seq_len)).astype(F32) * scale
    ca, sa, cb, sb = jnp.cos(ang_a), jnp.sin(ang_a), jnp.cos(ang_b), jnp.sin(ang_b)
    cosm = (ca[:, None, :] * cb[None, :, :] - sa[:, None, :] * sb[None, :, :]).reshape(seq_len, seq_len)
    sinm = (sa[:, None, :] * cb[None, :, :] + ca[:, None, :] * sb[None, :, :]).reshape(seq_len, seq_len)
    nyq = jnp.where(d % 2 == 0, 1.0, -1.0).astype(F32)
    krow = jnp.arange(seq_len, dtype=jnp.int32)[:, None]
    sinm = jnp.where(krow == 0, nyq[None, :], sinm)
    f = jnp.concatenate([cosm, sinm], axis=0).astype(BF16)
    cos_t = (ca.T[:, :, None] * cb.T[:, None, :] - sa.T[:, :, None] * sb.T[:, None, :]).reshape(seq_len, seq_len)
    sin_t = (sa.T[:, :, None] * cb.T[:, None, :] + ca.T[:, :, None] * sb.T[:, None, :]).reshape(seq_len, seq_len)
    kcol = jnp.arange(seq_len, dtype=jnp.int32)[None, :]
    sin_t = jnp.where(kcol == 0, nyq[:, None], sin_t)
    ft = jnp.concatenate([cos_t, sin_t], axis=1).astype(BF16)
    return f, ft


def _filter_feats(seq_len):
    t = jnp.linspace(0.0, 1.0, seq_len, dtype=F32)[:, None]
    wpos = 2.0 * math.pi * jnp.arange(seq_len, dtype=F32)[:, None] / seq_len
    bands = jnp.linspace(1e-4, HY_BANDS - 1, HY_BANDS, dtype=F32)[None, :]
    z = jnp.concatenate([t, jnp.cos(bands * wpos), -jnp.sin(bands * wpos)], axis=-1)
    return jnp.pad(z, ((0, 0), (0, 128 - HY_EMB)))


def _filter_kernel(z_ref, w1_ref, b1_ref, w2_ref, b2_ref, w3_ref, b3_ref, dec_ref, f_ref,
                   a_ref, b_ref, d_ref, *, seq_len):
    n = 2 * seq_len
    z = z_ref[...]
    h = jnp.sin(_dot3(z, w1_ref[...]) + b1_ref[...])
    h = jnp.sin(_dot3(h, w2_ref[...]) + b2_ref[...])
    h = _dot3(h, w3_ref[...]) + b3_ref[...]
    t = z[:, 0:1]
    h = h * (jnp.exp(-t * jnp.abs(dec_ref[...])) + HY_MOD_SHIFT)
    ss = jnp.sum(h * h, axis=0, keepdims=True)
    inv = lax.rsqrt(ss[:, :HY_WIDTH] + ss[:, HY_WIDTH:] + EPS)
    hp = h[:, :HY_WIDTH] * inv
    hn = h[:, HY_WIDTH:] * inv
    ssum = hp + hn
    sdif = hp - hn
    hc = _dot(f_ref[0:seq_len, :], ssum.astype(BF16))
    hs = _dot(f_ref[seq_len:n, :], sdif.astype(BF16))
    di = lax.broadcasted_iota(jnp.int32, (seq_len, 1), 0)
    sgn = jnp.where(di % 2 == 0, 1.0, -1.0)
    nyq = jnp.sum(ssum * sgn, axis=0, keepdims=True)
    first = di == 0
    a_ref[0] = hc * jnp.where(first, 1.0 / n, 2.0 / n)
    b_ref[0] = jnp.where(first, 0.0, hs * (2.0 / n))
    d_ref[0] = jnp.where(first, nyq * (1.0 / n), hc * (2.0 / n))


def _hyena_filters(seq_len, f, w1p, b1, w2, b2, w3, b3, dec):
    z = _filter_feats(seq_len)
    hid = HY_FILTER_HIDDEN
    oc = 2 * HY_WIDTH
    full = lambda shape: pl.BlockSpec(shape, lambda o: tuple(0 for _ in shape))
    out = jax.ShapeDtypeStruct((HY_ORDER, seq_len, HY_WIDTH), F32)
    return pl.pallas_call(
        functools.partial(_filter_kernel, seq_len=seq_len),
        out_shape=(out, out, out),
        grid=(HY_ORDER,),
        in_specs=[full((seq_len, 128)), full((128, hid)), full((1, hid)), full((hid, hid)), full((1, hid)),
                  pl.BlockSpec((hid, oc), lambda o: (0, o)),
                  pl.BlockSpec((1, oc), lambda o: (0, o)),
                  pl.BlockSpec((1, oc), lambda o: (0, o)),
                  full((2 * seq_len, seq_len))],
        out_specs=tuple(pl.BlockSpec((1, seq_len, HY_WIDTH), lambda o: (o, 0, 0)) for _ in range(3)),
        compiler_params=_cparams(("arbitrary",)),
        name=f"hyena_filter_{seq_len}",
    )(z, w1p, b1, w2, b2, w3, b3, dec, f)


def _hyena_kernel(x1_ref, x2_ref, v_ref, cw1_ref, cw2_ref, cwv_ref, a_ref, b_ref, d_ref, bias_ref,
                  f_ref, ft_ref, z_ref, *, seq_len, width):
    ti = lax.broadcasted_iota(jnp.int32, (seq_len, 1), 0)
    has_prev = (ti % width) != 0
    has_next = (ti % width) != (width - 1)

    def short_conv(x_ref, w_ref):
        x = x_ref[...].astype(F32)
        prev = jnp.where(has_prev, pltpu.roll(x, 1, axis=0), 0.0)
        nxt = jnp.where(has_next, pltpu.roll(x, seq_len - 1, axis=0), 0.0)
        return w_ref[0:1, :] * prev + w_ref[1:2, :] * x + w_ref[2:3, :] * nxt

    gates = (short_conv(x1_ref, cw1_ref), short_conv(x2_ref, cw2_ref))
    z = short_conv(v_ref, cwv_ref)
    for o in range(HY_ORDER):
        u = _dot(f_ref[...], z.astype(BF16))
        ut = u[:seq_len]
        ub = u[seq_len:]
        a, b, dd = a_ref[o], b_ref[o], d_ref[o]
        yt = ut * a - ub * b
        yb = ut * b + ub * dd
        y = _dot(ft_ref[:, :seq_len], yt.astype(BF16)) + _dot(ft_ref[:, seq_len:], yb.astype(BF16))
        z = gates[o] * (y + bias_ref[o:o + 1, :] * z)
    z_ref[...] = z.astype(BF16)


def _hyena(proj, conv_w, coefs, hy_bias, f, ft, seq_len, n_seq, row_block_off, width):
    cb = 256
    nblk = HY_WIDTH // cb
    base = ML_QKVO_COLS // cb
    off = row_block_off
    a, b, d = coefs

    def col_spec(part):
        return pl.BlockSpec((seq_len, cb), lambda s, j: (off + s, base + part * nblk + j))

    def w_spec(part):
        return pl.BlockSpec((3, cb), lambda s, j: (0, part * nblk + j))

    coef_spec = pl.BlockSpec((HY_ORDER, seq_len, cb), lambda s, j: (0, 0, j))
    return pl.pallas_call(
        functools.partial(_hyena_kernel, seq_len=seq_len, width=width),
        out_shape=jax.ShapeDtypeStruct((n_seq * seq_len, HY_WIDTH), BF16),
        grid=(n_seq, nblk),
        in_specs=[col_spec(0), col_spec(1), col_spec(2), w_spec(0), w_spec(1), w_spec(2),
                  coef_spec, coef_spec, coef_spec,
                  pl.BlockSpec((HY_ORDER, cb), lambda s, j: (0, j)),
                  pl.BlockSpec((2 * seq_len, seq_len), lambda s, j: (0, 0)),
                  pl.BlockSpec((seq_len, 2 * seq_len), lambda s, j: (0, 0))],
        out_specs=pl.BlockSpec((seq_len, cb), lambda s, j: (s, j)),
        compiler_params=_cparams(("arbitrary", "arbitrary")),
        name=f"hyena_conv_{seq_len}",
    )(proj, proj, proj, conv_w, conv_w, conv_w, a, b, d, hy_bias, f, ft)


def _first_max(x, n):
    mx = jnp.max(x, axis=0, keepdims=True)
    row = lax.broadcasted_iota(jnp.int32, x.shape, 0).astype(F32)
    idx = jnp.min(jnp.where(x == mx, row, float(n)), axis=0, keepdims=True)
    return mx, idx.astype(jnp.int32)


ROUTER_ROWS = 32
PAIRS_PER_GROUP = 6
N_BUCKETS = N_GROUPS * PAIRS_PER_GROUP
PAIR_SLOTS = ((0, 1), (0, 2), (0, 3), (1, 3), (1, 2), (3, 2))
LANES = 128
FEAT_SLABS = D_MODEL // LANES
H2_SLABS = FEAT_SLABS + 1
ROW_TILE = 256
ROW_CAP = T_ALL + N_BUCKETS * ROW_TILE
N_ROW_TILES = ROW_CAP // ROW_TILE


def _outproj_kernel(xp_ref, xs_ref, yp_ref, ys_ref, zp_ref, zs_ref, m_ref, gn_ref, wo_ref, wr_ref, br_ref,
                    x1_ref, h2_ref, bid_ref):
    i = pl.program_id(0)
    is_p = i < N_TILES_P
    x = jnp.where(is_p, xp_ref[...], xs_ref[...])
    yml = jnp.where(is_p, yp_ref[...], ys_ref[...])
    zz = jnp.where(is_p, zp_ref[...], zs_ref[...])
    y = _dot(yml, wo_ref[0:ML_WIDTH, :]) + _dot(zz, wo_ref[ML_WIDTH:, :])
    x1 = x + m_ref[0, 2:3, :] * _rms(y, gn_ref[1:2, :])
    x1_ref[...] = x1
    h2 = _rms(x1, gn_ref[2:3, :]) * (1.0 + m_ref[0, 4:5, :]) + m_ref[0, 3:4, :]
    for s in range(FEAT_SLABS):
        h2_ref[:, s, :] = h2[:, s * LANES:(s + 1) * LANES]
    h2h, h2l = _split2(h2)
    wrh, wrl = _split2(wr_ref[...])
    logits = _dot_nt(wrh, h2h) + _dot_nt(wrh, h2l) + _dot_nt(wrl, h2h) + br_ref[...]
    lc = logits[0:N_GROUPS]
    mx, gi = _first_max(lc, N_GROUPS)
    p_grp = 1.0 / jnp.sum(jnp.exp(lc - mx), axis=0, keepdims=True)
    lsel = jnp.zeros((EXPERTS_PER_GROUP, TILE), F32)
    for g in range(N_GROUPS):
        lo = N_GROUPS + g * EXPERTS_PER_GROUP
        lsel = jnp.where(gi == g, logits[lo:lo + EXPERTS_PER_GROUP], lsel)
    l1, i1 = _first_max(lsel, EXPERTS_PER_GROUP)
    sub4 = lax.broadcasted_iota(jnp.int32, lsel.shape, 0)
    l2, i2 = _first_max(jnp.where(sub4 == i1, -jnp.inf, lsel), EXPERTS_PER_GROUP)
    e2 = jnp.exp(l2 - l1)
    w1 = p_grp / (1.0 + e2)
    w2 = p_grp * e2 / (1.0 + e2)
    lo_e = jnp.minimum(i1, i2)
    hi_e = jnp.maximum(i1, i2)
    pair = jnp.where(lo_e == 0, hi_e - 1, jnp.where(lo_e == 1, jnp.where(hi_e == 3, 3, 4), 5))
    slot_a = jnp.where(pair == 5, hi_e, lo_e)
    first_in_a = i1 == slot_a
    w_a = jnp.where(first_in_a, w1, w2)
    w_b = jnp.where(first_in_a, w2, w1)
    sub = lax.broadcasted_iota(jnp.int32, (LANES, TILE), 0)
    gate_rows = jnp.where(sub == 0, w_a, jnp.where(sub == 1, w_b, 0.0))
    h2_ref[:, FEAT_SLABS, :] = _rows_to_cols(gate_rows)
    bid_ref[0] = gi * PAIRS_PER_GROUP + pair


def _outproj(xp, xs, yp, ys, zp, zs, mods3, g_norm, w_out, w_r, b_r):
    tps = DEC_SEQ // TILE
    pidx = lambda i: (jnp.minimum(i, N_TILES_P - 1), 0)
    sidx = lambda i: (jnp.maximum(i - N_TILES_P, 0), 0)
    return pl.pallas_call(
        _outproj_kernel,
        out_shape=(jax.ShapeDtypeStruct((T_ALL, D_MODEL), F32),
                   jax.ShapeDtypeStruct((T_ALL, H2_SLABS, LANES), F32),
                   jax.ShapeDtypeStruct((N_TILES, 1, TILE), jnp.int32)),
        grid=(N_TILES,),
        in_specs=[pl.BlockSpec((TILE, D_MODEL), pidx), pl.BlockSpec((TILE, D_MODEL), sidx),
                  pl.BlockSpec((TILE, ML_WIDTH), pidx), pl.BlockSpec((TILE, ML_WIDTH), sidx),
                  pl.BlockSpec((TILE, HY_WIDTH), pidx), pl.BlockSpec((TILE, HY_WIDTH), sidx),
                  pl.BlockSpec((1, N_MOD, D_MODEL), lambda i: (_mod_row_of_tile(i, tps, N_TILES_P), 0, 0)),
                  pl.BlockSpec((4, D_MODEL), lambda i: (0, 0)),
                  pl.BlockSpec((D_MODEL, D_MODEL), lambda i: (0, 0)),
                  pl.BlockSpec((ROUTER_ROWS, D_MODEL), lambda i: (0, 0)),
                  pl.BlockSpec((ROUTER_ROWS, 1), lambda i: (0, 0))],
        out_specs=(pl.BlockSpec((TILE, D_MODEL), lambda i: (i, 0)),
                   pl.BlockSpec((TILE, H2_SLABS, LANES), lambda i: (i, 0, 0)),
                   pl.BlockSpec((1, 1, TILE), lambda i: (i, 0, 0))),
        compiler_params=_cparams(("arbitrary",)),
        name="out_proj_router",
    )(xp, xs, yp, ys, zp, zs, mods3, g_norm, w_out, w_r, b_r)


def _route_kernel(bid_ref, pos_ref, meta_ref):
    nb = 32
    tm = float(ROW_TILE)
    sub = lax.broadcasted_iota(jnp.int32, (nb, TILE), 0)
    ri = lax.broadcasted_iota(jnp.int32, (TILE, TILE), 0)
    ci = lax.broadcasted_iota(jnp.int32, (TILE, TILE), 1)
    before = jnp.where(ri < ci, 1.0, 0.0).astype(BF16)

    def onehot(blk):
        return jnp.where(sub == bid_ref[blk], 1.0, 0.0)

    zeros = jnp.zeros((nb, 1), F32)
    cnt = lax.fori_loop(0, N_TILES, lambda blk, c: c + jnp.sum(onehot(blk), axis=1, keepdims=True), zeros)
    padded = jnp.floor((cnt + (tm - 1.0)) * (1.0 / tm)) * tm
    r32 = lax.broadcasted_iota(jnp.int32, (nb, nb), 0)
    c32 = lax.broadcasted_iota(jnp.int32, (nb, nb), 1)
    padded_row = jnp.sum(jnp.where(r32 == c32, padded, 0.0), axis=0, keepdims=True)
    offs = jnp.sum(jnp.where(c32 < r32, padded_row, 0.0), axis=1, keepdims=True)
    ends = offs + padded

    def place(blk, seen):
        oh = onehot(blk)
        rank = _dot(oh.astype(BF16), before)
        pos = jnp.sum(oh * (rank + seen + offs), axis=0, keepdims=True)
        pos_ref[blk] = pos.astype(jnp.int32)
        return seen + jnp.sum(oh, axis=1, keepdims=True)

    lax.fori_loop(0, N_TILES, place, zeros)

    start = lax.broadcasted_iota(jnp.int32, (nb, 128), 1).astype(F32) * tm
    bsub = lax.broadcasted_iota(jnp.int32, (nb, 128), 0)
    done = jnp.where((bsub < N_BUCKETS) & (ends <= start), 1.0, 0.0)
    tb = jnp.sum(done, axis=0, keepdims=True)
    valid = jnp.where(tb < N_BUCKETS, 1.0, 0.0)
    tbc = jnp.minimum(tb, N_BUCKETS - 1.0)
    grp = jnp.floor((tbc + 0.5) * (1.0 / PAIRS_PER_GROUP))
    pair = tbc - PAIRS_PER_GROUP * grp
    loc_a = jnp.zeros_like(pair)
    loc_b = jnp.zeros_like(pair)
    for k, (sa, sb) in enumerate(PAIR_SLOTS):
        loc_a = jnp.where(pair == k, float(sa), loc_a)
        loc_b = jnp.where(pair == k, float(sb), loc_b)
    row8 = lax.broadcasted_iota(jnp.int32, (8, 128), 0)
    meta = jnp.where(row8 == 0, grp * EXPERTS_PER_GROUP + loc_a,
                     jnp.where(row8 == 1, grp * EXPERTS_PER_GROUP + loc_b,
                               jnp.where(row8 == 2, valid, 0.0)))
    meta_ref[...] = meta.astype(jnp.int32)


def _route(bid):
    return pl.pallas_call(
        _route_kernel,
        out_shape=(jax.ShapeDtypeStruct((N_TILES, 1, TILE), jnp.int32),
                   jax.ShapeDtypeStruct((8, 128), jnp.int32)),
        compiler_params=pltpu.CompilerParams(vmem_limit_bytes=VMEM_LIMIT),
        name="moe_route",
    )(bid)


def _moe_kernel(meta_ref, pos_ref, h2_hbm, wga_ref, wua_ref, wda_ref, wgb_ref, wub_ref, wdb_ref,
                y_ref, src_ref, xbuf, sem, wga_s, wua_s, wda_s, wgb_s, wub_s, wdb_s):
    j = pl.program_id(0)

    def row_copy(tile, r, slot):
        tok = src_ref[tile * ROW_TILE + r]
        return pltpu.make_async_copy(h2_hbm.at[tok], xbuf.at[slot, r], sem.at[slot])

    def issue(tile, slot, lo, hi):
        def body(r, c):
            row_copy(tile, r, slot).start()
            return c
        lax.fori_loop(lo, hi, body, 0, unroll=8)

    def wait(tile, slot):
        def body(r, c):
            row_copy(tile, r, slot).wait()
            return c
        lax.fori_loop(0, ROW_TILE, body, 0, unroll=8)

    @pl.when(j == 0)
    def _():
        def clear(p, c):
            src_ref[p] = 0
            return c
        lax.fori_loop(0, ROW_CAP, clear, 0, unroll=8)

        def invert(t, c):
            src_ref[pos_ref[t]] = t
            return c
        lax.fori_loop(0, T_ALL, invert, 0, unroll=8)

        @pl.when(meta_ref[2, 0] == 1)
        def _():
            issue(0, 0, 0, ROW_TILE)

    nxt = jnp.minimum(j + 1, N_ROW_TILES - 1)
    has_next = (j + 1 < N_ROW_TILES) & (meta_ref[2, nxt] == 1)
    part = ROW_TILE // 4

    def issue_next(k):
        @pl.when(has_next)
        def _():
            issue(nxt, nxt % 2, k * part, (k + 1) * part)

    valid = meta_ref[2, j] == 1
    prev = jnp.maximum(j - 1, 0)

    @pl.when(valid & ((j == 0) | (meta_ref[0, j] != meta_ref[0, prev])))
    def _():
        wga_s[...] = wga_ref[0].astype(BF16)
        wua_s[...] = wua_ref[0].astype(BF16)
        wda_s[...] = wda_ref[0].astype(BF16)

    @pl.when(valid & ((j == 0) | (meta_ref[1, j] != meta_ref[1, prev])))
    def _():
        wgb_s[...] = wgb_ref[0].astype(BF16)
        wub_s[...] = wub_ref[0].astype(BF16)
        wdb_s[...] = wdb_ref[0].astype(BF16)

    @pl.when(valid)
    def _():
        slot = j % 2
        wait(j, slot)
        x = jnp.concatenate([xbuf[slot, :, s, :] for s in range(FEAT_SLABS)], axis=1).astype(BF16)
        gates = xbuf[slot, :, FEAT_SLABS, :]

        def expert(wg, wu, gate):
            hg = _dot(x, wg[...])
            hu = _dot(x, wu[...])
            return (hg * jax.nn.sigmoid(hg) * hu * gate).astype(BF16)

        issue_next(0)
        act_a = expert(wga_s, wua_s, gates[:, 0:1])
        issue_next(1)
        act_b = expert(wgb_s, wub_s, gates[:, 1:2])
        issue_next(2)
        y = _dot(act_a, wda_s[...]) + _dot(act_b, wdb_s[...])
        issue_next(3)
        for s in range(FEAT_SLABS):
            y_ref[:, s, :] = y[:, s * LANES:(s + 1) * LANES]

    @pl.when(jnp.logical_not(valid))
    def _():
        y_ref[...] = jnp.zeros_like(y_ref)


def _moe(meta, pos, h2ext, w_gate, w_up, w_down):
    up_spec = lambda slot: pl.BlockSpec((1, D_MODEL, EXPERT_FF), lambda j, meta, pos: (meta[slot, j], 0, 0))
    down_spec = lambda slot: pl.BlockSpec((1, EXPERT_FF, D_MODEL), lambda j, meta, pos: (meta[slot, j], 0, 0))
    grid_spec = pltpu.PrefetchScalarGridSpec(
        num_scalar_prefetch=2,
        grid=(N_ROW_TILES,),
        in_specs=[pl.BlockSpec(memory_space=pl.ANY),
                  up_spec(0), up_spec(0), down_spec(0), up_spec(1), up_spec(1), down_spec(1)],
        out_specs=pl.BlockSpec((ROW_TILE, FEAT_SLABS, LANES), lambda j, meta, pos: (j, 0, 0)),
        scratch_shapes=[pltpu.SMEM((ROW_CAP,), jnp.int32),
                        pltpu.VMEM((2, ROW_TILE, H2_SLABS, LANES), F32),
                        pltpu.SemaphoreType.DMA((2,)),
                        pltpu.VMEM((D_MODEL, EXPERT_FF), BF16), pltpu.VMEM((D_MODEL, EXPERT_FF), BF16),
                        pltpu.VMEM((EXPERT_FF, D_MODEL), BF16),
                        pltpu.VMEM((D_MODEL, EXPERT_FF), BF16), pltpu.VMEM((D_MODEL, EXPERT_FF), BF16),
                        pltpu.VMEM((EXPERT_FF, D_MODEL), BF16)])
    return pl.pallas_call(
        _moe_kernel,
        out_shape=jax.ShapeDtypeStruct((ROW_CAP, FEAT_SLABS, LANES), F32),
        grid_spec=grid_spec,
        compiler_params=_cparams(("arbitrary",)),
        name="moe_experts",
    )(meta, pos, h2ext, w_gate, w_up, w_down, w_gate, w_up, w_down)


def _final_kernel(pos_ref, y_hbm, x1_ref, m_ref, gn_ref, op_ref, os_ref, ybuf, sem):
    i = pl.program_id(0)

    def row_copy(tile, r, slot):
        p = pos_ref[tile * TILE + r]
        return pltpu.make_async_copy(y_hbm.at[p], ybuf.at[slot, r], sem.at[slot])

    def issue(tile, slot):
        def body(r2, c):
            row_copy(tile, 2 * r2, slot).start(priority=0)
            row_copy(tile, 2 * r2 + 1, slot).start(priority=1)
            return c
        lax.fori_loop(0, TILE // 2, body, 0, unroll=4)

    def wait(tile, slot):
        def body(r, c):
            row_copy(tile, r, slot).wait()
            return c
        lax.fori_loop(0, TILE, body, 0, unroll=8)

    @pl.when(i == 0)
    def _():
        issue(0, 0)

    @pl.when(i + 1 < N_TILES)
    def _():
        issue(i + 1, (i + 1) % 2)

    slot = i % 2
    wait(i, slot)
    y = jnp.concatenate([ybuf[slot, :, s, :] for s in range(FEAT_SLABS)], axis=1)
    out = x1_ref[...] + m_ref[0, 5:6, :] * _rms(y, gn_ref[3:4, :])

    @pl.when(i < N_TILES_P)
    def _():
        op_ref[...] = out

    @pl.when(i >= N_TILES_P)
    def _():
        os_ref[...] = out


def _final(pos, y_sorted, x1, mods3, g_norm):
    tps = DEC_SEQ // TILE
    grid_spec = pltpu.PrefetchScalarGridSpec(
        num_scalar_prefetch=1,
        grid=(N_TILES,),
        in_specs=[pl.BlockSpec(memory_space=pl.ANY),
                  pl.BlockSpec((TILE, D_MODEL), lambda i, pos: (i, 0)),
                  pl.BlockSpec((1, N_MOD, D_MODEL), lambda i, pos: (_mod_row_of_tile(i, tps, N_TILES_P), 0, 0)),
                  pl.BlockSpec((4, D_MODEL), lambda i, pos: (0, 0))],
        out_specs=(pl.BlockSpec((TILE, D_MODEL), lambda i, pos: (jnp.minimum(i, N_TILES_P - 1), 0)),
                   pl.BlockSpec((TILE, D_MODEL), lambda i, pos: (jnp.maximum(i - N_TILES_P, 0), 0))),
        scratch_shapes=[pltpu.VMEM((2, TILE, FEAT_SLABS, LANES), F32), pltpu.SemaphoreType.DMA((2,))])
    return pl.pallas_call(
        _final_kernel,
        out_shape=(jax.ShapeDtypeStruct((T_PROMPT, D_MODEL), F32),
                   jax.ShapeDtypeStruct((T_SAMPLE, D_MODEL), F32)),
        grid_spec=grid_spec,
        compiler_params=_cparams(("arbitrary",)),
        name="moe_combine_final",
    )(pos, y_sorted, x1, mods3, g_norm)


def kernel(x_prompt, x_sample, state_C, state_n, state_m, c, c_ctx, w_ada, b_ada, g_norm, w_in, ml_gate_bias, ml_head_gain, hy_conv_w, hy_f_w1, hy_f_b1, hy_f_w2, hy_f_b2, hy_f_w3, hy_f_b3, hy_decay, hy_bias, w_out, w_rc, b_rc, w_rf, b_rf, w_gate, w_up, w_down):
    xp = x_prompt.reshape(T_PROMPT, D_MODEL)
    xs = x_sample.reshape(T_SAMPLE, D_MODEL)
    gn = g_norm[0]

    cv = jnp.concatenate([c_ctx[None, :], c, jnp.zeros((MOD_ROWS - 1 - DEC_BATCH, D_MODEL), F32)], axis=0)
    mods3 = _ada(cv, w_ada[0], b_ada[0]).reshape(MOD_ROWS, N_MOD, D_MODEL)

    w_in0 = w_in[0]
    w_main = jnp.concatenate([w_in0[:, :ML_QKVO_COLS], w_in0[:, ML_QKVO_COLS + ML_GATE_COLS:]],
                             axis=1).astype(BF16)
    wg = w_in0[:, ML_QKVO_COLS:ML_QKVO_COLS + ML_GATE_COLS]
    gbt = ml_gate_bias[0].reshape(ML_GATE_COLS, 1)
    proj, gates, gates_t = _inproj(xp, xs, mods3, gn, w_main, wg.T, gbt)

    gain = ml_head_gain[0].reshape(1, ML_WIDTH)
    y_ml_p, c_new, n_new, m_new = _mlstm(proj, gates, gates_t, gain, None, SEQ, BATCH, 0)
    state = (state_C[:, 0], state_n[:, 0], state_m[:, 0].reshape(DEC_BATCH, 2 * ML_HEADS, 1))
    y_ml_s, _, _, _ = _mlstm(proj, gates, gates_t, gain, state, DEC_SEQ, DEC_BATCH, T_PROMPT // DEC_SEQ)

    w1p = jnp.pad(hy_f_w1[0], ((0, 128 - HY_EMB), (0, 0)))
    b1 = hy_f_b1[0].reshape(1, -1)
    b2 = hy_f_b2[0].reshape(1, -1)
    b3 = hy_f_b3[0].reshape(1, -1)
    dec = hy_decay[0].reshape(1, -1)
    z_parts = []
    for seq_len, n_seq, off, width in ((SEQ, BATCH, 0, SEQ), (DEC_SEQ, DEC_BATCH, T_PROMPT // DEC_SEQ, GRID_W)):
        f, ft = _dft_mats(seq_len)
        coefs = _hyena_filters(seq_len, f, w1p, b1, hy_f_w2[0], b2, hy_f_w3[0], b3, dec)
        z_parts.append(_hyena(proj, hy_conv_w[0], coefs, hy_bias[0], f, ft, seq_len, n_seq, off, width))
    z_p, z_s = z_parts

    pad_r = ROUTER_ROWS - N_GROUPS - N_EXPERTS
    w_r = jnp.pad(jnp.concatenate([w_rc[0], w_rf[0]], axis=1).T, ((0, pad_r), (0, 0)))
    b_r = jnp.pad(jnp.concatenate([b_rc[0], b_rf[0]], axis=0), (0, pad_r)).reshape(ROUTER_ROWS, 1)
    x1, h2ext, bid = _outproj(xp, xs, y_ml_p, y_ml_s, z_p, z_s, mods3, gn, w_out[0].astype(BF16), w_r, b_r)

    pos3, meta = _route(bid)
    pos = pos3.reshape(T_ALL)
    y_sorted = _moe(meta, pos, h2ext, w_gate[0], w_up[0], w_down[0])
    y_p, y_s = _final(pos, y_sorted, x1, mods3, gn)

    new_c = c_new.reshape(BATCH, 1, 2, ML_HEADS, ML_HEAD_DIM, ML_HEAD_DIM)
    new_n = n_new.reshape(BATCH, 1, 2, ML_HEADS, ML_HEAD_DIM)
    new_m = m_new[:, :, 0].reshape(BATCH, 1, 2, ML_HEADS)
    return (y_p.reshape(BATCH, SEQ, D_MODEL), y_s.reshape(DEC_BATCH, DEC_SEQ, D_MODEL), new_c, new_n, new_m)
```

```python
import functools
import math

import jax
import jax.numpy as jnp
from jax import lax
from jax.experimental import pallas as pl
from jax.experimental.pallas import tpu as pltpu

F32 = jnp.float32
BF16 = jnp.bfloat16

D_MODEL = 1024
BATCH = 16
SEQ = 256
DEC_BATCH = 4
DEC_SEQ = 1024
GRID_W = 64
ML_WIDTH = 512
ML_HEADS = 4
ML_HEAD_DIM = 128
HY_WIDTH = 512
HY_ORDER = 2
HY_EMB = 33
HY_BANDS = 16
HY_FILTER_HIDDEN = 64
HY_MOD_SHIFT = 0.05
N_GROUPS = 4
EXPERTS_PER_GROUP = 4
N_EXPERTS = 16
EXPERT_FF = 512
N_MOD = 6
EPS = 1e-6
ML_QKVO_COLS = 4 * ML_WIDTH
ML_GATE_COLS = 4 * ML_HEADS
HY_COLS = 3 * HY_WIDTH
MAIN_COLS = ML_QKVO_COLS + HY_COLS

T_PROMPT = BATCH * SEQ
T_SAMPLE = DEC_BATCH * DEC_SEQ
T_ALL = T_PROMPT + T_SAMPLE
TILE = 256
N_TILES_P = T_PROMPT // TILE
N_TILES = T_ALL // TILE
MOD_ROWS = 8
K_SCALE = ML_HEAD_DIM ** -0.5
VMEM_LIMIT = 56 * 1024 * 1024


def _cparams(sem):
    return pltpu.CompilerParams(dimension_semantics=sem, vmem_limit_bytes=VMEM_LIMIT)


def _split2(x):
    hi = x.astype(BF16)
    lo = (x - hi.astype(F32)).astype(BF16)
    return hi, lo


def _dot(a, b):
    return jnp.dot(a, b, preferred_element_type=F32)


def _dot_nt(a, b):
    return lax.dot_general(a, b, (((1,), (1,)), ((), ())), preferred_element_type=F32)


def _dot_tn(a, b):
    return lax.dot_general(a, b, (((0,), (0,)), ((), ())), preferred_element_type=F32)


def _dot3(a, b):
    ah, al = _split2(a)
    bh, bl = _split2(b)
    return _dot(ah, bh) + _dot(al, bh) + _dot(ah, bl)


def _dot3_nt(a, b):
    ah, al = _split2(a)
    bh, bl = _split2(b)
    return _dot_nt(ah, bh) + _dot_nt(al, bh) + _dot_nt(ah, bl)


def _dot_exact_lhs(t, x):
    x1 = x.astype(BF16)
    r1 = x - x1.astype(F32)
    x2 = r1.astype(BF16)
    x3 = (r1 - x2.astype(F32)).astype(BF16)
    return _dot(t, x1) + _dot(t, x2) + _dot(t, x3)


def _dot_exact_rhs(x, t):
    x1 = x.astype(BF16)
    r1 = x - x1.astype(F32)
    x2 = r1.astype(BF16)
    x3 = (r1 - x2.astype(F32)).astype(BF16)
    return _dot(x1, t) + _dot(x2, t) + _dot(x3, t)


def _rms(x, g):
    return x * lax.rsqrt(jnp.mean(x * x, axis=-1, keepdims=True) + EPS) * g


def _mod_row_of_tile(i, tiles_per_sample_seq, n_prompt_tiles):
    return jnp.where(i < n_prompt_tiles, 0, 1 + (i - n_prompt_tiles) // tiles_per_sample_seq)


def _ada_kernel(cv_ref, w_ref, b_ref, o_ref):
    cv = cv_ref[...]
    s = cv * jax.nn.sigmoid(cv)
    o_ref[...] = _dot3(s, w_ref[...]) + b_ref[...]


def _ada(cv, w_ada, b_ada):
    n = N_MOD * D_MODEL
    return pl.pallas_call(
        _ada_kernel,
        out_shape=jax.ShapeDtypeStruct((MOD_ROWS, n), F32),
        grid=(N_MOD,),
        in_specs=[pl.BlockSpec((MOD_ROWS, D_MODEL), lambda j: (0, 0)),
                  pl.BlockSpec((D_MODEL, D_MODEL), lambda j: (0, j)),
                  pl.BlockSpec((1, D_MODEL), lambda j: (0, j))],
        out_specs=pl.BlockSpec((MOD_ROWS, D_MODEL), lambda j: (0, j)),
        compiler_params=_cparams(("arbitrary",)),
        name="ada_mod",
    )(cv, w_ada, b_ada.reshape(1, n))


def _log_sigmoid(x):
    return jnp.minimum(x, 0.0) - jnp.log1p(jnp.exp(-jnp.abs(x)))


def _rows_to_cols(rows):
    ri = lax.broadcasted_iota(jnp.int32, (TILE, TILE), 0)
    ci = lax.broadcasted_iota(jnp.int32, (TILE, TILE), 1)
    eye = jnp.where(ri == ci, 1.0, 0.0).astype(BF16)
    p1 = rows.astype(BF16)
    r1 = rows - p1.astype(F32)
    p2 = r1.astype(BF16)
    p3 = (r1 - p2.astype(F32)).astype(BF16)
    return _dot_nt(eye, p1) + _dot_nt(eye, p2) + _dot_nt(eye, p3)


def _inproj_kernel(xp_ref, xs_ref, m_ref, gn_ref, w_ref, wgt_ref, gbt_ref, proj_ref, gate_ref, gatet_ref):
    i = pl.program_id(0)
    x = jnp.where(i < N_TILES_P, xp_ref[...], xs_ref[...])
    h = _rms(x, gn_ref[0:1, :]) * (1.0 + m_ref[0, 1:2, :]) + m_ref[0, 0:1, :]
    hb = h.astype(BF16)
    cb = 512
    for j in range(MAIN_COLS // cb):
        proj_ref[:, j * cb:(j + 1) * cb] = _dot(hb, w_ref[:, j * cb:(j + 1) * cb]).astype(BF16)
    hl = (h - hb.astype(F32)).astype(BF16)
    wth, wtl = _split2(wgt_ref[...])
    gt = _dot_nt(wth, hb) + _dot_nt(wth, hl) + _dot_nt(wtl, hb) + gbt_ref[...]
    row = lax.broadcasted_iota(jnp.int32, gt.shape, 0)
    gt = jnp.where((row % 8) >= 4, _log_sigmoid(gt), gt)
    gatet_ref[0] = gt
    gate_ref[...] = _rows_to_cols(gt)


def _inproj(xp, xs, mods3, g_norm, w_main, wgt, gbt):
    tps = DEC_SEQ // TILE
    return pl.pallas_call(
        _inproj_kernel,
        out_shape=(jax.ShapeDtypeStruct((T_ALL, MAIN_COLS), BF16),
                   jax.ShapeDtypeStruct((T_ALL, ML_GATE_COLS), F32),
                   jax.ShapeDtypeStruct((N_TILES, ML_GATE_COLS, TILE), F32)),
        grid=(N_TILES,),
        in_specs=[pl.BlockSpec((TILE, D_MODEL), lambda i: (jnp.minimum(i, N_TILES_P - 1), 0)),
                  pl.BlockSpec((TILE, D_MODEL), lambda i: (jnp.maximum(i - N_TILES_P, 0), 0)),
                  pl.BlockSpec((1, N_MOD, D_MODEL), lambda i: (_mod_row_of_tile(i, tps, N_TILES_P), 0, 0)),
                  pl.BlockSpec((4, D_MODEL), lambda i: (0, 0)),
                  pl.BlockSpec((D_MODEL, MAIN_COLS), lambda i: (0, 0)),
                  pl.BlockSpec((ML_GATE_COLS, D_MODEL), lambda i: (0, 0)),
                  pl.BlockSpec((ML_GATE_COLS, 1), lambda i: (0, 0))],
        out_specs=(pl.BlockSpec((TILE, MAIN_COLS), lambda i: (i, 0)),
                   pl.BlockSpec((TILE, ML_GATE_COLS), lambda i: (i, 0)),
                   pl.BlockSpec((1, ML_GATE_COLS, TILE), lambda i: (i, 0, 0))),
        compiler_params=_cparams(("arbitrary",)),
        name="in_proj",
    )(xp, xs, mods3, g_norm, w_main, wgt, gbt)


def _mlstm_kernel(*refs, seq_len, has_state):
    if has_state:
        (q_ref, k_ref, v_ref, o_ref, g_ref, gt_ref, gain_ref, c0_ref, n0_ref, m0_ref,
         y_ref, c_ref, n_ref, m_ref, hf_ref, hb_ref, cs_ref, ns_ref, ms_ref) = refs
    else:
        (q_ref, k_ref, v_ref, o_ref, g_ref, gt_ref, gain_ref,
         y_ref, c_ref, n_ref, m_ref, hf_ref, hb_ref, cs_ref, ns_ref, ms_ref) = refs
    ch = TILE
    nc = seq_len // ch
    hd = ML_HEAD_DIM
    ri = lax.broadcasted_iota(jnp.int32, (ch, ch), 0)
    ci = lax.broadcasted_iota(jnp.int32, (ch, ch), 1)
    lower = ci <= ri
    upper = ci >= ri
    t_low = jnp.where(lower, 1.0, 0.0).astype(BF16)
    t_up = jnp.where(upper, 1.0, 0.0).astype(BF16)

    for d in range(2):
        for h in range(ML_HEADS):
            r = d * ML_HEADS + h
            if has_state:
                cs_ref[r] = c0_ref[0, d, h]
                ns_ref[r] = n0_ref[0, d, h:h + 1, :]
                ms_ref[r] = jnp.broadcast_to(m0_ref[0, r:r + 1, :], (1, hd))
            else:
                cs_ref[r] = jnp.zeros((hd, hd), F32)
                ns_ref[r] = jnp.zeros((1, hd), F32)
                ms_ref[r] = jnp.zeros((1, hd), F32)

    def step(t, carry):
        for d in range(2):
            c = t if d == 0 else nc - 1 - t
            r0 = pl.multiple_of(c * ch, ch)
            rows = pl.ds(r0, ch)
            gcol = g_ref[rows, :]
            grow = gt_ref[c]
            tmat_c = t_low if d == 0 else t_up
            tmat_r = t_up if d == 0 else t_low
            bcol_all = _dot_exact_lhs(tmat_c, gcol)
            brow_all = _dot_exact_rhs(grow, tmat_r)
            mask = lower if d == 0 else upper
            hacc_ref = hf_ref if d == 0 else hb_ref
            for h in range(ML_HEADS):
                r = d * ML_HEADS + h
                fcol = (1 + 2 * d) * ML_HEADS + h
                icol = (2 * d) * ML_HEADS + h
                cols = slice(h * hd, (h + 1) * hd)
                q = q_ref[rows, cols]
                k = k_ref[rows, cols]
                v = v_ref[rows, cols]
                c_prev = cs_ref[r]
                n_prev = ns_ref[r]
                m_prev = ms_ref[r][:, 0:1]
                b_col = bcol_all[:, fcol:fcol + 1]
                b_row = brow_all[fcol:fcol + 1, :]
                ig_row = grow[icol:icol + 1, :]
                ig_col = gcol[:, icol:icol + 1]
                logd = jnp.where(mask, b_col - b_row + ig_row, -jnp.inf)
                inter = b_col + m_prev
                m_pos = jnp.maximum(inter, jnp.max(logd, axis=-1, keepdims=True))
                s = _dot_nt(q, k) * K_SCALE * jnp.exp(logd - m_pos)
                sc_inter = jnp.exp(inter - m_pos)
                qf = q.astype(F32)
                num = sc_inter * _dot(q, c_prev.astype(BF16)) + _dot(s.astype(BF16), v)
                den = (sc_inter * jnp.sum(qf * n_prev, axis=-1, keepdims=True)
                       + jnp.sum(s, axis=-1, keepdims=True))
                hh = num / jnp.maximum(jnp.abs(den), jnp.exp(-m_pos))
                hacc_ref[rows, cols] = hh
                b_last = b_col[ch - 1:ch, :] if d == 0 else b_col[0:1, :]
                logw = b_last - b_col + ig_col
                m_new = jnp.maximum(b_last + m_prev, jnp.max(logw, axis=0, keepdims=True))
                w = jnp.exp(logw - m_new)
                decay = jnp.exp(b_last + m_prev - m_new)
                kw = k.astype(F32) * (w * K_SCALE)
                cs_ref[r] = decay * c_prev + _dot_tn(kw.astype(BF16), v)
                ns_ref[r] = decay * n_prev + jnp.sum(kw, axis=0, keepdims=True)
                ms_ref[r] = jnp.broadcast_to(m_new, (1, hd))
        return carry

    lax.fori_loop(0, nc, step, 0)

    for d in range(2):
        for h in range(ML_HEADS):
            r = d * ML_HEADS + h
            c_ref[0, d, h] = cs_ref[r]
            n_ref[0, d, h:h + 1, :] = ns_ref[r]
            m_ref[0, r:r + 1, :] = ms_ref[r]
    for h in range(ML_HEADS):
        cols = slice(h * hd, (h + 1) * hd)
        hh = hf_ref[:, cols] + hb_ref[:, cols]
        hh = hh * lax.rsqrt(jnp.mean(hh * hh, axis=-1, keepdims=True) + EPS)
        y = hh * gain_ref[:, cols] * jax.nn.sigmoid(o_ref[:, cols].astype(F32))
        y_ref[:, cols] = y.astype(BF16)


def _mlstm(proj, gates, gates_t, gain, state, seq_len, n_seq, row_block_off):
    has_state = state is not None
    tiles = seq_len // TILE
    off = row_block_off
    qkvo_specs = [pl.BlockSpec((seq_len, ML_WIDTH), functools.partial(lambda b, j: (off + b, j), j=j))
                  for j in range(4)]
    in_specs = qkvo_specs + [
        pl.BlockSpec((seq_len, ML_GATE_COLS), lambda b: (off + b, 0)),
        pl.BlockSpec((tiles, ML_GATE_COLS, TILE), lambda b: (off + b, 0, 0)),
        pl.BlockSpec((1, ML_WIDTH), lambda b: (0, 0)),
    ]
    args = [proj, proj, proj, proj, gates, gates_t, gain]
    if has_state:
        c0, n0, m0 = state
        in_specs += [
            pl.BlockSpec((1, 2, ML_HEADS, ML_HEAD_DIM, ML_HEAD_DIM), lambda b: (b, 0, 0, 0, 0)),
            pl.BlockSpec((1, 2, ML_HEADS, ML_HEAD_DIM), lambda b: (b, 0, 0, 0)),
            pl.BlockSpec((1, 2 * ML_HEADS, 1), lambda b: (b, 0, 0)),
        ]
        args += [c0, n0, m0]
    out_shape = (jax.ShapeDtypeStruct((n_seq * seq_len, ML_WIDTH), BF16),
                 jax.ShapeDtypeStruct((n_seq, 2, ML_HEADS, ML_HEAD_DIM, ML_HEAD_DIM), F32),
                 jax.ShapeDtypeStruct((n_seq, 2, ML_HEADS, ML_HEAD_DIM), F32),
                 jax.ShapeDtypeStruct((n_seq, 2 * ML_HEADS, ML_HEAD_DIM), F32))
    out_specs = (pl.BlockSpec((seq_len, ML_WIDTH), lambda b: (b, 0)),
                 pl.BlockSpec((1, 2, ML_HEADS, ML_HEAD_DIM, ML_HEAD_DIM), lambda b: (b, 0, 0, 0, 0)),
                 pl.BlockSpec((1, 2, ML_HEADS, ML_HEAD_DIM), lambda b: (b, 0, 0, 0)),
                 pl.BlockSpec((1, 2 * ML_HEADS, ML_HEAD_DIM), lambda b: (b, 0, 0)))
    scratch = [pltpu.VMEM((seq_len, ML_WIDTH), F32), pltpu.VMEM((seq_len, ML_WIDTH), F32),
               pltpu.VMEM((2 * ML_HEADS, ML_HEAD_DIM, ML_HEAD_DIM), F32),
               pltpu.VMEM((2 * ML_HEADS, 1, ML_HEAD_DIM), F32),
               pltpu.VMEM((2 * ML_HEADS, 1, ML_HEAD_DIM), F32)]
    return pl.pallas_call(
        functools.partial(_mlstm_kernel, seq_len=seq_len, has_state=has_state),
        out_shape=out_shape, grid=(n_seq,), in_specs=in_specs, out_specs=out_specs,
        scratch_shapes=scratch, compiler_params=_cparams(("arbitrary",)),
        name=f"mlstm_{seq_len}",
    )(*args)


def _dft_mats(seq_len):
    lo = 32
    hi = seq_len // lo
    d = jnp.arange(seq_len, dtype=jnp.int32)
    a = jnp.arange(hi, dtype=jnp.int32) * lo
    b = jnp.arange(lo, dtype=jnp.int32)
    scale = math.pi / seq_len
    ang_a = ((a[:, None] * d[None, :]) % (2 * seq_len)).astype(F32) * scale
    ang_b = ((b[:, None] * d[None, :]) % (2 * seq_len)).astype(F32) * scale
    ca, sa, cb, sb = jnp.cos(ang_a), jnp.sin(ang_a), jnp.cos(ang_b), jnp.sin(ang_b)
    cosm = (ca[:, None, :] * cb[None, :, :] - sa[:, None, :] * sb[None, :, :]).reshape(seq_len, seq_len)
    sinm = (sa[:, None, :] * cb[None, :, :] + ca[:, None, :] * sb[None, :, :]).reshape(seq_len, seq_len)
    nyq = jnp.where(d % 2 == 0, 1.0, -1.0).astype(F32)
    krow = jnp.arange(seq_len, dtype=jnp.int32)[:, None]
    sinm = jnp.where(krow == 0, nyq[None, :], sinm)
    f = jnp.concatenate([cosm, sinm], axis=0).astype(BF16)
    cos_t = (ca.T[:, :, None] * cb.T[:, None, :] - sa.T[:, :, None] * sb.T[:, None, :]).reshape(seq_len, seq_len)
    sin_t = (sa.T[:, :, None] * cb.T[:, None, :] + ca.T[:, :, None] * sb.T[:, None, :]).reshape(seq_len, seq_len)
    kcol = jnp.arange(seq_len, dtype=jnp.int32)[None, :]
    sin_t = jnp.where(kcol == 0, nyq[:, None], sin_t)
    ft = jnp.concatenate([cos_t, sin_t], axis=1).astype(BF16)
    return f, ft


def _filter_feats(seq_len):
    t = jnp.linspace(0.0, 1.0, seq_len, dtype=F32)[:, None]
    wpos = 2.0 * math.pi * jnp.arange(seq_len, dtype=F32)[:, None] / seq_len
    bands = jnp.linspace(1e-4, HY_BANDS - 1, HY_BANDS, dtype=F32)[None, :]
    z = jnp.concatenate([t, jnp.cos(bands * wpos), -jnp.sin(bands * wpos)], axis=-1)
    return jnp.pad(z, ((0, 0), (0, 128 - HY_EMB)))


def _filter_kernel(z_ref, w1_ref, b1_ref, w2_ref, b2_ref, w3_ref, b3_ref, dec_ref, f_ref,
                   a_ref, b_ref, d_ref, *, seq_len):
    n = 2 * seq_len
    z = z_ref[...]
    h = jnp.sin(_dot3(z, w1_ref[...]) + b1_ref[...])
    h = jnp.sin(_dot3(h, w2_ref[...]) + b2_ref[...])
    h = _dot3(h, w3_ref[...]) + b3_ref[...]
    t = z[:, 0:1]
    h = h * (jnp.exp(-t * jnp.abs(dec_ref[...])) + HY_MOD_SHIFT)
    ss = jnp.sum(h * h, axis=0, keepdims=True)
    inv = lax.rsqrt(ss[:, :HY_WIDTH] + ss[:, HY_WIDTH:] + EPS)
    hp = h[:, :HY_WIDTH] * inv
    hn = h[:, HY_WIDTH:] * inv
    ssum = hp + hn
    sdif = hp - hn
    hc = _dot(f_ref[0:seq_len, :], ssum.astype(BF16))
    hs = _dot(f_ref[seq_len:n, :], sdif.astype(BF16))
    di = lax.broadcasted_iota(jnp.int32, (seq_len, 1), 0)
    sgn = jnp.where(di % 2 == 0, 1.0, -1.0)
    nyq = jnp.sum(ssum * sgn, axis=0, keepdims=True)
    first = di == 0
    a_ref[0] = hc * jnp.where(first, 1.0 / n, 2.0 / n)
    b_ref[0] = jnp.where(first, 0.0, hs * (2.0 / n))
    d_ref[0] = jnp.where(first, nyq * (1.0 / n), hc * (2.0 / n))


def _hyena_filters(seq_len, f, w1p, b1, w2, b2, w3, b3, dec):
    z = _filter_feats(seq_len)
    hid = HY_FILTER_HIDDEN
    oc = 2 * HY_WIDTH
    full = lambda shape: pl.BlockSpec(shape, lambda o: tuple(0 for _ in shape))
    out = jax.ShapeDtypeStruct((HY_ORDER, seq_len, HY_WIDTH), F32)
    return pl.pallas_call(
        functools.partial(_filter_kernel, seq_len=seq_len),
        out_shape=(out, out, out),
        grid=(HY_ORDER,),
        in_specs=[full((seq_len, 128)), full((128, hid)), full((1, hid)), full((hid, hid)), full((1, hid)),
                  pl.BlockSpec((hid, oc), lambda o: (0, o)),
                  pl.BlockSpec((1, oc), lambda o: (0, o)),
                  pl.BlockSpec((1, oc), lambda o: (0, o)),
                  full((2 * seq_len, seq_len))],
        out_specs=tuple(pl.BlockSpec((1, seq_len, HY_WIDTH), lambda o: (o, 0, 0)) for _ in range(3)),
        compiler_params=_cparams(("arbitrary",)),
        name=f"hyena_filter_{seq_len}",
    )(z, w1p, b1, w2, b2, w3, b3, dec, f)


def _hyena_kernel(x1_ref, x2_ref, v_ref, cw1_ref, cw2_ref, cwv_ref, a_ref, b_ref, d_ref, bias_ref,
                  f_ref, ft_ref, z_ref, *, seq_len, width):
    ti = lax.broadcasted_iota(jnp.int32, (seq_len, 1), 0)
    has_prev = (ti % width) != 0
    has_next = (ti % width) != (width - 1)

    def short_conv(x_ref, w_ref):
        x = x_ref[...].astype(F32)
        prev = jnp.where(has_prev, pltpu.roll(x, 1, axis=0), 0.0)
        nxt = jnp.where(has_next, pltpu.roll(x, seq_len - 1, axis=0), 0.0)
        return w_ref[0:1, :] * prev + w_ref[1:2, :] * x + w_ref[2:3, :] * nxt

    gates = (short_conv(x1_ref, cw1_ref), short_conv(x2_ref, cw2_ref))
    z = short_conv(v_ref, cwv_ref)
    for o in range(HY_ORDER):
        u = _dot(f_ref[...], z.astype(BF16))
        ut = u[:seq_len]
        ub = u[seq_len:]
        a, b, dd = a_ref[o], b_ref[o], d_ref[o]
        yt = ut * a - ub * b
        yb = ut * b + ub * dd
        y = _dot(ft_ref[:, :seq_len], yt.astype(BF16)) + _dot(ft_ref[:, seq_len:], yb.astype(BF16))
        z = gates[o] * (y + bias_ref[o:o + 1, :] * z)
    z_ref[...] = z.astype(BF16)


def _hyena(proj, conv_w, coefs, hy_bias, f, ft, seq_len, n_seq, row_block_off, width):
    cb = 256
    nblk = HY_WIDTH // cb
    base = ML_QKVO_COLS // cb
    off = row_block_off
    a, b, d = coefs

    def col_spec(part):
        return pl.BlockSpec((seq_len, cb), lambda s, j: (off + s, base + part * nblk + j))

    def w_spec(part):
        return pl.BlockSpec((3, cb), lambda s, j: (0, part * nblk + j))

    coef_spec = pl.BlockSpec((HY_ORDER, seq_len, cb), lambda s, j: (0, 0, j))
    return pl.pallas_call(
        functools.partial(_hyena_kernel, seq_len=seq_len, width=width),
        out_shape=jax.ShapeDtypeStruct((n_seq * seq_len, HY_WIDTH), BF16),
        grid=(n_seq, nblk),
        in_specs=[col_spec(0), col_spec(1), col_spec(2), w_spec(0), w_spec(1), w_spec(2),
                  coef_spec, coef_spec, coef_spec,
                  pl.BlockSpec((HY_ORDER, cb), lambda s, j: (0, j)),
                  pl.BlockSpec((2 * seq_len, seq_len), lambda s, j: (0, 0)),
                  pl.BlockSpec((seq_len, 2 * seq_len), lambda s, j: (0, 0))],
        out_specs=pl.BlockSpec((seq_len, cb), lambda s, j: (s, j)),
        compiler_params=_cparams(("arbitrary", "arbitrary")),
        name=f"hyena_conv_{seq_len}",
    )(proj, proj, proj, conv_w, conv_w, conv_w, a, b, d, hy_bias, f, ft)


def _first_max(x, n):
    mx = jnp.max(x, axis=0, keepdims=True)
    row = lax.broadcasted_iota(jnp.int32, x.shape, 0).astype(F32)
    idx = jnp.min(jnp.where(x == mx, row, float(n)), axis=0, keepdims=True)
    return mx, idx.astype(jnp.int32)


ROUTER_ROWS = 32
PAIRS_PER_GROUP = 6
N_BUCKETS = N_GROUPS * PAIRS_PER_GROUP
PAIR_SLOTS = ((0, 1), (0, 2), (0, 3), (1, 3), (1, 2), (3, 2))
LANES = 128
FEAT_SLABS = D_MODEL // LANES
H2_SLABS = FEAT_SLABS + 1
ROW_TILE = 256
ROW_CAP = T_ALL + N_BUCKETS * ROW_TILE
N_ROW_TILES = ROW_CAP // ROW_TILE


def _outproj_kernel(xp_ref, xs_ref, yp_ref, ys_ref, zp_ref, zs_ref, m_ref, gn_ref, wo_ref, wr_ref, br_ref,
                    x1_ref, h2_ref, bid_ref):
    i = pl.program_id(0)
    is_p = i < N_TILES_P
    x = jnp.where(is_p, xp_ref[...], xs_ref[...])
    yml = jnp.where(is_p, yp_ref[...], ys_ref[...])
    zz = jnp.where(is_p, zp_ref[...], zs_ref[...])
    y = _dot(yml, wo_ref[0:ML_WIDTH, :]) + _dot(zz, wo_ref[ML_WIDTH:, :])
    x1 = x + m_ref[0, 2:3, :] * _rms(y, gn_ref[1:2, :])
    x1_ref[...] = x1
    h2 = _rms(x1, gn_ref[2:3, :]) * (1.0 + m_ref[0, 4:5, :]) + m_ref[0, 3:4, :]
    for s in range(FEAT_SLABS):
        h2_ref[:, s, :] = h2[:, s * LANES:(s + 1) * LANES]
    h2h, h2l = _split2(h2)
    wrh, wrl = _split2(wr_ref[...])
    logits = _dot_nt(wrh, h2h) + _dot_nt(wrh, h2l) + _dot_nt(wrl, h2h) + br_ref[...]
    lc = logits[0:N_GROUPS]
    mx, gi = _first_max(lc, N_GROUPS)
    p_grp = 1.0 / jnp.sum(jnp.exp(lc - mx), axis=0, keepdims=True)
    lsel = jnp.zeros((EXPERTS_PER_GROUP, TILE), F32)
    for g in range(N_GROUPS):
        lo = N_GROUPS + g * EXPERTS_PER_GROUP
        lsel = jnp.where(gi == g, logits[lo:lo + EXPERTS_PER_GROUP], lsel)
    l1, i1 = _first_max(lsel, EXPERTS_PER_GROUP)
    sub4 = lax.broadcasted_iota(jnp.int32, lsel.shape, 0)
    l2, i2 = _first_max(jnp.where(sub4 == i1, -jnp.inf, lsel), EXPERTS_PER_GROUP)
    e2 = jnp.exp(l2 - l1)
    w1 = p_grp / (1.0 + e2)
    w2 = p_grp * e2 / (1.0 + e2)
    lo_e = jnp.minimum(i1, i2)
    hi_e = jnp.maximum(i1, i2)
    pair = jnp.where(lo_e == 0, hi_e - 1, jnp.where(lo_e == 1, jnp.where(hi_e == 3, 3, 4), 5))
    slot_a = jnp.where(pair == 5, hi_e, lo_e)
    first_in_a = i1 == slot_a
    w_a = jnp.where(first_in_a, w1, w2)
    w_b = jnp.where(first_in_a, w2, w1)
    sub = lax.broadcasted_iota(jnp.int32, (LANES, TILE), 0)
    gate_rows = jnp.where(sub == 0, w_a, jnp.where(sub == 1, w_b, 0.0))
    h2_ref[:, FEAT_SLABS, :] = _rows_to_cols(gate_rows)
    bid_ref[0] = gi * PAIRS_PER_GROUP + pair


def _outproj(xp, xs, yp, ys, zp, zs, mods3, g_norm, w_out, w_r, b_r):
    tps = DEC_SEQ // TILE
    pidx = lambda i: (jnp.minimum(i, N_TILES_P - 1), 0)
    sidx = lambda i: (jnp.maximum(i - N_TILES_P, 0), 0)
    return pl.pallas_call(
        _outproj_kernel,
        out_shape=(jax.ShapeDtypeStruct((T_ALL, D_MODEL), F32),
                   jax.ShapeDtypeStruct((T_ALL, H2_SLABS, LANES), F32),
                   jax.ShapeDtypeStruct((N_TILES, 1, TILE), jnp.int32)),
        grid=(N_TILES,),
        in_specs=[pl.BlockSpec((TILE, D_MODEL), pidx), pl.BlockSpec((TILE, D_MODEL), sidx),
                  pl.BlockSpec((TILE, ML_WIDTH), pidx), pl.BlockSpec((TILE, ML_WIDTH), sidx),
                  pl.BlockSpec((TILE, HY_WIDTH), pidx), pl.BlockSpec((TILE, HY_WIDTH), sidx),
                  pl.BlockSpec((1, N_MOD, D_MODEL), lambda i: (_mod_row_of_tile(i, tps, N_TILES_P), 0, 0)),
                  pl.BlockSpec((4, D_MODEL), lambda i: (0, 0)),
                  pl.BlockSpec((D_MODEL, D_MODEL), lambda i: (0, 0)),
                  pl.BlockSpec((ROUTER_ROWS, D_MODEL), lambda i: (0, 0)),
                  pl.BlockSpec((ROUTER_ROWS, 1), lambda i: (0, 0))],
        out_specs=(pl.BlockSpec((TILE, D_MODEL), lambda i: (i, 0)),
                   pl.BlockSpec((TILE, H2_SLABS, LANES), lambda i: (i, 0, 0)),
                   pl.BlockSpec((1, 1, TILE), lambda i: (i, 0, 0))),
        compiler_params=_cparams(("arbitrary",)),
        name="out_proj_router",
    )(xp, xs, yp, ys, zp, zs, mods3, g_norm, w_out, w_r, b_r)


def _route_kernel(bid_ref, pos_ref, meta_ref):
    nb = 32
    tm = float(ROW_TILE)
    sub = lax.broadcasted_iota(jnp.int32, (nb, TILE), 0)
    ri = lax.broadcasted_iota(jnp.int32, (TILE, TILE), 0)
    ci = lax.broadcasted_iota(jnp.int32, (TILE, TILE), 1)
    before = jnp.where(ri < ci, 1.0, 0.0).astype(BF16)

    def onehot(blk):
        return jnp.where(sub == bid_ref[blk], 1.0, 0.0)

    zeros = jnp.zeros((nb, 1), F32)
    cnt = lax.fori_loop(0, N_TILES, lambda blk, c: c + jnp.sum(onehot(blk), axis=1, keepdims=True), zeros)
    padded = jnp.floor((cnt + (tm - 1.0)) * (1.0 / tm)) * tm
    r32 = lax.broadcasted_iota(jnp.int32, (nb, nb), 0)
    c32 = lax.broadcasted_iota(jnp.int32, (nb, nb), 1)
    padded_row = jnp.sum(jnp.where(r32 == c32, padded, 0.0), axis=0, keepdims=True)
    offs = jnp.sum(jnp.where(c32 < r32, padded_row, 0.0), axis=1, keepdims=True)
    ends = offs + padded

    def place(blk, seen):
        oh = onehot(blk)
        rank = _dot(oh.astype(BF16), before)
        pos = jnp.sum(oh * (rank + seen + offs), axis=0, keepdims=True)
        pos_ref[blk] = pos.astype(jnp.int32)
        return seen + jnp.sum(oh, axis=1, keepdims=True)

    lax.fori_loop(0, N_TILES, place, zeros)

    start = lax.broadcasted_iota(jnp.int32, (nb, 128), 1).astype(F32) * tm
    bsub = lax.broadcasted_iota(jnp.int32, (nb, 128), 0)
    done = jnp.where((bsub < N_BUCKETS) & (ends <= start), 1.0, 0.0)
    tb = jnp.sum(done, axis=0, keepdims=True)
    valid = jnp.where(tb < N_BUCKETS, 1.0, 0.0)
    tbc = jnp.minimum(tb, N_BUCKETS - 1.0)
    grp = jnp.floor((tbc + 0.5) * (1.0 / PAIRS_PER_GROUP))
    pair = tbc - PAIRS_PER_GROUP * grp
    loc_a = jnp.zeros_like(pair)
    loc_b = jnp.zeros_like(pair)
    for k, (sa, sb) in enumerate(PAIR_SLOTS):
        loc_a = jnp.where(pair == k, float(sa), loc_a)
        loc_b = jnp.where(pair == k, float(sb), loc_b)
    row8 = lax.broadcasted_iota(jnp.int32, (8, 128), 0)
    meta = jnp.where(row8 == 0, grp * EXPERTS_PER_GROUP + loc_a,
                     jnp.where(row8 == 1, grp * EXPERTS_PER_GROUP + loc_b,
                               jnp.where(row8 == 2, valid, 0.0)))
    meta_ref[...] = meta.astype(jnp.int32)


def _route(bid):
    return pl.pallas_call(
        _route_kernel,
        out_shape=(jax.ShapeDtypeStruct((N_TILES, 1, TILE), jnp.int32),
                   jax.ShapeDtypeStruct((8, 128), jnp.int32)),
        compiler_params=pltpu.CompilerParams(vmem_limit_bytes=VMEM_LIMIT),
        name="moe_route",
    )(bid)


def _moe_kernel(meta_ref, pos_ref, h2_hbm, wga_ref, wua_ref, wda_ref, wgb_ref, wub_ref, wdb_ref,
                y_ref, src_ref, xbuf, sem, wga_s, wua_s, wda_s, wgb_s, wub_s, wdb_s):
    j = pl.program_id(0)

    def row_copy(tile, r, slot):
        tok = src_ref[tile * ROW_TILE + r]
        return pltpu.make_async_copy(h2_hbm.at[tok], xbuf.at[slot, r], sem.at[slot])

    def issue(tile, slot, lo, hi):
        def body(r, c):
            row_copy(tile, r, slot).start()
            return c
        lax.fori_loop(lo, hi, body, 0, unroll=8)

    def wait(slot):
        pltpu.make_async_copy(h2_hbm.at[pl.ds(0, ROW_TILE)], xbuf.at[slot], sem.at[slot]).wait()

    @pl.when(j == 0)
    def _():
        def clear(p, c):
            src_ref[p] = 0
            return c
        lax.fori_loop(0, ROW_CAP, clear, 0, unroll=8)

        def invert(t, c):
            src_ref[pos_ref[t]] = t
            return c
        lax.fori_loop(0, T_ALL, invert, 0, unroll=8)

        @pl.when(meta_ref[2, 0] == 1)
        def _():
            issue(0, 0, 0, ROW_TILE)

    nxt = jnp.minimum(j + 1, N_ROW_TILES - 1)

    @pl.when((j + 1 < N_ROW_TILES) & (meta_ref[2, nxt] == 1))
    def _():
        issue(nxt, nxt % 2, 0, ROW_TILE)

    valid = meta_ref[2, j] == 1
    prev = jnp.maximum(j - 1, 0)

    @pl.when(valid & ((j == 0) | (meta_ref[0, j] != meta_ref[0, prev])))
    def _():
        wga_s[...] = wga_ref[0].astype(BF16)
        wua_s[...] = wua_ref[0].astype(BF16)
        wda_s[...] = wda_ref[0].astype(BF16)

    @pl.when(valid & ((j == 0) | (meta_ref[1, j] != meta_ref[1, prev])))
    def _():
        wgb_s[...] = wgb_ref[0].astype(BF16)
        wub_s[...] = wub_ref[0].astype(BF16)
        wdb_s[...] = wdb_ref[0].astype(BF16)

    @pl.when(valid)
    def _():
        slot = j % 2
        wait(slot)
        x = jnp.concatenate([xbuf[slot, :, s, :] for s in range(FEAT_SLABS)], axis=1).astype(BF16)
        gates = xbuf[slot, :, FEAT_SLABS, :]

        def expert(wg, wu, gate):
            hg = _dot(x, wg[...])
            hu = _dot(x, wu[...])
            return (hg * jax.nn.sigmoid(hg) * hu * gate).astype(BF16)

        act_a = expert(wga_s, wua_s, gates[:, 0:1])
        act_b = expert(wgb_s, wub_s, gates[:, 1:2])
        y = _dot(act_a, wda_s[...]) + _dot(act_b, wdb_s[...])
        for s in range(FEAT_SLABS):
            y_ref[:, s, :] = y[:, s * LANES:(s + 1) * LANES]

    @pl.when(jnp.logical_not(valid))
    def _():
        y_ref[...] = jnp.zeros_like(y_ref)


def _moe(meta, pos, h2ext, w_gate, w_up, w_down):
    up_spec = lambda slot: pl.BlockSpec((1, D_MODEL, EXPERT_FF), lambda j, meta, pos: (meta[slot, j], 0, 0))
    down_spec = lambda slot: pl.BlockSpec((1, EXPERT_FF, D_MODEL), lambda j, meta, pos: (meta[slot, j], 0, 0))
    grid_spec = pltpu.PrefetchScalarGridSpec(
        num_scalar_prefetch=2,
        grid=(N_ROW_TILES,),
        in_specs=[pl.BlockSpec(memory_space=pl.ANY),
                  up_spec(0), up_spec(0), down_spec(0), up_spec(1), up_spec(1), down_spec(1)],
        out_specs=pl.BlockSpec((ROW_TILE, FEAT_SLABS, LANES), lambda j, meta, pos: (j, 0, 0)),
        scratch_shapes=[pltpu.SMEM((ROW_CAP,), jnp.int32),
                        pltpu.VMEM((2, ROW_TILE, H2_SLABS, LANES), F32),
                        pltpu.SemaphoreType.DMA((2,)),
                        pltpu.VMEM((D_MODEL, EXPERT_FF), BF16), pltpu.VMEM((D_MODEL, EXPERT_FF), BF16),
                        pltpu.VMEM((EXPERT_FF, D_MODEL), BF16),
                        pltpu.VMEM((D_MODEL, EXPERT_FF), BF16), pltpu.VMEM((D_MODEL, EXPERT_FF), BF16),
                        pltpu.VMEM((EXPERT_FF, D_MODEL), BF16)])
    return pl.pallas_call(
        _moe_kernel,
        out_shape=jax.ShapeDtypeStruct((ROW_CAP, FEAT_SLABS, LANES), F32),
        grid_spec=grid_spec,
        compiler_params=_cparams(("arbitrary",)),
        name="moe_experts",
    )(meta, pos, h2ext, w_gate, w_up, w_down, w_gate, w_up, w_down)


def _final_kernel(pos_ref, y_hbm, x1_ref, m_ref, gn_ref, op_ref, os_ref, ybuf, sem):
    i = pl.program_id(0)

    def row_copy(tile, r, slot):
        p = pos_ref[tile * TILE + r]
        return pltpu.make_async_copy(y_hbm.at[p], ybuf.at[slot, r], sem.at[slot])

    def issue(tile, slot):
        def body(r2, c):
            row_copy(tile, 2 * r2, slot).start(priority=0)
            row_copy(tile, 2 * r2 + 1, slot).start(priority=1)
            return c
        lax.fori_loop(0, TILE // 2, body, 0, unroll=4)

    def wait(slot):
        pltpu.make_async_copy(y_hbm.at[pl.ds(0, TILE)], ybuf.at[slot], sem.at[slot]).wait()

    @pl.when(i == 0)
    def _():
        issue(0, 0)

    @pl.when(i + 1 < N_TILES)
    def _():
        issue(i + 1, (i + 1) % 2)

    slot = i % 2
    wait(slot)
    y = jnp.concatenate([ybuf[slot, :, s, :] for s in range(FEAT_SLABS)], axis=1)
    out = x1_ref[...] + m_ref[0, 5:6, :] * _rms(y, gn_ref[3:4, :])

    @pl.when(i < N_TILES_P)
    def _():
        op_ref[...] = out

    @pl.when(i >= N_TILES_P)
    def _():
        os_ref[...] = out


def _final(pos, y_sorted, x1, mods3, g_norm):
    tps = DEC_SEQ // TILE
    grid_spec = pltpu.PrefetchScalarGridSpec(
        num_scalar_prefetch=1,
        grid=(N_TILES,),
        in_specs=[pl.BlockSpec(memory_space=pl.ANY),
                  pl.BlockSpec((TILE, D_MODEL), lambda i, pos: (i, 0)),
                  pl.BlockSpec((1, N_MOD, D_MODEL), lambda i, pos: (_mod_row_of_tile(i, tps, N_TILES_P), 0, 0)),
                  pl.BlockSpec((4, D_MODEL), lambda i, pos: (0, 0))],
        out_specs=(pl.BlockSpec((TILE, D_MODEL), lambda i, pos: (jnp.minimum(i, N_TILES_P - 1), 0)),
                   pl.BlockSpec((TILE, D_MODEL), lambda i, pos: (jnp.maximum(i - N_TILES_P, 0), 0))),
        scratch_shapes=[pltpu.VMEM((2, TILE, FEAT_SLABS, LANES), F32), pltpu.SemaphoreType.DMA((2,))])
    return pl.pallas_call(
        _final_kernel,
        out_shape=(jax.ShapeDtypeStruct((T_PROMPT, D_MODEL), F32),
                   jax.ShapeDtypeStruct((T_SAMPLE, D_MODEL), F32)),
        grid_spec=grid_spec,
        compiler_params=_cparams(("arbitrary",)),
        name="moe_combine_final",
    )(pos, y_sorted, x1, mods3, g_norm)


def kernel(x_prompt, x_sample, state_C, state_n, state_m, c, c_ctx, w_ada, b_ada, g_norm, w_in, ml_gate_bias, ml_head_gain, hy_conv_w, hy_f_w1, hy_f_b1, hy_f_w2, hy_f_b2, hy_f_w3, hy_f_b3, hy_decay, hy_bias, w_out, w_rc, b_rc, w_rf, b_rf, w_gate, w_up, w_down):
    xp = x_prompt.reshape(T_PROMPT, D_MODEL)
    xs = x_sample.reshape(T_SAMPLE, D_MODEL)
    gn = g_norm[0]

    cv = jnp.concatenate([c_ctx[None, :], c, jnp.zeros((MOD_ROWS - 1 - DEC_BATCH, D_MODEL), F32)], axis=0)
    mods3 = _ada(cv, w_ada[0], b_ada[0]).reshape(MOD_ROWS, N_MOD, D_MODEL)

    w_in0 = w_in[0]
    w_main = jnp.concatenate([w_in0[:, :ML_QKVO_COLS], w_in0[:, ML_QKVO_COLS + ML_GATE_COLS:]],
                             axis=1).astype(BF16)
    wg = w_in0[:, ML_QKVO_COLS:ML_QKVO_COLS + ML_GATE_COLS]
    gbt = ml_gate_bias[0].reshape(ML_GATE_COLS, 1)
    proj, gates, gates_t = _inproj(xp, xs, mods3, gn, w_main, wg.T, gbt)

    gain = ml_head_gain[0].reshape(1, ML_WIDTH)
    y_ml_p, c_new, n_new, m_new = _mlstm(proj, gates, gates_t, gain, None, SEQ, BATCH, 0)
    state = (state_C[:, 0], state_n[:, 0], state_m[:, 0].reshape(DEC_BATCH, 2 * ML_HEADS, 1))
    y_ml_s, _, _, _ = _mlstm(proj, gates, gates_t, gain, state, DEC_SEQ, DEC_BATCH, T_PROMPT // DEC_SEQ)

    w1p = jnp.pad(hy_f_w1[0], ((0, 128 - HY_EMB), (0, 0)))
    b1 = hy_f_b1[0].reshape(1, -1)
    b2 = hy_f_b2[0].reshape(1, -1)
    b3 = hy_f_b3[0].reshape(1, -1)
    dec = hy_decay[0].reshape(1, -1)
    z_parts = []
    for seq_len, n_seq, off, width in ((SEQ, BATCH, 0, SEQ), (DEC_SEQ, DEC_BATCH, T_PROMPT // DEC_SEQ, GRID_W)):
        f, ft = _dft_mats(seq_len)
        coefs = _hyena_filters(seq_len, f, w1p, b1, hy_f_w2[0], b2, hy_f_w3[0], b3, dec)
        z_parts.append(_hyena(proj, hy_conv_w[0], coefs, hy_bias[0], f, ft, seq_len, n_seq, off, width))
    z_p, z_s = z_parts

    pad_r = ROUTER_ROWS - N_GROUPS - N_EXPERTS
    w_r = jnp.pad(jnp.concatenate([w_rc[0], w_rf[0]], axis=1).T, ((0, pad_r), (0, 0)))
    b_r = jnp.pad(jnp.concatenate([b_rc[0], b_rf[0]], axis=0), (0, pad_r)).reshape(ROUTER_ROWS, 1)
    x1, h2ext, bid = _outproj(xp, xs, y_ml_p, y_ml_s, z_p, z_s, mods3, gn, w_out[0].astype(BF16), w_r, b_r)

    pos3, meta = _route(bid)
    pos = pos3.reshape(T_ALL)
    y_sorted = _moe(meta, pos, h2ext, w_gate[0], w_up[0], w_down[0])
    y_p, y_s = _final(pos, y_sorted, x1, mods3, gn)

    new_c = c_new.reshape(BATCH, 1, 2, ML_HEADS, ML_HEAD_DIM, ML_HEAD_DIM)
    new_n = n_new.reshape(BATCH, 1, 2, ML_HEADS, ML_HEAD_DIM)
    new_m = m_new[:, :, 0].reshape(BATCH, 1, 2, ML_HEADS)
    return (y_p.reshape(BATCH, SEQ, D_MODEL), y_s.reshape(DEC_BATCH, DEC_SEQ, D_MODEL), new_c, new_n, new_m)
```

```python
import functools
import math

import jax
import jax.numpy as jnp
from jax import lax
from jax.experimental import pallas as pl
from jax.experimental.pallas import tpu as pltpu

F32 = jnp.float32
BF16 = jnp.bfloat16

D_MODEL = 1024
BATCH = 16
SEQ = 256
DEC_BATCH = 4
DEC_SEQ = 1024
GRID_W = 64
ML_WIDTH = 512
ML_HEADS = 4
ML_HEAD_DIM = 128
HY_WIDTH = 512
HY_ORDER = 2
HY_EMB = 33
HY_BANDS = 16
HY_FILTER_HIDDEN = 64
HY_MOD_SHIFT = 0.05
N_GROUPS = 4
EXPERTS_PER_GROUP = 4
N_EXPERTS = 16
EXPERT_FF = 512
N_MOD = 6
EPS = 1e-6
ML_QKVO_COLS = 4 * ML_WIDTH
ML_GATE_COLS = 4 * ML_HEADS
HY_COLS = 3 * HY_WIDTH
MAIN_COLS = ML_QKVO_COLS + HY_COLS

T_PROMPT = BATCH * SEQ
T_SAMPLE = DEC_BATCH * DEC_SEQ
T_ALL = T_PROMPT + T_SAMPLE
TILE = 256
N_TILES_P = T_PROMPT // TILE
N_TILES = T_ALL // TILE
MOD_ROWS = 8
K_SCALE = ML_HEAD_DIM ** -0.5
VMEM_LIMIT = 56 * 1024 * 1024


def _cparams(sem):
    return pltpu.CompilerParams(dimension_semantics=sem, vmem_limit_bytes=VMEM_LIMIT)


def _split2(x):
    hi = x.astype(BF16)
    lo = (x - hi.astype(F32)).astype(BF16)
    return hi, lo


def _dot(a, b):
    return jnp.dot(a, b, preferred_element_type=F32)


def _dot_nt(a, b):
    return lax.dot_general(a, b, (((1,), (1,)), ((), ())), preferred_element_type=F32)


def _dot_tn(a, b):
    return lax.dot_general(a, b, (((0,), (0,)), ((), ())), preferred_element_type=F32)


def _dot3(a, b):
    ah, al = _split2(a)
    bh, bl = _split2(b)
    return _dot(ah, bh) + _dot(al, bh) + _dot(ah, bl)


def _dot3_nt(a, b):
    ah, al = _split2(a)
    bh, bl = _split2(b)
    return _dot_nt(ah, bh) + _dot_nt(al, bh) + _dot_nt(ah, bl)


def _dot_exact_lhs(t, x):
    x1 = x.astype(BF16)
    r1 = x - x1.astype(F32)
    x2 = r1.astype(BF16)
    x3 = (r1 - x2.astype(F32)).astype(BF16)
    return _dot(t, x1) + _dot(t, x2) + _dot(t, x3)


def _dot_exact_rhs(x, t):
    x1 = x.astype(BF16)
    r1 = x - x1.astype(F32)
    x2 = r1.astype(BF16)
    x3 = (r1 - x2.astype(F32)).astype(BF16)
    return _dot(x1, t) + _dot(x2, t) + _dot(x3, t)


def _rms(x, g):
    return x * lax.rsqrt(jnp.mean(x * x, axis=-1, keepdims=True) + EPS) * g


def _mod_row_of_tile(i, tiles_per_sample_seq, n_prompt_tiles):
    return jnp.where(i < n_prompt_tiles, 0, 1 + (i - n_prompt_tiles) // tiles_per_sample_seq)


def _ada_kernel(cv_ref, w_ref, b_ref, o_ref):
    cv = cv_ref[...]
    s = cv * jax.nn.sigmoid(cv)
    o_ref[...] = _dot3(s, w_ref[...]) + b_ref[...]


def _ada(cv, w_ada, b_ada):
    n = N_MOD * D_MODEL
    return pl.pallas_call(
        _ada_kernel,
        out_shape=jax.ShapeDtypeStruct((MOD_ROWS, n), F32),
        grid=(N_MOD,),
        in_specs=[pl.BlockSpec((MOD_ROWS, D_MODEL), lambda j: (0, 0)),
                  pl.BlockSpec((D_MODEL, D_MODEL), lambda j: (0, j)),
                  pl.BlockSpec((1, D_MODEL), lambda j: (0, j))],
        out_specs=pl.BlockSpec((MOD_ROWS, D_MODEL), lambda j: (0, j)),
        compiler_params=_cparams(("arbitrary",)),
        name="ada_mod",
    )(cv, w_ada, b_ada.reshape(1, n))


def _log_sigmoid(x):
    return jnp.minimum(x, 0.0) - jnp.log1p(jnp.exp(-jnp.abs(x)))


def _rows_to_cols(rows):
    ri = lax.broadcasted_iota(jnp.int32, (TILE, TILE), 0)
    ci = lax.broadcasted_iota(jnp.int32, (TILE, TILE), 1)
    eye = jnp.where(ri == ci, 1.0, 0.0).astype(BF16)
    p1 = rows.astype(BF16)
    r1 = rows - p1.astype(F32)
    p2 = r1.astype(BF16)
    p3 = (r1 - p2.astype(F32)).astype(BF16)
    return _dot_nt(eye, p1) + _dot_nt(eye, p2) + _dot_nt(eye, p3)


def _inproj_kernel(xp_ref, xs_ref, m_ref, gn_ref, w_ref, wgt_ref, gbt_ref, proj_ref, gate_ref, gatet_ref):
    i = pl.program_id(0)
    x = jnp.where(i < N_TILES_P, xp_ref[...], xs_ref[...])
    h = _rms(x, gn_ref[0:1, :]) * (1.0 + m_ref[0, 1:2, :]) + m_ref[0, 0:1, :]
    hb = h.astype(BF16)
    cb = 512
    for j in range(MAIN_COLS // cb):
        proj_ref[:, j * cb:(j + 1) * cb] = _dot(hb, w_ref[:, j * cb:(j + 1) * cb]).astype(BF16)
    hl = (h - hb.astype(F32)).astype(BF16)
    wth, wtl = _split2(wgt_ref[...])
    gt = _dot_nt(wth, hb) + _dot_nt(wth, hl) + _dot_nt(wtl, hb) + gbt_ref[...]
    row = lax.broadcasted_iota(jnp.int32, gt.shape, 0)
    gt = jnp.where((row % 8) >= 4, _log_sigmoid(gt), gt)
    gatet_ref[0] = gt
    gate_ref[...] = _rows_to_cols(gt)


def _inproj(xp, xs, mods3, g_norm, w_main, wgt, gbt):
    tps = DEC_SEQ // TILE
    return pl.pallas_call(
        _inproj_kernel,
        out_shape=(jax.ShapeDtypeStruct((T_ALL, MAIN_COLS), BF16),
                   jax.ShapeDtypeStruct((T_ALL, ML_GATE_COLS), F32),
                   jax.ShapeDtypeStruct((N_TILES, ML_GATE_COLS, TILE), F32)),
        grid=(N_TILES,),
        in_specs=[pl.BlockSpec((TILE, D_MODEL), lambda i: (jnp.minimum(i, N_TILES_P - 1), 0)),
                  pl.BlockSpec((TILE, D_MODEL), lambda i: (jnp.maximum(i - N_TILES_P, 0), 0)),
                  pl.BlockSpec((1, N_MOD, D_MODEL), lambda i: (_mod_row_of_tile(i, tps, N_TILES_P), 0, 0)),
                  pl.BlockSpec((4, D_MODEL), lambda i: (0, 0)),
                  pl.BlockSpec((D_MODEL, MAIN_COLS), lambda i: (0, 0)),
                  pl.BlockSpec((ML_GATE_COLS, D_MODEL), lambda i: (0, 0)),
                  pl.BlockSpec((ML_GATE_COLS, 1), lambda i: (0, 0))],
        out_specs=(pl.BlockSpec((TILE, MAIN_COLS), lambda i: (i, 0)),
                   pl.BlockSpec((TILE, ML_GATE_COLS), lambda i: (i, 0)),
                   pl.BlockSpec((1, ML_GATE_COLS, TILE), lambda i: (i, 0, 0))),
        compiler_params=_cparams(("arbitrary",)),
        name="in_proj",
    )(xp, xs, mods3, g_norm, w_main, wgt, gbt)


def _mlstm_kernel(*refs, seq_len, has_state):
    if has_state:
        (q_ref, k_ref, v_ref, o_ref, g_ref, gt_ref, gain_ref, c0_ref, n0_ref, m0_ref,
         y_ref, c_ref, n_ref, m_ref, hf_ref, hb_ref, cs_ref, ns_ref, ms_ref) = refs
    else:
        (q_ref, k_ref, v_ref, o_ref, g_ref, gt_ref, gain_ref,
         y_ref, c_ref, n_ref, m_ref, hf_ref, hb_ref, cs_ref, ns_ref, ms_ref) = refs
    ch = TILE
    nc = seq_len // ch
    hd = ML_HEAD_DIM
    ri = lax.broadcasted_iota(jnp.int32, (ch, ch), 0)
    ci = lax.broadcasted_iota(jnp.int32, (ch, ch), 1)
    lower = ci <= ri
    upper = ci >= ri
    t_low = jnp.where(lower, 1.0, 0.0).astype(BF16)
    t_up = jnp.where(upper, 1.0, 0.0).astype(BF16)

    for d in range(2):
        for h in range(ML_HEADS):
            r = d * ML_HEADS + h
            if has_state:
                cs_ref[r] = c0_ref[0, d, h]
                ns_ref[r] = n0_ref[0, d, h:h + 1, :]
                ms_ref[r] = jnp.broadcast_to(m0_ref[0, r:r + 1, :], (1, hd))
            else:
                cs_ref[r] = jnp.zeros((hd, hd), F32)
                ns_ref[r] = jnp.zeros((1, hd), F32)
                ms_ref[r] = jnp.zeros((1, hd), F32)

    def step(t, carry):
        for d in range(2):
            c = t if d == 0 else nc - 1 - t
            r0 = pl.multiple_of(c * ch, ch)
            rows = pl.ds(r0, ch)
            gcol = g_ref[rows, :]
            grow = gt_ref[c]
            tmat_c = t_low if d == 0 else t_up
            tmat_r = t_up if d == 0 else t_low
            bcol_all = _dot_exact_lhs(tmat_c, gcol)
            brow_all = _dot_exact_rhs(grow, tmat_r)
            mask = lower if d == 0 else upper
            hacc_ref = hf_ref if d == 0 else hb_ref
            for h in range(ML_HEADS):
                r = d * ML_HEADS + h
                fcol = (1 + 2 * d) * ML_HEADS + h
                icol = (2 * d) * ML_HEADS + h
                cols = slice(h * hd, (h + 1) * hd)
                q = q_ref[rows, cols]
                k = k_ref[rows, cols]
                v = v_ref[rows, cols]
                c_prev = cs_ref[r]
                n_prev = ns_ref[r]
                m_prev = ms_ref[r][:, 0:1]
                b_col = bcol_all[:, fcol:fcol + 1]
                b_row = brow_all[fcol:fcol + 1, :]
                ig_row = grow[icol:icol + 1, :]
                ig_col = gcol[:, icol:icol + 1]
                logd = jnp.where(mask, b_col - b_row + ig_row, -jnp.inf)
                inter = b_col + m_prev
                m_pos = jnp.maximum(inter, jnp.max(logd, axis=-1, keepdims=True))
                s = _dot_nt(q, k) * K_SCALE * jnp.exp(logd - m_pos)
                sc_inter = jnp.exp(inter - m_pos)
                qf = q.astype(F32)
                num = sc_inter * _dot(q, c_prev.astype(BF16)) + _dot(s.astype(BF16), v)
                den = (sc_inter * jnp.sum(qf * n_prev, axis=-1, keepdims=True)
                       + jnp.sum(s, axis=-1, keepdims=True))
                hh = num / jnp.maximum(jnp.abs(den), jnp.exp(-m_pos))
                hacc_ref[rows, cols] = hh
                b_last = b_col[ch - 1:ch, :] if d == 0 else b_col[0:1, :]
                logw = b_last - b_col + ig_col
                m_new = jnp.maximum(b_last + m_prev, jnp.max(logw, axis=0, keepdims=True))
                w = jnp.exp(logw - m_new)
                decay = jnp.exp(b_last + m_prev - m_new)
                kw = k.astype(F32) * (w * K_SCALE)
                cs_ref[r] = decay * c_prev + _dot_tn(kw.astype(BF16), v)
                ns_ref[r] = decay * n_prev + jnp.sum(kw, axis=0, keepdims=True)
                ms_ref[r] = jnp.broadcast_to(m_new, (1, hd))
        return carry

    lax.fori_loop(0, nc, step, 0)

    for d in range(2):
        for h in range(ML_HEADS):
            r = d * ML_HEADS + h
            c_ref[0, d, h] = cs_ref[r]
            n_ref[0, d, h:h + 1, :] = ns_ref[r]
            m_ref[0, r:r + 1, :] = ms_ref[r]
    for h in range(ML_HEADS):
        cols = slice(h * hd, (h + 1) * hd)
        hh = hf_ref[:, cols] + hb_ref[:, cols]
        hh = hh * lax.rsqrt(jnp.mean(hh * hh, axis=-1, keepdims=True) + EPS)
        y = hh * gain_ref[:, cols] * jax.nn.sigmoid(o_ref[:, cols].astype(F32))
        y_ref[:, cols] = y.astype(BF16)


def _mlstm(proj, gates, gates_t, gain, state, seq_len, n_seq, row_block_off):
    has_state = state is not None
    tiles = seq_len // TILE
    off = row_block_off
    qkvo_specs = [pl.BlockSpec((seq_len, ML_WIDTH), functools.partial(lambda b, j: (off + b, j), j=j))
                  for j in range(4)]
    in_specs = qkvo_specs + [
        pl.BlockSpec((seq_len, ML_GATE_COLS), lambda b: (off + b, 0)),
        pl.BlockSpec((tiles, ML_GATE_COLS, TILE), lambda b: (off + b, 0, 0)),
        pl.BlockSpec((1, ML_WIDTH), lambda b: (0, 0)),
    ]
    args = [proj, proj, proj, proj, gates, gates_t, gain]
    if has_state:
        c0, n0, m0 = state
        in_specs += [
            pl.BlockSpec((1, 2, ML_HEADS, ML_HEAD_DIM, ML_HEAD_DIM), lambda b: (b, 0, 0, 0, 0)),
            pl.BlockSpec((1, 2, ML_HEADS, ML_HEAD_DIM), lambda b: (b, 0, 0, 0)),
            pl.BlockSpec((1, 2 * ML_HEADS, 1), lambda b: (b, 0, 0)),
        ]
        args += [c0, n0, m0]
    out_shape = (jax.ShapeDtypeStruct((n_seq * seq_len, ML_WIDTH), BF16),
                 jax.ShapeDtypeStruct((n_seq, 2, ML_HEADS, ML_HEAD_DIM, ML_HEAD_DIM), F32),
                 jax.ShapeDtypeStruct((n_seq, 2, ML_HEADS, ML_HEAD_DIM), F32),
                 jax.ShapeDtypeStruct((n_seq, 2 * ML_HEADS, ML_HEAD_DIM), F32))
    out_specs = (pl.BlockSpec((seq_len, ML_WIDTH), lambda b: (b, 0)),
                 pl.BlockSpec((1, 2, ML_HEADS, ML_HEAD_DIM, ML_HEAD_DIM), lambda b: (b, 0, 0, 0, 0)),
                 pl.BlockSpec((1, 2, ML_HEADS, ML_HEAD_DIM), lambda b: (b, 0, 0, 0)),
                 pl.BlockSpec((1, 2 * ML_HEADS, ML_HEAD_DIM), lambda b: (b, 0, 0)))
    scratch = [pltpu.VMEM((seq_len, ML_WIDTH), F32), pltpu.VMEM((seq_len, ML_WIDTH), F32),
               pltpu.VMEM((2 * ML_HEADS, ML_HEAD_DIM, ML_HEAD_DIM), F32),
               pltpu.VMEM((2 * ML_HEADS, 1, ML_HEAD_DIM), F32),
               pltpu.VMEM((2 * ML_HEADS, 1, ML_HEAD_DIM), F32)]
    return pl.pallas_call(
        functools.partial(_mlstm_kernel, seq_len=seq_len, has_state=has_state),
        out_shape=out_shape, grid=(n_seq,), in_specs=in_specs, out_specs=out_specs,
        scratch_shapes=scratch, compiler_params=_cparams(("arbitrary",)),
        name=f"mlstm_{seq_len}",
    )(*args)


def _dft_mats(seq_len):
    lo = 32
    hi = seq_len // lo
    d = jnp.arange(seq_len, dtype=jnp.int32)
    a = jnp.arange(hi, dtype=jnp.int32) * lo
    b = jnp.arange(lo, dtype=jnp.int32)
    scale = math.pi / seq_len
    ang_a = ((a[:, None] * d[None, :]) % (2 * seq_len)).astype(F32) * scale
    ang_b = ((b[:, None] * d[None, :]) % (2 * seq_len)).astype(F32) * scale
    ca, sa, cb, sb = jnp.cos(ang_a), jnp.sin(ang_a), jnp.cos(ang_b), jnp.sin(ang_b)
    cosm = (ca[:, None, :] * cb[None, :, :] - sa[:, None, :] * sb[None, :, :]).reshape(seq_len, seq_len)
    sinm = (sa[:, None, :] * cb[None, :, :] + ca[:, None, :] * sb[None, :, :]).reshape(seq_len, seq_len)
    nyq = jnp.where(d % 2 == 0, 1.0, -1.0).astype(F32)
    krow = jnp.arange(seq_len, dtype=jnp.int32)[:, None]
    sinm = jnp.where(krow == 0, nyq[None, :], sinm)
    f = jnp.concatenate([cosm, sinm], axis=0).astype(BF16)
    cos_t = (ca.T[:, :, None] * cb.T[:, None, :] - sa.T[:, :, None] * sb.T[:, None, :]).reshape(seq_len, seq_len)
    sin_t = (sa.T[:, :, None] * cb.T[:, None, :] + ca.T[:, :, None] * sb.T[:, None, :]).reshape(seq_len, seq_len)
    kcol = jnp.arange(seq_len, dtype=jnp.int32)[None, :]
    sin_t = jnp.where(kcol == 0, nyq[:, None], sin_t)
    ft = jnp.concatenate([cos_t, sin_t], axis=1).astype(BF16)
    return f, ft


def _filter_feats(seq_len):
    t = jnp.linspace(0.0, 1.0, seq_len, dtype=F32)[:, None]
    wpos = 2.0 * math.pi * jnp.arange(seq_len, dtype=F32)[:, None] / seq_len
    bands = jnp.linspace(1e-4, HY_BANDS - 1, HY_BANDS, dtype=F32)[None, :]
    z = jnp.concatenate([t, jnp.cos(bands * wpos), -jnp.sin(bands * wpos)], axis=-1)
    return jnp.pad(z, ((0, 0), (0, 128 - HY_EMB)))


def _filter_kernel(z_ref, w1_ref, b1_ref, w2_ref, b2_ref, w3_ref, b3_ref, dec_ref, f_ref,
                   a_ref, b_ref, d_ref, *, seq_len):
    n = 2 * seq_len
    z = z_ref[...]
    h = jnp.sin(_dot3(z, w1_ref[...]) + b1_ref[...])
    h = jnp.sin(_dot3(h, w2_ref[...]) + b2_ref[...])
    h = _dot3(h, w3_ref[...]) + b3_ref[...]
    t = z[:, 0:1]
    h = h * (jnp.exp(-t * jnp.abs(dec_ref[...])) + HY_MOD_SHIFT)
    ss = jnp.sum(h * h, axis=0, keepdims=True)
    inv = lax.rsqrt(ss[:, :HY_WIDTH] + ss[:, HY_WIDTH:] + EPS)
    hp = h[:, :HY_WIDTH] * inv
    hn = h[:, HY_WIDTH:] * inv
    ssum = hp + hn
    sdif = hp - hn
    hc = _dot(f_ref[0:seq_len, :], ssum.astype(BF16))
    hs = _dot(f_ref[seq_len:n, :], sdif.astype(BF16))
    di = lax.broadcasted_iota(jnp.int32, (seq_len, 1), 0)
    sgn = jnp.where(di % 2 == 0, 1.0, -1.0)
    nyq = jnp.sum(ssum * sgn, axis=0, keepdims=True)
    first = di == 0
    a_ref[0] = hc * jnp.where(first, 1.0 / n, 2.0 / n)
    b_ref[0] = jnp.where(first, 0.0, hs * (2.0 / n))
    d_ref[0] = jnp.where(first, nyq * (1.0 / n), hc * (2.0 / n))


def _hyena_filters(seq_len, f, w1p, b1, w2, b2, w3, b3, dec):
    z = _filter_feats(seq_len)
    hid = HY_FILTER_HIDDEN
    oc = 2 * HY_WIDTH
    full = lambda shape: pl.BlockSpec(shape, lambda o: tuple(0 for _ in shape))
    out = jax.ShapeDtypeStruct((HY_ORDER, seq_len, HY_WIDTH), F32)
    return pl.pallas_call(
        functools.partial(_filter_kernel, seq_len=seq_len),
        out_shape=(out, out, out),
        grid=(HY_ORDER,),
        in_specs=[full((seq_len, 128)), full((128, hid)), full((1, hid)), full((hid, hid)), full((1, hid)),
                  pl.BlockSpec((hid, oc), lambda o: (0, o)),
                  pl.BlockSpec((1, oc), lambda o: (0, o)),
                  pl.BlockSpec((1, oc), lambda o: (0, o)),
                  full((2 * seq_len, seq_len))],
        out_specs=tuple(pl.BlockSpec((1, seq_len, HY_WIDTH), lambda o: (o, 0, 0)) for _ in range(3)),
        compiler_params=_cparams(("arbitrary",)),
        name=f"hyena_filter_{seq_len}",
    )(z, w1p, b1, w2, b2, w3, b3, dec, f)


def _hyena_kernel(x1_ref, x2_ref, v_ref, cw1_ref, cw2_ref, cwv_ref, a_ref, b_ref, d_ref, bias_ref,
                  f_ref, ft_ref, z_ref, *, seq_len, width):
    ti = lax.broadcasted_iota(jnp.int32, (seq_len, 1), 0)
    has_prev = (ti % width) != 0
    has_next = (ti % width) != (width - 1)

    def short_conv(x_ref, w_ref):
        x = x_ref[...].astype(F32)
        prev = jnp.where(has_prev, pltpu.roll(x, 1, axis=0), 0.0)
        nxt = jnp.where(has_next, pltpu.roll(x, seq_len - 1, axis=0), 0.0)
        return w_ref[0:1, :] * prev + w_ref[1:2, :] * x + w_ref[2:3, :] * nxt

    gates = (short_conv(x1_ref, cw1_ref), short_conv(x2_ref, cw2_ref))
    z = short_conv(v_ref, cwv_ref)
    for o in range(HY_ORDER):
        u = _dot(f_ref[...], z.astype(BF16))
        ut = u[:seq_len]
        ub = u[seq_len:]
        a, b, dd = a_ref[o], b_ref[o], d_ref[o]
        yt = ut * a - ub * b
        yb = ut * b + ub * dd
        y = _dot(ft_ref[:, :seq_len], yt.astype(BF16)) + _dot(ft_ref[:, seq_len:], yb.astype(BF16))
        z = gates[o] * (y + bias_ref[o:o + 1, :] * z)
    z_ref[...] = z.astype(BF16)


def _hyena(proj, conv_w, coefs, hy_bias, f, ft, seq_len, n_seq, row_block_off, width):
    cb = 256
    nblk = HY_WIDTH // cb
    base = ML_QKVO_COLS // cb
    off = row_block_off
    a, b, d = coefs

    def col_spec(part):
        return pl.BlockSpec((seq_len, cb), lambda s, j: (off + s, base + part * nblk + j))

    def w_spec(part):
        return pl.BlockSpec((3, cb), lambda s, j: (0, part * nblk + j))

    coef_spec = pl.BlockSpec((HY_ORDER, seq_len, cb), lambda s, j: (0, 0, j))
    return pl.pallas_call(
        functools.partial(_hyena_kernel, seq_len=seq_len, width=width),
        out_shape=jax.ShapeDtypeStruct((n_seq * seq_len, HY_WIDTH), BF16),
        grid=(n_seq, nblk),
        in_specs=[col_spec(0), col_spec(1), col_spec(2), w_spec(0), w_spec(1), w_spec(2),
                  coef_spec, coef_spec, coef_spec,
                  pl.BlockSpec((HY_ORDER, cb), lambda s, j: (0, j)),
                  pl.BlockSpec((2 * seq_len, seq_len), lambda s, j: (0, 0)),
                  pl.BlockSpec((seq_len, 2 * seq_len), lambda s, j: (0, 0))],
        out_specs=pl.BlockSpec((seq_len, cb), lambda s, j: (s, j)),
        compiler_params=_cparams(("arbitrary", "arbitrary")),
        name=f"hyena_conv_{seq_len}",
    )(proj, proj, proj, conv_w, conv_w, conv_w, a, b, d, hy_bias, f, ft)


def _first_max(x, n):
    mx = jnp.max(x, axis=0, keepdims=True)
    row = lax.broadcasted_iota(jnp.int32, x.shape, 0).astype(F32)
    idx = jnp.min(jnp.where(x == mx, row, float(n)), axis=0, keepdims=True)
    return mx, idx.astype(jnp.int32)


ROUTER_ROWS = 32
PAIRS_PER_GROUP = 6
N_BUCKETS = N_GROUPS * PAIRS_PER_GROUP
PAIR_SLOTS = ((0, 1), (0, 2), (0, 3), (1, 3), (1, 2), (3, 2))
LANES = 128
FEAT_SLABS = D_MODEL // LANES
H2_SLABS = FEAT_SLABS + 1
ROW_TILE = 256
ROW_CAP = T_ALL + N_BUCKETS * ROW_TILE
N_ROW_TILES = ROW_CAP // ROW_TILE


def _outproj_kernel(xp_ref, xs_ref, yp_ref, ys_ref, zp_ref, zs_ref, m_ref, gn_ref, wo_ref, wr_ref, br_ref,
                    x1_ref, h2_ref, bid_ref):
    i = pl.program_id(0)
    is_p = i < N_TILES_P
    x = jnp.where(is_p, xp_ref[...], xs_ref[...])
    yml = jnp.where(is_p, yp_ref[...], ys_ref[...])
    zz = jnp.where(is_p, zp_ref[...], zs_ref[...])
    y = _dot(yml, wo_ref[0:ML_WIDTH, :]) + _dot(zz, wo_ref[ML_WIDTH:, :])
    x1 = x + m_ref[0, 2:3, :] * _rms(y, gn_ref[1:2, :])
    x1_ref[...] = x1
    h2 = _rms(x1, gn_ref[2:3, :]) * (1.0 + m_ref[0, 4:5, :]) + m_ref[0, 3:4, :]
    for s in range(FEAT_SLABS):
        h2_ref[:, s, :] = h2[:, s * LANES:(s + 1) * LANES]
    h2h, h2l = _split2(h2)
    wrh, wrl = _split2(wr_ref[...])
    logits = _dot_nt(wrh, h2h) + _dot_nt(wrh, h2l) + _dot_nt(wrl, h2h) + br_ref[...]
    lc = logits[0:N_GROUPS]
    mx, gi = _first_max(lc, N_GROUPS)
    p_grp = 1.0 / jnp.sum(jnp.exp(lc - mx), axis=0, keepdims=True)
    lsel = jnp.zeros((EXPERTS_PER_GROUP, TILE), F32)
    for g in range(N_GROUPS):
        lo = N_GROUPS + g * EXPERTS_PER_GROUP
        lsel = jnp.where(gi == g, logits[lo:lo + EXPERTS_PER_GROUP], lsel)
    l1, i1 = _first_max(lsel, EXPERTS_PER_GROUP)
    sub4 = lax.broadcasted_iota(jnp.int32, lsel.shape, 0)
    l2, i2 = _first_max(jnp.where(sub4 == i1, -jnp.inf, lsel), EXPERTS_PER_GROUP)
    e2 = jnp.exp(l2 - l1)
    w1 = p_grp / (1.0 + e2)
    w2 = p_grp * e2 / (1.0 + e2)
    lo_e = jnp.minimum(i1, i2)
    hi_e = jnp.maximum(i1, i2)
    pair = jnp.where(lo_e == 0, hi_e - 1, jnp.where(lo_e == 1, jnp.where(hi_e == 3, 3, 4), 5))
    slot_a = jnp.where(pair == 5, hi_e, lo_e)
    first_in_a = i1 == slot_a
    w_a = jnp.where(first_in_a, w1, w2)
    w_b = jnp.where(first_in_a, w2, w1)
    sub = lax.broadcasted_iota(jnp.int32, (8, TILE), 0)
    gate_rows = jnp.where(sub == 0, w_a, jnp.where(sub == 1, w_b, 0.0))
    h2_ref[:, FEAT_SLABS, :] = jnp.zeros((TILE, LANES), F32)
    h2_ref[:, FEAT_SLABS, 0:8] = _rows_to_cols(gate_rows)
    bid_ref[0] = gi * PAIRS_PER_GROUP + pair


def _outproj(xp, xs, yp, ys, zp, zs, mods3, g_norm, w_out, w_r, b_r):
    tps = DEC_SEQ // TILE
    pidx = lambda i: (jnp.minimum(i, N_TILES_P - 1), 0)
    sidx = lambda i: (jnp.maximum(i - N_TILES_P, 0), 0)
    return pl.pallas_call(
        _outproj_kernel,
        out_shape=(jax.ShapeDtypeStruct((T_ALL, D_MODEL), F32),
                   jax.ShapeDtypeStruct((T_ALL, H2_SLABS, LANES), F32),
                   jax.ShapeDtypeStruct((N_TILES, 1, TILE), jnp.int32)),
        grid=(N_TILES,),
        in_specs=[pl.BlockSpec((TILE, D_MODEL), pidx), pl.BlockSpec((TILE, D_MODEL), sidx),
                  pl.BlockSpec((TILE, ML_WIDTH), pidx), pl.BlockSpec((TILE, ML_WIDTH), sidx),
                  pl.BlockSpec((TILE, HY_WIDTH), pidx), pl.BlockSpec((TILE, HY_WIDTH), sidx),
                  pl.BlockSpec((1, N_MOD, D_MODEL), lambda i: (_mod_row_of_tile(i, tps, N_TILES_P), 0, 0)),
                  pl.BlockSpec((4, D_MODEL), lambda i: (0, 0)),
                  pl.BlockSpec((D_MODEL, D_MODEL), lambda i: (0, 0)),
                  pl.BlockSpec((ROUTER_ROWS, D_MODEL), lambda i: (0, 0)),
                  pl.BlockSpec((ROUTER_ROWS, 1), lambda i: (0, 0))],
        out_specs=(pl.BlockSpec((TILE, D_MODEL), lambda i: (i, 0)),
                   pl.BlockSpec((TILE, H2_SLABS, LANES), lambda i: (i, 0, 0)),
                   pl.BlockSpec((1, 1, TILE), lambda i: (i, 0, 0))),
        compiler_params=_cparams(("arbitrary",)),
        name="out_proj_router",
    )(xp, xs, yp, ys, zp, zs, mods3, g_norm, w_out, w_r, b_r)


def _route_kernel(bid_ref, pos_ref, meta_ref):
    nb = 32
    tm = float(ROW_TILE)
    sub = lax.broadcasted_iota(jnp.int32, (nb, TILE), 0)
    ri = lax.broadcasted_iota(jnp.int32, (TILE, TILE), 0)
    ci = lax.broadcasted_iota(jnp.int32, (TILE, TILE), 1)
    before = jnp.where(ri < ci, 1.0, 0.0).astype(BF16)

    def onehot(blk):
        return jnp.where(sub == bid_ref[blk], 1.0, 0.0)

    zeros = jnp.zeros((nb, 1), F32)
    cnt = lax.fori_loop(0, N_TILES, lambda blk, c: c + jnp.sum(onehot(blk), axis=1, keepdims=True), zeros)
    padded = jnp.floor((cnt + (tm - 1.0)) * (1.0 / tm)) * tm
    r32 = lax.broadcasted_iota(jnp.int32, (nb, nb), 0)
    c32 = lax.broadcasted_iota(jnp.int32, (nb, nb), 1)
    padded_row = jnp.sum(jnp.where(r32 == c32, padded, 0.0), axis=0, keepdims=True)
    offs = jnp.sum(jnp.where(c32 < r32, padded_row, 0.0), axis=1, keepdims=True)
    ends = offs + padded

    def place(blk, seen):
        oh = onehot(blk)
        rank = _dot(oh.astype(BF16), before)
        pos = jnp.sum(oh * (rank + seen + offs), axis=0, keepdims=True)
        pos_ref[blk] = pos.astype(jnp.int32)
        return seen + jnp.sum(oh, axis=1, keepdims=True)

    lax.fori_loop(0, N_TILES, place, zeros)

    start = lax.broadcasted_iota(jnp.int32, (nb, 128), 1).astype(F32) * tm
    bsub = lax.broadcasted_iota(jnp.int32, (nb, 128), 0)
    done = jnp.where((bsub < N_BUCKETS) & (ends <= start), 1.0, 0.0)
    tb = jnp.sum(done, axis=0, keepdims=True)
    valid = jnp.where(tb < N_BUCKETS, 1.0, 0.0)
    tbc = jnp.minimum(tb, N_BUCKETS - 1.0)
    grp = jnp.floor((tbc + 0.5) * (1.0 / PAIRS_PER_GROUP))
    pair = tbc - PAIRS_PER_GROUP * grp
    loc_a = jnp.zeros_like(pair)
    loc_b = jnp.zeros_like(pair)
    for k, (sa, sb) in enumerate(PAIR_SLOTS):
        loc_a = jnp.where(pair == k, float(sa), loc_a)
        loc_b = jnp.where(pair == k, float(sb), loc_b)
    mine = bsub.astype(F32) == tbc
    used = jnp.sum(jnp.where(mine, offs + cnt, 0.0), axis=0, keepdims=True)
    n_rows = jnp.clip(used - start[0:1], 0.0, tm) * valid
    row8 = lax.broadcasted_iota(jnp.int32, (8, 128), 0)
    meta = jnp.where(row8 == 0, grp * EXPERTS_PER_GROUP + loc_a,
                     jnp.where(row8 == 1, grp * EXPERTS_PER_GROUP + loc_b,
                               jnp.where(row8 == 2, valid, jnp.where(row8 == 3, n_rows, 0.0))))
    meta_ref[...] = meta.astype(jnp.int32)


def _route(bid):
    return pl.pallas_call(
        _route_kernel,
        out_shape=(jax.ShapeDtypeStruct((N_TILES, 1, TILE), jnp.int32),
                   jax.ShapeDtypeStruct((8, 128), jnp.int32)),
        compiler_params=pltpu.CompilerParams(vmem_limit_bytes=VMEM_LIMIT),
        name="moe_route",
    )(bid)


def _moe_kernel(meta_ref, pos_ref, h2_hbm, wga_ref, wua_ref, wda_ref, wgb_ref, wub_ref, wdb_ref,
                y_ref, src_ref, xbuf, sem, wga_s, wua_s, wda_s, wgb_s, wub_s, wdb_s):
    j = pl.program_id(0)

    def row_copy(tile, r, slot):
        tok = src_ref[tile * ROW_TILE + r]
        return pltpu.make_async_copy(h2_hbm.at[tok], xbuf.at[slot, r], sem.at[slot])

    group = 8

    def row_groups(tile):
        return (meta_ref[3, tile] + (group - 1)) // group

    def issue(tile, slot):
        def body(g, c):
            for k in range(group):
                row_copy(tile, g * group + k, slot).start()
            return c
        lax.fori_loop(0, row_groups(tile), body, 0)

    def wait(tile, slot):
        def body(g, c):
            for k in range(group):
                row_copy(tile, g * group + k, slot).wait()
            return c
        lax.fori_loop(0, row_groups(tile), body, 0)

    @pl.when(j == 0)
    def _():
        xbuf[...] = jnp.zeros_like(xbuf)

        def clear(p, c):
            src_ref[p] = 0
            return c
        lax.fori_loop(0, ROW_CAP, clear, 0, unroll=8)

        def invert(t, c):
            src_ref[pos_ref[t]] = t
            return c
        lax.fori_loop(0, T_ALL, invert, 0, unroll=8)

        @pl.when(meta_ref[2, 0] == 1)
        def _():
            issue(0, 0)

    nxt = jnp.minimum(j + 1, N_ROW_TILES - 1)

    @pl.when((j + 1 < N_ROW_TILES) & (meta_ref[2, nxt] == 1))
    def _():
        issue(nxt, nxt % 2)

    valid = meta_ref[2, j] == 1
    prev = jnp.maximum(j - 1, 0)

    @pl.when(valid & ((j == 0) | (meta_ref[0, j] != meta_ref[0, prev])))
    def _():
        wga_s[...] = wga_ref[0].astype(BF16)
        wua_s[...] = wua_ref[0].astype(BF16)
        wda_s[...] = wda_ref[0].astype(BF16)

    @pl.when(valid & ((j == 0) | (meta_ref[1, j] != meta_ref[1, prev])))
    def _():
        wgb_s[...] = wgb_ref[0].astype(BF16)
        wub_s[...] = wub_ref[0].astype(BF16)
        wdb_s[...] = wdb_ref[0].astype(BF16)

    @pl.when(valid)
    def _():
        slot = j % 2
        wait(j, slot)
        x = jnp.concatenate([xbuf[slot, :, s, :] for s in range(FEAT_SLABS)], axis=1).astype(BF16)
        gates = xbuf[slot, :, FEAT_SLABS, :]

        def expert(wg, wu, gate):
            hg = _dot(x, wg[...])
            hu = _dot(x, wu[...])
            return (hg * jax.nn.sigmoid(hg) * hu * gate).astype(BF16)

        act_a = expert(wga_s, wua_s, gates[:, 0:1])
        act_b = expert(wgb_s, wub_s, gates[:, 1:2])
        y = _dot(act_a, wda_s[...]) + _dot(act_b, wdb_s[...])
        for s in range(FEAT_SLABS):
            y_ref[:, s, :] = y[:, s * LANES:(s + 1) * LANES]

    @pl.when(jnp.logical_not(valid))
    def _():
        y_ref[...] = jnp.zeros_like(y_ref)


def _moe(meta, pos, h2ext, w_gate, w_up, w_down):
    up_spec = lambda slot: pl.BlockSpec((1, D_MODEL, EXPERT_FF), lambda j, meta, pos: (meta[slot, j], 0, 0))
    down_spec = lambda slot: pl.BlockSpec((1, EXPERT_FF, D_MODEL), lambda j, meta, pos: (meta[slot, j], 0, 0))
    grid_spec = pltpu.PrefetchScalarGridSpec(
        num_scalar_prefetch=2,
        grid=(N_ROW_TILES,),
        in_specs=[pl.BlockSpec(memory_space=pl.ANY),
                  up_spec(0), up_spec(0), down_spec(0), up_spec(1), up_spec(1), down_spec(1)],
        out_specs=pl.BlockSpec((ROW_TILE, FEAT_SLABS, LANES), lambda j, meta, pos: (j, 0, 0)),
        scratch_shapes=[pltpu.SMEM((ROW_CAP,), jnp.int32),
                        pltpu.VMEM((2, ROW_TILE, H2_SLABS, LANES), F32),
                        pltpu.SemaphoreType.DMA((2,)),
                        pltpu.VMEM((D_MODEL, EXPERT_FF), BF16), pltpu.VMEM((D_MODEL, EXPERT_FF), BF16),
                        pltpu.VMEM((EXPERT_FF, D_MODEL), BF16),
                        pltpu.VMEM((D_MODEL, EXPERT_FF), BF16), pltpu.VMEM((D_MODEL, EXPERT_FF), BF16),
                        pltpu.VMEM((EXPERT_FF, D_MODEL), BF16)])
    return pl.pallas_call(
        _moe_kernel,
        out_shape=jax.ShapeDtypeStruct((ROW_CAP, FEAT_SLABS, LANES), F32),
        grid_spec=grid_spec,
        compiler_params=_cparams(("arbitrary",)),
        name="moe_experts",
    )(meta, pos, h2ext, w_gate, w_up, w_down, w_gate, w_up, w_down)


def _final_kernel(pos_ref, y_hbm, x1_ref, m_ref, gn_ref, op_ref, os_ref, ybuf, sem):
    i = pl.program_id(0)

    def row_copy(tile, r, slot):
        p = pos_ref[tile * TILE + r]
        return pltpu.make_async_copy(y_hbm.at[p], ybuf.at[slot, r], sem.at[slot])

    def issue(tile, slot):
        def body(r2, c):
            row_copy(tile, 2 * r2, slot).start(priority=0)
            row_copy(tile, 2 * r2 + 1, slot).start(priority=1)
            return c
        lax.fori_loop(0, TILE // 2, body, 0, unroll=4)

    def wait(slot):
        pltpu.make_async_copy(y_hbm.at[pl.ds(0, TILE)], ybuf.at[slot], sem.at[slot]).wait()

    @pl.when(i == 0)
    def _():
        issue(0, 0)

    @pl.when(i + 1 < N_TILES)
    def _():
        issue(i + 1, (i + 1) % 2)

    slot = i % 2
    wait(slot)
    y = jnp.concatenate([ybuf[slot, :, s, :] for s in range(FEAT_SLABS)], axis=1)
    out = x1_ref[...] + m_ref[0, 5:6, :] * _rms(y, gn_ref[3:4, :])

    @pl.when(i < N_TILES_P)
    def _():
        op_ref[...] = out

    @pl.when(i >= N_TILES_P)
    def _():
        os_ref[...] = out


def _final(pos, y_sorted, x1, mods3, g_norm):
    tps = DEC_SEQ // TILE
    grid_spec = pltpu.PrefetchScalarGridSpec(
        num_scalar_prefetch=1,
        grid=(N_TILES,),
        in_specs=[pl.BlockSpec(memory_space=pl.ANY),
                  pl.BlockSpec((TILE, D_MODEL), lambda i, pos: (i, 0)),
                  pl.BlockSpec((1, N_MOD, D_MODEL), lambda i, pos: (_mod_row_of_tile(i, tps, N_TILES_P), 0, 0)),
                  pl.BlockSpec((4, D_MODEL), lambda i, pos: (0, 0))],
        out_specs=(pl.BlockSpec((TILE, D_MODEL), lambda i, pos: (jnp.minimum(i, N_TILES_P - 1), 0)),
                   pl.BlockSpec((TILE, D_MODEL), lambda i, pos: (jnp.maximum(i - N_TILES_P, 0), 0))),
        scratch_shapes=[pltpu.VMEM((2, TILE, FEAT_SLABS, LANES), F32), pltpu.SemaphoreType.DMA((2,))])
    return pl.pallas_call(
        _final_kernel,
        out_shape=(jax.ShapeDtypeStruct((T_PROMPT, D_MODEL), F32),
                   jax.ShapeDtypeStruct((T_SAMPLE, D_MODEL), F32)),
        grid_spec=grid_spec,
        compiler_params=_cparams(("arbitrary",)),
        name="moe_combine_final",
    )(pos, y_sorted, x1, mods3, g_norm)


def kernel(x_prompt, x_sample, state_C, state_n, state_m, c, c_ctx, w_ada, b_ada, g_norm, w_in, ml_gate_bias, ml_head_gain, hy_conv_w, hy_f_w1, hy_f_b1, hy_f_w2, hy_f_b2, hy_f_w3, hy_f_b3, hy_decay, hy_bias, w_out, w_rc, b_rc, w_rf, b_rf, w_gate, w_up, w_down):
    xp = x_prompt.reshape(T_PROMPT, D_MODEL)
    xs = x_sample.reshape(T_SAMPLE, D_MODEL)
    gn = g_norm[0]

    cv = jnp.concatenate([c_ctx[None, :], c, jnp.zeros((MOD_ROWS - 1 - DEC_BATCH, D_MODEL), F32)], axis=0)
    mods3 = _ada(cv, w_ada[0], b_ada[0]).reshape(MOD_ROWS, N_MOD, D_MODEL)

    w_in0 = w_in[0]
    w_main = jnp.concatenate([w_in0[:, :ML_QKVO_COLS], w_in0[:, ML_QKVO_COLS + ML_GATE_COLS:]],
                             axis=1).astype(BF16)
    wg = w_in0[:, ML_QKVO_COLS:ML_QKVO_COLS + ML_GATE_COLS]
    gbt = ml_gate_bias[0].reshape(ML_GATE_COLS, 1)
    proj, gates, gates_t = _inproj(xp, xs, mods3, gn, w_main, wg.T, gbt)

    gain = ml_head_gain[0].reshape(1, ML_WIDTH)
    y_ml_p, c_new, n_new, m_new = _mlstm(proj, gates, gates_t, gain, None, SEQ, BATCH, 0)
    state = (state_C[:, 0], state_n[:, 0], state_m[:, 0].reshape(DEC_BATCH, 2 * ML_HEADS, 1))
    y_ml_s, _, _, _ = _mlstm(proj, gates, gates_t, gain, state, DEC_SEQ, DEC_BATCH, T_PROMPT // DEC_SEQ)

    w1p = jnp.pad(hy_f_w1[0], ((0, 128 - HY_EMB), (0, 0)))
    b1 = hy_f_b1[0].reshape(1, -1)
    b2 = hy_f_b2[0].reshape(1, -1)
    b3 = hy_f_b3[0].reshape(1, -1)
    dec = hy_decay[0].reshape(1, -1)
    z_parts = []
    for seq_len, n_seq, off, width in ((SEQ, BATCH, 0, SEQ), (DEC_SEQ, DEC_BATCH, T_PROMPT // DEC_SEQ, GRID_W)):
        f, ft = _dft_mats(seq_len)
        coefs = _hyena_filters(seq_len, f, w1p, b1, hy_f_w2[0], b2, hy_f_w3[0], b3, dec)
        z_parts.append(_hyena(proj, hy_conv_w[0], coefs, hy_bias[0], f, ft, seq_len, n_seq, off, width))
    z_p, z_s = z_parts

    pad_r = ROUTER_ROWS - N_GROUPS - N_EXPERTS
    w_r = jnp.pad(jnp.concatenate([w_rc[0], w_rf[0]], axis=1).T, ((0, pad_r), (0, 0)))
    b_r = jnp.pad(jnp.concatenate([b_rc[0], b_rf[0]], axis=0), (0, pad_r)).reshape(ROUTER_ROWS, 1)
    x1, h2ext, bid = _outproj(xp, xs, y_ml_p, y_ml_s, z_p, z_s, mods3, gn, w_out[0].astype(BF16), w_r, b_r)

    pos3, meta = _route(bid)
    pos = pos3.reshape(T_ALL)
    y_sorted = _moe(meta, pos, h2ext, w_gate[0], w_up[0], w_down[0])
    y_p, y_s = _final(pos, y_sorted, x1, mods3, gn)

    new_c = c_new.reshape(BATCH, 1, 2, ML_HEADS, ML_HEAD_DIM, ML_HEAD_DIM)
    new_n = n_new.reshape(BATCH, 1, 2, ML_HEADS, ML_HEAD_DIM)
    new_m = m_new[:, :, 0].reshape(BATCH, 1, 2, ML_HEADS)
    return (y_p.reshape(BATCH, SEQ, D_MODEL), y_s.reshape(DEC_BATCH, DEC_SEQ, D_MODEL), new_c, new_n, new_m)
```

```python
import functools
import math

import jax
import jax.numpy as jnp
from jax import lax
from jax.experimental import pallas as pl
from jax.experimental.pallas import tpu as pltpu

F32 = jnp.float32
BF16 = jnp.bfloat16

D_MODEL = 1024
BATCH = 16
SEQ = 256
DEC_BATCH = 4
DEC_SEQ = 1024
GRID_W = 64
ML_WIDTH = 512
ML_HEADS = 4
ML_HEAD_DIM = 128
HY_WIDTH = 512
HY_ORDER = 2
HY_EMB = 33
HY_BANDS = 16
HY_FILTER_HIDDEN = 64
HY_MOD_SHIFT = 0.05
N_GROUPS = 4
EXPERTS_PER_GROUP = 4
N_EXPERTS = 16
EXPERT_FF = 512
N_MOD = 6
EPS = 1e-6
ML_QKVO_COLS = 4 * ML_WIDTH
ML_GATE_COLS = 4 * ML_HEADS
HY_COLS = 3 * HY_WIDTH
MAIN_COLS = ML_QKVO_COLS + HY_COLS

T_PROMPT = BATCH * SEQ
T_SAMPLE = DEC_BATCH * DEC_SEQ
T_ALL = T_PROMPT + T_SAMPLE
TILE = 256
N_TILES_P = T_PROMPT // TILE
N_TILES = T_ALL // TILE
MOD_ROWS = 8
K_SCALE = ML_HEAD_DIM ** -0.5
VMEM_LIMIT = 56 * 1024 * 1024


def _cparams(sem):
    return pltpu.CompilerParams(dimension_semantics=sem, vmem_limit_bytes=VMEM_LIMIT)


def _split2(x):
    hi = x.astype(BF16)
    lo = (x - hi.astype(F32)).astype(BF16)
    return hi, lo


def _dot(a, b):
    return jnp.dot(a, b, preferred_element_type=F32)


def _dot_nt(a, b):
    return lax.dot_general(a, b, (((1,), (1,)), ((), ())), preferred_element_type=F32)


def _dot_tn(a, b):
    return lax.dot_general(a, b, (((0,), (0,)), ((), ())), preferred_element_type=F32)


def _dot3(a, b):
    ah, al = _split2(a)
    bh, bl = _split2(b)
    return _dot(ah, bh) + _dot(al, bh) + _dot(ah, bl)


def _dot3_nt(a, b):
    ah, al = _split2(a)
    bh, bl = _split2(b)
    return _dot_nt(ah, bh) + _dot_nt(al, bh) + _dot_nt(ah, bl)


def _dot_exact_lhs(t, x):
    x1 = x.astype(BF16)
    r1 = x - x1.astype(F32)
    x2 = r1.astype(BF16)
    x3 = (r1 - x2.astype(F32)).astype(BF16)
    return _dot(t, x1) + _dot(t, x2) + _dot(t, x3)


def _dot_exact_rhs(x, t):
    x1 = x.astype(BF16)
    r1 = x - x1.astype(F32)
    x2 = r1.astype(BF16)
    x3 = (r1 - x2.astype(F32)).astype(BF16)
    return _dot(x1, t) + _dot(x2, t) + _dot(x3, t)


def _rms(x, g):
    return x * lax.rsqrt(jnp.mean(x * x, axis=-1, keepdims=True) + EPS) * g


def _mod_row_of_tile(i, tiles_per_sample_seq, n_prompt_tiles):
    return jnp.where(i < n_prompt_tiles, 0, 1 + (i - n_prompt_tiles) // tiles_per_sample_seq)


def _ada_kernel(cv_ref, w_ref, b_ref, o_ref):
    cv = cv_ref[...]
    s = cv * jax.nn.sigmoid(cv)
    o_ref[...] = _dot3(s, w_ref[...]) + b_ref[...]


def _ada(cv, w_ada, b_ada):
    n = N_MOD * D_MODEL
    return pl.pallas_call(
        _ada_kernel,
        out_shape=jax.ShapeDtypeStruct((MOD_ROWS, n), F32),
        grid=(N_MOD,),
        in_specs=[pl.BlockSpec((MOD_ROWS, D_MODEL), lambda j: (0, 0)),
                  pl.BlockSpec((D_MODEL, D_MODEL), lambda j: (0, j)),
                  pl.BlockSpec((1, D_MODEL), lambda j: (0, j))],
        out_specs=pl.BlockSpec((MOD_ROWS, D_MODEL), lambda j: (0, j)),
        compiler_params=_cparams(("arbitrary",)),
        name="ada_mod",
    )(cv, w_ada, b_ada.reshape(1, n))


def _log_sigmoid(x):
    return jnp.minimum(x, 0.0) - jnp.log1p(jnp.exp(-jnp.abs(x)))


def _rows_to_cols(rows):
    ri = lax.broadcasted_iota(jnp.int32, (TILE, TILE), 0)
    ci = lax.broadcasted_iota(jnp.int32, (TILE, TILE), 1)
    eye = jnp.where(ri == ci, 1.0, 0.0).astype(BF16)
    p1 = rows.astype(BF16)
    r1 = rows - p1.astype(F32)
    p2 = r1.astype(BF16)
    p3 = (r1 - p2.astype(F32)).astype(BF16)
    return _dot_nt(eye, p1) + _dot_nt(eye, p2) + _dot_nt(eye, p3)


def _inproj_kernel(xp_ref, xs_ref, m_ref, gn_ref, w_ref, wgt_ref, gbt_ref, proj_ref, gate_ref, gatet_ref):
    i = pl.program_id(0)
    x = jnp.where(i < N_TILES_P, xp_ref[...], xs_ref[...])
    h = _rms(x, gn_ref[0:1, :]) * (1.0 + m_ref[0, 1:2, :]) + m_ref[0, 0:1, :]
    hb = h.astype(BF16)
    cb = 512
    for j in range(MAIN_COLS // cb):
        proj_ref[:, j * cb:(j + 1) * cb] = _dot(hb, w_ref[:, j * cb:(j + 1) * cb]).astype(BF16)
    hl = (h - hb.astype(F32)).astype(BF16)
    wth, wtl = _split2(wgt_ref[...])
    gt = _dot_nt(wth, hb) + _dot_nt(wth, hl) + _dot_nt(wtl, hb) + gbt_ref[...]
    row = lax.broadcasted_iota(jnp.int32, gt.shape, 0)
    gt = jnp.where((row % 8) >= 4, _log_sigmoid(gt), gt)
    gatet_ref[0] = gt
    gate_ref[...] = _rows_to_cols(gt)


def _inproj(xp, xs, mods3, g_norm, w_main, wgt, gbt):
    tps = DEC_SEQ // TILE
    return pl.pallas_call(
        _inproj_kernel,
        out_shape=(jax.ShapeDtypeStruct((T_ALL, MAIN_COLS), BF16),
                   jax.ShapeDtypeStruct((T_ALL, ML_GATE_COLS), F32),
                   jax.ShapeDtypeStruct((N_TILES, ML_GATE_COLS, TILE), F32)),
        grid=(N_TILES,),
        in_specs=[pl.BlockSpec((TILE, D_MODEL), lambda i: (jnp.minimum(i, N_TILES_P - 1), 0)),
                  pl.BlockSpec((TILE, D_MODEL), lambda i: (jnp.maximum(i - N_TILES_P, 0), 0)),
                  pl.BlockSpec((1, N_MOD, D_MODEL), lambda i: (_mod_row_of_tile(i, tps, N_TILES_P), 0, 0)),
                  pl.BlockSpec((4, D_MODEL), lambda i: (0, 0)),
                  pl.BlockSpec((D_MODEL, MAIN_COLS), lambda i: (0, 0)),
                  pl.BlockSpec((ML_GATE_COLS, D_MODEL), lambda i: (0, 0)),
                  pl.BlockSpec((ML_GATE_COLS, 1), lambda i: (0, 0))],
        out_specs=(pl.BlockSpec((TILE, MAIN_COLS), lambda i: (i, 0)),
                   pl.BlockSpec((TILE, ML_GATE_COLS), lambda i: (i, 0)),
                   pl.BlockSpec((1, ML_GATE_COLS, TILE), lambda i: (i, 0, 0))),
        compiler_params=_cparams(("arbitrary",)),
        name="in_proj",
    )(xp, xs, mods3, g_norm, w_main, wgt, gbt)


ST_ROWS = ML_HEAD_DIM + 16


def _mlstm_kernel(*refs, seq_len, has_state):
    if has_state:
        (q_ref, k_ref, v_ref, o_ref, g_ref, gt_ref, gain_ref, c0_ref, n0_ref, m0_ref,
         y_ref, c_ref, n_ref, m_ref, vt_ref, hf_ref, hb_ref, st_ref, ms_ref) = refs
    else:
        (q_ref, k_ref, v_ref, o_ref, g_ref, gt_ref, gain_ref,
         y_ref, c_ref, n_ref, m_ref, vt_ref, hf_ref, hb_ref, st_ref, ms_ref) = refs
    ch = TILE
    nc = seq_len // ch
    hd = ML_HEAD_DIM
    key = lax.broadcasted_iota(jnp.int32, (ch, ch), 0)
    qry = lax.broadcasted_iota(jnp.int32, (ch, ch), 1)
    key_le = key <= qry
    key_ge = key >= qry
    t_le = jnp.where(key_le, 1.0, 0.0).astype(BF16)
    t_ge = jnp.where(key_ge, 1.0, 0.0).astype(BF16)
    sub16 = lax.broadcasted_iota(jnp.int32, (16, ch), 0)
    ln_scale = math.log(K_SCALE)

    for c in range(nc):
        for h in range(ML_HEADS):
            cols = slice(h * hd, (h + 1) * hd)
            vt_ref[c, cols, :] = v_ref[c * ch:(c + 1) * ch, cols].astype(F32).T.astype(BF16)

    for d in range(2):
        for h in range(ML_HEADS):
            r = d * ML_HEADS + h
            st_ref[r] = jnp.zeros((ST_ROWS, hd), F32)
            if has_state:
                st_ref[r, 0:hd, :] = c0_ref[0, d, h].T
                st_ref[r, hd:hd + 1, :] = n0_ref[0, d, h:h + 1, :]
                ms_ref[r] = jnp.broadcast_to(m0_ref[0, r:r + 1, :], (1, ch))
            else:
                ms_ref[r] = jnp.zeros((1, ch), F32)

    def step(t, carry):
        for d in range(2):
            c = t if d == 0 else nc - 1 - t
            rows = pl.ds(pl.multiple_of(c * ch, ch), ch)
            gcol = g_ref[rows, :]
            grow = gt_ref[c]
            brow_all = _dot_exact_rhs(grow, t_le if d == 0 else t_ge)
            bcol_all = _dot_exact_lhs(t_ge if d == 0 else t_le, gcol)
            mask = key_le if d == 0 else key_ge
            hacc_ref = hf_ref if d == 0 else hb_ref
            for h in range(ML_HEADS):
                r = d * ML_HEADS + h
                fcol = (1 + 2 * d) * ML_HEADS + h
                icol = (2 * d) * ML_HEADS + h
                cols = slice(h * hd, (h + 1) * hd)
                q = q_ref[rows, cols]
                k = k_ref[rows, cols]
                vt = vt_ref[c, cols, :]
                st = st_ref[r]
                m_prev = ms_ref[r]
                b_row = brow_all[fcol:fcol + 1, :]
                ig_row = grow[icol:icol + 1, :]
                c_col = gcol[:, icol:icol + 1] - bcol_all[:, fcol:fcol + 1]
                logd = jnp.where(mask, b_row + c_col, -jnp.inf)
                inter = b_row + m_prev
                m_pos = jnp.maximum(inter, jnp.max(logd, axis=0, keepdims=True))
                s = _dot_nt(k, q) * jnp.exp(logd - (m_pos - ln_scale))
                sc_inter = jnp.exp(inter - m_pos)
                iq = _dot_nt(st.astype(BF16), q)
                num = sc_inter * iq[0:hd] + _dot(vt, s.astype(BF16))
                den = sc_inter * iq[hd:hd + 1] + jnp.sum(s, axis=0, keepdims=True)
                hacc_ref[c, cols, :] = num * (1.0 / jnp.maximum(jnp.abs(den), jnp.exp(-m_pos)))
                b_last = b_row[:, ch - 1:ch] if d == 0 else b_row[:, 0:1]
                logw = b_last - b_row + ig_row
                m_new = jnp.maximum(b_last + m_prev, jnp.max(logw, axis=1, keepdims=True))
                w = jnp.exp(logw - (m_new - ln_scale))
                decay = jnp.exp(b_last + m_prev - m_new)
                lhs = jnp.concatenate([(vt.astype(F32) * w).astype(BF16),
                                       jnp.where(sub16 == 0, w, 0.0).astype(BF16)], axis=0)
                st_ref[r] = decay[:, 0:hd] * st + _dot(lhs, k)
                ms_ref[r] = m_new
        return carry

    lax.fori_loop(0, nc, step, 0)

    for d in range(2):
        for h in range(ML_HEADS):
            r = d * ML_HEADS + h
            c_ref[0, d, h] = st_ref[r, 0:hd, :].T
            n_ref[0, d, h:h + 1, :] = st_ref[r, hd:hd + 1, :]
            m_ref[0, r:r + 1, :] = ms_ref[r][:, 0:hd]
    for c in range(nc):
        for h in range(ML_HEADS):
            cols = slice(h * hd, (h + 1) * hd)
            ht = hf_ref[c, cols, :] + hb_ref[c, cols, :]
            ht = ht * lax.rsqrt(jnp.mean(ht * ht, axis=0, keepdims=True) + EPS)
            rows = slice(c * ch, (c + 1) * ch)
            y = ht.T * gain_ref[:, cols] * jax.nn.sigmoid(o_ref[rows, cols].astype(F32))
            y_ref[rows, cols] = y.astype(BF16)


def _mlstm_kernel_rows(*refs, seq_len, has_state):
    if has_state:
        (q_ref, k_ref, v_ref, o_ref, g_ref, gt_ref, gain_ref, c0_ref, n0_ref, m0_ref,
         y_ref, c_ref, n_ref, m_ref, hf_ref, hb_ref, cs_ref, ns_ref, ms_ref) = refs
    else:
        (q_ref, k_ref, v_ref, o_ref, g_ref, gt_ref, gain_ref,
         y_ref, c_ref, n_ref, m_ref, hf_ref, hb_ref, cs_ref, ns_ref, ms_ref) = refs
    ch = TILE
    nc = seq_len // ch
    hd = ML_HEAD_DIM
    ri = lax.broadcasted_iota(jnp.int32, (ch, ch), 0)
    ci = lax.broadcasted_iota(jnp.int32, (ch, ch), 1)
    lower = ci <= ri
    upper = ci >= ri
    t_low = jnp.where(lower, 1.0, 0.0).astype(BF16)
    t_up = jnp.where(upper, 1.0, 0.0).astype(BF16)

    for d in range(2):
        for h in range(ML_HEADS):
            r = d * ML_HEADS + h
            if has_state:
                cs_ref[r] = c0_ref[0, d, h]
                ns_ref[r] = n0_ref[0, d, h:h + 1, :]
                ms_ref[r] = jnp.broadcast_to(m0_ref[0, r:r + 1, :], (1, hd))
            else:
                cs_ref[r] = jnp.zeros((hd, hd), F32)
                ns_ref[r] = jnp.zeros((1, hd), F32)
                ms_ref[r] = jnp.zeros((1, hd), F32)

    def step(t, carry):
        for d in range(2):
            c = t if d == 0 else nc - 1 - t
            r0 = pl.multiple_of(c * ch, ch)
            rows = pl.ds(r0, ch)
            gcol = g_ref[rows, :]
            grow = gt_ref[c]
            tmat_c = t_low if d == 0 else t_up
            tmat_r = t_up if d == 0 else t_low
            bcol_all = _dot_exact_lhs(tmat_c, gcol)
            brow_all = _dot_exact_rhs(grow, tmat_r)
            mask = lower if d == 0 else upper
            hacc_ref = hf_ref if d == 0 else hb_ref
            for h in range(ML_HEADS):
                r = d * ML_HEADS + h
                fcol = (1 + 2 * d) * ML_HEADS + h
                icol = (2 * d) * ML_HEADS + h
                cols = slice(h * hd, (h + 1) * hd)
                q = q_ref[rows, cols]
                k = k_ref[rows, cols]
                v = v_ref[rows, cols]
                c_prev = cs_ref[r]
                n_prev = ns_ref[r]
                m_prev = ms_ref[r][:, 0:1]
                b_col = bcol_all[:, fcol:fcol + 1]
                b_row = brow_all[fcol:fcol + 1, :]
                ig_row = grow[icol:icol + 1, :]
                ig_col = gcol[:, icol:icol + 1]
                logd = jnp.where(mask, b_col - b_row + ig_row, -jnp.inf)
                inter = b_col + m_prev
                m_pos = jnp.maximum(inter, jnp.max(logd, axis=-1, keepdims=True))
                s = _dot_nt(q, k) * K_SCALE * jnp.exp(logd - m_pos)
                sc_inter = jnp.exp(inter - m_pos)
                qf = q.astype(F32)
                num = sc_inter * _dot(q, c_prev.astype(BF16)) + _dot(s.astype(BF16), v)
                den = (sc_inter * jnp.sum(qf * n_prev, axis=-1, keepdims=True)
                       + jnp.sum(s, axis=-1, keepdims=True))
                hh = num / jnp.maximum(jnp.abs(den), jnp.exp(-m_pos))
                hacc_ref[rows, cols] = hh
                b_last = b_col[ch - 1:ch, :] if d == 0 else b_col[0:1, :]
                logw = b_last - b_col + ig_col
                m_new = jnp.maximum(b_last + m_prev, jnp.max(logw, axis=0, keepdims=True))
                w = jnp.exp(logw - m_new)
                decay = jnp.exp(b_last + m_prev - m_new)
                kw = k.astype(F32) * (w * K_SCALE)
                cs_ref[r] = decay * c_prev + _dot_tn(kw.astype(BF16), v)
                ns_ref[r] = decay * n_prev + jnp.sum(kw, axis=0, keepdims=True)
                ms_ref[r] = jnp.broadcast_to(m_new, (1, hd))
        return carry

    lax.fori_loop(0, nc, step, 0)

    for d in range(2):
        for h in range(ML_HEADS):
            r = d * ML_HEADS + h
            c_ref[0, d, h] = cs_ref[r]
            n_ref[0, d, h:h + 1, :] = ns_ref[r]
            m_ref[0, r:r + 1, :] = ms_ref[r]
    for h in range(ML_HEADS):
        cols = slice(h * hd, (h + 1) * hd)
        hh = hf_ref[:, cols] + hb_ref[:, cols]
        hh = hh * lax.rsqrt(jnp.mean(hh * hh, axis=-1, keepdims=True) + EPS)
        y = hh * gain_ref[:, cols] * jax.nn.sigmoid(o_ref[:, cols].astype(F32))
        y_ref[:, cols] = y.astype(BF16)


def _mlstm(proj, gates, gates_t, gain, state, seq_len, n_seq, row_block_off):
    has_state = state is not None
    tiles = seq_len // TILE
    off = row_block_off
    qkvo_specs = [pl.BlockSpec((seq_len, ML_WIDTH), functools.partial(lambda b, j: (off + b, j), j=j))
                  for j in range(4)]
    in_specs = qkvo_specs + [
        pl.BlockSpec((seq_len, ML_GATE_COLS), lambda b: (off + b, 0)),
        pl.BlockSpec((tiles, ML_GATE_COLS, TILE), lambda b: (off + b, 0, 0)),
        pl.BlockSpec((1, ML_WIDTH), lambda b: (0, 0)),
    ]
    args = [proj, proj, proj, proj, gates, gates_t, gain]
    if has_state:
        c0, n0, m0 = state
        in_specs += [
            pl.BlockSpec((1, 2, ML_HEADS, ML_HEAD_DIM, ML_HEAD_DIM), lambda b: (b, 0, 0, 0, 0)),
            pl.BlockSpec((1, 2, ML_HEADS, ML_HEAD_DIM), lambda b: (b, 0, 0, 0)),
            pl.BlockSpec((1, 2 * ML_HEADS, 1), lambda b: (b, 0, 0)),
        ]
        args += [c0, n0, m0]
    out_shape = (jax.ShapeDtypeStruct((n_seq * seq_len, ML_WIDTH), BF16),
                 jax.ShapeDtypeStruct((n_seq, 2, ML_HEADS, ML_HEAD_DIM, ML_HEAD_DIM), F32),
                 jax.ShapeDtypeStruct((n_seq, 2, ML_HEADS, ML_HEAD_DIM), F32),
                 jax.ShapeDtypeStruct((n_seq, 2 * ML_HEADS, ML_HEAD_DIM), F32))
    out_specs = (pl.BlockSpec((seq_len, ML_WIDTH), lambda b: (b, 0)),
                 pl.BlockSpec((1, 2, ML_HEADS, ML_HEAD_DIM, ML_HEAD_DIM), lambda b: (b, 0, 0, 0, 0)),
                 pl.BlockSpec((1, 2, ML_HEADS, ML_HEAD_DIM), lambda b: (b, 0, 0, 0)),
                 pl.BlockSpec((1, 2 * ML_HEADS, ML_HEAD_DIM), lambda b: (b, 0, 0)))
    scratch = [pltpu.VMEM((tiles, ML_WIDTH, TILE), BF16),
               pltpu.VMEM((tiles, ML_WIDTH, TILE), F32), pltpu.VMEM((tiles, ML_WIDTH, TILE), F32),
               pltpu.VMEM((2 * ML_HEADS, ST_ROWS, ML_HEAD_DIM), F32),
               pltpu.VMEM((2 * ML_HEADS, 1, TILE), F32)]
    return pl.pallas_call(
        functools.partial(_mlstm_kernel, seq_len=seq_len, has_state=has_state),
        out_shape=out_shape, grid=(n_seq,), in_specs=in_specs, out_specs=out_specs,
        scratch_shapes=scratch, compiler_params=_cparams(("arbitrary",)),
        name=f"mlstm_{seq_len}",
    )(*args)


def _dft_mats(seq_len):
    lo = 32
    hi = seq_len // lo
    d = jnp.arange(seq_len, dtype=jnp.int32)
    a = jnp.arange(hi, dtype=jnp.int32) * lo
    b = jnp.arange(lo, dtype=jnp.int32)
    scale = math.pi / seq_len
    ang_a = ((a[:, None] * d[None, :]) % (2 * seq_len)).astype(F32) * scale
    ang_b = ((b[:, None] * d[None, :]) % (2 * seq_len)).astype(F32) * scale
    ca, sa, cb, sb = jnp.cos(ang_a), jnp.sin(ang_a), jnp.cos(ang_b), jnp.sin(ang_b)
    cosm = (ca[:, None, :] * cb[None, :, :] - sa[:, None, :] * sb[None, :, :]).reshape(seq_len, seq_len)
    sinm = (sa[:, None, :] * cb[None, :, :] + ca[:, None, :] * sb[None, :, :]).reshape(seq_len, seq_len)
    nyq = jnp.where(d % 2 == 0, 1.0, -1.0).astype(F32)
    krow = jnp.arange(seq_len, dtype=jnp.int32)[:, None]
    sinm = jnp.where(krow == 0, nyq[None, :], sinm)
    f = jnp.concatenate([cosm, sinm], axis=0).astype(BF16)
    cos_t = (ca.T[:, :, None] * cb.T[:, None, :] - sa.T[:, :, None] * sb.T[:, None, :]).reshape(seq_len, seq_len)
    sin_t = (sa.T[:, :, None] * cb.T[:, None, :] + ca.T[:, :, None] * sb.T[:, None, :]).reshape(seq_len, seq_len)
    kcol = jnp.arange(seq_len, dtype=jnp.int32)[None, :]
    sin_t = jnp.where(kcol == 0, nyq[:, None], sin_t)
    ft = jnp.concatenate([cos_t, sin_t], axis=1).astype(BF16)
    return f, ft


def _filter_feats(seq_len):
    t = jnp.linspace(0.0, 1.0, seq_len, dtype=F32)[:, None]
    wpos = 2.0 * math.pi * jnp.arange(seq_len, dtype=F32)[:, None] / seq_len
    bands = jnp.linspace(1e-4, HY_BANDS - 1, HY_BANDS, dtype=F32)[None, :]
    z = jnp.concatenate([t, jnp.cos(bands * wpos), -jnp.sin(bands * wpos)], axis=-1)
    return jnp.pad(z, ((0, 0), (0, 128 - HY_EMB)))


def _filter_kernel(z_ref, w1_ref, b1_ref, w2_ref, b2_ref, w3_ref, b3_ref, dec_ref, f_ref,
                   a_ref, b_ref, d_ref, *, seq_len):
    n = 2 * seq_len
    z = z_ref[...]
    h = jnp.sin(_dot3(z, w1_ref[...]) + b1_ref[...])
    h = jnp.sin(_dot3(h, w2_ref[...]) + b2_ref[...])
    h = _dot3(h, w3_ref[...]) + b3_ref[...]
    t = z[:, 0:1]
    h = h * (jnp.exp(-t * jnp.abs(dec_ref[...])) + HY_MOD_SHIFT)
    ss = jnp.sum(h * h, axis=0, keepdims=True)
    inv = lax.rsqrt(ss[:, :HY_WIDTH] + ss[:, HY_WIDTH:] + EPS)
    hp = h[:, :HY_WIDTH] * inv
    hn = h[:, HY_WIDTH:] * inv
    ssum = hp + hn
    sdif = hp - hn
    hc = _dot(f_ref[0:seq_len, :], ssum.astype(BF16))
    hs = _dot(f_ref[seq_len:n, :], sdif.astype(BF16))
    di = lax.broadcasted_iota(jnp.int32, (seq_len, 1), 0)
    sgn = jnp.where(di % 2 == 0, 1.0, -1.0)
    nyq = jnp.sum(ssum * sgn, axis=0, keepdims=True)
    first = di == 0
    a_ref[0] = hc * jnp.where(first, 1.0 / n, 2.0 / n)
    b_ref[0] = jnp.where(first, 0.0, hs * (2.0 / n))
    d_ref[0] = jnp.where(first, nyq * (1.0 / n), hc * (2.0 / n))


def _hyena_filters(seq_len, f, w1p, b1, w2, b2, w3, b3, dec):
    z = _filter_feats(seq_len)
    hid = HY_FILTER_HIDDEN
    oc = 2 * HY_WIDTH
    full = lambda shape: pl.BlockSpec(shape, lambda o: tuple(0 for _ in shape))
    out = jax.ShapeDtypeStruct((HY_ORDER, seq_len, HY_WIDTH), F32)
    return pl.pallas_call(
        functools.partial(_filter_kernel, seq_len=seq_len),
        out_shape=(out, out, out),
        grid=(HY_ORDER,),
        in_specs=[full((seq_len, 128)), full((128, hid)), full((1, hid)), full((hid, hid)), full((1, hid)),
                  pl.BlockSpec((hid, oc), lambda o: (0, o)),
                  pl.BlockSpec((1, oc), lambda o: (0, o)),
                  pl.BlockSpec((1, oc), lambda o: (0, o)),
                  full((2 * seq_len, seq_len))],
        out_specs=tuple(pl.BlockSpec((1, seq_len, HY_WIDTH), lambda o: (o, 0, 0)) for _ in range(3)),
        compiler_params=_cparams(("arbitrary",)),
        name=f"hyena_filter_{seq_len}",
    )(z, w1p, b1, w2, b2, w3, b3, dec, f)


def _hyena_kernel(x1_ref, x2_ref, v_ref, cw1_ref, cw2_ref, cwv_ref, a_ref, b_ref, d_ref, bias_ref,
                  f_ref, ft_ref, z_ref, *, seq_len, width):
    ti = lax.broadcasted_iota(jnp.int32, (seq_len, 1), 0)
    has_prev = (ti % width) != 0
    has_next = (ti % width) != (width - 1)

    def short_conv(x_ref, w_ref):
        x = x_ref[...].astype(F32)
        prev = jnp.where(has_prev, pltpu.roll(x, 1, axis=0), 0.0)
        nxt = jnp.where(has_next, pltpu.roll(x, seq_len - 1, axis=0), 0.0)
        return w_ref[0:1, :] * prev + w_ref[1:2, :] * x + w_ref[2:3, :] * nxt

    gates = (short_conv(x1_ref, cw1_ref), short_conv(x2_ref, cw2_ref))
    z = short_conv(v_ref, cwv_ref)
    for o in range(HY_ORDER):
        u = _dot(f_ref[...], z.astype(BF16))
        ut = u[:seq_len]
        ub = u[seq_len:]
        a, b, dd = a_ref[o], b_ref[o], d_ref[o]
        yt = ut * a - ub * b
        yb = ut * b + ub * dd
        y = _dot(ft_ref[:, :seq_len], yt.astype(BF16)) + _dot(ft_ref[:, seq_len:], yb.astype(BF16))
        z = gates[o] * (y + bias_ref[o:o + 1, :] * z)
    z_ref[...] = z.astype(BF16)


def _hyena(proj, conv_w, coefs, hy_bias, f, ft, seq_len, n_seq, row_block_off, width):
    cb = 256
    nblk = HY_WIDTH // cb
    base = ML_QKVO_COLS // cb
    off = row_block_off
    a, b, d = coefs

    def col_spec(part):
        return pl.BlockSpec((seq_len, cb), lambda s, j: (off + s, base + part * nblk + j))

    def w_spec(part):
        return pl.BlockSpec((3, cb), lambda s, j: (0, part * nblk + j))

    coef_spec = pl.BlockSpec((HY_ORDER, seq_len, cb), lambda s, j: (0, 0, j))
    return pl.pallas_call(
        functools.partial(_hyena_kernel, seq_len=seq_len, width=width),
        out_shape=jax.ShapeDtypeStruct((n_seq * seq_len, HY_WIDTH), BF16),
        grid=(n_seq, nblk),
        in_specs=[col_spec(0), col_spec(1), col_spec(2), w_spec(0), w_spec(1), w_spec(2),
                  coef_spec, coef_spec, coef_spec,
                  pl.BlockSpec((HY_ORDER, cb), lambda s, j: (0, j)),
                  pl.BlockSpec((2 * seq_len, seq_len), lambda s, j: (0, 0)),
                  pl.BlockSpec((seq_len, 2 * seq_len), lambda s, j: (0, 0))],
        out_specs=pl.BlockSpec((seq_len, cb), lambda s, j: (s, j)),
        compiler_params=_cparams(("arbitrary", "arbitrary")),
        name=f"hyena_conv_{seq_len}",
    )(proj, proj, proj, conv_w, conv_w, conv_w, a, b, d, hy_bias, f, ft)


def _first_max(x, n):
    mx = jnp.max(x, axis=0, keepdims=True)
    row = lax.broadcasted_iota(jnp.int32, x.shape, 0).astype(F32)
    idx = jnp.min(jnp.where(x == mx, row, float(n)), axis=0, keepdims=True)
    return mx, idx.astype(jnp.int32)


ROUTER_ROWS = 32
PAIRS_PER_GROUP = 6
N_BUCKETS = N_GROUPS * PAIRS_PER_GROUP
PAIR_SLOTS = ((0, 1), (0, 2), (0, 3), (1, 3), (1, 2), (3, 2))
LANES = 128
FEAT_SLABS = D_MODEL // LANES
H2_SLABS = FEAT_SLABS + 1
ROW_TILE = 256
ROW_CAP = T_ALL + N_BUCKETS * ROW_TILE
N_ROW_TILES = ROW_CAP // ROW_TILE


def _outproj_kernel(xp_ref, xs_ref, yp_ref, ys_ref, zp_ref, zs_ref, m_ref, gn_ref, wo_ref, wr_ref, br_ref,
                    x1_ref, h2_ref, bid_ref):
    i = pl.program_id(0)
    is_p = i < N_TILES_P
    x = jnp.where(is_p, xp_ref[...], xs_ref[...])
    yml = jnp.where(is_p, yp_ref[...], ys_ref[...])
    zz = jnp.where(is_p, zp_ref[...], zs_ref[...])
    y = _dot(yml, wo_ref[0:ML_WIDTH, :]) + _dot(zz, wo_ref[ML_WIDTH:, :])
    x1 = x + m_ref[0, 2:3, :] * _rms(y, gn_ref[1:2, :])
    x1_ref[...] = x1
    h2 = _rms(x1, gn_ref[2:3, :]) * (1.0 + m_ref[0, 4:5, :]) + m_ref[0, 3:4, :]
    for s in range(FEAT_SLABS):
        h2_ref[:, s, :] = h2[:, s * LANES:(s + 1) * LANES]
    h2h, h2l = _split2(h2)
    wrh, wrl = _split2(wr_ref[...])
    logits = _dot_nt(wrh, h2h) + _dot_nt(wrh, h2l) + _dot_nt(wrl, h2h) + br_ref[...]
    lc = logits[0:N_GROUPS]
    mx, gi = _first_max(lc, N_GROUPS)
    p_grp = 1.0 / jnp.sum(jnp.exp(lc - mx), axis=0, keepdims=True)
    lsel = jnp.zeros((EXPERTS_PER_GROUP, TILE), F32)
    for g in range(N_GROUPS):
        lo = N_GROUPS + g * EXPERTS_PER_GROUP
        lsel = jnp.where(gi == g, logits[lo:lo + EXPERTS_PER_GROUP], lsel)
    l1, i1 = _first_max(lsel, EXPERTS_PER_GROUP)
    sub4 = lax.broadcasted_iota(jnp.int32, lsel.shape, 0)
    l2, i2 = _first_max(jnp.where(sub4 == i1, -jnp.inf, lsel), EXPERTS_PER_GROUP)
    e2 = jnp.exp(l2 - l1)
    w1 = p_grp / (1.0 + e2)
    w2 = p_grp * e2 / (1.0 + e2)
    lo_e = jnp.minimum(i1, i2)
    hi_e = jnp.maximum(i1, i2)
    pair = jnp.where(lo_e == 0, hi_e - 1, jnp.where(lo_e == 1, jnp.where(hi_e == 3, 3, 4), 5))
    slot_a = jnp.where(pair == 5, hi_e, lo_e)
    first_in_a = i1 == slot_a
    w_a = jnp.where(first_in_a, w1, w2)
    w_b = jnp.where(first_in_a, w2, w1)
    sub = lax.broadcasted_iota(jnp.int32, (8, TILE), 0)
    gate_rows = jnp.where(sub == 0, w_a, jnp.where(sub == 1, w_b, 0.0))
    h2_ref[:, FEAT_SLABS, :] = jnp.zeros((TILE, LANES), F32)
    h2_ref[:, FEAT_SLABS, 0:8] = _rows_to_cols(gate_rows)
    bid_ref[0] = gi * PAIRS_PER_GROUP + pair


def _outproj(xp, xs, yp, ys, zp, zs, mods3, g_norm, w_out, w_r, b_r):
    tps = DEC_SEQ // TILE
    pidx = lambda i: (jnp.minimum(i, N_TILES_P - 1), 0)
    sidx = lambda i: (jnp.maximum(i - N_TILES_P, 0), 0)
    return pl.pallas_call(
        _outproj_kernel,
        out_shape=(jax.ShapeDtypeStruct((T_ALL, D_MODEL), F32),
                   jax.ShapeDtypeStruct((T_ALL, H2_SLABS, LANES), F32),
                   jax.ShapeDtypeStruct((N_TILES, 1, TILE), jnp.int32)),
        grid=(N_TILES,),
        in_specs=[pl.BlockSpec((TILE, D_MODEL), pidx), pl.BlockSpec((TILE, D_MODEL), sidx),
                  pl.BlockSpec((TILE, ML_WIDTH), pidx), pl.BlockSpec((TILE, ML_WIDTH), sidx),
                  pl.BlockSpec((TILE, HY_WIDTH), pidx), pl.BlockSpec((TILE, HY_WIDTH), sidx),
                  pl.BlockSpec((1, N_MOD, D_MODEL), lambda i: (_mod_row_of_tile(i, tps, N_TILES_P), 0, 0)),
                  pl.BlockSpec((4, D_MODEL), lambda i: (0, 0)),
                  pl.BlockSpec((D_MODEL, D_MODEL), lambda i: (0, 0)),
                  pl.BlockSpec((ROUTER_ROWS, D_MODEL), lambda i: (0, 0)),
                  pl.BlockSpec((ROUTER_ROWS, 1), lambda i: (0, 0))],
        out_specs=(pl.BlockSpec((TILE, D_MODEL), lambda i: (i, 0)),
                   pl.BlockSpec((TILE, H2_SLABS, LANES), lambda i: (i, 0, 0)),
                   pl.BlockSpec((1, 1, TILE), lambda i: (i, 0, 0))),
        compiler_params=_cparams(("arbitrary",)),
        name="out_proj_router",
    )(xp, xs, yp, ys, zp, zs, mods3, g_norm, w_out, w_r, b_r)


def _route_kernel(bid_ref, pos_ref, meta_ref):
    nb = 32
    tm = float(ROW_TILE)
    sub = lax.broadcasted_iota(jnp.int32, (nb, TILE), 0)
    ri = lax.broadcasted_iota(jnp.int32, (TILE, TILE), 0)
    ci = lax.broadcasted_iota(jnp.int32, (TILE, TILE), 1)
    before = jnp.where(ri < ci, 1.0, 0.0).astype(BF16)

    def onehot(blk):
        return jnp.where(sub == bid_ref[blk], 1.0, 0.0)

    zeros = jnp.zeros((nb, 1), F32)
    cnt = lax.fori_loop(0, N_TILES, lambda blk, c: c + jnp.sum(onehot(blk), axis=1, keepdims=True), zeros)
    padded = jnp.floor((cnt + (tm - 1.0)) * (1.0 / tm)) * tm
    r32 = lax.broadcasted_iota(jnp.int32, (nb, nb), 0)
    c32 = lax.broadcasted_iota(jnp.int32, (nb, nb), 1)
    padded_row = jnp.sum(jnp.where(r32 == c32, padded, 0.0), axis=0, keepdims=True)
    offs = jnp.sum(jnp.where(c32 < r32, padded_row, 0.0), axis=1, keepdims=True)
    ends = offs + padded

    def place(blk, seen):
        oh = onehot(blk)
        rank = _dot(oh.astype(BF16), before)
        pos = jnp.sum(oh * (rank + seen + offs), axis=0, keepdims=True)
        pos_ref[blk] = pos.astype(jnp.int32)
        return seen + jnp.sum(oh, axis=1, keepdims=True)

    lax.fori_loop(0, N_TILES, place, zeros)

    start = lax.broadcasted_iota(jnp.int32, (nb, 128), 1).astype(F32) * tm
    bsub = lax.broadcasted_iota(jnp.int32, (nb, 128), 0)
    done = jnp.where((bsub < N_BUCKETS) & (ends <= start), 1.0, 0.0)
    tb = jnp.sum(done, axis=0, keepdims=True)
    valid = jnp.where(tb < N_BUCKETS, 1.0, 0.0)
    tbc = jnp.minimum(tb, N_BUCKETS - 1.0)
    grp = jnp.floor((tbc + 0.5) * (1.0 / PAIRS_PER_GROUP))
    pair = tbc - PAIRS_PER_GROUP * grp
    loc_a = jnp.zeros_like(pair)
    loc_b = jnp.zeros_like(pair)
    for k, (sa, sb) in enumerate(PAIR_SLOTS):
        loc_a = jnp.where(pair == k, float(sa), loc_a)
        loc_b = jnp.where(pair == k, float(sb), loc_b)
    mine = bsub.astype(F32) == tbc
    used = jnp.sum(jnp.where(mine, offs + cnt, 0.0), axis=0, keepdims=True)
    n_rows = jnp.clip(used - start[0:1], 0.0, tm) * valid
    row8 = lax.broadcasted_iota(jnp.int32, (8, 128), 0)
    meta = jnp.where(row8 == 0, grp * EXPERTS_PER_GROUP + loc_a,
                     jnp.where(row8 == 1, grp * EXPERTS_PER_GROUP + loc_b,
                               jnp.where(row8 == 2, valid, jnp.where(row8 == 3, n_rows, 0.0))))
    meta_ref[...] = meta.astype(jnp.int32)


def _route(bid):
    return pl.pallas_call(
        _route_kernel,
        out_shape=(jax.ShapeDtypeStruct((N_TILES, 1, TILE), jnp.int32),
                   jax.ShapeDtypeStruct((8, 128), jnp.int32)),
        compiler_params=pltpu.CompilerParams(vmem_limit_bytes=VMEM_LIMIT),
        name="moe_route",
    )(bid)


def _moe_kernel(meta_ref, pos_ref, h2_hbm, wga_ref, wua_ref, wda_ref, wgb_ref, wub_ref, wdb_ref,
                y_ref, src_ref, xbuf, sem, wga_s, wua_s, wda_s, wgb_s, wub_s, wdb_s):
    j = pl.program_id(0)

    def row_copy(tile, r, slot):
        tok = src_ref[tile * ROW_TILE + r]
        return pltpu.make_async_copy(h2_hbm.at[tok], xbuf.at[slot, r], sem.at[slot])

    group = 8

    def row_groups(tile):
        return (meta_ref[3, tile] + (group - 1)) // group

    def issue(tile, slot):
        def body(g, c):
            for k in range(group):
                row_copy(tile, g * group + k, slot).start()
            return c
        lax.fori_loop(0, row_groups(tile), body, 0)

    def wait(tile, slot):
        def body(g, c):
            for k in range(group):
                row_copy(tile, g * group + k, slot).wait()
            return c
        lax.fori_loop(0, row_groups(tile), body, 0)

    @pl.when(j == 0)
    def _():
        xbuf[...] = jnp.zeros_like(xbuf)

        def clear(p, c):
            src_ref[p] = 0
            return c
        lax.fori_loop(0, ROW_CAP, clear, 0, unroll=8)

        def invert(t, c):
            src_ref[pos_ref[t]] = t
            return c
        lax.fori_loop(0, T_ALL, invert, 0, unroll=8)

        @pl.when(meta_ref[2, 0] == 1)
        def _():
            issue(0, 0)

    nxt = jnp.minimum(j + 1, N_ROW_TILES - 1)

    @pl.when((j + 1 < N_ROW_TILES) & (meta_ref[2, nxt] == 1))
    def _():
        issue(nxt, nxt % 2)

    valid = meta_ref[2, j] == 1
    prev = jnp.maximum(j - 1, 0)

    @pl.when(valid & ((j == 0) | (meta_ref[0, j] != meta_ref[0, prev])))
    def _():
        wga_s[...] = wga_ref[0].astype(BF16)
        wua_s[...] = wua_ref[0].astype(BF16)
        wda_s[...] = wda_ref[0].astype(BF16)

    @pl.when(valid & ((j == 0) | (meta_ref[1, j] != meta_ref[1, prev])))
    def _():
        wgb_s[...] = wgb_ref[0].astype(BF16)
        wub_s[...] = wub_ref[0].astype(BF16)
        wdb_s[...] = wdb_ref[0].astype(BF16)

    @pl.when(valid)
    def _():
        slot = j % 2
        wait(j, slot)
        x = jnp.concatenate([xbuf[slot, :, s, :] for s in range(FEAT_SLABS)], axis=1).astype(BF16)
        gates = xbuf[slot, :, FEAT_SLABS, :]

        def expert(wg, wu, gate):
            hg = _dot(x, wg[...])
            hu = _dot(x, wu[...])
            return (hg * jax.nn.sigmoid(hg) * hu * gate).astype(BF16)

        act_a = expert(wga_s, wua_s, gates[:, 0:1])
        act_b = expert(wgb_s, wub_s, gates[:, 1:2])
        y = _dot(act_a, wda_s[...]) + _dot(act_b, wdb_s[...])
        for s in range(FEAT_SLABS):
            y_ref[:, s, :] = y[:, s * LANES:(s + 1) * LANES]

    @pl.when(jnp.logical_not(valid))
    def _():
        y_ref[...] = jnp.zeros_like(y_ref)


def _moe(meta, pos, h2ext, w_gate, w_up, w_down):
    up_spec = lambda slot: pl.BlockSpec((1, D_MODEL, EXPERT_FF), lambda j, meta, pos: (meta[slot, j], 0, 0))
    down_spec = lambda slot: pl.BlockSpec((1, EXPERT_FF, D_MODEL), lambda j, meta, pos: (meta[slot, j], 0, 0))
    grid_spec = pltpu.PrefetchScalarGridSpec(
        num_scalar_prefetch=2,
        grid=(N_ROW_TILES,),
        in_specs=[pl.BlockSpec(memory_space=pl.ANY),
                  up_spec(0), up_spec(0), down_spec(0), up_spec(1), up_spec(1), down_spec(1)],
        out_specs=pl.BlockSpec((ROW_TILE, FEAT_SLABS, LANES), lambda j, meta, pos: (j, 0, 0)),
        scratch_shapes=[pltpu.SMEM((ROW_CAP,), jnp.int32),
                        pltpu.VMEM((2, ROW_TILE, H2_SLABS, LANES), F32),
                        pltpu.SemaphoreType.DMA((2,)),
                        pltpu.VMEM((D_MODEL, EXPERT_FF), BF16), pltpu.VMEM((D_MODEL, EXPERT_FF), BF16),
                        pltpu.VMEM((EXPERT_FF, D_MODEL), BF16),
                        pltpu.VMEM((D_MODEL, EXPERT_FF), BF16), pltpu.VMEM((D_MODEL, EXPERT_FF), BF16),
                        pltpu.VMEM((EXPERT_FF, D_MODEL), BF16)])
    return pl.pallas_call(
        _moe_kernel,
        out_shape=jax.ShapeDtypeStruct((ROW_CAP, FEAT_SLABS, LANES), F32),
        grid_spec=grid_spec,
        compiler_params=_cparams(("arbitrary",)),
        name="moe_experts",
    )(meta, pos, h2ext, w_gate, w_up, w_down, w_gate, w_up, w_down)


def _final_kernel(pos_ref, y_hbm, x1_ref, m_ref, gn_ref, op_ref, os_ref, ybuf, sem):
    i = pl.program_id(0)

    def row_copy(tile, r, slot):
        p = pos_ref[tile * TILE + r]
        return pltpu.make_async_copy(y_hbm.at[p], ybuf.at[slot, r], sem.at[slot])

    def issue(tile, slot):
        def body(r2, c):
            row_copy(tile, 2 * r2, slot).start(priority=0)
            row_copy(tile, 2 * r2 + 1, slot).start(priority=1)
            return c
        lax.fori_loop(0, TILE // 2, body, 0, unroll=4)

    def wait(slot):
        pltpu.make_async_copy(y_hbm.at[pl.ds(0, TILE)], ybuf.at[slot], sem.at[slot]).wait()

    @pl.when(i == 0)
    def _():
        issue(0, 0)

    @pl.when(i + 1 < N_TILES)
    def _():
        issue(i + 1, (i + 1) % 2)

    slot = i % 2
    wait(slot)
    y = jnp.concatenate([ybuf[slot, :, s, :] for s in range(FEAT_SLABS)], axis=1)
    out = x1_ref[...] + m_ref[0, 5:6, :] * _rms(y, gn_ref[3:4, :])

    @pl.when(i < N_TILES_P)
    def _():
        op_ref[...] = out

    @pl.when(i >= N_TILES_P)
    def _():
        os_ref[...] = out


def _final(pos, y_sorted, x1, mods3, g_norm):
    tps = DEC_SEQ // TILE
    grid_spec = pltpu.PrefetchScalarGridSpec(
        num_scalar_prefetch=1,
        grid=(N_TILES,),
        in_specs=[pl.BlockSpec(memory_space=pl.ANY),
                  pl.BlockSpec((TILE, D_MODEL), lambda i, pos: (i, 0)),
                  pl.BlockSpec((1, N_MOD, D_MODEL), lambda i, pos: (_mod_row_of_tile(i, tps, N_TILES_P), 0, 0)),
                  pl.BlockSpec((4, D_MODEL), lambda i, pos: (0, 0))],
        out_specs=(pl.BlockSpec((TILE, D_MODEL), lambda i, pos: (jnp.minimum(i, N_TILES_P - 1), 0)),
                   pl.BlockSpec((TILE, D_MODEL), lambda i, pos: (jnp.maximum(i - N_TILES_P, 0), 0))),
        scratch_shapes=[pltpu.VMEM((2, TILE, FEAT_SLABS, LANES), F32), pltpu.SemaphoreType.DMA((2,))])
    return pl.pallas_call(
        _final_kernel,
        out_shape=(jax.ShapeDtypeStruct((T_PROMPT, D_MODEL), F32),
                   jax.ShapeDtypeStruct((T_SAMPLE, D_MODEL), F32)),
        grid_spec=grid_spec,
        compiler_params=_cparams(("arbitrary",)),
        name="moe_combine_final",
    )(pos, y_sorted, x1, mods3, g_norm)


def kernel(x_prompt, x_sample, state_C, state_n, state_m, c, c_ctx, w_ada, b_ada, g_norm, w_in, ml_gate_bias, ml_head_gain, hy_conv_w, hy_f_w1, hy_f_b1, hy_f_w2, hy_f_b2, hy_f_w3, hy_f_b3, hy_decay, hy_bias, w_out, w_rc, b_rc, w_rf, b_rf, w_gate, w_up, w_down):
    xp = x_prompt.reshape(T_PROMPT, D_MODEL)
    xs = x_sample.reshape(T_SAMPLE, D_MODEL)
    gn = g_norm[0]

    cv = jnp.concatenate([c_ctx[None, :], c, jnp.zeros((MOD_ROWS - 1 - DEC_BATCH, D_MODEL), F32)], axis=0)
    mods3 = _ada(cv, w_ada[0], b_ada[0]).reshape(MOD_ROWS, N_MOD, D_MODEL)

    w_in0 = w_in[0]
    w_main = jnp.concatenate([w_in0[:, :ML_QKVO_COLS], w_in0[:, ML_QKVO_COLS + ML_GATE_COLS:]],
                             axis=1).astype(BF16)
    wg = w_in0[:, ML_QKVO_COLS:ML_QKVO_COLS + ML_GATE_COLS]
    gbt = ml_gate_bias[0].reshape(ML_GATE_COLS, 1)
    proj, gates, gates_t = _inproj(xp, xs, mods3, gn, w_main, wg.T, gbt)

    gain = ml_head_gain[0].reshape(1, ML_WIDTH)
    y_ml_p, c_new, n_new, m_new = _mlstm(proj, gates, gates_t, gain, None, SEQ, BATCH, 0)
    state = (state_C[:, 0], state_n[:, 0], state_m[:, 0].reshape(DEC_BATCH, 2 * ML_HEADS, 1))
    y_ml_s, _, _, _ = _mlstm(proj, gates, gates_t, gain, state, DEC_SEQ, DEC_BATCH, T_PROMPT // DEC_SEQ)

    w1p = jnp.pad(hy_f_w1[0], ((0, 128 - HY_EMB), (0, 0)))
    b1 = hy_f_b1[0].reshape(1, -1)
    b2 = hy_f_b2[0].reshape(1, -1)
    b3 = hy_f_b3[0].reshape(1, -1)
    dec = hy_decay[0].reshape(1, -1)
    z_parts = []
    for seq_len, n_seq, off, width in ((SEQ, BATCH, 0, SEQ), (DEC_SEQ, DEC_BATCH, T_PROMPT // DEC_SEQ, GRID_W)):
        f, ft = _dft_mats(seq_len)
        coefs = _hyena_filters(seq_len, f, w1p, b1, hy_f_w2[0], b2, hy_f_w3[0], b3, dec)
        z_parts.append(_hyena(proj, hy_conv_w[0], coefs, hy_bias[0], f, ft, seq_len, n_seq, off, width))
    z_p, z_s = z_parts

    pad_r = ROUTER_ROWS - N_GROUPS - N_EXPERTS
    w_r = jnp.pad(jnp.concatenate([w_rc[0], w_rf[0]], axis=1).T, ((0, pad_r), (0, 0)))
    b_r = jnp.pad(jnp.concatenate([b_rc[0], b_rf[0]], axis=0), (0, pad_r)).reshape(ROUTER_ROWS, 1)
    x1, h2ext, bid = _outproj(xp, xs, y_ml_p, y_ml_s, z_p, z_s, mods3, gn, w_out[0].astype(BF16), w_r, b_r)

    pos3, meta = _route(bid)
    pos = pos3.reshape(T_ALL)
    y_sorted = _moe(meta, pos, h2ext, w_gate[0], w_up[0], w_down[0])
    y_p, y_s = _final(pos, y_sorted, x1, mods3, gn)

    new_c = c_new.reshape(BATCH, 1, 2, ML_HEADS, ML_HEAD_DIM, ML_HEAD_DIM)
    new_n = n_new.reshape(BATCH, 1, 2, ML_HEADS, ML_HEAD_DIM)
    new_m = m_new[:, :, 0].reshape(BATCH, 1, 2, ML_HEADS)
    return (y_p.reshape(BATCH, SEQ, D_MODEL), y_s.reshape(DEC_BATCH, DEC_SEQ, D_MODEL), new_c, new_n, new_m)
```

```python
import functools
import math

import jax
import jax.numpy as jnp
from jax import lax
from jax.experimental import pallas as pl
from jax.experimental.pallas import tpu as pltpu

F32 = jnp.float32
BF16 = jnp.bfloat16

D_MODEL = 1024
BATCH = 16
SEQ = 256
DEC_BATCH = 4
DEC_SEQ = 1024
GRID_W = 64
ML_WIDTH = 512
ML_HEADS = 4
ML_HEAD_DIM = 128
HY_WIDTH = 512
HY_ORDER = 2
HY_EMB = 33
HY_BANDS = 16
HY_FILTER_HIDDEN = 64
HY_MOD_SHIFT = 0.05
N_GROUPS = 4
EXPERTS_PER_GROUP = 4
N_EXPERTS = 16
EXPERT_FF = 512
N_MOD = 6
EPS = 1e-6
ML_QKVO_COLS = 4 * ML_WIDTH
ML_GATE_COLS = 4 * ML_HEADS
HY_COLS = 3 * HY_WIDTH
MAIN_COLS = ML_QKVO_COLS + HY_COLS

T_PROMPT = BATCH * SEQ
T_SAMPLE = DEC_BATCH * DEC_SEQ
T_ALL = T_PROMPT + T_SAMPLE
TILE = 256
N_TILES_P = T_PROMPT // TILE
N_TILES = T_ALL // TILE
MOD_ROWS = 8
K_SCALE = ML_HEAD_DIM ** -0.5
VMEM_LIMIT = 56 * 1024 * 1024


def _cparams(sem):
    return pltpu.CompilerParams(dimension_semantics=sem, vmem_limit_bytes=VMEM_LIMIT)


def _split2(x):
    hi = x.astype(BF16)
    lo = (x - hi.astype(F32)).astype(BF16)
    return hi, lo


def _dot(a, b):
    return jnp.dot(a, b, preferred_element_type=F32)


def _dot_nt(a, b):
    return lax.dot_general(a, b, (((1,), (1,)), ((), ())), preferred_element_type=F32)


def _dot_tn(a, b):
    return lax.dot_general(a, b, (((0,), (0,)), ((), ())), preferred_element_type=F32)


def _dot3(a, b):
    ah, al = _split2(a)
    bh, bl = _split2(b)
    return _dot(ah, bh) + _dot(al, bh) + _dot(ah, bl)


def _dot3_nt(a, b):
    ah, al = _split2(a)
    bh, bl = _split2(b)
    return _dot_nt(ah, bh) + _dot_nt(al, bh) + _dot_nt(ah, bl)


def _dot_exact_lhs(t, x):
    x1 = x.astype(BF16)
    r1 = x - x1.astype(F32)
    x2 = r1.astype(BF16)
    x3 = (r1 - x2.astype(F32)).astype(BF16)
    return _dot(t, x1) + _dot(t, x2) + _dot(t, x3)


def _dot_exact_rhs(x, t):
    x1 = x.astype(BF16)
    r1 = x - x1.astype(F32)
    x2 = r1.astype(BF16)
    x3 = (r1 - x2.astype(F32)).astype(BF16)
    return _dot(x1, t) + _dot(x2, t) + _dot(x3, t)


def _rms(x, g):
    return x * lax.rsqrt(jnp.mean(x * x, axis=-1, keepdims=True) + EPS) * g


def _mod_row_of_tile(i, tiles_per_sample_seq, n_prompt_tiles):
    return jnp.where(i < n_prompt_tiles, 0, 1 + (i - n_prompt_tiles) // tiles_per_sample_seq)


def _ada_kernel(cv_ref, w_ref, b_ref, o_ref):
    cv = cv_ref[...]
    s = cv * jax.nn.sigmoid(cv)
    o_ref[...] = _dot3(s, w_ref[...]) + b_ref[...]


def _ada(cv, w_ada, b_ada):
    n = N_MOD * D_MODEL
    return pl.pallas_call(
        _ada_kernel,
        out_shape=jax.ShapeDtypeStruct((MOD_ROWS, n), F32),
        grid=(N_MOD,),
        in_specs=[pl.BlockSpec((MOD_ROWS, D_MODEL), lambda j: (0, 0)),
                  pl.BlockSpec((D_MODEL, D_MODEL), lambda j: (0, j)),
                  pl.BlockSpec((1, D_MODEL), lambda j: (0, j))],
        out_specs=pl.BlockSpec((MOD_ROWS, D_MODEL), lambda j: (0, j)),
        compiler_params=_cparams(("arbitrary",)),
        name="ada_mod",
    )(cv, w_ada, b_ada.reshape(1, n))


def _log_sigmoid(x):
    return jnp.minimum(x, 0.0) - jnp.log1p(jnp.exp(-jnp.abs(x)))


def _rows_to_cols(rows):
    ri = lax.broadcasted_iota(jnp.int32, (TILE, TILE), 0)
    ci = lax.broadcasted_iota(jnp.int32, (TILE, TILE), 1)
    eye = jnp.where(ri == ci, 1.0, 0.0).astype(BF16)
    p1 = rows.astype(BF16)
    r1 = rows - p1.astype(F32)
    p2 = r1.astype(BF16)
    p3 = (r1 - p2.astype(F32)).astype(BF16)
    return _dot_nt(eye, p1) + _dot_nt(eye, p2) + _dot_nt(eye, p3)


def _inproj_kernel(xp_ref, xs_ref, m_ref, gn_ref, w_ref, wgt_ref, gbt_ref, proj_ref, gate_ref, gatet_ref):
    i = pl.program_id(0)
    x = jnp.where(i < N_TILES_P, xp_ref[...], xs_ref[...])
    h = _rms(x, gn_ref[0:1, :]) * (1.0 + m_ref[0, 1:2, :]) + m_ref[0, 0:1, :]
    hb = h.astype(BF16)
    cb = 512
    for j in range(MAIN_COLS // cb):
        proj_ref[:, j * cb:(j + 1) * cb] = _dot(hb, w_ref[:, j * cb:(j + 1) * cb]).astype(BF16)
    hl = (h - hb.astype(F32)).astype(BF16)
    wth, wtl = _split2(wgt_ref[...])
    gt = _dot_nt(wth, hb) + _dot_nt(wth, hl) + _dot_nt(wtl, hb) + gbt_ref[...]
    row = lax.broadcasted_iota(jnp.int32, gt.shape, 0)
    gt = jnp.where((row % 8) >= 4, _log_sigmoid(gt), gt)
    gatet_ref[0] = gt
    gate_ref[...] = _rows_to_cols(gt)


def _inproj(xp, xs, mods3, g_norm, w_main, wgt, gbt):
    tps = DEC_SEQ // TILE
    return pl.pallas_call(
        _inproj_kernel,
        out_shape=(jax.ShapeDtypeStruct((T_ALL, MAIN_COLS), BF16),
                   jax.ShapeDtypeStruct((T_ALL, ML_GATE_COLS), F32),
                   jax.ShapeDtypeStruct((N_TILES, ML_GATE_COLS, TILE), F32)),
        grid=(N_TILES,),
        in_specs=[pl.BlockSpec((TILE, D_MODEL), lambda i: (jnp.minimum(i, N_TILES_P - 1), 0)),
                  pl.BlockSpec((TILE, D_MODEL), lambda i: (jnp.maximum(i - N_TILES_P, 0), 0)),
                  pl.BlockSpec((1, N_MOD, D_MODEL), lambda i: (_mod_row_of_tile(i, tps, N_TILES_P), 0, 0)),
                  pl.BlockSpec((4, D_MODEL), lambda i: (0, 0)),
                  pl.BlockSpec((D_MODEL, MAIN_COLS), lambda i: (0, 0)),
                  pl.BlockSpec((ML_GATE_COLS, D_MODEL), lambda i: (0, 0)),
                  pl.BlockSpec((ML_GATE_COLS, 1), lambda i: (0, 0))],
        out_specs=(pl.BlockSpec((TILE, MAIN_COLS), lambda i: (i, 0)),
                   pl.BlockSpec((TILE, ML_GATE_COLS), lambda i: (i, 0)),
                   pl.BlockSpec((1, ML_GATE_COLS, TILE), lambda i: (i, 0, 0))),
        compiler_params=_cparams(("arbitrary",)),
        name="in_proj",
    )(xp, xs, mods3, g_norm, w_main, wgt, gbt)


ST_ROWS = ML_HEAD_DIM + 16


def _mlstm_kernel(*refs, seq_len, has_state):
    if has_state:
        (q_ref, k_ref, v_ref, o_ref, g_ref, gt_ref, gain_ref, c0_ref, n0_ref, m0_ref,
         y_ref, c_ref, n_ref, m_ref, vt_ref, hf_ref, hb_ref, st_ref, ms_ref) = refs
    else:
        (q_ref, k_ref, v_ref, o_ref, g_ref, gt_ref, gain_ref,
         y_ref, c_ref, n_ref, m_ref, vt_ref, hf_ref, hb_ref, st_ref, ms_ref) = refs
    ch = TILE
    nc = seq_len // ch
    hd = ML_HEAD_DIM
    key = lax.broadcasted_iota(jnp.int32, (ch, ch), 0)
    qry = lax.broadcasted_iota(jnp.int32, (ch, ch), 1)
    key_le = key <= qry
    key_ge = key >= qry
    t_le = jnp.where(key_le, 1.0, 0.0).astype(BF16)
    t_ge = jnp.where(key_ge, 1.0, 0.0).astype(BF16)
    sub16 = lax.broadcasted_iota(jnp.int32, (16, ch), 0)
    ln_scale = math.log(K_SCALE)

    for c in range(nc):
        for h in range(ML_HEADS):
            cols = slice(h * hd, (h + 1) * hd)
            vt_ref[c, cols, :] = v_ref[c * ch:(c + 1) * ch, cols].astype(F32).T.astype(BF16)

    for d in range(2):
        for h in range(ML_HEADS):
            r = d * ML_HEADS + h
            st_ref[r] = jnp.zeros((ST_ROWS, hd), F32)
            if has_state:
                st_ref[r, 0:hd, :] = c0_ref[0, d, h].T
                st_ref[r, hd:hd + 1, :] = n0_ref[0, d, h:h + 1, :]
                ms_ref[r] = jnp.broadcast_to(m0_ref[0, r:r + 1, :], (1, ch))
            else:
                ms_ref[r] = jnp.zeros((1, ch), F32)

    def step(t, carry):
        for d in range(2):
            c = t if d == 0 else nc - 1 - t
            rows = pl.ds(pl.multiple_of(c * ch, ch), ch)
            gcol = g_ref[rows, :]
            grow = gt_ref[c]
            brow_all = _dot_exact_rhs(grow, t_le if d == 0 else t_ge)
            bcol_all = _dot_exact_lhs(t_ge if d == 0 else t_le, gcol)
            mask = key_le if d == 0 else key_ge
            hacc_ref = hf_ref if d == 0 else hb_ref
            for h in range(ML_HEADS):
                r = d * ML_HEADS + h
                fcol = (1 + 2 * d) * ML_HEADS + h
                icol = (2 * d) * ML_HEADS + h
                cols = slice(h * hd, (h + 1) * hd)
                q = q_ref[rows, cols]
                k = k_ref[rows, cols]
                vt = vt_ref[c, cols, :]
                st = st_ref[r]
                m_prev = ms_ref[r]
                b_row = brow_all[fcol:fcol + 1, :]
                ig_row = grow[icol:icol + 1, :]
                c_col = gcol[:, icol:icol + 1] - bcol_all[:, fcol:fcol + 1]
                logd = jnp.where(mask, b_row + c_col, -jnp.inf)
                inter = b_row + m_prev
                m_pos = jnp.maximum(inter, jnp.max(logd, axis=0, keepdims=True))
                s = _dot_nt(k, q) * jnp.exp(logd - (m_pos - ln_scale))
                sc_inter = jnp.exp(inter - m_pos)
                iq = _dot_nt(st.astype(BF16), q)
                num = sc_inter * iq[0:hd] + _dot(vt, s.astype(BF16))
                den = sc_inter * iq[hd:hd + 1] + jnp.sum(s, axis=0, keepdims=True)
                hacc_ref[c, cols, :] = num * (1.0 / jnp.maximum(jnp.abs(den), jnp.exp(-m_pos)))
                b_last = b_row[:, ch - 1:ch] if d == 0 else b_row[:, 0:1]
                logw = b_last - b_row + ig_row
                m_new = jnp.maximum(b_last + m_prev, jnp.max(logw, axis=1, keepdims=True))
                w = jnp.exp(logw - (m_new - ln_scale))
                decay = jnp.exp(b_last + m_prev - m_new)
                lhs = jnp.concatenate([(vt.astype(F32) * w).astype(BF16),
                                       jnp.where(sub16 == 0, w, 0.0).astype(BF16)], axis=0)
                st_ref[r] = decay[:, 0:hd] * st + _dot(lhs, k)
                ms_ref[r] = m_new
        return carry

    lax.fori_loop(0, nc, step, 0)

    for d in range(2):
        for h in range(ML_HEADS):
            r = d * ML_HEADS + h
            c_ref[0, d, h] = st_ref[r, 0:hd, :].T
            n_ref[0, d, h:h + 1, :] = st_ref[r, hd:hd + 1, :]
            m_ref[0, r:r + 1, :] = ms_ref[r][:, 0:hd]
    for c in range(nc):
        for h in range(ML_HEADS):
            cols = slice(h * hd, (h + 1) * hd)
            ht = hf_ref[c, cols, :] + hb_ref[c, cols, :]
            ht = ht * lax.rsqrt(jnp.mean(ht * ht, axis=0, keepdims=True) + EPS)
            rows = slice(c * ch, (c + 1) * ch)
            y = ht.T * gain_ref[:, cols] * jax.nn.sigmoid(o_ref[rows, cols].astype(F32))
            y_ref[rows, cols] = y.astype(BF16)


def _mlstm_kernel_rows(*refs, seq_len, has_state):
    if has_state:
        (q_ref, k_ref, v_ref, o_ref, g_ref, gt_ref, gain_ref, c0_ref, n0_ref, m0_ref,
         y_ref, c_ref, n_ref, m_ref, hf_ref, hb_ref, cs_ref, ns_ref, ms_ref) = refs
    else:
        (q_ref, k_ref, v_ref, o_ref, g_ref, gt_ref, gain_ref,
         y_ref, c_ref, n_ref, m_ref, hf_ref, hb_ref, cs_ref, ns_ref, ms_ref) = refs
    ch = TILE
    nc = seq_len // ch
    hd = ML_HEAD_DIM
    ri = lax.broadcasted_iota(jnp.int32, (ch, ch), 0)
    ci = lax.broadcasted_iota(jnp.int32, (ch, ch), 1)
    lower = ci <= ri
    upper = ci >= ri
    t_low = jnp.where(lower, 1.0, 0.0).astype(BF16)
    t_up = jnp.where(upper, 1.0, 0.0).astype(BF16)

    for d in range(2):
        for h in range(ML_HEADS):
            r = d * ML_HEADS + h
            if has_state:
                cs_ref[r] = c0_ref[0, d, h]
                ns_ref[r] = n0_ref[0, d, h:h + 1, :]
                ms_ref[r] = jnp.broadcast_to(m0_ref[0, r:r + 1, :], (1, hd))
            else:
                cs_ref[r] = jnp.zeros((hd, hd), F32)
                ns_ref[r] = jnp.zeros((1, hd), F32)
                ms_ref[r] = jnp.zeros((1, hd), F32)

    def step(t, carry):
        for d in range(2):
            c = t if d == 0 else nc - 1 - t
            r0 = pl.multiple_of(c * ch, ch)
            rows = pl.ds(r0, ch)
            gcol = g_ref[rows, :]
            grow = gt_ref[c]
            tmat_c = t_low if d == 0 else t_up
            tmat_r = t_up if d == 0 else t_low
            bcol_all = _dot_exact_lhs(tmat_c, gcol)
            brow_all = _dot_exact_rhs(grow, tmat_r)
            mask = lower if d == 0 else upper
            hacc_ref = hf_ref if d == 0 else hb_ref
            for h in range(ML_HEADS):
                r = d * ML_HEADS + h
                fcol = (1 + 2 * d) * ML_HEADS + h
                icol = (2 * d) * ML_HEADS + h
                cols = slice(h * hd, (h + 1) * hd)
                q = q_ref[rows, cols]
                k = k_ref[rows, cols]
                v = v_ref[rows, cols]
                c_prev = cs_ref[r]
                n_prev = ns_ref[r]
                m_prev = ms_ref[r][:, 0:1]
                b_col = bcol_all[:, fcol:fcol + 1]
                b_row = brow_all[fcol:fcol + 1, :]
                ig_row = grow[icol:icol + 1, :]
                ig_col = gcol[:, icol:icol + 1]
                logd = jnp.where(mask, b_col - b_row + ig_row, -jnp.inf)
                inter = b_col + m_prev
                m_pos = jnp.maximum(inter, jnp.max(logd, axis=-1, keepdims=True))
                s = _dot_nt(q, k) * K_SCALE * jnp.exp(logd - m_pos)
                sc_inter = jnp.exp(inter - m_pos)
                qf = q.astype(F32)
                num = sc_inter * _dot(q, c_prev.astype(BF16)) + _dot(s.astype(BF16), v)
                den = (sc_inter * jnp.sum(qf * n_prev, axis=-1, keepdims=True)
                       + jnp.sum(s, axis=-1, keepdims=True))
                hh = num / jnp.maximum(jnp.abs(den), jnp.exp(-m_pos))
                hacc_ref[rows, cols] = hh
                b_last = b_col[ch - 1:ch, :] if d == 0 else b_col[0:1, :]
                logw = b_last - b_col + ig_col
                m_new = jnp.maximum(b_last + m_prev, jnp.max(logw, axis=0, keepdims=True))
                w = jnp.exp(logw - m_new)
                decay = jnp.exp(b_last + m_prev - m_new)
                kw = k.astype(F32) * (w * K_SCALE)
                cs_ref[r] = decay * c_prev + _dot_tn(kw.astype(BF16), v)
                ns_ref[r] = decay * n_prev + jnp.sum(kw, axis=0, keepdims=True)
                ms_ref[r] = jnp.broadcast_to(m_new, (1, hd))
        return carry

    lax.fori_loop(0, nc, step, 0)

    for d in range(2):
        for h in range(ML_HEADS):
            r = d * ML_HEADS + h
            c_ref[0, d, h] = cs_ref[r]
            n_ref[0, d, h:h + 1, :] = ns_ref[r]
            m_ref[0, r:r + 1, :] = ms_ref[r]
    for h in range(ML_HEADS):
        cols = slice(h * hd, (h + 1) * hd)
        hh = hf_ref[:, cols] + hb_ref[:, cols]
        hh = hh * lax.rsqrt(jnp.mean(hh * hh, axis=-1, keepdims=True) + EPS)
        y = hh * gain_ref[:, cols] * jax.nn.sigmoid(o_ref[:, cols].astype(F32))
        y_ref[:, cols] = y.astype(BF16)


def _mlstm(proj, gates, gates_t, gain, state, seq_len, n_seq, row_block_off):
    has_state = state is not None
    tiles = seq_len // TILE
    off = row_block_off
    qkvo_specs = [pl.BlockSpec((seq_len, ML_WIDTH), functools.partial(lambda b, j: (off + b, j), j=j))
                  for j in range(4)]
    in_specs = qkvo_specs + [
        pl.BlockSpec((seq_len, ML_GATE_COLS), lambda b: (off + b, 0)),
        pl.BlockSpec((tiles, ML_GATE_COLS, TILE), lambda b: (off + b, 0, 0)),
        pl.BlockSpec((1, ML_WIDTH), lambda b: (0, 0)),
    ]
    args = [proj, proj, proj, proj, gates, gates_t, gain]
    if has_state:
        c0, n0, m0 = state
        in_specs += [
            pl.BlockSpec((1, 2, ML_HEADS, ML_HEAD_DIM, ML_HEAD_DIM), lambda b: (b, 0, 0, 0, 0)),
            pl.BlockSpec((1, 2, ML_HEADS, ML_HEAD_DIM), lambda b: (b, 0, 0, 0)),
            pl.BlockSpec((1, 2 * ML_HEADS, 1), lambda b: (b, 0, 0)),
        ]
        args += [c0, n0, m0]
    out_shape = (jax.ShapeDtypeStruct((n_seq * seq_len, ML_WIDTH), BF16),
                 jax.ShapeDtypeStruct((n_seq, 2, ML_HEADS, ML_HEAD_DIM, ML_HEAD_DIM), F32),
                 jax.ShapeDtypeStruct((n_seq, 2, ML_HEADS, ML_HEAD_DIM), F32),
                 jax.ShapeDtypeStruct((n_seq, 2 * ML_HEADS, ML_HEAD_DIM), F32))
    out_specs = (pl.BlockSpec((seq_len, ML_WIDTH), lambda b: (b, 0)),
                 pl.BlockSpec((1, 2, ML_HEADS, ML_HEAD_DIM, ML_HEAD_DIM), lambda b: (b, 0, 0, 0, 0)),
                 pl.BlockSpec((1, 2, ML_HEADS, ML_HEAD_DIM), lambda b: (b, 0, 0, 0)),
                 pl.BlockSpec((1, 2 * ML_HEADS, ML_HEAD_DIM), lambda b: (b, 0, 0)))
    scratch = [pltpu.VMEM((tiles, ML_WIDTH, TILE), BF16),
               pltpu.VMEM((tiles, ML_WIDTH, TILE), F32), pltpu.VMEM((tiles, ML_WIDTH, TILE), F32),
               pltpu.VMEM((2 * ML_HEADS, ST_ROWS, ML_HEAD_DIM), F32),
               pltpu.VMEM((2 * ML_HEADS, 1, TILE), F32)]
    return pl.pallas_call(
        functools.partial(_mlstm_kernel, seq_len=seq_len, has_state=has_state),
        out_shape=out_shape, grid=(n_seq,), in_specs=in_specs, out_specs=out_specs,
        scratch_shapes=scratch, compiler_params=_cparams(("arbitrary",)),
        name=f"mlstm_{seq_len}",
    )(*args)


def _dft_mats(seq_len):
    lo = 32
    hi = seq_len // lo
    d = jnp.arange(seq_len, dtype=jnp.int32)
    a = jnp.arange(hi, dtype=jnp.int32) * lo
    b = jnp.arange(lo, dtype=jnp.int32)
    scale = math.pi / seq_len
    ang_a = ((a[:, None] * d[None, :]) % (2 * seq_len)).astype(F32) * scale
    ang_b = ((b[:, None] * d[None, :]) % (2 * seq_len)).astype(F32) * scale
    ca, sa, cb, sb = jnp.cos(ang_a), jnp.sin(ang_a), jnp.cos(ang_b), jnp.sin(ang_b)
    cosm = (ca[:, None, :] * cb[None, :, :] - sa[:, None, :] * sb[None, :, :]).reshape(seq_len, seq_len)
    sinm = (sa[:, None, :] * cb[None, :, :] + ca[:, None, :] * sb[None, :, :]).reshape(seq_len, seq_len)
    nyq = jnp.where(d % 2 == 0, 1.0, -1.0).astype(F32)
    krow = jnp.arange(seq_len, dtype=jnp.int32)[:, None]
    sinm = jnp.where(krow == 0, nyq[None, :], sinm)
    f = jnp.concatenate([cosm, sinm], axis=0).astype(BF16)
    cos_t = (ca.T[:, :, None] * cb.T[:, None, :] - sa.T[:, :, None] * sb.T[:, None, :]).reshape(seq_len, seq_len)
    sin_t = (sa.T[:, :, None] * cb.T[:, None, :] + ca.T[:, :, None] * sb.T[:, None, :]).reshape(seq_len, seq_len)
    kcol = jnp.arange(seq_len, dtype=jnp.int32)[None, :]
    sin_t = jnp.where(kcol == 0, nyq[:, None], sin_t)
    ft = jnp.concatenate([cos_t, sin_t], axis=1).astype(BF16)
    return f, ft


def _filter_feats(seq_len):
    t = jnp.linspace(0.0, 1.0, seq_len, dtype=F32)[:, None]
    wpos = 2.0 * math.pi * jnp.arange(seq_len, dtype=F32)[:, None] / seq_len
    bands = jnp.linspace(1e-4, HY_BANDS - 1, HY_BANDS, dtype=F32)[None, :]
    z = jnp.concatenate([t, jnp.cos(bands * wpos), -jnp.sin(bands * wpos)], axis=-1)
    return jnp.pad(z, ((0, 0), (0, 128 - HY_EMB)))


def _filter_kernel(z_ref, w1_ref, b1_ref, w2_ref, b2_ref, w3_ref, b3_ref, dec_ref, f_ref,
                   a_ref, b_ref, d_ref, *, seq_len):
    n = 2 * seq_len
    z = z_ref[...]
    h = jnp.sin(_dot3(z, w1_ref[...]) + b1_ref[...])
    h = jnp.sin(_dot3(h, w2_ref[...]) + b2_ref[...])
    h = _dot3(h, w3_ref[...]) + b3_ref[...]
    t = z[:, 0:1]
    h = h * (jnp.exp(-t * jnp.abs(dec_ref[...])) + HY_MOD_SHIFT)
    ss = jnp.sum(h * h, axis=0, keepdims=True)
    inv = lax.rsqrt(ss[:, :HY_WIDTH] + ss[:, HY_WIDTH:] + EPS)
    hp = h[:, :HY_WIDTH] * inv
    hn = h[:, HY_WIDTH:] * inv
    ssum = hp + hn
    sdif = hp - hn
    hc = _dot(f_ref[0:seq_len, :], ssum.astype(BF16))
    hs = _dot(f_ref[seq_len:n, :], sdif.astype(BF16))
    di = lax.broadcasted_iota(jnp.int32, (seq_len, 1), 0)
    sgn = jnp.where(di % 2 == 0, 1.0, -1.0)
    nyq = jnp.sum(ssum * sgn, axis=0, keepdims=True)
    first = di == 0
    a_ref[0] = hc * jnp.where(first, 1.0 / n, 2.0 / n)
    b_ref[0] = jnp.where(first, 0.0, hs * (2.0 / n))
    d_ref[0] = jnp.where(first, nyq * (1.0 / n), hc * (2.0 / n))


def _hyena_filters(seq_len, f, w1p, b1, w2, b2, w3, b3, dec):
    z = _filter_feats(seq_len)
    hid = HY_FILTER_HIDDEN
    oc = 2 * HY_WIDTH
    full = lambda shape: pl.BlockSpec(shape, lambda o: tuple(0 for _ in shape))
    out = jax.ShapeDtypeStruct((HY_ORDER, seq_len, HY_WIDTH), F32)
    return pl.pallas_call(
        functools.partial(_filter_kernel, seq_len=seq_len),
        out_shape=(out, out, out),
        grid=(HY_ORDER,),
        in_specs=[full((seq_len, 128)), full((128, hid)), full((1, hid)), full((hid, hid)), full((1, hid)),
                  pl.BlockSpec((hid, oc), lambda o: (0, o)),
                  pl.BlockSpec((1, oc), lambda o: (0, o)),
                  pl.BlockSpec((1, oc), lambda o: (0, o)),
                  full((2 * seq_len, seq_len))],
        out_specs=tuple(pl.BlockSpec((1, seq_len, HY_WIDTH), lambda o: (o, 0, 0)) for _ in range(3)),
        compiler_params=_cparams(("arbitrary",)),
        name=f"hyena_filter_{seq_len}",
    )(z, w1p, b1, w2, b2, w3, b3, dec, f)


def _hyena_kernel(x1_ref, x2_ref, v_ref, cw1_ref, cw2_ref, cwv_ref, a_ref, b_ref, d_ref, bias_ref,
                  f_ref, ft_ref, z_ref, *, seq_len, width, seqs):
    rows = seqs * seq_len
    ti = lax.broadcasted_iota(jnp.int32, (rows, 1), 0)
    has_prev = (ti % width) != 0
    has_next = (ti % width) != (width - 1)

    def short_conv(x_ref, w_ref):
        x = x_ref[...].astype(F32)
        prev = jnp.where(has_prev, pltpu.roll(x, 1, axis=0), 0.0)
        nxt = jnp.where(has_next, pltpu.roll(x, rows - 1, axis=0), 0.0)
        return w_ref[0:1, :] * prev + w_ref[1:2, :] * x + w_ref[2:3, :] * nxt

    gates = (short_conv(x1_ref, cw1_ref), short_conv(x2_ref, cw2_ref))
    v = short_conv(v_ref, cwv_ref)
    for i in range(seqs):
        sl = slice(i * seq_len, (i + 1) * seq_len)
        z = v[sl]
        for o in range(HY_ORDER):
            u = _dot(f_ref[...], z.astype(BF16))
            ut = u[:seq_len]
            ub = u[seq_len:]
            a, b, dd = a_ref[o], b_ref[o], d_ref[o]
            yt = ut * a - ub * b
            yb = ut * b + ub * dd
            y = _dot(ft_ref[:, :seq_len], yt.astype(BF16)) + _dot(ft_ref[:, seq_len:], yb.astype(BF16))
            z = gates[o][sl] * (y + bias_ref[o:o + 1, :] * z)
        z_ref[sl, :] = z.astype(BF16)


def _hyena(proj, conv_w, coefs, hy_bias, f, ft, seq_len, n_seq, row_off, width, seqs):
    cb = 256
    nblk = HY_WIDTH // cb
    base = ML_QKVO_COLS // cb
    rows = seqs * seq_len
    off = row_off // rows
    a, b, d = coefs

    def col_spec(part):
        return pl.BlockSpec((rows, cb), lambda j, s: (off + s, base + part * nblk + j))

    def w_spec(part):
        return pl.BlockSpec((3, cb), lambda j, s: (0, part * nblk + j))

    coef_spec = pl.BlockSpec((HY_ORDER, seq_len, cb), lambda j, s: (0, 0, j))
    return pl.pallas_call(
        functools.partial(_hyena_kernel, seq_len=seq_len, width=width, seqs=seqs),
        out_shape=jax.ShapeDtypeStruct((n_seq * seq_len, HY_WIDTH), BF16),
        grid=(nblk, n_seq // seqs),
        in_specs=[col_spec(0), col_spec(1), col_spec(2), w_spec(0), w_spec(1), w_spec(2),
                  coef_spec, coef_spec, coef_spec,
                  pl.BlockSpec((HY_ORDER, cb), lambda j, s: (0, j)),
                  pl.BlockSpec((2 * seq_len, seq_len), lambda j, s: (0, 0)),
                  pl.BlockSpec((seq_len, 2 * seq_len), lambda j, s: (0, 0))],
        out_specs=pl.BlockSpec((rows, cb), lambda j, s: (s, j)),
        compiler_params=_cparams(("arbitrary", "arbitrary")),
        name=f"hyena_conv_{seq_len}",
    )(proj, proj, proj, conv_w, conv_w, conv_w, a, b, d, hy_bias, f, ft)


def _first_max(x, n):
    mx = jnp.max(x, axis=0, keepdims=True)
    row = lax.broadcasted_iota(jnp.int32, x.shape, 0).astype(F32)
    idx = jnp.min(jnp.where(x == mx, row, float(n)), axis=0, keepdims=True)
    return mx, idx.astype(jnp.int32)


ROUTER_ROWS = 32
PAIRS_PER_GROUP = 6
N_BUCKETS = N_GROUPS * PAIRS_PER_GROUP
PAIR_SLOTS = ((0, 1), (0, 2), (0, 3), (1, 3), (1, 2), (3, 2))
LANES = 128
H2_EXT = D_MODEL + LANES
ROW_TILE = 256
ROW_CAP = T_ALL + N_BUCKETS * ROW_TILE
N_ROW_TILES = ROW_CAP // ROW_TILE


def _outproj_kernel(xp_ref, xs_ref, yp_ref, ys_ref, zp_ref, zs_ref, m_ref, gn_ref, wo_ref, wr_ref, br_ref,
                    x1_ref, h2_ref, bid_ref):
    i = pl.program_id(0)
    is_p = i < N_TILES_P
    x = jnp.where(is_p, xp_ref[...], xs_ref[...])
    yml = jnp.where(is_p, yp_ref[...], ys_ref[...])
    zz = jnp.where(is_p, zp_ref[...], zs_ref[...])
    y = _dot(yml, wo_ref[0:ML_WIDTH, :]) + _dot(zz, wo_ref[ML_WIDTH:, :])
    x1 = x + m_ref[0, 2:3, :] * _rms(y, gn_ref[1:2, :])
    x1_ref[...] = x1
    h2 = _rms(x1, gn_ref[2:3, :]) * (1.0 + m_ref[0, 4:5, :]) + m_ref[0, 3:4, :]
    h2_ref[:, 0:D_MODEL] = h2
    h2h, h2l = _split2(h2)
    wrh, wrl = _split2(wr_ref[...])
    logits = _dot_nt(wrh, h2h) + _dot_nt(wrh, h2l) + _dot_nt(wrl, h2h) + br_ref[...]
    lc = logits[0:N_GROUPS]
    mx, gi = _first_max(lc, N_GROUPS)
    p_grp = 1.0 / jnp.sum(jnp.exp(lc - mx), axis=0, keepdims=True)
    lsel = jnp.zeros((EXPERTS_PER_GROUP, TILE), F32)
    for g in range(N_GROUPS):
        lo = N_GROUPS + g * EXPERTS_PER_GROUP
        lsel = jnp.where(gi == g, logits[lo:lo + EXPERTS_PER_GROUP], lsel)
    l1, i1 = _first_max(lsel, EXPERTS_PER_GROUP)
    sub4 = lax.broadcasted_iota(jnp.int32, lsel.shape, 0)
    l2, i2 = _first_max(jnp.where(sub4 == i1, -jnp.inf, lsel), EXPERTS_PER_GROUP)
    e2 = jnp.exp(l2 - l1)
    w1 = p_grp / (1.0 + e2)
    w2 = p_grp * e2 / (1.0 + e2)
    lo_e = jnp.minimum(i1, i2)
    hi_e = jnp.maximum(i1, i2)
    pair = jnp.where(lo_e == 0, hi_e - 1, jnp.where(lo_e == 1, jnp.where(hi_e == 3, 3, 4), 5))
    slot_a = jnp.where(pair == 5, hi_e, lo_e)
    first_in_a = i1 == slot_a
    w_a = jnp.where(first_in_a, w1, w2)
    w_b = jnp.where(first_in_a, w2, w1)
    sub = lax.broadcasted_iota(jnp.int32, (8, TILE), 0)
    gate_rows = jnp.where(sub == 0, w_a, jnp.where(sub == 1, w_b, 0.0))
    h2_ref[:, D_MODEL:H2_EXT] = jnp.zeros((TILE, LANES), F32)
    h2_ref[:, D_MODEL:D_MODEL + 8] = _rows_to_cols(gate_rows)
    bid_ref[0] = gi * PAIRS_PER_GROUP + pair


def _outproj(xp, xs, yp, ys, zp, zs, mods3, g_norm, w_out, w_r, b_r):
    tps = DEC_SEQ // TILE
    pidx = lambda i: (jnp.minimum(i, N_TILES_P - 1), 0)
    sidx = lambda i: (jnp.maximum(i - N_TILES_P, 0), 0)
    return pl.pallas_call(
        _outproj_kernel,
        out_shape=(jax.ShapeDtypeStruct((T_ALL, D_MODEL), F32),
                   jax.ShapeDtypeStruct((T_ALL, H2_EXT), F32),
                   jax.ShapeDtypeStruct((N_TILES, 1, TILE), jnp.int32)),
        grid=(N_TILES,),
        in_specs=[pl.BlockSpec((TILE, D_MODEL), pidx), pl.BlockSpec((TILE, D_MODEL), sidx),
                  pl.BlockSpec((TILE, ML_WIDTH), pidx), pl.BlockSpec((TILE, ML_WIDTH), sidx),
                  pl.BlockSpec((TILE, HY_WIDTH), pidx), pl.BlockSpec((TILE, HY_WIDTH), sidx),
                  pl.BlockSpec((1, N_MOD, D_MODEL), lambda i: (_mod_row_of_tile(i, tps, N_TILES_P), 0, 0)),
                  pl.BlockSpec((4, D_MODEL), lambda i: (0, 0)),
                  pl.BlockSpec((D_MODEL, D_MODEL), lambda i: (0, 0)),
                  pl.BlockSpec((ROUTER_ROWS, D_MODEL), lambda i: (0, 0)),
                  pl.BlockSpec((ROUTER_ROWS, 1), lambda i: (0, 0))],
        out_specs=(pl.BlockSpec((TILE, D_MODEL), lambda i: (i, 0)),
                   pl.BlockSpec((TILE, H2_EXT), lambda i: (i, 0)),
                   pl.BlockSpec((1, 1, TILE), lambda i: (i, 0, 0))),
        compiler_params=_cparams(("arbitrary",)),
        name="out_proj_router",
    )(xp, xs, yp, ys, zp, zs, mods3, g_norm, w_out, w_r, b_r)


def _route_kernel(bid_ref, pos_ref, meta_ref):
    nb = 32
    tm = float(ROW_TILE)
    sub = lax.broadcasted_iota(jnp.int32, (nb, TILE), 0)
    ri = lax.broadcasted_iota(jnp.int32, (TILE, TILE), 0)
    ci = lax.broadcasted_iota(jnp.int32, (TILE, TILE), 1)
    before = jnp.where(ri < ci, 1.0, 0.0).astype(BF16)

    def onehot(blk):
        return jnp.where(sub == bid_ref[blk], 1.0, 0.0)

    zeros = jnp.zeros((nb, 1), F32)
    cnt = lax.fori_loop(0, N_TILES, lambda blk, c: c + jnp.sum(onehot(blk), axis=1, keepdims=True), zeros)
    padded = jnp.floor((cnt + (tm - 1.0)) * (1.0 / tm)) * tm
    r32 = lax.broadcasted_iota(jnp.int32, (nb, nb), 0)
    c32 = lax.broadcasted_iota(jnp.int32, (nb, nb), 1)
    padded_row = jnp.sum(jnp.where(r32 == c32, padded, 0.0), axis=0, keepdims=True)
    offs = jnp.sum(jnp.where(c32 < r32, padded_row, 0.0), axis=1, keepdims=True)
    ends = offs + padded

    def place(blk, seen):
        oh = onehot(blk)
        rank = _dot(oh.astype(BF16), before)
        pos = jnp.sum(oh * (rank + seen + offs), axis=0, keepdims=True)
        pos_ref[blk] = pos.astype(jnp.int32)
        return seen + jnp.sum(oh, axis=1, keepdims=True)

    lax.fori_loop(0, N_TILES, place, zeros)

    start = lax.broadcasted_iota(jnp.int32, (nb, 128), 1).astype(F32) * tm
    bsub = lax.broadcasted_iota(jnp.int32, (nb, 128), 0)
    done = jnp.where((bsub < N_BUCKETS) & (ends <= start), 1.0, 0.0)
    tb = jnp.sum(done, axis=0, keepdims=True)
    valid = jnp.where(tb < N_BUCKETS, 1.0, 0.0)
    tbc = jnp.minimum(tb, N_BUCKETS - 1.0)
    grp = jnp.floor((tbc + 0.5) * (1.0 / PAIRS_PER_GROUP))
    pair = tbc - PAIRS_PER_GROUP * grp
    loc_a = jnp.zeros_like(pair)
    loc_b = jnp.zeros_like(pair)
    for k, (sa, sb) in enumerate(PAIR_SLOTS):
        loc_a = jnp.where(pair == k, float(sa), loc_a)
        loc_b = jnp.where(pair == k, float(sb), loc_b)
    mine = bsub.astype(F32) == tbc
    used = jnp.sum(jnp.where(mine, offs + cnt, 0.0), axis=0, keepdims=True)
    n_rows = jnp.clip(used - start[0:1], 0.0, tm) * valid
    row8 = lax.broadcasted_iota(jnp.int32, (8, 128), 0)
    meta = jnp.where(row8 == 0, grp * EXPERTS_PER_GROUP + loc_a,
                     jnp.where(row8 == 1, grp * EXPERTS_PER_GROUP + loc_b,
                               jnp.where(row8 == 2, valid, jnp.where(row8 == 3, n_rows, 0.0))))
    meta_ref[...] = meta.astype(jnp.int32)


def _route(bid):
    return pl.pallas_call(
        _route_kernel,
        out_shape=(jax.ShapeDtypeStruct((N_TILES, 1, TILE), jnp.int32),
                   jax.ShapeDtypeStruct((8, 128), jnp.int32)),
        compiler_params=pltpu.CompilerParams(vmem_limit_bytes=VMEM_LIMIT),
        name="moe_route",
    )(bid)


def _moe_kernel(meta_ref, pos_ref, h2_hbm, wga_ref, wua_ref, wda_ref, wgb_ref, wub_ref, wdb_ref,
                y_ref, src_ref, xbuf, sem, wga_s, wua_s, wda_s, wgb_s, wub_s, wdb_s):
    j = pl.program_id(0)

    def row_copy(tile, r, slot):
        tok = src_ref[tile * ROW_TILE + r]
        return pltpu.make_async_copy(h2_hbm.at[pl.ds(tok, 1), :], xbuf.at[slot, pl.ds(r, 1), :], sem.at[slot])

    group = 8

    def row_groups(tile):
        return (meta_ref[3, tile] + (group - 1)) // group

    def issue(tile, slot):
        def body(g, c):
            for k in range(group):
                row_copy(tile, g * group + k, slot).start()
            return c
        lax.fori_loop(0, row_groups(tile), body, 0)

    def wait(tile, slot):
        def body(g, c):
            for k in range(group):
                row_copy(tile, g * group + k, slot).wait()
            return c
        lax.fori_loop(0, row_groups(tile), body, 0)

    @pl.when(j == 0)
    def _():
        xbuf[...] = jnp.zeros_like(xbuf)

        def clear(p, c):
            src_ref[p] = 0
            return c
        lax.fori_loop(0, ROW_CAP, clear, 0, unroll=8)

        def invert(t, c):
            src_ref[pos_ref[t]] = t
            return c
        lax.fori_loop(0, T_ALL, invert, 0, unroll=8)

        @pl.when(meta_ref[2, 0] == 1)
        def _():
            issue(0, 0)

    nxt = jnp.minimum(j + 1, N_ROW_TILES - 1)

    @pl.when((j + 1 < N_ROW_TILES) & (meta_ref[2, nxt] == 1))
    def _():
        issue(nxt, nxt % 2)

    valid = meta_ref[2, j] == 1
    prev = jnp.maximum(j - 1, 0)

    @pl.when(valid & ((j == 0) | (meta_ref[0, j] != meta_ref[0, prev])))
    def _():
        wga_s[...] = wga_ref[0].astype(BF16)
        wua_s[...] = wua_ref[0].astype(BF16)
        wda_s[...] = wda_ref[0].astype(BF16)

    @pl.when(valid & ((j == 0) | (meta_ref[1, j] != meta_ref[1, prev])))
    def _():
        wgb_s[...] = wgb_ref[0].astype(BF16)
        wub_s[...] = wub_ref[0].astype(BF16)
        wdb_s[...] = wdb_ref[0].astype(BF16)

    @pl.when(valid)
    def _():
        slot = j % 2
        wait(j, slot)
        x = xbuf[slot, :, 0:D_MODEL].astype(BF16)
        gates = xbuf[slot, :, D_MODEL:H2_EXT]

        def expert(wg, wu, gate):
            hg = _dot(x, wg[...])
            hu = _dot(x, wu[...])
            return (hg * jax.nn.sigmoid(hg) * hu * gate).astype(BF16)

        act_a = expert(wga_s, wua_s, gates[:, 0:1])
        act_b = expert(wgb_s, wub_s, gates[:, 1:2])
        y_ref[...] = _dot(act_a, wda_s[...]) + _dot(act_b, wdb_s[...])

    @pl.when(jnp.logical_not(valid))
    def _():
        y_ref[...] = jnp.zeros_like(y_ref)


def _moe(meta, pos, h2ext, w_gate, w_up, w_down):
    up_spec = lambda slot: pl.BlockSpec((1, D_MODEL, EXPERT_FF), lambda j, meta, pos: (meta[slot, j], 0, 0))
    down_spec = lambda slot: pl.BlockSpec((1, EXPERT_FF, D_MODEL), lambda j, meta, pos: (meta[slot, j], 0, 0))
    grid_spec = pltpu.PrefetchScalarGridSpec(
        num_scalar_prefetch=2,
        grid=(N_ROW_TILES,),
        in_specs=[pl.BlockSpec(memory_space=pl.ANY),
                  up_spec(0), up_spec(0), down_spec(0), up_spec(1), up_spec(1), down_spec(1)],
        out_specs=pl.BlockSpec((ROW_TILE, D_MODEL), lambda j, meta, pos: (j, 0)),
        scratch_shapes=[pltpu.SMEM((ROW_CAP,), jnp.int32),
                        pltpu.VMEM((2, ROW_TILE, H2_EXT), F32),
                        pltpu.SemaphoreType.DMA((2,)),
                        pltpu.VMEM((D_MODEL, EXPERT_FF), BF16), pltpu.VMEM((D_MODEL, EXPERT_FF), BF16),
                        pltpu.VMEM((EXPERT_FF, D_MODEL), BF16),
                        pltpu.VMEM((D_MODEL, EXPERT_FF), BF16), pltpu.VMEM((D_MODEL, EXPERT_FF), BF16),
                        pltpu.VMEM((EXPERT_FF, D_MODEL), BF16)])
    return pl.pallas_call(
        _moe_kernel,
        out_shape=jax.ShapeDtypeStruct((ROW_CAP, D_MODEL), F32),
        grid_spec=grid_spec,
        compiler_params=_cparams(("arbitrary",)),
        name="moe_experts",
    )(meta, pos, h2ext, w_gate, w_up, w_down, w_gate, w_up, w_down)


def _final_kernel(pos_ref, y_hbm, x1_ref, m_ref, gn_ref, op_ref, os_ref, ybuf, sem):
    i = pl.program_id(0)

    def row_copy(tile, r, slot):
        p = pos_ref[tile * TILE + r]
        return pltpu.make_async_copy(y_hbm.at[pl.ds(p, 1), :], ybuf.at[slot, pl.ds(r, 1), :], sem.at[slot])

    def issue(tile, slot):
        def body(r2, c):
            row_copy(tile, 2 * r2, slot).start(priority=0)
            row_copy(tile, 2 * r2 + 1, slot).start(priority=1)
            return c
        lax.fori_loop(0, TILE // 2, body, 0, unroll=4)

    def wait(slot):
        pltpu.make_async_copy(y_hbm.at[pl.ds(0, TILE)], ybuf.at[slot], sem.at[slot]).wait()

    @pl.when(i == 0)
    def _():
        issue(0, 0)

    @pl.when(i + 1 < N_TILES)
    def _():
        issue(i + 1, (i + 1) % 2)

    slot = i % 2
    wait(slot)
    out = x1_ref[...] + m_ref[0, 5:6, :] * _rms(ybuf[slot], gn_ref[3:4, :])

    @pl.when(i < N_TILES_P)
    def _():
        op_ref[...] = out

    @pl.when(i >= N_TILES_P)
    def _():
        os_ref[...] = out


def _final(pos, y_sorted, x1, mods3, g_norm):
    tps = DEC_SEQ // TILE
    grid_spec = pltpu.PrefetchScalarGridSpec(
        num_scalar_prefetch=1,
        grid=(N_TILES,),
        in_specs=[pl.BlockSpec(memory_space=pl.ANY),
                  pl.BlockSpec((TILE, D_MODEL), lambda i, pos: (i, 0)),
                  pl.BlockSpec((1, N_MOD, D_MODEL), lambda i, pos: (_mod_row_of_tile(i, tps, N_TILES_P), 0, 0)),
                  pl.BlockSpec((4, D_MODEL), lambda i, pos: (0, 0))],
        out_specs=(pl.BlockSpec((TILE, D_MODEL), lambda i, pos: (jnp.minimum(i, N_TILES_P - 1), 0)),
                   pl.BlockSpec((TILE, D_MODEL), lambda i, pos: (jnp.maximum(i - N_TILES_P, 0), 0))),
        scratch_shapes=[pltpu.VMEM((2, TILE, D_MODEL), F32), pltpu.SemaphoreType.DMA((2,))])
    return pl.pallas_call(
        _final_kernel,
        out_shape=(jax.ShapeDtypeStruct((T_PROMPT, D_MODEL), F32),
                   jax.ShapeDtypeStruct((T_SAMPLE, D_MODEL), F32)),
        grid_spec=grid_spec,
        compiler_params=_cparams(("arbitrary",)),
        name="moe_combine_final",
    )(pos, y_sorted, x1, mods3, g_norm)


def kernel(x_prompt, x_sample, state_C, state_n, state_m, c, c_ctx, w_ada, b_ada, g_norm, w_in, ml_gate_bias, ml_head_gain, hy_conv_w, hy_f_w1, hy_f_b1, hy_f_w2, hy_f_b2, hy_f_w3, hy_f_b3, hy_decay, hy_bias, w_out, w_rc, b_rc, w_rf, b_rf, w_gate, w_up, w_down):
    xp = x_prompt.reshape(T_PROMPT, D_MODEL)
    xs = x_sample.reshape(T_SAMPLE, D_MODEL)
    gn = g_norm[0]

    cv = jnp.concatenate([c_ctx[None, :], c, jnp.zeros((MOD_ROWS - 1 - DEC_BATCH, D_MODEL), F32)], axis=0)
    mods3 = _ada(cv, w_ada[0], b_ada[0]).reshape(MOD_ROWS, N_MOD, D_MODEL)

    w_in0 = w_in[0]
    w_main = jnp.concatenate([w_in0[:, :ML_QKVO_COLS], w_in0[:, ML_QKVO_COLS + ML_GATE_COLS:]],
                             axis=1).astype(BF16)
    wg = w_in0[:, ML_QKVO_COLS:ML_QKVO_COLS + ML_GATE_COLS]
    gbt = ml_gate_bias[0].reshape(ML_GATE_COLS, 1)
    proj, gates, gates_t = _inproj(xp, xs, mods3, gn, w_main, wg.T, gbt)

    gain = ml_head_gain[0].reshape(1, ML_WIDTH)
    y_ml_p, c_new, n_new, m_new = _mlstm(proj, gates, gates_t, gain, None, SEQ, BATCH, 0)
    state = (state_C[:, 0], state_n[:, 0], state_m[:, 0].reshape(DEC_BATCH, 2 * ML_HEADS, 1))
    y_ml_s, _, _, _ = _mlstm(proj, gates, gates_t, gain, state, DEC_SEQ, DEC_BATCH, T_PROMPT // DEC_SEQ)

    w1p = jnp.pad(hy_f_w1[0], ((0, 128 - HY_EMB), (0, 0)))
    b1 = hy_f_b1[0].reshape(1, -1)
    b2 = hy_f_b2[0].reshape(1, -1)
    b3 = hy_f_b3[0].reshape(1, -1)
    dec = hy_decay[0].reshape(1, -1)
    z_parts = []
    for seq_len, n_seq, row_off, width, seqs in ((SEQ, BATCH, 0, SEQ, 4), (DEC_SEQ, DEC_BATCH, T_PROMPT, GRID_W, 2)):
        f, ft = _dft_mats(seq_len)
        coefs = _hyena_filters(seq_len, f, w1p, b1, hy_f_w2[0], b2, hy_f_w3[0], b3, dec)
        z_parts.append(_hyena(proj, hy_conv_w[0], coefs, hy_bias[0], f, ft, seq_len, n_seq, row_off, width, seqs))
    z_p, z_s = z_parts

    pad_r = ROUTER_ROWS - N_GROUPS - N_EXPERTS
    w_r = jnp.pad(jnp.concatenate([w_rc[0], w_rf[0]], axis=1).T, ((0, pad_r), (0, 0)))
    b_r = jnp.pad(jnp.concatenate([b_rc[0], b_rf[0]], axis=0), (0, pad_r)).reshape(ROUTER_ROWS, 1)
    x1, h2ext, bid = _outproj(xp, xs, y_ml_p, y_ml_s, z_p, z_s, mods3, gn, w_out[0].astype(BF16), w_r, b_r)

    pos3, meta = _route(bid)
    pos = pos3.reshape(T_ALL)
    y_sorted = _moe(meta, pos, h2ext, w_gate[0], w_up[0], w_down[0])
    y_p, y_s = _final(pos, y_sorted, x1, mods3, gn)

    new_c = c_new.reshape(BATCH, 1, 2, ML_HEADS, ML_HEAD_DIM, ML_HEAD_DIM)
    new_n = n_new.reshape(BATCH, 1, 2, ML_HEADS, ML_HEAD_DIM)
    new_m = m_new[:, :, 0].reshape(BATCH, 1, 2, ML_HEADS)
    return (y_p.reshape(BATCH, SEQ, D_MODEL), y_s.reshape(DEC_BATCH, DEC_SEQ, D_MODEL), new_c, new_n, new_m)
```

```python
import functools
import math

import jax
import jax.numpy as jnp
from jax import lax
from jax.experimental import pallas as pl
from jax.experimental.pallas import tpu as pltpu

F32 = jnp.float32
BF16 = jnp.bfloat16

D_MODEL = 1024
BATCH = 16
SEQ = 256
DEC_BATCH = 4
DEC_SEQ = 1024
GRID_W = 64
ML_WIDTH = 512
ML_HEADS = 4
ML_HEAD_DIM = 128
HY_WIDTH = 512
HY_ORDER = 2
HY_EMB = 33
HY_BANDS = 16
HY_FILTER_HIDDEN = 64
HY_MOD_SHIFT = 0.05
N_GROUPS = 4
EXPERTS_PER_GROUP = 4
N_EXPERTS = 16
EXPERT_FF = 512
N_MOD = 6
EPS = 1e-6
ML_QKVO_COLS = 4 * ML_WIDTH
ML_GATE_COLS = 4 * ML_HEADS
HY_COLS = 3 * HY_WIDTH
MAIN_COLS = ML_QKVO_COLS + HY_COLS

T_PROMPT = BATCH * SEQ
T_SAMPLE = DEC_BATCH * DEC_SEQ
T_ALL = T_PROMPT + T_SAMPLE
TILE = 256
N_TILES_P = T_PROMPT // TILE
N_TILES = T_ALL // TILE
MOD_ROWS = 8
K_SCALE = ML_HEAD_DIM ** -0.5
VMEM_LIMIT = 56 * 1024 * 1024


def _cparams(sem):
    return pltpu.CompilerParams(dimension_semantics=sem, vmem_limit_bytes=VMEM_LIMIT)


def _split2(x):
    hi = x.astype(BF16)
    lo = (x - hi.astype(F32)).astype(BF16)
    return hi, lo


def _dot(a, b):
    return jnp.dot(a, b, preferred_element_type=F32)


def _dot_nt(a, b):
    return lax.dot_general(a, b, (((1,), (1,)), ((), ())), preferred_element_type=F32)


def _dot_tn(a, b):
    return lax.dot_general(a, b, (((0,), (0,)), ((), ())), preferred_element_type=F32)


def _dot3(a, b):
    ah, al = _split2(a)
    bh, bl = _split2(b)
    return _dot(ah, bh) + _dot(al, bh) + _dot(ah, bl)


def _dot3_nt(a, b):
    ah, al = _split2(a)
    bh, bl = _split2(b)
    return _dot_nt(ah, bh) + _dot_nt(al, bh) + _dot_nt(ah, bl)


def _dot_exact_lhs(t, x):
    x1 = x.astype(BF16)
    r1 = x - x1.astype(F32)
    x2 = r1.astype(BF16)
    x3 = (r1 - x2.astype(F32)).astype(BF16)
    return _dot(t, x1) + _dot(t, x2) + _dot(t, x3)


def _dot_exact_rhs(x, t):
    x1 = x.astype(BF16)
    r1 = x - x1.astype(F32)
    x2 = r1.astype(BF16)
    x3 = (r1 - x2.astype(F32)).astype(BF16)
    return _dot(x1, t) + _dot(x2, t) + _dot(x3, t)


def _rms(x, g):
    return x * lax.rsqrt(jnp.mean(x * x, axis=-1, keepdims=True) + EPS) * g


def _mod_row_of_tile(i, tiles_per_sample_seq, n_prompt_tiles):
    return jnp.where(i < n_prompt_tiles, 0, 1 + (i - n_prompt_tiles) // tiles_per_sample_seq)


def _ada_kernel(cv_ref, w_ref, b_ref, o_ref):
    cv = cv_ref[...]
    s = cv * jax.nn.sigmoid(cv)
    o_ref[...] = _dot3(s, w_ref[...]) + b_ref[...]


def _ada(cv, w_ada, b_ada):
    n = N_MOD * D_MODEL
    return pl.pallas_call(
        _ada_kernel,
        out_shape=jax.ShapeDtypeStruct((MOD_ROWS, n), F32),
        grid=(N_MOD,),
        in_specs=[pl.BlockSpec((MOD_ROWS, D_MODEL), lambda j: (0, 0)),
                  pl.BlockSpec((D_MODEL, D_MODEL), lambda j: (0, j)),
                  pl.BlockSpec((1, D_MODEL), lambda j: (0, j))],
        out_specs=pl.BlockSpec((MOD_ROWS, D_MODEL), lambda j: (0, j)),
        compiler_params=_cparams(("arbitrary",)),
        name="ada_mod",
    )(cv, w_ada, b_ada.reshape(1, n))


def _log_sigmoid(x):
    return jnp.minimum(x, 0.0) - jnp.log1p(jnp.exp(-jnp.abs(x)))


def _rows_to_cols(rows):
    ri = lax.broadcasted_iota(jnp.int32, (TILE, TILE), 0)
    ci = lax.broadcasted_iota(jnp.int32, (TILE, TILE), 1)
    eye = jnp.where(ri == ci, 1.0, 0.0).astype(BF16)
    p1 = rows.astype(BF16)
    r1 = rows - p1.astype(F32)
    p2 = r1.astype(BF16)
    p3 = (r1 - p2.astype(F32)).astype(BF16)
    return _dot_nt(eye, p1) + _dot_nt(eye, p2) + _dot_nt(eye, p3)


BIG_TILE = 2 * TILE
N_BIG_P = T_PROMPT // BIG_TILE
N_BIG = T_ALL // BIG_TILE


def _inproj_kernel(xp_ref, xs_ref, m_ref, gn_ref, wq_ref, wh_ref, wgt_ref, gbt_ref,
                   proj_ref, gate_ref, gatet_ref, w_s):
    i = pl.program_id(0)

    @pl.when(i == 0)
    def _():
        w_s[:, 0:ML_QKVO_COLS] = wq_ref[...].astype(BF16)
        w_s[:, ML_QKVO_COLS:MAIN_COLS] = wh_ref[...].astype(BF16)

    wth, wtl = _split2(wgt_ref[...])
    cb = 512
    for r in range(BIG_TILE // TILE):
        rows = slice(r * TILE, (r + 1) * TILE)
        x = jnp.where(i < N_BIG_P, xp_ref[rows, :], xs_ref[rows, :])
        h = _rms(x, gn_ref[0:1, :]) * (1.0 + m_ref[0, 1:2, :]) + m_ref[0, 0:1, :]
        hb = h.astype(BF16)
        for j in range(MAIN_COLS // cb):
            proj_ref[rows, j * cb:(j + 1) * cb] = _dot(hb, w_s[:, j * cb:(j + 1) * cb]).astype(BF16)
        hl = (h - hb.astype(F32)).astype(BF16)
        gt = _dot_nt(wth, hb) + _dot_nt(wth, hl) + _dot_nt(wtl, hb) + gbt_ref[...]
        row = lax.broadcasted_iota(jnp.int32, gt.shape, 0)
        gt = jnp.where((row % 8) >= 4, _log_sigmoid(gt), gt)
        gatet_ref[r] = gt
        gate_ref[rows, :] = _rows_to_cols(gt)


def _inproj(xp, xs, mods3, g_norm, w_qkvo, w_hy, wgt, gbt):
    tps = DEC_SEQ // BIG_TILE
    per = BIG_TILE // TILE
    const = lambda shape: pl.BlockSpec(shape, lambda i: tuple(0 for _ in shape), pipeline_mode=pl.Buffered(1))
    return pl.pallas_call(
        _inproj_kernel,
        out_shape=(jax.ShapeDtypeStruct((T_ALL, MAIN_COLS), BF16),
                   jax.ShapeDtypeStruct((T_ALL, ML_GATE_COLS), F32),
                   jax.ShapeDtypeStruct((N_TILES, ML_GATE_COLS, TILE), F32)),
        grid=(N_BIG,),
        in_specs=[pl.BlockSpec((BIG_TILE, D_MODEL), lambda i: (jnp.minimum(i, N_BIG_P - 1), 0)),
                  pl.BlockSpec((BIG_TILE, D_MODEL), lambda i: (jnp.maximum(i - N_BIG_P, 0), 0)),
                  pl.BlockSpec((1, N_MOD, D_MODEL), lambda i: (_mod_row_of_tile(i, tps, N_BIG_P), 0, 0)),
                  pl.BlockSpec((4, D_MODEL), lambda i: (0, 0)),
                  const((D_MODEL, ML_QKVO_COLS)),
                  const((D_MODEL, HY_COLS)),
                  pl.BlockSpec((ML_GATE_COLS, D_MODEL), lambda i: (0, 0)),
                  pl.BlockSpec((ML_GATE_COLS, 1), lambda i: (0, 0))],
        out_specs=(pl.BlockSpec((BIG_TILE, MAIN_COLS), lambda i: (i, 0)),
                   pl.BlockSpec((BIG_TILE, ML_GATE_COLS), lambda i: (i, 0)),
                   pl.BlockSpec((per, ML_GATE_COLS, TILE), lambda i: (i, 0, 0))),
        scratch_shapes=[pltpu.VMEM((D_MODEL, MAIN_COLS), BF16)],
        compiler_params=_cparams(("arbitrary",)),
        name="in_proj",
    )(xp, xs, mods3, g_norm, w_qkvo, w_hy, wgt, gbt)


ST_ROWS = ML_HEAD_DIM + 16


def _mlstm_kernel(*refs, seq_len, has_state):
    if has_state:
        (q_ref, k_ref, v_ref, o_ref, g_ref, gt_ref, gain_ref, c0_ref, n0_ref, m0_ref,
         y_ref, c_ref, n_ref, m_ref, vt_ref, hf_ref, hb_ref, st_ref, ms_ref) = refs
    else:
        (q_ref, k_ref, v_ref, o_ref, g_ref, gt_ref, gain_ref,
         y_ref, c_ref, n_ref, m_ref, vt_ref, hf_ref, hb_ref, st_ref, ms_ref) = refs
    ch = TILE
    nc = seq_len // ch
    hd = ML_HEAD_DIM
    key = lax.broadcasted_iota(jnp.int32, (ch, ch), 0)
    qry = lax.broadcasted_iota(jnp.int32, (ch, ch), 1)
    key_le = key <= qry
    key_ge = key >= qry
    t_le = jnp.where(key_le, 1.0, 0.0).astype(BF16)
    t_ge = jnp.where(key_ge, 1.0, 0.0).astype(BF16)
    sub16 = lax.broadcasted_iota(jnp.int32, (16, ch), 0)
    ln_scale = math.log(K_SCALE)

    for c in range(nc):
        for h in range(ML_HEADS):
            cols = slice(h * hd, (h + 1) * hd)
            vt_ref[c, cols, :] = v_ref[c * ch:(c + 1) * ch, cols].astype(F32).T.astype(BF16)

    for d in range(2):
        for h in range(ML_HEADS):
            r = d * ML_HEADS + h
            st_ref[r] = jnp.zeros((ST_ROWS, hd), F32)
            if has_state:
                st_ref[r, 0:hd, :] = c0_ref[0, d, h].T
                st_ref[r, hd:hd + 1, :] = n0_ref[0, d, h:h + 1, :]
                ms_ref[r] = jnp.broadcast_to(m0_ref[0, r:r + 1, :], (1, ch))
            else:
                ms_ref[r] = jnp.zeros((1, ch), F32)

    def step(t, carry):
        for d in range(2):
            c = t if d == 0 else nc - 1 - t
            rows = pl.ds(pl.multiple_of(c * ch, ch), ch)
            gcol = g_ref[rows, :]
            grow = gt_ref[c]
            brow_all = _dot_exact_rhs(grow, t_le if d == 0 else t_ge)
            bcol_all = _dot_exact_lhs(t_ge if d == 0 else t_le, gcol)
            mask = key_le if d == 0 else key_ge
            hacc_ref = hf_ref if d == 0 else hb_ref
            for h in range(ML_HEADS):
                r = d * ML_HEADS + h
                fcol = (1 + 2 * d) * ML_HEADS + h
                icol = (2 * d) * ML_HEADS + h
                cols = slice(h * hd, (h + 1) * hd)
                q = q_ref[rows, cols]
                k = k_ref[rows, cols]
                vt = vt_ref[c, cols, :]
                st = st_ref[r]
                m_prev = ms_ref[r]
                b_row = brow_all[fcol:fcol + 1, :]
                ig_row = grow[icol:icol + 1, :]
                c_col = gcol[:, icol:icol + 1] - bcol_all[:, fcol:fcol + 1]
                logd = jnp.where(mask, b_row + c_col, -jnp.inf)
                inter = b_row + m_prev
                m_pos = jnp.maximum(inter, jnp.max(logd, axis=0, keepdims=True))
                s = _dot_nt(k, q) * jnp.exp(logd - (m_pos - ln_scale))
                sc_inter = jnp.exp(inter - m_pos)
                iq = _dot_nt(st.astype(BF16), q)
                num = sc_inter * iq[0:hd] + _dot(vt, s.astype(BF16))
                den = sc_inter * iq[hd:hd + 1] + jnp.sum(s, axis=0, keepdims=True)
                hacc_ref[c, cols, :] = num * (1.0 / jnp.maximum(jnp.abs(den), jnp.exp(-m_pos)))
                b_last = b_row[:, ch - 1:ch] if d == 0 else b_row[:, 0:1]
                logw = b_last - b_row + ig_row
                m_new = jnp.maximum(b_last + m_prev, jnp.max(logw, axis=1, keepdims=True))
                w = jnp.exp(logw - (m_new - ln_scale))
                decay = jnp.exp(b_last + m_prev - m_new)
                lhs = jnp.concatenate([(vt.astype(F32) * w).astype(BF16),
                                       jnp.where(sub16 == 0, w, 0.0).astype(BF16)], axis=0)
                st_ref[r] = decay[:, 0:hd] * st + _dot(lhs, k)
                ms_ref[r] = m_new
        return carry

    lax.fori_loop(0, nc, step, 0)

    for d in range(2):
        for h in range(ML_HEADS):
            r = d * ML_HEADS + h
            c_ref[0, d, h] = st_ref[r, 0:hd, :].T
            n_ref[0, d, h:h + 1, :] = st_ref[r, hd:hd + 1, :]
            m_ref[0, r:r + 1, :] = ms_ref[r][:, 0:hd]
    for c in range(nc):
        for h in range(ML_HEADS):
            cols = slice(h * hd, (h + 1) * hd)
            ht = hf_ref[c, cols, :] + hb_ref[c, cols, :]
            ht = ht * lax.rsqrt(jnp.mean(ht * ht, axis=0, keepdims=True) + EPS)
            rows = slice(c * ch, (c + 1) * ch)
            y = ht.T * gain_ref[:, cols] * jax.nn.sigmoid(o_ref[rows, cols].astype(F32))
            y_ref[rows, cols] = y.astype(BF16)


def _mlstm_kernel_rows(*refs, seq_len, has_state):
    if has_state:
        (q_ref, k_ref, v_ref, o_ref, g_ref, gt_ref, gain_ref, c0_ref, n0_ref, m0_ref,
         y_ref, c_ref, n_ref, m_ref, hf_ref, hb_ref, cs_ref, ns_ref, ms_ref) = refs
    else:
        (q_ref, k_ref, v_ref, o_ref, g_ref, gt_ref, gain_ref,
         y_ref, c_ref, n_ref, m_ref, hf_ref, hb_ref, cs_ref, ns_ref, ms_ref) = refs
    ch = TILE
    nc = seq_len // ch
    hd = ML_HEAD_DIM
    ri = lax.broadcasted_iota(jnp.int32, (ch, ch), 0)
    ci = lax.broadcasted_iota(jnp.int32, (ch, ch), 1)
    lower = ci <= ri
    upper = ci >= ri
    t_low = jnp.where(lower, 1.0, 0.0).astype(BF16)
    t_up = jnp.where(upper, 1.0, 0.0).astype(BF16)

    for d in range(2):
        for h in range(ML_HEADS):
            r = d * ML_HEADS + h
            if has_state:
                cs_ref[r] = c0_ref[0, d, h]
                ns_ref[r] = n0_ref[0, d, h:h + 1, :]
                ms_ref[r] = jnp.broadcast_to(m0_ref[0, r:r + 1, :], (1, hd))
            else:
                cs_ref[r] = jnp.zeros((hd, hd), F32)
                ns_ref[r] = jnp.zeros((1, hd), F32)
                ms_ref[r] = jnp.zeros((1, hd), F32)

    def step(t, carry):
        for d in range(2):
            c = t if d == 0 else nc - 1 - t
            r0 = pl.multiple_of(c * ch, ch)
            rows = pl.ds(r0, ch)
            gcol = g_ref[rows, :]
            grow = gt_ref[c]
            tmat_c = t_low if d == 0 else t_up
            tmat_r = t_up if d == 0 else t_low
            bcol_all = _dot_exact_lhs(tmat_c, gcol)
            brow_all = _dot_exact_rhs(grow, tmat_r)
            mask = lower if d == 0 else upper
            hacc_ref = hf_ref if d == 0 else hb_ref
            for h in range(ML_HEADS):
                r = d * ML_HEADS + h
                fcol = (1 + 2 * d) * ML_HEADS + h
                icol = (2 * d) * ML_HEADS + h
                cols = slice(h * hd, (h + 1) * hd)
                q = q_ref[rows, cols]
                k = k_ref[rows, cols]
                v = v_ref[rows, cols]
                c_prev = cs_ref[r]
                n_prev = ns_ref[r]
                m_prev = ms_ref[r][:, 0:1]
                b_col = bcol_all[:, fcol:fcol + 1]
                b_row = brow_all[fcol:fcol + 1, :]
                ig_row = grow[icol:icol + 1, :]
                ig_col = gcol[:, icol:icol + 1]
                logd = jnp.where(mask, b_col - b_row + ig_row, -jnp.inf)
                inter = b_col + m_prev
                m_pos = jnp.maximum(inter, jnp.max(logd, axis=-1, keepdims=True))
                s = _dot_nt(q, k) * K_SCALE * jnp.exp(logd - m_pos)
                sc_inter = jnp.exp(inter - m_pos)
                qf = q.astype(F32)
                num = sc_inter * _dot(q, c_prev.astype(BF16)) + _dot(s.astype(BF16), v)
                den = (sc_inter * jnp.sum(qf * n_prev, axis=-1, keepdims=True)
                       + jnp.sum(s, axis=-1, keepdims=True))
                hh = num / jnp.maximum(jnp.abs(den), jnp.exp(-m_pos))
                hacc_ref[rows, cols] = hh
                b_last = b_col[ch - 1:ch, :] if d == 0 else b_col[0:1, :]
                logw = b_last - b_col + ig_col
                m_new = jnp.maximum(b_last + m_prev, jnp.max(logw, axis=0, keepdims=True))
                w = jnp.exp(logw - m_new)
                decay = jnp.exp(b_last + m_prev - m_new)
                kw = k.astype(F32) * (w * K_SCALE)
                cs_ref[r] = decay * c_prev + _dot_tn(kw.astype(BF16), v)
                ns_ref[r] = decay * n_prev + jnp.sum(kw, axis=0, keepdims=True)
                ms_ref[r] = jnp.broadcast_to(m_new, (1, hd))
        return carry

    lax.fori_loop(0, nc, step, 0)

    for d in range(2):
        for h in range(ML_HEADS):
            r = d * ML_HEADS + h
            c_ref[0, d, h] = cs_ref[r]
            n_ref[0, d, h:h + 1, :] = ns_ref[r]
            m_ref[0, r:r + 1, :] = ms_ref[r]
    for h in range(ML_HEADS):
        cols = slice(h * hd, (h + 1) * hd)
        hh = hf_ref[:, cols] + hb_ref[:, cols]
        hh = hh * lax.rsqrt(jnp.mean(hh * hh, axis=-1, keepdims=True) + EPS)
        y = hh * gain_ref[:, cols] * jax.nn.sigmoid(o_ref[:, cols].astype(F32))
        y_ref[:, cols] = y.astype(BF16)


def _mlstm(proj, gates, gates_t, gain, state, seq_len, n_seq, row_block_off):
    has_state = state is not None
    tiles = seq_len // TILE
    off = row_block_off
    qkvo_specs = [pl.BlockSpec((seq_len, ML_WIDTH), functools.partial(lambda b, j: (off + b, j), j=j))
                  for j in range(4)]
    in_specs = qkvo_specs + [
        pl.BlockSpec((seq_len, ML_GATE_COLS), lambda b: (off + b, 0)),
        pl.BlockSpec((tiles, ML_GATE_COLS, TILE), lambda b: (off + b, 0, 0)),
        pl.BlockSpec((1, ML_WIDTH), lambda b: (0, 0)),
    ]
    args = [proj, proj, proj, proj, gates, gates_t, gain]
    if has_state:
        c0, n0, m0 = state
        in_specs += [
            pl.BlockSpec((1, 2, ML_HEADS, ML_HEAD_DIM, ML_HEAD_DIM), lambda b: (b, 0, 0, 0, 0)),
            pl.BlockSpec((1, 2, ML_HEADS, ML_HEAD_DIM), lambda b: (b, 0, 0, 0)),
            pl.BlockSpec((1, 2 * ML_HEADS, 1), lambda b: (b, 0, 0)),
        ]
        args += [c0, n0, m0]
    out_shape = (jax.ShapeDtypeStruct((n_seq * seq_len, ML_WIDTH), BF16),
                 jax.ShapeDtypeStruct((n_seq, 2, ML_HEADS, ML_HEAD_DIM, ML_HEAD_DIM), F32),
                 jax.ShapeDtypeStruct((n_seq, 2, ML_HEADS, ML_HEAD_DIM), F32),
                 jax.ShapeDtypeStruct((n_seq, 2 * ML_HEADS, ML_HEAD_DIM), F32))
    out_specs = (pl.BlockSpec((seq_len, ML_WIDTH), lambda b: (b, 0)),
                 pl.BlockSpec((1, 2, ML_HEADS, ML_HEAD_DIM, ML_HEAD_DIM), lambda b: (b, 0, 0, 0, 0)),
                 pl.BlockSpec((1, 2, ML_HEADS, ML_HEAD_DIM), lambda b: (b, 0, 0, 0)),
                 pl.BlockSpec((1, 2 * ML_HEADS, ML_HEAD_DIM), lambda b: (b, 0, 0)))
    scratch = [pltpu.VMEM((tiles, ML_WIDTH, TILE), BF16),
               pltpu.VMEM((tiles, ML_WIDTH, TILE), F32), pltpu.VMEM((tiles, ML_WIDTH, TILE), F32),
               pltpu.VMEM((2 * ML_HEADS, ST_ROWS, ML_HEAD_DIM), F32),
               pltpu.VMEM((2 * ML_HEADS, 1, TILE), F32)]
    return pl.pallas_call(
        functools.partial(_mlstm_kernel, seq_len=seq_len, has_state=has_state),
        out_shape=out_shape, grid=(n_seq,), in_specs=in_specs, out_specs=out_specs,
        scratch_shapes=scratch, compiler_params=_cparams(("arbitrary",)),
        name=f"mlstm_{seq_len}",
    )(*args)


def _dft_mats(seq_len):
    lo = 32
    hi = seq_len // lo
    d = jnp.arange(seq_len, dtype=jnp.int32)
    a = jnp.arange(hi, dtype=jnp.int32) * lo
    b = jnp.arange(lo, dtype=jnp.int32)
    scale = math.pi / seq_len
    ang_a = ((a[:, None] * d[None, :]) % (2 * seq_len)).astype(F32) * scale
    ang_b = ((b[:, None] * d[None, :]) % (2 * seq_len)).astype(F32) * scale
    ca, sa, cb, sb = jnp.cos(ang_a), jnp.sin(ang_a), jnp.cos(ang_b), jnp.sin(ang_b)
    cosm = (ca[:, None, :] * cb[None, :, :] - sa[:, None, :] * sb[None, :, :]).reshape(seq_len, seq_len)
    sinm = (sa[:, None, :] * cb[None, :, :] + ca[:, None, :] * sb[None, :, :]).reshape(seq_len, seq_len)
    nyq = jnp.where(d % 2 == 0, 1.0, -1.0).astype(F32)
    krow = jnp.arange(seq_len, dtype=jnp.int32)[:, None]
    sinm = jnp.where(krow == 0, nyq[None, :], sinm)
    f = jnp.concatenate([cosm, sinm], axis=0).astype(BF16)
    cos_t = (ca.T[:, :, None] * cb.T[:, None, :] - sa.T[:, :, None] * sb.T[:, None, :]).reshape(seq_len, seq_len)
    sin_t = (sa.T[:, :, None] * cb.T[:, None, :] + ca.T[:, :, None] * sb.T[:, None, :]).reshape(seq_len, seq_len)
    kcol = jnp.arange(seq_len, dtype=jnp.int32)[None, :]
    sin_t = jnp.where(kcol == 0, nyq[:, None], sin_t)
    ft = jnp.concatenate([cos_t, sin_t], axis=1).astype(BF16)
    return f, ft


def _filter_feats(seq_len):
    t = jnp.linspace(0.0, 1.0, seq_len, dtype=F32)[:, None]
    wpos = 2.0 * math.pi * jnp.arange(seq_len, dtype=F32)[:, None] / seq_len
    bands = jnp.linspace(1e-4, HY_BANDS - 1, HY_BANDS, dtype=F32)[None, :]
    z = jnp.concatenate([t, jnp.cos(bands * wpos), -jnp.sin(bands * wpos)], axis=-1)
    return jnp.pad(z, ((0, 0), (0, 128 - HY_EMB)))


def _filter_kernel(z_ref, w1_ref, b1_ref, w2_ref, b2_ref, w3_ref, b3_ref, dec_ref, f_ref,
                   a_ref, b_ref, d_ref, *, seq_len):
    n = 2 * seq_len
    z = z_ref[...]
    h = jnp.sin(_dot3(z, w1_ref[...]) + b1_ref[...])
    h = jnp.sin(_dot3(h, w2_ref[...]) + b2_ref[...])
    h = _dot3(h, w3_ref[...]) + b3_ref[...]
    t = z[:, 0:1]
    h = h * (jnp.exp(-t * jnp.abs(dec_ref[...])) + HY_MOD_SHIFT)
    ss = jnp.sum(h * h, axis=0, keepdims=True)
    inv = lax.rsqrt(ss[:, :HY_WIDTH] + ss[:, HY_WIDTH:] + EPS)
    hp = h[:, :HY_WIDTH] * inv
    hn = h[:, HY_WIDTH:] * inv
    ssum = hp + hn
    sdif = hp - hn
    hc = _dot(f_ref[0:seq_len, :], ssum.astype(BF16))
    hs = _dot(f_ref[seq_len:n, :], sdif.astype(BF16))
    di = lax.broadcasted_iota(jnp.int32, (seq_len, 1), 0)
    sgn = jnp.where(di % 2 == 0, 1.0, -1.0)
    nyq = jnp.sum(ssum * sgn, axis=0, keepdims=True)
    first = di == 0
    a_ref[0] = hc * jnp.where(first, 1.0 / n, 2.0 / n)
    b_ref[0] = jnp.where(first, 0.0, hs * (2.0 / n))
    d_ref[0] = jnp.where(first, nyq * (1.0 / n), hc * (2.0 / n))


def _hyena_filters(seq_len, f, w1p, b1, w2, b2, w3, b3, dec):
    z = _filter_feats(seq_len)
    hid = HY_FILTER_HIDDEN
    oc = 2 * HY_WIDTH
    full = lambda shape: pl.BlockSpec(shape, lambda o: tuple(0 for _ in shape))
    out = jax.ShapeDtypeStruct((HY_ORDER, seq_len, HY_WIDTH), F32)
    return pl.pallas_call(
        functools.partial(_filter_kernel, seq_len=seq_len),
        out_shape=(out, out, out),
        grid=(HY_ORDER,),
        in_specs=[full((seq_len, 128)), full((128, hid)), full((1, hid)), full((hid, hid)), full((1, hid)),
                  pl.BlockSpec((hid, oc), lambda o: (0, o)),
                  pl.BlockSpec((1, oc), lambda o: (0, o)),
                  pl.BlockSpec((1, oc), lambda o: (0, o)),
                  full((2 * seq_len, seq_len))],
        out_specs=tuple(pl.BlockSpec((1, seq_len, HY_WIDTH), lambda o: (o, 0, 0)) for _ in range(3)),
        compiler_params=_cparams(("arbitrary",)),
        name=f"hyena_filter_{seq_len}",
    )(z, w1p, b1, w2, b2, w3, b3, dec, f)


def _hyena_kernel(x1_ref, x2_ref, v_ref, cw1_ref, cw2_ref, cwv_ref, a_ref, b_ref, d_ref, bias_ref,
                  f_ref, ft_ref, z_ref, *, seq_len, width, seqs):
    rows = seqs * seq_len
    ti = lax.broadcasted_iota(jnp.int32, (rows, 1), 0)
    has_prev = (ti % width) != 0
    has_next = (ti % width) != (width - 1)

    def short_conv(x_ref, w_ref):
        x = x_ref[...].astype(F32)
        prev = jnp.where(has_prev, pltpu.roll(x, 1, axis=0), 0.0)
        nxt = jnp.where(has_next, pltpu.roll(x, rows - 1, axis=0), 0.0)
        return w_ref[0:1, :] * prev + w_ref[1:2, :] * x + w_ref[2:3, :] * nxt

    gates = (short_conv(x1_ref, cw1_ref), short_conv(x2_ref, cw2_ref))
    v = short_conv(v_ref, cwv_ref)
    for i in range(seqs):
        sl = slice(i * seq_len, (i + 1) * seq_len)
        z = v[sl]
        for o in range(HY_ORDER):
            u = _dot(f_ref[...], z.astype(BF16))
            ut = u[:seq_len]
            ub = u[seq_len:]
            a, b, dd = a_ref[o], b_ref[o], d_ref[o]
            yt = ut * a - ub * b
            yb = ut * b + ub * dd
            y = _dot(ft_ref[:, :seq_len], yt.astype(BF16)) + _dot(ft_ref[:, seq_len:], yb.astype(BF16))
            z = gates[o][sl] * (y + bias_ref[o:o + 1, :] * z)
        z_ref[sl, :] = z.astype(BF16)


def _hyena(proj, conv_w, coefs, hy_bias, f, ft, seq_len, n_seq, row_off, width, seqs):
    cb = 256
    nblk = HY_WIDTH // cb
    base = ML_QKVO_COLS // cb
    rows = seqs * seq_len
    off = row_off // rows
    a, b, d = coefs

    def col_spec(part):
        return pl.BlockSpec((rows, cb), lambda j, s: (off + s, base + part * nblk + j))

    def w_spec(part):
        return pl.BlockSpec((3, cb), lambda j, s: (0, part * nblk + j))

    coef_spec = pl.BlockSpec((HY_ORDER, seq_len, cb), lambda j, s: (0, 0, j))
    return pl.pallas_call(
        functools.partial(_hyena_kernel, seq_len=seq_len, width=width, seqs=seqs),
        out_shape=jax.ShapeDtypeStruct((n_seq * seq_len, HY_WIDTH), BF16),
        grid=(nblk, n_seq // seqs),
        in_specs=[col_spec(0), col_spec(1), col_spec(2), w_spec(0), w_spec(1), w_spec(2),
                  coef_spec, coef_spec, coef_spec,
                  pl.BlockSpec((HY_ORDER, cb), lambda j, s: (0, j)),
                  pl.BlockSpec((2 * seq_len, seq_len), lambda j, s: (0, 0)),
                  pl.BlockSpec((seq_len, 2 * seq_len), lambda j, s: (0, 0))],
        out_specs=pl.BlockSpec((rows, cb), lambda j, s: (s, j)),
        compiler_params=_cparams(("arbitrary", "arbitrary")),
        name=f"hyena_conv_{seq_len}",
    )(proj, proj, proj, conv_w, conv_w, conv_w, a, b, d, hy_bias, f, ft)


def _first_max(x, n):
    mx = jnp.max(x, axis=0, keepdims=True)
    row = lax.broadcasted_iota(jnp.int32, x.shape, 0).astype(F32)
    idx = jnp.min(jnp.where(x == mx, row, float(n)), axis=0, keepdims=True)
    return mx, idx.astype(jnp.int32)


ROUTER_ROWS = 32
PAIRS_PER_GROUP = 6
N_BUCKETS = N_GROUPS * PAIRS_PER_GROUP
PAIR_SLOTS = ((0, 1), (0, 2), (0, 3), (1, 3), (1, 2), (3, 2))
LANES = 128
H2_EXT = D_MODEL + LANES
ROW_TILE = 256
ROW_CAP = T_ALL + N_BUCKETS * ROW_TILE
N_ROW_TILES = ROW_CAP // ROW_TILE


def _outproj_kernel(xp_ref, xs_ref, yp_ref, ys_ref, zp_ref, zs_ref, m_ref, gn_ref, wo_ref, wr_ref, br_ref,
                    x1_ref, h2_ref, bid_ref):
    is_p = pl.program_id(0) < N_BIG_P
    wrh, wrl = _split2(wr_ref[...])
    for r in range(BIG_TILE // TILE):
        rows = slice(r * TILE, (r + 1) * TILE)
        x = jnp.where(is_p, xp_ref[rows, :], xs_ref[rows, :])
        yml = jnp.where(is_p, yp_ref[rows, :], ys_ref[rows, :])
        zz = jnp.where(is_p, zp_ref[rows, :], zs_ref[rows, :])
        y = _dot(yml, wo_ref[0:ML_WIDTH, :]) + _dot(zz, wo_ref[ML_WIDTH:, :])
        x1 = x + m_ref[0, 2:3, :] * _rms(y, gn_ref[1:2, :])
        x1_ref[rows, :] = x1
        h2 = _rms(x1, gn_ref[2:3, :]) * (1.0 + m_ref[0, 4:5, :]) + m_ref[0, 3:4, :]
        h2_ref[rows, 0:D_MODEL] = h2
        gate_cols, bucket = _route_tile(h2, wrh, wrl, br_ref[...])
        h2_ref[rows, D_MODEL:H2_EXT] = jnp.zeros((TILE, LANES), F32)
        h2_ref[rows, D_MODEL:D_MODEL + 8] = gate_cols
        bid_ref[r] = bucket


def _route_tile(h2, wrh, wrl, bias):
    h2h, h2l = _split2(h2)
    logits = _dot_nt(wrh, h2h) + _dot_nt(wrh, h2l) + _dot_nt(wrl, h2h) + bias
    lc = logits[0:N_GROUPS]
    mx, gi = _first_max(lc, N_GROUPS)
    p_grp = 1.0 / jnp.sum(jnp.exp(lc - mx), axis=0, keepdims=True)
    lsel = jnp.zeros((EXPERTS_PER_GROUP, TILE), F32)
    for g in range(N_GROUPS):
        lo = N_GROUPS + g * EXPERTS_PER_GROUP
        lsel = jnp.where(gi == g, logits[lo:lo + EXPERTS_PER_GROUP], lsel)
    l1, i1 = _first_max(lsel, EXPERTS_PER_GROUP)
    sub4 = lax.broadcasted_iota(jnp.int32, lsel.shape, 0)
    l2, i2 = _first_max(jnp.where(sub4 == i1, -jnp.inf, lsel), EXPERTS_PER_GROUP)
    e2 = jnp.exp(l2 - l1)
    w1 = p_grp / (1.0 + e2)
    w2 = p_grp * e2 / (1.0 + e2)
    lo_e = jnp.minimum(i1, i2)
    hi_e = jnp.maximum(i1, i2)
    pair = jnp.where(lo_e == 0, hi_e - 1, jnp.where(lo_e == 1, jnp.where(hi_e == 3, 3, 4), 5))
    slot_a = jnp.where(pair == 5, hi_e, lo_e)
    first_in_a = i1 == slot_a
    w_a = jnp.where(first_in_a, w1, w2)
    w_b = jnp.where(first_in_a, w2, w1)
    sub = lax.broadcasted_iota(jnp.int32, (8, TILE), 0)
    gate_rows = jnp.where(sub == 0, w_a, jnp.where(sub == 1, w_b, 0.0))
    return _rows_to_cols(gate_rows), gi * PAIRS_PER_GROUP + pair


def _outproj(xp, xs, yp, ys, zp, zs, mods3, g_norm, w_out, w_r, b_r):
    tps = DEC_SEQ // BIG_TILE
    per = BIG_TILE // TILE
    pidx = lambda i: (jnp.minimum(i, N_BIG_P - 1), 0)
    sidx = lambda i: (jnp.maximum(i - N_BIG_P, 0), 0)
    return pl.pallas_call(
        _outproj_kernel,
        out_shape=(jax.ShapeDtypeStruct((T_ALL, D_MODEL), F32),
                   jax.ShapeDtypeStruct((T_ALL, H2_EXT), F32),
                   jax.ShapeDtypeStruct((N_TILES, 1, TILE), jnp.int32)),
        grid=(N_BIG,),
        in_specs=[pl.BlockSpec((BIG_TILE, D_MODEL), pidx), pl.BlockSpec((BIG_TILE, D_MODEL), sidx),
                  pl.BlockSpec((BIG_TILE, ML_WIDTH), pidx), pl.BlockSpec((BIG_TILE, ML_WIDTH), sidx),
                  pl.BlockSpec((BIG_TILE, HY_WIDTH), pidx), pl.BlockSpec((BIG_TILE, HY_WIDTH), sidx),
                  pl.BlockSpec((1, N_MOD, D_MODEL), lambda i: (_mod_row_of_tile(i, tps, N_BIG_P), 0, 0)),
                  pl.BlockSpec((4, D_MODEL), lambda i: (0, 0)),
                  pl.BlockSpec((D_MODEL, D_MODEL), lambda i: (0, 0)),
                  pl.BlockSpec((ROUTER_ROWS, D_MODEL), lambda i: (0, 0)),
                  pl.BlockSpec((ROUTER_ROWS, 1), lambda i: (0, 0))],
        out_specs=(pl.BlockSpec((BIG_TILE, D_MODEL), lambda i: (i, 0)),
                   pl.BlockSpec((BIG_TILE, H2_EXT), lambda i: (i, 0)),
                   pl.BlockSpec((per, 1, TILE), lambda i: (i, 0, 0))),
        compiler_params=_cparams(("arbitrary",)),
        name="out_proj_router",
    )(xp, xs, yp, ys, zp, zs, mods3, g_norm, w_out, w_r, b_r)


def _route_kernel(bid_ref, pos_ref, meta_ref):
    nb = 32
    tm = float(ROW_TILE)
    sub = lax.broadcasted_iota(jnp.int32, (nb, TILE), 0)
    ri = lax.broadcasted_iota(jnp.int32, (TILE, TILE), 0)
    ci = lax.broadcasted_iota(jnp.int32, (TILE, TILE), 1)
    before = jnp.where(ri < ci, 1.0, 0.0).astype(BF16)

    def onehot(blk):
        return jnp.where(sub == bid_ref[blk], 1.0, 0.0)

    zeros = jnp.zeros((nb, 1), F32)
    cnt = lax.fori_loop(0, N_TILES, lambda blk, c: c + jnp.sum(onehot(blk), axis=1, keepdims=True), zeros)
    padded = jnp.floor((cnt + (tm - 1.0)) * (1.0 / tm)) * tm
    r32 = lax.broadcasted_iota(jnp.int32, (nb, nb), 0)
    c32 = lax.broadcasted_iota(jnp.int32, (nb, nb), 1)
    padded_row = jnp.sum(jnp.where(r32 == c32, padded, 0.0), axis=0, keepdims=True)
    offs = jnp.sum(jnp.where(c32 < r32, padded_row, 0.0), axis=1, keepdims=True)
    ends = offs + padded

    def place(blk, seen):
        oh = onehot(blk)
        rank = _dot(oh.astype(BF16), before)
        pos = jnp.sum(oh * (rank + seen + offs), axis=0, keepdims=True)
        pos_ref[blk] = pos.astype(jnp.int32)
        return seen + jnp.sum(oh, axis=1, keepdims=True)

    lax.fori_loop(0, N_TILES, place, zeros)

    start = lax.broadcasted_iota(jnp.int32, (nb, 128), 1).astype(F32) * tm
    bsub = lax.broadcasted_iota(jnp.int32, (nb, 128), 0)
    done = jnp.where((bsub < N_BUCKETS) & (ends <= start), 1.0, 0.0)
    tb = jnp.sum(done, axis=0, keepdims=True)
    valid = jnp.where(tb < N_BUCKETS, 1.0, 0.0)
    tbc = jnp.minimum(tb, N_BUCKETS - 1.0)
    grp = jnp.floor((tbc + 0.5) * (1.0 / PAIRS_PER_GROUP))
    pair = tbc - PAIRS_PER_GROUP * grp
    loc_a = jnp.zeros_like(pair)
    loc_b = jnp.zeros_like(pair)
    for k, (sa, sb) in enumerate(PAIR_SLOTS):
        loc_a = jnp.where(pair == k, float(sa), loc_a)
        loc_b = jnp.where(pair == k, float(sb), loc_b)
    mine = bsub.astype(F32) == tbc
    used = jnp.sum(jnp.where(mine, offs + cnt, 0.0), axis=0, keepdims=True)
    n_rows = jnp.clip(used - start[0:1], 0.0, tm) * valid
    row8 = lax.broadcasted_iota(jnp.int32, (8, 128), 0)
    meta = jnp.where(row8 == 0, grp * EXPERTS_PER_GROUP + loc_a,
                     jnp.where(row8 == 1, grp * EXPERTS_PER_GROUP + loc_b,
                               jnp.where(row8 == 2, valid, jnp.where(row8 == 3, n_rows, 0.0))))
    meta_ref[...] = meta.astype(jnp.int32)


def _route(bid):
    return pl.pallas_call(
        _route_kernel,
        out_shape=(jax.ShapeDtypeStruct((N_TILES, 1, TILE), jnp.int32),
                   jax.ShapeDtypeStruct((8, 128), jnp.int32)),
        compiler_params=pltpu.CompilerParams(vmem_limit_bytes=VMEM_LIMIT),
        name="moe_route",
    )(bid)


def _moe_kernel(meta_ref, pos_ref, h2_hbm, wga_ref, wua_ref, wda_ref, wgb_ref, wub_ref, wdb_ref,
                y_ref, src_ref, xbuf, sem, wga_s, wua_s, wda_s, wgb_s, wub_s, wdb_s):
    j = pl.program_id(0)

    def row_copy(tile, r, slot):
        tok = src_ref[tile * ROW_TILE + r]
        return pltpu.make_async_copy(h2_hbm.at[pl.ds(tok, 1), :], xbuf.at[slot, pl.ds(r, 1), :], sem.at[slot])

    group = 8

    def row_groups(tile):
        return (meta_ref[3, tile] + (group - 1)) // group

    def issue(tile, slot):
        def body(g, c):
            for k in range(group):
                row_copy(tile, g * group + k, slot).start()
            return c
        lax.fori_loop(0, row_groups(tile), body, 0)

    def wait(tile, slot):
        def body(g, c):
            for k in range(group):
                row_copy(tile, g * group + k, slot).wait()
            return c
        lax.fori_loop(0, row_groups(tile), body, 0)

    @pl.when(j == 0)
    def _():
        xbuf[...] = jnp.zeros_like(xbuf)

        def clear(p, c):
            src_ref[p] = 0
            return c
        lax.fori_loop(0, ROW_CAP, clear, 0, unroll=8)

        def invert(t, c):
            src_ref[pos_ref[t]] = t
            return c
        lax.fori_loop(0, T_ALL, invert, 0, unroll=8)

        @pl.when(meta_ref[2, 0] == 1)
        def _():
            issue(0, 0)

    nxt = jnp.minimum(j + 1, N_ROW_TILES - 1)

    @pl.when((j + 1 < N_ROW_TILES) & (meta_ref[2, nxt] == 1))
    def _():
        issue(nxt, nxt % 2)

    valid = meta_ref[2, j] == 1
    prev = jnp.maximum(j - 1, 0)

    @pl.when(valid & ((j == 0) | (meta_ref[0, j] != meta_ref[0, prev])))
    def _():
        wga_s[...] = wga_ref[0].astype(BF16)
        wua_s[...] = wua_ref[0].astype(BF16)
        wda_s[...] = wda_ref[0].astype(BF16)

    @pl.when(valid & ((j == 0) | (meta_ref[1, j] != meta_ref[1, prev])))
    def _():
        wgb_s[...] = wgb_ref[0].astype(BF16)
        wub_s[...] = wub_ref[0].astype(BF16)
        wdb_s[...] = wdb_ref[0].astype(BF16)

    @pl.when(valid)
    def _():
        slot = j % 2
        wait(j, slot)
        x = xbuf[slot, :, 0:D_MODEL].astype(BF16)
        gates = xbuf[slot, :, D_MODEL:H2_EXT]

        def expert(wg, wu, gate):
            hg = _dot(x, wg[...])
            hu = _dot(x, wu[...])
            return (hg * jax.nn.sigmoid(hg) * hu * gate).astype(BF16)

        act_a = expert(wga_s, wua_s, gates[:, 0:1])
        act_b = expert(wgb_s, wub_s, gates[:, 1:2])
        y_ref[...] = _dot(act_a, wda_s[...]) + _dot(act_b, wdb_s[...])

    @pl.when(jnp.logical_not(valid))
    def _():
        y_ref[...] = jnp.zeros_like(y_ref)


def _moe(meta, pos, h2ext, w_gate, w_up, w_down):
    up_spec = lambda slot: pl.BlockSpec((1, D_MODEL, EXPERT_FF), lambda j, meta, pos: (meta[slot, j], 0, 0))
    down_spec = lambda slot: pl.BlockSpec((1, EXPERT_FF, D_MODEL), lambda j, meta, pos: (meta[slot, j], 0, 0))
    grid_spec = pltpu.PrefetchScalarGridSpec(
        num_scalar_prefetch=2,
        grid=(N_ROW_TILES,),
        in_specs=[pl.BlockSpec(memory_space=pl.ANY),
                  up_spec(0), up_spec(0), down_spec(0), up_spec(1), up_spec(1), down_spec(1)],
        out_specs=pl.BlockSpec((ROW_TILE, D_MODEL), lambda j, meta, pos: (j, 0)),
        scratch_shapes=[pltpu.SMEM((ROW_CAP,), jnp.int32),
                        pltpu.VMEM((2, ROW_TILE, H2_EXT), F32),
                        pltpu.SemaphoreType.DMA((2,)),
                        pltpu.VMEM((D_MODEL, EXPERT_FF), BF16), pltpu.VMEM((D_MODEL, EXPERT_FF), BF16),
                        pltpu.VMEM((EXPERT_FF, D_MODEL), BF16),
                        pltpu.VMEM((D_MODEL, EXPERT_FF), BF16), pltpu.VMEM((D_MODEL, EXPERT_FF), BF16),
                        pltpu.VMEM((EXPERT_FF, D_MODEL), BF16)])
    return pl.pallas_call(
        _moe_kernel,
        out_shape=jax.ShapeDtypeStruct((ROW_CAP, D_MODEL), F32),
        grid_spec=grid_spec,
        compiler_params=_cparams(("arbitrary",)),
        name="moe_experts",
    )(meta, pos, h2ext, w_gate, w_up, w_down, w_gate, w_up, w_down)


def _final_kernel(pos_ref, y_hbm, x1_ref, m_ref, gn_ref, op_ref, os_ref, ybuf, sem):
    i = pl.program_id(0)

    def row_copy(tile, r, slot):
        p = pos_ref[tile * TILE + r]
        return pltpu.make_async_copy(y_hbm.at[pl.ds(p, 1), :], ybuf.at[slot, pl.ds(r, 1), :], sem.at[slot])

    def issue(tile, slot):
        def body(r2, c):
            row_copy(tile, 2 * r2, slot).start(priority=0)
            row_copy(tile, 2 * r2 + 1, slot).start(priority=1)
            return c
        lax.fori_loop(0, TILE // 2, body, 0, unroll=4)

    def wait(slot):
        pltpu.make_async_copy(y_hbm.at[pl.ds(0, TILE)], ybuf.at[slot], sem.at[slot]).wait()

    @pl.when(i == 0)
    def _():
        issue(0, 0)

    @pl.when(i + 1 < N_TILES)
    def _():
        issue(i + 1, (i + 1) % 2)

    slot = i % 2
    wait(slot)
    out = x1_ref[...] + m_ref[0, 5:6, :] * _rms(ybuf[slot], gn_ref[3:4, :])

    @pl.when(i < N_TILES_P)
    def _():
        op_ref[...] = out

    @pl.when(i >= N_TILES_P)
    def _():
        os_ref[...] = out


def _final(pos, y_sorted, x1, mods3, g_norm):
    tps = DEC_SEQ // TILE
    grid_spec = pltpu.PrefetchScalarGridSpec(
        num_scalar_prefetch=1,
        grid=(N_TILES,),
        in_specs=[pl.BlockSpec(memory_space=pl.ANY),
                  pl.BlockSpec((TILE, D_MODEL), lambda i, pos: (i, 0)),
                  pl.BlockSpec((1, N_MOD, D_MODEL), lambda i, pos: (_mod_row_of_tile(i, tps, N_TILES_P), 0, 0)),
                  pl.BlockSpec((4, D_MODEL), lambda i, pos: (0, 0))],
        out_specs=(pl.BlockSpec((TILE, D_MODEL), lambda i, pos: (jnp.minimum(i, N_TILES_P - 1), 0)),
                   pl.BlockSpec((TILE, D_MODEL), lambda i, pos: (jnp.maximum(i - N_TILES_P, 0), 0))),
        scratch_shapes=[pltpu.VMEM((2, TILE, D_MODEL), F32), pltpu.SemaphoreType.DMA((2,))])
    return pl.pallas_call(
        _final_kernel,
        out_shape=(jax.ShapeDtypeStruct((T_PROMPT, D_MODEL), F32),
                   jax.ShapeDtypeStruct((T_SAMPLE, D_MODEL), F32)),
        grid_spec=grid_spec,
        compiler_params=_cparams(("arbitrary",)),
        name="moe_combine_final",
    )(pos, y_sorted, x1, mods3, g_norm)


def kernel(x_prompt, x_sample, state_C, state_n, state_m, c, c_ctx, w_ada, b_ada, g_norm, w_in, ml_gate_bias, ml_head_gain, hy_conv_w, hy_f_w1, hy_f_b1, hy_f_w2, hy_f_b2, hy_f_w3, hy_f_b3, hy_decay, hy_bias, w_out, w_rc, b_rc, w_rf, b_rf, w_gate, w_up, w_down):
    xp = x_prompt.reshape(T_PROMPT, D_MODEL)
    xs = x_sample.reshape(T_SAMPLE, D_MODEL)
    gn = g_norm[0]

    cv = jnp.concatenate([c_ctx[None, :], c, jnp.zeros((MOD_ROWS - 1 - DEC_BATCH, D_MODEL), F32)], axis=0)
    mods3 = _ada(cv, w_ada[0], b_ada[0]).reshape(MOD_ROWS, N_MOD, D_MODEL)

    w_in0 = w_in[0]
    w_hy = w_in0[:, ML_QKVO_COLS + ML_GATE_COLS:]
    wg = w_in0[:, ML_QKVO_COLS:ML_QKVO_COLS + ML_GATE_COLS]
    gbt = ml_gate_bias[0].reshape(ML_GATE_COLS, 1)
    proj, gates, gates_t = _inproj(xp, xs, mods3, gn, w_in0, w_hy, wg.T, gbt)

    gain = ml_head_gain[0].reshape(1, ML_WIDTH)
    y_ml_p, c_new, n_new, m_new = _mlstm(proj, gates, gates_t, gain, None, SEQ, BATCH, 0)
    state = (state_C[:, 0], state_n[:, 0], state_m[:, 0].reshape(DEC_BATCH, 2 * ML_HEADS, 1))
    y_ml_s, _, _, _ = _mlstm(proj, gates, gates_t, gain, state, DEC_SEQ, DEC_BATCH, T_PROMPT // DEC_SEQ)

    w1p = jnp.pad(hy_f_w1[0], ((0, 128 - HY_EMB), (0, 0)))
    b1 = hy_f_b1[0].reshape(1, -1)
    b2 = hy_f_b2[0].reshape(1, -1)
    b3 = hy_f_b3[0].reshape(1, -1)
    dec = hy_decay[0].reshape(1, -1)
    z_parts = []
    for seq_len, n_seq, row_off, width, seqs in ((SEQ, BATCH, 0, SEQ, 4), (DEC_SEQ, DEC_BATCH, T_PROMPT, GRID_W, 2)):
        f, ft = _dft_mats(seq_len)
        coefs = _hyena_filters(seq_len, f, w1p, b1, hy_f_w2[0], b2, hy_f_w3[0], b3, dec)
        z_parts.append(_hyena(proj, hy_conv_w[0], coefs, hy_bias[0], f, ft, seq_len, n_seq, row_off, width, seqs))
    z_p, z_s = z_parts

    pad_r = ROUTER_ROWS - N_GROUPS - N_EXPERTS
    w_r = jnp.pad(jnp.concatenate([w_rc[0], w_rf[0]], axis=1).T, ((0, pad_r), (0, 0)))
    b_r = jnp.pad(jnp.concatenate([b_rc[0], b_rf[0]], axis=0), (0, pad_r)).reshape(ROUTER_ROWS, 1)
    x1, h2ext, bid = _outproj(xp, xs, y_ml_p, y_ml_s, z_p, z_s, mods3, gn, w_out[0].astype(BF16), w_r, b_r)

    pos3, meta = _route(bid)
    pos = pos3.reshape(T_ALL)
    y_sorted = _moe(meta, pos, h2ext, w_gate[0], w_up[0], w_down[0])
    y_p, y_s = _final(pos, y_sorted, x1, mods3, gn)

    new_c = c_new.reshape(BATCH, 1, 2, ML_HEADS, ML_HEAD_DIM, ML_HEAD_DIM)
    new_n = n_new.reshape(BATCH, 1, 2, ML_HEADS, ML_HEAD_DIM)
    new_m = m_new[:, :, 0].reshape(BATCH, 1, 2, ML_HEADS)
    return (y_p.reshape(BATCH, SEQ, D_MODEL), y_s.reshape(DEC_BATCH, DEC_SEQ, D_MODEL), new_c, new_n, new_m)
```

```python
import functools
import math

import jax
import jax.numpy as jnp
import numpy as np
from jax import lax
from jax.experimental import pallas as pl
from jax.experimental.pallas import tpu as pltpu

F32 = jnp.float32
BF16 = jnp.bfloat16

D_MODEL = 1024
BATCH = 16
SEQ = 256
DEC_BATCH = 4
DEC_SEQ = 1024
GRID_W = 64
ML_WIDTH = 512
ML_HEADS = 4
ML_HEAD_DIM = 128
HY_WIDTH = 512
HY_ORDER = 2
HY_EMB = 33
HY_BANDS = 16
HY_FILTER_HIDDEN = 64
HY_MOD_SHIFT = 0.05
N_GROUPS = 4
EXPERTS_PER_GROUP = 4
N_EXPERTS = 16
EXPERT_FF = 512
N_MOD = 6
EPS = 1e-6
ML_QKVO_COLS = 4 * ML_WIDTH
ML_GATE_COLS = 4 * ML_HEADS
HY_COLS = 3 * HY_WIDTH
MAIN_COLS = ML_QKVO_COLS + HY_COLS

T_PROMPT = BATCH * SEQ
T_SAMPLE = DEC_BATCH * DEC_SEQ
T_ALL = T_PROMPT + T_SAMPLE
TILE = 256
N_TILES_P = T_PROMPT // TILE
N_TILES = T_ALL // TILE
MOD_ROWS = 8
K_SCALE = ML_HEAD_DIM ** -0.5
VMEM_LIMIT = 56 * 1024 * 1024


def _cparams(sem):
    return pltpu.CompilerParams(dimension_semantics=sem, vmem_limit_bytes=VMEM_LIMIT)


def _split2(x):
    hi = x.astype(BF16)
    lo = (x - hi.astype(F32)).astype(BF16)
    return hi, lo


def _dot(a, b):
    return jnp.dot(a, b, preferred_element_type=F32)


def _dot_nt(a, b):
    return lax.dot_general(a, b, (((1,), (1,)), ((), ())), preferred_element_type=F32)


def _dot_tn(a, b):
    return lax.dot_general(a, b, (((0,), (0,)), ((), ())), preferred_element_type=F32)


def _dot3(a, b):
    ah, al = _split2(a)
    bh, bl = _split2(b)
    return _dot(ah, bh) + _dot(al, bh) + _dot(ah, bl)


def _dot3_nt(a, b):
    ah, al = _split2(a)
    bh, bl = _split2(b)
    return _dot_nt(ah, bh) + _dot_nt(al, bh) + _dot_nt(ah, bl)


def _dot_exact_lhs(t, x):
    x1 = x.astype(BF16)
    r1 = x - x1.astype(F32)
    x2 = r1.astype(BF16)
    x3 = (r1 - x2.astype(F32)).astype(BF16)
    return _dot(t, x1) + _dot(t, x2) + _dot(t, x3)


def _dot_exact_rhs(x, t):
    x1 = x.astype(BF16)
    r1 = x - x1.astype(F32)
    x2 = r1.astype(BF16)
    x3 = (r1 - x2.astype(F32)).astype(BF16)
    return _dot(x1, t) + _dot(x2, t) + _dot(x3, t)


def _rms(x, g):
    return x * lax.rsqrt(jnp.mean(x * x, axis=-1, keepdims=True) + EPS) * g


def _mod_row_of_tile(i, tiles_per_sample_seq, n_prompt_tiles):
    return jnp.where(i < n_prompt_tiles, 0, 1 + (i - n_prompt_tiles) // tiles_per_sample_seq)


def _ada_kernel(cv_ref, w_ref, b_ref, o_ref):
    cv = cv_ref[...]
    s = cv * jax.nn.sigmoid(cv)
    o_ref[...] = _dot3(s, w_ref[...]) + b_ref[...]


def _ada(cv, w_ada, b_ada):
    n = N_MOD * D_MODEL
    return pl.pallas_call(
        _ada_kernel,
        out_shape=jax.ShapeDtypeStruct((MOD_ROWS, n), F32),
        grid=(N_MOD,),
        in_specs=[pl.BlockSpec((MOD_ROWS, D_MODEL), lambda j: (0, 0)),
                  pl.BlockSpec((D_MODEL, D_MODEL), lambda j: (0, j)),
                  pl.BlockSpec((1, D_MODEL), lambda j: (0, j))],
        out_specs=pl.BlockSpec((MOD_ROWS, D_MODEL), lambda j: (0, j)),
        compiler_params=_cparams(("arbitrary",)),
        name="ada_mod",
    )(cv, w_ada, b_ada.reshape(1, n))


def _log_sigmoid(x):
    return jnp.minimum(x, 0.0) - jnp.log1p(jnp.exp(-jnp.abs(x)))


def _rows_to_cols(rows):
    ri = lax.broadcasted_iota(jnp.int32, (TILE, TILE), 0)
    ci = lax.broadcasted_iota(jnp.int32, (TILE, TILE), 1)
    eye = jnp.where(ri == ci, 1.0, 0.0).astype(BF16)
    p1 = rows.astype(BF16)
    r1 = rows - p1.astype(F32)
    p2 = r1.astype(BF16)
    p3 = (r1 - p2.astype(F32)).astype(BF16)
    return _dot_nt(eye, p1) + _dot_nt(eye, p2) + _dot_nt(eye, p3)


BIG_TILE = 2 * TILE
N_BIG_P = T_PROMPT // BIG_TILE
N_BIG = T_ALL // BIG_TILE


def _inproj_kernel(xp_ref, xs_ref, m_ref, gn_ref, wq_ref, wh_ref, wgt_ref, gbt_ref, proj_ref, gate_ref, gatet_ref):
    i = pl.program_id(0)
    x = jnp.where(i < N_TILES_P, xp_ref[...], xs_ref[...])
    h = _rms(x, gn_ref[0:1, :]) * (1.0 + m_ref[0, 1:2, :]) + m_ref[0, 0:1, :]
    hb = h.astype(BF16)
    cb = 512
    for j in range(ML_QKVO_COLS // cb):
        proj_ref[:, j * cb:(j + 1) * cb] = _dot(hb, wq_ref[:, j * cb:(j + 1) * cb]).astype(BF16)
    for j in range(HY_COLS // cb):
        lo = ML_QKVO_COLS + j * cb
        proj_ref[:, lo:lo + cb] = _dot(hb, wh_ref[:, j * cb:(j + 1) * cb]).astype(BF16)
    hl = (h - hb.astype(F32)).astype(BF16)
    wth, wtl = _split2(wgt_ref[...])
    gt = _dot_nt(wth, hb) + _dot_nt(wth, hl) + _dot_nt(wtl, hb) + gbt_ref[...]
    row = lax.broadcasted_iota(jnp.int32, gt.shape, 0)
    gt = jnp.where((row % 8) >= 4, _log_sigmoid(gt), gt)
    gatet_ref[0] = gt
    gate_ref[...] = _rows_to_cols(gt)


def _inproj(xp, xs, mods3, g_norm, w_qkvo, w_hy, wgt, gbt):
    tps = DEC_SEQ // TILE
    return pl.pallas_call(
        _inproj_kernel,
        out_shape=(jax.ShapeDtypeStruct((T_ALL, MAIN_COLS), BF16),
                   jax.ShapeDtypeStruct((T_ALL, ML_GATE_COLS), F32),
                   jax.ShapeDtypeStruct((N_TILES, ML_GATE_COLS, TILE), F32)),
        grid=(N_TILES,),
        in_specs=[pl.BlockSpec((TILE, D_MODEL), lambda i: (jnp.minimum(i, N_TILES_P - 1), 0)),
                  pl.BlockSpec((TILE, D_MODEL), lambda i: (jnp.maximum(i - N_TILES_P, 0), 0)),
                  pl.BlockSpec((1, N_MOD, D_MODEL), lambda i: (_mod_row_of_tile(i, tps, N_TILES_P), 0, 0)),
                  pl.BlockSpec((4, D_MODEL), lambda i: (0, 0)),
                  pl.BlockSpec((D_MODEL, ML_QKVO_COLS), lambda i: (0, 0)),
                  pl.BlockSpec((D_MODEL, HY_COLS), lambda i: (0, 0)),
                  pl.BlockSpec((ML_GATE_COLS, D_MODEL), lambda i: (0, 0)),
                  pl.BlockSpec((ML_GATE_COLS, 1), lambda i: (0, 0))],
        out_specs=(pl.BlockSpec((TILE, MAIN_COLS), lambda i: (i, 0)),
                   pl.BlockSpec((TILE, ML_GATE_COLS), lambda i: (i, 0)),
                   pl.BlockSpec((1, ML_GATE_COLS, TILE), lambda i: (i, 0, 0))),
        compiler_params=_cparams(("arbitrary",)),
        name="in_proj",
    )(xp, xs, mods3, g_norm, w_qkvo, w_hy, wgt, gbt)


ST_ROWS = ML_HEAD_DIM + 16


def _mlstm_kernel(*refs, seq_len, has_state):
    if has_state:
        (q_ref, k_ref, v_ref, o_ref, g_ref, gt_ref, gain_ref, c0_ref, n0_ref, m0_ref,
         y_ref, c_ref, n_ref, m_ref, vt_ref, hf_ref, hb_ref, st_ref, ms_ref) = refs
    else:
        (q_ref, k_ref, v_ref, o_ref, g_ref, gt_ref, gain_ref,
         y_ref, c_ref, n_ref, m_ref, vt_ref, hf_ref, hb_ref, st_ref, ms_ref) = refs
    ch = TILE
    nc = seq_len // ch
    hd = ML_HEAD_DIM
    key = lax.broadcasted_iota(jnp.int32, (ch, ch), 0)
    qry = lax.broadcasted_iota(jnp.int32, (ch, ch), 1)
    key_le = key <= qry
    key_ge = key >= qry
    t_le = jnp.where(key_le, 1.0, 0.0).astype(BF16)
    t_ge = jnp.where(key_ge, 1.0, 0.0).astype(BF16)
    sub16 = lax.broadcasted_iota(jnp.int32, (16, ch), 0)
    ln_scale = math.log(K_SCALE)

    for c in range(nc):
        for h in range(ML_HEADS):
            cols = slice(h * hd, (h + 1) * hd)
            vt_ref[c, cols, :] = v_ref[c * ch:(c + 1) * ch, cols].astype(F32).T.astype(BF16)

    for d in range(2):
        for h in range(ML_HEADS):
            r = d * ML_HEADS + h
            st_ref[r] = jnp.zeros((ST_ROWS, hd), F32)
            if has_state:
                st_ref[r, 0:hd, :] = c0_ref[0, d, h].T
                st_ref[r, hd:hd + 1, :] = n0_ref[0, d, h:h + 1, :]
                ms_ref[r] = jnp.broadcast_to(m0_ref[0, r:r + 1, :], (1, ch))
            else:
                ms_ref[r] = jnp.zeros((1, ch), F32)

    def step(t, carry):
        for d in range(2):
            c = t if d == 0 else nc - 1 - t
            rows = pl.ds(pl.multiple_of(c * ch, ch), ch)
            gcol = g_ref[rows, :]
            grow = gt_ref[c]
            brow_all = _dot_exact_rhs(grow, t_le if d == 0 else t_ge)
            bcol_all = _dot_exact_lhs(t_ge if d == 0 else t_le, gcol)
            mask = key_le if d == 0 else key_ge
            hacc_ref = hf_ref if d == 0 else hb_ref
            for h in range(ML_HEADS):
                r = d * ML_HEADS + h
                fcol = (1 + 2 * d) * ML_HEADS + h
                icol = (2 * d) * ML_HEADS + h
                cols = slice(h * hd, (h + 1) * hd)
                q = q_ref[rows, cols]
                k = k_ref[rows, cols]
                vt = vt_ref[c, cols, :]
                st = st_ref[r]
                m_prev = ms_ref[r]
                b_row = brow_all[fcol:fcol + 1, :]
                ig_row = grow[icol:icol + 1, :]
                c_col = gcol[:, icol:icol + 1] - bcol_all[:, fcol:fcol + 1]
                logd = jnp.where(mask, b_row + c_col, -jnp.inf)
                inter = b_row + m_prev
                m_pos = jnp.maximum(inter, jnp.max(logd, axis=0, keepdims=True))
                s = _dot_nt(k, q) * jnp.exp(logd - (m_pos - ln_scale))
                sc_inter = jnp.exp(inter - m_pos)
                iq = _dot_nt(st.astype(BF16), q)
                num = sc_inter * iq[0:hd] + _dot(vt, s.astype(BF16))
                den = sc_inter * iq[hd:hd + 1] + jnp.sum(s, axis=0, keepdims=True)
                hacc_ref[c, cols, :] = num * (1.0 / jnp.maximum(jnp.abs(den), jnp.exp(-m_pos)))
                b_last = b_row[:, ch - 1:ch] if d == 0 else b_row[:, 0:1]
                logw = b_last - b_row + ig_row
                m_new = jnp.maximum(b_last + m_prev, jnp.max(logw, axis=1, keepdims=True))
                w = jnp.exp(logw - (m_new - ln_scale))
                decay = jnp.exp(b_last + m_prev - m_new)
                lhs = jnp.concatenate([(vt.astype(F32) * w).astype(BF16),
                                       jnp.where(sub16 == 0, w, 0.0).astype(BF16)], axis=0)
                st_ref[r] = decay[:, 0:hd] * st + _dot(lhs, k)
                ms_ref[r] = m_new
        return carry

    lax.fori_loop(0, nc, step, 0)

    for d in range(2):
        for h in range(ML_HEADS):
            r = d * ML_HEADS + h
            c_ref[0, d, h] = st_ref[r, 0:hd, :].T
            n_ref[0, d, h:h + 1, :] = st_ref[r, hd:hd + 1, :]
            m_ref[0, r:r + 1, :] = ms_ref[r][:, 0:hd]
    for c in range(nc):
        for h in range(ML_HEADS):
            cols = slice(h * hd, (h + 1) * hd)
            ht = hf_ref[c, cols, :] + hb_ref[c, cols, :]
            ht = ht * lax.rsqrt(jnp.mean(ht * ht, axis=0, keepdims=True) + EPS)
            rows = slice(c * ch, (c + 1) * ch)
            y = ht.T * gain_ref[:, cols] * jax.nn.sigmoid(o_ref[rows, cols].astype(F32))
            y_ref[rows, cols] = y.astype(BF16)


def _mlstm_kernel_rows(*refs, seq_len, has_state):
    if has_state:
        (q_ref, k_ref, v_ref, o_ref, g_ref, gt_ref, gain_ref, c0_ref, n0_ref, m0_ref,
         y_ref, c_ref, n_ref, m_ref, hf_ref, hb_ref, cs_ref, ns_ref, ms_ref) = refs
    else:
        (q_ref, k_ref, v_ref, o_ref, g_ref, gt_ref, gain_ref,
         y_ref, c_ref, n_ref, m_ref, hf_ref, hb_ref, cs_ref, ns_ref, ms_ref) = refs
    ch = TILE
    nc = seq_len // ch
    hd = ML_HEAD_DIM
    ri = lax.broadcasted_iota(jnp.int32, (ch, ch), 0)
    ci = lax.broadcasted_iota(jnp.int32, (ch, ch), 1)
    lower = ci <= ri
    upper = ci >= ri
    t_low = jnp.where(lower, 1.0, 0.0).astype(BF16)
    t_up = jnp.where(upper, 1.0, 0.0).astype(BF16)

    for d in range(2):
        for h in range(ML_HEADS):
            r = d * ML_HEADS + h
            if has_state:
                cs_ref[r] = c0_ref[0, d, h]
                ns_ref[r] = n0_ref[0, d, h:h + 1, :]
                ms_ref[r] = jnp.broadcast_to(m0_ref[0, r:r + 1, :], (1, hd))
            else:
                cs_ref[r] = jnp.zeros((hd, hd), F32)
                ns_ref[r] = jnp.zeros((1, hd), F32)
                ms_ref[r] = jnp.zeros((1, hd), F32)

    def step(t, carry):
        for d in range(2):
            c = t if d == 0 else nc - 1 - t
            r0 = pl.multiple_of(c * ch, ch)
            rows = pl.ds(r0, ch)
            gcol = g_ref[rows, :]
            grow = gt_ref[c]
            tmat_c = t_low if d == 0 else t_up
            tmat_r = t_up if d == 0 else t_low
            bcol_all = _dot_exact_lhs(tmat_c, gcol)
            brow_all = _dot_exact_rhs(grow, tmat_r)
            mask = lower if d == 0 else upper
            hacc_ref = hf_ref if d == 0 else hb_ref
            for h in range(ML_HEADS):
                r = d * ML_HEADS + h
                fcol = (1 + 2 * d) * ML_HEADS + h
                icol = (2 * d) * ML_HEADS + h
                cols = slice(h * hd, (h + 1) * hd)
                q = q_ref[rows, cols]
                k = k_ref[rows, cols]
                v = v_ref[rows, cols]
                c_prev = cs_ref[r]
                n_prev = ns_ref[r]
                m_prev = ms_ref[r][:, 0:1]
                b_col = bcol_all[:, fcol:fcol + 1]
                b_row = brow_all[fcol:fcol + 1, :]
                ig_row = grow[icol:icol + 1, :]
                ig_col = gcol[:, icol:icol + 1]
                logd = jnp.where(mask, b_col - b_row + ig_row, -jnp.inf)
                inter = b_col + m_prev
                m_pos = jnp.maximum(inter, jnp.max(logd, axis=-1, keepdims=True))
                s = _dot_nt(q, k) * K_SCALE * jnp.exp(logd - m_pos)
                sc_inter = jnp.exp(inter - m_pos)
                qf = q.astype(F32)
                num = sc_inter * _dot(q, c_prev.astype(BF16)) + _dot(s.astype(BF16), v)
                den = (sc_inter * jnp.sum(qf * n_prev, axis=-1, keepdims=True)
                       + jnp.sum(s, axis=-1, keepdims=True))
                hh = num / jnp.maximum(jnp.abs(den), jnp.exp(-m_pos))
                hacc_ref[rows, cols] = hh
                b_last = b_col[ch - 1:ch, :] if d == 0 else b_col[0:1, :]
                logw = b_last - b_col + ig_col
                m_new = jnp.maximum(b_last + m_prev, jnp.max(logw, axis=0, keepdims=True))
                w = jnp.exp(logw - m_new)
                decay = jnp.exp(b_last + m_prev - m_new)
                kw = k.astype(F32) * (w * K_SCALE)
                cs_ref[r] = decay * c_prev + _dot_tn(kw.astype(BF16), v)
                ns_ref[r] = decay * n_prev + jnp.sum(kw, axis=0, keepdims=True)
                ms_ref[r] = jnp.broadcast_to(m_new, (1, hd))
        return carry

    lax.fori_loop(0, nc, step, 0)

    for d in range(2):
        for h in range(ML_HEADS):
            r = d * ML_HEADS + h
            c_ref[0, d, h] = cs_ref[r]
            n_ref[0, d, h:h + 1, :] = ns_ref[r]
            m_ref[0, r:r + 1, :] = ms_ref[r]
    for h in range(ML_HEADS):
        cols = slice(h * hd, (h + 1) * hd)
        hh = hf_ref[:, cols] + hb_ref[:, cols]
        hh = hh * lax.rsqrt(jnp.mean(hh * hh, axis=-1, keepdims=True) + EPS)
        y = hh * gain_ref[:, cols] * jax.nn.sigmoid(o_ref[:, cols].astype(F32))
        y_ref[:, cols] = y.astype(BF16)


def _mlstm(proj, gates, gates_t, gain, state, seq_len, n_seq, row_block_off):
    has_state = state is not None
    tiles = seq_len // TILE
    off = row_block_off
    qkvo_specs = [pl.BlockSpec((seq_len, ML_WIDTH), functools.partial(lambda b, j: (off + b, j), j=j))
                  for j in range(4)]
    in_specs = qkvo_specs + [
        pl.BlockSpec((seq_len, ML_GATE_COLS), lambda b: (off + b, 0)),
        pl.BlockSpec((tiles, ML_GATE_COLS, TILE), lambda b: (off + b, 0, 0)),
        pl.BlockSpec((1, ML_WIDTH), lambda b: (0, 0)),
    ]
    args = [proj, proj, proj, proj, gates, gates_t, gain]
    if has_state:
        c0, n0, m0 = state
        in_specs += [
            pl.BlockSpec((1, 2, ML_HEADS, ML_HEAD_DIM, ML_HEAD_DIM), lambda b: (b, 0, 0, 0, 0)),
            pl.BlockSpec((1, 2, ML_HEADS, ML_HEAD_DIM), lambda b: (b, 0, 0, 0)),
            pl.BlockSpec((1, 2 * ML_HEADS, 1), lambda b: (b, 0, 0)),
        ]
        args += [c0, n0, m0]
    out_shape = (jax.ShapeDtypeStruct((n_seq * seq_len, ML_WIDTH), BF16),
                 jax.ShapeDtypeStruct((n_seq, 2, ML_HEADS, ML_HEAD_DIM, ML_HEAD_DIM), F32),
                 jax.ShapeDtypeStruct((n_seq, 2, ML_HEADS, ML_HEAD_DIM), F32),
                 jax.ShapeDtypeStruct((n_seq, 2 * ML_HEADS, ML_HEAD_DIM), F32))
    out_specs = (pl.BlockSpec((seq_len, ML_WIDTH), lambda b: (b, 0)),
                 pl.BlockSpec((1, 2, ML_HEADS, ML_HEAD_DIM, ML_HEAD_DIM), lambda b: (b, 0, 0, 0, 0)),
                 pl.BlockSpec((1, 2, ML_HEADS, ML_HEAD_DIM), lambda b: (b, 0, 0, 0)),
                 pl.BlockSpec((1, 2 * ML_HEADS, ML_HEAD_DIM), lambda b: (b, 0, 0)))
    scratch = [pltpu.VMEM((tiles, ML_WIDTH, TILE), BF16),
               pltpu.VMEM((tiles, ML_WIDTH, TILE), F32), pltpu.VMEM((tiles, ML_WIDTH, TILE), F32),
               pltpu.VMEM((2 * ML_HEADS, ST_ROWS, ML_HEAD_DIM), F32),
               pltpu.VMEM((2 * ML_HEADS, 1, TILE), F32)]
    return pl.pallas_call(
        functools.partial(_mlstm_kernel, seq_len=seq_len, has_state=has_state),
        out_shape=out_shape, grid=(n_seq,), in_specs=in_specs, out_specs=out_specs,
        scratch_shapes=scratch, compiler_params=_cparams(("arbitrary",)),
        name=f"mlstm_{seq_len}",
    )(*args)


def _dft_mats(seq_len):
    k = np.arange(seq_len, dtype=np.int64)[:, None]
    d = np.arange(seq_len, dtype=np.int64)[None, :]
    ang = np.pi * ((k * d) % (2 * seq_len)).astype(np.float64) / seq_len
    sinm = np.sin(ang)
    sinm[0, :] = np.where(d[0] % 2 == 0, 1.0, -1.0)
    f = np.concatenate([np.cos(ang), sinm], axis=0).astype(np.float32)
    return jnp.asarray(f).astype(BF16), jnp.asarray(np.ascontiguousarray(f.T)).astype(BF16)


def _filter_feats(seq_len):
    t = np.linspace(0.0, 1.0, seq_len, dtype=np.float64)[:, None]
    wpos = 2.0 * np.pi * np.arange(seq_len, dtype=np.float64)[:, None] / seq_len
    bands = np.linspace(1e-4, HY_BANDS - 1, HY_BANDS, dtype=np.float64)[None, :]
    z = np.concatenate([t, np.cos(bands * wpos), -np.sin(bands * wpos)], axis=-1)
    return jnp.asarray(np.pad(z, ((0, 0), (0, 128 - HY_EMB))).astype(np.float32))


def _filter_kernel(z_ref, w1_ref, b1_ref, w2_ref, b2_ref, w3_ref, b3_ref, dec_ref, f_ref,
                   a_ref, b_ref, d_ref, *, seq_len):
    n = 2 * seq_len
    z = z_ref[...]
    h = jnp.sin(_dot3(z, w1_ref[...]) + b1_ref[...])
    h = jnp.sin(_dot3(h, w2_ref[...]) + b2_ref[...])
    h = _dot3(h, w3_ref[...]) + b3_ref[...]
    t = z[:, 0:1]
    h = h * (jnp.exp(-t * jnp.abs(dec_ref[...])) + HY_MOD_SHIFT)
    ss = jnp.sum(h * h, axis=0, keepdims=True)
    inv = lax.rsqrt(ss[:, :HY_WIDTH] + ss[:, HY_WIDTH:] + EPS)
    hp = h[:, :HY_WIDTH] * inv
    hn = h[:, HY_WIDTH:] * inv
    ssum = hp + hn
    sdif = hp - hn
    hc = _dot(f_ref[0:seq_len, :], ssum.astype(BF16))
    hs = _dot(f_ref[seq_len:n, :], sdif.astype(BF16))
    di = lax.broadcasted_iota(jnp.int32, (seq_len, 1), 0)
    sgn = jnp.where(di % 2 == 0, 1.0, -1.0)
    nyq = jnp.sum(ssum * sgn, axis=0, keepdims=True)
    first = di == 0
    a_ref[0] = hc * jnp.where(first, 1.0 / n, 2.0 / n)
    b_ref[0] = jnp.where(first, 0.0, hs * (2.0 / n))
    d_ref[0] = jnp.where(first, nyq * (1.0 / n), hc * (2.0 / n))


def _hyena_filters(seq_len, f, w1p, b1, w2, b2, w3, b3, dec):
    z = _filter_feats(seq_len)
    hid = HY_FILTER_HIDDEN
    oc = 2 * HY_WIDTH
    full = lambda shape: pl.BlockSpec(shape, lambda o: tuple(0 for _ in shape))
    out = jax.ShapeDtypeStruct((HY_ORDER, seq_len, HY_WIDTH), F32)
    return pl.pallas_call(
        functools.partial(_filter_kernel, seq_len=seq_len),
        out_shape=(out, out, out),
        grid=(HY_ORDER,),
        in_specs=[full((seq_len, 128)), full((128, hid)), full((1, hid)), full((hid, hid)), full((1, hid)),
                  pl.BlockSpec((hid, oc), lambda o: (0, o)),
                  pl.BlockSpec((1, oc), lambda o: (0, o)),
                  pl.BlockSpec((1, oc), lambda o: (0, o)),
                  full((2 * seq_len, seq_len))],
        out_specs=tuple(pl.BlockSpec((1, seq_len, HY_WIDTH), lambda o: (o, 0, 0)) for _ in range(3)),
        compiler_params=_cparams(("arbitrary",)),
        name=f"hyena_filter_{seq_len}",
    )(z, w1p, b1, w2, b2, w3, b3, dec, f)


def _hyena_kernel(x1_ref, x2_ref, v_ref, cw1_ref, cw2_ref, cwv_ref, a_ref, b_ref, d_ref, bias_ref,
                  f_ref, ft_ref, z_ref, *, seq_len, width, seqs):
    rows = seqs * seq_len
    ti = lax.broadcasted_iota(jnp.int32, (rows, 1), 0)
    has_prev = (ti % width) != 0
    has_next = (ti % width) != (width - 1)

    def short_conv(x_ref, w_ref):
        x = x_ref[...].astype(F32)
        prev = jnp.where(has_prev, pltpu.roll(x, 1, axis=0), 0.0)
        nxt = jnp.where(has_next, pltpu.roll(x, rows - 1, axis=0), 0.0)
        return w_ref[0:1, :] * prev + w_ref[1:2, :] * x + w_ref[2:3, :] * nxt

    gates = (short_conv(x1_ref, cw1_ref), short_conv(x2_ref, cw2_ref))
    v = short_conv(v_ref, cwv_ref)
    for i in range(seqs):
        sl = slice(i * seq_len, (i + 1) * seq_len)
        z = v[sl]
        for o in range(HY_ORDER):
            u = _dot(f_ref[...], z.astype(BF16))
            ut = u[:seq_len]
            ub = u[seq_len:]
            a, b, dd = a_ref[o], b_ref[o], d_ref[o]
            yt = ut * a - ub * b
            yb = ut * b + ub * dd
            y = _dot(ft_ref[:, :seq_len], yt.astype(BF16)) + _dot(ft_ref[:, seq_len:], yb.astype(BF16))
            z = gates[o][sl] * (y + bias_ref[o:o + 1, :] * z)
        z_ref[sl, :] = z.astype(BF16)


def _hyena(proj, conv_w, coefs, hy_bias, f, ft, seq_len, n_seq, row_off, width, seqs):
    cb = 256
    nblk = HY_WIDTH // cb
    base = ML_QKVO_COLS // cb
    rows = seqs * seq_len
    off = row_off // rows
    a, b, d = coefs

    def col_spec(part):
        return pl.BlockSpec((rows, cb), lambda j, s: (off + s, base + part * nblk + j))

    def w_spec(part):
        return pl.BlockSpec((3, cb), lambda j, s: (0, part * nblk + j))

    coef_spec = pl.BlockSpec((HY_ORDER, seq_len, cb), lambda j, s: (0, 0, j))
    return pl.pallas_call(
        functools.partial(_hyena_kernel, seq_len=seq_len, width=width, seqs=seqs),
        out_shape=jax.ShapeDtypeStruct((n_seq * seq_len, HY_WIDTH), BF16),
        grid=(nblk, n_seq // seqs),
        in_specs=[col_spec(0), col_spec(1), col_spec(2), w_spec(0), w_spec(1), w_spec(2),
                  coef_spec, coef_spec, coef_spec,
                  pl.BlockSpec((HY_ORDER, cb), lambda j, s: (0, j)),
                  pl.BlockSpec((2 * seq_len, seq_len), lambda j, s: (0, 0)),
                  pl.BlockSpec((seq_len, 2 * seq_len), lambda j, s: (0, 0))],
        out_specs=pl.BlockSpec((rows, cb), lambda j, s: (s, j)),
        compiler_params=_cparams(("arbitrary", "arbitrary")),
        name=f"hyena_conv_{seq_len}",
    )(proj, proj, proj, conv_w, conv_w, conv_w, a, b, d, hy_bias, f, ft)


def _first_max(x, n):
    mx = jnp.max(x, axis=0, keepdims=True)
    row = lax.broadcasted_iota(jnp.int32, x.shape, 0).astype(F32)
    idx = jnp.min(jnp.where(x == mx, row, float(n)), axis=0, keepdims=True)
    return mx, idx.astype(jnp.int32)


ROUTER_ROWS = 32
PAIRS_PER_GROUP = 6
N_BUCKETS = N_GROUPS * PAIRS_PER_GROUP
PAIR_SLOTS = ((0, 1), (0, 2), (0, 3), (1, 3), (1, 2), (3, 2))
LANES = 128
H2_EXT = D_MODEL + LANES
ROW_TILE = 256
ROW_CAP = T_ALL + N_BUCKETS * ROW_TILE
N_ROW_TILES = ROW_CAP // ROW_TILE


def _outproj_kernel(xp_ref, xs_ref, yp_ref, ys_ref, zp_ref, zs_ref, m_ref, gn_ref, wo_ref, wr_ref, br_ref,
                    x1_ref, h2_ref, bid_ref):
    is_p = pl.program_id(0) < N_BIG_P
    wrh, wrl = _split2(wr_ref[...])
    for r in range(BIG_TILE // TILE):
        rows = slice(r * TILE, (r + 1) * TILE)
        x = jnp.where(is_p, xp_ref[rows, :], xs_ref[rows, :])
        yml = jnp.where(is_p, yp_ref[rows, :], ys_ref[rows, :])
        zz = jnp.where(is_p, zp_ref[rows, :], zs_ref[rows, :])
        y = _dot(yml, wo_ref[0:ML_WIDTH, :]) + _dot(zz, wo_ref[ML_WIDTH:, :])
        x1 = x + m_ref[0, 2:3, :] * _rms(y, gn_ref[1:2, :])
        x1_ref[rows, :] = x1
        h2 = _rms(x1, gn_ref[2:3, :]) * (1.0 + m_ref[0, 4:5, :]) + m_ref[0, 3:4, :]
        h2_ref[rows, 0:D_MODEL] = h2
        gate_cols, bucket = _route_tile(h2, wrh, wrl, br_ref[...])
        h2_ref[rows, D_MODEL:H2_EXT] = jnp.zeros((TILE, LANES), F32)
        h2_ref[rows, D_MODEL:D_MODEL + 8] = gate_cols
        bid_ref[r] = bucket


def _route_tile(h2, wrh, wrl, bias):
    h2h, h2l = _split2(h2)
    logits = _dot_nt(wrh, h2h) + _dot_nt(wrh, h2l) + _dot_nt(wrl, h2h) + bias
    lc = logits[0:N_GROUPS]
    mx, gi = _first_max(lc, N_GROUPS)
    p_grp = 1.0 / jnp.sum(jnp.exp(lc - mx), axis=0, keepdims=True)
    lsel = jnp.zeros((EXPERTS_PER_GROUP, TILE), F32)
    for g in range(N_GROUPS):
        lo = N_GROUPS + g * EXPERTS_PER_GROUP
        lsel = jnp.where(gi == g, logits[lo:lo + EXPERTS_PER_GROUP], lsel)
    l1, i1 = _first_max(lsel, EXPERTS_PER_GROUP)
    sub4 = lax.broadcasted_iota(jnp.int32, lsel.shape, 0)
    l2, i2 = _first_max(jnp.where(sub4 == i1, -jnp.inf, lsel), EXPERTS_PER_GROUP)
    e2 = jnp.exp(l2 - l1)
    w1 = p_grp / (1.0 + e2)
    w2 = p_grp * e2 / (1.0 + e2)
    lo_e = jnp.minimum(i1, i2)
    hi_e = jnp.maximum(i1, i2)
    pair = jnp.where(lo_e == 0, hi_e - 1, jnp.where(lo_e == 1, jnp.where(hi_e == 3, 3, 4), 5))
    slot_a = jnp.where(pair == 5, hi_e, lo_e)
    first_in_a = i1 == slot_a
    w_a = jnp.where(first_in_a, w1, w2)
    w_b = jnp.where(first_in_a, w2, w1)
    sub = lax.broadcasted_iota(jnp.int32, (8, TILE), 0)
    gate_rows = jnp.where(sub == 0, w_a, jnp.where(sub == 1, w_b, 0.0))
    return _rows_to_cols(gate_rows), gi * PAIRS_PER_GROUP + pair


def _outproj(xp, xs, yp, ys, zp, zs, mods3, g_norm, w_out, w_r, b_r):
    tps = DEC_SEQ // BIG_TILE
    per = BIG_TILE // TILE
    pidx = lambda i: (jnp.minimum(i, N_BIG_P - 1), 0)
    sidx = lambda i: (jnp.maximum(i - N_BIG_P, 0), 0)
    return pl.pallas_call(
        _outproj_kernel,
        out_shape=(jax.ShapeDtypeStruct((T_ALL, D_MODEL), F32),
                   jax.ShapeDtypeStruct((T_ALL, H2_EXT), F32),
                   jax.ShapeDtypeStruct((N_TILES, 1, TILE), jnp.int32)),
        grid=(N_BIG,),
        in_specs=[pl.BlockSpec((BIG_TILE, D_MODEL), pidx), pl.BlockSpec((BIG_TILE, D_MODEL), sidx),
                  pl.BlockSpec((BIG_TILE, ML_WIDTH), pidx), pl.BlockSpec((BIG_TILE, ML_WIDTH), sidx),
                  pl.BlockSpec((BIG_TILE, HY_WIDTH), pidx), pl.BlockSpec((BIG_TILE, HY_WIDTH), sidx),
                  pl.BlockSpec((1, N_MOD, D_MODEL), lambda i: (_mod_row_of_tile(i, tps, N_BIG_P), 0, 0)),
                  pl.BlockSpec((4, D_MODEL), lambda i: (0, 0)),
                  pl.BlockSpec((D_MODEL, D_MODEL), lambda i: (0, 0)),
                  pl.BlockSpec((ROUTER_ROWS, D_MODEL), lambda i: (0, 0)),
                  pl.BlockSpec((ROUTER_ROWS, 1), lambda i: (0, 0))],
        out_specs=(pl.BlockSpec((BIG_TILE, D_MODEL), lambda i: (i, 0)),
                   pl.BlockSpec((BIG_TILE, H2_EXT), lambda i: (i, 0)),
                   pl.BlockSpec((per, 1, TILE), lambda i: (i, 0, 0))),
        compiler_params=_cparams(("arbitrary",)),
        name="out_proj_router",
    )(xp, xs, yp, ys, zp, zs, mods3, g_norm, w_out, w_r, b_r)


def _route_kernel(bid_ref, pos_ref, meta_ref):
    nb = 32
    tm = float(ROW_TILE)
    sub = lax.broadcasted_iota(jnp.int32, (nb, TILE), 0)
    ri = lax.broadcasted_iota(jnp.int32, (TILE, TILE), 0)
    ci = lax.broadcasted_iota(jnp.int32, (TILE, TILE), 1)
    before = jnp.where(ri < ci, 1.0, 0.0).astype(BF16)

    def onehot(blk):
        return jnp.where(sub == bid_ref[blk], 1.0, 0.0)

    zeros = jnp.zeros((nb, 1), F32)
    cnt = lax.fori_loop(0, N_TILES, lambda blk, c: c + jnp.sum(onehot(blk), axis=1, keepdims=True), zeros)
    padded = jnp.floor((cnt + (tm - 1.0)) * (1.0 / tm)) * tm
    r32 = lax.broadcasted_iota(jnp.int32, (nb, nb), 0)
    c32 = lax.broadcasted_iota(jnp.int32, (nb, nb), 1)
    padded_row = jnp.sum(jnp.where(r32 == c32, padded, 0.0), axis=0, keepdims=True)
    offs = jnp.sum(jnp.where(c32 < r32, padded_row, 0.0), axis=1, keepdims=True)
    ends = offs + padded

    def place(blk, seen):
        oh = onehot(blk)
        rank = _dot(oh.astype(BF16), before)
        pos = jnp.sum(oh * (rank + seen + offs), axis=0, keepdims=True)
        pos_ref[blk] = pos.astype(jnp.int32)
        return seen + jnp.sum(oh, axis=1, keepdims=True)

    lax.fori_loop(0, N_TILES, place, zeros)

    start = lax.broadcasted_iota(jnp.int32, (nb, 128), 1).astype(F32) * tm
    bsub = lax.broadcasted_iota(jnp.int32, (nb, 128), 0)
    done = jnp.where((bsub < N_BUCKETS) & (ends <= start), 1.0, 0.0)
    tb = jnp.sum(done, axis=0, keepdims=True)
    valid = jnp.where(tb < N_BUCKETS, 1.0, 0.0)
    tbc = jnp.minimum(tb, N_BUCKETS - 1.0)
    grp = jnp.floor((tbc + 0.5) * (1.0 / PAIRS_PER_GROUP))
    pair = tbc - PAIRS_PER_GROUP * grp
    loc_a = jnp.zeros_like(pair)
    loc_b = jnp.zeros_like(pair)
    for k, (sa, sb) in enumerate(PAIR_SLOTS):
        loc_a = jnp.where(pair == k, float(sa), loc_a)
        loc_b = jnp.where(pair == k, float(sb), loc_b)
    mine = bsub.astype(F32) == tbc
    used = jnp.sum(jnp.where(mine, offs + cnt, 0.0), axis=0, keepdims=True)
    n_rows = jnp.clip(used - start[0:1], 0.0, tm) * valid
    row8 = lax.broadcasted_iota(jnp.int32, (8, 128), 0)
    meta = jnp.where(row8 == 0, grp * EXPERTS_PER_GROUP + loc_a,
                     jnp.where(row8 == 1, grp * EXPERTS_PER_GROUP + loc_b,
                               jnp.where(row8 == 2, valid, jnp.where(row8 == 3, n_rows, 0.0))))
    meta_ref[...] = meta.astype(jnp.int32)


def _route(bid):
    return pl.pallas_call(
        _route_kernel,
        out_shape=(jax.ShapeDtypeStruct((N_TILES, 1, TILE), jnp.int32),
                   jax.ShapeDtypeStruct((8, 128), jnp.int32)),
        compiler_params=pltpu.CompilerParams(vmem_limit_bytes=VMEM_LIMIT),
        name="moe_route",
    )(bid)


def _moe_kernel(meta_ref, pos_ref, h2_hbm, wga_ref, wua_ref, wda_ref, wgb_ref, wub_ref, wdb_ref,
                y_ref, src_ref, xbuf, sem, wga_s, wua_s, wda_s, wgb_s, wub_s, wdb_s):
    j = pl.program_id(0)

    def row_copy(tile, r, slot):
        tok = src_ref[tile * ROW_TILE + r]
        return pltpu.make_async_copy(h2_hbm.at[pl.ds(tok, 1), :], xbuf.at[slot, pl.ds(r, 1), :], sem.at[slot])

    group = 8

    def row_groups(tile):
        return (meta_ref[3, tile] + (group - 1)) // group

    def issue(tile, slot):
        def body(g, c):
            for k in range(group):
                row_copy(tile, g * group + k, slot).start()
            return c
        lax.fori_loop(0, row_groups(tile), body, 0)

    def wait(tile, slot):
        def body(g, c):
            for k in range(group):
                row_copy(tile, g * group + k, slot).wait()
            return c
        lax.fori_loop(0, row_groups(tile), body, 0)

    @pl.when(j == 0)
    def _():
        xbuf[...] = jnp.zeros_like(xbuf)

        def clear(t, c):
            n = meta_ref[3, t]
            for k in range(group - 1):
                src_ref[t * ROW_TILE + jnp.minimum(n + k, ROW_TILE - 1)] = 0
            return c
        lax.fori_loop(0, N_ROW_TILES, clear, 0)

        def invert(t, c):
            src_ref[pos_ref[t]] = t
            return c
        lax.fori_loop(0, T_ALL, invert, 0, unroll=8)

        @pl.when(meta_ref[2, 0] == 1)
        def _():
            issue(0, 0)

    nxt = jnp.minimum(j + 1, N_ROW_TILES - 1)

    @pl.when((j + 1 < N_ROW_TILES) & (meta_ref[2, nxt] == 1))
    def _():
        issue(nxt, nxt % 2)

    valid = meta_ref[2, j] == 1
    prev = jnp.maximum(j - 1, 0)

    @pl.when(valid & ((j == 0) | (meta_ref[0, j] != meta_ref[0, prev])))
    def _():
        wga_s[...] = wga_ref[0].astype(BF16)
        wua_s[...] = wua_ref[0].astype(BF16)
        wda_s[...] = wda_ref[0].astype(BF16)

    @pl.when(valid & ((j == 0) | (meta_ref[1, j] != meta_ref[1, prev])))
    def _():
        wgb_s[...] = wgb_ref[0].astype(BF16)
        wub_s[...] = wub_ref[0].astype(BF16)
        wdb_s[...] = wdb_ref[0].astype(BF16)

    @pl.when(valid)
    def _():
        slot = j % 2
        wait(j, slot)
        x = xbuf[slot, :, 0:D_MODEL].astype(BF16)
        gates = xbuf[slot, :, D_MODEL:H2_EXT]

        def expert(wg, wu, gate):
            hg = _dot(x, wg[...])
            hu = _dot(x, wu[...])
            return (hg * jax.nn.sigmoid(hg) * hu * gate).astype(BF16)

        act_a = expert(wga_s, wua_s, gates[:, 0:1])
        act_b = expert(wgb_s, wub_s, gates[:, 1:2])
        y_ref[...] = _dot(act_a, wda_s[...]) + _dot(act_b, wdb_s[...])

    @pl.when(jnp.logical_not(valid))
    def _():
        y_ref[...] = jnp.zeros_like(y_ref)


def _moe(meta, pos, h2ext, w_gate, w_up, w_down):
    up_spec = lambda slot: pl.BlockSpec((1, D_MODEL, EXPERT_FF), lambda j, meta, pos: (meta[slot, j], 0, 0))
    down_spec = lambda slot: pl.BlockSpec((1, EXPERT_FF, D_MODEL), lambda j, meta, pos: (meta[slot, j], 0, 0))
    grid_spec = pltpu.PrefetchScalarGridSpec(
        num_scalar_prefetch=2,
        grid=(N_ROW_TILES,),
        in_specs=[pl.BlockSpec(memory_space=pl.ANY),
                  up_spec(0), up_spec(0), down_spec(0), up_spec(1), up_spec(1), down_spec(1)],
        out_specs=pl.BlockSpec((ROW_TILE, D_MODEL), lambda j, meta, pos: (j, 0)),
        scratch_shapes=[pltpu.SMEM((ROW_CAP,), jnp.int32),
                        pltpu.VMEM((2, ROW_TILE, H2_EXT), F32),
                        pltpu.SemaphoreType.DMA((2,)),
                        pltpu.VMEM((D_MODEL, EXPERT_FF), BF16), pltpu.VMEM((D_MODEL, EXPERT_FF), BF16),
                        pltpu.VMEM((EXPERT_FF, D_MODEL), BF16),
                        pltpu.VMEM((D_MODEL, EXPERT_FF), BF16), pltpu.VMEM((D_MODEL, EXPERT_FF), BF16),
                        pltpu.VMEM((EXPERT_FF, D_MODEL), BF16)])
    return pl.pallas_call(
        _moe_kernel,
        out_shape=jax.ShapeDtypeStruct((ROW_CAP, D_MODEL), F32),
        grid_spec=grid_spec,
        compiler_params=_cparams(("arbitrary",)),
        name="moe_experts",
    )(meta, pos, h2ext, w_gate, w_up, w_down, w_gate, w_up, w_down)


def _final_kernel(pos_ref, y_hbm, x1_ref, m_ref, gn_ref, op_ref, os_ref, ybuf, sem):
    i = pl.program_id(0)

    def row_copy(tile, r, slot):
        p = pos_ref[tile * TILE + r]
        return pltpu.make_async_copy(y_hbm.at[pl.ds(p, 1), :], ybuf.at[slot, pl.ds(r, 1), :], sem.at[slot])

    def issue(tile, slot):
        def body(r2, c):
            row_copy(tile, 2 * r2, slot).start(priority=0)
            row_copy(tile, 2 * r2 + 1, slot).start(priority=1)
            return c
        lax.fori_loop(0, TILE // 2, body, 0, unroll=4)

    def wait(slot):
        pltpu.make_async_copy(y_hbm.at[pl.ds(0, TILE)], ybuf.at[slot], sem.at[slot]).wait()

    @pl.when(i == 0)
    def _():
        issue(0, 0)

    @pl.when(i + 1 < N_TILES)
    def _():
        issue(i + 1, (i + 1) % 2)

    slot = i % 2
    wait(slot)
    out = x1_ref[...] + m_ref[0, 5:6, :] * _rms(ybuf[slot], gn_ref[3:4, :])

    @pl.when(i < N_TILES_P)
    def _():
        op_ref[...] = out

    @pl.when(i >= N_TILES_P)
    def _():
        os_ref[...] = out


def _final(pos, y_sorted, x1, mods3, g_norm):
    tps = DEC_SEQ // TILE
    grid_spec = pltpu.PrefetchScalarGridSpec(
        num_scalar_prefetch=1,
        grid=(N_TILES,),
        in_specs=[pl.BlockSpec(memory_space=pl.ANY),
                  pl.BlockSpec((TILE, D_MODEL), lambda i, pos: (i, 0)),
                  pl.BlockSpec((1, N_MOD, D_MODEL), lambda i, pos: (_mod_row_of_tile(i, tps, N_TILES_P), 0, 0)),
                  pl.BlockSpec((4, D_MODEL), lambda i, pos: (0, 0))],
        out_specs=(pl.BlockSpec((TILE, D_MODEL), lambda i, pos: (jnp.minimum(i, N_TILES_P - 1), 0)),
                   pl.BlockSpec((TILE, D_MODEL), lambda i, pos: (jnp.maximum(i - N_TILES_P, 0), 0))),
        scratch_shapes=[pltpu.VMEM((2, TILE, D_MODEL), F32), pltpu.SemaphoreType.DMA((2,))])
    return pl.pallas_call(
        _final_kernel,
        out_shape=(jax.ShapeDtypeStruct((T_PROMPT, D_MODEL), F32),
                   jax.ShapeDtypeStruct((T_SAMPLE, D_MODEL), F32)),
        grid_spec=grid_spec,
        compiler_params=_cparams(("arbitrary",)),
        name="moe_combine_final",
    )(pos, y_sorted, x1, mods3, g_norm)


def kernel(x_prompt, x_sample, state_C, state_n, state_m, c, c_ctx, w_ada, b_ada, g_norm, w_in, ml_gate_bias, ml_head_gain, hy_conv_w, hy_f_w1, hy_f_b1, hy_f_w2, hy_f_b2, hy_f_w3, hy_f_b3, hy_decay, hy_bias, w_out, w_rc, b_rc, w_rf, b_rf, w_gate, w_up, w_down):
    xp = x_prompt.reshape(T_PROMPT, D_MODEL)
    xs = x_sample.reshape(T_SAMPLE, D_MODEL)
    gn = g_norm[0]

    cv = jnp.concatenate([c_ctx[None, :], c, jnp.zeros((MOD_ROWS - 1 - DEC_BATCH, D_MODEL), F32)], axis=0)
    mods3 = _ada(cv, w_ada[0], b_ada[0]).reshape(MOD_ROWS, N_MOD, D_MODEL)

    w_in0 = w_in[0]
    w_qkvo = w_in0[:, :ML_QKVO_COLS].astype(BF16)
    w_hy = w_in0[:, ML_QKVO_COLS + ML_GATE_COLS:].astype(BF16)
    wg = w_in0[:, ML_QKVO_COLS:ML_QKVO_COLS + ML_GATE_COLS]
    gbt = ml_gate_bias[0].reshape(ML_GATE_COLS, 1)
    proj, gates, gates_t = _inproj(xp, xs, mods3, gn, w_qkvo, w_hy, wg.T, gbt)

    gain = ml_head_gain[0].reshape(1, ML_WIDTH)
    y_ml_p, c_new, n_new, m_new = _mlstm(proj, gates, gates_t, gain, None, SEQ, BATCH, 0)
    state = (state_C[:, 0], state_n[:, 0], state_m[:, 0].reshape(DEC_BATCH, 2 * ML_HEADS, 1))
    y_ml_s, _, _, _ = _mlstm(proj, gates, gates_t, gain, state, DEC_SEQ, DEC_BATCH, T_PROMPT // DEC_SEQ)

    w1p = jnp.pad(hy_f_w1[0], ((0, 128 - HY_EMB), (0, 0)))
    b1 = hy_f_b1[0].reshape(1, -1)
    b2 = hy_f_b2[0].reshape(1, -1)
    b3 = hy_f_b3[0].reshape(1, -1)
    dec = hy_decay[0].reshape(1, -1)
    z_parts = []
    for seq_len, n_seq, row_off, width, seqs in ((SEQ, BATCH, 0, SEQ, 4), (DEC_SEQ, DEC_BATCH, T_PROMPT, GRID_W, 2)):
        f, ft = _dft_mats(seq_len)
        coefs = _hyena_filters(seq_len, f, w1p, b1, hy_f_w2[0], b2, hy_f_w3[0], b3, dec)
        z_parts.append(_hyena(proj, hy_conv_w[0], coefs, hy_bias[0], f, ft, seq_len, n_seq, row_off, width, seqs))
    z_p, z_s = z_parts

    pad_r = ROUTER_ROWS - N_GROUPS - N_EXPERTS
    w_r = jnp.pad(jnp.concatenate([w_rc[0], w_rf[0]], axis=1).T, ((0, pad_r), (0, 0)))
    b_r = jnp.pad(jnp.concatenate([b_rc[0], b_rf[0]], axis=0), (0, pad_r)).reshape(ROUTER_ROWS, 1)
    x1, h2ext, bid = _outproj(xp, xs, y_ml_p, y_ml_s, z_p, z_s, mods3, gn, w_out[0].astype(BF16), w_r, b_r)

    pos3, meta = _route(bid)
    pos = pos3.reshape(T_ALL)
    y_sorted = _moe(meta, pos, h2ext, w_gate[0], w_up[0], w_down[0])
    y_p, y_s = _final(pos, y_sorted, x1, mods3, gn)

    new_c = c_new.reshape(BATCH, 1, 2, ML_HEADS, ML_HEAD_DIM, ML_HEAD_DIM)
    new_n = n_new.reshape(BATCH, 1, 2, ML_HEADS, ML_HEAD_DIM)
    new_m = m_new[:, :, 0].reshape(BATCH, 1, 2, ML_HEADS)
    return (y_p.reshape(BATCH, SEQ, D_MODEL), y_s.reshape(DEC_BATCH, DEC_SEQ, D_MODEL), new_c, new_n, new_m)
```

```python
import functools
import math

import jax
import jax.numpy as jnp
import numpy as np
from jax import lax
from jax.experimental import pallas as pl
from jax.experimental.pallas import tpu as pltpu

F32 = jnp.float32
BF16 = jnp.bfloat16

D_MODEL = 1024
BATCH = 16
SEQ = 256
DEC_BATCH = 4
DEC_SEQ = 1024
GRID_W = 64
ML_WIDTH = 512
ML_HEADS = 4
ML_HEAD_DIM = 128
HY_WIDTH = 512
HY_ORDER = 2
HY_EMB = 33
HY_BANDS = 16
HY_FILTER_HIDDEN = 64
HY_MOD_SHIFT = 0.05
N_GROUPS = 4
EXPERTS_PER_GROUP = 4
N_EXPERTS = 16
EXPERT_FF = 512
N_MOD = 6
EPS = 1e-6
ML_QKVO_COLS = 4 * ML_WIDTH
ML_GATE_COLS = 4 * ML_HEADS
HY_COLS = 3 * HY_WIDTH
MAIN_COLS = ML_QKVO_COLS + HY_COLS

T_PROMPT = BATCH * SEQ
T_SAMPLE = DEC_BATCH * DEC_SEQ
T_ALL = T_PROMPT + T_SAMPLE
TILE = 256
N_TILES_P = T_PROMPT // TILE
N_TILES = T_ALL // TILE
MOD_ROWS = 8
K_SCALE = ML_HEAD_DIM ** -0.5
VMEM_LIMIT = 56 * 1024 * 1024


def _cparams(sem):
    return pltpu.CompilerParams(dimension_semantics=sem, vmem_limit_bytes=VMEM_LIMIT)


def _split2(x):
    hi = x.astype(BF16)
    lo = (x - hi.astype(F32)).astype(BF16)
    return hi, lo


def _dot(a, b):
    return jnp.dot(a, b, preferred_element_type=F32)


def _dot_nt(a, b):
    return lax.dot_general(a, b, (((1,), (1,)), ((), ())), preferred_element_type=F32)


def _dot_tn(a, b):
    return lax.dot_general(a, b, (((0,), (0,)), ((), ())), preferred_element_type=F32)


def _dot3(a, b):
    ah, al = _split2(a)
    bh, bl = _split2(b)
    return _dot(ah, bh) + _dot(al, bh) + _dot(ah, bl)


def _dot3_nt(a, b):
    ah, al = _split2(a)
    bh, bl = _split2(b)
    return _dot_nt(ah, bh) + _dot_nt(al, bh) + _dot_nt(ah, bl)


def _dot_exact_lhs(t, x):
    x1 = x.astype(BF16)
    r1 = x - x1.astype(F32)
    x2 = r1.astype(BF16)
    x3 = (r1 - x2.astype(F32)).astype(BF16)
    return _dot(t, x1) + _dot(t, x2) + _dot(t, x3)


def _dot_exact_rhs(x, t):
    x1 = x.astype(BF16)
    r1 = x - x1.astype(F32)
    x2 = r1.astype(BF16)
    x3 = (r1 - x2.astype(F32)).astype(BF16)
    return _dot(x1, t) + _dot(x2, t) + _dot(x3, t)


def _rms(x, g):
    return x * lax.rsqrt(jnp.mean(x * x, axis=-1, keepdims=True) + EPS) * g


def _mod_row_of_tile(i, tiles_per_sample_seq, n_prompt_tiles):
    return jnp.where(i < n_prompt_tiles, 0, 1 + (i - n_prompt_tiles) // tiles_per_sample_seq)


def _ada_kernel(cv_ref, w_ref, b_ref, o_ref):
    cv = cv_ref[...]
    s = cv * jax.nn.sigmoid(cv)
    o_ref[...] = _dot3(s, w_ref[...]) + b_ref[...]


def _ada(cv, w_ada, b_ada):
    n = N_MOD * D_MODEL
    return pl.pallas_call(
        _ada_kernel,
        out_shape=jax.ShapeDtypeStruct((MOD_ROWS, n), F32),
        grid=(N_MOD,),
        in_specs=[pl.BlockSpec((MOD_ROWS, D_MODEL), lambda j: (0, 0)),
                  pl.BlockSpec((D_MODEL, D_MODEL), lambda j: (0, j)),
                  pl.BlockSpec((1, D_MODEL), lambda j: (0, j))],
        out_specs=pl.BlockSpec((MOD_ROWS, D_MODEL), lambda j: (0, j)),
        compiler_params=_cparams(("arbitrary",)),
        name="ada_mod",
    )(cv, w_ada, b_ada.reshape(1, n))


def _log_sigmoid(x):
    return jnp.minimum(x, 0.0) - jnp.log1p(jnp.exp(-jnp.abs(x)))


def _rows_to_cols(rows):
    ri = lax.broadcasted_iota(jnp.int32, (TILE, TILE), 0)
    ci = lax.broadcasted_iota(jnp.int32, (TILE, TILE), 1)
    eye = jnp.where(ri == ci, 1.0, 0.0).astype(BF16)
    p1 = rows.astype(BF16)
    r1 = rows - p1.astype(F32)
    p2 = r1.astype(BF16)
    p3 = (r1 - p2.astype(F32)).astype(BF16)
    return _dot_nt(eye, p1) + _dot_nt(eye, p2) + _dot_nt(eye, p3)


BIG_TILE = 2 * TILE
N_BIG_P = T_PROMPT // BIG_TILE
N_BIG = T_ALL // BIG_TILE


def _inproj_kernel(xp_ref, xs_ref, m_ref, gn_ref, wq_ref, wh_ref, wgt_ref, gbt_ref, proj_ref, gate_ref, gatet_ref):
    i = pl.program_id(0)
    x = jnp.where(i < N_TILES_P, xp_ref[...], xs_ref[...])
    h = _rms(x, gn_ref[0:1, :]) * (1.0 + m_ref[0, 1:2, :]) + m_ref[0, 0:1, :]
    hb = h.astype(BF16)
    cb = 512
    for j in range(ML_QKVO_COLS // cb):
        proj_ref[:, j * cb:(j + 1) * cb] = _dot(hb, wq_ref[:, j * cb:(j + 1) * cb]).astype(BF16)
    for j in range(HY_COLS // cb):
        lo = ML_QKVO_COLS + j * cb
        proj_ref[:, lo:lo + cb] = _dot(hb, wh_ref[:, j * cb:(j + 1) * cb]).astype(BF16)
    hl = (h - hb.astype(F32)).astype(BF16)
    wth, wtl = _split2(wgt_ref[...])
    gt = _dot_nt(wth, hb) + _dot_nt(wth, hl) + _dot_nt(wtl, hb) + gbt_ref[...]
    row = lax.broadcasted_iota(jnp.int32, gt.shape, 0)
    gt = jnp.where((row % 8) >= 4, _log_sigmoid(gt), gt)
    gatet_ref[0] = gt
    gate_ref[...] = _rows_to_cols(gt)


def _inproj(xp, xs, mods3, g_norm, w_qkvo, w_hy, wgt, gbt):
    tps = DEC_SEQ // TILE
    return pl.pallas_call(
        _inproj_kernel,
        out_shape=(jax.ShapeDtypeStruct((T_ALL, MAIN_COLS), BF16),
                   jax.ShapeDtypeStruct((T_ALL, ML_GATE_COLS), F32),
                   jax.ShapeDtypeStruct((N_TILES, ML_GATE_COLS, TILE), F32)),
        grid=(N_TILES,),
        in_specs=[pl.BlockSpec((TILE, D_MODEL), lambda i: (jnp.minimum(i, N_TILES_P - 1), 0)),
                  pl.BlockSpec((TILE, D_MODEL), lambda i: (jnp.maximum(i - N_TILES_P, 0), 0)),
                  pl.BlockSpec((1, N_MOD, D_MODEL), lambda i: (_mod_row_of_tile(i, tps, N_TILES_P), 0, 0)),
                  pl.BlockSpec((4, D_MODEL), lambda i: (0, 0)),
                  pl.BlockSpec((D_MODEL, ML_QKVO_COLS), lambda i: (0, 0)),
                  pl.BlockSpec((D_MODEL, HY_COLS), lambda i: (0, 0)),
                  pl.BlockSpec((ML_GATE_COLS, D_MODEL), lambda i: (0, 0)),
                  pl.BlockSpec((ML_GATE_COLS, 1), lambda i: (0, 0))],
        out_specs=(pl.BlockSpec((TILE, MAIN_COLS), lambda i: (i, 0)),
                   pl.BlockSpec((TILE, ML_GATE_COLS), lambda i: (i, 0)),
                   pl.BlockSpec((1, ML_GATE_COLS, TILE), lambda i: (i, 0, 0))),
        compiler_params=_cparams(("arbitrary",)),
        name="in_proj",
    )(xp, xs, mods3, g_norm, w_qkvo, w_hy, wgt, gbt)


ST_ROWS = ML_HEAD_DIM + 16


def _mlstm_kernel(*refs, seq_len, has_state):
    if has_state:
        (q_ref, k_ref, v_ref, o_ref, g_ref, gt_ref, gain_ref, c0_ref, n0_ref, m0_ref,
         y_ref, c_ref, n_ref, m_ref, vt_ref, hf_ref, hb_ref, st_ref, ms_ref) = refs
    else:
        (q_ref, k_ref, v_ref, o_ref, g_ref, gt_ref, gain_ref,
         y_ref, c_ref, n_ref, m_ref, vt_ref, hf_ref, hb_ref, st_ref, ms_ref) = refs
    ch = TILE
    nc = seq_len // ch
    hd = ML_HEAD_DIM
    key = lax.broadcasted_iota(jnp.int32, (ch, ch), 0)
    qry = lax.broadcasted_iota(jnp.int32, (ch, ch), 1)
    key_le = key <= qry
    key_ge = key >= qry
    t_le = jnp.where(key_le, 1.0, 0.0).astype(BF16)
    t_ge = jnp.where(key_ge, 1.0, 0.0).astype(BF16)
    sub16 = lax.broadcasted_iota(jnp.int32, (16, ch), 0)
    ln_scale = math.log(K_SCALE)

    for c in range(nc):
        for h in range(ML_HEADS):
            cols = slice(h * hd, (h + 1) * hd)
            vt_ref[c, cols, :] = v_ref[c * ch:(c + 1) * ch, cols].astype(F32).T.astype(BF16)

    for d in range(2):
        for h in range(ML_HEADS):
            r = d * ML_HEADS + h
            st_ref[r] = jnp.zeros((ST_ROWS, hd), F32)
            if has_state:
                st_ref[r, 0:hd, :] = c0_ref[0, d, h].T
                st_ref[r, hd:hd + 1, :] = n0_ref[0, d, h:h + 1, :]
                ms_ref[r] = jnp.broadcast_to(m0_ref[0, r:r + 1, :], (1, ch))
            else:
                ms_ref[r] = jnp.zeros((1, ch), F32)

    def step(t, carry):
        for d in range(2):
            c = t if d == 0 else nc - 1 - t
            rows = pl.ds(pl.multiple_of(c * ch, ch), ch)
            gcol = g_ref[rows, :]
            grow = gt_ref[c]
            brow_all = _dot_exact_rhs(grow, t_le if d == 0 else t_ge)
            bcol_all = _dot_exact_lhs(t_ge if d == 0 else t_le, gcol)
            mask = key_le if d == 0 else key_ge
            hacc_ref = hf_ref if d == 0 else hb_ref
            heads = range(ML_HEADS)
            regs = [d * ML_HEADS + h for h in heads]
            colss = [slice(h * hd, (h + 1) * hd) for h in heads]
            qs = [q_ref[rows, cols] for cols in colss]
            ks = [k_ref[rows, cols] for cols in colss]
            vts = [vt_ref[c, cols, :] for cols in colss]
            sts = [st_ref[r] for r in regs]
            m_prevs = [ms_ref[r] for r in regs]
            b_rows = [brow_all[(1 + 2 * d) * ML_HEADS + h:(1 + 2 * d) * ML_HEADS + h + 1, :] for h in heads]
            ig_rows = [grow[2 * d * ML_HEADS + h:2 * d * ML_HEADS + h + 1, :] for h in heads]
            qks = [_dot_nt(k, q) for k, q in zip(ks, qs)]
            iqs = [_dot_nt(st.astype(BF16), q) for st, q in zip(sts, qs)]
            ss, sc_inters, m_poss = [], [], []
            for h in heads:
                fcol = (1 + 2 * d) * ML_HEADS + h
                icol = 2 * d * ML_HEADS + h
                c_col = gcol[:, icol:icol + 1] - bcol_all[:, fcol:fcol + 1]
                logd = jnp.where(mask, b_rows[h] + c_col, -jnp.inf)
                inter = b_rows[h] + m_prevs[h]
                m_pos = jnp.maximum(inter, jnp.max(logd, axis=0, keepdims=True))
                ss.append(qks[h] * jnp.exp(logd - (m_pos - ln_scale)))
                sc_inters.append(jnp.exp(inter - m_pos))
                m_poss.append(m_pos)
            pvs = [_dot(vt, s.astype(BF16)) for vt, s in zip(vts, ss)]
            for h in heads:
                num = sc_inters[h] * iqs[h][0:hd] + pvs[h]
                den = sc_inters[h] * iqs[h][hd:hd + 1] + jnp.sum(ss[h], axis=0, keepdims=True)
                hacc_ref[c, colss[h], :] = num * (1.0 / jnp.maximum(jnp.abs(den), jnp.exp(-m_poss[h])))
            lhss, decays = [], []
            for h in heads:
                b_row = b_rows[h]
                b_last = b_row[:, ch - 1:ch] if d == 0 else b_row[:, 0:1]
                logw = b_last - b_row + ig_rows[h]
                m_new = jnp.maximum(b_last + m_prevs[h], jnp.max(logw, axis=1, keepdims=True))
                w = jnp.exp(logw - (m_new - ln_scale))
                decays.append(jnp.exp(b_last + m_prevs[h] - m_new))
                lhss.append(jnp.concatenate([(vts[h].astype(F32) * w).astype(BF16),
                                             jnp.where(sub16 == 0, w, 0.0).astype(BF16)], axis=0))
                ms_ref[regs[h]] = m_new
            upds = [_dot(lhs, k) for lhs, k in zip(lhss, ks)]
            for h in heads:
                st_ref[regs[h]] = decays[h][:, 0:hd] * sts[h] + upds[h]
        return carry

    lax.fori_loop(0, nc, step, 0)

    for d in range(2):
        for h in range(ML_HEADS):
            r = d * ML_HEADS + h
            c_ref[0, d, h] = st_ref[r, 0:hd, :].T
            n_ref[0, d, h:h + 1, :] = st_ref[r, hd:hd + 1, :]
            m_ref[0, r:r + 1, :] = ms_ref[r][:, 0:hd]
    for c in range(nc):
        for h in range(ML_HEADS):
            cols = slice(h * hd, (h + 1) * hd)
            ht = hf_ref[c, cols, :] + hb_ref[c, cols, :]
            ht = ht * lax.rsqrt(jnp.mean(ht * ht, axis=0, keepdims=True) + EPS)
            rows = slice(c * ch, (c + 1) * ch)
            y = ht.T * gain_ref[:, cols] * jax.nn.sigmoid(o_ref[rows, cols].astype(F32))
            y_ref[rows, cols] = y.astype(BF16)


def _mlstm(proj, gates, gates_t, gain, state, seq_len, n_seq, row_block_off):
    has_state = state is not None
    tiles = seq_len // TILE
    off = row_block_off
    qkvo_specs = [pl.BlockSpec((seq_len, ML_WIDTH), functools.partial(lambda b, j: (off + b, j), j=j))
                  for j in range(4)]
    in_specs = qkvo_specs + [
        pl.BlockSpec((seq_len, ML_GATE_COLS), lambda b: (off + b, 0)),
        pl.BlockSpec((tiles, ML_GATE_COLS, TILE), lambda b: (off + b, 0, 0)),
        pl.BlockSpec((1, ML_WIDTH), lambda b: (0, 0)),
    ]
    args = [proj, proj, proj, proj, gates, gates_t, gain]
    if has_state:
        c0, n0, m0 = state
        in_specs += [
            pl.BlockSpec((1, 2, ML_HEADS, ML_HEAD_DIM, ML_HEAD_DIM), lambda b: (b, 0, 0, 0, 0)),
            pl.BlockSpec((1, 2, ML_HEADS, ML_HEAD_DIM), lambda b: (b, 0, 0, 0)),
            pl.BlockSpec((1, 2 * ML_HEADS, 1), lambda b: (b, 0, 0)),
        ]
        args += [c0, n0, m0]
    out_shape = (jax.ShapeDtypeStruct((n_seq * seq_len, ML_WIDTH), BF16),
                 jax.ShapeDtypeStruct((n_seq, 2, ML_HEADS, ML_HEAD_DIM, ML_HEAD_DIM), F32),
                 jax.ShapeDtypeStruct((n_seq, 2, ML_HEADS, ML_HEAD_DIM), F32),
                 jax.ShapeDtypeStruct((n_seq, 2 * ML_HEADS, ML_HEAD_DIM), F32))
    out_specs = (pl.BlockSpec((seq_len, ML_WIDTH), lambda b: (b, 0)),
                 pl.BlockSpec((1, 2, ML_HEADS, ML_HEAD_DIM, ML_HEAD_DIM), lambda b: (b, 0, 0, 0, 0)),
                 pl.BlockSpec((1, 2, ML_HEADS, ML_HEAD_DIM), lambda b: (b, 0, 0, 0)),
                 pl.BlockSpec((1, 2 * ML_HEADS, ML_HEAD_DIM), lambda b: (b, 0, 0)))
    scratch = [pltpu.VMEM((tiles, ML_WIDTH, TILE), BF16),
               pltpu.VMEM((tiles, ML_WIDTH, TILE), F32), pltpu.VMEM((tiles, ML_WIDTH, TILE), F32),
               pltpu.VMEM((2 * ML_HEADS, ST_ROWS, ML_HEAD_DIM), F32),
               pltpu.VMEM((2 * ML_HEADS, 1, TILE), F32)]
    return pl.pallas_call(
        functools.partial(_mlstm_kernel, seq_len=seq_len, has_state=has_state),
        out_shape=out_shape, grid=(n_seq,), in_specs=in_specs, out_specs=out_specs,
        scratch_shapes=scratch, compiler_params=_cparams(("arbitrary",)),
        name=f"mlstm_{seq_len}",
    )(*args)


def _dft_mats(seq_len):
    k = np.arange(seq_len, dtype=np.int64)[:, None]
    d = np.arange(seq_len, dtype=np.int64)[None, :]
    ang = np.pi * ((k * d) % (2 * seq_len)).astype(np.float64) / seq_len
    sinm = np.sin(ang)
    sinm[0, :] = np.where(d[0] % 2 == 0, 1.0, -1.0)
    f = np.concatenate([np.cos(ang), sinm], axis=0).astype(np.float32)
    return jnp.asarray(f).astype(BF16), jnp.asarray(np.ascontiguousarray(f.T)).astype(BF16)


def _filter_feats(seq_len):
    t = np.linspace(0.0, 1.0, seq_len, dtype=np.float64)[:, None]
    wpos = 2.0 * np.pi * np.arange(seq_len, dtype=np.float64)[:, None] / seq_len
    bands = np.linspace(1e-4, HY_BANDS - 1, HY_BANDS, dtype=np.float64)[None, :]
    z = np.concatenate([t, np.cos(bands * wpos), -np.sin(bands * wpos)], axis=-1)
    return jnp.asarray(np.pad(z, ((0, 0), (0, 128 - HY_EMB))).astype(np.float32))


def _filter_kernel(z_ref, w1_ref, b1_ref, w2_ref, b2_ref, w3_ref, b3_ref, dec_ref, f_ref,
                   a_ref, b_ref, d_ref, *, seq_len):
    n = 2 * seq_len
    z = z_ref[...]
    h = jnp.sin(_dot3(z, w1_ref[...]) + b1_ref[...])
    h = jnp.sin(_dot3(h, w2_ref[...]) + b2_ref[...])
    h = _dot3(h, w3_ref[...]) + b3_ref[...]
    t = z[:, 0:1]
    h = h * (jnp.exp(-t * jnp.abs(dec_ref[...])) + HY_MOD_SHIFT)
    ss = jnp.sum(h * h, axis=0, keepdims=True)
    inv = lax.rsqrt(ss[:, :HY_WIDTH] + ss[:, HY_WIDTH:] + EPS)
    hp = h[:, :HY_WIDTH] * inv
    hn = h[:, HY_WIDTH:] * inv
    ssum = hp + hn
    sdif = hp - hn
    hc = _dot(f_ref[0:seq_len, :], ssum.astype(BF16))
    hs = _dot(f_ref[seq_len:n, :], sdif.astype(BF16))
    di = lax.broadcasted_iota(jnp.int32, (seq_len, 1), 0)
    sgn = jnp.where(di % 2 == 0, 1.0, -1.0)
    nyq = jnp.sum(ssum * sgn, axis=0, keepdims=True)
    first = di == 0
    a_ref[0] = hc * jnp.where(first, 1.0 / n, 2.0 / n)
    b_ref[0] = jnp.where(first, 0.0, hs * (2.0 / n))
    d_ref[0] = jnp.where(first, nyq * (1.0 / n), hc * (2.0 / n))


def _hyena_filters(seq_len, f, w1p, b1, w2, b2, w3, b3, dec):
    z = _filter_feats(seq_len)
    hid = HY_FILTER_HIDDEN
    oc = 2 * HY_WIDTH
    full = lambda shape: pl.BlockSpec(shape, lambda o: tuple(0 for _ in shape))
    out = jax.ShapeDtypeStruct((HY_ORDER, seq_len, HY_WIDTH), F32)
    return pl.pallas_call(
        functools.partial(_filter_kernel, seq_len=seq_len),
        out_shape=(out, out, out),
        grid=(HY_ORDER,),
        in_specs=[full((seq_len, 128)), full((128, hid)), full((1, hid)), full((hid, hid)), full((1, hid)),
                  pl.BlockSpec((hid, oc), lambda o: (0, o)),
                  pl.BlockSpec((1, oc), lambda o: (0, o)),
                  pl.BlockSpec((1, oc), lambda o: (0, o)),
                  full((2 * seq_len, seq_len))],
        out_specs=tuple(pl.BlockSpec((1, seq_len, HY_WIDTH), lambda o: (o, 0, 0)) for _ in range(3)),
        compiler_params=_cparams(("arbitrary",)),
        name=f"hyena_filter_{seq_len}",
    )(z, w1p, b1, w2, b2, w3, b3, dec, f)


def _hyena_kernel(x1_ref, x2_ref, v_ref, cw1_ref, cw2_ref, cwv_ref, a_ref, b_ref, d_ref, bias_ref,
                  f_ref, ft_ref, z_ref, *, seq_len, width, seqs):
    rows = seqs * seq_len
    ti = lax.broadcasted_iota(jnp.int32, (rows, 1), 0)
    has_prev = (ti % width) != 0
    has_next = (ti % width) != (width - 1)

    def short_conv(x_ref, w_ref):
        x = x_ref[...].astype(F32)
        prev = jnp.where(has_prev, pltpu.roll(x, 1, axis=0), 0.0)
        nxt = jnp.where(has_next, pltpu.roll(x, rows - 1, axis=0), 0.0)
        return w_ref[0:1, :] * prev + w_ref[1:2, :] * x + w_ref[2:3, :] * nxt

    gates = (short_conv(x1_ref, cw1_ref), short_conv(x2_ref, cw2_ref))
    v = short_conv(v_ref, cwv_ref)
    sls = [slice(i * seq_len, (i + 1) * seq_len) for i in range(seqs)]
    zs = [v[sl] for sl in sls]
    for o in range(HY_ORDER):
        a, b, dd = a_ref[o], b_ref[o], d_ref[o]
        us = [_dot(f_ref[...], z.astype(BF16)) for z in zs]
        ys = []
        for u in us:
            ut = u[:seq_len]
            ub = u[seq_len:]
            ys.append(((ut * a - ub * b).astype(BF16), (ut * b + ub * dd).astype(BF16)))
        convs = [_dot(ft_ref[:, :seq_len], yt) + _dot(ft_ref[:, seq_len:], yb) for yt, yb in ys]
        zs = [gates[o][sl] * (y + bias_ref[o:o + 1, :] * z) for sl, y, z in zip(sls, convs, zs)]
    for sl, z in zip(sls, zs):
        z_ref[sl, :] = z.astype(BF16)


def _hyena(proj, conv_w, coefs, hy_bias, f, ft, seq_len, n_seq, row_off, width, seqs):
    cb = 256
    nblk = HY_WIDTH // cb
    base = ML_QKVO_COLS // cb
    rows = seqs * seq_len
    off = row_off // rows
    a, b, d = coefs

    def col_spec(part):
        return pl.BlockSpec((rows, cb), lambda j, s: (off + s, base + part * nblk + j))

    def w_spec(part):
        return pl.BlockSpec((3, cb), lambda j, s: (0, part * nblk + j))

    coef_spec = pl.BlockSpec((HY_ORDER, seq_len, cb), lambda j, s: (0, 0, j))
    return pl.pallas_call(
        functools.partial(_hyena_kernel, seq_len=seq_len, width=width, seqs=seqs),
        out_shape=jax.ShapeDtypeStruct((n_seq * seq_len, HY_WIDTH), BF16),
        grid=(nblk, n_seq // seqs),
        in_specs=[col_spec(0), col_spec(1), col_spec(2), w_spec(0), w_spec(1), w_spec(2),
                  coef_spec, coef_spec, coef_spec,
                  pl.BlockSpec((HY_ORDER, cb), lambda j, s: (0, j)),
                  pl.BlockSpec((2 * seq_len, seq_len), lambda j, s: (0, 0)),
                  pl.BlockSpec((seq_len, 2 * seq_len), lambda j, s: (0, 0))],
        out_specs=pl.BlockSpec((rows, cb), lambda j, s: (s, j)),
        compiler_params=_cparams(("arbitrary", "arbitrary")),
        name=f"hyena_conv_{seq_len}",
    )(proj, proj, proj, conv_w, conv_w, conv_w, a, b, d, hy_bias, f, ft)


def _first_max(x, n):
    mx = jnp.max(x, axis=0, keepdims=True)
    row = lax.broadcasted_iota(jnp.int32, x.shape, 0).astype(F32)
    idx = jnp.min(jnp.where(x == mx, row, float(n)), axis=0, keepdims=True)
    return mx, idx.astype(jnp.int32)


ROUTER_ROWS = 32
PAIRS_PER_GROUP = 6
N_BUCKETS = N_GROUPS * PAIRS_PER_GROUP
PAIR_SLOTS = ((0, 1), (0, 2), (0, 3), (1, 3), (1, 2), (3, 2))
LANES = 128
H2_EXT = D_MODEL + LANES
ROW_TILE = 256
ROW_CAP = T_ALL + N_BUCKETS * ROW_TILE
N_ROW_TILES = ROW_CAP // ROW_TILE


def _outproj_kernel(xp_ref, xs_ref, yp_ref, ys_ref, zp_ref, zs_ref, m_ref, gn_ref, wo_ref, wr_ref, br_ref,
                    x1_ref, h2_ref, bid_ref):
    is_p = pl.program_id(0) < N_BIG_P
    wrh, wrl = _split2(wr_ref[...])
    for r in range(BIG_TILE // TILE):
        rows = slice(r * TILE, (r + 1) * TILE)
        x = jnp.where(is_p, xp_ref[rows, :], xs_ref[rows, :])
        yml = jnp.where(is_p, yp_ref[rows, :], ys_ref[rows, :])
        zz = jnp.where(is_p, zp_ref[rows, :], zs_ref[rows, :])
        y = _dot(yml, wo_ref[0:ML_WIDTH, :]) + _dot(zz, wo_ref[ML_WIDTH:, :])
        x1 = x + m_ref[0, 2:3, :] * _rms(y, gn_ref[1:2, :])
        x1_ref[rows, :] = x1
        h2 = _rms(x1, gn_ref[2:3, :]) * (1.0 + m_ref[0, 4:5, :]) + m_ref[0, 3:4, :]
        h2_ref[rows, 0:D_MODEL] = h2
        gate_cols, bucket = _route_tile(h2, wrh, wrl, br_ref[...])
        h2_ref[rows, D_MODEL:H2_EXT] = jnp.zeros((TILE, LANES), F32)
        h2_ref[rows, D_MODEL:D_MODEL + 8] = gate_cols
        bid_ref[r] = bucket


def _route_tile(h2, wrh, wrl, bias):
    h2h, h2l = _split2(h2)
    logits = _dot_nt(wrh, h2h) + _dot_nt(wrh, h2l) + _dot_nt(wrl, h2h) + bias
    lc = logits[0:N_GROUPS]
    mx, gi = _first_max(lc, N_GROUPS)
    p_grp = 1.0 / jnp.sum(jnp.exp(lc - mx), axis=0, keepdims=True)
    lsel = jnp.zeros((EXPERTS_PER_GROUP, TILE), F32)
    for g in range(N_GROUPS):
        lo = N_GROUPS + g * EXPERTS_PER_GROUP
        lsel = jnp.where(gi == g, logits[lo:lo + EXPERTS_PER_GROUP], lsel)
    l1, i1 = _first_max(lsel, EXPERTS_PER_GROUP)
    sub4 = lax.broadcasted_iota(jnp.int32, lsel.shape, 0)
    l2, i2 = _first_max(jnp.where(sub4 == i1, -jnp.inf, lsel), EXPERTS_PER_GROUP)
    e2 = jnp.exp(l2 - l1)
    w1 = p_grp / (1.0 + e2)
    w2 = p_grp * e2 / (1.0 + e2)
    lo_e = jnp.minimum(i1, i2)
    hi_e = jnp.maximum(i1, i2)
    pair = jnp.where(lo_e == 0, hi_e - 1, jnp.where(lo_e == 1, jnp.where(hi_e == 3, 3, 4), 5))
    slot_a = jnp.where(pair == 5, hi_e, lo_e)
    first_in_a = i1 == slot_a
    w_a = jnp.where(first_in_a, w1, w2)
    w_b = jnp.where(first_in_a, w2, w1)
    sub = lax.broadcasted_iota(jnp.int32, (8, TILE), 0)
    gate_rows = jnp.where(sub == 0, w_a, jnp.where(sub == 1, w_b, 0.0))
    return _rows_to_cols(gate_rows), gi * PAIRS_PER_GROUP + pair


def _outproj(xp, xs, yp, ys, zp, zs, mods3, g_norm, w_out, w_r, b_r):
    tps = DEC_SEQ // BIG_TILE
    per = BIG_TILE // TILE
    pidx = lambda i: (jnp.minimum(i, N_BIG_P - 1), 0)
    sidx = lambda i: (jnp.maximum(i - N_BIG_P, 0), 0)
    return pl.pallas_call(
        _outproj_kernel,
        out_shape=(jax.ShapeDtypeStruct((T_ALL, D_MODEL), F32),
                   jax.ShapeDtypeStruct((T_ALL, H2_EXT), F32),
                   jax.ShapeDtypeStruct((N_TILES, 1, TILE), jnp.int32)),
        grid=(N_BIG,),
        in_specs=[pl.BlockSpec((BIG_TILE, D_MODEL), pidx), pl.BlockSpec((BIG_TILE, D_MODEL), sidx),
                  pl.BlockSpec((BIG_TILE, ML_WIDTH), pidx), pl.BlockSpec((BIG_TILE, ML_WIDTH), sidx),
                  pl.BlockSpec((BIG_TILE, HY_WIDTH), pidx), pl.BlockSpec((BIG_TILE, HY_WIDTH), sidx),
                  pl.BlockSpec((1, N_MOD, D_MODEL), lambda i: (_mod_row_of_tile(i, tps, N_BIG_P), 0, 0)),
                  pl.BlockSpec((4, D_MODEL), lambda i: (0, 0)),
                  pl.BlockSpec((D_MODEL, D_MODEL), lambda i: (0, 0)),
                  pl.BlockSpec((ROUTER_ROWS, D_MODEL), lambda i: (0, 0)),
                  pl.BlockSpec((ROUTER_ROWS, 1), lambda i: (0, 0))],
        out_specs=(pl.BlockSpec((BIG_TILE, D_MODEL), lambda i: (i, 0)),
                   pl.BlockSpec((BIG_TILE, H2_EXT), lambda i: (i, 0)),
                   pl.BlockSpec((per, 1, TILE), lambda i: (i, 0, 0))),
        compiler_params=_cparams(("arbitrary",)),
        name="out_proj_router",
    )(xp, xs, yp, ys, zp, zs, mods3, g_norm, w_out, w_r, b_r)


def _route_kernel(bid_ref, pos_ref, meta_ref):
    nb = 32
    tm = float(ROW_TILE)
    sub = lax.broadcasted_iota(jnp.int32, (nb, TILE), 0)
    ri = lax.broadcasted_iota(jnp.int32, (TILE, TILE), 0)
    ci = lax.broadcasted_iota(jnp.int32, (TILE, TILE), 1)
    before = jnp.where(ri < ci, 1.0, 0.0).astype(BF16)

    def onehot(blk):
        return jnp.where(sub == bid_ref[blk], 1.0, 0.0)

    zeros = jnp.zeros((nb, 1), F32)
    cnt = lax.fori_loop(0, N_TILES, lambda blk, c: c + jnp.sum(onehot(blk), axis=1, keepdims=True), zeros)
    padded = jnp.floor((cnt + (tm - 1.0)) * (1.0 / tm)) * tm
    r32 = lax.broadcasted_iota(jnp.int32, (nb, nb), 0)
    c32 = lax.broadcasted_iota(jnp.int32, (nb, nb), 1)
    padded_row = jnp.sum(jnp.where(r32 == c32, padded, 0.0), axis=0, keepdims=True)
    offs = jnp.sum(jnp.where(c32 < r32, padded_row, 0.0), axis=1, keepdims=True)
    ends = offs + padded

    def place(blk, seen):
        oh = onehot(blk)
        rank = _dot(oh.astype(BF16), before)
        pos = jnp.sum(oh * (rank + seen + offs), axis=0, keepdims=True)
        pos_ref[blk] = pos.astype(jnp.int32)
        return seen + jnp.sum(oh, axis=1, keepdims=True)

    lax.fori_loop(0, N_TILES, place, zeros)

    start = lax.broadcasted_iota(jnp.int32, (nb, 128), 1).astype(F32) * tm
    bsub = lax.broadcasted_iota(jnp.int32, (nb, 128), 0)
    done = jnp.where((bsub < N_BUCKETS) & (ends <= start), 1.0, 0.0)
    tb = jnp.sum(done, axis=0, keepdims=True)
    valid = jnp.where(tb < N_BUCKETS, 1.0, 0.0)
    tbc = jnp.minimum(tb, N_BUCKETS - 1.0)
    grp = jnp.floor((tbc + 0.5) * (1.0 / PAIRS_PER_GROUP))
    pair = tbc - PAIRS_PER_GROUP * grp
    loc_a = jnp.zeros_like(pair)
    loc_b = jnp.zeros_like(pair)
    for k, (sa, sb) in enumerate(PAIR_SLOTS):
        loc_a = jnp.where(pair == k, float(sa), loc_a)
        loc_b = jnp.where(pair == k, float(sb), loc_b)
    mine = bsub.astype(F32) == tbc
    used = jnp.sum(jnp.where(mine, offs + cnt, 0.0), axis=0, keepdims=True)
    n_rows = jnp.clip(used - start[0:1], 0.0, tm) * valid
    row8 = lax.broadcasted_iota(jnp.int32, (8, 128), 0)
    meta = jnp.where(row8 == 0, grp * EXPERTS_PER_GROUP + loc_a,
                     jnp.where(row8 == 1, grp * EXPERTS_PER_GROUP + loc_b,
                               jnp.where(row8 == 2, valid, jnp.where(row8 == 3, n_rows, 0.0))))
    meta_ref[...] = meta.astype(jnp.int32)


def _route(bid):
    return pl.pallas_call(
        _route_kernel,
        out_shape=(jax.ShapeDtypeStruct((N_TILES, 1, TILE), jnp.int32),
                   jax.ShapeDtypeStruct((8, 128), jnp.int32)),
        compiler_params=pltpu.CompilerParams(vmem_limit_bytes=VMEM_LIMIT),
        name="moe_route",
    )(bid)


def _moe_kernel(meta_ref, pos_ref, h2_hbm, wga_ref, wua_ref, wda_ref, wgb_ref, wub_ref, wdb_ref,
                y_ref, src_ref, xbuf, sem, wga_s, wua_s, wda_s, wgb_s, wub_s, wdb_s):
    j = pl.program_id(0)

    def row_copy(tile, r, slot):
        tok = src_ref[tile * ROW_TILE + r]
        return pltpu.make_async_copy(h2_hbm.at[pl.ds(tok, 1), :], xbuf.at[slot, pl.ds(r, 1), :], sem.at[slot])

    group = 8

    def row_groups(tile):
        return (meta_ref[3, tile] + (group - 1)) // group

    def issue(tile, slot):
        def body(g, c):
            for k in range(group):
                row_copy(tile, g * group + k, slot).start()
            return c
        lax.fori_loop(0, row_groups(tile), body, 0)

    def wait(tile, slot):
        def body(g, c):
            for k in range(group):
                row_copy(tile, g * group + k, slot).wait()
            return c
        lax.fori_loop(0, row_groups(tile), body, 0)

    @pl.when(j == 0)
    def _():
        xbuf[...] = jnp.zeros_like(xbuf)

        def clear(t, c):
            n = meta_ref[3, t]
            for k in range(group - 1):
                src_ref[t * ROW_TILE + jnp.minimum(n + k, ROW_TILE - 1)] = 0
            return c
        lax.fori_loop(0, N_ROW_TILES, clear, 0)

        def invert(t, c):
            src_ref[pos_ref[t]] = t
            return c
        lax.fori_loop(0, T_ALL, invert, 0, unroll=8)

        @pl.when(meta_ref[2, 0] == 1)
        def _():
            issue(0, 0)

    nxt = jnp.minimum(j + 1, N_ROW_TILES - 1)

    @pl.when((j + 1 < N_ROW_TILES) & (meta_ref[2, nxt] == 1))
    def _():
        issue(nxt, nxt % 2)

    valid = meta_ref[2, j] == 1
    prev = jnp.maximum(j - 1, 0)

    @pl.when(valid & ((j == 0) | (meta_ref[0, j] != meta_ref[0, prev])))
    def _():
        wga_s[...] = wga_ref[0].astype(BF16)
        wua_s[...] = wua_ref[0].astype(BF16)
        wda_s[...] = wda_ref[0].astype(BF16)

    @pl.when(valid & ((j == 0) | (meta_ref[1, j] != meta_ref[1, prev])))
    def _():
        wgb_s[...] = wgb_ref[0].astype(BF16)
        wub_s[...] = wub_ref[0].astype(BF16)
        wdb_s[...] = wdb_ref[0].astype(BF16)

    @pl.when(valid)
    def _():
        slot = j % 2
        wait(j, slot)
        x = xbuf[slot, :, 0:D_MODEL].astype(BF16)
        gates = xbuf[slot, :, D_MODEL:H2_EXT]

        hg_a = _dot(x, wga_s[...])
        hu_a = _dot(x, wua_s[...])
        hg_b = _dot(x, wgb_s[...])
        hu_b = _dot(x, wub_s[...])
        act_a = (hg_a * jax.nn.sigmoid(hg_a) * hu_a * gates[:, 0:1]).astype(BF16)
        act_b = (hg_b * jax.nn.sigmoid(hg_b) * hu_b * gates[:, 1:2]).astype(BF16)
        y_ref[...] = _dot(act_a, wda_s[...]) + _dot(act_b, wdb_s[...])

    @pl.when(jnp.logical_not(valid))
    def _():
        y_ref[...] = jnp.zeros_like(y_ref)


def _moe(meta, pos, h2ext, w_gate, w_up, w_down):
    up_spec = lambda slot: pl.BlockSpec((1, D_MODEL, EXPERT_FF), lambda j, meta, pos: (meta[slot, j], 0, 0))
    down_spec = lambda slot: pl.BlockSpec((1, EXPERT_FF, D_MODEL), lambda j, meta, pos: (meta[slot, j], 0, 0))
    grid_spec = pltpu.PrefetchScalarGridSpec(
        num_scalar_prefetch=2,
        grid=(N_ROW_TILES,),
        in_specs=[pl.BlockSpec(memory_space=pl.ANY),
                  up_spec(0), up_spec(0), down_spec(0), up_spec(1), up_spec(1), down_spec(1)],
        out_specs=pl.BlockSpec((ROW_TILE, D_MODEL), lambda j, meta, pos: (j, 0)),
        scratch_shapes=[pltpu.SMEM((ROW_CAP,), jnp.int32),
                        pltpu.VMEM((2, ROW_TILE, H2_EXT), F32),
                        pltpu.SemaphoreType.DMA((2,)),
                        pltpu.VMEM((D_MODEL, EXPERT_FF), BF16), pltpu.VMEM((D_MODEL, EXPERT_FF), BF16),
                        pltpu.VMEM((EXPERT_FF, D_MODEL), BF16),
                        pltpu.VMEM((D_MODEL, EXPERT_FF), BF16), pltpu.VMEM((D_MODEL, EXPERT_FF), BF16),
                        pltpu.VMEM((EXPERT_FF, D_MODEL), BF16)])
    return pl.pallas_call(
        _moe_kernel,
        out_shape=jax.ShapeDtypeStruct((ROW_CAP, D_MODEL), F32),
        grid_spec=grid_spec,
        compiler_params=_cparams(("arbitrary",)),
        name="moe_experts",
    )(meta, pos, h2ext, w_gate, w_up, w_down, w_gate, w_up, w_down)


def _final_kernel(pos_ref, y_hbm, x1_ref, m_ref, gn_ref, op_ref, os_ref, ybuf, sem):
    i = pl.program_id(0)

    def row_copy(tile, r, slot):
        p = pos_ref[tile * TILE + r]
        return pltpu.make_async_copy(y_hbm.at[pl.ds(p, 1), :], ybuf.at[slot, pl.ds(r, 1), :], sem.at[slot])

    def issue(tile, slot):
        def body(r2, c):
            row_copy(tile, 2 * r2, slot).start(priority=0)
            row_copy(tile, 2 * r2 + 1, slot).start(priority=1)
            return c
        lax.fori_loop(0, TILE // 2, body, 0, unroll=4)

    def wait(slot):
        pltpu.make_async_copy(y_hbm.at[pl.ds(0, TILE)], ybuf.at[slot], sem.at[slot]).wait()

    @pl.when(i == 0)
    def _():
        issue(0, 0)

    @pl.when(i + 1 < N_TILES)
    def _():
        issue(i + 1, (i + 1) % 2)

    slot = i % 2
    wait(slot)
    out = x1_ref[...] + m_ref[0, 5:6, :] * _rms(ybuf[slot], gn_ref[3:4, :])

    @pl.when(i < N_TILES_P)
    def _():
        op_ref[...] = out

    @pl.when(i >= N_TILES_P)
    def _():
        os_ref[...] = out


def _final(pos, y_sorted, x1, mods3, g_norm):
    tps = DEC_SEQ // TILE
    grid_spec = pltpu.PrefetchScalarGridSpec(
        num_scalar_prefetch=1,
        grid=(N_TILES,),
        in_specs=[pl.BlockSpec(memory_space=pl.ANY),
                  pl.BlockSpec((TILE, D_MODEL), lambda i, pos: (i, 0)),
                  pl.BlockSpec((1, N_MOD, D_MODEL), lambda i, pos: (_mod_row_of_tile(i, tps, N_TILES_P), 0, 0)),
                  pl.BlockSpec((4, D_MODEL), lambda i, pos: (0, 0))],
        out_specs=(pl.BlockSpec((TILE, D_MODEL), lambda i, pos: (jnp.minimum(i, N_TILES_P - 1), 0)),
                   pl.BlockSpec((TILE, D_MODEL), lambda i, pos: (jnp.maximum(i - N_TILES_P, 0), 0))),
        scratch_shapes=[pltpu.VMEM((2, TILE, D_MODEL), F32), pltpu.SemaphoreType.DMA((2,))])
    return pl.pallas_call(
        _final_kernel,
        out_shape=(jax.ShapeDtypeStruct((T_PROMPT, D_MODEL), F32),
                   jax.ShapeDtypeStruct((T_SAMPLE, D_MODEL), F32)),
        grid_spec=grid_spec,
        compiler_params=_cparams(("arbitrary",)),
        name="moe_combine_final",
    )(pos, y_sorted, x1, mods3, g_norm)


def kernel(x_prompt, x_sample, state_C, state_n, state_m, c, c_ctx, w_ada, b_ada, g_norm, w_in, ml_gate_bias, ml_head_gain, hy_conv_w, hy_f_w1, hy_f_b1, hy_f_w2, hy_f_b2, hy_f_w3, hy_f_b3, hy_decay, hy_bias, w_out, w_rc, b_rc, w_rf, b_rf, w_gate, w_up, w_down):
    xp = x_prompt.reshape(T_PROMPT, D_MODEL)
    xs = x_sample.reshape(T_SAMPLE, D_MODEL)
    gn = g_norm[0]

    cv = jnp.concatenate([c_ctx[None, :], c, jnp.zeros((MOD_ROWS - 1 - DEC_BATCH, D_MODEL), F32)], axis=0)
    mods3 = _ada(cv, w_ada[0], b_ada[0]).reshape(MOD_ROWS, N_MOD, D_MODEL)

    w_in0 = w_in[0]
    w_qkvo = w_in0[:, :ML_QKVO_COLS].astype(BF16)
    w_hy = w_in0[:, ML_QKVO_COLS + ML_GATE_COLS:].astype(BF16)
    wg = w_in0[:, ML_QKVO_COLS:ML_QKVO_COLS + ML_GATE_COLS]
    gbt = ml_gate_bias[0].reshape(ML_GATE_COLS, 1)
    proj, gates, gates_t = _inproj(xp, xs, mods3, gn, w_qkvo, w_hy, wg.T, gbt)

    gain = ml_head_gain[0].reshape(1, ML_WIDTH)
    y_ml_p, c_new, n_new, m_new = _mlstm(proj, gates, gates_t, gain, None, SEQ, BATCH, 0)
    state = (state_C[:, 0], state_n[:, 0], state_m[:, 0].reshape(DEC_BATCH, 2 * ML_HEADS, 1))
    y_ml_s, _, _, _ = _mlstm(proj, gates, gates_t, gain, state, DEC_SEQ, DEC_BATCH, T_PROMPT // DEC_SEQ)

    w1p = jnp.pad(hy_f_w1[0], ((0, 128 - HY_EMB), (0, 0)))
    b1 = hy_f_b1[0].reshape(1, -1)
    b2 = hy_f_b2[0].reshape(1, -1)
    b3 = hy_f_b3[0].reshape(1, -1)
    dec = hy_decay[0].reshape(1, -1)
    z_parts = []
    for seq_len, n_seq, row_off, width, seqs in ((SEQ, BATCH, 0, SEQ, 4), (DEC_SEQ, DEC_BATCH, T_PROMPT, GRID_W, 2)):
        f, ft = _dft_mats(seq_len)
        coefs = _hyena_filters(seq_len, f, w1p, b1, hy_f_w2[0], b2, hy_f_w3[0], b3, dec)
        z_parts.append(_hyena(proj, hy_conv_w[0], coefs, hy_bias[0], f, ft, seq_len, n_seq, row_off, width, seqs))
    z_p, z_s = z_parts

    pad_r = ROUTER_ROWS - N_GROUPS - N_EXPERTS
    w_r = jnp.pad(jnp.concatenate([w_rc[0], w_rf[0]], axis=1).T, ((0, pad_r), (0, 0)))
    b_r = jnp.pad(jnp.concatenate([b_rc[0], b_rf[0]], axis=0), (0, pad_r)).reshape(ROUTER_ROWS, 1)
    x1, h2ext, bid = _outproj(xp, xs, y_ml_p, y_ml_s, z_p, z_s, mods3, gn, w_out[0].astype(BF16), w_r, b_r)

    pos3, meta = _route(bid)
    pos = pos3.reshape(T_ALL)
    y_sorted = _moe(meta, pos, h2ext, w_gate[0], w_up[0], w_down[0])
    y_p, y_s = _final(pos, y_sorted, x1, mods3, gn)

    new_c = c_new.reshape(BATCH, 1, 2, ML_HEADS, ML_HEAD_DIM, ML_HEAD_DIM)
    new_n = n_new.reshape(BATCH, 1, 2, ML_HEADS, ML_HEAD_DIM)
    new_m = m_new[:, :, 0].reshape(BATCH, 1, 2, ML_HEADS)
    return (y_p.reshape(BATCH, SEQ, D_MODEL), y_s.reshape(DEC_BATCH, DEC_SEQ, D_MODEL), new_c, new_n, new_m)
```

```python
import functools
import math

import jax
import jax.numpy as jnp
import numpy as np
from jax import lax
from jax.experimental import pallas as pl
from jax.experimental.pallas import tpu as pltpu

F32 = jnp.float32
BF16 = jnp.bfloat16

D_MODEL = 1024
BATCH = 16
SEQ = 256
DEC_BATCH = 4
DEC_SEQ = 1024
GRID_W = 64
ML_WIDTH = 512
ML_HEADS = 4
ML_HEAD_DIM = 128
HY_WIDTH = 512
HY_ORDER = 2
HY_EMB = 33
HY_BANDS = 16
HY_FILTER_HIDDEN = 64
HY_MOD_SHIFT = 0.05
N_GROUPS = 4
EXPERTS_PER_GROUP = 4
N_EXPERTS = 16
EXPERT_FF = 512
N_MOD = 6
EPS = 1e-6
ML_QKVO_COLS = 4 * ML_WIDTH
ML_GATE_COLS = 4 * ML_HEADS
HY_COLS = 3 * HY_WIDTH
MAIN_COLS = ML_QKVO_COLS + HY_COLS

T_PROMPT = BATCH * SEQ
T_SAMPLE = DEC_BATCH * DEC_SEQ
T_ALL = T_PROMPT + T_SAMPLE
TILE = 256
N_TILES_P = T_PROMPT // TILE
N_TILES = T_ALL // TILE
MOD_ROWS = 8
K_SCALE = ML_HEAD_DIM ** -0.5
VMEM_LIMIT = 56 * 1024 * 1024


def _cparams(sem):
    return pltpu.CompilerParams(dimension_semantics=sem, vmem_limit_bytes=VMEM_LIMIT)


def _split2(x):
    hi = x.astype(BF16)
    lo = (x - hi.astype(F32)).astype(BF16)
    return hi, lo


def _dot(a, b):
    return jnp.dot(a, b, preferred_element_type=F32)


def _dot_nt(a, b):
    return lax.dot_general(a, b, (((1,), (1,)), ((), ())), preferred_element_type=F32)


def _dot_tn(a, b):
    return lax.dot_general(a, b, (((0,), (0,)), ((), ())), preferred_element_type=F32)


def _dot3(a, b):
    ah, al = _split2(a)
    bh, bl = _split2(b)
    return _dot(ah, bh) + _dot(al, bh) + _dot(ah, bl)


def _dot3_nt(a, b):
    ah, al = _split2(a)
    bh, bl = _split2(b)
    return _dot_nt(ah, bh) + _dot_nt(al, bh) + _dot_nt(ah, bl)


def _dot_exact_lhs(t, x):
    x1 = x.astype(BF16)
    r1 = x - x1.astype(F32)
    x2 = r1.astype(BF16)
    x3 = (r1 - x2.astype(F32)).astype(BF16)
    return _dot(t, x1) + _dot(t, x2) + _dot(t, x3)


def _dot_exact_rhs(x, t):
    x1 = x.astype(BF16)
    r1 = x - x1.astype(F32)
    x2 = r1.astype(BF16)
    x3 = (r1 - x2.astype(F32)).astype(BF16)
    return _dot(x1, t) + _dot(x2, t) + _dot(x3, t)


def _rms(x, g):
    return x * lax.rsqrt(jnp.mean(x * x, axis=-1, keepdims=True) + EPS) * g


def _mod_row_of_tile(i, tiles_per_sample_seq, n_prompt_tiles):
    return jnp.where(i < n_prompt_tiles, 0, 1 + (i - n_prompt_tiles) // tiles_per_sample_seq)


def _ada_kernel(cv_ref, w_ref, b_ref, o_ref):
    cv = cv_ref[...]
    s = cv * jax.nn.sigmoid(cv)
    o_ref[...] = _dot3(s, w_ref[...]) + b_ref[...]


def _ada(cv, w_ada, b_ada):
    n = N_MOD * D_MODEL
    return pl.pallas_call(
        _ada_kernel,
        out_shape=jax.ShapeDtypeStruct((MOD_ROWS, n), F32),
        grid=(N_MOD,),
        in_specs=[pl.BlockSpec((MOD_ROWS, D_MODEL), lambda j: (0, 0)),
                  pl.BlockSpec((D_MODEL, D_MODEL), lambda j: (0, j)),
                  pl.BlockSpec((1, D_MODEL), lambda j: (0, j))],
        out_specs=pl.BlockSpec((MOD_ROWS, D_MODEL), lambda j: (0, j)),
        compiler_params=_cparams(("arbitrary",)),
        name="ada_mod",
    )(cv, w_ada, b_ada.reshape(1, n))


def _log_sigmoid(x):
    return jnp.minimum(x, 0.0) - jnp.log1p(jnp.exp(-jnp.abs(x)))


def _rows_to_cols(rows):
    ri = lax.broadcasted_iota(jnp.int32, (TILE, TILE), 0)
    ci = lax.broadcasted_iota(jnp.int32, (TILE, TILE), 1)
    eye = jnp.where(ri == ci, 1.0, 0.0).astype(BF16)
    p1 = rows.astype(BF16)
    r1 = rows - p1.astype(F32)
    p2 = r1.astype(BF16)
    p3 = (r1 - p2.astype(F32)).astype(BF16)
    return _dot_nt(eye, p1) + _dot_nt(eye, p2) + _dot_nt(eye, p3)


BIG_TILE = 2 * TILE
N_BIG_P = T_PROMPT // BIG_TILE
N_BIG = T_ALL // BIG_TILE


def _inproj_kernel(xp_ref, xs_ref, m_ref, gn_ref, wq_ref, wh_ref, wgt_ref, gbt_ref, proj_ref, gate_ref, gatet_ref):
    is_p = pl.program_id(0) < N_BIG_P
    halves = [slice(r * TILE, (r + 1) * TILE) for r in range(BIG_TILE // TILE)]
    hs = [_rms(jnp.where(is_p, xp_ref[rows, :], xs_ref[rows, :]), gn_ref[0:1, :]) * (1.0 + m_ref[0, 1:2, :])
          + m_ref[0, 0:1, :] for rows in halves]
    hbs = [h.astype(BF16) for h in hs]
    cb = 512
    for j in range(ML_QKVO_COLS // cb):
        for rows, hb in zip(halves, hbs):
            proj_ref[rows, j * cb:(j + 1) * cb] = _dot(hb, wq_ref[:, j * cb:(j + 1) * cb]).astype(BF16)
    for j in range(HY_COLS // cb):
        lo = ML_QKVO_COLS + j * cb
        for rows, hb in zip(halves, hbs):
            proj_ref[rows, lo:lo + cb] = _dot(hb, wh_ref[:, j * cb:(j + 1) * cb]).astype(BF16)
    wth, wtl = _split2(wgt_ref[...])
    gts = []
    for h, hb in zip(hs, hbs):
        hl = (h - hb.astype(F32)).astype(BF16)
        gt = _dot_nt(wth, hb) + _dot_nt(wth, hl) + _dot_nt(wtl, hb) + gbt_ref[...]
        row = lax.broadcasted_iota(jnp.int32, gt.shape, 0)
        gts.append(jnp.where((row % 8) >= 4, _log_sigmoid(gt), gt))
    for r, (rows, gt) in enumerate(zip(halves, gts)):
        gatet_ref[r] = gt
        gate_ref[rows, :] = _rows_to_cols(gt)


def _inproj(xp, xs, mods3, g_norm, w_qkvo, w_hy, wgt, gbt):
    tps = DEC_SEQ // BIG_TILE
    per = BIG_TILE // TILE
    return pl.pallas_call(
        _inproj_kernel,
        out_shape=(jax.ShapeDtypeStruct((T_ALL, MAIN_COLS), BF16),
                   jax.ShapeDtypeStruct((T_ALL, ML_GATE_COLS), F32),
                   jax.ShapeDtypeStruct((N_TILES, ML_GATE_COLS, TILE), F32)),
        grid=(N_BIG,),
        in_specs=[pl.BlockSpec((BIG_TILE, D_MODEL), lambda i: (jnp.minimum(i, N_BIG_P - 1), 0)),
                  pl.BlockSpec((BIG_TILE, D_MODEL), lambda i: (jnp.maximum(i - N_BIG_P, 0), 0)),
                  pl.BlockSpec((1, N_MOD, D_MODEL), lambda i: (_mod_row_of_tile(i, tps, N_BIG_P), 0, 0)),
                  pl.BlockSpec((4, D_MODEL), lambda i: (0, 0)),
                  pl.BlockSpec((D_MODEL, ML_QKVO_COLS), lambda i: (0, 0)),
                  pl.BlockSpec((D_MODEL, HY_COLS), lambda i: (0, 0)),
                  pl.BlockSpec((ML_GATE_COLS, D_MODEL), lambda i: (0, 0)),
                  pl.BlockSpec((ML_GATE_COLS, 1), lambda i: (0, 0))],
        out_specs=(pl.BlockSpec((BIG_TILE, MAIN_COLS), lambda i: (i, 0)),
                   pl.BlockSpec((BIG_TILE, ML_GATE_COLS), lambda i: (i, 0)),
                   pl.BlockSpec((per, ML_GATE_COLS, TILE), lambda i: (i, 0, 0))),
        compiler_params=_cparams(("arbitrary",)),
        name="in_proj",
    )(xp, xs, mods3, g_norm, w_qkvo, w_hy, wgt, gbt)


ST_ROWS = ML_HEAD_DIM + 16


def _mlstm_kernel(*refs, seq_len, has_state):
    if has_state:
        (q_ref, k_ref, v_ref, o_ref, g_ref, gt_ref, gain_ref, c0_ref, n0_ref, m0_ref,
         y_ref, c_ref, n_ref, m_ref, vt_ref, hf_ref, hb_ref, st_ref, ms_ref) = refs
    else:
        (q_ref, k_ref, v_ref, o_ref, g_ref, gt_ref, gain_ref,
         y_ref, c_ref, n_ref, m_ref, vt_ref, hf_ref, hb_ref, st_ref, ms_ref) = refs
    ch = TILE
    nc = seq_len // ch
    hd = ML_HEAD_DIM
    key = lax.broadcasted_iota(jnp.int32, (ch, ch), 0)
    qry = lax.broadcasted_iota(jnp.int32, (ch, ch), 1)
    key_le = key <= qry
    key_ge = key >= qry
    t_le = jnp.where(key_le, 1.0, 0.0).astype(BF16)
    t_ge = jnp.where(key_ge, 1.0, 0.0).astype(BF16)
    sub16 = lax.broadcasted_iota(jnp.int32, (16, ch), 0)
    ln_scale = math.log(K_SCALE)

    for c in range(nc):
        for h in range(ML_HEADS):
            cols = slice(h * hd, (h + 1) * hd)
            vt_ref[c, cols, :] = v_ref[c * ch:(c + 1) * ch, cols].astype(F32).T.astype(BF16)

    for d in range(2):
        for h in range(ML_HEADS):
            r = d * ML_HEADS + h
            st_ref[r] = jnp.zeros((ST_ROWS, hd), F32)
            if has_state:
                st_ref[r, 0:hd, :] = c0_ref[0, d, h].T
                st_ref[r, hd:hd + 1, :] = n0_ref[0, d, h:h + 1, :]
                ms_ref[r] = jnp.broadcast_to(m0_ref[0, r:r + 1, :], (1, ch))
            else:
                ms_ref[r] = jnp.zeros((1, ch), F32)

    def step(t, carry):
        for d in range(2):
            c = t if d == 0 else nc - 1 - t
            rows = pl.ds(pl.multiple_of(c * ch, ch), ch)
            gcol = g_ref[rows, :]
            grow = gt_ref[c]
            brow_all = _dot_exact_rhs(grow, t_le if d == 0 else t_ge)
            bcol_all = _dot_exact_lhs(t_ge if d == 0 else t_le, gcol)
            mask = key_le if d == 0 else key_ge
            hacc_ref = hf_ref if d == 0 else hb_ref
            heads = range(ML_HEADS)
            regs = [d * ML_HEADS + h for h in heads]
            colss = [slice(h * hd, (h + 1) * hd) for h in heads]
            qs = [q_ref[rows, cols] for cols in colss]
            ks = [k_ref[rows, cols] for cols in colss]
            vts = [vt_ref[c, cols, :] for cols in colss]
            sts = [st_ref[r] for r in regs]
            m_prevs = [ms_ref[r] for r in regs]
            b_rows = [brow_all[(1 + 2 * d) * ML_HEADS + h:(1 + 2 * d) * ML_HEADS + h + 1, :] for h in heads]
            ig_rows = [grow[2 * d * ML_HEADS + h:2 * d * ML_HEADS + h + 1, :] for h in heads]
            qks = [_dot_nt(k, q) for k, q in zip(ks, qs)]
            iqs = [_dot_nt(st.astype(BF16), q) for st, q in zip(sts, qs)]
            ss, sc_inters, m_poss = [], [], []
            for h in heads:
                fcol = (1 + 2 * d) * ML_HEADS + h
                icol = 2 * d * ML_HEADS + h
                c_col = gcol[:, icol:icol + 1] - bcol_all[:, fcol:fcol + 1]
                logd = jnp.where(mask, b_rows[h] + c_col, -jnp.inf)
                inter = b_rows[h] + m_prevs[h]
                m_pos = jnp.maximum(inter, jnp.max(logd, axis=0, keepdims=True))
                ss.append(qks[h] * jnp.exp(logd - (m_pos - ln_scale)))
                sc_inters.append(jnp.exp(inter - m_pos))
                m_poss.append(m_pos)
            pvs = [_dot(vt, s.astype(BF16)) for vt, s in zip(vts, ss)]
            for h in heads:
                num = sc_inters[h] * iqs[h][0:hd] + pvs[h]
                den = sc_inters[h] * iqs[h][hd:hd + 1] + jnp.sum(ss[h], axis=0, keepdims=True)
                hacc_ref[c, colss[h], :] = num * (1.0 / jnp.maximum(jnp.abs(den), jnp.exp(-m_poss[h])))
            lhss, decays = [], []
            for h in heads:
                b_row = b_rows[h]
                b_last = b_row[:, ch - 1:ch] if d == 0 else b_row[:, 0:1]
                logw = b_last - b_row + ig_rows[h]
                m_new = jnp.maximum(b_last + m_prevs[h], jnp.max(logw, axis=1, keepdims=True))
                w = jnp.exp(logw - (m_new - ln_scale))
                decays.append(jnp.exp(b_last + m_prevs[h] - m_new))
                lhss.append(jnp.concatenate([(vts[h].astype(F32) * w).astype(BF16),
                                             jnp.where(sub16 == 0, w, 0.0).astype(BF16)], axis=0))
                ms_ref[regs[h]] = m_new
            upds = [_dot(lhs, k) for lhs, k in zip(lhss, ks)]
            for h in heads:
                st_ref[regs[h]] = decays[h][:, 0:hd] * sts[h] + upds[h]
        return carry

    lax.fori_loop(0, nc, step, 0)

    for d in range(2):
        for h in range(ML_HEADS):
            r = d * ML_HEADS + h
            c_ref[0, d, h] = st_ref[r, 0:hd, :].T
            n_ref[0, d, h:h + 1, :] = st_ref[r, hd:hd + 1, :]
            m_ref[0, r:r + 1, :] = ms_ref[r][:, 0:hd]
    for c in range(nc):
        for h in range(ML_HEADS):
            cols = slice(h * hd, (h + 1) * hd)
            ht = hf_ref[c, cols, :] + hb_ref[c, cols, :]
            ht = ht * lax.rsqrt(jnp.mean(ht * ht, axis=0, keepdims=True) + EPS)
            rows = slice(c * ch, (c + 1) * ch)
            y = ht.T * gain_ref[:, cols] * jax.nn.sigmoid(o_ref[rows, cols].astype(F32))
            y_ref[rows, cols] = y.astype(BF16)


def _mlstm(proj, gates, gates_t, gain, state, seq_len, n_seq, row_block_off):
    has_state = state is not None
    tiles = seq_len // TILE
    off = row_block_off
    qkvo_specs = [pl.BlockSpec((seq_len, ML_WIDTH), functools.partial(lambda b, j: (off + b, j), j=j))
                  for j in range(4)]
    in_specs = qkvo_specs + [
        pl.BlockSpec((seq_len, ML_GATE_COLS), lambda b: (off + b, 0)),
        pl.BlockSpec((tiles, ML_GATE_COLS, TILE), lambda b: (off + b, 0, 0)),
        pl.BlockSpec((1, ML_WIDTH), lambda b: (0, 0)),
    ]
    args = [proj, proj, proj, proj, gates, gates_t, gain]
    if has_state:
        c0, n0, m0 = state
        in_specs += [
            pl.BlockSpec((1, 2, ML_HEADS, ML_HEAD_DIM, ML_HEAD_DIM), lambda b: (b, 0, 0, 0, 0)),
            pl.BlockSpec((1, 2, ML_HEADS, ML_HEAD_DIM), lambda b: (b, 0, 0, 0)),
            pl.BlockSpec((1, 2 * ML_HEADS, 1), lambda b: (b, 0, 0)),
        ]
        args += [c0, n0, m0]
    out_shape = (jax.ShapeDtypeStruct((n_seq * seq_len, ML_WIDTH), BF16),
                 jax.ShapeDtypeStruct((n_seq, 2, ML_HEADS, ML_HEAD_DIM, ML_HEAD_DIM), F32),
                 jax.ShapeDtypeStruct((n_seq, 2, ML_HEADS, ML_HEAD_DIM), F32),
                 jax.ShapeDtypeStruct((n_seq, 2 * ML_HEADS, ML_HEAD_DIM), F32))
    out_specs = (pl.BlockSpec((seq_len, ML_WIDTH), lambda b: (b, 0)),
                 pl.BlockSpec((1, 2, ML_HEADS, ML_HEAD_DIM, ML_HEAD_DIM), lambda b: (b, 0, 0, 0, 0)),
                 pl.BlockSpec((1, 2, ML_HEADS, ML_HEAD_DIM), lambda b: (b, 0, 0, 0)),
                 pl.BlockSpec((1, 2 * ML_HEADS, ML_HEAD_DIM), lambda b: (b, 0, 0)))
    scratch = [pltpu.VMEM((tiles, ML_WIDTH, TILE), BF16),
               pltpu.VMEM((tiles, ML_WIDTH, TILE), F32), pltpu.VMEM((tiles, ML_WIDTH, TILE), F32),
               pltpu.VMEM((2 * ML_HEADS, ST_ROWS, ML_HEAD_DIM), F32),
               pltpu.VMEM((2 * ML_HEADS, 1, TILE), F32)]
    return pl.pallas_call(
        functools.partial(_mlstm_kernel, seq_len=seq_len, has_state=has_state),
        out_shape=out_shape, grid=(n_seq,), in_specs=in_specs, out_specs=out_specs,
        scratch_shapes=scratch, compiler_params=_cparams(("arbitrary",)),
        name=f"mlstm_{seq_len}",
    )(*args)


def _dft_mats(seq_len):
    k = np.arange(seq_len, dtype=np.int64)[:, None]
    d = np.arange(seq_len, dtype=np.int64)[None, :]
    ang = np.pi * ((k * d) % (2 * seq_len)).astype(np.float64) / seq_len
    sinm = np.sin(ang)
    sinm[0, :] = np.where(d[0] % 2 == 0, 1.0, -1.0)
    f = np.concatenate([np.cos(ang), sinm], axis=0).astype(np.float32)
    return jnp.asarray(f).astype(BF16), jnp.asarray(np.ascontiguousarray(f.T)).astype(BF16)


def _filter_feats(seq_len):
    t = np.linspace(0.0, 1.0, seq_len, dtype=np.float64)[:, None]
    wpos = 2.0 * np.pi * np.arange(seq_len, dtype=np.float64)[:, None] / seq_len
    bands = np.linspace(1e-4, HY_BANDS - 1, HY_BANDS, dtype=np.float64)[None, :]
    z = np.concatenate([t, np.cos(bands * wpos), -np.sin(bands * wpos)], axis=-1)
    return jnp.asarray(np.pad(z, ((0, 0), (0, 128 - HY_EMB))).astype(np.float32))


def _filter_kernel(z_ref, w1_ref, b1_ref, w2_ref, b2_ref, w3_ref, b3_ref, dec_ref, f_ref,
                   a_ref, b_ref, d_ref, *, seq_len):
    n = 2 * seq_len
    z = z_ref[...]
    h = jnp.sin(_dot3(z, w1_ref[...]) + b1_ref[...])
    h = jnp.sin(_dot3(h, w2_ref[...]) + b2_ref[...])
    h = _dot3(h, w3_ref[...]) + b3_ref[...]
    t = z[:, 0:1]
    h = h * (jnp.exp(-t * jnp.abs(dec_ref[...])) + HY_MOD_SHIFT)
    ss = jnp.sum(h * h, axis=0, keepdims=True)
    inv = lax.rsqrt(ss[:, :HY_WIDTH] + ss[:, HY_WIDTH:] + EPS)
    hp = h[:, :HY_WIDTH] * inv
    hn = h[:, HY_WIDTH:] * inv
    ssum = hp + hn
    sdif = hp - hn
    hc = _dot(f_ref[0:seq_len, :], ssum.astype(BF16))
    hs = _dot(f_ref[seq_len:n, :], sdif.astype(BF16))
    di = lax.broadcasted_iota(jnp.int32, (seq_len, 1), 0)
    sgn = jnp.where(di % 2 == 0, 1.0, -1.0)
    nyq = jnp.sum(ssum * sgn, axis=0, keepdims=True)
    first = di == 0
    a_ref[0] = hc * jnp.where(first, 1.0 / n, 2.0 / n)
    b_ref[0] = jnp.where(first, 0.0, hs * (2.0 / n))
    d_ref[0] = jnp.where(first, nyq * (1.0 / n), hc * (2.0 / n))


def _hyena_filters(seq_len, f, w1p, b1, w2, b2, w3, b3, dec):
    z = _filter_feats(seq_len)
    hid = HY_FILTER_HIDDEN
    oc = 2 * HY_WIDTH
    full = lambda shape: pl.BlockSpec(shape, lambda o: tuple(0 for _ in shape))
    out = jax.ShapeDtypeStruct((HY_ORDER, seq_len, HY_WIDTH), F32)
    return pl.pallas_call(
        functools.partial(_filter_kernel, seq_len=seq_len),
        out_shape=(out, out, out),
        grid=(HY_ORDER,),
        in_specs=[full((seq_len, 128)), full((128, hid)), full((1, hid)), full((hid, hid)), full((1, hid)),
                  pl.BlockSpec((hid, oc), lambda o: (0, o)),
                  pl.BlockSpec((1, oc), lambda o: (0, o)),
                  pl.BlockSpec((1, oc), lambda o: (0, o)),
                  full((2 * seq_len, seq_len))],
        out_specs=tuple(pl.BlockSpec((1, seq_len, HY_WIDTH), lambda o: (o, 0, 0)) for _ in range(3)),
        compiler_params=_cparams(("arbitrary",)),
        name=f"hyena_filter_{seq_len}",
    )(z, w1p, b1, w2, b2, w3, b3, dec, f)


def _hyena_kernel(x1_ref, x2_ref, v_ref, cw1_ref, cw2_ref, cwv_ref, a_ref, b_ref, d_ref, bias_ref,
                  f_ref, ft_ref, z_ref, *, seq_len, width, seqs):
    rows = seqs * seq_len
    ti = lax.broadcasted_iota(jnp.int32, (rows, 1), 0)
    has_prev = (ti % width) != 0
    has_next = (ti % width) != (width - 1)

    def short_conv(x_ref, w_ref):
        x = x_ref[...].astype(F32)
        prev = jnp.where(has_prev, pltpu.roll(x, 1, axis=0), 0.0)
        nxt = jnp.where(has_next, pltpu.roll(x, rows - 1, axis=0), 0.0)
        return w_ref[0:1, :] * prev + w_ref[1:2, :] * x + w_ref[2:3, :] * nxt

    gates = (short_conv(x1_ref, cw1_ref), short_conv(x2_ref, cw2_ref))
    v = short_conv(v_ref, cwv_ref)
    sls = [slice(i * seq_len, (i + 1) * seq_len) for i in range(seqs)]
    zs = [v[sl] for sl in sls]
    for o in range(HY_ORDER):
        a, b, dd = a_ref[o], b_ref[o], d_ref[o]
        us = [_dot(f_ref[...], z.astype(BF16)) for z in zs]
        ys = []
        for u in us:
            ut = u[:seq_len]
            ub = u[seq_len:]
            ys.append(((ut * a - ub * b).astype(BF16), (ut * b + ub * dd).astype(BF16)))
        convs = [_dot(ft_ref[:, :seq_len], yt) + _dot(ft_ref[:, seq_len:], yb) for yt, yb in ys]
        zs = [gates[o][sl] * (y + bias_ref[o:o + 1, :] * z) for sl, y, z in zip(sls, convs, zs)]
    for sl, z in zip(sls, zs):
        z_ref[sl, :] = z.astype(BF16)


def _hyena(proj, conv_w, coefs, hy_bias, f, ft, seq_len, n_seq, row_off, width, seqs):
    cb = 256
    nblk = HY_WIDTH // cb
    base = ML_QKVO_COLS // cb
    rows = seqs * seq_len
    off = row_off // rows
    a, b, d = coefs

    def col_spec(part):
        return pl.BlockSpec((rows, cb), lambda j, s: (off + s, base + part * nblk + j))

    def w_spec(part):
        return pl.BlockSpec((3, cb), lambda j, s: (0, part * nblk + j))

    coef_spec = pl.BlockSpec((HY_ORDER, seq_len, cb), lambda j, s: (0, 0, j))
    return pl.pallas_call(
        functools.partial(_hyena_kernel, seq_len=seq_len, width=width, seqs=seqs),
        out_shape=jax.ShapeDtypeStruct((n_seq * seq_len, HY_WIDTH), BF16),
        grid=(nblk, n_seq // seqs),
        in_specs=[col_spec(0), col_spec(1), col_spec(2), w_spec(0), w_spec(1), w_spec(2),
                  coef_spec, coef_spec, coef_spec,
                  pl.BlockSpec((HY_ORDER, cb), lambda j, s: (0, j)),
                  pl.BlockSpec((2 * seq_len, seq_len), lambda j, s: (0, 0)),
                  pl.BlockSpec((seq_len, 2 * seq_len), lambda j, s: (0, 0))],
        out_specs=pl.BlockSpec((rows, cb), lambda j, s: (s, j)),
        compiler_params=_cparams(("arbitrary", "arbitrary")),
        name=f"hyena_conv_{seq_len}",
    )(proj, proj, proj, conv_w, conv_w, conv_w, a, b, d, hy_bias, f, ft)


def _first_max(x, n):
    mx = jnp.max(x, axis=0, keepdims=True)
    row = lax.broadcasted_iota(jnp.int32, x.shape, 0).astype(F32)
    idx = jnp.min(jnp.where(x == mx, row, float(n)), axis=0, keepdims=True)
    return mx, idx.astype(jnp.int32)


ROUTER_ROWS = 32
PAIRS_PER_GROUP = 6
N_BUCKETS = N_GROUPS * PAIRS_PER_GROUP
PAIR_SLOTS = ((0, 1), (0, 2), (0, 3), (1, 3), (1, 2), (3, 2))
LANES = 128
H2_EXT = D_MODEL + LANES
ROW_TILE = 256
ROW_CAP = T_ALL + N_BUCKETS * ROW_TILE
N_ROW_TILES = ROW_CAP // ROW_TILE


def _outproj_kernel(xp_ref, xs_ref, yp_ref, ys_ref, zp_ref, zs_ref, m_ref, gn_ref, wo_ref, wr_ref, br_ref,
                    x1_ref, h2_ref, bid_ref):
    is_p = pl.program_id(0) < N_BIG_P
    wrh, wrl = _split2(wr_ref[...])
    halves = [slice(r * TILE, (r + 1) * TILE) for r in range(BIG_TILE // TILE)]
    ys = [_dot(jnp.where(is_p, yp_ref[rows, :], ys_ref[rows, :]), wo_ref[0:ML_WIDTH, :])
          + _dot(jnp.where(is_p, zp_ref[rows, :], zs_ref[rows, :]), wo_ref[ML_WIDTH:, :]) for rows in halves]
    h2s = []
    for rows, y in zip(halves, ys):
        x = jnp.where(is_p, xp_ref[rows, :], xs_ref[rows, :])
        x1 = x + m_ref[0, 2:3, :] * _rms(y, gn_ref[1:2, :])
        x1_ref[rows, :] = x1
        h2 = _rms(x1, gn_ref[2:3, :]) * (1.0 + m_ref[0, 4:5, :]) + m_ref[0, 3:4, :]
        h2_ref[rows, 0:D_MODEL] = h2
        h2s.append(h2)
    logits = []
    for h2 in h2s:
        h2h, h2l = _split2(h2)
        logits.append(_dot_nt(wrh, h2h) + _dot_nt(wrh, h2l) + _dot_nt(wrl, h2h) + br_ref[...])
    routed = [_route_tile(lg) for lg in logits]
    for r, (rows, (gate_rows, bucket)) in enumerate(zip(halves, routed)):
        h2_ref[rows, D_MODEL:H2_EXT] = jnp.zeros((TILE, LANES), F32)
        h2_ref[rows, D_MODEL:D_MODEL + 8] = _rows_to_cols(gate_rows)
        bid_ref[r] = bucket


def _route_tile(logits):
    lc = logits[0:N_GROUPS]
    mx, gi = _first_max(lc, N_GROUPS)
    p_grp = 1.0 / jnp.sum(jnp.exp(lc - mx), axis=0, keepdims=True)
    lsel = jnp.zeros((EXPERTS_PER_GROUP, TILE), F32)
    for g in range(N_GROUPS):
        lo = N_GROUPS + g * EXPERTS_PER_GROUP
        lsel = jnp.where(gi == g, logits[lo:lo + EXPERTS_PER_GROUP], lsel)
    l1, i1 = _first_max(lsel, EXPERTS_PER_GROUP)
    sub4 = lax.broadcasted_iota(jnp.int32, lsel.shape, 0)
    l2, i2 = _first_max(jnp.where(sub4 == i1, -jnp.inf, lsel), EXPERTS_PER_GROUP)
    e2 = jnp.exp(l2 - l1)
    w1 = p_grp / (1.0 + e2)
    w2 = p_grp * e2 / (1.0 + e2)
    lo_e = jnp.minimum(i1, i2)
    hi_e = jnp.maximum(i1, i2)
    pair = jnp.where(lo_e == 0, hi_e - 1, jnp.where(lo_e == 1, jnp.where(hi_e == 3, 3, 4), 5))
    slot_a = jnp.where(pair == 5, hi_e, lo_e)
    first_in_a = i1 == slot_a
    w_a = jnp.where(first_in_a, w1, w2)
    w_b = jnp.where(first_in_a, w2, w1)
    sub = lax.broadcasted_iota(jnp.int32, (8, TILE), 0)
    gate_rows = jnp.where(sub == 0, w_a, jnp.where(sub == 1, w_b, 0.0))
    return gate_rows, gi * PAIRS_PER_GROUP + pair


def _outproj(xp, xs, yp, ys, zp, zs, mods3, g_norm, w_out, w_r, b_r):
    tps = DEC_SEQ // BIG_TILE
    per = BIG_TILE // TILE
    pidx = lambda i: (jnp.minimum(i, N_BIG_P - 1), 0)
    sidx = lambda i: (jnp.maximum(i - N_BIG_P, 0), 0)
    return pl.pallas_call(
        _outproj_kernel,
        out_shape=(jax.ShapeDtypeStruct((T_ALL, D_MODEL), F32),
                   jax.ShapeDtypeStruct((T_ALL, H2_EXT), F32),
                   jax.ShapeDtypeStruct((N_TILES, 1, TILE), jnp.int32)),
        grid=(N_BIG,),
        in_specs=[pl.BlockSpec((BIG_TILE, D_MODEL), pidx), pl.BlockSpec((BIG_TILE, D_MODEL), sidx),
                  pl.BlockSpec((BIG_TILE, ML_WIDTH), pidx), pl.BlockSpec((BIG_TILE, ML_WIDTH), sidx),
                  pl.BlockSpec((BIG_TILE, HY_WIDTH), pidx), pl.BlockSpec((BIG_TILE, HY_WIDTH), sidx),
                  pl.BlockSpec((1, N_MOD, D_MODEL), lambda i: (_mod_row_of_tile(i, tps, N_BIG_P), 0, 0)),
                  pl.BlockSpec((4, D_MODEL), lambda i: (0, 0)),
                  pl.BlockSpec((D_MODEL, D_MODEL), lambda i: (0, 0)),
                  pl.BlockSpec((ROUTER_ROWS, D_MODEL), lambda i: (0, 0)),
                  pl.BlockSpec((ROUTER_ROWS, 1), lambda i: (0, 0))],
        out_specs=(pl.BlockSpec((BIG_TILE, D_MODEL), lambda i: (i, 0)),
                   pl.BlockSpec((BIG_TILE, H2_EXT), lambda i: (i, 0)),
                   pl.BlockSpec((per, 1, TILE), lambda i: (i, 0, 0))),
        compiler_params=_cparams(("arbitrary",)),
        name="out_proj_router",
    )(xp, xs, yp, ys, zp, zs, mods3, g_norm, w_out, w_r, b_r)


def _route_kernel(bid_ref, pos_ref, meta_ref):
    nb = 32
    tm = float(ROW_TILE)
    sub = lax.broadcasted_iota(jnp.int32, (nb, TILE), 0)
    ri = lax.broadcasted_iota(jnp.int32, (TILE, TILE), 0)
    ci = lax.broadcasted_iota(jnp.int32, (TILE, TILE), 1)
    before = jnp.where(ri < ci, 1.0, 0.0).astype(BF16)

    def onehot(blk):
        return jnp.where(sub == bid_ref[blk], 1.0, 0.0)

    zeros = jnp.zeros((nb, 1), F32)
    cnt = lax.fori_loop(0, N_TILES, lambda blk, c: c + jnp.sum(onehot(blk), axis=1, keepdims=True), zeros)
    padded = jnp.floor((cnt + (tm - 1.0)) * (1.0 / tm)) * tm
    r32 = lax.broadcasted_iota(jnp.int32, (nb, nb), 0)
    c32 = lax.broadcasted_iota(jnp.int32, (nb, nb), 1)
    padded_row = jnp.sum(jnp.where(r32 == c32, padded, 0.0), axis=0, keepdims=True)
    offs = jnp.sum(jnp.where(c32 < r32, padded_row, 0.0), axis=1, keepdims=True)
    ends = offs + padded

    def place(blk, seen):
        oh = onehot(blk)
        rank = _dot(oh.astype(BF16), before)
        pos = jnp.sum(oh * (rank + seen + offs), axis=0, keepdims=True)
        pos_ref[blk] = pos.astype(jnp.int32)
        return seen + jnp.sum(oh, axis=1, keepdims=True)

    lax.fori_loop(0, N_TILES, place, zeros)

    start = lax.broadcasted_iota(jnp.int32, (nb, 128), 1).astype(F32) * tm
    bsub = lax.broadcasted_iota(jnp.int32, (nb, 128), 0)
    done = jnp.where((bsub < N_BUCKETS) & (ends <= start), 1.0, 0.0)
    tb = jnp.sum(done, axis=0, keepdims=True)
    valid = jnp.where(tb < N_BUCKETS, 1.0, 0.0)
    tbc = jnp.minimum(tb, N_BUCKETS - 1.0)
    grp = jnp.floor((tbc + 0.5) * (1.0 / PAIRS_PER_GROUP))
    pair = tbc - PAIRS_PER_GROUP * grp
    loc_a = jnp.zeros_like(pair)
    loc_b = jnp.zeros_like(pair)
    for k, (sa, sb) in enumerate(PAIR_SLOTS):
        loc_a = jnp.where(pair == k, float(sa), loc_a)
        loc_b = jnp.where(pair == k, float(sb), loc_b)
    mine = bsub.astype(F32) == tbc
    used = jnp.sum(jnp.where(mine, offs + cnt, 0.0), axis=0, keepdims=True)
    n_rows = jnp.clip(used - start[0:1], 0.0, tm) * valid
    row8 = lax.broadcasted_iota(jnp.int32, (8, 128), 0)
    meta = jnp.where(row8 == 0, grp * EXPERTS_PER_GROUP + loc_a,
                     jnp.where(row8 == 1, grp * EXPERTS_PER_GROUP + loc_b,
                               jnp.where(row8 == 2, valid, jnp.where(row8 == 3, n_rows, 0.0))))
    meta_ref[...] = meta.astype(jnp.int32)


def _route(bid):
    return pl.pallas_call(
        _route_kernel,
        out_shape=(jax.ShapeDtypeStruct((N_TILES, 1, TILE), jnp.int32),
                   jax.ShapeDtypeStruct((8, 128), jnp.int32)),
        compiler_params=pltpu.CompilerParams(vmem_limit_bytes=VMEM_LIMIT),
        name="moe_route",
    )(bid)


def _moe_kernel(meta_ref, pos_ref, h2_hbm, wga_ref, wua_ref, wda_ref, wgb_ref, wub_ref, wdb_ref,
                y_ref, src_ref, xbuf, sem, wga_s, wua_s, wda_s, wgb_s, wub_s, wdb_s):
    j = pl.program_id(0)

    def row_copy(tile, r, slot):
        tok = src_ref[tile * ROW_TILE + r]
        return pltpu.make_async_copy(h2_hbm.at[pl.ds(tok, 1), :], xbuf.at[slot, pl.ds(r, 1), :], sem.at[slot])

    group = 8

    def row_groups(tile):
        return (meta_ref[3, tile] + (group - 1)) // group

    def issue(tile, slot):
        def body(g, c):
            for k in range(group):
                row_copy(tile, g * group + k, slot).start()
            return c
        lax.fori_loop(0, row_groups(tile), body, 0)

    def wait(tile, slot):
        def body(g, c):
            for k in range(group):
                row_copy(tile, g * group + k, slot).wait()
            return c
        lax.fori_loop(0, row_groups(tile), body, 0)

    @pl.when(j == 0)
    def _():
        xbuf[...] = jnp.zeros_like(xbuf)

        def clear(t, c):
            n = meta_ref[3, t]
            for k in range(group - 1):
                src_ref[t * ROW_TILE + jnp.minimum(n + k, ROW_TILE - 1)] = 0
            return c
        lax.fori_loop(0, N_ROW_TILES, clear, 0)

        def invert(t, c):
            src_ref[pos_ref[t]] = t
            return c
        lax.fori_loop(0, T_ALL, invert, 0, unroll=8)

        @pl.when(meta_ref[2, 0] == 1)
        def _():
            issue(0, 0)

    nxt = jnp.minimum(j + 1, N_ROW_TILES - 1)

    @pl.when((j + 1 < N_ROW_TILES) & (meta_ref[2, nxt] == 1))
    def _():
        issue(nxt, nxt % 2)

    valid = meta_ref[2, j] == 1
    prev = jnp.maximum(j - 1, 0)

    @pl.when(valid & ((j == 0) | (meta_ref[0, j] != meta_ref[0, prev])))
    def _():
        wga_s[...] = wga_ref[0].astype(BF16)
        wua_s[...] = wua_ref[0].astype(BF16)
        wda_s[...] = wda_ref[0].astype(BF16)

    @pl.when(valid & ((j == 0) | (meta_ref[1, j] != meta_ref[1, prev])))
    def _():
        wgb_s[...] = wgb_ref[0].astype(BF16)
        wub_s[...] = wub_ref[0].astype(BF16)
        wdb_s[...] = wdb_ref[0].astype(BF16)

    @pl.when(valid)
    def _():
        slot = j % 2
        wait(j, slot)
        x = xbuf[slot, :, 0:D_MODEL].astype(BF16)
        gates = xbuf[slot, :, D_MODEL:H2_EXT]

        hg_a = _dot(x, wga_s[...])
        hu_a = _dot(x, wua_s[...])
        hg_b = _dot(x, wgb_s[...])
        hu_b = _dot(x, wub_s[...])
        act_a = (hg_a * jax.nn.sigmoid(hg_a) * hu_a * gates[:, 0:1]).astype(BF16)
        act_b = (hg_b * jax.nn.sigmoid(hg_b) * hu_b * gates[:, 1:2]).astype(BF16)
        y_ref[...] = _dot(act_a, wda_s[...]) + _dot(act_b, wdb_s[...])

    @pl.when(jnp.logical_not(valid))
    def _():
        y_ref[...] = jnp.zeros_like(y_ref)


def _moe(meta, pos, h2ext, w_gate, w_up, w_down):
    up_spec = lambda slot: pl.BlockSpec((1, D_MODEL, EXPERT_FF), lambda j, meta, pos: (meta[slot, j], 0, 0))
    down_spec = lambda slot: pl.BlockSpec((1, EXPERT_FF, D_MODEL), lambda j, meta, pos: (meta[slot, j], 0, 0))
    grid_spec = pltpu.PrefetchScalarGridSpec(
        num_scalar_prefetch=2,
        grid=(N_ROW_TILES,),
        in_specs=[pl.BlockSpec(memory_space=pl.ANY),
                  up_spec(0), up_spec(0), down_spec(0), up_spec(1), up_spec(1), down_spec(1)],
        out_specs=pl.BlockSpec((ROW_TILE, D_MODEL), lambda j, meta, pos: (j, 0)),
        scratch_shapes=[pltpu.SMEM((ROW_CAP,), jnp.int32),
                        pltpu.VMEM((2, ROW_TILE, H2_EXT), F32),
                        pltpu.SemaphoreType.DMA((2,)),
                        pltpu.VMEM((D_MODEL, EXPERT_FF), BF16), pltpu.VMEM((D_MODEL, EXPERT_FF), BF16),
                        pltpu.VMEM((EXPERT_FF, D_MODEL), BF16),
                        pltpu.VMEM((D_MODEL, EXPERT_FF), BF16), pltpu.VMEM((D_MODEL, EXPERT_FF), BF16),
                        pltpu.VMEM((EXPERT_FF, D_MODEL), BF16)])
    return pl.pallas_call(
        _moe_kernel,
        out_shape=jax.ShapeDtypeStruct((ROW_CAP, D_MODEL), F32),
        grid_spec=grid_spec,
        compiler_params=_cparams(("arbitrary",)),
        name="moe_experts",
    )(meta, pos, h2ext, w_gate, w_up, w_down, w_gate, w_up, w_down)


def _final_kernel(pos_ref, y_hbm, x1_ref, m_ref, gn_ref, op_ref, os_ref, ybuf, sem):
    i = pl.program_id(0)

    def row_copy(tile, r, slot):
        p = pos_ref[tile * TILE + r]
        return pltpu.make_async_copy(y_hbm.at[pl.ds(p, 1), :], ybuf.at[slot, pl.ds(r, 1), :], sem.at[slot])

    def issue(tile, slot):
        def body(r2, c):
            row_copy(tile, 2 * r2, slot).start(priority=0)
            row_copy(tile, 2 * r2 + 1, slot).start(priority=1)
            return c
        lax.fori_loop(0, TILE // 2, body, 0, unroll=4)

    def wait(slot):
        pltpu.make_async_copy(y_hbm.at[pl.ds(0, TILE)], ybuf.at[slot], sem.at[slot]).wait()

    @pl.when(i == 0)
    def _():
        issue(0, 0)

    @pl.when(i + 1 < N_TILES)
    def _():
        issue(i + 1, (i + 1) % 2)

    slot = i % 2
    wait(slot)
    out = x1_ref[...] + m_ref[0, 5:6, :] * _rms(ybuf[slot], gn_ref[3:4, :])

    @pl.when(i < N_TILES_P)
    def _():
        op_ref[...] = out

    @pl.when(i >= N_TILES_P)
    def _():
        os_ref[...] = out


def _final(pos, y_sorted, x1, mods3, g_norm):
    tps = DEC_SEQ // TILE
    grid_spec = pltpu.PrefetchScalarGridSpec(
        num_scalar_prefetch=1,
        grid=(N_TILES,),
        in_specs=[pl.BlockSpec(memory_space=pl.ANY),
                  pl.BlockSpec((TILE, D_MODEL), lambda i, pos: (i, 0)),
                  pl.BlockSpec((1, N_MOD, D_MODEL), lambda i, pos: (_mod_row_of_tile(i, tps, N_TILES_P), 0, 0)),
                  pl.BlockSpec((4, D_MODEL), lambda i, pos: (0, 0))],
        out_specs=(pl.BlockSpec((TILE, D_MODEL), lambda i, pos: (jnp.minimum(i, N_TILES_P - 1), 0)),
                   pl.BlockSpec((TILE, D_MODEL), lambda i, pos: (jnp.maximum(i - N_TILES_P, 0), 0))),
        scratch_shapes=[pltpu.VMEM((2, TILE, D_MODEL), F32), pltpu.SemaphoreType.DMA((2,))])
    return pl.pallas_call(
        _final_kernel,
        out_shape=(jax.ShapeDtypeStruct((T_PROMPT, D_MODEL), F32),
                   jax.ShapeDtypeStruct((T_SAMPLE, D_MODEL), F32)),
        grid_spec=grid_spec,
        compiler_params=_cparams(("arbitrary",)),
        name="moe_combine_final",
    )(pos, y_sorted, x1, mods3, g_norm)


def kernel(x_prompt, x_sample, state_C, state_n, state_m, c, c_ctx, w_ada, b_ada, g_norm, w_in, ml_gate_bias, ml_head_gain, hy_conv_w, hy_f_w1, hy_f_b1, hy_f_w2, hy_f_b2, hy_f_w3, hy_f_b3, hy_decay, hy_bias, w_out, w_rc, b_rc, w_rf, b_rf, w_gate, w_up, w_down):
    xp = x_prompt.reshape(T_PROMPT, D_MODEL)
    xs = x_sample.reshape(T_SAMPLE, D_MODEL)
    gn = g_norm[0]

    cv = jnp.concatenate([c_ctx[None, :], c, jnp.zeros((MOD_ROWS - 1 - DEC_BATCH, D_MODEL), F32)], axis=0)
    mods3 = _ada(cv, w_ada[0], b_ada[0]).reshape(MOD_ROWS, N_MOD, D_MODEL)

    w_in0 = w_in[0]
    w_qkvo = w_in0[:, :ML_QKVO_COLS].astype(BF16)
    w_hy = w_in0[:, ML_QKVO_COLS + ML_GATE_COLS:].astype(BF16)
    wg = w_in0[:, ML_QKVO_COLS:ML_QKVO_COLS + ML_GATE_COLS]
    gbt = ml_gate_bias[0].reshape(ML_GATE_COLS, 1)
    proj, gates, gates_t = _inproj(xp, xs, mods3, gn, w_qkvo, w_hy, wg.T, gbt)

    gain = ml_head_gain[0].reshape(1, ML_WIDTH)
    y_ml_p, c_new, n_new, m_new = _mlstm(proj, gates, gates_t, gain, None, SEQ, BATCH, 0)
    state = (state_C[:, 0], state_n[:, 0], state_m[:, 0].reshape(DEC_BATCH, 2 * ML_HEADS, 1))
    y_ml_s, _, _, _ = _mlstm(proj, gates, gates_t, gain, state, DEC_SEQ, DEC_BATCH, T_PROMPT // DEC_SEQ)

    w1p = jnp.pad(hy_f_w1[0], ((0, 128 - HY_EMB), (0, 0)))
    b1 = hy_f_b1[0].reshape(1, -1)
    b2 = hy_f_b2[0].reshape(1, -1)
    b3 = hy_f_b3[0].reshape(1, -1)
    dec = hy_decay[0].reshape(1, -1)
    z_parts = []
    for seq_len, n_seq, row_off, width, seqs in ((SEQ, BATCH, 0, SEQ, 4), (DEC_SEQ, DEC_BATCH, T_PROMPT, GRID_W, 2)):
        f, ft = _dft_mats(seq_len)
        coefs = _hyena_filters(seq_len, f, w1p, b1, hy_f_w2[0], b2, hy_f_w3[0], b3, dec)
        z_parts.append(_hyena(proj, hy_conv_w[0], coefs, hy_bias[0], f, ft, seq_len, n_seq, row_off, width, seqs))
    z_p, z_s = z_parts

    pad_r = ROUTER_ROWS - N_GROUPS - N_EXPERTS
    w_r = jnp.pad(jnp.concatenate([w_rc[0], w_rf[0]], axis=1).T, ((0, pad_r), (0, 0)))
    b_r = jnp.pad(jnp.concatenate([b_rc[0], b_rf[0]], axis=0), (0, pad_r)).reshape(ROUTER_ROWS, 1)
    x1, h2ext, bid = _outproj(xp, xs, y_ml_p, y_ml_s, z_p, z_s, mods3, gn, w_out[0].astype(BF16), w_r, b_r)

    pos3, meta = _route(bid)
    pos = pos3.reshape(T_ALL)
    y_sorted = _moe(meta, pos, h2ext, w_gate[0], w_up[0], w_down[0])
    y_p, y_s = _final(pos, y_sorted, x1, mods3, gn)

    new_c = c_new.reshape(BATCH, 1, 2, ML_HEADS, ML_HEAD_DIM, ML_HEAD_DIM)
    new_n = n_new.reshape(BATCH, 1, 2, ML_HEADS, ML_HEAD_DIM)
    new_m = m_new[:, :, 0].reshape(BATCH, 1, 2, ML_HEADS)
    return (y_p.reshape(BATCH, SEQ, D_MODEL), y_s.reshape(DEC_BATCH, DEC_SEQ, D_MODEL), new_c, new_n, new_m)
```

```python
import functools
import math

import jax
import jax.numpy as jnp
import numpy as np
from jax import lax
from jax.experimental import pallas as pl
from jax.experimental.pallas import tpu as pltpu

F32 = jnp.float32
BF16 = jnp.bfloat16

D_MODEL = 1024
BATCH = 16
SEQ = 256
DEC_BATCH = 4
DEC_SEQ = 1024
GRID_W = 64
ML_WIDTH = 512
ML_HEADS = 4
ML_HEAD_DIM = 128
HY_WIDTH = 512
HY_ORDER = 2
HY_EMB = 33
HY_BANDS = 16
HY_FILTER_HIDDEN = 64
HY_MOD_SHIFT = 0.05
N_GROUPS = 4
EXPERTS_PER_GROUP = 4
N_EXPERTS = 16
EXPERT_FF = 512
N_MOD = 6
EPS = 1e-6
ML_QKVO_COLS = 4 * ML_WIDTH
ML_GATE_COLS = 4 * ML_HEADS
HY_COLS = 3 * HY_WIDTH
MAIN_COLS = ML_QKVO_COLS + HY_COLS

T_PROMPT = BATCH * SEQ
T_SAMPLE = DEC_BATCH * DEC_SEQ
T_ALL = T_PROMPT + T_SAMPLE
TILE = 256
N_TILES_P = T_PROMPT // TILE
N_TILES = T_ALL // TILE
MOD_ROWS = 8
K_SCALE = ML_HEAD_DIM ** -0.5
VMEM_LIMIT = 56 * 1024 * 1024


def _cparams(sem):
    return pltpu.CompilerParams(dimension_semantics=sem, vmem_limit_bytes=VMEM_LIMIT)


def _split2(x):
    hi = x.astype(BF16)
    lo = (x - hi.astype(F32)).astype(BF16)
    return hi, lo


def _dot(a, b):
    return jnp.dot(a, b, preferred_element_type=F32)


def _dot_nt(a, b):
    return lax.dot_general(a, b, (((1,), (1,)), ((), ())), preferred_element_type=F32)


def _dot_tn(a, b):
    return lax.dot_general(a, b, (((0,), (0,)), ((), ())), preferred_element_type=F32)


def _dot3(a, b):
    ah, al = _split2(a)
    bh, bl = _split2(b)
    return _dot(ah, bh) + _dot(al, bh) + _dot(ah, bl)


def _dot3_nt(a, b):
    ah, al = _split2(a)
    bh, bl = _split2(b)
    return _dot_nt(ah, bh) + _dot_nt(al, bh) + _dot_nt(ah, bl)


def _dot_exact_lhs(t, x):
    x1 = x.astype(BF16)
    r1 = x - x1.astype(F32)
    x2 = r1.astype(BF16)
    x3 = (r1 - x2.astype(F32)).astype(BF16)
    return _dot(t, x1) + _dot(t, x2) + _dot(t, x3)


def _dot_exact_rhs(x, t):
    x1 = x.astype(BF16)
    r1 = x - x1.astype(F32)
    x2 = r1.astype(BF16)
    x3 = (r1 - x2.astype(F32)).astype(BF16)
    return _dot(x1, t) + _dot(x2, t) + _dot(x3, t)


def _rms(x, g):
    return x * lax.rsqrt(jnp.mean(x * x, axis=-1, keepdims=True) + EPS) * g


def _mod_row_of_tile(i, tiles_per_sample_seq, n_prompt_tiles):
    return jnp.where(i < n_prompt_tiles, 0, 1 + (i - n_prompt_tiles) // tiles_per_sample_seq)


def _ada_kernel(cv_ref, w_ref, b_ref, o_ref):
    cv = cv_ref[...]
    s = cv * jax.nn.sigmoid(cv)
    o_ref[...] = _dot3(s, w_ref[...]) + b_ref[...]


def _ada(cv, w_ada, b_ada):
    n = N_MOD * D_MODEL
    return pl.pallas_call(
        _ada_kernel,
        out_shape=jax.ShapeDtypeStruct((MOD_ROWS, n), F32),
        grid=(N_MOD,),
        in_specs=[pl.BlockSpec((MOD_ROWS, D_MODEL), lambda j: (0, 0)),
                  pl.BlockSpec((D_MODEL, D_MODEL), lambda j: (0, j)),
                  pl.BlockSpec((1, D_MODEL), lambda j: (0, j))],
        out_specs=pl.BlockSpec((MOD_ROWS, D_MODEL), lambda j: (0, j)),
        compiler_params=_cparams(("arbitrary",)),
        name="ada_mod",
    )(cv, w_ada, b_ada.reshape(1, n))


def _log_sigmoid(x):
    return jnp.minimum(x, 0.0) - jnp.log1p(jnp.exp(-jnp.abs(x)))


def _rows_to_cols(rows):
    ri = lax.broadcasted_iota(jnp.int32, (TILE, TILE), 0)
    ci = lax.broadcasted_iota(jnp.int32, (TILE, TILE), 1)
    eye = jnp.where(ri == ci, 1.0, 0.0).astype(BF16)
    p1 = rows.astype(BF16)
    r1 = rows - p1.astype(F32)
    p2 = r1.astype(BF16)
    p3 = (r1 - p2.astype(F32)).astype(BF16)
    return _dot_nt(eye, p1) + _dot_nt(eye, p2) + _dot_nt(eye, p3)


BIG_TILE = 2 * TILE
N_BIG_P = T_PROMPT // BIG_TILE
N_BIG = T_ALL // BIG_TILE


def _inproj_kernel(xp_ref, xs_ref, m_ref, gn_ref, wq_ref, wh_ref, wgt_ref, gbt_ref, proj_ref, gate_ref, gatet_ref):
    is_p = pl.program_id(0) < N_BIG_P
    halves = [slice(r * TILE, (r + 1) * TILE) for r in range(BIG_TILE // TILE)]
    hs = [_rms(jnp.where(is_p, xp_ref[rows, :], xs_ref[rows, :]), gn_ref[0:1, :]) * (1.0 + m_ref[0, 1:2, :])
          + m_ref[0, 0:1, :] for rows in halves]
    hbs = [h.astype(BF16) for h in hs]
    cb = 512
    for j in range(ML_QKVO_COLS // cb):
        for rows, hb in zip(halves, hbs):
            proj_ref[rows, j * cb:(j + 1) * cb] = _dot(hb, wq_ref[:, j * cb:(j + 1) * cb]).astype(BF16)
    for j in range(HY_COLS // cb):
        lo = ML_QKVO_COLS + j * cb
        for rows, hb in zip(halves, hbs):
            proj_ref[rows, lo:lo + cb] = _dot(hb, wh_ref[:, j * cb:(j + 1) * cb]).astype(BF16)
    wth, wtl = _split2(wgt_ref[...])
    gts = []
    for h, hb in zip(hs, hbs):
        hl = (h - hb.astype(F32)).astype(BF16)
        gt = _dot_nt(wth, hb) + _dot_nt(wth, hl) + _dot_nt(wtl, hb) + gbt_ref[...]
        row = lax.broadcasted_iota(jnp.int32, gt.shape, 0)
        gts.append(jnp.where((row % 8) >= 4, _log_sigmoid(gt), gt))
    for r, (rows, gt) in enumerate(zip(halves, gts)):
        gatet_ref[r] = gt
        gate_ref[rows, :] = _rows_to_cols(gt)


def _inproj(xp, xs, mods3, g_norm, w_qkvo, w_hy, wgt, gbt):
    tps = DEC_SEQ // BIG_TILE
    per = BIG_TILE // TILE
    return pl.pallas_call(
        _inproj_kernel,
        out_shape=(jax.ShapeDtypeStruct((T_ALL, MAIN_COLS), BF16),
                   jax.ShapeDtypeStruct((T_ALL, ML_GATE_COLS), F32),
                   jax.ShapeDtypeStruct((N_TILES, ML_GATE_COLS, TILE), F32)),
        grid=(N_BIG,),
        in_specs=[pl.BlockSpec((BIG_TILE, D_MODEL), lambda i: (jnp.minimum(i, N_BIG_P - 1), 0)),
                  pl.BlockSpec((BIG_TILE, D_MODEL), lambda i: (jnp.maximum(i - N_BIG_P, 0), 0)),
                  pl.BlockSpec((1, N_MOD, D_MODEL), lambda i: (_mod_row_of_tile(i, tps, N_BIG_P), 0, 0)),
                  pl.BlockSpec((4, D_MODEL), lambda i: (0, 0)),
                  pl.BlockSpec((D_MODEL, ML_QKVO_COLS), lambda i: (0, 0)),
                  pl.BlockSpec((D_MODEL, HY_COLS), lambda i: (0, 0)),
                  pl.BlockSpec((ML_GATE_COLS, D_MODEL), lambda i: (0, 0)),
                  pl.BlockSpec((ML_GATE_COLS, 1), lambda i: (0, 0))],
        out_specs=(pl.BlockSpec((BIG_TILE, MAIN_COLS), lambda i: (i, 0)),
                   pl.BlockSpec((BIG_TILE, ML_GATE_COLS), lambda i: (i, 0)),
                   pl.BlockSpec((per, ML_GATE_COLS, TILE), lambda i: (i, 0, 0))),
        compiler_params=_cparams(("arbitrary",)),
        name="in_proj",
    )(xp, xs, mods3, g_norm, w_qkvo, w_hy, wgt, gbt)


ST_ROWS = ML_HEAD_DIM + 16


def _mlstm_kernel(*refs, seq_len, has_state):
    if has_state:
        (q_ref, k_ref, v_ref, o_ref, g_ref, gt_ref, gain_ref, c0_ref, n0_ref, m0_ref,
         y_ref, c_ref, n_ref, m_ref, vt_ref, hf_ref, hb_ref, st_ref, ms_ref) = refs
    else:
        (q_ref, k_ref, v_ref, o_ref, g_ref, gt_ref, gain_ref,
         y_ref, c_ref, n_ref, m_ref, vt_ref, hf_ref, hb_ref, st_ref, ms_ref) = refs
    ch = TILE
    nc = seq_len // ch
    hd = ML_HEAD_DIM
    key = lax.broadcasted_iota(jnp.int32, (ch, ch), 0)
    qry = lax.broadcasted_iota(jnp.int32, (ch, ch), 1)
    key_le = key <= qry
    key_ge = key >= qry
    t_le = jnp.where(key_le, 1.0, 0.0).astype(BF16)
    t_ge = jnp.where(key_ge, 1.0, 0.0).astype(BF16)
    sub16 = lax.broadcasted_iota(jnp.int32, (16, ch), 0)
    ln_scale = math.log(K_SCALE)

    for c in range(nc):
        for h in range(ML_HEADS):
            cols = slice(h * hd, (h + 1) * hd)
            vt_ref[c, cols, :] = v_ref[c * ch:(c + 1) * ch, cols].astype(F32).T.astype(BF16)

    for d in range(2):
        for h in range(ML_HEADS):
            r = d * ML_HEADS + h
            st_ref[r] = jnp.zeros((ST_ROWS, hd), F32)
            if has_state:
                st_ref[r, 0:hd, :] = c0_ref[0, d, h].T
                st_ref[r, hd:hd + 1, :] = n0_ref[0, d, h:h + 1, :]
                ms_ref[r] = jnp.broadcast_to(m0_ref[0, r:r + 1, :], (1, ch))
            else:
                ms_ref[r] = jnp.zeros((1, ch), F32)

    def step(t, carry):
        for d in range(2):
            c = t if d == 0 else nc - 1 - t
            rows = pl.ds(pl.multiple_of(c * ch, ch), ch)
            gcol = g_ref[rows, :]
            grow = gt_ref[c]
            brow_all = _dot_exact_rhs(grow, t_le if d == 0 else t_ge)
            bcol_all = _dot_exact_lhs(t_ge if d == 0 else t_le, gcol)
            mask = key_le if d == 0 else key_ge
            hacc_ref = hf_ref if d == 0 else hb_ref
            heads = range(ML_HEADS)
            regs = [d * ML_HEADS + h for h in heads]
            colss = [slice(h * hd, (h + 1) * hd) for h in heads]
            qs = [q_ref[rows, cols] for cols in colss]
            ks = [k_ref[rows, cols] for cols in colss]
            vts = [vt_ref[c, cols, :] for cols in colss]
            sts = [st_ref[r] for r in regs]
            m_prevs = [ms_ref[r] for r in regs]
            b_rows = [brow_all[(1 + 2 * d) * ML_HEADS + h:(1 + 2 * d) * ML_HEADS + h + 1, :] for h in heads]
            ig_rows = [grow[2 * d * ML_HEADS + h:2 * d * ML_HEADS + h + 1, :] for h in heads]
            qks = [_dot_nt(k, q) for k, q in zip(ks, qs)]
            iqs = [_dot_nt(st.astype(BF16), q) for st, q in zip(sts, qs)]
            ss, sc_inters, m_poss = [], [], []
            for h in heads:
                fcol = (1 + 2 * d) * ML_HEADS + h
                icol = 2 * d * ML_HEADS + h
                c_col = gcol[:, icol:icol + 1] - bcol_all[:, fcol:fcol + 1]
                logd = jnp.where(mask, b_rows[h] + c_col, -jnp.inf)
                inter = b_rows[h] + m_prevs[h]
                m_pos = jnp.maximum(inter, jnp.max(logd, axis=0, keepdims=True))
                ss.append(qks[h] * jnp.exp(logd - (m_pos - ln_scale)))
                sc_inters.append(jnp.exp(inter - m_pos))
                m_poss.append(m_pos)
            pvs = [_dot(vt, s.astype(BF16)) for vt, s in zip(vts, ss)]
            for h in heads:
                num = sc_inters[h] * iqs[h][0:hd] + pvs[h]
                den = sc_inters[h] * iqs[h][hd:hd + 1] + jnp.sum(ss[h], axis=0, keepdims=True)
                hacc_ref[c, colss[h], :] = num * (1.0 / jnp.maximum(jnp.abs(den), jnp.exp(-m_poss[h])))
            lhss, decays = [], []
            for h in heads:
                b_row = b_rows[h]
                b_last = b_row[:, ch - 1:ch] if d == 0 else b_row[:, 0:1]
                logw = b_last - b_row + ig_rows[h]
                m_new = jnp.maximum(b_last + m_prevs[h], jnp.max(logw, axis=1, keepdims=True))
                w = jnp.exp(logw - (m_new - ln_scale))
                decays.append(jnp.exp(b_last + m_prevs[h] - m_new))
                lhss.append(jnp.concatenate([(vts[h].astype(F32) * w).astype(BF16),
                                             jnp.where(sub16 == 0, w, 0.0).astype(BF16)], axis=0))
                ms_ref[regs[h]] = m_new
            upds = [_dot(lhs, k) for lhs, k in zip(lhss, ks)]
            for h in heads:
                st_ref[regs[h]] = decays[h][:, 0:hd] * sts[h] + upds[h]
        return carry

    lax.fori_loop(0, nc, step, 0)

    for d in range(2):
        for h in range(ML_HEADS):
            r = d * ML_HEADS + h
            c_ref[0, d, h] = st_ref[r, 0:hd, :].T
            n_ref[0, d, h:h + 1, :] = st_ref[r, hd:hd + 1, :]
            m_ref[0, r:r + 1, :] = ms_ref[r][:, 0:hd]
    for c in range(nc):
        for h in range(ML_HEADS):
            cols = slice(h * hd, (h + 1) * hd)
            ht = hf_ref[c, cols, :] + hb_ref[c, cols, :]
            ht = ht * lax.rsqrt(jnp.mean(ht * ht, axis=0, keepdims=True) + EPS)
            rows = slice(c * ch, (c + 1) * ch)
            y = ht.T * gain_ref[:, cols] * jax.nn.sigmoid(o_ref[rows, cols].astype(F32))
            y_ref[rows, cols] = y.astype(BF16)


def _mlstm(proj, gates, gates_t, gain, state, seq_len, n_seq, row_block_off):
    has_state = state is not None
    tiles = seq_len // TILE
    off = row_block_off
    qkvo_specs = [pl.BlockSpec((seq_len, ML_WIDTH), functools.partial(lambda b, j: (off + b, j), j=j))
                  for j in range(4)]
    in_specs = qkvo_specs + [
        pl.BlockSpec((seq_len, ML_GATE_COLS), lambda b: (off + b, 0)),
        pl.BlockSpec((tiles, ML_GATE_COLS, TILE), lambda b: (off + b, 0, 0)),
        pl.BlockSpec((1, ML_WIDTH), lambda b: (0, 0)),
    ]
    args = [proj, proj, proj, proj, gates, gates_t, gain]
    if has_state:
        c0, n0, m0 = state
        in_specs += [
            pl.BlockSpec((1, 2, ML_HEADS, ML_HEAD_DIM, ML_HEAD_DIM), lambda b: (b, 0, 0, 0, 0)),
            pl.BlockSpec((1, 2, ML_HEADS, ML_HEAD_DIM), lambda b: (b, 0, 0, 0)),
            pl.BlockSpec((1, 2 * ML_HEADS, 1), lambda b: (b, 0, 0)),
        ]
        args += [c0, n0, m0]
    out_shape = (jax.ShapeDtypeStruct((n_seq * seq_len, ML_WIDTH), BF16),
                 jax.ShapeDtypeStruct((n_seq, 2, ML_HEADS, ML_HEAD_DIM, ML_HEAD_DIM), F32),
                 jax.ShapeDtypeStruct((n_seq, 2, ML_HEADS, ML_HEAD_DIM), F32),
                 jax.ShapeDtypeStruct((n_seq, 2 * ML_HEADS, ML_HEAD_DIM), F32))
    out_specs = (pl.BlockSpec((seq_len, ML_WIDTH), lambda b: (b, 0)),
                 pl.BlockSpec((1, 2, ML_HEADS, ML_HEAD_DIM, ML_HEAD_DIM), lambda b: (b, 0, 0, 0, 0)),
                 pl.BlockSpec((1, 2, ML_HEADS, ML_HEAD_DIM), lambda b: (b, 0, 0, 0)),
                 pl.BlockSpec((1, 2 * ML_HEADS, ML_HEAD_DIM), lambda b: (b, 0, 0)))
    scratch = [pltpu.VMEM((tiles, ML_WIDTH, TILE), BF16),
               pltpu.VMEM((tiles, ML_WIDTH, TILE), F32), pltpu.VMEM((tiles, ML_WIDTH, TILE), F32),
               pltpu.VMEM((2 * ML_HEADS, ST_ROWS, ML_HEAD_DIM), F32),
               pltpu.VMEM((2 * ML_HEADS, 1, TILE), F32)]
    return pl.pallas_call(
        functools.partial(_mlstm_kernel, seq_len=seq_len, has_state=has_state),
        out_shape=out_shape, grid=(n_seq,), in_specs=in_specs, out_specs=out_specs,
        scratch_shapes=scratch, compiler_params=_cparams(("arbitrary",)),
        name=f"mlstm_{seq_len}",
    )(*args)


def _dft_mats(seq_len):
    k = np.arange(seq_len, dtype=np.int64)[:, None]
    d = np.arange(seq_len, dtype=np.int64)[None, :]
    ang = np.pi * ((k * d) % (2 * seq_len)).astype(np.float64) / seq_len
    sinm = np.sin(ang)
    sinm[0, :] = np.where(d[0] % 2 == 0, 1.0, -1.0)
    f = np.concatenate([np.cos(ang), sinm], axis=0).astype(np.float32)
    return jnp.asarray(f).astype(BF16), jnp.asarray(np.ascontiguousarray(f.T)).astype(BF16)


def _filter_feats(seq_len):
    t = np.linspace(0.0, 1.0, seq_len, dtype=np.float64)[:, None]
    wpos = 2.0 * np.pi * np.arange(seq_len, dtype=np.float64)[:, None] / seq_len
    bands = np.linspace(1e-4, HY_BANDS - 1, HY_BANDS, dtype=np.float64)[None, :]
    z = np.concatenate([t, np.cos(bands * wpos), -np.sin(bands * wpos)], axis=-1)
    return jnp.asarray(np.pad(z, ((0, 0), (0, 128 - HY_EMB))).astype(np.float32))


def _filter_kernel(z_ref, w1_ref, b1_ref, w2_ref, b2_ref, w3_ref, b3_ref, dec_ref, f_ref,
                   a_ref, b_ref, d_ref, *, seq_len):
    n = 2 * seq_len
    z = z_ref[...]
    h = jnp.sin(_dot3(z, w1_ref[...]) + b1_ref[...])
    h = jnp.sin(_dot3(h, w2_ref[...]) + b2_ref[...])
    h = _dot3(h, w3_ref[...]) + b3_ref[...]
    t = z[:, 0:1]
    h = h * (jnp.exp(-t * jnp.abs(dec_ref[...])) + HY_MOD_SHIFT)
    ss = jnp.sum(h * h, axis=0, keepdims=True)
    inv = lax.rsqrt(ss[:, :HY_WIDTH] + ss[:, HY_WIDTH:] + EPS)
    hp = h[:, :HY_WIDTH] * inv
    hn = h[:, HY_WIDTH:] * inv
    ssum = hp + hn
    sdif = hp - hn
    hc = _dot(f_ref[0:seq_len, :], ssum.astype(BF16))
    hs = _dot(f_ref[seq_len:n, :], sdif.astype(BF16))
    di = lax.broadcasted_iota(jnp.int32, (seq_len, 1), 0)
    sgn = jnp.where(di % 2 == 0, 1.0, -1.0)
    nyq = jnp.sum(ssum * sgn, axis=0, keepdims=True)
    first = di == 0
    a_ref[0] = hc * jnp.where(first, 1.0 / n, 2.0 / n)
    b_ref[0] = jnp.where(first, 0.0, hs * (2.0 / n))
    d_ref[0] = jnp.where(first, nyq * (1.0 / n), hc * (2.0 / n))


def _hyena_filters(seq_len, f, w1p, b1, w2, b2, w3, b3, dec):
    z = _filter_feats(seq_len)
    hid = HY_FILTER_HIDDEN
    oc = 2 * HY_WIDTH
    full = lambda shape: pl.BlockSpec(shape, lambda o: tuple(0 for _ in shape))
    out = jax.ShapeDtypeStruct((HY_ORDER, seq_len, HY_WIDTH), F32)
    return pl.pallas_call(
        functools.partial(_filter_kernel, seq_len=seq_len),
        out_shape=(out, out, out),
        grid=(HY_ORDER,),
        in_specs=[full((seq_len, 128)), full((128, hid)), full((1, hid)), full((hid, hid)), full((1, hid)),
                  pl.BlockSpec((hid, oc), lambda o: (0, o)),
                  pl.BlockSpec((1, oc), lambda o: (0, o)),
                  pl.BlockSpec((1, oc), lambda o: (0, o)),
                  full((2 * seq_len, seq_len))],
        out_specs=tuple(pl.BlockSpec((1, seq_len, HY_WIDTH), lambda o: (o, 0, 0)) for _ in range(3)),
        compiler_params=_cparams(("arbitrary",)),
        name=f"hyena_filter_{seq_len}",
    )(z, w1p, b1, w2, b2, w3, b3, dec, f)


def _hyena_kernel(x1_ref, x2_ref, v_ref, cw1_ref, cw2_ref, cwv_ref, a_ref, b_ref, d_ref, bias_ref,
                  f_ref, ft_ref, z_ref, *, seq_len, width, seqs):
    rows = seqs * seq_len
    ti = lax.broadcasted_iota(jnp.int32, (rows, 1), 0)
    has_prev = (ti % width) != 0
    has_next = (ti % width) != (width - 1)

    def short_conv(x_ref, w_ref):
        x = x_ref[...].astype(F32)
        prev = jnp.where(has_prev, pltpu.roll(x, 1, axis=0), 0.0)
        nxt = jnp.where(has_next, pltpu.roll(x, rows - 1, axis=0), 0.0)
        return w_ref[0:1, :] * prev + w_ref[1:2, :] * x + w_ref[2:3, :] * nxt

    gates = (short_conv(x1_ref, cw1_ref), short_conv(x2_ref, cw2_ref))
    v = short_conv(v_ref, cwv_ref)
    sls = [slice(i * seq_len, (i + 1) * seq_len) for i in range(seqs)]
    zs = [v[sl] for sl in sls]
    for o in range(HY_ORDER):
        a, b, dd = a_ref[o], b_ref[o], d_ref[o]
        us = [_dot(f_ref[...], z.astype(BF16)) for z in zs]
        ys = []
        for u in us:
            ut = u[:seq_len]
            ub = u[seq_len:]
            ys.append(((ut * a - ub * b).astype(BF16), (ut * b + ub * dd).astype(BF16)))
        convs = [_dot(ft_ref[:, :seq_len], yt) + _dot(ft_ref[:, seq_len:], yb) for yt, yb in ys]
        zs = [gates[o][sl] * (y + bias_ref[o:o + 1, :] * z) for sl, y, z in zip(sls, convs, zs)]
    for sl, z in zip(sls, zs):
        z_ref[sl, :] = z.astype(BF16)


def _hyena(proj, conv_w, coefs, hy_bias, f, ft, seq_len, n_seq, row_off, width, seqs):
    cb = 256
    nblk = HY_WIDTH // cb
    base = ML_QKVO_COLS // cb
    rows = seqs * seq_len
    off = row_off // rows
    a, b, d = coefs

    def col_spec(part):
        return pl.BlockSpec((rows, cb), lambda j, s: (off + s, base + part * nblk + j))

    def w_spec(part):
        return pl.BlockSpec((3, cb), lambda j, s: (0, part * nblk + j))

    coef_spec = pl.BlockSpec((HY_ORDER, seq_len, cb), lambda j, s: (0, 0, j))
    return pl.pallas_call(
        functools.partial(_hyena_kernel, seq_len=seq_len, width=width, seqs=seqs),
        out_shape=jax.ShapeDtypeStruct((n_seq * seq_len, HY_WIDTH), BF16),
        grid=(nblk, n_seq // seqs),
        in_specs=[col_spec(0), col_spec(1), col_spec(2), w_spec(0), w_spec(1), w_spec(2),
                  coef_spec, coef_spec, coef_spec,
                  pl.BlockSpec((HY_ORDER, cb), lambda j, s: (0, j)),
                  pl.BlockSpec((2 * seq_len, seq_len), lambda j, s: (0, 0)),
                  pl.BlockSpec((seq_len, 2 * seq_len), lambda j, s: (0, 0))],
        out_specs=pl.BlockSpec((rows, cb), lambda j, s: (s, j)),
        compiler_params=_cparams(("arbitrary", "arbitrary")),
        name=f"hyena_conv_{seq_len}",
    )(proj, proj, proj, conv_w, conv_w, conv_w, a, b, d, hy_bias, f, ft)


def _first_max(x, n):
    mx = jnp.max(x, axis=0, keepdims=True)
    row = lax.broadcasted_iota(jnp.int32, x.shape, 0).astype(F32)
    idx = jnp.min(jnp.where(x == mx, row, float(n)), axis=0, keepdims=True)
    return mx, idx.astype(jnp.int32)


ROUTER_ROWS = 32
PAIRS_PER_GROUP = 6
N_BUCKETS = N_GROUPS * PAIRS_PER_GROUP
PAIR_SLOTS = ((0, 1), (0, 2), (0, 3), (1, 3), (1, 2), (3, 2))
LANES = 128
H2_EXT = D_MODEL + LANES
ROW_TILE = 256
ROW_CAP = T_ALL + N_BUCKETS * ROW_TILE
N_ROW_TILES = ROW_CAP // ROW_TILE


def _outproj_kernel(xp_ref, xs_ref, yp_ref, ys_ref, zp_ref, zs_ref, m_ref, gn_ref, wo_ref, wr_ref, br_ref,
                    x1_ref, h2_ref, bid_ref):
    is_p = pl.program_id(0) < N_BIG_P
    wrh, wrl = _split2(wr_ref[...])
    halves = [slice(r * TILE, (r + 1) * TILE) for r in range(BIG_TILE // TILE)]
    ys = [_dot(jnp.where(is_p, yp_ref[rows, :], ys_ref[rows, :]), wo_ref[0:ML_WIDTH, :])
          + _dot(jnp.where(is_p, zp_ref[rows, :], zs_ref[rows, :]), wo_ref[ML_WIDTH:, :]) for rows in halves]
    h2s = []
    for rows, y in zip(halves, ys):
        x = jnp.where(is_p, xp_ref[rows, :], xs_ref[rows, :])
        x1 = x + m_ref[0, 2:3, :] * _rms(y, gn_ref[1:2, :])
        x1_ref[rows, :] = x1
        h2 = _rms(x1, gn_ref[2:3, :]) * (1.0 + m_ref[0, 4:5, :]) + m_ref[0, 3:4, :]
        h2_ref[rows, 0:D_MODEL] = h2
        h2s.append(h2)
    logits = []
    for h2 in h2s:
        h2h, h2l = _split2(h2)
        logits.append(_dot_nt(wrh, h2h) + _dot_nt(wrh, h2l) + _dot_nt(wrl, h2h) + br_ref[...])
    routed = [_route_tile(lg) for lg in logits]
    for r, (rows, (gate_rows, bucket)) in enumerate(zip(halves, routed)):
        h2_ref[rows, D_MODEL:H2_EXT] = jnp.zeros((TILE, LANES), F32)
        h2_ref[rows, D_MODEL:D_MODEL + 8] = _rows_to_cols(gate_rows)
        bid_ref[r] = bucket


def _route_tile(logits):
    lc = logits[0:N_GROUPS]
    mx, gi = _first_max(lc, N_GROUPS)
    p_grp = 1.0 / jnp.sum(jnp.exp(lc - mx), axis=0, keepdims=True)
    lsel = jnp.zeros((EXPERTS_PER_GROUP, TILE), F32)
    for g in range(N_GROUPS):
        lo = N_GROUPS + g * EXPERTS_PER_GROUP
        lsel = jnp.where(gi == g, logits[lo:lo + EXPERTS_PER_GROUP], lsel)
    l1, i1 = _first_max(lsel, EXPERTS_PER_GROUP)
    sub4 = lax.broadcasted_iota(jnp.int32, lsel.shape, 0)
    l2, i2 = _first_max(jnp.where(sub4 == i1, -jnp.inf, lsel), EXPERTS_PER_GROUP)
    e2 = jnp.exp(l2 - l1)
    w1 = p_grp / (1.0 + e2)
    w2 = p_grp * e2 / (1.0 + e2)
    lo_e = jnp.minimum(i1, i2)
    hi_e = jnp.maximum(i1, i2)
    pair = jnp.where(lo_e == 0, hi_e - 1, jnp.where(lo_e == 1, jnp.where(hi_e == 3, 3, 4), 5))
    slot_a = jnp.where(pair == 5, hi_e, lo_e)
    first_in_a = i1 == slot_a
    w_a = jnp.where(first_in_a, w1, w2)
    w_b = jnp.where(first_in_a, w2, w1)
    sub = lax.broadcasted_iota(jnp.int32, (8, TILE), 0)
    gate_rows = jnp.where(sub == 0, w_a, jnp.where(sub == 1, w_b, 0.0))
    return gate_rows, gi * PAIRS_PER_GROUP + pair


def _outproj(xp, xs, yp, ys, zp, zs, mods3, g_norm, w_out, w_r, b_r):
    tps = DEC_SEQ // BIG_TILE
    per = BIG_TILE // TILE
    pidx = lambda i: (jnp.minimum(i, N_BIG_P - 1), 0)
    sidx = lambda i: (jnp.maximum(i - N_BIG_P, 0), 0)
    return pl.pallas_call(
        _outproj_kernel,
        out_shape=(jax.ShapeDtypeStruct((T_ALL, D_MODEL), F32),
                   jax.ShapeDtypeStruct((T_ALL, H2_EXT), F32),
                   jax.ShapeDtypeStruct((N_TILES, 1, TILE), jnp.int32)),
        grid=(N_BIG,),
        in_specs=[pl.BlockSpec((BIG_TILE, D_MODEL), pidx), pl.BlockSpec((BIG_TILE, D_MODEL), sidx),
                  pl.BlockSpec((BIG_TILE, ML_WIDTH), pidx), pl.BlockSpec((BIG_TILE, ML_WIDTH), sidx),
                  pl.BlockSpec((BIG_TILE, HY_WIDTH), pidx), pl.BlockSpec((BIG_TILE, HY_WIDTH), sidx),
                  pl.BlockSpec((1, N_MOD, D_MODEL), lambda i: (_mod_row_of_tile(i, tps, N_BIG_P), 0, 0)),
                  pl.BlockSpec((4, D_MODEL), lambda i: (0, 0)),
                  pl.BlockSpec((D_MODEL, D_MODEL), lambda i: (0, 0)),
                  pl.BlockSpec((ROUTER_ROWS, D_MODEL), lambda i: (0, 0)),
                  pl.BlockSpec((ROUTER_ROWS, 1), lambda i: (0, 0))],
        out_specs=(pl.BlockSpec((BIG_TILE, D_MODEL), lambda i: (i, 0)),
                   pl.BlockSpec((BIG_TILE, H2_EXT), lambda i: (i, 0)),
                   pl.BlockSpec((per, 1, TILE), lambda i: (i, 0, 0))),
        compiler_params=_cparams(("arbitrary",)),
        name="out_proj_router",
    )(xp, xs, yp, ys, zp, zs, mods3, g_norm, w_out, w_r, b_r)


def _route_kernel(bid_ref, pos_ref, meta_ref):
    nb = 32
    tm = float(ROW_TILE)
    sub = lax.broadcasted_iota(jnp.int32, (nb, TILE), 0)
    ri = lax.broadcasted_iota(jnp.int32, (TILE, TILE), 0)
    ci = lax.broadcasted_iota(jnp.int32, (TILE, TILE), 1)
    before = jnp.where(ri < ci, 1.0, 0.0).astype(BF16)

    def onehot(blk):
        return jnp.where(sub == bid_ref[blk], 1.0, 0.0)

    zeros = jnp.zeros((nb, 1), F32)
    cnt = lax.fori_loop(0, N_TILES, lambda blk, c: c + jnp.sum(onehot(blk), axis=1, keepdims=True), zeros)
    padded = jnp.floor((cnt + (tm - 1.0)) * (1.0 / tm)) * tm
    r32 = lax.broadcasted_iota(jnp.int32, (nb, nb), 0)
    c32 = lax.broadcasted_iota(jnp.int32, (nb, nb), 1)
    padded_row = jnp.sum(jnp.where(r32 == c32, padded, 0.0), axis=0, keepdims=True)
    offs = jnp.sum(jnp.where(c32 < r32, padded_row, 0.0), axis=1, keepdims=True)
    ends = offs + padded

    def place(blk, seen):
        oh = onehot(blk)
        rank = _dot(oh.astype(BF16), before)
        pos = jnp.sum(oh * (rank + seen + offs), axis=0, keepdims=True)
        pos_ref[blk] = pos.astype(jnp.int32)
        return seen + jnp.sum(oh, axis=1, keepdims=True)

    lax.fori_loop(0, N_TILES, place, zeros)

    start = lax.broadcasted_iota(jnp.int32, (nb, 128), 1).astype(F32) * tm
    bsub = lax.broadcasted_iota(jnp.int32, (nb, 128), 0)
    done = jnp.where((bsub < N_BUCKETS) & (ends <= start), 1.0, 0.0)
    tb = jnp.sum(done, axis=0, keepdims=True)
    valid = jnp.where(tb < N_BUCKETS, 1.0, 0.0)
    tbc = jnp.minimum(tb, N_BUCKETS - 1.0)
    grp = jnp.floor((tbc + 0.5) * (1.0 / PAIRS_PER_GROUP))
    pair = tbc - PAIRS_PER_GROUP * grp
    loc_a = jnp.zeros_like(pair)
    loc_b = jnp.zeros_like(pair)
    for k, (sa, sb) in enumerate(PAIR_SLOTS):
        loc_a = jnp.where(pair == k, float(sa), loc_a)
        loc_b = jnp.where(pair == k, float(sb), loc_b)
    mine = bsub.astype(F32) == tbc
    used = jnp.sum(jnp.where(mine, offs + cnt, 0.0), axis=0, keepdims=True)
    n_rows = jnp.clip(used - start[0:1], 0.0, tm) * valid
    row8 = lax.broadcasted_iota(jnp.int32, (8, 128), 0)
    meta = jnp.where(row8 == 0, grp * EXPERTS_PER_GROUP + loc_a,
                     jnp.where(row8 == 1, grp * EXPERTS_PER_GROUP + loc_b,
                               jnp.where(row8 == 2, valid, jnp.where(row8 == 3, n_rows, 0.0))))
    meta_ref[...] = meta.astype(jnp.int32)


def _route(bid):
    return pl.pallas_call(
        _route_kernel,
        out_shape=(jax.ShapeDtypeStruct((N_TILES, 1, TILE), jnp.int32),
                   jax.ShapeDtypeStruct((8, 128), jnp.int32)),
        compiler_params=pltpu.CompilerParams(vmem_limit_bytes=VMEM_LIMIT),
        name="moe_route",
    )(bid)


def _moe_kernel(meta_ref, pos_ref, h2_hbm, wga_ref, wua_ref, wda_ref, wgb_ref, wub_ref, wdb_ref,
                y_ref, src_ref, xbuf, sem, wga_s, wua_s, wda_s, wgb_s, wub_s, wdb_s):
    j = pl.program_id(0)

    def row_copy(tile, r, slot):
        tok = src_ref[tile * ROW_TILE + r]
        return pltpu.make_async_copy(h2_hbm.at[pl.ds(tok, 1), :], xbuf.at[slot, pl.ds(r, 1), :], sem.at[slot])

    def issue_rows(tile, slot, lo, hi):
        for r in range(lo, hi):
            row_copy(tile, r, slot).start()

    def wait(slot):
        pltpu.make_async_copy(h2_hbm.at[pl.ds(0, ROW_TILE), :], xbuf.at[slot], sem.at[slot]).wait()

    @pl.when(j == 0)
    def _():
        def fill(t, c):
            def body(r, c2):
                p = t * ROW_TILE + r
                src_ref[p] = p % T_ALL
                return c2
            n = meta_ref[3, t]
            lax.fori_loop(n, jnp.where(meta_ref[2, t] == 1, ROW_TILE, n), body, 0)
            return c
        lax.fori_loop(0, N_ROW_TILES, fill, 0)

        def invert(t, c):
            src_ref[pos_ref[t]] = t
            return c
        lax.fori_loop(0, T_ALL, invert, 0, unroll=8)

        @pl.when(meta_ref[2, 0] == 1)
        def _():
            def body(g, c):
                for k in range(8):
                    row_copy(0, g * 8 + k, 0).start()
                return c
            lax.fori_loop(0, ROW_TILE // 8, body, 0)

    nxt = jnp.minimum(j + 1, N_ROW_TILES - 1)
    has_next = (j + 1 < N_ROW_TILES) & (meta_ref[2, nxt] == 1)
    valid = meta_ref[2, j] == 1
    prev = jnp.maximum(j - 1, 0)

    @pl.when(valid & ((j == 0) | (meta_ref[0, j] != meta_ref[0, prev])))
    def _():
        wga_s[...] = wga_ref[0].astype(BF16)
        wua_s[...] = wua_ref[0].astype(BF16)
        wda_s[...] = wda_ref[0].astype(BF16)

    @pl.when(valid & ((j == 0) | (meta_ref[1, j] != meta_ref[1, prev])))
    def _():
        wgb_s[...] = wgb_ref[0].astype(BF16)
        wub_s[...] = wub_ref[0].astype(BF16)
        wdb_s[...] = wdb_ref[0].astype(BF16)

    def compute(fetch_next):
        slot = j % 2
        nslot = nxt % 2
        step = ROW_TILE // 8
        batches = iter(range(0, ROW_TILE, step))

        def fetch():
            if fetch_next:
                lo = next(batches)
                issue_rows(nxt, nslot, lo, lo + step)

        wait(slot)
        x = xbuf[slot, :, 0:D_MODEL].astype(BF16)
        gates = xbuf[slot, :, D_MODEL:H2_EXT]
        hg_a = _dot(x, wga_s[...])
        fetch()
        hu_a = _dot(x, wua_s[...])
        fetch()
        hg_b = _dot(x, wgb_s[...])
        fetch()
        hu_b = _dot(x, wub_s[...])
        fetch()
        act_a = (hg_a * jax.nn.sigmoid(hg_a) * hu_a * gates[:, 0:1]).astype(BF16)
        fetch()
        act_b = (hg_b * jax.nn.sigmoid(hg_b) * hu_b * gates[:, 1:2]).astype(BF16)
        fetch()
        y = _dot(act_a, wda_s[...])
        fetch()
        y = y + _dot(act_b, wdb_s[...])
        fetch()
        y_ref[...] = y

    @pl.when(valid & has_next)
    def _():
        compute(True)

    @pl.when(valid & jnp.logical_not(has_next))
    def _():
        compute(False)

    @pl.when(jnp.logical_not(valid))
    def _():
        y_ref[...] = jnp.zeros_like(y_ref)


def _moe(meta, pos, h2ext, w_gate, w_up, w_down):
    up_spec = lambda slot: pl.BlockSpec((1, D_MODEL, EXPERT_FF), lambda j, meta, pos: (meta[slot, j], 0, 0))
    down_spec = lambda slot: pl.BlockSpec((1, EXPERT_FF, D_MODEL), lambda j, meta, pos: (meta[slot, j], 0, 0))
    grid_spec = pltpu.PrefetchScalarGridSpec(
        num_scalar_prefetch=2,
        grid=(N_ROW_TILES,),
        in_specs=[pl.BlockSpec(memory_space=pl.ANY),
                  up_spec(0), up_spec(0), down_spec(0), up_spec(1), up_spec(1), down_spec(1)],
        out_specs=pl.BlockSpec((ROW_TILE, D_MODEL), lambda j, meta, pos: (j, 0)),
        scratch_shapes=[pltpu.SMEM((ROW_CAP,), jnp.int32),
                        pltpu.VMEM((2, ROW_TILE, H2_EXT), F32),
                        pltpu.SemaphoreType.DMA((2,)),
                        pltpu.VMEM((D_MODEL, EXPERT_FF), BF16), pltpu.VMEM((D_MODEL, EXPERT_FF), BF16),
                        pltpu.VMEM((EXPERT_FF, D_MODEL), BF16),
                        pltpu.VMEM((D_MODEL, EXPERT_FF), BF16), pltpu.VMEM((D_MODEL, EXPERT_FF), BF16),
                        pltpu.VMEM((EXPERT_FF, D_MODEL), BF16)])
    return pl.pallas_call(
        _moe_kernel,
        out_shape=jax.ShapeDtypeStruct((ROW_CAP, D_MODEL), F32),
        grid_spec=grid_spec,
        compiler_params=_cparams(("arbitrary",)),
        name="moe_experts",
    )(meta, pos, h2ext, w_gate, w_up, w_down, w_gate, w_up, w_down)


def _final_kernel(pos_ref, y_hbm, x1_ref, m_ref, gn_ref, op_ref, os_ref, ybuf, sem):
    i = pl.program_id(0)

    def row_copy(tile, r, slot):
        p = pos_ref[tile * TILE + r]
        return pltpu.make_async_copy(y_hbm.at[pl.ds(p, 1), :], ybuf.at[slot, pl.ds(r, 1), :], sem.at[slot])

    def issue(tile, slot):
        def body(r2, c):
            row_copy(tile, 2 * r2, slot).start(priority=0)
            row_copy(tile, 2 * r2 + 1, slot).start(priority=1)
            return c
        lax.fori_loop(0, TILE // 2, body, 0, unroll=4)

    def wait(slot):
        pltpu.make_async_copy(y_hbm.at[pl.ds(0, TILE)], ybuf.at[slot], sem.at[slot]).wait()

    @pl.when(i == 0)
    def _():
        issue(0, 0)

    @pl.when(i + 1 < N_TILES)
    def _():
        issue(i + 1, (i + 1) % 2)

    slot = i % 2
    wait(slot)
    out = x1_ref[...] + m_ref[0, 5:6, :] * _rms(ybuf[slot], gn_ref[3:4, :])

    @pl.when(i < N_TILES_P)
    def _():
        op_ref[...] = out

    @pl.when(i >= N_TILES_P)
    def _():
        os_ref[...] = out


def _final(pos, y_sorted, x1, mods3, g_norm):
    tps = DEC_SEQ // TILE
    grid_spec = pltpu.PrefetchScalarGridSpec(
        num_scalar_prefetch=1,
        grid=(N_TILES,),
        in_specs=[pl.BlockSpec(memory_space=pl.ANY),
                  pl.BlockSpec((TILE, D_MODEL), lambda i, pos: (i, 0)),
                  pl.BlockSpec((1, N_MOD, D_MODEL), lambda i, pos: (_mod_row_of_tile(i, tps, N_TILES_P), 0, 0)),
                  pl.BlockSpec((4, D_MODEL), lambda i, pos: (0, 0))],
        out_specs=(pl.BlockSpec((TILE, D_MODEL), lambda i, pos: (jnp.minimum(i, N_TILES_P - 1), 0)),
                   pl.BlockSpec((TILE, D_MODEL), lambda i, pos: (jnp.maximum(i - N_TILES_P, 0), 0))),
        scratch_shapes=[pltpu.VMEM((2, TILE, D_MODEL), F32), pltpu.SemaphoreType.DMA((2,))])
    return pl.pallas_call(
        _final_kernel,
        out_shape=(jax.ShapeDtypeStruct((T_PROMPT, D_MODEL), F32),
                   jax.ShapeDtypeStruct((T_SAMPLE, D_MODEL), F32)),
        grid_spec=grid_spec,
        compiler_params=_cparams(("arbitrary",)),
        name="moe_combine_final",
    )(pos, y_sorted, x1, mods3, g_norm)


def kernel(x_prompt, x_sample, state_C, state_n, state_m, c, c_ctx, w_ada, b_ada, g_norm, w_in, ml_gate_bias, ml_head_gain, hy_conv_w, hy_f_w1, hy_f_b1, hy_f_w2, hy_f_b2, hy_f_w3, hy_f_b3, hy_decay, hy_bias, w_out, w_rc, b_rc, w_rf, b_rf, w_gate, w_up, w_down):
    xp = x_prompt.reshape(T_PROMPT, D_MODEL)
    xs = x_sample.reshape(T_SAMPLE, D_MODEL)
    gn = g_norm[0]

    cv = jnp.concatenate([c_ctx[None, :], c, jnp.zeros((MOD_ROWS - 1 - DEC_BATCH, D_MODEL), F32)], axis=0)
    mods3 = _ada(cv, w_ada[0], b_ada[0]).reshape(MOD_ROWS, N_MOD, D_MODEL)

    w_in0 = w_in[0]
    w_qkvo = w_in0[:, :ML_QKVO_COLS].astype(BF16)
    w_hy = w_in0[:, ML_QKVO_COLS + ML_GATE_COLS:].astype(BF16)
    wg = w_in0[:, ML_QKVO_COLS:ML_QKVO_COLS + ML_GATE_COLS]
    gbt = ml_gate_bias[0].reshape(ML_GATE_COLS, 1)
    proj, gates, gates_t = _inproj(xp, xs, mods3, gn, w_qkvo, w_hy, wg.T, gbt)

    gain = ml_head_gain[0].reshape(1, ML_WIDTH)
    y_ml_p, c_new, n_new, m_new = _mlstm(proj, gates, gates_t, gain, None, SEQ, BATCH, 0)
    state = (state_C[:, 0], state_n[:, 0], state_m[:, 0].reshape(DEC_BATCH, 2 * ML_HEADS, 1))
    y_ml_s, _, _, _ = _mlstm(proj, gates, gates_t, gain, state, DEC_SEQ, DEC_BATCH, T_PROMPT // DEC_SEQ)

    w1p = jnp.pad(hy_f_w1[0], ((0, 128 - HY_EMB), (0, 0)))
    b1 = hy_f_b1[0].reshape(1, -1)
    b2 = hy_f_b2[0].reshape(1, -1)
    b3 = hy_f_b3[0].reshape(1, -1)
    dec = hy_decay[0].reshape(1, -1)
    z_parts = []
    for seq_len, n_seq, row_off, width, seqs in ((SEQ, BATCH, 0, SEQ, 4), (DEC_SEQ, DEC_BATCH, T_PROMPT, GRID_W, 2)):
        f, ft = _dft_mats(seq_len)
        coefs = _hyena_filters(seq_len, f, w1p, b1, hy_f_w2[0], b2, hy_f_w3[0], b3, dec)
        z_parts.append(_hyena(proj, hy_conv_w[0], coefs, hy_bias[0], f, ft, seq_len, n_seq, row_off, width, seqs))
    z_p, z_s = z_parts

    pad_r = ROUTER_ROWS - N_GROUPS - N_EXPERTS
    w_r = jnp.pad(jnp.concatenate([w_rc[0], w_rf[0]], axis=1).T, ((0, pad_r), (0, 0)))
    b_r = jnp.pad(jnp.concatenate([b_rc[0], b_rf[0]], axis=0), (0, pad_r)).reshape(ROUTER_ROWS, 1)
    x1, h2ext, bid = _outproj(xp, xs, y_ml_p, y_ml_s, z_p, z_s, mods3, gn, w_out[0].astype(BF16), w_r, b_r)

    pos3, meta = _route(bid)
    pos = pos3.reshape(T_ALL)
    y_sorted = _moe(meta, pos, h2ext, w_gate[0], w_up[0], w_down[0])
    y_p, y_s = _final(pos, y_sorted, x1, mods3, gn)

    new_c = c_new.reshape(BATCH, 1, 2, ML_HEADS, ML_HEAD_DIM, ML_HEAD_DIM)
    new_n = n_new.reshape(BATCH, 1, 2, ML_HEADS, ML_HEAD_DIM)
    new_m = m_new[:, :, 0].reshape(BATCH, 1, 2, ML_HEADS)
    return (y_p.reshape(BATCH, SEQ, D_MODEL), y_s.reshape(DEC_BATCH, DEC_SEQ, D_MODEL), new_c, new_n, new_m)
```

```python
import functools
import math

import jax
import jax.numpy as jnp
import numpy as np
from jax import lax
from jax.experimental import pallas as pl
from jax.experimental.pallas import tpu as pltpu

F32 = jnp.float32
BF16 = jnp.bfloat16

D_MODEL = 1024
BATCH = 16
SEQ = 256
DEC_BATCH = 4
DEC_SEQ = 1024
GRID_W = 64
ML_WIDTH = 512
ML_HEADS = 4
ML_HEAD_DIM = 128
HY_WIDTH = 512
HY_ORDER = 2
HY_EMB = 33
HY_BANDS = 16
HY_FILTER_HIDDEN = 64
HY_MOD_SHIFT = 0.05
N_GROUPS = 4
EXPERTS_PER_GROUP = 4
N_EXPERTS = 16
EXPERT_FF = 512
N_MOD = 6
EPS = 1e-6
ML_QKVO_COLS = 4 * ML_WIDTH
ML_GATE_COLS = 4 * ML_HEADS
HY_COLS = 3 * HY_WIDTH
MAIN_COLS = ML_QKVO_COLS + HY_COLS

T_PROMPT = BATCH * SEQ
T_SAMPLE = DEC_BATCH * DEC_SEQ
T_ALL = T_PROMPT + T_SAMPLE
TILE = 256
N_TILES_P = T_PROMPT // TILE
N_TILES = T_ALL // TILE
MOD_ROWS = 8
K_SCALE = ML_HEAD_DIM ** -0.5
VMEM_LIMIT = 56 * 1024 * 1024


def _cparams(sem):
    return pltpu.CompilerParams(dimension_semantics=sem, vmem_limit_bytes=VMEM_LIMIT)


def _split2(x):
    hi = x.astype(BF16)
    lo = (x - hi.astype(F32)).astype(BF16)
    return hi, lo


def _dot(a, b):
    return jnp.dot(a, b, preferred_element_type=F32)


def _dot_nt(a, b):
    return lax.dot_general(a, b, (((1,), (1,)), ((), ())), preferred_element_type=F32)


def _dot_tn(a, b):
    return lax.dot_general(a, b, (((0,), (0,)), ((), ())), preferred_element_type=F32)


def _dot3(a, b):
    ah, al = _split2(a)
    bh, bl = _split2(b)
    return _dot(ah, bh) + _dot(al, bh) + _dot(ah, bl)


def _dot3_nt(a, b):
    ah, al = _split2(a)
    bh, bl = _split2(b)
    return _dot_nt(ah, bh) + _dot_nt(al, bh) + _dot_nt(ah, bl)


def _dot_exact_lhs(t, x):
    x1 = x.astype(BF16)
    r1 = x - x1.astype(F32)
    x2 = r1.astype(BF16)
    x3 = (r1 - x2.astype(F32)).astype(BF16)
    return _dot(t, x1) + _dot(t, x2) + _dot(t, x3)


def _dot_exact_rhs(x, t):
    x1 = x.astype(BF16)
    r1 = x - x1.astype(F32)
    x2 = r1.astype(BF16)
    x3 = (r1 - x2.astype(F32)).astype(BF16)
    return _dot(x1, t) + _dot(x2, t) + _dot(x3, t)


def _rms(x, g):
    return x * lax.rsqrt(jnp.mean(x * x, axis=-1, keepdims=True) + EPS) * g


def _mod_row_of_tile(i, tiles_per_sample_seq, n_prompt_tiles):
    return jnp.where(i < n_prompt_tiles, 0, 1 + (i - n_prompt_tiles) // tiles_per_sample_seq)


def _ada_kernel(cv_ref, w_ref, b_ref, o_ref):
    cv = cv_ref[...]
    s = cv * jax.nn.sigmoid(cv)
    o_ref[...] = _dot3(s, w_ref[...]) + b_ref[...]


def _ada(cv, w_ada, b_ada):
    n = N_MOD * D_MODEL
    return pl.pallas_call(
        _ada_kernel,
        out_shape=jax.ShapeDtypeStruct((MOD_ROWS, n), F32),
        grid=(N_MOD,),
        in_specs=[pl.BlockSpec((MOD_ROWS, D_MODEL), lambda j: (0, 0)),
                  pl.BlockSpec((D_MODEL, D_MODEL), lambda j: (0, j)),
                  pl.BlockSpec((1, D_MODEL), lambda j: (0, j))],
        out_specs=pl.BlockSpec((MOD_ROWS, D_MODEL), lambda j: (0, j)),
        compiler_params=_cparams(("arbitrary",)),
        name="ada_mod",
    )(cv, w_ada, b_ada.reshape(1, n))


PREP_COLS = 512


def _prep_cast_kernel(w_ref, o_ref):
    o_ref[...] = w_ref[...].astype(BF16)


def _prep_shift_kernel(a_ref, b_ref, o_ref):
    sh = ML_GATE_COLS
    o_ref[...] = jnp.concatenate([a_ref[:, sh:], b_ref[:, :sh]], axis=1).astype(BF16)


def _prep_in_weights(w_in0):
    blk = lambda off: pl.BlockSpec((D_MODEL, PREP_COLS), lambda j: (0, off + j))
    w_qkvo = pl.pallas_call(
        _prep_cast_kernel,
        out_shape=jax.ShapeDtypeStruct((D_MODEL, ML_QKVO_COLS), BF16),
        grid=(ML_QKVO_COLS // PREP_COLS,), in_specs=[blk(0)], out_specs=blk(0),
        compiler_params=_cparams(("arbitrary",)), name="prep_w_qkvo",
    )(w_in0)
    first = ML_QKVO_COLS // PREP_COLS
    w_hy = pl.pallas_call(
        _prep_shift_kernel,
        out_shape=jax.ShapeDtypeStruct((D_MODEL, HY_COLS), BF16),
        grid=(HY_COLS // PREP_COLS,), in_specs=[blk(first), blk(first + 1)], out_specs=blk(0),
        compiler_params=_cparams(("arbitrary",)), name="prep_w_hy",
    )(w_in0, w_in0)
    return w_qkvo, w_hy


def _log_sigmoid(x):
    return jnp.minimum(x, 0.0) - jnp.log1p(jnp.exp(-jnp.abs(x)))


def _rows_to_cols(rows):
    ri = lax.broadcasted_iota(jnp.int32, (TILE, TILE), 0)
    ci = lax.broadcasted_iota(jnp.int32, (TILE, TILE), 1)
    eye = jnp.where(ri == ci, 1.0, 0.0).astype(BF16)
    p1 = rows.astype(BF16)
    r1 = rows - p1.astype(F32)
    p2 = r1.astype(BF16)
    p3 = (r1 - p2.astype(F32)).astype(BF16)
    return _dot_nt(eye, p1) + _dot_nt(eye, p2) + _dot_nt(eye, p3)


BIG_TILE = 2 * TILE
N_BIG_P = T_PROMPT // BIG_TILE
N_BIG = T_ALL // BIG_TILE


def _inproj_kernel(xp_ref, xs_ref, m_ref, gn_ref, wq_ref, wh_ref, wgt_ref, gbt_ref, proj_ref, gate_ref, gatet_ref):
    is_p = pl.program_id(0) < N_BIG_P
    halves = [slice(r * TILE, (r + 1) * TILE) for r in range(BIG_TILE // TILE)]
    hs = [_rms(jnp.where(is_p, xp_ref[rows, :], xs_ref[rows, :]), gn_ref[0:1, :]) * (1.0 + m_ref[0, 1:2, :])
          + m_ref[0, 0:1, :] for rows in halves]
    hbs = [h.astype(BF16) for h in hs]
    cb = 512
    for j in range(ML_QKVO_COLS // cb):
        for rows, hb in zip(halves, hbs):
            proj_ref[rows, j * cb:(j + 1) * cb] = _dot(hb, wq_ref[:, j * cb:(j + 1) * cb]).astype(BF16)
    for j in range(HY_COLS // cb):
        lo = ML_QKVO_COLS + j * cb
        for rows, hb in zip(halves, hbs):
            proj_ref[rows, lo:lo + cb] = _dot(hb, wh_ref[:, j * cb:(j + 1) * cb]).astype(BF16)
    wth, wtl = _split2(wgt_ref[...])
    gts = []
    for h, hb in zip(hs, hbs):
        hl = (h - hb.astype(F32)).astype(BF16)
        gt = _dot_nt(wth, hb) + _dot_nt(wth, hl) + _dot_nt(wtl, hb) + gbt_ref[...]
        row = lax.broadcasted_iota(jnp.int32, gt.shape, 0)
        gts.append(jnp.where((row % 8) >= 4, _log_sigmoid(gt), gt))
    for r, (rows, gt) in enumerate(zip(halves, gts)):
        gatet_ref[r] = gt
        gate_ref[rows, :] = _rows_to_cols(gt)


def _inproj(xp, xs, mods3, g_norm, w_qkvo, w_hy, wgt, gbt):
    tps = DEC_SEQ // BIG_TILE
    per = BIG_TILE // TILE
    return pl.pallas_call(
        _inproj_kernel,
        out_shape=(jax.ShapeDtypeStruct((T_ALL, MAIN_COLS), BF16),
                   jax.ShapeDtypeStruct((T_ALL, ML_GATE_COLS), F32),
                   jax.ShapeDtypeStruct((N_TILES, ML_GATE_COLS, TILE), F32)),
        grid=(N_BIG,),
        in_specs=[pl.BlockSpec((BIG_TILE, D_MODEL), lambda i: (jnp.minimum(i, N_BIG_P - 1), 0)),
                  pl.BlockSpec((BIG_TILE, D_MODEL), lambda i: (jnp.maximum(i - N_BIG_P, 0), 0)),
                  pl.BlockSpec((1, N_MOD, D_MODEL), lambda i: (_mod_row_of_tile(i, tps, N_BIG_P), 0, 0)),
                  pl.BlockSpec((4, D_MODEL), lambda i: (0, 0)),
                  pl.BlockSpec((D_MODEL, ML_QKVO_COLS), lambda i: (0, 0)),
                  pl.BlockSpec((D_MODEL, HY_COLS), lambda i: (0, 0)),
                  pl.BlockSpec((ML_GATE_COLS, D_MODEL), lambda i: (0, 0)),
                  pl.BlockSpec((ML_GATE_COLS, 1), lambda i: (0, 0))],
        out_specs=(pl.BlockSpec((BIG_TILE, MAIN_COLS), lambda i: (i, 0)),
                   pl.BlockSpec((BIG_TILE, ML_GATE_COLS), lambda i: (i, 0)),
                   pl.BlockSpec((per, ML_GATE_COLS, TILE), lambda i: (i, 0, 0))),
        compiler_params=_cparams(("arbitrary",)),
        name="in_proj",
    )(xp, xs, mods3, g_norm, w_qkvo, w_hy, wgt, gbt)


ST_ROWS = ML_HEAD_DIM + 16


def _mlstm_kernel(*refs, seq_len, has_state):
    if has_state:
        (q_ref, k_ref, v_ref, o_ref, g_ref, gt_ref, gain_ref, c0_ref, n0_ref, m0_ref,
         y_ref, c_ref, n_ref, m_ref, vt_ref, hf_ref, hb_ref, st_ref, ms_ref) = refs
    else:
        (q_ref, k_ref, v_ref, o_ref, g_ref, gt_ref, gain_ref,
         y_ref, c_ref, n_ref, m_ref, vt_ref, hf_ref, hb_ref, st_ref, ms_ref) = refs
    ch = TILE
    nc = seq_len // ch
    hd = ML_HEAD_DIM
    key = lax.broadcasted_iota(jnp.int32, (ch, ch), 0)
    qry = lax.broadcasted_iota(jnp.int32, (ch, ch), 1)
    key_le = key <= qry
    key_ge = key >= qry
    t_le = jnp.where(key_le, 1.0, 0.0).astype(BF16)
    t_ge = jnp.where(key_ge, 1.0, 0.0).astype(BF16)
    sub16 = lax.broadcasted_iota(jnp.int32, (16, ch), 0)
    ln_scale = math.log(K_SCALE)

    for c in range(nc):
        for h in range(ML_HEADS):
            cols = slice(h * hd, (h + 1) * hd)
            vt_ref[c, cols, :] = v_ref[c * ch:(c + 1) * ch, cols].astype(F32).T.astype(BF16)

    for d in range(2):
        for h in range(ML_HEADS):
            r = d * ML_HEADS + h
            st_ref[r] = jnp.zeros((ST_ROWS, hd), F32)
            if has_state:
                st_ref[r, 0:hd, :] = c0_ref[0, d, h].T
                st_ref[r, hd:hd + 1, :] = n0_ref[0, d, h:h + 1, :]
                ms_ref[r] = jnp.broadcast_to(m0_ref[0, r:r + 1, :], (1, ch))
            else:
                ms_ref[r] = jnp.zeros((1, ch), F32)

    def step(t, carry):
        for d in range(2):
            c = t if d == 0 else nc - 1 - t
            rows = pl.ds(pl.multiple_of(c * ch, ch), ch)
            gcol = g_ref[rows, :]
            grow = gt_ref[c]
            brow_all = _dot_exact_rhs(grow, t_le if d == 0 else t_ge)
            bcol_all = _dot_exact_lhs(t_ge if d == 0 else t_le, gcol)
            mask = key_le if d == 0 else key_ge
            hacc_ref = hf_ref if d == 0 else hb_ref
            heads = range(ML_HEADS)
            regs = [d * ML_HEADS + h for h in heads]
            colss = [slice(h * hd, (h + 1) * hd) for h in heads]
            qs = [q_ref[rows, cols] for cols in colss]
            ks = [k_ref[rows, cols] for cols in colss]
            vts = [vt_ref[c, cols, :] for cols in colss]
            sts = [st_ref[r] for r in regs]
            m_prevs = [ms_ref[r] for r in regs]
            b_rows = [brow_all[(1 + 2 * d) * ML_HEADS + h:(1 + 2 * d) * ML_HEADS + h + 1, :] for h in heads]
            ig_rows = [grow[2 * d * ML_HEADS + h:2 * d * ML_HEADS + h + 1, :] for h in heads]
            qks = [_dot_nt(k, q) for k, q in zip(ks, qs)]
            iqs = [_dot_nt(st.astype(BF16), q) for st, q in zip(sts, qs)]
            ss, sc_inters, m_poss = [], [], []
            for h in heads:
                fcol = (1 + 2 * d) * ML_HEADS + h
                icol = 2 * d * ML_HEADS + h
                c_col = gcol[:, icol:icol + 1] - bcol_all[:, fcol:fcol + 1]
                logd = jnp.where(mask, b_rows[h] + c_col, -jnp.inf)
                inter = b_rows[h] + m_prevs[h]
                m_pos = jnp.maximum(inter, jnp.max(logd, axis=0, keepdims=True))
                ss.append(qks[h] * jnp.exp(logd - (m_pos - ln_scale)))
                sc_inters.append(jnp.exp(inter - m_pos))
                m_poss.append(m_pos)
            pvs = [_dot(vt, s.astype(BF16)) for vt, s in zip(vts, ss)]
            for h in heads:
                num = sc_inters[h] * iqs[h][0:hd] + pvs[h]
                den = sc_inters[h] * iqs[h][hd:hd + 1] + jnp.sum(ss[h], axis=0, keepdims=True)
                hacc_ref[c, colss[h], :] = num * (1.0 / jnp.maximum(jnp.abs(den), jnp.exp(-m_poss[h])))
            lhss, decays = [], []
            for h in heads:
                b_row = b_rows[h]
                b_last = b_row[:, ch - 1:ch] if d == 0 else b_row[:, 0:1]
                logw = b_last - b_row + ig_rows[h]
                m_new = jnp.maximum(b_last + m_prevs[h], jnp.max(logw, axis=1, keepdims=True))
                w = jnp.exp(logw - (m_new - ln_scale))
                decays.append(jnp.exp(b_last + m_prevs[h] - m_new))
                lhss.append(jnp.concatenate([(vts[h].astype(F32) * w).astype(BF16),
                                             jnp.where(sub16 == 0, w, 0.0).astype(BF16)], axis=0))
                ms_ref[regs[h]] = m_new
            upds = [_dot(lhs, k) for lhs, k in zip(lhss, ks)]
            for h in heads:
                st_ref[regs[h]] = decays[h][:, 0:hd] * sts[h] + upds[h]
        return carry

    lax.fori_loop(0, nc, step, 0)

    for d in range(2):
        for h in range(ML_HEADS):
            r = d * ML_HEADS + h
            c_ref[0, d, h] = st_ref[r, 0:hd, :].T
            n_ref[0, d, h:h + 1, :] = st_ref[r, hd:hd + 1, :]
            m_ref[0, r:r + 1, :] = ms_ref[r][:, 0:hd]
    for c in range(nc):
        for h in range(ML_HEADS):
            cols = slice(h * hd, (h + 1) * hd)
            ht = hf_ref[c, cols, :] + hb_ref[c, cols, :]
            ht = ht * lax.rsqrt(jnp.mean(ht * ht, axis=0, keepdims=True) + EPS)
            rows = slice(c * ch, (c + 1) * ch)
            y = ht.T * gain_ref[:, cols] * jax.nn.sigmoid(o_ref[rows, cols].astype(F32))
            y_ref[rows, cols] = y.astype(BF16)


def _mlstm(proj, gates, gates_t, gain, state, seq_len, n_seq, row_block_off):
    has_state = state is not None
    tiles = seq_len // TILE
    off = row_block_off
    qkvo_specs = [pl.BlockSpec((seq_len, ML_WIDTH), functools.partial(lambda b, j: (off + b, j), j=j))
                  for j in range(4)]
    in_specs = qkvo_specs + [
        pl.BlockSpec((seq_len, ML_GATE_COLS), lambda b: (off + b, 0)),
        pl.BlockSpec((tiles, ML_GATE_COLS, TILE), lambda b: (off + b, 0, 0)),
        pl.BlockSpec((1, ML_WIDTH), lambda b: (0, 0)),
    ]
    args = [proj, proj, proj, proj, gates, gates_t, gain]
    if has_state:
        c0, n0, m0 = state
        in_specs += [
            pl.BlockSpec((1, 2, ML_HEADS, ML_HEAD_DIM, ML_HEAD_DIM), lambda b: (b, 0, 0, 0, 0)),
            pl.BlockSpec((1, 2, ML_HEADS, ML_HEAD_DIM), lambda b: (b, 0, 0, 0)),
            pl.BlockSpec((1, 2 * ML_HEADS, 1), lambda b: (b, 0, 0)),
        ]
        args += [c0, n0, m0]
    out_shape = (jax.ShapeDtypeStruct((n_seq * seq_len, ML_WIDTH), BF16),
                 jax.ShapeDtypeStruct((n_seq, 2, ML_HEADS, ML_HEAD_DIM, ML_HEAD_DIM), F32),
                 jax.ShapeDtypeStruct((n_seq, 2, ML_HEADS, ML_HEAD_DIM), F32),
                 jax.ShapeDtypeStruct((n_seq, 2 * ML_HEADS, ML_HEAD_DIM), F32))
    out_specs = (pl.BlockSpec((seq_len, ML_WIDTH), lambda b: (b, 0)),
                 pl.BlockSpec((1, 2, ML_HEADS, ML_HEAD_DIM, ML_HEAD_DIM), lambda b: (b, 0, 0, 0, 0)),
                 pl.BlockSpec((1, 2, ML_HEADS, ML_HEAD_DIM), lambda b: (b, 0, 0, 0)),
                 pl.BlockSpec((1, 2 * ML_HEADS, ML_HEAD_DIM), lambda b: (b, 0, 0)))
    scratch = [pltpu.VMEM((tiles, ML_WIDTH, TILE), BF16),
               pltpu.VMEM((tiles, ML_WIDTH, TILE), F32), pltpu.VMEM((tiles, ML_WIDTH, TILE), F32),
               pltpu.VMEM((2 * ML_HEADS, ST_ROWS, ML_HEAD_DIM), F32),
               pltpu.VMEM((2 * ML_HEADS, 1, TILE), F32)]
    return pl.pallas_call(
        functools.partial(_mlstm_kernel, seq_len=seq_len, has_state=has_state),
        out_shape=out_shape, grid=(n_seq,), in_specs=in_specs, out_specs=out_specs,
        scratch_shapes=scratch, compiler_params=_cparams(("arbitrary",)),
        name=f"mlstm_{seq_len}",
    )(*args)


def _dft_mats(seq_len):
    k = np.arange(seq_len, dtype=np.int64)[:, None]
    d = np.arange(seq_len, dtype=np.int64)[None, :]
    ang = np.pi * ((k * d) % (2 * seq_len)).astype(np.float64) / seq_len
    sinm = np.sin(ang)
    sinm[0, :] = np.where(d[0] % 2 == 0, 1.0, -1.0)
    f = np.concatenate([np.cos(ang), sinm], axis=0).astype(np.float32)
    return jnp.asarray(f).astype(BF16), jnp.asarray(np.ascontiguousarray(f.T)).astype(BF16)


def _filter_feats(seq_len):
    t = np.linspace(0.0, 1.0, seq_len, dtype=np.float64)[:, None]
    wpos = 2.0 * np.pi * np.arange(seq_len, dtype=np.float64)[:, None] / seq_len
    bands = np.linspace(1e-4, HY_BANDS - 1, HY_BANDS, dtype=np.float64)[None, :]
    z = np.concatenate([t, np.cos(bands * wpos), -np.sin(bands * wpos)], axis=-1)
    return jnp.asarray(np.pad(z, ((0, 0), (0, 128 - HY_EMB))).astype(np.float32))


def _filter_kernel(z_ref, w1_ref, b1_ref, w2_ref, b2_ref, w3_ref, b3_ref, dec_ref, f_ref,
                   a_ref, b_ref, d_ref, *, seq_len):
    n = 2 * seq_len
    oc = 2 * HY_WIDTH
    z = z_ref[...]
    h = jnp.sin(_dot3(z, w1_ref[...]) + b1_ref[...])
    h = jnp.sin(_dot3(h, w2_ref[...]) + b2_ref[...])
    t = z[:, 0:1]
    di = lax.broadcasted_iota(jnp.int32, (seq_len, 1), 0)
    sgn = jnp.where(di % 2 == 0, 1.0, -1.0)
    first = di == 0
    ssums, sdifs = [], []
    for o in range(HY_ORDER):
        cols = slice(o * oc, (o + 1) * oc)
        g = _dot3(h, w3_ref[:, cols]) + b3_ref[:, cols]
        g = g * (jnp.exp(-t * jnp.abs(dec_ref[:, cols])) + HY_MOD_SHIFT)
        ss = jnp.sum(g * g, axis=0, keepdims=True)
        inv = lax.rsqrt(ss[:, :HY_WIDTH] + ss[:, HY_WIDTH:] + EPS)
        hp = g[:, :HY_WIDTH] * inv
        hn = g[:, HY_WIDTH:] * inv
        ssums.append(hp + hn)
        sdifs.append(hp - hn)
    hcs = [_dot(f_ref[0:seq_len, :], s.astype(BF16)) for s in ssums]
    hss = [_dot(f_ref[seq_len:n, :], s.astype(BF16)) for s in sdifs]
    for o in range(HY_ORDER):
        nyq = jnp.sum(ssums[o] * sgn, axis=0, keepdims=True)
        a_ref[o] = hcs[o] * jnp.where(first, 1.0 / n, 2.0 / n)
        b_ref[o] = jnp.where(first, 0.0, hss[o] * (2.0 / n))
        d_ref[o] = jnp.where(first, nyq * (1.0 / n), hcs[o] * (2.0 / n))


def _hyena_filters(seq_len, f, w1p, b1, w2, b2, w3, b3, dec):
    z = _filter_feats(seq_len)
    out = jax.ShapeDtypeStruct((HY_ORDER, seq_len, HY_WIDTH), F32)
    return pl.pallas_call(
        functools.partial(_filter_kernel, seq_len=seq_len),
        out_shape=(out, out, out),
        compiler_params=pltpu.CompilerParams(vmem_limit_bytes=VMEM_LIMIT),
        name=f"hyena_filter_{seq_len}",
    )(z, w1p, b1, w2, b2, w3, b3, dec, f)


def _hyena_kernel(x1_ref, x2_ref, v_ref, cw1_ref, cw2_ref, cwv_ref, a_ref, b_ref, d_ref, bias_ref,
                  f_ref, ft_ref, z_ref, *, seq_len, width, seqs):
    rows = seqs * seq_len
    ti = lax.broadcasted_iota(jnp.int32, (rows, 1), 0)
    has_prev = (ti % width) != 0
    has_next = (ti % width) != (width - 1)

    def short_conv(x_ref, w_ref):
        x = x_ref[...].astype(F32)
        prev = jnp.where(has_prev, pltpu.roll(x, 1, axis=0), 0.0)
        nxt = jnp.where(has_next, pltpu.roll(x, rows - 1, axis=0), 0.0)
        return w_ref[0:1, :] * prev + w_ref[1:2, :] * x + w_ref[2:3, :] * nxt

    gates = (short_conv(x1_ref, cw1_ref), short_conv(x2_ref, cw2_ref))
    v = short_conv(v_ref, cwv_ref)
    sls = [slice(i * seq_len, (i + 1) * seq_len) for i in range(seqs)]
    zs = [v[sl] for sl in sls]
    for o in range(HY_ORDER):
        a, b, dd = a_ref[o], b_ref[o], d_ref[o]
        us = [_dot(f_ref[...], z.astype(BF16)) for z in zs]
        ys = []
        for u in us:
            ut = u[:seq_len]
            ub = u[seq_len:]
            ys.append(((ut * a - ub * b).astype(BF16), (ut * b + ub * dd).astype(BF16)))
        convs = [_dot(ft_ref[:, :seq_len], yt) + _dot(ft_ref[:, seq_len:], yb) for yt, yb in ys]
        zs = [gates[o][sl] * (y + bias_ref[o:o + 1, :] * z) for sl, y, z in zip(sls, convs, zs)]
    for sl, z in zip(sls, zs):
        z_ref[sl, :] = z.astype(BF16)


def _hyena(proj, conv_w, coefs, hy_bias, f, ft, seq_len, n_seq, row_off, width, seqs):
    cb = 256
    nblk = HY_WIDTH // cb
    base = ML_QKVO_COLS // cb
    rows = seqs * seq_len
    off = row_off // rows
    a, b, d = coefs

    def col_spec(part):
        return pl.BlockSpec((rows, cb), lambda j, s: (off + s, base + part * nblk + j))

    def w_spec(part):
        return pl.BlockSpec((3, cb), lambda j, s: (0, part * nblk + j))

    coef_spec = pl.BlockSpec((HY_ORDER, seq_len, cb), lambda j, s: (0, 0, j))
    return pl.pallas_call(
        functools.partial(_hyena_kernel, seq_len=seq_len, width=width, seqs=seqs),
        out_shape=jax.ShapeDtypeStruct((n_seq * seq_len, HY_WIDTH), BF16),
        grid=(nblk, n_seq // seqs),
        in_specs=[col_spec(0), col_spec(1), col_spec(2), w_spec(0), w_spec(1), w_spec(2),
                  coef_spec, coef_spec, coef_spec,
                  pl.BlockSpec((HY_ORDER, cb), lambda j, s: (0, j)),
                  pl.BlockSpec((2 * seq_len, seq_len), lambda j, s: (0, 0)),
                  pl.BlockSpec((seq_len, 2 * seq_len), lambda j, s: (0, 0))],
        out_specs=pl.BlockSpec((rows, cb), lambda j, s: (s, j)),
        compiler_params=_cparams(("arbitrary", "arbitrary")),
        name=f"hyena_conv_{seq_len}",
    )(proj, proj, proj, conv_w, conv_w, conv_w, a, b, d, hy_bias, f, ft)


def _first_max(x, n):
    mx = jnp.max(x, axis=0, keepdims=True)
    row = lax.broadcasted_iota(jnp.int32, x.shape, 0).astype(F32)
    idx = jnp.min(jnp.where(x == mx, row, float(n)), axis=0, keepdims=True)
    return mx, idx.astype(jnp.int32)


ROUTER_ROWS = 32
PAIRS_PER_GROUP = 6
N_BUCKETS = N_GROUPS * PAIRS_PER_GROUP
PAIR_SLOTS = ((0, 1), (0, 2), (0, 3), (1, 3), (1, 2), (3, 2))
LANES = 128
H2_EXT = D_MODEL + LANES
ROW_TILE = 256
ROW_CAP = T_ALL + N_BUCKETS * ROW_TILE
N_ROW_TILES = ROW_CAP // ROW_TILE


def _outproj_kernel(xp_ref, xs_ref, yp_ref, ys_ref, zp_ref, zs_ref, m_ref, gn_ref, wo_ref, wr_ref, br_ref,
                    x1_ref, h2_ref, bid_ref):
    is_p = pl.program_id(0) < N_BIG_P
    wrh, wrl = _split2(wr_ref[...])
    halves = [slice(r * TILE, (r + 1) * TILE) for r in range(BIG_TILE // TILE)]
    ys = [_dot(jnp.where(is_p, yp_ref[rows, :], ys_ref[rows, :]), wo_ref[0:ML_WIDTH, :])
          + _dot(jnp.where(is_p, zp_ref[rows, :], zs_ref[rows, :]), wo_ref[ML_WIDTH:, :]) for rows in halves]
    h2s = []
    for rows, y in zip(halves, ys):
        x = jnp.where(is_p, xp_ref[rows, :], xs_ref[rows, :])
        x1 = x + m_ref[0, 2:3, :] * _rms(y, gn_ref[1:2, :])
        x1_ref[rows, :] = x1
        h2 = _rms(x1, gn_ref[2:3, :]) * (1.0 + m_ref[0, 4:5, :]) + m_ref[0, 3:4, :]
        h2_ref[rows, 0:D_MODEL] = h2
        h2s.append(h2)
    logits = []
    for h2 in h2s:
        h2h, h2l = _split2(h2)
        logits.append(_dot_nt(wrh, h2h) + _dot_nt(wrh, h2l) + _dot_nt(wrl, h2h) + br_ref[...])
    routed = [_route_tile(lg) for lg in logits]
    for r, (rows, (gate_rows, bucket)) in enumerate(zip(halves, routed)):
        h2_ref[rows, D_MODEL:H2_EXT] = jnp.zeros((TILE, LANES), F32)
        h2_ref[rows, D_MODEL:D_MODEL + 8] = _rows_to_cols(gate_rows)
        bid_ref[r] = bucket


def _route_tile(logits):
    lc = logits[0:N_GROUPS]
    mx, gi = _first_max(lc, N_GROUPS)
    p_grp = 1.0 / jnp.sum(jnp.exp(lc - mx), axis=0, keepdims=True)
    lsel = jnp.zeros((EXPERTS_PER_GROUP, TILE), F32)
    for g in range(N_GROUPS):
        lo = N_GROUPS + g * EXPERTS_PER_GROUP
        lsel = jnp.where(gi == g, logits[lo:lo + EXPERTS_PER_GROUP], lsel)
    l1, i1 = _first_max(lsel, EXPERTS_PER_GROUP)
    sub4 = lax.broadcasted_iota(jnp.int32, lsel.shape, 0)
    l2, i2 = _first_max(jnp.where(sub4 == i1, -jnp.inf, lsel), EXPERTS_PER_GROUP)
    e2 = jnp.exp(l2 - l1)
    w1 = p_grp / (1.0 + e2)
    w2 = p_grp * e2 / (1.0 + e2)
    lo_e = jnp.minimum(i1, i2)
    hi_e = jnp.maximum(i1, i2)
    pair = jnp.where(lo_e == 0, hi_e - 1, jnp.where(lo_e == 1, jnp.where(hi_e == 3, 3, 4), 5))
    slot_a = jnp.where(pair == 5, hi_e, lo_e)
    first_in_a = i1 == slot_a
    w_a = jnp.where(first_in_a, w1, w2)
    w_b = jnp.where(first_in_a, w2, w1)
    sub = lax.broadcasted_iota(jnp.int32, (8, TILE), 0)
    gate_rows = jnp.where(sub == 0, w_a, jnp.where(sub == 1, w_b, 0.0))
    return gate_rows, gi * PAIRS_PER_GROUP + pair


def _outproj(xp, xs, yp, ys, zp, zs, mods3, g_norm, w_out, w_r, b_r):
    tps = DEC_SEQ // BIG_TILE
    per = BIG_TILE // TILE
    pidx = lambda i: (jnp.minimum(i, N_BIG_P - 1), 0)
    sidx = lambda i: (jnp.maximum(i - N_BIG_P, 0), 0)
    return pl.pallas_call(
        _outproj_kernel,
        out_shape=(jax.ShapeDtypeStruct((T_ALL, D_MODEL), F32),
                   jax.ShapeDtypeStruct((T_ALL, H2_EXT), F32),
                   jax.ShapeDtypeStruct((N_TILES, 1, TILE), jnp.int32)),
        grid=(N_BIG,),
        in_specs=[pl.BlockSpec((BIG_TILE, D_MODEL), pidx), pl.BlockSpec((BIG_TILE, D_MODEL), sidx),
                  pl.BlockSpec((BIG_TILE, ML_WIDTH), pidx), pl.BlockSpec((BIG_TILE, ML_WIDTH), sidx),
                  pl.BlockSpec((BIG_TILE, HY_WIDTH), pidx), pl.BlockSpec((BIG_TILE, HY_WIDTH), sidx),
                  pl.BlockSpec((1, N_MOD, D_MODEL), lambda i: (_mod_row_of_tile(i, tps, N_BIG_P), 0, 0)),
                  pl.BlockSpec((4, D_MODEL), lambda i: (0, 0)),
                  pl.BlockSpec((D_MODEL, D_MODEL), lambda i: (0, 0)),
                  pl.BlockSpec((ROUTER_ROWS, D_MODEL), lambda i: (0, 0)),
                  pl.BlockSpec((ROUTER_ROWS, 1), lambda i: (0, 0))],
        out_specs=(pl.BlockSpec((BIG_TILE, D_MODEL), lambda i: (i, 0)),
                   pl.BlockSpec((BIG_TILE, H2_EXT), lambda i: (i, 0)),
                   pl.BlockSpec((per, 1, TILE), lambda i: (i, 0, 0))),
        compiler_params=_cparams(("arbitrary",)),
        name="out_proj_router",
    )(xp, xs, yp, ys, zp, zs, mods3, g_norm, w_out, w_r, b_r)


def _route_kernel(bid_ref, pos_ref, meta_ref):
    nb = 32
    tm = float(ROW_TILE)
    sub = lax.broadcasted_iota(jnp.int32, (nb, TILE), 0)
    ri = lax.broadcasted_iota(jnp.int32, (TILE, TILE), 0)
    ci = lax.broadcasted_iota(jnp.int32, (TILE, TILE), 1)
    before = jnp.where(ri < ci, 1.0, 0.0).astype(BF16)

    def onehot(blk):
        return jnp.where(sub == bid_ref[blk], 1.0, 0.0)

    zeros = jnp.zeros((nb, 1), F32)
    cnt = lax.fori_loop(0, N_TILES, lambda blk, c: c + jnp.sum(onehot(blk), axis=1, keepdims=True), zeros)
    padded = jnp.floor((cnt + (tm - 1.0)) * (1.0 / tm)) * tm
    r32 = lax.broadcasted_iota(jnp.int32, (nb, nb), 0)
    c32 = lax.broadcasted_iota(jnp.int32, (nb, nb), 1)
    padded_row = jnp.sum(jnp.where(r32 == c32, padded, 0.0), axis=0, keepdims=True)
    offs = jnp.sum(jnp.where(c32 < r32, padded_row, 0.0), axis=1, keepdims=True)
    ends = offs + padded

    def place(blk, seen):
        oh = onehot(blk)
        rank = _dot(oh.astype(BF16), before)
        pos = jnp.sum(oh * (rank + seen + offs), axis=0, keepdims=True)
        pos_ref[blk] = pos.astype(jnp.int32)
        return seen + jnp.sum(oh, axis=1, keepdims=True)

    lax.fori_loop(0, N_TILES, place, zeros)

    start = lax.broadcasted_iota(jnp.int32, (nb, 128), 1).astype(F32) * tm
    bsub = lax.broadcasted_iota(jnp.int32, (nb, 128), 0)
    done = jnp.where((bsub < N_BUCKETS) & (ends <= start), 1.0, 0.0)
    tb = jnp.sum(done, axis=0, keepdims=True)
    valid = jnp.where(tb < N_BUCKETS, 1.0, 0.0)
    tbc = jnp.minimum(tb, N_BUCKETS - 1.0)
    grp = jnp.floor((tbc + 0.5) * (1.0 / PAIRS_PER_GROUP))
    pair = tbc - PAIRS_PER_GROUP * grp
    loc_a = jnp.zeros_like(pair)
    loc_b = jnp.zeros_like(pair)
    for k, (sa, sb) in enumerate(PAIR_SLOTS):
        loc_a = jnp.where(pair == k, float(sa), loc_a)
        loc_b = jnp.where(pair == k, float(sb), loc_b)
    mine = bsub.astype(F32) == tbc
    used = jnp.sum(jnp.where(mine, offs + cnt, 0.0), axis=0, keepdims=True)
    n_rows = jnp.clip(used - start[0:1], 0.0, tm) * valid
    row8 = lax.broadcasted_iota(jnp.int32, (8, 128), 0)
    meta = jnp.where(row8 == 0, grp * EXPERTS_PER_GROUP + loc_a,
                     jnp.where(row8 == 1, grp * EXPERTS_PER_GROUP + loc_b,
                               jnp.where(row8 == 2, valid, jnp.where(row8 == 3, n_rows, 0.0))))
    meta_ref[...] = meta.astype(jnp.int32)


def _route(bid):
    return pl.pallas_call(
        _route_kernel,
        out_shape=(jax.ShapeDtypeStruct((N_TILES, 1, TILE), jnp.int32),
                   jax.ShapeDtypeStruct((8, 128), jnp.int32)),
        compiler_params=pltpu.CompilerParams(vmem_limit_bytes=VMEM_LIMIT),
        name="moe_route",
    )(bid)


def _moe_kernel(meta_ref, pos_ref, h2_hbm, wga_ref, wua_ref, wda_ref, wgb_ref, wub_ref, wdb_ref,
                y_ref, src_ref, xbuf, sem, wga_s, wua_s, wda_s, wgb_s, wub_s, wdb_s):
    j = pl.program_id(0)

    def row_copy(tile, r, slot):
        tok = src_ref[tile * ROW_TILE + r]
        return pltpu.make_async_copy(h2_hbm.at[pl.ds(tok, 1), :], xbuf.at[slot, pl.ds(r, 1), :], sem.at[slot])

    def issue_rows(tile, slot, lo, hi):
        for r in range(lo, hi):
            row_copy(tile, r, slot).start()

    def wait(slot):
        pltpu.make_async_copy(h2_hbm.at[pl.ds(0, ROW_TILE), :], xbuf.at[slot], sem.at[slot]).wait()

    @pl.when(j == 0)
    def _():
        def fill(t, c):
            def body(r, c2):
                p = t * ROW_TILE + r
                src_ref[p] = p % T_ALL
                return c2
            n = meta_ref[3, t]
            lax.fori_loop(n, jnp.where(meta_ref[2, t] == 1, ROW_TILE, n), body, 0)
            return c
        lax.fori_loop(0, N_ROW_TILES, fill, 0)

        def invert(t, c):
            src_ref[pos_ref[t]] = t
            return c
        lax.fori_loop(0, T_ALL, invert, 0, unroll=8)

        @pl.when(meta_ref[2, 0] == 1)
        def _():
            def body(g, c):
                for k in range(8):
                    row_copy(0, g * 8 + k, 0).start()
                return c
            lax.fori_loop(0, ROW_TILE // 8, body, 0)

    nxt = jnp.minimum(j + 1, N_ROW_TILES - 1)
    has_next = (j + 1 < N_ROW_TILES) & (meta_ref[2, nxt] == 1)
    valid = meta_ref[2, j] == 1
    prev = jnp.maximum(j - 1, 0)

    @pl.when(valid & ((j == 0) | (meta_ref[0, j] != meta_ref[0, prev])))
    def _():
        wga_s[...] = wga_ref[0].astype(BF16)
        wua_s[...] = wua_ref[0].astype(BF16)
        wda_s[...] = wda_ref[0].astype(BF16)

    @pl.when(valid & ((j == 0) | (meta_ref[1, j] != meta_ref[1, prev])))
    def _():
        wgb_s[...] = wgb_ref[0].astype(BF16)
        wub_s[...] = wub_ref[0].astype(BF16)
        wdb_s[...] = wdb_ref[0].astype(BF16)

    def compute(fetch_next):
        slot = j % 2
        nslot = nxt % 2
        step = ROW_TILE // 8
        batches = iter(range(0, ROW_TILE, step))

        def fetch():
            if fetch_next:
                lo = next(batches)
                issue_rows(nxt, nslot, lo, lo + step)

        wait(slot)
        x = xbuf[slot, :, 0:D_MODEL].astype(BF16)
        gates = xbuf[slot, :, D_MODEL:H2_EXT]
        hg_a = _dot(x, wga_s[...])
        fetch()
        hu_a = _dot(x, wua_s[...])
        fetch()
        hg_b = _dot(x, wgb_s[...])
        fetch()
        hu_b = _dot(x, wub_s[...])
        fetch()
        act_a = (hg_a * jax.nn.sigmoid(hg_a) * hu_a * gates[:, 0:1]).astype(BF16)
        fetch()
        act_b = (hg_b * jax.nn.sigmoid(hg_b) * hu_b * gates[:, 1:2]).astype(BF16)
        fetch()
        y = _dot(act_a, wda_s[...])
        fetch()
        y = y + _dot(act_b, wdb_s[...])
        fetch()
        y_ref[...] = y

    @pl.when(valid & has_next)
    def _():
        compute(True)

    @pl.when(valid & jnp.logical_not(has_next))
    def _():
        compute(False)

    @pl.when(jnp.logical_not(valid))
    def _():
        y_ref[...] = jnp.zeros_like(y_ref)


def _moe(meta, pos, h2ext, w_gate, w_up, w_down):
    up_spec = lambda slot: pl.BlockSpec((1, D_MODEL, EXPERT_FF), lambda j, meta, pos: (meta[slot, j], 0, 0))
    down_spec = lambda slot: pl.BlockSpec((1, EXPERT_FF, D_MODEL), lambda j, meta, pos: (meta[slot, j], 0, 0))
    grid_spec = pltpu.PrefetchScalarGridSpec(
        num_scalar_prefetch=2,
        grid=(N_ROW_TILES,),
        in_specs=[pl.BlockSpec(memory_space=pl.ANY),
                  up_spec(0), up_spec(0), down_spec(0), up_spec(1), up_spec(1), down_spec(1)],
        out_specs=pl.BlockSpec((ROW_TILE, D_MODEL), lambda j, meta, pos: (j, 0)),
        scratch_shapes=[pltpu.SMEM((ROW_CAP,), jnp.int32),
                        pltpu.VMEM((2, ROW_TILE, H2_EXT), F32),
                        pltpu.SemaphoreType.DMA((2,)),
                        pltpu.VMEM((D_MODEL, EXPERT_FF), BF16), pltpu.VMEM((D_MODEL, EXPERT_FF), BF16),
                        pltpu.VMEM((EXPERT_FF, D_MODEL), BF16),
                        pltpu.VMEM((D_MODEL, EXPERT_FF), BF16), pltpu.VMEM((D_MODEL, EXPERT_FF), BF16),
                        pltpu.VMEM((EXPERT_FF, D_MODEL), BF16)])
    return pl.pallas_call(
        _moe_kernel,
        out_shape=jax.ShapeDtypeStruct((ROW_CAP, D_MODEL), F32),
        grid_spec=grid_spec,
        compiler_params=_cparams(("arbitrary",)),
        name="moe_experts",
    )(meta, pos, h2ext, w_gate, w_up, w_down, w_gate, w_up, w_down)


def _final_kernel(pos_ref, y_hbm, x1_ref, m_ref, gn_ref, op_ref, os_ref, ybuf, sem):
    i = pl.program_id(0)

    def row_copy(tile, r, slot):
        p = pos_ref[tile * TILE + r]
        return pltpu.make_async_copy(y_hbm.at[pl.ds(p, 1), :], ybuf.at[slot, pl.ds(r, 1), :], sem.at[slot])

    def issue(tile, slot):
        def body(r2, c):
            row_copy(tile, 2 * r2, slot).start(priority=0)
            row_copy(tile, 2 * r2 + 1, slot).start(priority=1)
            return c
        lax.fori_loop(0, TILE // 2, body, 0, unroll=4)

    def wait(slot):
        pltpu.make_async_copy(y_hbm.at[pl.ds(0, TILE)], ybuf.at[slot], sem.at[slot]).wait()

    @pl.when(i == 0)
    def _():
        issue(0, 0)

    @pl.when(i + 1 < N_TILES)
    def _():
        issue(i + 1, (i + 1) % 2)

    slot = i % 2
    wait(slot)
    out = x1_ref[...] + m_ref[0, 5:6, :] * _rms(ybuf[slot], gn_ref[3:4, :])

    @pl.when(i < N_TILES_P)
    def _():
        op_ref[...] = out

    @pl.when(i >= N_TILES_P)
    def _():
        os_ref[...] = out


def _final(pos, y_sorted, x1, mods3, g_norm):
    tps = DEC_SEQ // TILE
    grid_spec = pltpu.PrefetchScalarGridSpec(
        num_scalar_prefetch=1,
        grid=(N_TILES,),
        in_specs=[pl.BlockSpec(memory_space=pl.ANY),
                  pl.BlockSpec((TILE, D_MODEL), lambda i, pos: (i, 0)),
                  pl.BlockSpec((1, N_MOD, D_MODEL), lambda i, pos: (_mod_row_of_tile(i, tps, N_TILES_P), 0, 0)),
                  pl.BlockSpec((4, D_MODEL), lambda i, pos: (0, 0))],
        out_specs=(pl.BlockSpec((TILE, D_MODEL), lambda i, pos: (jnp.minimum(i, N_TILES_P - 1), 0)),
                   pl.BlockSpec((TILE, D_MODEL), lambda i, pos: (jnp.maximum(i - N_TILES_P, 0), 0))),
        scratch_shapes=[pltpu.VMEM((2, TILE, D_MODEL), F32), pltpu.SemaphoreType.DMA((2,))])
    return pl.pallas_call(
        _final_kernel,
        out_shape=(jax.ShapeDtypeStruct((T_PROMPT, D_MODEL), F32),
                   jax.ShapeDtypeStruct((T_SAMPLE, D_MODEL), F32)),
        grid_spec=grid_spec,
        compiler_params=_cparams(("arbitrary",)),
        name="moe_combine_final",
    )(pos, y_sorted, x1, mods3, g_norm)


def kernel(x_prompt, x_sample, state_C, state_n, state_m, c, c_ctx, w_ada, b_ada, g_norm, w_in, ml_gate_bias, ml_head_gain, hy_conv_w, hy_f_w1, hy_f_b1, hy_f_w2, hy_f_b2, hy_f_w3, hy_f_b3, hy_decay, hy_bias, w_out, w_rc, b_rc, w_rf, b_rf, w_gate, w_up, w_down):
    xp = x_prompt.reshape(T_PROMPT, D_MODEL)
    xs = x_sample.reshape(T_SAMPLE, D_MODEL)
    gn = g_norm[0]

    cv = jnp.concatenate([c_ctx[None, :], c, jnp.zeros((MOD_ROWS - 1 - DEC_BATCH, D_MODEL), F32)], axis=0)
    mods3 = _ada(cv, w_ada[0], b_ada[0]).reshape(MOD_ROWS, N_MOD, D_MODEL)

    w_in0 = w_in[0]
    w_qkvo, w_hy = _prep_in_weights(w_in0)
    wg = w_in0[:, ML_QKVO_COLS:ML_QKVO_COLS + ML_GATE_COLS]
    gbt = ml_gate_bias[0].reshape(ML_GATE_COLS, 1)
    proj, gates, gates_t = _inproj(xp, xs, mods3, gn, w_qkvo, w_hy, wg.T, gbt)

    gain = ml_head_gain[0].reshape(1, ML_WIDTH)
    y_ml_p, c_new, n_new, m_new = _mlstm(proj, gates, gates_t, gain, None, SEQ, BATCH, 0)
    state = (state_C[:, 0], state_n[:, 0], state_m[:, 0].reshape(DEC_BATCH, 2 * ML_HEADS, 1))
    y_ml_s, _, _, _ = _mlstm(proj, gates, gates_t, gain, state, DEC_SEQ, DEC_BATCH, T_PROMPT // DEC_SEQ)

    w1p = jnp.pad(hy_f_w1[0], ((0, 128 - HY_EMB), (0, 0)))
    b1 = hy_f_b1[0].reshape(1, -1)
    b2 = hy_f_b2[0].reshape(1, -1)
    b3 = hy_f_b3[0].reshape(1, -1)
    dec = hy_decay[0].reshape(1, -1)
    z_parts = []
    for seq_len, n_seq, row_off, width, seqs in ((SEQ, BATCH, 0, SEQ, 4), (DEC_SEQ, DEC_BATCH, T_PROMPT, GRID_W, 2)):
        f, ft = _dft_mats(seq_len)
        coefs = _hyena_filters(seq_len, f, w1p, b1, hy_f_w2[0], b2, hy_f_w3[0], b3, dec)
        z_parts.append(_hyena(proj, hy_conv_w[0], coefs, hy_bias[0], f, ft, seq_len, n_seq, row_off, width, seqs))
    z_p, z_s = z_parts

    pad_r = ROUTER_ROWS - N_GROUPS - N_EXPERTS
    w_r = jnp.pad(jnp.concatenate([w_rc[0], w_rf[0]], axis=1).T, ((0, pad_r), (0, 0)))
    b_r = jnp.pad(jnp.concatenate([b_rc[0], b_rf[0]], axis=0), (0, pad_r)).reshape(ROUTER_ROWS, 1)
    x1, h2ext, bid = _outproj(xp, xs, y_ml_p, y_ml_s, z_p, z_s, mods3, gn, w_out[0].astype(BF16), w_r, b_r)

    pos3, meta = _route(bid)
    pos = pos3.reshape(T_ALL)
    y_sorted = _moe(meta, pos, h2ext, w_gate[0], w_up[0], w_down[0])
    y_p, y_s = _final(pos, y_sorted, x1, mods3, gn)

    new_c = c_new.reshape(BATCH, 1, 2, ML_HEADS, ML_HEAD_DIM, ML_HEAD_DIM)
    new_n = n_new.reshape(BATCH, 1, 2, ML_HEADS, ML_HEAD_DIM)
    new_m = m_new[:, :, 0].reshape(BATCH, 1, 2, ML_HEADS)
    return (y_p.reshape(BATCH, SEQ, D_MODEL), y_s.reshape(DEC_BATCH, DEC_SEQ, D_MODEL), new_c, new_n, new_m)
```

```python
import functools
import math

import jax
import jax.numpy as jnp
import numpy as np
from jax import lax
from jax.experimental import pallas as pl
from jax.experimental.pallas import tpu as pltpu

F32 = jnp.float32
BF16 = jnp.bfloat16

D_MODEL = 1024
BATCH = 16
SEQ = 256
DEC_BATCH = 4
DEC_SEQ = 1024
GRID_W = 64
ML_WIDTH = 512
ML_HEADS = 4
ML_HEAD_DIM = 128
HY_WIDTH = 512
HY_ORDER = 2
HY_EMB = 33
HY_BANDS = 16
HY_FILTER_HIDDEN = 64
HY_MOD_SHIFT = 0.05
N_GROUPS = 4
EXPERTS_PER_GROUP = 4
N_EXPERTS = 16
EXPERT_FF = 512
N_MOD = 6
EPS = 1e-6
ML_QKVO_COLS = 4 * ML_WIDTH
ML_GATE_COLS = 4 * ML_HEADS
HY_COLS = 3 * HY_WIDTH
MAIN_COLS = ML_QKVO_COLS + HY_COLS

T_PROMPT = BATCH * SEQ
T_SAMPLE = DEC_BATCH * DEC_SEQ
T_ALL = T_PROMPT + T_SAMPLE
TILE = 256
N_TILES_P = T_PROMPT // TILE
N_TILES = T_ALL // TILE
MOD_ROWS = 8
K_SCALE = ML_HEAD_DIM ** -0.5
VMEM_LIMIT = 56 * 1024 * 1024


def _cparams(sem):
    return pltpu.CompilerParams(dimension_semantics=sem, vmem_limit_bytes=VMEM_LIMIT)


def _split2(x):
    hi = x.astype(BF16)
    lo = (x - hi.astype(F32)).astype(BF16)
    return hi, lo


def _dot(a, b):
    return jnp.dot(a, b, preferred_element_type=F32)


def _dot_nt(a, b):
    return lax.dot_general(a, b, (((1,), (1,)), ((), ())), preferred_element_type=F32)


def _dot_tn(a, b):
    return lax.dot_general(a, b, (((0,), (0,)), ((), ())), preferred_element_type=F32)


def _dot3(a, b):
    ah, al = _split2(a)
    bh, bl = _split2(b)
    return _dot(ah, bh) + _dot(al, bh) + _dot(ah, bl)


def _dot3_nt(a, b):
    ah, al = _split2(a)
    bh, bl = _split2(b)
    return _dot_nt(ah, bh) + _dot_nt(al, bh) + _dot_nt(ah, bl)


def _dot_exact_lhs(t, x):
    x1 = x.astype(BF16)
    r1 = x - x1.astype(F32)
    x2 = r1.astype(BF16)
    x3 = (r1 - x2.astype(F32)).astype(BF16)
    return _dot(t, x1) + _dot(t, x2) + _dot(t, x3)


def _dot_exact_rhs(x, t):
    x1 = x.astype(BF16)
    r1 = x - x1.astype(F32)
    x2 = r1.astype(BF16)
    x3 = (r1 - x2.astype(F32)).astype(BF16)
    return _dot(x1, t) + _dot(x2, t) + _dot(x3, t)


def _rms(x, g):
    return x * lax.rsqrt(jnp.mean(x * x, axis=-1, keepdims=True) + EPS) * g


def _mod_row_of_tile(i, tiles_per_sample_seq, n_prompt_tiles):
    return jnp.where(i < n_prompt_tiles, 0, 1 + (i - n_prompt_tiles) // tiles_per_sample_seq)


def _ada_kernel(cv_ref, w_ref, b_ref, o_ref):
    cv = cv_ref[...]
    s = cv * jax.nn.sigmoid(cv)
    o_ref[...] = _dot3(s, w_ref[...]) + b_ref[...]


def _ada(cv, w_ada, b_ada):
    n = N_MOD * D_MODEL
    return pl.pallas_call(
        _ada_kernel,
        out_shape=jax.ShapeDtypeStruct((MOD_ROWS, n), F32),
        grid=(N_MOD,),
        in_specs=[pl.BlockSpec((MOD_ROWS, D_MODEL), lambda j: (0, 0)),
                  pl.BlockSpec((D_MODEL, D_MODEL), lambda j: (0, j)),
                  pl.BlockSpec((1, D_MODEL), lambda j: (0, j))],
        out_specs=pl.BlockSpec((MOD_ROWS, D_MODEL), lambda j: (0, j)),
        compiler_params=_cparams(("arbitrary",)),
        name="ada_mod",
    )(cv, w_ada, b_ada.reshape(1, n))


PREP_COLS = 512


def _prep_q_kernel(wt_ref, o_ref):
    o_ref[...] = wt_ref[...].T.astype(BF16)


def _prep_hy_kernel(wt_hbm, o_ref, buf, sem):
    start = pl.multiple_of(ML_QKVO_COLS + ML_GATE_COLS + pl.program_id(0) * PREP_COLS, 8)
    copy = pltpu.make_async_copy(wt_hbm.at[pl.ds(start, PREP_COLS), :], buf, sem)
    copy.start()
    copy.wait()
    o_ref[...] = buf[...].T.astype(BF16)


def _prep_in_weights(w_in_t):
    out_blk = pl.BlockSpec((D_MODEL, PREP_COLS), lambda j: (0, j))
    w_qkvo = pl.pallas_call(
        _prep_q_kernel,
        out_shape=jax.ShapeDtypeStruct((D_MODEL, ML_QKVO_COLS), BF16),
        grid=(ML_QKVO_COLS // PREP_COLS,),
        in_specs=[pl.BlockSpec((PREP_COLS, D_MODEL), lambda j: (j, 0))], out_specs=out_blk,
        compiler_params=_cparams(("arbitrary",)), name="prep_w_qkvo",
    )(w_in_t)
    w_hy = pl.pallas_call(
        _prep_hy_kernel,
        out_shape=jax.ShapeDtypeStruct((D_MODEL, HY_COLS), BF16),
        grid=(HY_COLS // PREP_COLS,),
        in_specs=[pl.BlockSpec(memory_space=pl.ANY)], out_specs=out_blk,
        scratch_shapes=[pltpu.VMEM((PREP_COLS, D_MODEL), F32), pltpu.SemaphoreType.DMA(())],
        compiler_params=_cparams(("arbitrary",)), name="prep_w_hy",
    )(w_in_t)
    return w_qkvo, w_hy


def _log_sigmoid(x):
    return jnp.minimum(x, 0.0) - jnp.log1p(jnp.exp(-jnp.abs(x)))


def _rows_to_cols(rows):
    ri = lax.broadcasted_iota(jnp.int32, (TILE, TILE), 0)
    ci = lax.broadcasted_iota(jnp.int32, (TILE, TILE), 1)
    eye = jnp.where(ri == ci, 1.0, 0.0).astype(BF16)
    p1 = rows.astype(BF16)
    r1 = rows - p1.astype(F32)
    p2 = r1.astype(BF16)
    p3 = (r1 - p2.astype(F32)).astype(BF16)
    return _dot_nt(eye, p1) + _dot_nt(eye, p2) + _dot_nt(eye, p3)


BIG_TILE = 2 * TILE
N_BIG_P = T_PROMPT // BIG_TILE
N_BIG = T_ALL // BIG_TILE


def _inproj_kernel(xp_ref, xs_ref, m_ref, gn_ref, wq_ref, wh_ref, wgt_ref, gbt_ref, proj_ref, gate_ref, gatet_ref):
    is_p = pl.program_id(0) < N_BIG_P
    halves = [slice(r * TILE, (r + 1) * TILE) for r in range(BIG_TILE // TILE)]
    hs = [_rms(jnp.where(is_p, xp_ref[rows, :], xs_ref[rows, :]), gn_ref[0:1, :]) * (1.0 + m_ref[0, 1:2, :])
          + m_ref[0, 0:1, :] for rows in halves]
    hbs = [h.astype(BF16) for h in hs]
    cb = 512
    for j in range(ML_QKVO_COLS // cb):
        for rows, hb in zip(halves, hbs):
            proj_ref[rows, j * cb:(j + 1) * cb] = _dot(hb, wq_ref[:, j * cb:(j + 1) * cb]).astype(BF16)
    for j in range(HY_COLS // cb):
        lo = ML_QKVO_COLS + j * cb
        for rows, hb in zip(halves, hbs):
            proj_ref[rows, lo:lo + cb] = _dot(hb, wh_ref[:, j * cb:(j + 1) * cb]).astype(BF16)
    wth, wtl = _split2(wgt_ref[...])
    gts = []
    for h, hb in zip(hs, hbs):
        hl = (h - hb.astype(F32)).astype(BF16)
        gt = _dot_nt(wth, hb) + _dot_nt(wth, hl) + _dot_nt(wtl, hb) + gbt_ref[...]
        row = lax.broadcasted_iota(jnp.int32, gt.shape, 0)
        gts.append(jnp.where((row % 8) >= 4, _log_sigmoid(gt), gt))
    for r, (rows, gt) in enumerate(zip(halves, gts)):
        gatet_ref[r] = gt
        gate_ref[rows, :] = _rows_to_cols(gt)


def _inproj(xp, xs, mods3, g_norm, w_qkvo, w_hy, wgt, gbt):
    tps = DEC_SEQ // BIG_TILE
    per = BIG_TILE // TILE
    return pl.pallas_call(
        _inproj_kernel,
        out_shape=(jax.ShapeDtypeStruct((T_ALL, MAIN_COLS), BF16),
                   jax.ShapeDtypeStruct((T_ALL, ML_GATE_COLS), F32),
                   jax.ShapeDtypeStruct((N_TILES, ML_GATE_COLS, TILE), F32)),
        grid=(N_BIG,),
        in_specs=[pl.BlockSpec((BIG_TILE, D_MODEL), lambda i: (jnp.minimum(i, N_BIG_P - 1), 0)),
                  pl.BlockSpec((BIG_TILE, D_MODEL), lambda i: (jnp.maximum(i - N_BIG_P, 0), 0)),
                  pl.BlockSpec((1, N_MOD, D_MODEL), lambda i: (_mod_row_of_tile(i, tps, N_BIG_P), 0, 0)),
                  pl.BlockSpec((4, D_MODEL), lambda i: (0, 0)),
                  pl.BlockSpec((D_MODEL, ML_QKVO_COLS), lambda i: (0, 0)),
                  pl.BlockSpec((D_MODEL, HY_COLS), lambda i: (0, 0)),
                  pl.BlockSpec((ML_GATE_COLS, D_MODEL), lambda i: (0, 0)),
                  pl.BlockSpec((ML_GATE_COLS, 1), lambda i: (0, 0))],
        out_specs=(pl.BlockSpec((BIG_TILE, MAIN_COLS), lambda i: (i, 0)),
                   pl.BlockSpec((BIG_TILE, ML_GATE_COLS), lambda i: (i, 0)),
                   pl.BlockSpec((per, ML_GATE_COLS, TILE), lambda i: (i, 0, 0))),
        compiler_params=_cparams(("arbitrary",)),
        name="in_proj",
    )(xp, xs, mods3, g_norm, w_qkvo, w_hy, wgt, gbt)


ST_ROWS = ML_HEAD_DIM + 16


def _mlstm_kernel(*refs, seq_len, has_state):
    if has_state:
        (q_ref, k_ref, v_ref, o_ref, g_ref, gt_ref, gain_ref, c0_ref, n0_ref, m0_ref,
         y_ref, c_ref, n_ref, m_ref, vt_ref, hf_ref, hb_ref, st_ref, ms_ref) = refs
    else:
        (q_ref, k_ref, v_ref, o_ref, g_ref, gt_ref, gain_ref,
         y_ref, c_ref, n_ref, m_ref, vt_ref, hf_ref, hb_ref, st_ref, ms_ref) = refs
    ch = TILE
    nc = seq_len // ch
    hd = ML_HEAD_DIM
    key = lax.broadcasted_iota(jnp.int32, (ch, ch), 0)
    qry = lax.broadcasted_iota(jnp.int32, (ch, ch), 1)
    key_le = key <= qry
    key_ge = key >= qry
    t_le = jnp.where(key_le, 1.0, 0.0).astype(BF16)
    t_ge = jnp.where(key_ge, 1.0, 0.0).astype(BF16)
    sub16 = lax.broadcasted_iota(jnp.int32, (16, ch), 0)
    ln_scale = math.log(K_SCALE)

    for c in range(nc):
        for h in range(ML_HEADS):
            cols = slice(h * hd, (h + 1) * hd)
            vt_ref[c, cols, :] = v_ref[c * ch:(c + 1) * ch, cols].astype(F32).T.astype(BF16)

    for d in range(2):
        for h in range(ML_HEADS):
            r = d * ML_HEADS + h
            st_ref[r] = jnp.zeros((ST_ROWS, hd), F32)
            if has_state:
                st_ref[r, 0:hd, :] = c0_ref[0, d, h].T
                st_ref[r, hd:hd + 1, :] = n0_ref[0, d, h:h + 1, :]
                ms_ref[r] = jnp.broadcast_to(m0_ref[0, r:r + 1, :], (1, ch))
            else:
                ms_ref[r] = jnp.zeros((1, ch), F32)

    def step(t, carry):
        for d in range(2):
            c = t if d == 0 else nc - 1 - t
            rows = pl.ds(pl.multiple_of(c * ch, ch), ch)
            gcol = g_ref[rows, :]
            grow = gt_ref[c]
            brow_all = _dot_exact_rhs(grow, t_le if d == 0 else t_ge)
            bcol_all = _dot_exact_lhs(t_ge if d == 0 else t_le, gcol)
            mask = key_le if d == 0 else key_ge
            hacc_ref = hf_ref if d == 0 else hb_ref
            heads = range(ML_HEADS)
            regs = [d * ML_HEADS + h for h in heads]
            colss = [slice(h * hd, (h + 1) * hd) for h in heads]
            qs = [q_ref[rows, cols] for cols in colss]
            ks = [k_ref[rows, cols] for cols in colss]
            vts = [vt_ref[c, cols, :] for cols in colss]
            sts = [st_ref[r] for r in regs]
            m_prevs = [ms_ref[r] for r in regs]
            b_rows = [brow_all[(1 + 2 * d) * ML_HEADS + h:(1 + 2 * d) * ML_HEADS + h + 1, :] for h in heads]
            ig_rows = [grow[2 * d * ML_HEADS + h:2 * d * ML_HEADS + h + 1, :] for h in heads]
            qks = [_dot_nt(k, q) for k, q in zip(ks, qs)]
            iqs = [_dot_nt(st.astype(BF16), q) for st, q in zip(sts, qs)]
            ss, sc_inters, m_poss = [], [], []
            for h in heads:
                fcol = (1 + 2 * d) * ML_HEADS + h
                icol = 2 * d * ML_HEADS + h
                c_col = gcol[:, icol:icol + 1] - bcol_all[:, fcol:fcol + 1]
                logd = jnp.where(mask, b_rows[h] + c_col, -jnp.inf)
                inter = b_rows[h] + m_prevs[h]
                m_pos = jnp.maximum(inter, jnp.max(logd, axis=0, keepdims=True))
                ss.append(qks[h] * jnp.exp(logd - (m_pos - ln_scale)))
                sc_inters.append(jnp.exp(inter - m_pos))
                m_poss.append(m_pos)
            pvs = [_dot(vt, s.astype(BF16)) for vt, s in zip(vts, ss)]
            for h in heads:
                num = sc_inters[h] * iqs[h][0:hd] + pvs[h]
                den = sc_inters[h] * iqs[h][hd:hd + 1] + jnp.sum(ss[h], axis=0, keepdims=True)
                hacc_ref[c, colss[h], :] = num * (1.0 / jnp.maximum(jnp.abs(den), jnp.exp(-m_poss[h])))
            lhss, decays = [], []
            for h in heads:
                b_row = b_rows[h]
                b_last = b_row[:, ch - 1:ch] if d == 0 else b_row[:, 0:1]
                logw = b_last - b_row + ig_rows[h]
                m_new = jnp.maximum(b_last + m_prevs[h], jnp.max(logw, axis=1, keepdims=True))
                w = jnp.exp(logw - (m_new - ln_scale))
                decays.append(jnp.exp(b_last + m_prevs[h] - m_new))
                lhss.append(jnp.concatenate([(vts[h].astype(F32) * w).astype(BF16),
                                             jnp.where(sub16 == 0, w, 0.0).astype(BF16)], axis=0))
                ms_ref[regs[h]] = m_new
            upds = [_dot(lhs, k) for lhs, k in zip(lhss, ks)]
            for h in heads:
                st_ref[regs[h]] = decays[h][:, 0:hd] * sts[h] + upds[h]
        return carry

    lax.fori_loop(0, nc, step, 0)

    for d in range(2):
        for h in range(ML_HEADS):
            r = d * ML_HEADS + h
            c_ref[0, d, h] = st_ref[r, 0:hd, :].T
            n_ref[0, d, h:h + 1, :] = st_ref[r, hd:hd + 1, :]
            m_ref[0, r:r + 1, :] = ms_ref[r][:, 0:hd]
    for c in range(nc):
        for h in range(ML_HEADS):
            cols = slice(h * hd, (h + 1) * hd)
            ht = hf_ref[c, cols, :] + hb_ref[c, cols, :]
            ht = ht * lax.rsqrt(jnp.mean(ht * ht, axis=0, keepdims=True) + EPS)
            rows = slice(c * ch, (c + 1) * ch)
            y = ht.T * gain_ref[:, cols] * jax.nn.sigmoid(o_ref[rows, cols].astype(F32))
            y_ref[rows, cols] = y.astype(BF16)


def _mlstm(proj, gates, gates_t, gain, state, seq_len, n_seq, row_block_off):
    has_state = state is not None
    tiles = seq_len // TILE
    off = row_block_off
    qkvo_specs = [pl.BlockSpec((seq_len, ML_WIDTH), functools.partial(lambda b, j: (off + b, j), j=j))
                  for j in range(4)]
    in_specs = qkvo_specs + [
        pl.BlockSpec((seq_len, ML_GATE_COLS), lambda b: (off + b, 0)),
        pl.BlockSpec((tiles, ML_GATE_COLS, TILE), lambda b: (off + b, 0, 0)),
        pl.BlockSpec((1, ML_WIDTH), lambda b: (0, 0)),
    ]
    args = [proj, proj, proj, proj, gates, gates_t, gain]
    if has_state:
        c0, n0, m0 = state
        in_specs += [
            pl.BlockSpec((1, 2, ML_HEADS, ML_HEAD_DIM, ML_HEAD_DIM), lambda b: (b, 0, 0, 0, 0)),
            pl.BlockSpec((1, 2, ML_HEADS, ML_HEAD_DIM), lambda b: (b, 0, 0, 0)),
            pl.BlockSpec((1, 2 * ML_HEADS, 1), lambda b: (b, 0, 0)),
        ]
        args += [c0, n0, m0]
    out_shape = (jax.ShapeDtypeStruct((n_seq * seq_len, ML_WIDTH), BF16),
                 jax.ShapeDtypeStruct((n_seq, 2, ML_HEADS, ML_HEAD_DIM, ML_HEAD_DIM), F32),
                 jax.ShapeDtypeStruct((n_seq, 2, ML_HEADS, ML_HEAD_DIM), F32),
                 jax.ShapeDtypeStruct((n_seq, 2 * ML_HEADS, ML_HEAD_DIM), F32))
    out_specs = (pl.BlockSpec((seq_len, ML_WIDTH), lambda b: (b, 0)),
                 pl.BlockSpec((1, 2, ML_HEADS, ML_HEAD_DIM, ML_HEAD_DIM), lambda b: (b, 0, 0, 0, 0)),
                 pl.BlockSpec((1, 2, ML_HEADS, ML_HEAD_DIM), lambda b: (b, 0, 0, 0)),
                 pl.BlockSpec((1, 2 * ML_HEADS, ML_HEAD_DIM), lambda b: (b, 0, 0)))
    scratch = [pltpu.VMEM((tiles, ML_WIDTH, TILE), BF16),
               pltpu.VMEM((tiles, ML_WIDTH, TILE), F32), pltpu.VMEM((tiles, ML_WIDTH, TILE), F32),
               pltpu.VMEM((2 * ML_HEADS, ST_ROWS, ML_HEAD_DIM), F32),
               pltpu.VMEM((2 * ML_HEADS, 1, TILE), F32)]
    return pl.pallas_call(
        functools.partial(_mlstm_kernel, seq_len=seq_len, has_state=has_state),
        out_shape=out_shape, grid=(n_seq,), in_specs=in_specs, out_specs=out_specs,
        scratch_shapes=scratch, compiler_params=_cparams(("arbitrary",)),
        name=f"mlstm_{seq_len}",
    )(*args)


def _dft_mats(seq_len):
    k = np.arange(seq_len, dtype=np.int64)[:, None]
    d = np.arange(seq_len, dtype=np.int64)[None, :]
    ang = np.pi * ((k * d) % (2 * seq_len)).astype(np.float64) / seq_len
    sinm = np.sin(ang)
    sinm[0, :] = np.where(d[0] % 2 == 0, 1.0, -1.0)
    f = np.concatenate([np.cos(ang), sinm], axis=0).astype(np.float32)
    return jnp.asarray(f).astype(BF16), jnp.asarray(np.ascontiguousarray(f.T)).astype(BF16)


def _filter_feats(seq_len):
    t = np.linspace(0.0, 1.0, seq_len, dtype=np.float64)[:, None]
    wpos = 2.0 * np.pi * np.arange(seq_len, dtype=np.float64)[:, None] / seq_len
    bands = np.linspace(1e-4, HY_BANDS - 1, HY_BANDS, dtype=np.float64)[None, :]
    z = np.concatenate([t, np.cos(bands * wpos), -np.sin(bands * wpos)], axis=-1)
    return jnp.asarray(np.pad(z, ((0, 0), (0, 128 - HY_EMB))).astype(np.float32))


def _filter_kernel(z_ref, w1_ref, b1_ref, w2_ref, b2_ref, w3_ref, b3_ref, dec_ref, f_ref,
                   a_ref, b_ref, d_ref, *, seq_len):
    n = 2 * seq_len
    oc = 2 * HY_WIDTH
    z = z_ref[...]
    h = jnp.sin(_dot3(z, w1_ref[...]) + b1_ref[...])
    h = jnp.sin(_dot3(h, w2_ref[...]) + b2_ref[...])
    t = z[:, 0:1]
    di = lax.broadcasted_iota(jnp.int32, (seq_len, 1), 0)
    sgn = jnp.where(di % 2 == 0, 1.0, -1.0)
    first = di == 0
    ssums, sdifs = [], []
    for o in range(HY_ORDER):
        cols = slice(o * oc, (o + 1) * oc)
        g = _dot3(h, w3_ref[:, cols]) + b3_ref[:, cols]
        g = g * (jnp.exp(-t * jnp.abs(dec_ref[:, cols])) + HY_MOD_SHIFT)
        ss = jnp.sum(g * g, axis=0, keepdims=True)
        inv = lax.rsqrt(ss[:, :HY_WIDTH] + ss[:, HY_WIDTH:] + EPS)
        hp = g[:, :HY_WIDTH] * inv
        hn = g[:, HY_WIDTH:] * inv
        ssums.append(hp + hn)
        sdifs.append(hp - hn)
    hcs = [_dot(f_ref[0:seq_len, :], s.astype(BF16)) for s in ssums]
    hss = [_dot(f_ref[seq_len:n, :], s.astype(BF16)) for s in sdifs]
    for o in range(HY_ORDER):
        nyq = jnp.sum(ssums[o] * sgn, axis=0, keepdims=True)
        a_ref[o] = hcs[o] * jnp.where(first, 1.0 / n, 2.0 / n)
        b_ref[o] = jnp.where(first, 0.0, hss[o] * (2.0 / n))
        d_ref[o] = jnp.where(first, nyq * (1.0 / n), hcs[o] * (2.0 / n))


def _hyena_filters(seq_len, f, w1p, b1, w2, b2, w3, b3, dec):
    z = _filter_feats(seq_len)
    out = jax.ShapeDtypeStruct((HY_ORDER, seq_len, HY_WIDTH), F32)
    return pl.pallas_call(
        functools.partial(_filter_kernel, seq_len=seq_len),
        out_shape=(out, out, out),
        compiler_params=pltpu.CompilerParams(vmem_limit_bytes=VMEM_LIMIT),
        name=f"hyena_filter_{seq_len}",
    )(z, w1p, b1, w2, b2, w3, b3, dec, f)


def _hyena_kernel(x1_ref, x2_ref, v_ref, cw1_ref, cw2_ref, cwv_ref, a_ref, b_ref, d_ref, bias_ref,
                  f_ref, ft_ref, z_ref, *, seq_len, width, seqs):
    rows = seqs * seq_len
    ti = lax.broadcasted_iota(jnp.int32, (rows, 1), 0)
    has_prev = (ti % width) != 0
    has_next = (ti % width) != (width - 1)

    def short_conv(x_ref, w_ref):
        x = x_ref[...].astype(F32)
        prev = jnp.where(has_prev, pltpu.roll(x, 1, axis=0), 0.0)
        nxt = jnp.where(has_next, pltpu.roll(x, rows - 1, axis=0), 0.0)
        return w_ref[0:1, :] * prev + w_ref[1:2, :] * x + w_ref[2:3, :] * nxt

    gates = (short_conv(x1_ref, cw1_ref), short_conv(x2_ref, cw2_ref))
    v = short_conv(v_ref, cwv_ref)
    sls = [slice(i * seq_len, (i + 1) * seq_len) for i in range(seqs)]
    zs = [v[sl] for sl in sls]
    for o in range(HY_ORDER):
        a, b, dd = a_ref[o], b_ref[o], d_ref[o]
        us = [_dot(f_ref[...], z.astype(BF16)) for z in zs]
        ys = []
        for u in us:
            ut = u[:seq_len]
            ub = u[seq_len:]
            ys.append(((ut * a - ub * b).astype(BF16), (ut * b + ub * dd).astype(BF16)))
        convs = [_dot(ft_ref[:, :seq_len], yt) + _dot(ft_ref[:, seq_len:], yb) for yt, yb in ys]
        zs = [gates[o][sl] * (y + bias_ref[o:o + 1, :] * z) for sl, y, z in zip(sls, convs, zs)]
    for sl, z in zip(sls, zs):
        z_ref[sl, :] = z.astype(BF16)


def _hyena(proj, conv_w, coefs, hy_bias, f, ft, seq_len, n_seq, row_off, width, seqs):
    cb = 256
    nblk = HY_WIDTH // cb
    base = ML_QKVO_COLS // cb
    rows = seqs * seq_len
    off = row_off // rows
    a, b, d = coefs

    def col_spec(part):
        return pl.BlockSpec((rows, cb), lambda j, s: (off + s, base + part * nblk + j))

    def w_spec(part):
        return pl.BlockSpec((3, cb), lambda j, s: (0, part * nblk + j))

    coef_spec = pl.BlockSpec((HY_ORDER, seq_len, cb), lambda j, s: (0, 0, j))
    return pl.pallas_call(
        functools.partial(_hyena_kernel, seq_len=seq_len, width=width, seqs=seqs),
        out_shape=jax.ShapeDtypeStruct((n_seq * seq_len, HY_WIDTH), BF16),
        grid=(nblk, n_seq // seqs),
        in_specs=[col_spec(0), col_spec(1), col_spec(2), w_spec(0), w_spec(1), w_spec(2),
                  coef_spec, coef_spec, coef_spec,
                  pl.BlockSpec((HY_ORDER, cb), lambda j, s: (0, j)),
                  pl.BlockSpec((2 * seq_len, seq_len), lambda j, s: (0, 0)),
                  pl.BlockSpec((seq_len, 2 * seq_len), lambda j, s: (0, 0))],
        out_specs=pl.BlockSpec((rows, cb), lambda j, s: (s, j)),
        compiler_params=_cparams(("arbitrary", "arbitrary")),
        name=f"hyena_conv_{seq_len}",
    )(proj, proj, proj, conv_w, conv_w, conv_w, a, b, d, hy_bias, f, ft)


def _first_max(x, n):
    mx = jnp.max(x, axis=0, keepdims=True)
    row = lax.broadcasted_iota(jnp.int32, x.shape, 0).astype(F32)
    idx = jnp.min(jnp.where(x == mx, row, float(n)), axis=0, keepdims=True)
    return mx, idx.astype(jnp.int32)


ROUTER_ROWS = 32
PAIRS_PER_GROUP = 6
N_BUCKETS = N_GROUPS * PAIRS_PER_GROUP
PAIR_SLOTS = ((0, 1), (0, 2), (0, 3), (1, 3), (1, 2), (3, 2))
LANES = 128
H2_EXT = D_MODEL + LANES
ROW_TILE = 256
ROW_CAP = T_ALL + N_BUCKETS * ROW_TILE
N_ROW_TILES = ROW_CAP // ROW_TILE


def _outproj_kernel(xp_ref, xs_ref, yp_ref, ys_ref, zp_ref, zs_ref, m_ref, gn_ref, wo_ref, wr_ref, br_ref,
                    x1_ref, h2_ref, bid_ref):
    is_p = pl.program_id(0) < N_BIG_P
    wrh, wrl = _split2(wr_ref[...])
    halves = [slice(r * TILE, (r + 1) * TILE) for r in range(BIG_TILE // TILE)]
    ys = [_dot(jnp.where(is_p, yp_ref[rows, :], ys_ref[rows, :]), wo_ref[0:ML_WIDTH, :])
          + _dot(jnp.where(is_p, zp_ref[rows, :], zs_ref[rows, :]), wo_ref[ML_WIDTH:, :]) for rows in halves]
    h2s = []
    for rows, y in zip(halves, ys):
        x = jnp.where(is_p, xp_ref[rows, :], xs_ref[rows, :])
        x1 = x + m_ref[0, 2:3, :] * _rms(y, gn_ref[1:2, :])
        x1_ref[rows, :] = x1
        h2 = _rms(x1, gn_ref[2:3, :]) * (1.0 + m_ref[0, 4:5, :]) + m_ref[0, 3:4, :]
        h2_ref[rows, 0:D_MODEL] = h2
        h2s.append(h2)
    logits = []
    for h2 in h2s:
        h2h, h2l = _split2(h2)
        logits.append(_dot_nt(wrh, h2h) + _dot_nt(wrh, h2l) + _dot_nt(wrl, h2h) + br_ref[...])
    routed = [_route_tile(lg) for lg in logits]
    for r, (rows, (gate_rows, bucket)) in enumerate(zip(halves, routed)):
        h2_ref[rows, D_MODEL:H2_EXT] = jnp.zeros((TILE, LANES), F32)
        h2_ref[rows, D_MODEL:D_MODEL + 8] = _rows_to_cols(gate_rows)
        bid_ref[r] = bucket


def _route_tile(logits):
    lc = logits[0:N_GROUPS]
    mx, gi = _first_max(lc, N_GROUPS)
    p_grp = 1.0 / jnp.sum(jnp.exp(lc - mx), axis=0, keepdims=True)
    lsel = jnp.zeros((EXPERTS_PER_GROUP, TILE), F32)
    for g in range(N_GROUPS):
        lo = N_GROUPS + g * EXPERTS_PER_GROUP
        lsel = jnp.where(gi == g, logits[lo:lo + EXPERTS_PER_GROUP], lsel)
    l1, i1 = _first_max(lsel, EXPERTS_PER_GROUP)
    sub4 = lax.broadcasted_iota(jnp.int32, lsel.shape, 0)
    l2, i2 = _first_max(jnp.where(sub4 == i1, -jnp.inf, lsel), EXPERTS_PER_GROUP)
    e2 = jnp.exp(l2 - l1)
    w1 = p_grp / (1.0 + e2)
    w2 = p_grp * e2 / (1.0 + e2)
    lo_e = jnp.minimum(i1, i2)
    hi_e = jnp.maximum(i1, i2)
    pair = jnp.where(lo_e == 0, hi_e - 1, jnp.where(lo_e == 1, jnp.where(hi_e == 3, 3, 4), 5))
    slot_a = jnp.where(pair == 5, hi_e, lo_e)
    first_in_a = i1 == slot_a
    w_a = jnp.where(first_in_a, w1, w2)
    w_b = jnp.where(first_in_a, w2, w1)
    sub = lax.broadcasted_iota(jnp.int32, (8, TILE), 0)
    gate_rows = jnp.where(sub == 0, w_a, jnp.where(sub == 1, w_b, 0.0))
    return gate_rows, gi * PAIRS_PER_GROUP + pair


def _outproj(xp, xs, yp, ys, zp, zs, mods3, g_norm, w_out, w_r, b_r):
    tps = DEC_SEQ // BIG_TILE
    per = BIG_TILE // TILE
    pidx = lambda i: (jnp.minimum(i, N_BIG_P - 1), 0)
    sidx = lambda i: (jnp.maximum(i - N_BIG_P, 0), 0)
    return pl.pallas_call(
        _outproj_kernel,
        out_shape=(jax.ShapeDtypeStruct((T_ALL, D_MODEL), F32),
                   jax.ShapeDtypeStruct((T_ALL, H2_EXT), F32),
                   jax.ShapeDtypeStruct((N_TILES, 1, TILE), jnp.int32)),
        grid=(N_BIG,),
        in_specs=[pl.BlockSpec((BIG_TILE, D_MODEL), pidx), pl.BlockSpec((BIG_TILE, D_MODEL), sidx),
                  pl.BlockSpec((BIG_TILE, ML_WIDTH), pidx), pl.BlockSpec((BIG_TILE, ML_WIDTH), sidx),
                  pl.BlockSpec((BIG_TILE, HY_WIDTH), pidx), pl.BlockSpec((BIG_TILE, HY_WIDTH), sidx),
                  pl.BlockSpec((1, N_MOD, D_MODEL), lambda i: (_mod_row_of_tile(i, tps, N_BIG_P), 0, 0)),
                  pl.BlockSpec((4, D_MODEL), lambda i: (0, 0)),
                  pl.BlockSpec((D_MODEL, D_MODEL), lambda i: (0, 0)),
                  pl.BlockSpec((ROUTER_ROWS, D_MODEL), lambda i: (0, 0)),
                  pl.BlockSpec((ROUTER_ROWS, 1), lambda i: (0, 0))],
        out_specs=(pl.BlockSpec((BIG_TILE, D_MODEL), lambda i: (i, 0)),
                   pl.BlockSpec((BIG_TILE, H2_EXT), lambda i: (i, 0)),
                   pl.BlockSpec((per, 1, TILE), lambda i: (i, 0, 0))),
        compiler_params=_cparams(("arbitrary",)),
        name="out_proj_router",
    )(xp, xs, yp, ys, zp, zs, mods3, g_norm, w_out, w_r, b_r)


def _route_kernel(bid_ref, pos_ref, meta_ref):
    nb = 32
    tm = float(ROW_TILE)
    sub = lax.broadcasted_iota(jnp.int32, (nb, TILE), 0)
    ri = lax.broadcasted_iota(jnp.int32, (TILE, TILE), 0)
    ci = lax.broadcasted_iota(jnp.int32, (TILE, TILE), 1)
    before = jnp.where(ri < ci, 1.0, 0.0).astype(BF16)

    def onehot(blk):
        return jnp.where(sub == bid_ref[blk], 1.0, 0.0)

    zeros = jnp.zeros((nb, 1), F32)
    cnt = lax.fori_loop(0, N_TILES, lambda blk, c: c + jnp.sum(onehot(blk), axis=1, keepdims=True), zeros)
    padded = jnp.floor((cnt + (tm - 1.0)) * (1.0 / tm)) * tm
    r32 = lax.broadcasted_iota(jnp.int32, (nb, nb), 0)
    c32 = lax.broadcasted_iota(jnp.int32, (nb, nb), 1)
    padded_row = jnp.sum(jnp.where(r32 == c32, padded, 0.0), axis=0, keepdims=True)
    offs = jnp.sum(jnp.where(c32 < r32, padded_row, 0.0), axis=1, keepdims=True)
    ends = offs + padded

    def place(blk, seen):
        oh = onehot(blk)
        rank = _dot(oh.astype(BF16), before)
        pos = jnp.sum(oh * (rank + seen + offs), axis=0, keepdims=True)
        pos_ref[blk] = pos.astype(jnp.int32)
        return seen + jnp.sum(oh, axis=1, keepdims=True)

    lax.fori_loop(0, N_TILES, place, zeros)

    start = lax.broadcasted_iota(jnp.int32, (nb, 128), 1).astype(F32) * tm
    bsub = lax.broadcasted_iota(jnp.int32, (nb, 128), 0)
    done = jnp.where((bsub < N_BUCKETS) & (ends <= start), 1.0, 0.0)
    tb = jnp.sum(done, axis=0, keepdims=True)
    valid = jnp.where(tb < N_BUCKETS, 1.0, 0.0)
    tbc = jnp.minimum(tb, N_BUCKETS - 1.0)
    grp = jnp.floor((tbc + 0.5) * (1.0 / PAIRS_PER_GROUP))
    pair = tbc - PAIRS_PER_GROUP * grp
    loc_a = jnp.zeros_like(pair)
    loc_b = jnp.zeros_like(pair)
    for k, (sa, sb) in enumerate(PAIR_SLOTS):
        loc_a = jnp.where(pair == k, float(sa), loc_a)
        loc_b = jnp.where(pair == k, float(sb), loc_b)
    mine = bsub.astype(F32) == tbc
    used = jnp.sum(jnp.where(mine, offs + cnt, 0.0), axis=0, keepdims=True)
    n_rows = jnp.clip(used - start[0:1], 0.0, tm) * valid
    row8 = lax.broadcasted_iota(jnp.int32, (8, 128), 0)
    meta = jnp.where(row8 == 0, grp * EXPERTS_PER_GROUP + loc_a,
                     jnp.where(row8 == 1, grp * EXPERTS_PER_GROUP + loc_b,
                               jnp.where(row8 == 2, valid, jnp.where(row8 == 3, n_rows, 0.0))))
    meta_ref[...] = meta.astype(jnp.int32)


def _route(bid):
    return pl.pallas_call(
        _route_kernel,
        out_shape=(jax.ShapeDtypeStruct((N_TILES, 1, TILE), jnp.int32),
                   jax.ShapeDtypeStruct((8, 128), jnp.int32)),
        compiler_params=pltpu.CompilerParams(vmem_limit_bytes=VMEM_LIMIT),
        name="moe_route",
    )(bid)


def _moe_kernel(meta_ref, pos_ref, h2_hbm, wga_ref, wua_ref, wda_ref, wgb_ref, wub_ref, wdb_ref,
                y_ref, src_ref, xbuf, sem, wga_s, wua_s, wda_s, wgb_s, wub_s, wdb_s):
    j = pl.program_id(0)

    def row_copy(tile, r, slot):
        tok = src_ref[tile * ROW_TILE + r]
        return pltpu.make_async_copy(h2_hbm.at[pl.ds(tok, 1), :], xbuf.at[slot, pl.ds(r, 1), :], sem.at[slot])

    def issue_rows(tile, slot, lo, hi):
        for r in range(lo, hi):
            row_copy(tile, r, slot).start()

    def wait(slot):
        pltpu.make_async_copy(h2_hbm.at[pl.ds(0, ROW_TILE), :], xbuf.at[slot], sem.at[slot]).wait()

    @pl.when(j == 0)
    def _():
        def fill(t, c):
            def body(r, c2):
                p = t * ROW_TILE + r
                src_ref[p] = p % T_ALL
                return c2
            n = meta_ref[3, t]
            lax.fori_loop(n, jnp.where(meta_ref[2, t] == 1, ROW_TILE, n), body, 0)
            return c
        lax.fori_loop(0, N_ROW_TILES, fill, 0)

        def invert(t, c):
            src_ref[pos_ref[t]] = t
            return c
        lax.fori_loop(0, T_ALL, invert, 0, unroll=8)

        @pl.when(meta_ref[2, 0] == 1)
        def _():
            def body(g, c):
                for k in range(8):
                    row_copy(0, g * 8 + k, 0).start()
                return c
            lax.fori_loop(0, ROW_TILE // 8, body, 0)

    nxt = jnp.minimum(j + 1, N_ROW_TILES - 1)
    has_next = (j + 1 < N_ROW_TILES) & (meta_ref[2, nxt] == 1)
    valid = meta_ref[2, j] == 1
    prev = jnp.maximum(j - 1, 0)

    @pl.when(valid & ((j == 0) | (meta_ref[0, j] != meta_ref[0, prev])))
    def _():
        wga_s[...] = wga_ref[0].astype(BF16)
        wua_s[...] = wua_ref[0].astype(BF16)
        wda_s[...] = wda_ref[0].astype(BF16)

    @pl.when(valid & ((j == 0) | (meta_ref[1, j] != meta_ref[1, prev])))
    def _():
        wgb_s[...] = wgb_ref[0].astype(BF16)
        wub_s[...] = wub_ref[0].astype(BF16)
        wdb_s[...] = wdb_ref[0].astype(BF16)

    def compute(fetch_next):
        slot = j % 2
        nslot = nxt % 2
        step = ROW_TILE // 8
        batches = iter(range(0, ROW_TILE, step))

        def fetch():
            if fetch_next:
                lo = next(batches)
                issue_rows(nxt, nslot, lo, lo + step)

        wait(slot)
        x = xbuf[slot, :, 0:D_MODEL].astype(BF16)
        gates = xbuf[slot, :, D_MODEL:H2_EXT]
        hg_a = _dot(x, wga_s[...])
        fetch()
        hu_a = _dot(x, wua_s[...])
        fetch()
        hg_b = _dot(x, wgb_s[...])
        fetch()
        hu_b = _dot(x, wub_s[...])
        fetch()
        act_a = (hg_a * jax.nn.sigmoid(hg_a) * hu_a * gates[:, 0:1]).astype(BF16)
        fetch()
        act_b = (hg_b * jax.nn.sigmoid(hg_b) * hu_b * gates[:, 1:2]).astype(BF16)
        fetch()
        y = _dot(act_a, wda_s[...])
        fetch()
        y = y + _dot(act_b, wdb_s[...])
        fetch()
        y_ref[...] = y

    @pl.when(valid & has_next)
    def _():
        compute(True)

    @pl.when(valid & jnp.logical_not(has_next))
    def _():
        compute(False)

    @pl.when(jnp.logical_not(valid))
    def _():
        y_ref[...] = jnp.zeros_like(y_ref)


def _moe(meta, pos, h2ext, w_gate, w_up, w_down):
    up_spec = lambda slot: pl.BlockSpec((1, D_MODEL, EXPERT_FF), lambda j, meta, pos: (meta[slot, j], 0, 0))
    down_spec = lambda slot: pl.BlockSpec((1, EXPERT_FF, D_MODEL), lambda j, meta, pos: (meta[slot, j], 0, 0))
    grid_spec = pltpu.PrefetchScalarGridSpec(
        num_scalar_prefetch=2,
        grid=(N_ROW_TILES,),
        in_specs=[pl.BlockSpec(memory_space=pl.ANY),
                  up_spec(0), up_spec(0), down_spec(0), up_spec(1), up_spec(1), down_spec(1)],
        out_specs=pl.BlockSpec((ROW_TILE, D_MODEL), lambda j, meta, pos: (j, 0)),
        scratch_shapes=[pltpu.SMEM((ROW_CAP,), jnp.int32),
                        pltpu.VMEM((2, ROW_TILE, H2_EXT), F32),
                        pltpu.SemaphoreType.DMA((2,)),
                        pltpu.VMEM((D_MODEL, EXPERT_FF), BF16), pltpu.VMEM((D_MODEL, EXPERT_FF), BF16),
                        pltpu.VMEM((EXPERT_FF, D_MODEL), BF16),
                        pltpu.VMEM((D_MODEL, EXPERT_FF), BF16), pltpu.VMEM((D_MODEL, EXPERT_FF), BF16),
                        pltpu.VMEM((EXPERT_FF, D_MODEL), BF16)])
    return pl.pallas_call(
        _moe_kernel,
        out_shape=jax.ShapeDtypeStruct((ROW_CAP, D_MODEL), F32),
        grid_spec=grid_spec,
        compiler_params=_cparams(("arbitrary",)),
        name="moe_experts",
    )(meta, pos, h2ext, w_gate, w_up, w_down, w_gate, w_up, w_down)


def _final_kernel(pos_ref, y_hbm, x1_ref, m_ref, gn_ref, op_ref, os_ref, ybuf, sem):
    i = pl.program_id(0)

    def row_copy(tile, r, slot):
        p = pos_ref[tile * TILE + r]
        return pltpu.make_async_copy(y_hbm.at[pl.ds(p, 1), :], ybuf.at[slot, pl.ds(r, 1), :], sem.at[slot])

    def issue(tile, slot):
        def body(r2, c):
            row_copy(tile, 2 * r2, slot).start(priority=0)
            row_copy(tile, 2 * r2 + 1, slot).start(priority=1)
            return c
        lax.fori_loop(0, TILE // 2, body, 0, unroll=4)

    def wait(slot):
        pltpu.make_async_copy(y_hbm.at[pl.ds(0, TILE)], ybuf.at[slot], sem.at[slot]).wait()

    @pl.when(i == 0)
    def _():
        issue(0, 0)

    @pl.when(i + 1 < N_TILES)
    def _():
        issue(i + 1, (i + 1) % 2)

    slot = i % 2
    wait(slot)
    out = x1_ref[...] + m_ref[0, 5:6, :] * _rms(ybuf[slot], gn_ref[3:4, :])

    @pl.when(i < N_TILES_P)
    def _():
        op_ref[...] = out

    @pl.when(i >= N_TILES_P)
    def _():
        os_ref[...] = out


def _final(pos, y_sorted, x1, mods3, g_norm):
    tps = DEC_SEQ // TILE
    grid_spec = pltpu.PrefetchScalarGridSpec(
        num_scalar_prefetch=1,
        grid=(N_TILES,),
        in_specs=[pl.BlockSpec(memory_space=pl.ANY),
                  pl.BlockSpec((TILE, D_MODEL), lambda i, pos: (i, 0)),
                  pl.BlockSpec((1, N_MOD, D_MODEL), lambda i, pos: (_mod_row_of_tile(i, tps, N_TILES_P), 0, 0)),
                  pl.BlockSpec((4, D_MODEL), lambda i, pos: (0, 0))],
        out_specs=(pl.BlockSpec((TILE, D_MODEL), lambda i, pos: (jnp.minimum(i, N_TILES_P - 1), 0)),
                   pl.BlockSpec((TILE, D_MODEL), lambda i, pos: (jnp.maximum(i - N_TILES_P, 0), 0))),
        scratch_shapes=[pltpu.VMEM((2, TILE, D_MODEL), F32), pltpu.SemaphoreType.DMA((2,))])
    return pl.pallas_call(
        _final_kernel,
        out_shape=(jax.ShapeDtypeStruct((T_PROMPT, D_MODEL), F32),
                   jax.ShapeDtypeStruct((T_SAMPLE, D_MODEL), F32)),
        grid_spec=grid_spec,
        compiler_params=_cparams(("arbitrary",)),
        name="moe_combine_final",
    )(pos, y_sorted, x1, mods3, g_norm)


def kernel(x_prompt, x_sample, state_C, state_n, state_m, c, c_ctx, w_ada, b_ada, g_norm, w_in, ml_gate_bias, ml_head_gain, hy_conv_w, hy_f_w1, hy_f_b1, hy_f_w2, hy_f_b2, hy_f_w3, hy_f_b3, hy_decay, hy_bias, w_out, w_rc, b_rc, w_rf, b_rf, w_gate, w_up, w_down):
    xp = x_prompt.reshape(T_PROMPT, D_MODEL)
    xs = x_sample.reshape(T_SAMPLE, D_MODEL)
    gn = g_norm[0]

    cv = jnp.concatenate([c_ctx[None, :], c, jnp.zeros((MOD_ROWS - 1 - DEC_BATCH, D_MODEL), F32)], axis=0)
    mods3 = _ada(cv, w_ada[0], b_ada[0]).reshape(MOD_ROWS, N_MOD, D_MODEL)

    w_in0 = w_in[0]
    w_qkvo, w_hy = _prep_in_weights(w_in0.T)
    wg = w_in0[:, ML_QKVO_COLS:ML_QKVO_COLS + ML_GATE_COLS]
    gbt = ml_gate_bias[0].reshape(ML_GATE_COLS, 1)
    proj, gates, gates_t = _inproj(xp, xs, mods3, gn, w_qkvo, w_hy, wg.T, gbt)

    gain = ml_head_gain[0].reshape(1, ML_WIDTH)
    y_ml_p, c_new, n_new, m_new = _mlstm(proj, gates, gates_t, gain, None, SEQ, BATCH, 0)
    state = (state_C[:, 0], state_n[:, 0], state_m[:, 0].reshape(DEC_BATCH, 2 * ML_HEADS, 1))
    y_ml_s, _, _, _ = _mlstm(proj, gates, gates_t, gain, state, DEC_SEQ, DEC_BATCH, T_PROMPT // DEC_SEQ)

    w1p = jnp.pad(hy_f_w1[0], ((0, 128 - HY_EMB), (0, 0)))
    b1 = hy_f_b1[0].reshape(1, -1)
    b2 = hy_f_b2[0].reshape(1, -1)
    b3 = hy_f_b3[0].reshape(1, -1)
    dec = hy_decay[0].reshape(1, -1)
    z_parts = []
    for seq_len, n_seq, row_off, width, seqs in ((SEQ, BATCH, 0, SEQ, 4), (DEC_SEQ, DEC_BATCH, T_PROMPT, GRID_W, 2)):
        f, ft = _dft_mats(seq_len)
        coefs = _hyena_filters(seq_len, f, w1p, b1, hy_f_w2[0], b2, hy_f_w3[0], b3, dec)
        z_parts.append(_hyena(proj, hy_conv_w[0], coefs, hy_bias[0], f, ft, seq_len, n_seq, row_off, width, seqs))
    z_p, z_s = z_parts

    pad_r = ROUTER_ROWS - N_GROUPS - N_EXPERTS
    w_r = jnp.pad(jnp.concatenate([w_rc[0], w_rf[0]], axis=1).T, ((0, pad_r), (0, 0)))
    b_r = jnp.pad(jnp.concatenate([b_rc[0], b_rf[0]], axis=0), (0, pad_r)).reshape(ROUTER_ROWS, 1)
    x1, h2ext, bid = _outproj(xp, xs, y_ml_p, y_ml_s, z_p, z_s, mods3, gn, w_out[0].astype(BF16), w_r, b_r)

    pos3, meta = _route(bid)
    pos = pos3.reshape(T_ALL)
    y_sorted = _moe(meta, pos, h2ext, w_gate[0], w_up[0], w_down[0])
    y_p, y_s = _final(pos, y_sorted, x1, mods3, gn)

    new_c = c_new.reshape(BATCH, 1, 2, ML_HEADS, ML_HEAD_DIM, ML_HEAD_DIM)
    new_n = n_new.reshape(BATCH, 1, 2, ML_HEADS, ML_HEAD_DIM)
    new_m = m_new[:, :, 0].reshape(BATCH, 1, 2, ML_HEADS)
    return (y_p.reshape(BATCH, SEQ, D_MODEL), y_s.reshape(DEC_BATCH, DEC_SEQ, D_MODEL), new_c, new_n, new_m)
```

```python
import functools
import math

import jax
import jax.numpy as jnp
import numpy as np
from jax import lax
from jax.experimental import pallas as pl
from jax.experimental.pallas import tpu as pltpu

F32 = jnp.float32
BF16 = jnp.bfloat16

D_MODEL = 1024
BATCH = 16
SEQ = 256
DEC_BATCH = 4
DEC_SEQ = 1024
GRID_W = 64
ML_WIDTH = 512
ML_HEADS = 4
ML_HEAD_DIM = 128
HY_WIDTH = 512
HY_ORDER = 2
HY_EMB = 33
HY_BANDS = 16
HY_FILTER_HIDDEN = 64
HY_MOD_SHIFT = 0.05
N_GROUPS = 4
EXPERTS_PER_GROUP = 4
N_EXPERTS = 16
EXPERT_FF = 512
N_MOD = 6
EPS = 1e-6
ML_QKVO_COLS = 4 * ML_WIDTH
ML_GATE_COLS = 4 * ML_HEADS
HY_COLS = 3 * HY_WIDTH
MAIN_COLS = ML_QKVO_COLS + HY_COLS

T_PROMPT = BATCH * SEQ
T_SAMPLE = DEC_BATCH * DEC_SEQ
T_ALL = T_PROMPT + T_SAMPLE
TILE = 256
N_TILES_P = T_PROMPT // TILE
N_TILES = T_ALL // TILE
MOD_ROWS = 8
K_SCALE = ML_HEAD_DIM ** -0.5
VMEM_LIMIT = 56 * 1024 * 1024


def _cparams(sem):
    return pltpu.CompilerParams(dimension_semantics=sem, vmem_limit_bytes=VMEM_LIMIT)


def _split2(x):
    hi = x.astype(BF16)
    lo = (x - hi.astype(F32)).astype(BF16)
    return hi, lo


def _dot(a, b):
    return jnp.dot(a, b, preferred_element_type=F32)


def _dot_nt(a, b):
    return lax.dot_general(a, b, (((1,), (1,)), ((), ())), preferred_element_type=F32)


def _dot_tn(a, b):
    return lax.dot_general(a, b, (((0,), (0,)), ((), ())), preferred_element_type=F32)


def _dot3(a, b):
    ah, al = _split2(a)
    bh, bl = _split2(b)
    return _dot(ah, bh) + _dot(al, bh) + _dot(ah, bl)


def _dot3_nt(a, b):
    ah, al = _split2(a)
    bh, bl = _split2(b)
    return _dot_nt(ah, bh) + _dot_nt(al, bh) + _dot_nt(ah, bl)


def _dot_exact_rhs(x, t):
    x1 = x.astype(BF16)
    r1 = x - x1.astype(F32)
    x2 = r1.astype(BF16)
    x3 = (r1 - x2.astype(F32)).astype(BF16)
    return _dot(x1, t) + _dot(x2, t) + _dot(x3, t)


def _rms(x, g):
    return x * lax.rsqrt(jnp.mean(x * x, axis=-1, keepdims=True) + EPS) * g


def _mod_row_of_tile(i, tiles_per_sample_seq, n_prompt_tiles):
    return jnp.where(i < n_prompt_tiles, 0, 1 + (i - n_prompt_tiles) // tiles_per_sample_seq)


def _ada_kernel(cv_ref, w_ref, b_ref, o_ref):
    cv = cv_ref[...]
    s = cv * jax.nn.sigmoid(cv)
    o_ref[...] = _dot3(s, w_ref[...]) + b_ref[...]


def _ada(cv, w_ada, b_ada):
    n = N_MOD * D_MODEL
    return pl.pallas_call(
        _ada_kernel,
        out_shape=jax.ShapeDtypeStruct((MOD_ROWS, n), F32),
        grid=(N_MOD,),
        in_specs=[pl.BlockSpec((MOD_ROWS, D_MODEL), lambda j: (0, 0)),
                  pl.BlockSpec((D_MODEL, D_MODEL), lambda j: (0, j)),
                  pl.BlockSpec((1, D_MODEL), lambda j: (0, j))],
        out_specs=pl.BlockSpec((MOD_ROWS, D_MODEL), lambda j: (0, j)),
        compiler_params=_cparams(("arbitrary",)),
        name="ada_mod",
    )(cv, w_ada, b_ada.reshape(1, n))


PREP_COLS = 512


def _prep_q_kernel(wt_ref, o_ref):
    o_ref[...] = wt_ref[...].T.astype(BF16)


def _prep_hy_kernel(wt_hbm, o_ref, buf, sem):
    start = pl.multiple_of(ML_QKVO_COLS + ML_GATE_COLS + pl.program_id(0) * PREP_COLS, 8)
    copy = pltpu.make_async_copy(wt_hbm.at[pl.ds(start, PREP_COLS), :], buf, sem)
    copy.start()
    copy.wait()
    o_ref[...] = buf[...].T.astype(BF16)


def _prep_in_weights(w_in_t):
    out_blk = pl.BlockSpec((D_MODEL, PREP_COLS), lambda j: (0, j))
    w_qkvo = pl.pallas_call(
        _prep_q_kernel,
        out_shape=jax.ShapeDtypeStruct((D_MODEL, ML_QKVO_COLS), BF16),
        grid=(ML_QKVO_COLS // PREP_COLS,),
        in_specs=[pl.BlockSpec((PREP_COLS, D_MODEL), lambda j: (j, 0))], out_specs=out_blk,
        compiler_params=_cparams(("arbitrary",)), name="prep_w_qkvo",
    )(w_in_t)
    w_hy = pl.pallas_call(
        _prep_hy_kernel,
        out_shape=jax.ShapeDtypeStruct((D_MODEL, HY_COLS), BF16),
        grid=(HY_COLS // PREP_COLS,),
        in_specs=[pl.BlockSpec(memory_space=pl.ANY)], out_specs=out_blk,
        scratch_shapes=[pltpu.VMEM((PREP_COLS, D_MODEL), F32), pltpu.SemaphoreType.DMA(())],
        compiler_params=_cparams(("arbitrary",)), name="prep_w_hy",
    )(w_in_t)
    return w_qkvo, w_hy


def _log_sigmoid(x):
    return jnp.minimum(x, 0.0) - jnp.log1p(jnp.exp(-jnp.abs(x)))


def _rows_to_cols(rows):
    ri = lax.broadcasted_iota(jnp.int32, (TILE, TILE), 0)
    ci = lax.broadcasted_iota(jnp.int32, (TILE, TILE), 1)
    eye = jnp.where(ri == ci, 1.0, 0.0).astype(BF16)
    p1 = rows.astype(BF16)
    r1 = rows - p1.astype(F32)
    p2 = r1.astype(BF16)
    p3 = (r1 - p2.astype(F32)).astype(BF16)
    return _dot_nt(eye, p1) + _dot_nt(eye, p2) + _dot_nt(eye, p3)


BIG_TILE = 2 * TILE
N_BIG_P = T_PROMPT // BIG_TILE
N_BIG = T_ALL // BIG_TILE


def _inproj_kernel(xp_ref, xs_ref, m_ref, gn_ref, wq_ref, wh_ref, wgt_ref, gbt_ref, proj_ref, gatet_ref):
    is_p = pl.program_id(0) < N_BIG_P
    halves = [slice(r * TILE, (r + 1) * TILE) for r in range(BIG_TILE // TILE)]
    hs = [_rms(jnp.where(is_p, xp_ref[rows, :], xs_ref[rows, :]), gn_ref[0:1, :]) * (1.0 + m_ref[0, 1:2, :])
          + m_ref[0, 0:1, :] for rows in halves]
    hbs = [h.astype(BF16) for h in hs]
    cb = 512
    for j in range(ML_QKVO_COLS // cb):
        for rows, hb in zip(halves, hbs):
            proj_ref[rows, j * cb:(j + 1) * cb] = _dot(hb, wq_ref[:, j * cb:(j + 1) * cb]).astype(BF16)
    for j in range(HY_COLS // cb):
        lo = ML_QKVO_COLS + j * cb
        for rows, hb in zip(halves, hbs):
            proj_ref[rows, lo:lo + cb] = _dot(hb, wh_ref[:, j * cb:(j + 1) * cb]).astype(BF16)
    wth, wtl = _split2(wgt_ref[...])
    gts = []
    for h, hb in zip(hs, hbs):
        hl = (h - hb.astype(F32)).astype(BF16)
        gt = _dot_nt(wth, hb) + _dot_nt(wth, hl) + _dot_nt(wtl, hb) + gbt_ref[...]
        row = lax.broadcasted_iota(jnp.int32, gt.shape, 0)
        gts.append(jnp.where((row % 8) >= 4, _log_sigmoid(gt), gt))
    for r, gt in enumerate(gts):
        gatet_ref[r] = gt


def _inproj(xp, xs, mods3, g_norm, w_qkvo, w_hy, wgt, gbt):
    tps = DEC_SEQ // BIG_TILE
    per = BIG_TILE // TILE
    return pl.pallas_call(
        _inproj_kernel,
        out_shape=(jax.ShapeDtypeStruct((T_ALL, MAIN_COLS), BF16),
                   jax.ShapeDtypeStruct((N_TILES, ML_GATE_COLS, TILE), F32)),
        grid=(N_BIG,),
        in_specs=[pl.BlockSpec((BIG_TILE, D_MODEL), lambda i: (jnp.minimum(i, N_BIG_P - 1), 0)),
                  pl.BlockSpec((BIG_TILE, D_MODEL), lambda i: (jnp.maximum(i - N_BIG_P, 0), 0)),
                  pl.BlockSpec((1, N_MOD, D_MODEL), lambda i: (_mod_row_of_tile(i, tps, N_BIG_P), 0, 0)),
                  pl.BlockSpec((4, D_MODEL), lambda i: (0, 0)),
                  pl.BlockSpec((D_MODEL, ML_QKVO_COLS), lambda i: (0, 0)),
                  pl.BlockSpec((D_MODEL, HY_COLS), lambda i: (0, 0)),
                  pl.BlockSpec((ML_GATE_COLS, D_MODEL), lambda i: (0, 0)),
                  pl.BlockSpec((ML_GATE_COLS, 1), lambda i: (0, 0))],
        out_specs=(pl.BlockSpec((BIG_TILE, MAIN_COLS), lambda i: (i, 0)),
                   pl.BlockSpec((per, ML_GATE_COLS, TILE), lambda i: (i, 0, 0))),
        compiler_params=_cparams(("arbitrary",)),
        name="in_proj",
    )(xp, xs, mods3, g_norm, w_qkvo, w_hy, wgt, gbt)


ST_ROWS = ML_HEAD_DIM + 16


def _mlstm_kernel(*refs, seq_len, has_state):
    if has_state:
        (q_ref, k_ref, v_ref, o_ref, gt_ref, gain_ref, c0_ref, n0_ref, m0_ref,
         y_ref, c_ref, n_ref, m_ref, vt_ref, hf_ref, hb_ref, st_ref, ms_ref) = refs
    else:
        (q_ref, k_ref, v_ref, o_ref, gt_ref, gain_ref,
         y_ref, c_ref, n_ref, m_ref, vt_ref, hf_ref, hb_ref, st_ref, ms_ref) = refs
    ch = TILE
    nc = seq_len // ch
    hd = ML_HEAD_DIM
    key = lax.broadcasted_iota(jnp.int32, (ch, ch), 0)
    qry = lax.broadcasted_iota(jnp.int32, (ch, ch), 1)
    key_le = key <= qry
    key_ge = key >= qry
    t_le = jnp.where(key_le, 1.0, 0.0).astype(BF16)
    t_ge = jnp.where(key_ge, 1.0, 0.0).astype(BF16)
    sub16 = lax.broadcasted_iota(jnp.int32, (16, ch), 0)
    ln_scale = math.log(K_SCALE)

    for c in range(nc):
        for h in range(ML_HEADS):
            cols = slice(h * hd, (h + 1) * hd)
            vt_ref[c, cols, :] = v_ref[c * ch:(c + 1) * ch, cols].astype(F32).T.astype(BF16)

    for d in range(2):
        for h in range(ML_HEADS):
            r = d * ML_HEADS + h
            st_ref[r] = jnp.zeros((ST_ROWS, hd), F32)
            if has_state:
                st_ref[r, 0:hd, :] = c0_ref[0, d, h].T
                st_ref[r, hd:hd + 1, :] = n0_ref[0, d, h:h + 1, :]
                ms_ref[r] = jnp.broadcast_to(m0_ref[0, r:r + 1, :], (1, ch))
            else:
                ms_ref[r] = jnp.zeros((1, ch), F32)

    def step(t, carry):
        for d in range(2):
            c = t if d == 0 else nc - 1 - t
            rows = pl.ds(pl.multiple_of(c * ch, ch), ch)
            grow = gt_ref[c]
            brow_all = _dot_exact_rhs(grow, t_le if d == 0 else t_ge)
            ccol_all = _rows_to_cols(grow - pltpu.roll(brow_all, ML_GATE_COLS - ML_HEADS, axis=0))
            mask = key_le if d == 0 else key_ge
            hacc_ref = hf_ref if d == 0 else hb_ref
            heads = range(ML_HEADS)
            regs = [d * ML_HEADS + h for h in heads]
            colss = [slice(h * hd, (h + 1) * hd) for h in heads]
            qs = [q_ref[rows, cols] for cols in colss]
            ks = [k_ref[rows, cols] for cols in colss]
            vts = [vt_ref[c, cols, :] for cols in colss]
            sts = [st_ref[r] for r in regs]
            m_prevs = [ms_ref[r] for r in regs]
            b_rows = [brow_all[(1 + 2 * d) * ML_HEADS + h:(1 + 2 * d) * ML_HEADS + h + 1, :] for h in heads]
            ig_rows = [grow[2 * d * ML_HEADS + h:2 * d * ML_HEADS + h + 1, :] for h in heads]
            qks = [_dot_nt(k, q) for k, q in zip(ks, qs)]
            iqs = [_dot_nt(st.astype(BF16), q) for st, q in zip(sts, qs)]
            ss, sc_inters, m_poss = [], [], []
            for h in heads:
                icol = 2 * d * ML_HEADS + h
                c_col = ccol_all[:, icol:icol + 1]
                logd = jnp.where(mask, b_rows[h] + c_col, -jnp.inf)
                inter = b_rows[h] + m_prevs[h]
                m_pos = jnp.maximum(inter, jnp.max(logd, axis=0, keepdims=True))
                ss.append(qks[h] * jnp.exp(logd - (m_pos - ln_scale)))
                sc_inters.append(jnp.exp(inter - m_pos))
                m_poss.append(m_pos)
            pvs = [_dot(vt, s.astype(BF16)) for vt, s in zip(vts, ss)]
            for h in heads:
                num = sc_inters[h] * iqs[h][0:hd] + pvs[h]
                den = sc_inters[h] * iqs[h][hd:hd + 1] + jnp.sum(ss[h], axis=0, keepdims=True)
                hacc_ref[c, colss[h], :] = num * (1.0 / jnp.maximum(jnp.abs(den), jnp.exp(-m_poss[h])))
            lhss, decays = [], []
            for h in heads:
                b_row = b_rows[h]
                b_last = b_row[:, ch - 1:ch] if d == 0 else b_row[:, 0:1]
                logw = b_last - b_row + ig_rows[h]
                m_new = jnp.maximum(b_last + m_prevs[h], jnp.max(logw, axis=1, keepdims=True))
                w = jnp.exp(logw - (m_new - ln_scale))
                decays.append(jnp.exp(b_last + m_prevs[h] - m_new))
                lhss.append(jnp.concatenate([(vts[h].astype(F32) * w).astype(BF16),
                                             jnp.where(sub16 == 0, w, 0.0).astype(BF16)], axis=0))
                ms_ref[regs[h]] = m_new
            upds = [_dot(lhs, k) for lhs, k in zip(lhss, ks)]
            for h in heads:
                st_ref[regs[h]] = decays[h][:, 0:hd] * sts[h] + upds[h]
        return carry

    lax.fori_loop(0, nc, step, 0)

    for d in range(2):
        for h in range(ML_HEADS):
            r = d * ML_HEADS + h
            c_ref[0, d, h] = st_ref[r, 0:hd, :].T
            n_ref[0, d, h:h + 1, :] = st_ref[r, hd:hd + 1, :]
            m_ref[0, r:r + 1, :] = ms_ref[r][:, 0:hd]
    for c in range(nc):
        for h in range(ML_HEADS):
            cols = slice(h * hd, (h + 1) * hd)
            ht = hf_ref[c, cols, :] + hb_ref[c, cols, :]
            ht = ht * lax.rsqrt(jnp.mean(ht * ht, axis=0, keepdims=True) + EPS)
            rows = slice(c * ch, (c + 1) * ch)
            y = ht.T * gain_ref[:, cols] * jax.nn.sigmoid(o_ref[rows, cols].astype(F32))
            y_ref[rows, cols] = y.astype(BF16)


def _mlstm(proj, gates_t, gain, state, seq_len, n_seq, row_block_off):
    has_state = state is not None
    tiles = seq_len // TILE
    off = row_block_off
    qkvo_specs = [pl.BlockSpec((seq_len, ML_WIDTH), functools.partial(lambda b, j: (off + b, j), j=j))
                  for j in range(4)]
    in_specs = qkvo_specs + [
        pl.BlockSpec((tiles, ML_GATE_COLS, TILE), lambda b: (off + b, 0, 0)),
        pl.BlockSpec((1, ML_WIDTH), lambda b: (0, 0)),
    ]
    args = [proj, proj, proj, proj, gates_t, gain]
    if has_state:
        c0, n0, m0 = state
        in_specs += [
            pl.BlockSpec((1, 2, ML_HEADS, ML_HEAD_DIM, ML_HEAD_DIM), lambda b: (b, 0, 0, 0, 0)),
            pl.BlockSpec((1, 2, ML_HEADS, ML_HEAD_DIM), lambda b: (b, 0, 0, 0)),
            pl.BlockSpec((1, 2 * ML_HEADS, 1), lambda b: (b, 0, 0)),
        ]
        args += [c0, n0, m0]
    out_shape = (jax.ShapeDtypeStruct((n_seq * seq_len, ML_WIDTH), BF16),
                 jax.ShapeDtypeStruct((n_seq, 2, ML_HEADS, ML_HEAD_DIM, ML_HEAD_DIM), F32),
                 jax.ShapeDtypeStruct((n_seq, 2, ML_HEADS, ML_HEAD_DIM), F32),
                 jax.ShapeDtypeStruct((n_seq, 2 * ML_HEADS, ML_HEAD_DIM), F32))
    out_specs = (pl.BlockSpec((seq_len, ML_WIDTH), lambda b: (b, 0)),
                 pl.BlockSpec((1, 2, ML_HEADS, ML_HEAD_DIM, ML_HEAD_DIM), lambda b: (b, 0, 0, 0, 0)),
                 pl.BlockSpec((1, 2, ML_HEADS, ML_HEAD_DIM), lambda b: (b, 0, 0, 0)),
                 pl.BlockSpec((1, 2 * ML_HEADS, ML_HEAD_DIM), lambda b: (b, 0, 0)))
    scratch = [pltpu.VMEM((tiles, ML_WIDTH, TILE), BF16),
               pltpu.VMEM((tiles, ML_WIDTH, TILE), F32), pltpu.VMEM((tiles, ML_WIDTH, TILE), F32),
               pltpu.VMEM((2 * ML_HEADS, ST_ROWS, ML_HEAD_DIM), F32),
               pltpu.VMEM((2 * ML_HEADS, 1, TILE), F32)]
    return pl.pallas_call(
        functools.partial(_mlstm_kernel, seq_len=seq_len, has_state=has_state),
        out_shape=out_shape, grid=(n_seq,), in_specs=in_specs, out_specs=out_specs,
        scratch_shapes=scratch, compiler_params=_cparams(("arbitrary",)),
        name=f"mlstm_{seq_len}",
    )(*args)


def _dft_mats(seq_len):
    k = np.arange(seq_len, dtype=np.int64)[:, None]
    d = np.arange(seq_len, dtype=np.int64)[None, :]
    ang = np.pi * ((k * d) % (2 * seq_len)).astype(np.float64) / seq_len
    sinm = np.sin(ang)
    sinm[0, :] = np.where(d[0] % 2 == 0, 1.0, -1.0)
    f = np.concatenate([np.cos(ang), sinm], axis=0).astype(np.float32)
    return jnp.asarray(f).astype(BF16), jnp.asarray(np.ascontiguousarray(f.T)).astype(BF16)


def _filter_feats(seq_len):
    t = np.linspace(0.0, 1.0, seq_len, dtype=np.float64)[:, None]
    wpos = 2.0 * np.pi * np.arange(seq_len, dtype=np.float64)[:, None] / seq_len
    bands = np.linspace(1e-4, HY_BANDS - 1, HY_BANDS, dtype=np.float64)[None, :]
    z = np.concatenate([t, np.cos(bands * wpos), -np.sin(bands * wpos)], axis=-1)
    return jnp.asarray(np.pad(z, ((0, 0), (0, 128 - HY_EMB))).astype(np.float32))


def _filter_kernel(z_ref, w1_ref, b1_ref, w2_ref, b2_ref, w3_ref, b3_ref, dec_ref, f_ref,
                   a_ref, b_ref, d_ref, *, seq_len):
    n = 2 * seq_len
    oc = 2 * HY_WIDTH
    z = z_ref[...]
    h = jnp.sin(_dot3(z, w1_ref[...]) + b1_ref[...])
    h = jnp.sin(_dot3(h, w2_ref[...]) + b2_ref[...])
    t = z[:, 0:1]
    di = lax.broadcasted_iota(jnp.int32, (seq_len, 1), 0)
    sgn = jnp.where(di % 2 == 0, 1.0, -1.0)
    first = di == 0
    ssums, sdifs = [], []
    for o in range(HY_ORDER):
        cols = slice(o * oc, (o + 1) * oc)
        g = _dot3(h, w3_ref[:, cols]) + b3_ref[:, cols]
        g = g * (jnp.exp(-t * jnp.abs(dec_ref[:, cols])) + HY_MOD_SHIFT)
        ss = jnp.sum(g * g, axis=0, keepdims=True)
        inv = lax.rsqrt(ss[:, :HY_WIDTH] + ss[:, HY_WIDTH:] + EPS)
        hp = g[:, :HY_WIDTH] * inv
        hn = g[:, HY_WIDTH:] * inv
        ssums.append(hp + hn)
        sdifs.append(hp - hn)
    hcs = [_dot(f_ref[0:seq_len, :], s.astype(BF16)) for s in ssums]
    hss = [_dot(f_ref[seq_len:n, :], s.astype(BF16)) for s in sdifs]
    for o in range(HY_ORDER):
        nyq = jnp.sum(ssums[o] * sgn, axis=0, keepdims=True)
        a_ref[o] = hcs[o] * jnp.where(first, 1.0 / n, 2.0 / n)
        b_ref[o] = jnp.where(first, 0.0, hss[o] * (2.0 / n))
        d_ref[o] = jnp.where(first, nyq * (1.0 / n), hcs[o] * (2.0 / n))


def _hyena_filters(seq_len, f, w1p, b1, w2, b2, w3, b3, dec):
    z = _filter_feats(seq_len)
    out = jax.ShapeDtypeStruct((HY_ORDER, seq_len, HY_WIDTH), F32)
    return pl.pallas_call(
        functools.partial(_filter_kernel, seq_len=seq_len),
        out_shape=(out, out, out),
        compiler_params=pltpu.CompilerParams(vmem_limit_bytes=VMEM_LIMIT),
        name=f"hyena_filter_{seq_len}",
    )(z, w1p, b1, w2, b2, w3, b3, dec, f)


def _hyena_kernel(x1_ref, x2_ref, v_ref, cw1_ref, cw2_ref, cwv_ref, a_ref, b_ref, d_ref, bias_ref,
                  f_ref, ft_ref, z_ref, *, seq_len, width, seqs):
    rows = seqs * seq_len
    ti = lax.broadcasted_iota(jnp.int32, (rows, 1), 0)
    has_prev = (ti % width) != 0
    has_next = (ti % width) != (width - 1)

    def short_conv(x_ref, w_ref):
        x = x_ref[...].astype(F32)
        prev = jnp.where(has_prev, pltpu.roll(x, 1, axis=0), 0.0)
        nxt = jnp.where(has_next, pltpu.roll(x, rows - 1, axis=0), 0.0)
        return w_ref[0:1, :] * prev + w_ref[1:2, :] * x + w_ref[2:3, :] * nxt

    gates = (short_conv(x1_ref, cw1_ref), short_conv(x2_ref, cw2_ref))
    v = short_conv(v_ref, cwv_ref)
    sls = [slice(i * seq_len, (i + 1) * seq_len) for i in range(seqs)]
    zs = [v[sl] for sl in sls]
    for o in range(HY_ORDER):
        a, b, dd = a_ref[o], b_ref[o], d_ref[o]
        us = [_dot(f_ref[...], z.astype(BF16)) for z in zs]
        ys = []
        for u in us:
            ut = u[:seq_len]
            ub = u[seq_len:]
            ys.append(((ut * a - ub * b).astype(BF16), (ut * b + ub * dd).astype(BF16)))
        convs = [_dot(ft_ref[:, :seq_len], yt) + _dot(ft_ref[:, seq_len:], yb) for yt, yb in ys]
        zs = [gates[o][sl] * (y + bias_ref[o:o + 1, :] * z) for sl, y, z in zip(sls, convs, zs)]
    for sl, z in zip(sls, zs):
        z_ref[sl, :] = z.astype(BF16)


def _hyena(proj, conv_w, coefs, hy_bias, f, ft, seq_len, n_seq, row_off, width, seqs):
    cb = 256
    nblk = HY_WIDTH // cb
    base = ML_QKVO_COLS // cb
    rows = seqs * seq_len
    off = row_off // rows
    a, b, d = coefs

    def col_spec(part):
        return pl.BlockSpec((rows, cb), lambda j, s: (off + s, base + part * nblk + j))

    def w_spec(part):
        return pl.BlockSpec((3, cb), lambda j, s: (0, part * nblk + j))

    coef_spec = pl.BlockSpec((HY_ORDER, seq_len, cb), lambda j, s: (0, 0, j))
    return pl.pallas_call(
        functools.partial(_hyena_kernel, seq_len=seq_len, width=width, seqs=seqs),
        out_shape=jax.ShapeDtypeStruct((n_seq * seq_len, HY_WIDTH), BF16),
        grid=(nblk, n_seq // seqs),
        in_specs=[col_spec(0), col_spec(1), col_spec(2), w_spec(0), w_spec(1), w_spec(2),
                  coef_spec, coef_spec, coef_spec,
                  pl.BlockSpec((HY_ORDER, cb), lambda j, s: (0, j)),
                  pl.BlockSpec((2 * seq_len, seq_len), lambda j, s: (0, 0)),
                  pl.BlockSpec((seq_len, 2 * seq_len), lambda j, s: (0, 0))],
        out_specs=pl.BlockSpec((rows, cb), lambda j, s: (s, j)),
        compiler_params=_cparams(("arbitrary", "arbitrary")),
        name=f"hyena_conv_{seq_len}",
    )(proj, proj, proj, conv_w, conv_w, conv_w, a, b, d, hy_bias, f, ft)


def _first_max(x, n):
    mx = jnp.max(x, axis=0, keepdims=True)
    row = lax.broadcasted_iota(jnp.int32, x.shape, 0).astype(F32)
    idx = jnp.min(jnp.where(x == mx, row, float(n)), axis=0, keepdims=True)
    return mx, idx.astype(jnp.int32)


ROUTER_ROWS = 32
PAIRS_PER_GROUP = 6
N_BUCKETS = N_GROUPS * PAIRS_PER_GROUP
PAIR_SLOTS = ((0, 1), (0, 2), (0, 3), (1, 3), (1, 2), (3, 2))
LANES = 128
H2_EXT = D_MODEL + LANES
ROW_TILE = 256
ROW_CAP = T_ALL + N_BUCKETS * ROW_TILE
N_ROW_TILES = ROW_CAP // ROW_TILE


def _outproj_kernel(xp_ref, xs_ref, yp_ref, ys_ref, zp_ref, zs_ref, m_ref, gn_ref, wo_ref, wr_ref, br_ref,
                    x1_ref, h2_ref, bid_ref):
    is_p = pl.program_id(0) < N_BIG_P
    wrh, wrl = _split2(wr_ref[...])
    halves = [slice(r * TILE, (r + 1) * TILE) for r in range(BIG_TILE // TILE)]
    ys = [_dot(jnp.where(is_p, yp_ref[rows, :], ys_ref[rows, :]), wo_ref[0:ML_WIDTH, :])
          + _dot(jnp.where(is_p, zp_ref[rows, :], zs_ref[rows, :]), wo_ref[ML_WIDTH:, :]) for rows in halves]
    h2s = []
    for rows, y in zip(halves, ys):
        x = jnp.where(is_p, xp_ref[rows, :], xs_ref[rows, :])
        x1 = x + m_ref[0, 2:3, :] * _rms(y, gn_ref[1:2, :])
        x1_ref[rows, :] = x1
        h2 = _rms(x1, gn_ref[2:3, :]) * (1.0 + m_ref[0, 4:5, :]) + m_ref[0, 3:4, :]
        h2_ref[rows, 0:D_MODEL] = h2
        h2s.append(h2)
    logits = []
    for h2 in h2s:
        h2h, h2l = _split2(h2)
        logits.append(_dot_nt(wrh, h2h) + _dot_nt(wrh, h2l) + _dot_nt(wrl, h2h) + br_ref[...])
    routed = [_route_tile(lg) for lg in logits]
    for r, (rows, (gate_rows, bucket)) in enumerate(zip(halves, routed)):
        h2_ref[rows, D_MODEL:H2_EXT] = jnp.zeros((TILE, LANES), F32)
        h2_ref[rows, D_MODEL:D_MODEL + 8] = _rows_to_cols(gate_rows)
        bid_ref[r] = bucket


def _route_tile(logits):
    lc = logits[0:N_GROUPS]
    mx, gi = _first_max(lc, N_GROUPS)
    p_grp = 1.0 / jnp.sum(jnp.exp(lc - mx), axis=0, keepdims=True)
    lsel = jnp.zeros((EXPERTS_PER_GROUP, TILE), F32)
    for g in range(N_GROUPS):
        lo = N_GROUPS + g * EXPERTS_PER_GROUP
        lsel = jnp.where(gi == g, logits[lo:lo + EXPERTS_PER_GROUP], lsel)
    l1, i1 = _first_max(lsel, EXPERTS_PER_GROUP)
    sub4 = lax.broadcasted_iota(jnp.int32, lsel.shape, 0)
    l2, i2 = _first_max(jnp.where(sub4 == i1, -jnp.inf, lsel), EXPERTS_PER_GROUP)
    e2 = jnp.exp(l2 - l1)
    w1 = p_grp / (1.0 + e2)
    w2 = p_grp * e2 / (1.0 + e2)
    lo_e = jnp.minimum(i1, i2)
    hi_e = jnp.maximum(i1, i2)
    pair = jnp.where(lo_e == 0, hi_e - 1, jnp.where(lo_e == 1, jnp.where(hi_e == 3, 3, 4), 5))
    slot_a = jnp.where(pair == 5, hi_e, lo_e)
    first_in_a = i1 == slot_a
    w_a = jnp.where(first_in_a, w1, w2)
    w_b = jnp.where(first_in_a, w2, w1)
    sub = lax.broadcasted_iota(jnp.int32, (8, TILE), 0)
    gate_rows = jnp.where(sub == 0, w_a, jnp.where(sub == 1, w_b, 0.0))
    return gate_rows, gi * PAIRS_PER_GROUP + pair


def _outproj(xp, xs, yp, ys, zp, zs, mods3, g_norm, w_out, w_r, b_r):
    tps = DEC_SEQ // BIG_TILE
    per = BIG_TILE // TILE
    pidx = lambda i: (jnp.minimum(i, N_BIG_P - 1), 0)
    sidx = lambda i: (jnp.maximum(i - N_BIG_P, 0), 0)
    return pl.pallas_call(
        _outproj_kernel,
        out_shape=(jax.ShapeDtypeStruct((T_ALL, D_MODEL), F32),
                   jax.ShapeDtypeStruct((T_ALL, H2_EXT), F32),
                   jax.ShapeDtypeStruct((N_TILES, 1, TILE), jnp.int32)),
        grid=(N_BIG,),
        in_specs=[pl.BlockSpec((BIG_TILE, D_MODEL), pidx), pl.BlockSpec((BIG_TILE, D_MODEL), sidx),
                  pl.BlockSpec((BIG_TILE, ML_WIDTH), pidx), pl.BlockSpec((BIG_TILE, ML_WIDTH), sidx),
                  pl.BlockSpec((BIG_TILE, HY_WIDTH), pidx), pl.BlockSpec((BIG_TILE, HY_WIDTH), sidx),
                  pl.BlockSpec((1, N_MOD, D_MODEL), lambda i: (_mod_row_of_tile(i, tps, N_BIG_P), 0, 0)),
                  pl.BlockSpec((4, D_MODEL), lambda i: (0, 0)),
                  pl.BlockSpec((D_MODEL, D_MODEL), lambda i: (0, 0)),
                  pl.BlockSpec((ROUTER_ROWS, D_MODEL), lambda i: (0, 0)),
                  pl.BlockSpec((ROUTER_ROWS, 1), lambda i: (0, 0))],
        out_specs=(pl.BlockSpec((BIG_TILE, D_MODEL), lambda i: (i, 0)),
                   pl.BlockSpec((BIG_TILE, H2_EXT), lambda i: (i, 0)),
                   pl.BlockSpec((per, 1, TILE), lambda i: (i, 0, 0))),
        compiler_params=_cparams(("arbitrary",)),
        name="out_proj_router",
    )(xp, xs, yp, ys, zp, zs, mods3, g_norm, w_out, w_r, b_r)


def _route_kernel(bid_ref, pos_ref, meta_ref):
    nb = 32
    tm = float(ROW_TILE)
    sub = lax.broadcasted_iota(jnp.int32, (nb, TILE), 0)
    ri = lax.broadcasted_iota(jnp.int32, (TILE, TILE), 0)
    ci = lax.broadcasted_iota(jnp.int32, (TILE, TILE), 1)
    before = jnp.where(ri < ci, 1.0, 0.0).astype(BF16)

    def onehot(blk):
        return jnp.where(sub == bid_ref[blk], 1.0, 0.0)

    zeros = jnp.zeros((nb, 1), F32)
    cnt = lax.fori_loop(0, N_TILES, lambda blk, c: c + jnp.sum(onehot(blk), axis=1, keepdims=True), zeros)
    padded = jnp.floor((cnt + (tm - 1.0)) * (1.0 / tm)) * tm
    r32 = lax.broadcasted_iota(jnp.int32, (nb, nb), 0)
    c32 = lax.broadcasted_iota(jnp.int32, (nb, nb), 1)
    padded_row = jnp.sum(jnp.where(r32 == c32, padded, 0.0), axis=0, keepdims=True)
    offs = jnp.sum(jnp.where(c32 < r32, padded_row, 0.0), axis=1, keepdims=True)
    ends = offs + padded

    def place(blk, seen):
        oh = onehot(blk)
        rank = _dot(oh.astype(BF16), before)
        pos = jnp.sum(oh * (rank + seen + offs), axis=0, keepdims=True)
        pos_ref[blk] = pos.astype(jnp.int32)
        return seen + jnp.sum(oh, axis=1, keepdims=True)

    lax.fori_loop(0, N_TILES, place, zeros)

    start = lax.broadcasted_iota(jnp.int32, (nb, 128), 1).astype(F32) * tm
    bsub = lax.broadcasted_iota(jnp.int32, (nb, 128), 0)
    done = jnp.where((bsub < N_BUCKETS) & (ends <= start), 1.0, 0.0)
    tb = jnp.sum(done, axis=0, keepdims=True)
    valid = jnp.where(tb < N_BUCKETS, 1.0, 0.0)
    tbc = jnp.minimum(tb, N_BUCKETS - 1.0)
    grp = jnp.floor((tbc + 0.5) * (1.0 / PAIRS_PER_GROUP))
    pair = tbc - PAIRS_PER_GROUP * grp
    loc_a = jnp.zeros_like(pair)
    loc_b = jnp.zeros_like(pair)
    for k, (sa, sb) in enumerate(PAIR_SLOTS):
        loc_a = jnp.where(pair == k, float(sa), loc_a)
        loc_b = jnp.where(pair == k, float(sb), loc_b)
    mine = bsub.astype(F32) == tbc
    used = jnp.sum(jnp.where(mine, offs + cnt, 0.0), axis=0, keepdims=True)
    n_rows = jnp.clip(used - start[0:1], 0.0, tm) * valid
    row8 = lax.broadcasted_iota(jnp.int32, (8, 128), 0)
    meta = jnp.where(row8 == 0, grp * EXPERTS_PER_GROUP + loc_a,
                     jnp.where(row8 == 1, grp * EXPERTS_PER_GROUP + loc_b,
                               jnp.where(row8 == 2, valid, jnp.where(row8 == 3, n_rows, 0.0))))
    meta_ref[...] = meta.astype(jnp.int32)


def _route(bid):
    return pl.pallas_call(
        _route_kernel,
        out_shape=(jax.ShapeDtypeStruct((N_TILES, 1, TILE), jnp.int32),
                   jax.ShapeDtypeStruct((8, 128), jnp.int32)),
        compiler_params=pltpu.CompilerParams(vmem_limit_bytes=VMEM_LIMIT),
        name="moe_route",
    )(bid)


def _moe_kernel(meta_ref, pos_ref, h2_hbm, wga_ref, wua_ref, wda_ref, wgb_ref, wub_ref, wdb_ref,
                y_ref, src_ref, xbuf, sem, wga_s, wua_s, wda_s, wgb_s, wub_s, wdb_s):
    j = pl.program_id(0)

    def row_copy(tile, r, slot):
        tok = src_ref[tile * ROW_TILE + r]
        return pltpu.make_async_copy(h2_hbm.at[pl.ds(tok, 1), :], xbuf.at[slot, pl.ds(r, 1), :], sem.at[slot])

    def issue_rows(tile, slot, lo, hi):
        for r in range(lo, hi):
            row_copy(tile, r, slot).start()

    def wait(slot):
        pltpu.make_async_copy(h2_hbm.at[pl.ds(0, ROW_TILE), :], xbuf.at[slot], sem.at[slot]).wait()

    @pl.when(j == 0)
    def _():
        def fill(t, c):
            def body(r, c2):
                p = t * ROW_TILE + r
                src_ref[p] = p % T_ALL
                return c2
            n = meta_ref[3, t]
            lax.fori_loop(n, jnp.where(meta_ref[2, t] == 1, ROW_TILE, n), body, 0)
            return c
        lax.fori_loop(0, N_ROW_TILES, fill, 0)

        def invert(t, c):
            src_ref[pos_ref[t]] = t
            return c
        lax.fori_loop(0, T_ALL, invert, 0, unroll=8)

        @pl.when(meta_ref[2, 0] == 1)
        def _():
            def body(g, c):
                for k in range(8):
                    row_copy(0, g * 8 + k, 0).start()
                return c
            lax.fori_loop(0, ROW_TILE // 8, body, 0)

    nxt = jnp.minimum(j + 1, N_ROW_TILES - 1)
    has_next = (j + 1 < N_ROW_TILES) & (meta_ref[2, nxt] == 1)
    valid = meta_ref[2, j] == 1
    prev = jnp.maximum(j - 1, 0)

    @pl.when(valid & ((j == 0) | (meta_ref[0, j] != meta_ref[0, prev])))
    def _():
        wga_s[...] = wga_ref[0].astype(BF16)
        wua_s[...] = wua_ref[0].astype(BF16)
        wda_s[...] = wda_ref[0].astype(BF16)

    @pl.when(valid & ((j == 0) | (meta_ref[1, j] != meta_ref[1, prev])))
    def _():
        wgb_s[...] = wgb_ref[0].astype(BF16)
        wub_s[...] = wub_ref[0].astype(BF16)
        wdb_s[...] = wdb_ref[0].astype(BF16)

    def compute(fetch_next):
        slot = j % 2
        nslot = nxt % 2
        step = ROW_TILE // 8
        batches = iter(range(0, ROW_TILE, step))

        def fetch():
            if fetch_next:
                lo = next(batches)
                issue_rows(nxt, nslot, lo, lo + step)

        wait(slot)
        x = xbuf[slot, :, 0:D_MODEL].astype(BF16)
        gates = xbuf[slot, :, D_MODEL:H2_EXT]
        hg_a = _dot(x, wga_s[...])
        fetch()
        hu_a = _dot(x, wua_s[...])
        fetch()
        hg_b = _dot(x, wgb_s[...])
        fetch()
        hu_b = _dot(x, wub_s[...])
        fetch()
        act_a = (hg_a * jax.nn.sigmoid(hg_a) * hu_a * gates[:, 0:1]).astype(BF16)
        fetch()
        act_b = (hg_b * jax.nn.sigmoid(hg_b) * hu_b * gates[:, 1:2]).astype(BF16)
        fetch()
        y = _dot(act_a, wda_s[...])
        fetch()
        y = y + _dot(act_b, wdb_s[...])
        fetch()
        y_ref[...] = y

    @pl.when(valid & has_next)
    def _():
        compute(True)

    @pl.when(valid & jnp.logical_not(has_next))
    def _():
        compute(False)

    @pl.when(jnp.logical_not(valid))
    def _():
        y_ref[...] = jnp.zeros_like(y_ref)


def _moe(meta, pos, h2ext, w_gate, w_up, w_down):
    up_spec = lambda slot: pl.BlockSpec((1, D_MODEL, EXPERT_FF), lambda j, meta, pos: (meta[slot, j], 0, 0))
    down_spec = lambda slot: pl.BlockSpec((1, EXPERT_FF, D_MODEL), lambda j, meta, pos: (meta[slot, j], 0, 0))
    grid_spec = pltpu.PrefetchScalarGridSpec(
        num_scalar_prefetch=2,
        grid=(N_ROW_TILES,),
        in_specs=[pl.BlockSpec(memory_space=pl.ANY),
                  up_spec(0), up_spec(0), down_spec(0), up_spec(1), up_spec(1), down_spec(1)],
        out_specs=pl.BlockSpec((ROW_TILE, D_MODEL), lambda j, meta, pos: (j, 0)),
        scratch_shapes=[pltpu.SMEM((ROW_CAP,), jnp.int32),
                        pltpu.VMEM((2, ROW_TILE, H2_EXT), F32),
                        pltpu.SemaphoreType.DMA((2,)),
                        pltpu.VMEM((D_MODEL, EXPERT_FF), BF16), pltpu.VMEM((D_MODEL, EXPERT_FF), BF16),
                        pltpu.VMEM((EXPERT_FF, D_MODEL), BF16),
                        pltpu.VMEM((D_MODEL, EXPERT_FF), BF16), pltpu.VMEM((D_MODEL, EXPERT_FF), BF16),
                        pltpu.VMEM((EXPERT_FF, D_MODEL), BF16)])
    return pl.pallas_call(
        _moe_kernel,
        out_shape=jax.ShapeDtypeStruct((ROW_CAP, D_MODEL), F32),
        grid_spec=grid_spec,
        compiler_params=_cparams(("arbitrary",)),
        name="moe_experts",
    )(meta, pos, h2ext, w_gate, w_up, w_down, w_gate, w_up, w_down)


def _final_kernel(pos_ref, y_hbm, x1_ref, m_ref, gn_ref, op_ref, os_ref, ybuf, sem):
    i = pl.program_id(0)

    def row_copy(tile, r, slot):
        p = pos_ref[tile * TILE + r]
        return pltpu.make_async_copy(y_hbm.at[pl.ds(p, 1), :], ybuf.at[slot, pl.ds(r, 1), :], sem.at[slot])

    def issue(tile, slot):
        def body(r2, c):
            row_copy(tile, 2 * r2, slot).start(priority=0)
            row_copy(tile, 2 * r2 + 1, slot).start(priority=1)
            return c
        lax.fori_loop(0, TILE // 2, body, 0, unroll=4)

    def wait(slot):
        pltpu.make_async_copy(y_hbm.at[pl.ds(0, TILE)], ybuf.at[slot], sem.at[slot]).wait()

    @pl.when(i == 0)
    def _():
        issue(0, 0)

    @pl.when(i + 1 < N_TILES)
    def _():
        issue(i + 1, (i + 1) % 2)

    slot = i % 2
    wait(slot)
    out = x1_ref[...] + m_ref[0, 5:6, :] * _rms(ybuf[slot], gn_ref[3:4, :])

    @pl.when(i < N_TILES_P)
    def _():
        op_ref[...] = out

    @pl.when(i >= N_TILES_P)
    def _():
        os_ref[...] = out


def _final(pos, y_sorted, x1, mods3, g_norm):
    tps = DEC_SEQ // TILE
    grid_spec = pltpu.PrefetchScalarGridSpec(
        num_scalar_prefetch=1,
        grid=(N_TILES,),
        in_specs=[pl.BlockSpec(memory_space=pl.ANY),
                  pl.BlockSpec((TILE, D_MODEL), lambda i, pos: (i, 0)),
                  pl.BlockSpec((1, N_MOD, D_MODEL), lambda i, pos: (_mod_row_of_tile(i, tps, N_TILES_P), 0, 0)),
                  pl.BlockSpec((4, D_MODEL), lambda i, pos: (0, 0))],
        out_specs=(pl.BlockSpec((TILE, D_MODEL), lambda i, pos: (jnp.minimum(i, N_TILES_P - 1), 0)),
                   pl.BlockSpec((TILE, D_MODEL), lambda i, pos: (jnp.maximum(i - N_TILES_P, 0), 0))),
        scratch_shapes=[pltpu.VMEM((2, TILE, D_MODEL), F32), pltpu.SemaphoreType.DMA((2,))])
    return pl.pallas_call(
        _final_kernel,
        out_shape=(jax.ShapeDtypeStruct((T_PROMPT, D_MODEL), F32),
                   jax.ShapeDtypeStruct((T_SAMPLE, D_MODEL), F32)),
        grid_spec=grid_spec,
        compiler_params=_cparams(("arbitrary",)),
        name="moe_combine_final",
    )(pos, y_sorted, x1, mods3, g_norm)


def kernel(x_prompt, x_sample, state_C, state_n, state_m, c, c_ctx, w_ada, b_ada, g_norm, w_in, ml_gate_bias, ml_head_gain, hy_conv_w, hy_f_w1, hy_f_b1, hy_f_w2, hy_f_b2, hy_f_w3, hy_f_b3, hy_decay, hy_bias, w_out, w_rc, b_rc, w_rf, b_rf, w_gate, w_up, w_down):
    xp = x_prompt.reshape(T_PROMPT, D_MODEL)
    xs = x_sample.reshape(T_SAMPLE, D_MODEL)
    gn = g_norm[0]

    cv = jnp.concatenate([c_ctx[None, :], c, jnp.zeros((MOD_ROWS - 1 - DEC_BATCH, D_MODEL), F32)], axis=0)
    mods3 = _ada(cv, w_ada[0], b_ada[0]).reshape(MOD_ROWS, N_MOD, D_MODEL)

    w_in0 = w_in[0]
    w_qkvo, w_hy = _prep_in_weights(w_in0.T)
    wg = w_in0[:, ML_QKVO_COLS:ML_QKVO_COLS + ML_GATE_COLS]
    gbt = ml_gate_bias[0].reshape(ML_GATE_COLS, 1)
    proj, gates_t = _inproj(xp, xs, mods3, gn, w_qkvo, w_hy, wg.T, gbt)

    gain = ml_head_gain[0].reshape(1, ML_WIDTH)
    y_ml_p, c_new, n_new, m_new = _mlstm(proj, gates_t, gain, None, SEQ, BATCH, 0)
    state = (state_C[:, 0], state_n[:, 0], state_m[:, 0].reshape(DEC_BATCH, 2 * ML_HEADS, 1))
    y_ml_s, _, _, _ = _mlstm(proj, gates_t, gain, state, DEC_SEQ, DEC_BATCH, T_PROMPT // DEC_SEQ)

    w1p = jnp.pad(hy_f_w1[0], ((0, 128 - HY_EMB), (0, 0)))
    b1 = hy_f_b1[0].reshape(1, -1)
    b2 = hy_f_b2[0].reshape(1, -1)
    b3 = hy_f_b3[0].reshape(1, -1)
    dec = hy_decay[0].reshape(1, -1)
    z_parts = []
    for seq_len, n_seq, row_off, width, seqs in ((SEQ, BATCH, 0, SEQ, 8), (DEC_SEQ, DEC_BATCH, T_PROMPT, GRID_W, 2)):
        f, ft = _dft_mats(seq_len)
        coefs = _hyena_filters(seq_len, f, w1p, b1, hy_f_w2[0], b2, hy_f_w3[0], b3, dec)
        z_parts.append(_hyena(proj, hy_conv_w[0], coefs, hy_bias[0], f, ft, seq_len, n_seq, row_off, width, seqs))
    z_p, z_s = z_parts

    pad_r = ROUTER_ROWS - N_GROUPS - N_EXPERTS
    w_r = jnp.pad(jnp.concatenate([w_rc[0], w_rf[0]], axis=1).T, ((0, pad_r), (0, 0)))
    b_r = jnp.pad(jnp.concatenate([b_rc[0], b_rf[0]], axis=0), (0, pad_r)).reshape(ROUTER_ROWS, 1)
    x1, h2ext, bid = _outproj(xp, xs, y_ml_p, y_ml_s, z_p, z_s, mods3, gn, w_out[0].astype(BF16), w_r, b_r)

    pos3, meta = _route(bid)
    pos = pos3.reshape(T_ALL)
    y_sorted = _moe(meta, pos, h2ext, w_gate[0], w_up[0], w_down[0])
    y_p, y_s = _final(pos, y_sorted, x1, mods3, gn)

    new_c = c_new.reshape(BATCH, 1, 2, ML_HEADS, ML_HEAD_DIM, ML_HEAD_DIM)
    new_n = n_new.reshape(BATCH, 1, 2, ML_HEADS, ML_HEAD_DIM)
    new_m = m_new[:, :, 0].reshape(BATCH, 1, 2, ML_HEADS)
    return (y_p.reshape(BATCH, SEQ, D_MODEL), y_s.reshape(DEC_BATCH, DEC_SEQ, D_MODEL), new_c, new_n, new_m)
```

```python
import functools
import math

import jax
import jax.numpy as jnp
import numpy as np
from jax import lax
from jax.experimental import pallas as pl
from jax.experimental.pallas import tpu as pltpu

F32 = jnp.float32
BF16 = jnp.bfloat16

D_MODEL = 1024
BATCH = 16
SEQ = 256
DEC_BATCH = 4
DEC_SEQ = 1024
GRID_W = 64
ML_WIDTH = 512
ML_HEADS = 4
ML_HEAD_DIM = 128
HY_WIDTH = 512
HY_ORDER = 2
HY_EMB = 33
HY_BANDS = 16
HY_FILTER_HIDDEN = 64
HY_MOD_SHIFT = 0.05
N_GROUPS = 4
EXPERTS_PER_GROUP = 4
N_EXPERTS = 16
EXPERT_FF = 512
N_MOD = 6
EPS = 1e-6
ML_QKVO_COLS = 4 * ML_WIDTH
ML_GATE_COLS = 4 * ML_HEADS
HY_COLS = 3 * HY_WIDTH
MAIN_COLS = ML_QKVO_COLS + HY_COLS

T_PROMPT = BATCH * SEQ
T_SAMPLE = DEC_BATCH * DEC_SEQ
T_ALL = T_PROMPT + T_SAMPLE
TILE = 256
N_TILES_P = T_PROMPT // TILE
N_TILES = T_ALL // TILE
MOD_ROWS = 8
K_SCALE = ML_HEAD_DIM ** -0.5
VMEM_LIMIT = 56 * 1024 * 1024


def _cparams(sem):
    return pltpu.CompilerParams(dimension_semantics=sem, vmem_limit_bytes=VMEM_LIMIT)


def _split2(x):
    hi = x.astype(BF16)
    lo = (x - hi.astype(F32)).astype(BF16)
    return hi, lo


def _dot(a, b):
    return jnp.dot(a, b, preferred_element_type=F32)


def _dot_nt(a, b):
    return lax.dot_general(a, b, (((1,), (1,)), ((), ())), preferred_element_type=F32)


def _dot_tn(a, b):
    return lax.dot_general(a, b, (((0,), (0,)), ((), ())), preferred_element_type=F32)


def _dot3(a, b):
    ah, al = _split2(a)
    bh, bl = _split2(b)
    return _dot(ah, bh) + _dot(al, bh) + _dot(ah, bl)


def _dot3_nt(a, b):
    ah, al = _split2(a)
    bh, bl = _split2(b)
    return _dot_nt(ah, bh) + _dot_nt(al, bh) + _dot_nt(ah, bl)


def _dot_exact_rhs(x, t):
    x1 = x.astype(BF16)
    r1 = x - x1.astype(F32)
    x2 = r1.astype(BF16)
    x3 = (r1 - x2.astype(F32)).astype(BF16)
    return _dot(x1, t) + _dot(x2, t) + _dot(x3, t)


def _rms(x, g):
    return x * lax.rsqrt(jnp.mean(x * x, axis=-1, keepdims=True) + EPS) * g


def _mod_row_of_tile(i, tiles_per_sample_seq, n_prompt_tiles):
    return jnp.where(i < n_prompt_tiles, 0, 1 + (i - n_prompt_tiles) // tiles_per_sample_seq)


def _ada_kernel(cv_ref, w_ref, b_ref, o_ref):
    cv = cv_ref[...]
    s = cv * jax.nn.sigmoid(cv)
    sh, sl = _split2(s)
    wh, wl = _split2(w_ref[...])
    both = _dot(jnp.concatenate([sh.astype(F32), sl.astype(F32)], axis=0).astype(BF16), wh)
    o_ref[...] = both[0:MOD_ROWS] + both[MOD_ROWS:] + _dot(sh, wl) + b_ref[...]


def _ada(cv, w_ada, b_ada):
    n = N_MOD * D_MODEL
    return pl.pallas_call(
        _ada_kernel,
        out_shape=jax.ShapeDtypeStruct((MOD_ROWS, n), F32),
        grid=(N_MOD,),
        in_specs=[pl.BlockSpec((MOD_ROWS, D_MODEL), lambda j: (0, 0)),
                  pl.BlockSpec((D_MODEL, D_MODEL), lambda j: (0, j)),
                  pl.BlockSpec((1, D_MODEL), lambda j: (0, j))],
        out_specs=pl.BlockSpec((MOD_ROWS, D_MODEL), lambda j: (0, j)),
        compiler_params=_cparams(("arbitrary",)),
        name="ada_mod",
    )(cv, w_ada, b_ada.reshape(1, n))


PREP_COLS = 512


def _prep_q_kernel(wt_ref, o_ref):
    o_ref[...] = wt_ref[...].T.astype(BF16)


def _prep_hy_kernel(wt_hbm, o_ref, buf, sem):
    start = pl.multiple_of(ML_QKVO_COLS + ML_GATE_COLS + pl.program_id(0) * PREP_COLS, 8)
    copy = pltpu.make_async_copy(wt_hbm.at[pl.ds(start, PREP_COLS), :], buf, sem)
    copy.start()
    copy.wait()
    o_ref[...] = buf[...].T.astype(BF16)


def _prep_in_weights(w_in_t):
    out_blk = pl.BlockSpec((D_MODEL, PREP_COLS), lambda j: (0, j))
    w_qkvo = pl.pallas_call(
        _prep_q_kernel,
        out_shape=jax.ShapeDtypeStruct((D_MODEL, ML_QKVO_COLS), BF16),
        grid=(ML_QKVO_COLS // PREP_COLS,),
        in_specs=[pl.BlockSpec((PREP_COLS, D_MODEL), lambda j: (j, 0))], out_specs=out_blk,
        compiler_params=_cparams(("arbitrary",)), name="prep_w_qkvo",
    )(w_in_t)
    w_hy = pl.pallas_call(
        _prep_hy_kernel,
        out_shape=jax.ShapeDtypeStruct((D_MODEL, HY_COLS), BF16),
        grid=(HY_COLS // PREP_COLS,),
        in_specs=[pl.BlockSpec(memory_space=pl.ANY)], out_specs=out_blk,
        scratch_shapes=[pltpu.VMEM((PREP_COLS, D_MODEL), F32), pltpu.SemaphoreType.DMA(())],
        compiler_params=_cparams(("arbitrary",)), name="prep_w_hy",
    )(w_in_t)
    return w_qkvo, w_hy


def _log_sigmoid(x):
    return jnp.minimum(x, 0.0) - jnp.log1p(jnp.exp(-jnp.abs(x)))


def _rows_to_cols(rows):
    ri = lax.broadcasted_iota(jnp.int32, (TILE, TILE), 0)
    ci = lax.broadcasted_iota(jnp.int32, (TILE, TILE), 1)
    eye = jnp.where(ri == ci, 1.0, 0.0).astype(BF16)
    p1 = rows.astype(BF16)
    r1 = rows - p1.astype(F32)
    p2 = r1.astype(BF16)
    p3 = (r1 - p2.astype(F32)).astype(BF16)
    return _dot_nt(eye, p1) + _dot_nt(eye, p2) + _dot_nt(eye, p3)


BIG_TILE = 2 * TILE
N_BIG_P = T_PROMPT // BIG_TILE
N_BIG = T_ALL // BIG_TILE


def _inproj_kernel(xp_ref, xs_ref, m_ref, gn_ref, wq_ref, wh_ref, wgt_ref, gbt_ref, proj_ref, gatet_ref):
    is_p = pl.program_id(0) < N_BIG_P
    halves = [slice(r * TILE, (r + 1) * TILE) for r in range(BIG_TILE // TILE)]
    hs = [_rms(jnp.where(is_p, xp_ref[rows, :], xs_ref[rows, :]), gn_ref[0:1, :]) * (1.0 + m_ref[0, 1:2, :])
          + m_ref[0, 0:1, :] for rows in halves]
    hbs = [h.astype(BF16) for h in hs]
    cb = 512
    for j in range(ML_QKVO_COLS // cb):
        for rows, hb in zip(halves, hbs):
            proj_ref[rows, j * cb:(j + 1) * cb] = _dot(hb, wq_ref[:, j * cb:(j + 1) * cb]).astype(BF16)
    for j in range(HY_COLS // cb):
        lo = ML_QKVO_COLS + j * cb
        for rows, hb in zip(halves, hbs):
            proj_ref[rows, lo:lo + cb] = _dot(hb, wh_ref[:, j * cb:(j + 1) * cb]).astype(BF16)
    wth, wtl = _split2(wgt_ref[...])
    gts = []
    for h, hb in zip(hs, hbs):
        hl = (h - hb.astype(F32)).astype(BF16)
        gt = _dot_nt(wth, hb) + _dot_nt(wth, hl) + _dot_nt(wtl, hb) + gbt_ref[...]
        row = lax.broadcasted_iota(jnp.int32, gt.shape, 0)
        gts.append(jnp.where((row % 8) >= 4, _log_sigmoid(gt), gt))
    for r, gt in enumerate(gts):
        gatet_ref[r] = gt


def _inproj(xp, xs, mods3, g_norm, w_qkvo, w_hy, wgt, gbt):
    tps = DEC_SEQ // BIG_TILE
    per = BIG_TILE // TILE
    return pl.pallas_call(
        _inproj_kernel,
        out_shape=(jax.ShapeDtypeStruct((T_ALL, MAIN_COLS), BF16),
                   jax.ShapeDtypeStruct((N_TILES, ML_GATE_COLS, TILE), F32)),
        grid=(N_BIG,),
        in_specs=[pl.BlockSpec((BIG_TILE, D_MODEL), lambda i: (jnp.minimum(i, N_BIG_P - 1), 0)),
                  pl.BlockSpec((BIG_TILE, D_MODEL), lambda i: (jnp.maximum(i - N_BIG_P, 0), 0)),
                  pl.BlockSpec((1, N_MOD, D_MODEL), lambda i: (_mod_row_of_tile(i, tps, N_BIG_P), 0, 0)),
                  pl.BlockSpec((4, D_MODEL), lambda i: (0, 0)),
                  pl.BlockSpec((D_MODEL, ML_QKVO_COLS), lambda i: (0, 0)),
                  pl.BlockSpec((D_MODEL, HY_COLS), lambda i: (0, 0)),
                  pl.BlockSpec((ML_GATE_COLS, D_MODEL), lambda i: (0, 0)),
                  pl.BlockSpec((ML_GATE_COLS, 1), lambda i: (0, 0))],
        out_specs=(pl.BlockSpec((BIG_TILE, MAIN_COLS), lambda i: (i, 0)),
                   pl.BlockSpec((per, ML_GATE_COLS, TILE), lambda i: (i, 0, 0))),
        compiler_params=_cparams(("arbitrary",)),
        name="in_proj",
    )(xp, xs, mods3, g_norm, w_qkvo, w_hy, wgt, gbt)


ST_ROWS = ML_HEAD_DIM + 16


def _mlstm_kernel(*refs, seq_len, has_state):
    if has_state:
        (q_ref, k_ref, v_ref, o_ref, gt_ref, gain_ref, c0_ref, n0_ref, m0_ref,
         y_ref, c_ref, n_ref, m_ref, vt_ref, hf_ref, hb_ref, st_ref, ms_ref) = refs
    else:
        (q_ref, k_ref, v_ref, o_ref, gt_ref, gain_ref,
         y_ref, c_ref, n_ref, m_ref, vt_ref, hf_ref, hb_ref, st_ref, ms_ref) = refs
    ch = TILE
    nc = seq_len // ch
    hd = ML_HEAD_DIM
    key = lax.broadcasted_iota(jnp.int32, (ch, ch), 0)
    qry = lax.broadcasted_iota(jnp.int32, (ch, ch), 1)
    key_le = key <= qry
    key_ge = key >= qry
    t_le = jnp.where(key_le, 1.0, 0.0).astype(BF16)
    t_ge = jnp.where(key_ge, 1.0, 0.0).astype(BF16)
    sub16 = lax.broadcasted_iota(jnp.int32, (16, ch), 0)
    ln_scale = math.log(K_SCALE)

    for c in range(nc):
        for h in range(ML_HEADS):
            cols = slice(h * hd, (h + 1) * hd)
            vt_ref[c, cols, :] = v_ref[c * ch:(c + 1) * ch, cols].astype(F32).T.astype(BF16)

    for d in range(2):
        for h in range(ML_HEADS):
            r = d * ML_HEADS + h
            st_ref[r] = jnp.zeros((ST_ROWS, hd), F32)
            if has_state:
                st_ref[r, 0:hd, :] = c0_ref[0, d, h].T
                st_ref[r, hd:hd + 1, :] = n0_ref[0, d, h:h + 1, :]
                ms_ref[r] = jnp.broadcast_to(m0_ref[0, r:r + 1, :], (1, ch))
            else:
                ms_ref[r] = jnp.zeros((1, ch), F32)

    def step(t, carry):
        for d in range(2):
            c = t if d == 0 else nc - 1 - t
            rows = pl.ds(pl.multiple_of(c * ch, ch), ch)
            grow = gt_ref[c]
            brow_all = _dot_exact_rhs(grow, t_le if d == 0 else t_ge)
            ccol_all = _rows_to_cols(grow - pltpu.roll(brow_all, ML_GATE_COLS - ML_HEADS, axis=0))
            mask = key_le if d == 0 else key_ge
            hacc_ref = hf_ref if d == 0 else hb_ref
            heads = range(ML_HEADS)
            regs = [d * ML_HEADS + h for h in heads]
            colss = [slice(h * hd, (h + 1) * hd) for h in heads]
            qs = [q_ref[rows, cols] for cols in colss]
            ks = [k_ref[rows, cols] for cols in colss]
            vts = [vt_ref[c, cols, :] for cols in colss]
            sts = [st_ref[r] for r in regs]
            m_prevs = [ms_ref[r] for r in regs]
            b_rows = [brow_all[(1 + 2 * d) * ML_HEADS + h:(1 + 2 * d) * ML_HEADS + h + 1, :] for h in heads]
            ig_rows = [grow[2 * d * ML_HEADS + h:2 * d * ML_HEADS + h + 1, :] for h in heads]
            qks = [_dot_nt(k, q) for k, q in zip(ks, qs)]
            iqs = [_dot_nt(st.astype(BF16), q) for st, q in zip(sts, qs)]
            ss, sc_inters, m_poss = [], [], []
            for h in heads:
                icol = 2 * d * ML_HEADS + h
                c_col = ccol_all[:, icol:icol + 1]
                logd = jnp.where(mask, b_rows[h] + c_col, -jnp.inf)
                inter = b_rows[h] + m_prevs[h]
                m_pos = jnp.maximum(inter, jnp.max(logd, axis=0, keepdims=True))
                ss.append(qks[h] * jnp.exp(logd - (m_pos - ln_scale)))
                sc_inters.append(jnp.exp(inter - m_pos))
                m_poss.append(m_pos)
            pvs = [_dot(vt, s.astype(BF16)) for vt, s in zip(vts, ss)]
            for h in heads:
                num = sc_inters[h] * iqs[h][0:hd] + pvs[h]
                den = sc_inters[h] * iqs[h][hd:hd + 1] + jnp.sum(ss[h], axis=0, keepdims=True)
                hacc_ref[c, colss[h], :] = num * (1.0 / jnp.maximum(jnp.abs(den), jnp.exp(-m_poss[h])))
            lhss, decays = [], []
            for h in heads:
                b_row = b_rows[h]
                b_last = b_row[:, ch - 1:ch] if d == 0 else b_row[:, 0:1]
                logw = b_last - b_row + ig_rows[h]
                m_new = jnp.maximum(b_last + m_prevs[h], jnp.max(logw, axis=1, keepdims=True))
                w = jnp.exp(logw - (m_new - ln_scale))
                decays.append(jnp.exp(b_last + m_prevs[h] - m_new))
                lhss.append(jnp.concatenate([(vts[h].astype(F32) * w).astype(BF16),
                                             jnp.where(sub16 == 0, w, 0.0).astype(BF16)], axis=0))
                ms_ref[regs[h]] = m_new
            upds = [_dot(lhs, k) for lhs, k in zip(lhss, ks)]
            for h in heads:
                st_ref[regs[h]] = decays[h][:, 0:hd] * sts[h] + upds[h]
        return carry

    lax.fori_loop(0, nc, step, 0)

    for d in range(2):
        for h in range(ML_HEADS):
            r = d * ML_HEADS + h
            c_ref[0, d, h] = st_ref[r, 0:hd, :].T
            n_ref[0, d, h:h + 1, :] = st_ref[r, hd:hd + 1, :]
            m_ref[0, r:r + 1, :] = ms_ref[r][:, 0:hd]
    for c in range(nc):
        for h in range(ML_HEADS):
            cols = slice(h * hd, (h + 1) * hd)
            ht = hf_ref[c, cols, :] + hb_ref[c, cols, :]
            ht = ht * lax.rsqrt(jnp.mean(ht * ht, axis=0, keepdims=True) + EPS)
            rows = slice(c * ch, (c + 1) * ch)
            y = ht.T * gain_ref[:, cols] * jax.nn.sigmoid(o_ref[rows, cols].astype(F32))
            y_ref[rows, cols] = y.astype(BF16)


def _mlstm(proj, gates_t, gain, state, seq_len, n_seq, row_block_off):
    has_state = state is not None
    tiles = seq_len // TILE
    off = row_block_off
    qkvo_specs = [pl.BlockSpec((seq_len, ML_WIDTH), functools.partial(lambda b, j: (off + b, j), j=j))
                  for j in range(4)]
    in_specs = qkvo_specs + [
        pl.BlockSpec((tiles, ML_GATE_COLS, TILE), lambda b: (off + b, 0, 0)),
        pl.BlockSpec((1, ML_WIDTH), lambda b: (0, 0)),
    ]
    args = [proj, proj, proj, proj, gates_t, gain]
    if has_state:
        c0, n0, m0 = state
        in_specs += [
            pl.BlockSpec((1, 2, ML_HEADS, ML_HEAD_DIM, ML_HEAD_DIM), lambda b: (b, 0, 0, 0, 0)),
            pl.BlockSpec((1, 2, ML_HEADS, ML_HEAD_DIM), lambda b: (b, 0, 0, 0)),
            pl.BlockSpec((1, 2 * ML_HEADS, 1), lambda b: (b, 0, 0)),
        ]
        args += [c0, n0, m0]
    out_shape = (jax.ShapeDtypeStruct((n_seq * seq_len, ML_WIDTH), BF16),
                 jax.ShapeDtypeStruct((n_seq, 2, ML_HEADS, ML_HEAD_DIM, ML_HEAD_DIM), F32),
                 jax.ShapeDtypeStruct((n_seq, 2, ML_HEADS, ML_HEAD_DIM), F32),
                 jax.ShapeDtypeStruct((n_seq, 2 * ML_HEADS, ML_HEAD_DIM), F32))
    out_specs = (pl.BlockSpec((seq_len, ML_WIDTH), lambda b: (b, 0)),
                 pl.BlockSpec((1, 2, ML_HEADS, ML_HEAD_DIM, ML_HEAD_DIM), lambda b: (b, 0, 0, 0, 0)),
                 pl.BlockSpec((1, 2, ML_HEADS, ML_HEAD_DIM), lambda b: (b, 0, 0, 0)),
                 pl.BlockSpec((1, 2 * ML_HEADS, ML_HEAD_DIM), lambda b: (b, 0, 0)))
    scratch = [pltpu.VMEM((tiles, ML_WIDTH, TILE), BF16),
               pltpu.VMEM((tiles, ML_WIDTH, TILE), F32), pltpu.VMEM((tiles, ML_WIDTH, TILE), F32),
               pltpu.VMEM((2 * ML_HEADS, ST_ROWS, ML_HEAD_DIM), F32),
               pltpu.VMEM((2 * ML_HEADS, 1, TILE), F32)]
    return pl.pallas_call(
        functools.partial(_mlstm_kernel, seq_len=seq_len, has_state=has_state),
        out_shape=out_shape, grid=(n_seq,), in_specs=in_specs, out_specs=out_specs,
        scratch_shapes=scratch, compiler_params=_cparams(("arbitrary",)),
        name=f"mlstm_{seq_len}",
    )(*args)


def _dft_mats(seq_len):
    k = np.arange(seq_len, dtype=np.int64)[:, None]
    d = np.arange(seq_len, dtype=np.int64)[None, :]
    ang = np.pi * ((k * d) % (2 * seq_len)).astype(np.float64) / seq_len
    sinm = np.sin(ang)
    sinm[0, :] = np.where(d[0] % 2 == 0, 1.0, -1.0)
    f = np.concatenate([np.cos(ang), sinm], axis=0).astype(np.float32)
    return jnp.asarray(f).astype(BF16), jnp.asarray(np.ascontiguousarray(f.T)).astype(BF16)


def _filter_feats(seq_len):
    t = np.linspace(0.0, 1.0, seq_len, dtype=np.float64)[:, None]
    wpos = 2.0 * np.pi * np.arange(seq_len, dtype=np.float64)[:, None] / seq_len
    bands = np.linspace(1e-4, HY_BANDS - 1, HY_BANDS, dtype=np.float64)[None, :]
    z = np.concatenate([t, np.cos(bands * wpos), -np.sin(bands * wpos)], axis=-1)
    return jnp.asarray(np.pad(z, ((0, 0), (0, 128 - HY_EMB))).astype(np.float32))


def _filter_kernel(z_ref, w1_ref, b1_ref, w2_ref, b2_ref, w3_ref, b3_ref, dec_ref, f_ref,
                   a_ref, b_ref, d_ref, *, seq_len):
    n = 2 * seq_len
    oc = 2 * HY_WIDTH
    z = z_ref[...]
    h = jnp.sin(_dot3(z, w1_ref[...]) + b1_ref[...])
    h = jnp.sin(_dot3(h, w2_ref[...]) + b2_ref[...])
    t = z[:, 0:1]
    di = lax.broadcasted_iota(jnp.int32, (seq_len, 1), 0)
    sgn = jnp.where(di % 2 == 0, 1.0, -1.0)
    first = di == 0
    ssums, sdifs = [], []
    for o in range(HY_ORDER):
        cols = slice(o * oc, (o + 1) * oc)
        g = _dot3(h, w3_ref[:, cols]) + b3_ref[:, cols]
        g = g * (jnp.exp(-t * jnp.abs(dec_ref[:, cols])) + HY_MOD_SHIFT)
        ss = jnp.sum(g * g, axis=0, keepdims=True)
        inv = lax.rsqrt(ss[:, :HY_WIDTH] + ss[:, HY_WIDTH:] + EPS)
        hp = g[:, :HY_WIDTH] * inv
        hn = g[:, HY_WIDTH:] * inv
        ssums.append(hp + hn)
        sdifs.append(hp - hn)
    hcs = [_dot(f_ref[0:seq_len, :], s.astype(BF16)) for s in ssums]
    hss = [_dot(f_ref[seq_len:n, :], s.astype(BF16)) for s in sdifs]
    for o in range(HY_ORDER):
        nyq = jnp.sum(ssums[o] * sgn, axis=0, keepdims=True)
        a_ref[o] = hcs[o] * jnp.where(first, 1.0 / n, 2.0 / n)
        b_ref[o] = jnp.where(first, 0.0, hss[o] * (2.0 / n))
        d_ref[o] = jnp.where(first, nyq * (1.0 / n), hcs[o] * (2.0 / n))


def _hyena_filters(seq_len, f, w1p, b1, w2, b2, w3, b3, dec):
    z = _filter_feats(seq_len)
    out = jax.ShapeDtypeStruct((HY_ORDER, seq_len, HY_WIDTH), F32)
    return pl.pallas_call(
        functools.partial(_filter_kernel, seq_len=seq_len),
        out_shape=(out, out, out),
        compiler_params=pltpu.CompilerParams(vmem_limit_bytes=VMEM_LIMIT),
        name=f"hyena_filter_{seq_len}",
    )(z, w1p, b1, w2, b2, w3, b3, dec, f)


def _hyena_kernel(x1_ref, x2_ref, v_ref, cw1_ref, cw2_ref, cwv_ref, a_ref, b_ref, d_ref, bias_ref,
                  f_ref, ft_ref, z_ref, *, seq_len, width, seqs):
    rows = seqs * seq_len
    ti = lax.broadcasted_iota(jnp.int32, (rows, 1), 0)
    has_prev = (ti % width) != 0
    has_next = (ti % width) != (width - 1)

    def short_conv(x_ref, w_ref):
        x = x_ref[...].astype(F32)
        prev = jnp.where(has_prev, pltpu.roll(x, 1, axis=0), 0.0)
        nxt = jnp.where(has_next, pltpu.roll(x, rows - 1, axis=0), 0.0)
        return w_ref[0:1, :] * prev + w_ref[1:2, :] * x + w_ref[2:3, :] * nxt

    gates = (short_conv(x1_ref, cw1_ref), short_conv(x2_ref, cw2_ref))
    v = short_conv(v_ref, cwv_ref)
    sls = [slice(i * seq_len, (i + 1) * seq_len) for i in range(seqs)]
    zs = [v[sl] for sl in sls]
    for o in range(HY_ORDER):
        a, b, dd = a_ref[o], b_ref[o], d_ref[o]
        us = [_dot(f_ref[...], z.astype(BF16)) for z in zs]
        ys = []
        for u in us:
            ut = u[:seq_len]
            ub = u[seq_len:]
            ys.append(((ut * a - ub * b).astype(BF16), (ut * b + ub * dd).astype(BF16)))
        convs = [_dot(ft_ref[:, :seq_len], yt) + _dot(ft_ref[:, seq_len:], yb) for yt, yb in ys]
        zs = [gates[o][sl] * (y + bias_ref[o:o + 1, :] * z) for sl, y, z in zip(sls, convs, zs)]
    for sl, z in zip(sls, zs):
        z_ref[sl, :] = z.astype(BF16)


def _hyena(proj, conv_w, coefs, hy_bias, f, ft, seq_len, n_seq, row_off, width, seqs):
    cb = 256
    nblk = HY_WIDTH // cb
    base = ML_QKVO_COLS // cb
    rows = seqs * seq_len
    off = row_off // rows
    a, b, d = coefs

    def col_spec(part):
        return pl.BlockSpec((rows, cb), lambda j, s: (off + s, base + part * nblk + j))

    def w_spec(part):
        return pl.BlockSpec((3, cb), lambda j, s: (0, part * nblk + j))

    coef_spec = pl.BlockSpec((HY_ORDER, seq_len, cb), lambda j, s: (0, 0, j))
    return pl.pallas_call(
        functools.partial(_hyena_kernel, seq_len=seq_len, width=width, seqs=seqs),
        out_shape=jax.ShapeDtypeStruct((n_seq * seq_len, HY_WIDTH), BF16),
        grid=(nblk, n_seq // seqs),
        in_specs=[col_spec(0), col_spec(1), col_spec(2), w_spec(0), w_spec(1), w_spec(2),
                  coef_spec, coef_spec, coef_spec,
                  pl.BlockSpec((HY_ORDER, cb), lambda j, s: (0, j)),
                  pl.BlockSpec((2 * seq_len, seq_len), lambda j, s: (0, 0)),
                  pl.BlockSpec((seq_len, 2 * seq_len), lambda j, s: (0, 0))],
        out_specs=pl.BlockSpec((rows, cb), lambda j, s: (s, j)),
        compiler_params=_cparams(("arbitrary", "arbitrary")),
        name=f"hyena_conv_{seq_len}",
    )(proj, proj, proj, conv_w, conv_w, conv_w, a, b, d, hy_bias, f, ft)


def _first_max(x, n):
    mx = jnp.max(x, axis=0, keepdims=True)
    row = lax.broadcasted_iota(jnp.int32, x.shape, 0).astype(F32)
    idx = jnp.min(jnp.where(x == mx, row, float(n)), axis=0, keepdims=True)
    return mx, idx.astype(jnp.int32)


ROUTER_ROWS = 32
PAIRS_PER_GROUP = 6
N_BUCKETS = N_GROUPS * PAIRS_PER_GROUP
PAIR_SLOTS = ((0, 1), (0, 2), (0, 3), (1, 3), (1, 2), (3, 2))
LANES = 128
H2_EXT = D_MODEL + LANES
ROW_TILE = 256
ROW_CAP = T_ALL + N_BUCKETS * ROW_TILE
N_ROW_TILES = ROW_CAP // ROW_TILE


def _outproj_kernel(xp_ref, xs_ref, yp_ref, ys_ref, zp_ref, zs_ref, m_ref, gn_ref, wo_ref, wr_ref, br_ref,
                    x1_ref, h2_ref, bid_ref):
    is_p = pl.program_id(0) < N_BIG_P
    wrh, wrl = _split2(wr_ref[...])
    halves = [slice(r * TILE, (r + 1) * TILE) for r in range(BIG_TILE // TILE)]
    ys = [_dot(jnp.where(is_p, yp_ref[rows, :], ys_ref[rows, :]), wo_ref[0:ML_WIDTH, :])
          + _dot(jnp.where(is_p, zp_ref[rows, :], zs_ref[rows, :]), wo_ref[ML_WIDTH:, :]) for rows in halves]
    h2s = []
    for rows, y in zip(halves, ys):
        x = jnp.where(is_p, xp_ref[rows, :], xs_ref[rows, :])
        x1 = x + m_ref[0, 2:3, :] * _rms(y, gn_ref[1:2, :])
        x1_ref[rows, :] = x1
        h2 = _rms(x1, gn_ref[2:3, :]) * (1.0 + m_ref[0, 4:5, :]) + m_ref[0, 3:4, :]
        h2_ref[rows, 0:D_MODEL] = h2
        h2s.append(h2)
    logits = []
    for h2 in h2s:
        h2h, h2l = _split2(h2)
        logits.append(_dot_nt(wrh, h2h) + _dot_nt(wrh, h2l) + _dot_nt(wrl, h2h) + br_ref[...])
    routed = [_route_tile(lg) for lg in logits]
    for r, (rows, (gate_rows, bucket)) in enumerate(zip(halves, routed)):
        h2_ref[rows, D_MODEL:H2_EXT] = jnp.zeros((TILE, LANES), F32)
        h2_ref[rows, D_MODEL:D_MODEL + 8] = _rows_to_cols(gate_rows)
        bid_ref[r] = bucket


def _route_tile(logits):
    lc = logits[0:N_GROUPS]
    mx, gi = _first_max(lc, N_GROUPS)
    p_grp = 1.0 / jnp.sum(jnp.exp(lc - mx), axis=0, keepdims=True)
    lsel = jnp.zeros((EXPERTS_PER_GROUP, TILE), F32)
    for g in range(N_GROUPS):
        lo = N_GROUPS + g * EXPERTS_PER_GROUP
        lsel = jnp.where(gi == g, logits[lo:lo + EXPERTS_PER_GROUP], lsel)
    l1, i1 = _first_max(lsel, EXPERTS_PER_GROUP)
    sub4 = lax.broadcasted_iota(jnp.int32, lsel.shape, 0)
    l2, i2 = _first_max(jnp.where(sub4 == i1, -jnp.inf, lsel), EXPERTS_PER_GROUP)
    e2 = jnp.exp(l2 - l1)
    w1 = p_grp / (1.0 + e2)
    w2 = p_grp * e2 / (1.0 + e2)
    lo_e = jnp.minimum(i1, i2)
    hi_e = jnp.maximum(i1, i2)
    pair = jnp.where(lo_e == 0, hi_e - 1, jnp.where(lo_e == 1, jnp.where(hi_e == 3, 3, 4), 5))
    slot_a = jnp.where(pair == 5, hi_e, lo_e)
    first_in_a = i1 == slot_a
    w_a = jnp.where(first_in_a, w1, w2)
    w_b = jnp.where(first_in_a, w2, w1)
    sub = lax.broadcasted_iota(jnp.int32, (8, TILE), 0)
    gate_rows = jnp.where(sub == 0, w_a, jnp.where(sub == 1, w_b, 0.0))
    return gate_rows, gi * PAIRS_PER_GROUP + pair


def _outproj(xp, xs, yp, ys, zp, zs, mods3, g_norm, w_out, w_r, b_r):
    tps = DEC_SEQ // BIG_TILE
    per = BIG_TILE // TILE
    pidx = lambda i: (jnp.minimum(i, N_BIG_P - 1), 0)
    sidx = lambda i: (jnp.maximum(i - N_BIG_P, 0), 0)
    return pl.pallas_call(
        _outproj_kernel,
        out_shape=(jax.ShapeDtypeStruct((T_ALL, D_MODEL), F32),
                   jax.ShapeDtypeStruct((T_ALL, H2_EXT), F32),
                   jax.ShapeDtypeStruct((N_TILES, 1, TILE), jnp.int32)),
        grid=(N_BIG,),
        in_specs=[pl.BlockSpec((BIG_TILE, D_MODEL), pidx), pl.BlockSpec((BIG_TILE, D_MODEL), sidx),
                  pl.BlockSpec((BIG_TILE, ML_WIDTH), pidx), pl.BlockSpec((BIG_TILE, ML_WIDTH), sidx),
                  pl.BlockSpec((BIG_TILE, HY_WIDTH), pidx), pl.BlockSpec((BIG_TILE, HY_WIDTH), sidx),
                  pl.BlockSpec((1, N_MOD, D_MODEL), lambda i: (_mod_row_of_tile(i, tps, N_BIG_P), 0, 0)),
                  pl.BlockSpec((4, D_MODEL), lambda i: (0, 0)),
                  pl.BlockSpec((D_MODEL, D_MODEL), lambda i: (0, 0)),
                  pl.BlockSpec((ROUTER_ROWS, D_MODEL), lambda i: (0, 0)),
                  pl.BlockSpec((ROUTER_ROWS, 1), lambda i: (0, 0))],
        out_specs=(pl.BlockSpec((BIG_TILE, D_MODEL), lambda i: (i, 0)),
                   pl.BlockSpec((BIG_TILE, H2_EXT), lambda i: (i, 0)),
                   pl.BlockSpec((per, 1, TILE), lambda i: (i, 0, 0))),
        compiler_params=_cparams(("arbitrary",)),
        name="out_proj_router",
    )(xp, xs, yp, ys, zp, zs, mods3, g_norm, w_out, w_r, b_r)


def _route_kernel(bid_ref, pos_ref, meta_ref):
    nb = 32
    tm = float(ROW_TILE)
    sub = lax.broadcasted_iota(jnp.int32, (nb, TILE), 0)
    ri = lax.broadcasted_iota(jnp.int32, (TILE, TILE), 0)
    ci = lax.broadcasted_iota(jnp.int32, (TILE, TILE), 1)
    before = jnp.where(ri < ci, 1.0, 0.0).astype(BF16)

    def onehot(blk):
        return jnp.where(sub == bid_ref[blk], 1.0, 0.0)

    zeros = jnp.zeros((nb, 1), F32)
    cnt = lax.fori_loop(0, N_TILES, lambda blk, c: c + jnp.sum(onehot(blk), axis=1, keepdims=True), zeros)
    padded = jnp.floor((cnt + (tm - 1.0)) * (1.0 / tm)) * tm
    r32 = lax.broadcasted_iota(jnp.int32, (nb, nb), 0)
    c32 = lax.broadcasted_iota(jnp.int32, (nb, nb), 1)
    padded_row = jnp.sum(jnp.where(r32 == c32, padded, 0.0), axis=0, keepdims=True)
    offs = jnp.sum(jnp.where(c32 < r32, padded_row, 0.0), axis=1, keepdims=True)
    ends = offs + padded

    def place(blk, seen):
        oh = onehot(blk)
        rank = _dot(oh.astype(BF16), before)
        pos = jnp.sum(oh * (rank + seen + offs), axis=0, keepdims=True)
        pos_ref[blk] = pos.astype(jnp.int32)
        return seen + jnp.sum(oh, axis=1, keepdims=True)

    lax.fori_loop(0, N_TILES, place, zeros)

    start = lax.broadcasted_iota(jnp.int32, (nb, 128), 1).astype(F32) * tm
    bsub = lax.broadcasted_iota(jnp.int32, (nb, 128), 0)
    done = jnp.where((bsub < N_BUCKETS) & (ends <= start), 1.0, 0.0)
    tb = jnp.sum(done, axis=0, keepdims=True)
    valid = jnp.where(tb < N_BUCKETS, 1.0, 0.0)
    tbc = jnp.minimum(tb, N_BUCKETS - 1.0)
    grp = jnp.floor((tbc + 0.5) * (1.0 / PAIRS_PER_GROUP))
    pair = tbc - PAIRS_PER_GROUP * grp
    loc_a = jnp.zeros_like(pair)
    loc_b = jnp.zeros_like(pair)
    for k, (sa, sb) in enumerate(PAIR_SLOTS):
        loc_a = jnp.where(pair == k, float(sa), loc_a)
        loc_b = jnp.where(pair == k, float(sb), loc_b)
    mine = bsub.astype(F32) == tbc
    used = jnp.sum(jnp.where(mine, offs + cnt, 0.0), axis=0, keepdims=True)
    n_rows = jnp.clip(used - start[0:1], 0.0, tm) * valid
    row8 = lax.broadcasted_iota(jnp.int32, (8, 128), 0)
    meta = jnp.where(row8 == 0, grp * EXPERTS_PER_GROUP + loc_a,
                     jnp.where(row8 == 1, grp * EXPERTS_PER_GROUP + loc_b,
                               jnp.where(row8 == 2, valid, jnp.where(row8 == 3, n_rows, 0.0))))
    meta_ref[...] = meta.astype(jnp.int32)


def _route(bid):
    return pl.pallas_call(
        _route_kernel,
        out_shape=(jax.ShapeDtypeStruct((N_TILES, 1, TILE), jnp.int32),
                   jax.ShapeDtypeStruct((8, 128), jnp.int32)),
        compiler_params=pltpu.CompilerParams(vmem_limit_bytes=VMEM_LIMIT),
        name="moe_route",
    )(bid)


def _moe_kernel(meta_ref, pos_ref, h2_hbm, wga_ref, wua_ref, wda_ref, wgb_ref, wub_ref, wdb_ref,
                y_ref, src_ref, xbuf, sem, wga_s, wua_s, wda_s, wgb_s, wub_s, wdb_s):
    j = pl.program_id(0)

    def row_copy(tile, r, slot):
        tok = src_ref[tile * ROW_TILE + r]
        return pltpu.make_async_copy(h2_hbm.at[pl.ds(tok, 1), :], xbuf.at[slot, pl.ds(r, 1), :], sem.at[slot])

    def issue_rows(tile, slot, lo, hi):
        for r in range(lo, hi):
            row_copy(tile, r, slot).start()

    def wait(slot):
        pltpu.make_async_copy(h2_hbm.at[pl.ds(0, ROW_TILE), :], xbuf.at[slot], sem.at[slot]).wait()

    @pl.when(j == 0)
    def _():
        def fill(t, c):
            def body(r, c2):
                p = t * ROW_TILE + r
                src_ref[p] = p % T_ALL
                return c2
            n = meta_ref[3, t]
            lax.fori_loop(n, jnp.where(meta_ref[2, t] == 1, ROW_TILE, n), body, 0)
            return c
        lax.fori_loop(0, N_ROW_TILES, fill, 0)

        def invert(t, c):
            src_ref[pos_ref[t]] = t
            return c
        lax.fori_loop(0, T_ALL, invert, 0, unroll=8)

        @pl.when(meta_ref[2, 0] == 1)
        def _():
            def body(g, c):
                for k in range(8):
                    row_copy(0, g * 8 + k, 0).start()
                return c
            lax.fori_loop(0, ROW_TILE // 8, body, 0)

    nxt = jnp.minimum(j + 1, N_ROW_TILES - 1)
    has_next = (j + 1 < N_ROW_TILES) & (meta_ref[2, nxt] == 1)
    valid = meta_ref[2, j] == 1
    prev = jnp.maximum(j - 1, 0)

    @pl.when(valid & ((j == 0) | (meta_ref[0, j] != meta_ref[0, prev])))
    def _():
        wga_s[...] = wga_ref[0].astype(BF16)
        wua_s[...] = wua_ref[0].astype(BF16)
        wda_s[...] = wda_ref[0].astype(BF16)

    @pl.when(valid & ((j == 0) | (meta_ref[1, j] != meta_ref[1, prev])))
    def _():
        wgb_s[...] = wgb_ref[0].astype(BF16)
        wub_s[...] = wub_ref[0].astype(BF16)
        wdb_s[...] = wdb_ref[0].astype(BF16)

    def compute(fetch_next):
        slot = j % 2
        nslot = nxt % 2
        step = ROW_TILE // 8
        batches = iter(range(0, ROW_TILE, step))

        def fetch():
            if fetch_next:
                lo = next(batches)
                issue_rows(nxt, nslot, lo, lo + step)

        wait(slot)
        x = xbuf[slot, :, 0:D_MODEL].astype(BF16)
        gates = xbuf[slot, :, D_MODEL:H2_EXT]
        hg_a = _dot(x, wga_s[...])
        fetch()
        hu_a = _dot(x, wua_s[...])
        fetch()
        hg_b = _dot(x, wgb_s[...])
        fetch()
        hu_b = _dot(x, wub_s[...])
        fetch()
        act_a = (hg_a * jax.nn.sigmoid(hg_a) * hu_a * gates[:, 0:1]).astype(BF16)
        fetch()
        act_b = (hg_b * jax.nn.sigmoid(hg_b) * hu_b * gates[:, 1:2]).astype(BF16)
        fetch()
        y = _dot(act_a, wda_s[...])
        fetch()
        y = y + _dot(act_b, wdb_s[...])
        fetch()
        y_ref[...] = y

    @pl.when(valid & has_next)
    def _():
        compute(True)

    @pl.when(valid & jnp.logical_not(has_next))
    def _():
        compute(False)

    @pl.when(jnp.logical_not(valid))
    def _():
        y_ref[...] = jnp.zeros_like(y_ref)


def _moe(meta, pos, h2ext, w_gate, w_up, w_down):
    up_spec = lambda slot: pl.BlockSpec((1, D_MODEL, EXPERT_FF), lambda j, meta, pos: (meta[slot, j], 0, 0))
    down_spec = lambda slot: pl.BlockSpec((1, EXPERT_FF, D_MODEL), lambda j, meta, pos: (meta[slot, j], 0, 0))
    grid_spec = pltpu.PrefetchScalarGridSpec(
        num_scalar_prefetch=2,
        grid=(N_ROW_TILES,),
        in_specs=[pl.BlockSpec(memory_space=pl.ANY),
                  up_spec(0), up_spec(0), down_spec(0), up_spec(1), up_spec(1), down_spec(1)],
        out_specs=pl.BlockSpec((ROW_TILE, D_MODEL), lambda j, meta, pos: (j, 0)),
        scratch_shapes=[pltpu.SMEM((ROW_CAP,), jnp.int32),
                        pltpu.VMEM((2, ROW_TILE, H2_EXT), F32),
                        pltpu.SemaphoreType.DMA((2,)),
                        pltpu.VMEM((D_MODEL, EXPERT_FF), BF16), pltpu.VMEM((D_MODEL, EXPERT_FF), BF16),
                        pltpu.VMEM((EXPERT_FF, D_MODEL), BF16),
                        pltpu.VMEM((D_MODEL, EXPERT_FF), BF16), pltpu.VMEM((D_MODEL, EXPERT_FF), BF16),
                        pltpu.VMEM((EXPERT_FF, D_MODEL), BF16)])
    return pl.pallas_call(
        _moe_kernel,
        out_shape=jax.ShapeDtypeStruct((ROW_CAP, D_MODEL), F32),
        grid_spec=grid_spec,
        compiler_params=_cparams(("arbitrary",)),
        name="moe_experts",
    )(meta, pos, h2ext, w_gate, w_up, w_down, w_gate, w_up, w_down)


def _final_kernel(pos_ref, y_hbm, x1_ref, m_ref, gn_ref, op_ref, os_ref, ybuf, sem):
    i = pl.program_id(0)

    def row_copy(tile, r, slot):
        p = pos_ref[tile * TILE + r]
        return pltpu.make_async_copy(y_hbm.at[pl.ds(p, 1), :], ybuf.at[slot, pl.ds(r, 1), :], sem.at[slot])

    def issue(tile, slot):
        def body(r2, c):
            row_copy(tile, 2 * r2, slot).start(priority=0)
            row_copy(tile, 2 * r2 + 1, slot).start(priority=1)
            return c
        lax.fori_loop(0, TILE // 2, body, 0, unroll=4)

    def wait(slot):
        pltpu.make_async_copy(y_hbm.at[pl.ds(0, TILE)], ybuf.at[slot], sem.at[slot]).wait()

    @pl.when(i == 0)
    def _():
        issue(0, 0)

    @pl.when(i + 1 < N_TILES)
    def _():
        issue(i + 1, (i + 1) % 2)

    slot = i % 2
    wait(slot)
    out = x1_ref[...] + m_ref[0, 5:6, :] * _rms(ybuf[slot], gn_ref[3:4, :])

    @pl.when(i < N_TILES_P)
    def _():
        op_ref[...] = out

    @pl.when(i >= N_TILES_P)
    def _():
        os_ref[...] = out


def _final(pos, y_sorted, x1, mods3, g_norm):
    tps = DEC_SEQ // TILE
    grid_spec = pltpu.PrefetchScalarGridSpec(
        num_scalar_prefetch=1,
        grid=(N_TILES,),
        in_specs=[pl.BlockSpec(memory_space=pl.ANY),
                  pl.BlockSpec((TILE, D_MODEL), lambda i, pos: (i, 0)),
                  pl.BlockSpec((1, N_MOD, D_MODEL), lambda i, pos: (_mod_row_of_tile(i, tps, N_TILES_P), 0, 0)),
                  pl.BlockSpec((4, D_MODEL), lambda i, pos: (0, 0))],
        out_specs=(pl.BlockSpec((TILE, D_MODEL), lambda i, pos: (jnp.minimum(i, N_TILES_P - 1), 0)),
                   pl.BlockSpec((TILE, D_MODEL), lambda i, pos: (jnp.maximum(i - N_TILES_P, 0), 0))),
        scratch_shapes=[pltpu.VMEM((2, TILE, D_MODEL), F32), pltpu.SemaphoreType.DMA((2,))])
    return pl.pallas_call(
        _final_kernel,
        out_shape=(jax.ShapeDtypeStruct((T_PROMPT, D_MODEL), F32),
                   jax.ShapeDtypeStruct((T_SAMPLE, D_MODEL), F32)),
        grid_spec=grid_spec,
        compiler_params=_cparams(("arbitrary",)),
        name="moe_combine_final",
    )(pos, y_sorted, x1, mods3, g_norm)


def kernel(x_prompt, x_sample, state_C, state_n, state_m, c, c_ctx, w_ada, b_ada, g_norm, w_in, ml_gate_bias, ml_head_gain, hy_conv_w, hy_f_w1, hy_f_b1, hy_f_w2, hy_f_b2, hy_f_w3, hy_f_b3, hy_decay, hy_bias, w_out, w_rc, b_rc, w_rf, b_rf, w_gate, w_up, w_down):
    xp = x_prompt.reshape(T_PROMPT, D_MODEL)
    xs = x_sample.reshape(T_SAMPLE, D_MODEL)
    gn = g_norm[0]

    cv = jnp.concatenate([c_ctx[None, :], c, jnp.zeros((MOD_ROWS - 1 - DEC_BATCH, D_MODEL), F32)], axis=0)
    mods3 = _ada(cv, w_ada[0], b_ada[0]).reshape(MOD_ROWS, N_MOD, D_MODEL)

    w_in0 = w_in[0]
    w_qkvo, w_hy = _prep_in_weights(w_in0.T)
    wg = w_in0[:, ML_QKVO_COLS:ML_QKVO_COLS + ML_GATE_COLS]
    gbt = ml_gate_bias[0].reshape(ML_GATE_COLS, 1)
    proj, gates_t = _inproj(xp, xs, mods3, gn, w_qkvo, w_hy, wg.T, gbt)

    gain = ml_head_gain[0].reshape(1, ML_WIDTH)
    y_ml_p, c_new, n_new, m_new = _mlstm(proj, gates_t, gain, None, SEQ, BATCH, 0)
    state = (state_C[:, 0], state_n[:, 0], state_m[:, 0].reshape(DEC_BATCH, 2 * ML_HEADS, 1))
    y_ml_s, _, _, _ = _mlstm(proj, gates_t, gain, state, DEC_SEQ, DEC_BATCH, T_PROMPT // DEC_SEQ)

    w1p = jnp.pad(hy_f_w1[0], ((0, 128 - HY_EMB), (0, 0)))
    b1 = hy_f_b1[0].reshape(1, -1)
    b2 = hy_f_b2[0].reshape(1, -1)
    b3 = hy_f_b3[0].reshape(1, -1)
    dec = hy_decay[0].reshape(1, -1)
    z_parts = []
    for seq_len, n_seq, row_off, width, seqs in ((SEQ, BATCH, 0, SEQ, 4), (DEC_SEQ, DEC_BATCH, T_PROMPT, GRID_W, 2)):
        f, ft = _dft_mats(seq_len)
        coefs = _hyena_filters(seq_len, f, w1p, b1, hy_f_w2[0], b2, hy_f_w3[0], b3, dec)
        z_parts.append(_hyena(proj, hy_conv_w[0], coefs, hy_bias[0], f, ft, seq_len, n_seq, row_off, width, seqs))
    z_p, z_s = z_parts

    pad_r = ROUTER_ROWS - N_GROUPS - N_EXPERTS
    w_r = jnp.pad(jnp.concatenate([w_rc[0], w_rf[0]], axis=1).T, ((0, pad_r), (0, 0)))
    b_r = jnp.pad(jnp.concatenate([b_rc[0], b_rf[0]], axis=0), (0, pad_r)).reshape(ROUTER_ROWS, 1)
    x1, h2ext, bid = _outproj(xp, xs, y_ml_p, y_ml_s, z_p, z_s, mods3, gn, w_out[0].astype(BF16), w_r, b_r)

    pos3, meta = _route(bid)
    pos = pos3.reshape(T_ALL)
    y_sorted = _moe(meta, pos, h2ext, w_gate[0], w_up[0], w_down[0])
    y_p, y_s = _final(pos, y_sorted, x1, mods3, gn)

    new_c = c_new.reshape(BATCH, 1, 2, ML_HEADS, ML_HEAD_DIM, ML_HEAD_DIM)
    new_n = n_new.reshape(BATCH, 1, 2, ML_HEADS, ML_HEAD_DIM)
    new_m = m_new[:, :, 0].reshape(BATCH, 1, 2, ML_HEADS)
    return (y_p.reshape(BATCH, SEQ, D_MODEL), y_s.reshape(DEC_BATCH, DEC_SEQ, D_MODEL), new_c, new_n, new_m)
```

```python
import functools
import math

import jax
import jax.numpy as jnp
import numpy as np
from jax import lax
from jax.experimental import pallas as pl
from jax.experimental.pallas import tpu as pltpu

F32 = jnp.float32
BF16 = jnp.bfloat16

D_MODEL = 1024
BATCH = 16
SEQ = 256
DEC_BATCH = 4
DEC_SEQ = 1024
GRID_W = 64
ML_WIDTH = 512
ML_HEADS = 4
ML_HEAD_DIM = 128
HY_WIDTH = 512
HY_ORDER = 2
HY_EMB = 33
HY_BANDS = 16
HY_FILTER_HIDDEN = 64
HY_MOD_SHIFT = 0.05
N_GROUPS = 4
EXPERTS_PER_GROUP = 4
N_EXPERTS = 16
EXPERT_FF = 512
N_MOD = 6
EPS = 1e-6
ML_QKVO_COLS = 4 * ML_WIDTH
ML_GATE_COLS = 4 * ML_HEADS
HY_COLS = 3 * HY_WIDTH
MAIN_COLS = ML_QKVO_COLS + HY_COLS

T_PROMPT = BATCH * SEQ
T_SAMPLE = DEC_BATCH * DEC_SEQ
T_ALL = T_PROMPT + T_SAMPLE
TILE = 256
N_TILES_P = T_PROMPT // TILE
N_TILES = T_ALL // TILE
MOD_ROWS = 8
K_SCALE = ML_HEAD_DIM ** -0.5
VMEM_LIMIT = 56 * 1024 * 1024


def _cparams(sem):
    return pltpu.CompilerParams(dimension_semantics=sem, vmem_limit_bytes=VMEM_LIMIT)


def _split2(x):
    hi = x.astype(BF16)
    lo = (x - hi.astype(F32)).astype(BF16)
    return hi, lo


def _dot(a, b):
    return jnp.dot(a, b, preferred_element_type=F32)


def _dot_nt(a, b):
    return lax.dot_general(a, b, (((1,), (1,)), ((), ())), preferred_element_type=F32)


def _dot_tn(a, b):
    return lax.dot_general(a, b, (((0,), (0,)), ((), ())), preferred_element_type=F32)


def _dot3(a, b):
    ah, al = _split2(a)
    bh, bl = _split2(b)
    return _dot(ah, bh) + _dot(al, bh) + _dot(ah, bl)


def _dot3_nt(a, b):
    ah, al = _split2(a)
    bh, bl = _split2(b)
    return _dot_nt(ah, bh) + _dot_nt(al, bh) + _dot_nt(ah, bl)


def _dot_exact_rhs(x, t):
    x1 = x.astype(BF16)
    r1 = x - x1.astype(F32)
    x2 = r1.astype(BF16)
    x3 = (r1 - x2.astype(F32)).astype(BF16)
    return _dot(x1, t) + _dot(x2, t) + _dot(x3, t)


def _rms(x, g):
    return x * lax.rsqrt(jnp.mean(x * x, axis=-1, keepdims=True) + EPS) * g


def _mod_row_of_tile(i, tiles_per_sample_seq, n_prompt_tiles):
    return jnp.where(i < n_prompt_tiles, 0, 1 + (i - n_prompt_tiles) // tiles_per_sample_seq)


ADA_STREAMS = 4


def _ada_kernel(cv_ref, *refs):
    w_refs, b_ref, o_ref = refs[:ADA_STREAMS], refs[ADA_STREAMS], refs[ADA_STREAMS + 1]
    cv = cv_ref[...]
    s = cv * jax.nn.sigmoid(cv)
    sh, sl = _split2(s)
    wh, wl = _split2(jnp.concatenate([w[...] for w in w_refs], axis=0))
    both = _dot(jnp.concatenate([sh.astype(F32), sl.astype(F32)], axis=0).astype(BF16), wh)
    o_ref[...] = both[0:MOD_ROWS] + both[MOD_ROWS:] + _dot(sh, wl) + b_ref[...]


def _ada(cv, w_ada, b_ada):
    n = N_MOD * D_MODEL
    return pl.pallas_call(
        _ada_kernel,
        out_shape=jax.ShapeDtypeStruct((MOD_ROWS, n), F32),
        grid=(N_MOD,),
        in_specs=[pl.BlockSpec((MOD_ROWS, D_MODEL), lambda j: (0, 0))]
        + [pl.BlockSpec((D_MODEL // ADA_STREAMS, D_MODEL), functools.partial(lambda j, k: (k, j), k=k))
           for k in range(ADA_STREAMS)]
        + [pl.BlockSpec((1, D_MODEL), lambda j: (0, j))],
        out_specs=pl.BlockSpec((MOD_ROWS, D_MODEL), lambda j: (0, j)),
        compiler_params=_cparams(("arbitrary",)),
        name="ada_mod",
    )(cv, *([w_ada] * ADA_STREAMS), b_ada.reshape(1, n))


PREP_COLS = 512


def _prep_q_kernel(wt_ref, o_ref):
    o_ref[...] = wt_ref[...].T.astype(BF16)


def _prep_hy_kernel(wt_hbm, o_ref, buf, sem):
    start = pl.multiple_of(ML_QKVO_COLS + ML_GATE_COLS + pl.program_id(0) * PREP_COLS, 8)
    copy = pltpu.make_async_copy(wt_hbm.at[pl.ds(start, PREP_COLS), :], buf, sem)
    copy.start()
    copy.wait()
    o_ref[...] = buf[...].T.astype(BF16)


def _prep_in_weights(w_in_t):
    out_blk = pl.BlockSpec((D_MODEL, PREP_COLS), lambda j: (0, j))
    w_qkvo = pl.pallas_call(
        _prep_q_kernel,
        out_shape=jax.ShapeDtypeStruct((D_MODEL, ML_QKVO_COLS), BF16),
        grid=(ML_QKVO_COLS // PREP_COLS,),
        in_specs=[pl.BlockSpec((PREP_COLS, D_MODEL), lambda j: (j, 0))], out_specs=out_blk,
        compiler_params=_cparams(("arbitrary",)), name="prep_w_qkvo",
    )(w_in_t)
    w_hy = pl.pallas_call(
        _prep_hy_kernel,
        out_shape=jax.ShapeDtypeStruct((D_MODEL, HY_COLS), BF16),
        grid=(HY_COLS // PREP_COLS,),
        in_specs=[pl.BlockSpec(memory_space=pl.ANY)], out_specs=out_blk,
        scratch_shapes=[pltpu.VMEM((PREP_COLS, D_MODEL), F32), pltpu.SemaphoreType.DMA(())],
        compiler_params=_cparams(("arbitrary",)), name="prep_w_hy",
    )(w_in_t)
    return w_qkvo, w_hy


def _log_sigmoid(x):
    return jnp.minimum(x, 0.0) - jnp.log1p(jnp.exp(-jnp.abs(x)))


def _rows_to_cols(rows):
    ri = lax.broadcasted_iota(jnp.int32, (TILE, TILE), 0)
    ci = lax.broadcasted_iota(jnp.int32, (TILE, TILE), 1)
    eye = jnp.where(ri == ci, 1.0, 0.0).astype(BF16)
    p1 = rows.astype(BF16)
    r1 = rows - p1.astype(F32)
    p2 = r1.astype(BF16)
    p3 = (r1 - p2.astype(F32)).astype(BF16)
    return _dot_nt(eye, p1) + _dot_nt(eye, p2) + _dot_nt(eye, p3)


BIG_TILE = 2 * TILE
N_BIG_P = T_PROMPT // BIG_TILE
N_BIG = T_ALL // BIG_TILE


def _inproj_kernel(xp_ref, xs_ref, m_ref, gn_ref, wq_ref, wh_ref, wgt_ref, gbt_ref, proj_ref, gatet_ref):
    is_p = pl.program_id(0) < N_BIG_P
    halves = [slice(r * TILE, (r + 1) * TILE) for r in range(BIG_TILE // TILE)]
    hs = [_rms(jnp.where(is_p, xp_ref[rows, :], xs_ref[rows, :]), gn_ref[0:1, :]) * (1.0 + m_ref[0, 1:2, :])
          + m_ref[0, 0:1, :] for rows in halves]
    hbs = [h.astype(BF16) for h in hs]
    cb = 512
    for j in range(ML_QKVO_COLS // cb):
        for rows, hb in zip(halves, hbs):
            proj_ref[rows, j * cb:(j + 1) * cb] = _dot(hb, wq_ref[:, j * cb:(j + 1) * cb]).astype(BF16)
    for j in range(HY_COLS // cb):
        lo = ML_QKVO_COLS + j * cb
        for rows, hb in zip(halves, hbs):
            proj_ref[rows, lo:lo + cb] = _dot(hb, wh_ref[:, j * cb:(j + 1) * cb]).astype(BF16)
    wth, wtl = _split2(wgt_ref[...])
    gts = []
    for h, hb in zip(hs, hbs):
        hl = (h - hb.astype(F32)).astype(BF16)
        gt = _dot_nt(wth, hb) + _dot_nt(wth, hl) + _dot_nt(wtl, hb) + gbt_ref[...]
        row = lax.broadcasted_iota(jnp.int32, gt.shape, 0)
        gts.append(jnp.where((row % 8) >= 4, _log_sigmoid(gt), gt))
    for r, gt in enumerate(gts):
        gatet_ref[r] = gt


def _inproj(xp, xs, mods3, g_norm, w_qkvo, w_hy, wgt, gbt):
    tps = DEC_SEQ // BIG_TILE
    per = BIG_TILE // TILE
    return pl.pallas_call(
        _inproj_kernel,
        out_shape=(jax.ShapeDtypeStruct((T_ALL, MAIN_COLS), BF16),
                   jax.ShapeDtypeStruct((N_TILES, ML_GATE_COLS, TILE), F32)),
        grid=(N_BIG,),
        in_specs=[pl.BlockSpec((BIG_TILE, D_MODEL), lambda i: (jnp.minimum(i, N_BIG_P - 1), 0)),
                  pl.BlockSpec((BIG_TILE, D_MODEL), lambda i: (jnp.maximum(i - N_BIG_P, 0), 0)),
                  pl.BlockSpec((1, N_MOD, D_MODEL), lambda i: (_mod_row_of_tile(i, tps, N_BIG_P), 0, 0)),
                  pl.BlockSpec((4, D_MODEL), lambda i: (0, 0)),
                  pl.BlockSpec((D_MODEL, ML_QKVO_COLS), lambda i: (0, 0)),
                  pl.BlockSpec((D_MODEL, HY_COLS), lambda i: (0, 0)),
                  pl.BlockSpec((ML_GATE_COLS, D_MODEL), lambda i: (0, 0)),
                  pl.BlockSpec((ML_GATE_COLS, 1), lambda i: (0, 0))],
        out_specs=(pl.BlockSpec((BIG_TILE, MAIN_COLS), lambda i: (i, 0)),
                   pl.BlockSpec((per, ML_GATE_COLS, TILE), lambda i: (i, 0, 0))),
        compiler_params=_cparams(("arbitrary",)),
        name="in_proj",
    )(xp, xs, mods3, g_norm, w_qkvo, w_hy, wgt, gbt)


ST_ROWS = ML_HEAD_DIM + 16


def _mlstm_kernel(*refs, seq_len, has_state):
    if has_state:
        (q_ref, k_ref, v_ref, o_ref, gt_ref, gain_ref, c0_ref, n0_ref, m0_ref,
         y_ref, c_ref, n_ref, m_ref, vt_ref, hf_ref, hb_ref, st_ref, ms_ref) = refs
    else:
        (q_ref, k_ref, v_ref, o_ref, gt_ref, gain_ref,
         y_ref, c_ref, n_ref, m_ref, vt_ref, hf_ref, hb_ref, st_ref, ms_ref) = refs
    ch = TILE
    nc = seq_len // ch
    hd = ML_HEAD_DIM
    key = lax.broadcasted_iota(jnp.int32, (ch, ch), 0)
    qry = lax.broadcasted_iota(jnp.int32, (ch, ch), 1)
    key_le = key <= qry
    key_ge = key >= qry
    t_le = jnp.where(key_le, 1.0, 0.0).astype(BF16)
    t_ge = jnp.where(key_ge, 1.0, 0.0).astype(BF16)
    sub16 = lax.broadcasted_iota(jnp.int32, (16, ch), 0)
    ln_scale = math.log(K_SCALE)

    for c in range(nc):
        for h in range(ML_HEADS):
            cols = slice(h * hd, (h + 1) * hd)
            vt_ref[c, cols, :] = v_ref[c * ch:(c + 1) * ch, cols].astype(F32).T.astype(BF16)

    for d in range(2):
        for h in range(ML_HEADS):
            r = d * ML_HEADS + h
            st_ref[r] = jnp.zeros((ST_ROWS, hd), F32)
            if has_state:
                st_ref[r, 0:hd, :] = c0_ref[0, d, h].T
                st_ref[r, hd:hd + 1, :] = n0_ref[0, d, h:h + 1, :]
                ms_ref[r] = jnp.broadcast_to(m0_ref[0, r:r + 1, :], (1, ch))
            else:
                ms_ref[r] = jnp.zeros((1, ch), F32)

    def step(t, carry):
        for d in range(2):
            c = t if d == 0 else nc - 1 - t
            rows = pl.ds(pl.multiple_of(c * ch, ch), ch)
            grow = gt_ref[c]
            brow_all = _dot_exact_rhs(grow, t_le if d == 0 else t_ge)
            ccol_all = _rows_to_cols(grow - pltpu.roll(brow_all, ML_GATE_COLS - ML_HEADS, axis=0))
            mask = key_le if d == 0 else key_ge
            hacc_ref = hf_ref if d == 0 else hb_ref
            heads = range(ML_HEADS)
            regs = [d * ML_HEADS + h for h in heads]
            colss = [slice(h * hd, (h + 1) * hd) for h in heads]
            qs = [q_ref[rows, cols] for cols in colss]
            ks = [k_ref[rows, cols] for cols in colss]
            vts = [vt_ref[c, cols, :] for cols in colss]
            sts = [st_ref[r] for r in regs]
            m_prevs = [ms_ref[r] for r in regs]
            b_rows = [brow_all[(1 + 2 * d) * ML_HEADS + h:(1 + 2 * d) * ML_HEADS + h + 1, :] for h in heads]
            ig_rows = [grow[2 * d * ML_HEADS + h:2 * d * ML_HEADS + h + 1, :] for h in heads]
            qks = [_dot_nt(k, q) for k, q in zip(ks, qs)]
            iqs = [_dot_nt(st.astype(BF16), q) for st, q in zip(sts, qs)]
            ss, sc_inters, m_poss = [], [], []
            for h in heads:
                icol = 2 * d * ML_HEADS + h
                c_col = ccol_all[:, icol:icol + 1]
                logd = jnp.where(mask, b_rows[h] + c_col, -jnp.inf)
                inter = b_rows[h] + m_prevs[h]
                m_pos = jnp.maximum(inter, jnp.max(logd, axis=0, keepdims=True))
                ss.append(qks[h] * jnp.exp(logd - (m_pos - ln_scale)))
                sc_inters.append(jnp.exp(inter - m_pos))
                m_poss.append(m_pos)
            pvs = [_dot(vt, s.astype(BF16)) for vt, s in zip(vts, ss)]
            for h in heads:
                num = sc_inters[h] * iqs[h][0:hd] + pvs[h]
                den = sc_inters[h] * iqs[h][hd:hd + 1] + jnp.sum(ss[h], axis=0, keepdims=True)
                hacc_ref[c, colss[h], :] = num * (1.0 / jnp.maximum(jnp.abs(den), jnp.exp(-m_poss[h])))
            lhss, decays = [], []
            for h in heads:
                b_row = b_rows[h]
                b_last = b_row[:, ch - 1:ch] if d == 0 else b_row[:, 0:1]
                logw = b_last - b_row + ig_rows[h]
                m_new = jnp.maximum(b_last + m_prevs[h], jnp.max(logw, axis=1, keepdims=True))
                w = jnp.exp(logw - (m_new - ln_scale))
                decays.append(jnp.exp(b_last + m_prevs[h] - m_new))
                lhss.append(jnp.concatenate([(vts[h].astype(F32) * w).astype(BF16),
                                             jnp.where(sub16 == 0, w, 0.0).astype(BF16)], axis=0))
                ms_ref[regs[h]] = m_new
            upds = [_dot(lhs, k) for lhs, k in zip(lhss, ks)]
            for h in heads:
                st_ref[regs[h]] = decays[h][:, 0:hd] * sts[h] + upds[h]
        return carry

    lax.fori_loop(0, nc, step, 0)

    for d in range(2):
        for h in range(ML_HEADS):
            r = d * ML_HEADS + h
            c_ref[0, d, h] = st_ref[r, 0:hd, :].T
            n_ref[0, d, h:h + 1, :] = st_ref[r, hd:hd + 1, :]
            m_ref[0, r:r + 1, :] = ms_ref[r][:, 0:hd]
    for c in range(nc):
        for h in range(ML_HEADS):
            cols = slice(h * hd, (h + 1) * hd)
            ht = hf_ref[c, cols, :] + hb_ref[c, cols, :]
            ht = ht * lax.rsqrt(jnp.mean(ht * ht, axis=0, keepdims=True) + EPS)
            rows = slice(c * ch, (c + 1) * ch)
            y = ht.T * gain_ref[:, cols] * jax.nn.sigmoid(o_ref[rows, cols].astype(F32))
            y_ref[rows, cols] = y.astype(BF16)


def _mlstm(proj, gates_t, gain, state, seq_len, n_seq, row_block_off):
    has_state = state is not None
    tiles = seq_len // TILE
    off = row_block_off
    qkvo_specs = [pl.BlockSpec((seq_len, ML_WIDTH), functools.partial(lambda b, j: (off + b, j), j=j))
                  for j in range(4)]
    in_specs = qkvo_specs + [
        pl.BlockSpec((tiles, ML_GATE_COLS, TILE), lambda b: (off + b, 0, 0)),
        pl.BlockSpec((1, ML_WIDTH), lambda b: (0, 0)),
    ]
    args = [proj, proj, proj, proj, gates_t, gain]
    if has_state:
        c0, n0, m0 = state
        in_specs += [
            pl.BlockSpec((1, 2, ML_HEADS, ML_HEAD_DIM, ML_HEAD_DIM), lambda b: (b, 0, 0, 0, 0)),
            pl.BlockSpec((1, 2, ML_HEADS, ML_HEAD_DIM), lambda b: (b, 0, 0, 0)),
            pl.BlockSpec((1, 2 * ML_HEADS, 1), lambda b: (b, 0, 0)),
        ]
        args += [c0, n0, m0]
    out_shape = (jax.ShapeDtypeStruct((n_seq * seq_len, ML_WIDTH), BF16),
                 jax.ShapeDtypeStruct((n_seq, 2, ML_HEADS, ML_HEAD_DIM, ML_HEAD_DIM), F32),
                 jax.ShapeDtypeStruct((n_seq, 2, ML_HEADS, ML_HEAD_DIM), F32),
                 jax.ShapeDtypeStruct((n_seq, 2 * ML_HEADS, ML_HEAD_DIM), F32))
    out_specs = (pl.BlockSpec((seq_len, ML_WIDTH), lambda b: (b, 0)),
                 pl.BlockSpec((1, 2, ML_HEADS, ML_HEAD_DIM, ML_HEAD_DIM), lambda b: (b, 0, 0, 0, 0)),
                 pl.BlockSpec((1, 2, ML_HEADS, ML_HEAD_DIM), lambda b: (b, 0, 0, 0)),
                 pl.BlockSpec((1, 2 * ML_HEADS, ML_HEAD_DIM), lambda b: (b, 0, 0)))
    scratch = [pltpu.VMEM((tiles, ML_WIDTH, TILE), BF16),
               pltpu.VMEM((tiles, ML_WIDTH, TILE), F32), pltpu.VMEM((tiles, ML_WIDTH, TILE), F32),
               pltpu.VMEM((2 * ML_HEADS, ST_ROWS, ML_HEAD_DIM), F32),
               pltpu.VMEM((2 * ML_HEADS, 1, TILE), F32)]
    return pl.pallas_call(
        functools.partial(_mlstm_kernel, seq_len=seq_len, has_state=has_state),
        out_shape=out_shape, grid=(n_seq,), in_specs=in_specs, out_specs=out_specs,
        scratch_shapes=scratch, compiler_params=_cparams(("arbitrary",)),
        name=f"mlstm_{seq_len}",
    )(*args)


def _dft_mats(seq_len):
    k = np.arange(seq_len, dtype=np.int64)[:, None]
    d = np.arange(seq_len, dtype=np.int64)[None, :]
    ang = np.pi * ((k * d) % (2 * seq_len)).astype(np.float64) / seq_len
    sinm = np.sin(ang)
    sinm[0, :] = np.where(d[0] % 2 == 0, 1.0, -1.0)
    f = np.concatenate([np.cos(ang), sinm], axis=0).astype(np.float32)
    return jnp.asarray(f).astype(BF16), jnp.asarray(np.ascontiguousarray(f.T)).astype(BF16)


def _filter_feats(seq_len):
    t = np.linspace(0.0, 1.0, seq_len, dtype=np.float64)[:, None]
    wpos = 2.0 * np.pi * np.arange(seq_len, dtype=np.float64)[:, None] / seq_len
    bands = np.linspace(1e-4, HY_BANDS - 1, HY_BANDS, dtype=np.float64)[None, :]
    z = np.concatenate([t, np.cos(bands * wpos), -np.sin(bands * wpos)], axis=-1)
    return jnp.asarray(np.pad(z, ((0, 0), (0, 128 - HY_EMB))).astype(np.float32))


def _filter_kernel(z_ref, w1_ref, b1_ref, w2_ref, b2_ref, w3_ref, b3_ref, dec_ref, f_ref,
                   a_ref, b_ref, d_ref, *, seq_len):
    n = 2 * seq_len
    oc = 2 * HY_WIDTH
    z = z_ref[...]
    h = jnp.sin(_dot3(z, w1_ref[...]) + b1_ref[...])
    h = jnp.sin(_dot3(h, w2_ref[...]) + b2_ref[...])
    t = z[:, 0:1]
    di = lax.broadcasted_iota(jnp.int32, (seq_len, 1), 0)
    sgn = jnp.where(di % 2 == 0, 1.0, -1.0)
    first = di == 0
    ssums, sdifs = [], []
    for o in range(HY_ORDER):
        cols = slice(o * oc, (o + 1) * oc)
        g = _dot3(h, w3_ref[:, cols]) + b3_ref[:, cols]
        g = g * (jnp.exp(-t * jnp.abs(dec_ref[:, cols])) + HY_MOD_SHIFT)
        ss = jnp.sum(g * g, axis=0, keepdims=True)
        inv = lax.rsqrt(ss[:, :HY_WIDTH] + ss[:, HY_WIDTH:] + EPS)
        hp = g[:, :HY_WIDTH] * inv
        hn = g[:, HY_WIDTH:] * inv
        ssums.append(hp + hn)
        sdifs.append(hp - hn)
    hcs = [_dot(f_ref[0:seq_len, :], s.astype(BF16)) for s in ssums]
    hss = [_dot(f_ref[seq_len:n, :], s.astype(BF16)) for s in sdifs]
    for o in range(HY_ORDER):
        nyq = jnp.sum(ssums[o] * sgn, axis=0, keepdims=True)
        a_ref[o] = hcs[o] * jnp.where(first, 1.0 / n, 2.0 / n)
        b_ref[o] = jnp.where(first, 0.0, hss[o] * (2.0 / n))
        d_ref[o] = jnp.where(first, nyq * (1.0 / n), hcs[o] * (2.0 / n))


def _hyena_filters(seq_len, f, w1p, b1, w2, b2, w3, b3, dec):
    z = _filter_feats(seq_len)
    out = jax.ShapeDtypeStruct((HY_ORDER, seq_len, HY_WIDTH), F32)
    return pl.pallas_call(
        functools.partial(_filter_kernel, seq_len=seq_len),
        out_shape=(out, out, out),
        compiler_params=pltpu.CompilerParams(vmem_limit_bytes=VMEM_LIMIT),
        name=f"hyena_filter_{seq_len}",
    )(z, w1p, b1, w2, b2, w3, b3, dec, f)


def _hyena_kernel(x1_ref, x2_ref, v_ref, cw1_ref, cw2_ref, cwv_ref, a_ref, b_ref, d_ref, bias_ref,
                  f_ref, ft_ref, z_ref, *, seq_len, width, seqs):
    rows = seqs * seq_len
    ti = lax.broadcasted_iota(jnp.int32, (rows, 1), 0)
    has_prev = (ti % width) != 0
    has_next = (ti % width) != (width - 1)

    def short_conv(x_ref, w_ref):
        x = x_ref[...].astype(F32)
        prev = jnp.where(has_prev, pltpu.roll(x, 1, axis=0), 0.0)
        nxt = jnp.where(has_next, pltpu.roll(x, rows - 1, axis=0), 0.0)
        return w_ref[0:1, :] * prev + w_ref[1:2, :] * x + w_ref[2:3, :] * nxt

    gates = (short_conv(x1_ref, cw1_ref), short_conv(x2_ref, cw2_ref))
    v = short_conv(v_ref, cwv_ref)
    sls = [slice(i * seq_len, (i + 1) * seq_len) for i in range(seqs)]
    zs = [v[sl] for sl in sls]
    for o in range(HY_ORDER):
        a, b, dd = a_ref[o], b_ref[o], d_ref[o]
        us = [_dot(f_ref[...], z.astype(BF16)) for z in zs]
        ys = []
        for u in us:
            ut = u[:seq_len]
            ub = u[seq_len:]
            ys.append(((ut * a - ub * b).astype(BF16), (ut * b + ub * dd).astype(BF16)))
        convs = [_dot(ft_ref[:, :seq_len], yt) + _dot(ft_ref[:, seq_len:], yb) for yt, yb in ys]
        zs = [gates[o][sl] * (y + bias_ref[o:o + 1, :] * z) for sl, y, z in zip(sls, convs, zs)]
    for sl, z in zip(sls, zs):
        z_ref[sl, :] = z.astype(BF16)


def _hyena(proj, conv_w, coefs, hy_bias, f, ft, seq_len, n_seq, row_off, width, seqs):
    cb = 256
    nblk = HY_WIDTH // cb
    base = ML_QKVO_COLS // cb
    rows = seqs * seq_len
    off = row_off // rows
    a, b, d = coefs

    def col_spec(part):
        return pl.BlockSpec((rows, cb), lambda j, s: (off + s, base + part * nblk + j))

    def w_spec(part):
        return pl.BlockSpec((3, cb), lambda j, s: (0, part * nblk + j))

    coef_spec = pl.BlockSpec((HY_ORDER, seq_len, cb), lambda j, s: (0, 0, j))
    return pl.pallas_call(
        functools.partial(_hyena_kernel, seq_len=seq_len, width=width, seqs=seqs),
        out_shape=jax.ShapeDtypeStruct((n_seq * seq_len, HY_WIDTH), BF16),
        grid=(nblk, n_seq // seqs),
        in_specs=[col_spec(0), col_spec(1), col_spec(2), w_spec(0), w_spec(1), w_spec(2),
                  coef_spec, coef_spec, coef_spec,
                  pl.BlockSpec((HY_ORDER, cb), lambda j, s: (0, j)),
                  pl.BlockSpec((2 * seq_len, seq_len), lambda j, s: (0, 0)),
                  pl.BlockSpec((seq_len, 2 * seq_len), lambda j, s: (0, 0))],
        out_specs=pl.BlockSpec((rows, cb), lambda j, s: (s, j)),
        compiler_params=_cparams(("arbitrary", "arbitrary")),
        name=f"hyena_conv_{seq_len}",
    )(proj, proj, proj, conv_w, conv_w, conv_w, a, b, d, hy_bias, f, ft)


def _first_max(x, n):
    mx = jnp.max(x, axis=0, keepdims=True)
    row = lax.broadcasted_iota(jnp.int32, x.shape, 0).astype(F32)
    idx = jnp.min(jnp.where(x == mx, row, float(n)), axis=0, keepdims=True)
    return mx, idx.astype(jnp.int32)


ROUTER_ROWS = 32
PAIRS_PER_GROUP = 6
N_BUCKETS = N_GROUPS * PAIRS_PER_GROUP
PAIR_SLOTS = ((0, 1), (0, 2), (0, 3), (1, 3), (1, 2), (3, 2))
LANES = 128
H2_EXT = D_MODEL + LANES
ROW_TILE = 256
ROW_CAP = T_ALL + N_BUCKETS * ROW_TILE
N_ROW_TILES = ROW_CAP // ROW_TILE


def _outproj_kernel(xp_ref, xs_ref, yp_ref, ys_ref, zp_ref, zs_ref, m_ref, gn_ref, wo_ref, wr_ref, br_ref,
                    x1_ref, h2_ref, bid_ref):
    is_p = pl.program_id(0) < N_BIG_P
    wrh, wrl = _split2(wr_ref[...])
    halves = [slice(r * TILE, (r + 1) * TILE) for r in range(BIG_TILE // TILE)]
    ys = [_dot(jnp.where(is_p, yp_ref[rows, :], ys_ref[rows, :]), wo_ref[0:ML_WIDTH, :])
          + _dot(jnp.where(is_p, zp_ref[rows, :], zs_ref[rows, :]), wo_ref[ML_WIDTH:, :]) for rows in halves]
    h2s = []
    for rows, y in zip(halves, ys):
        x = jnp.where(is_p, xp_ref[rows, :], xs_ref[rows, :])
        x1 = x + m_ref[0, 2:3, :] * _rms(y, gn_ref[1:2, :])
        x1_ref[rows, :] = x1
        h2 = _rms(x1, gn_ref[2:3, :]) * (1.0 + m_ref[0, 4:5, :]) + m_ref[0, 3:4, :]
        h2_ref[rows, 0:D_MODEL] = h2
        h2s.append(h2)
    logits = []
    for h2 in h2s:
        h2h, h2l = _split2(h2)
        logits.append(_dot_nt(wrh, h2h) + _dot_nt(wrh, h2l) + _dot_nt(wrl, h2h) + br_ref[...])
    routed = [_route_tile(lg) for lg in logits]
    for r, (rows, (gate_rows, bucket)) in enumerate(zip(halves, routed)):
        h2_ref[rows, D_MODEL:H2_EXT] = jnp.zeros((TILE, LANES), F32)
        h2_ref[rows, D_MODEL:D_MODEL + 8] = _rows_to_cols(gate_rows)
        bid_ref[r] = bucket


def _route_tile(logits):
    lc = logits[0:N_GROUPS]
    mx, gi = _first_max(lc, N_GROUPS)
    p_grp = 1.0 / jnp.sum(jnp.exp(lc - mx), axis=0, keepdims=True)
    lsel = jnp.zeros((EXPERTS_PER_GROUP, TILE), F32)
    for g in range(N_GROUPS):
        lo = N_GROUPS + g * EXPERTS_PER_GROUP
        lsel = jnp.where(gi == g, logits[lo:lo + EXPERTS_PER_GROUP], lsel)
    l1, i1 = _first_max(lsel, EXPERTS_PER_GROUP)
    sub4 = lax.broadcasted_iota(jnp.int32, lsel.shape, 0)
    l2, i2 = _first_max(jnp.where(sub4 == i1, -jnp.inf, lsel), EXPERTS_PER_GROUP)
    e2 = jnp.exp(l2 - l1)
    w1 = p_grp / (1.0 + e2)
    w2 = p_grp * e2 / (1.0 + e2)
    lo_e = jnp.minimum(i1, i2)
    hi_e = jnp.maximum(i1, i2)
    pair = jnp.where(lo_e == 0, hi_e - 1, jnp.where(lo_e == 1, jnp.where(hi_e == 3, 3, 4), 5))
    slot_a = jnp.where(pair == 5, hi_e, lo_e)
    first_in_a = i1 == slot_a
    w_a = jnp.where(first_in_a, w1, w2)
    w_b = jnp.where(first_in_a, w2, w1)
    sub = lax.broadcasted_iota(jnp.int32, (8, TILE), 0)
    gate_rows = jnp.where(sub == 0, w_a, jnp.where(sub == 1, w_b, 0.0))
    return gate_rows, gi * PAIRS_PER_GROUP + pair


def _outproj(xp, xs, yp, ys, zp, zs, mods3, g_norm, w_out, w_r, b_r):
    tps = DEC_SEQ // BIG_TILE
    per = BIG_TILE // TILE
    pidx = lambda i: (jnp.minimum(i, N_BIG_P - 1), 0)
    sidx = lambda i: (jnp.maximum(i - N_BIG_P, 0), 0)
    return pl.pallas_call(
        _outproj_kernel,
        out_shape=(jax.ShapeDtypeStruct((T_ALL, D_MODEL), F32),
                   jax.ShapeDtypeStruct((T_ALL, H2_EXT), F32),
                   jax.ShapeDtypeStruct((N_TILES, 1, TILE), jnp.int32)),
        grid=(N_BIG,),
        in_specs=[pl.BlockSpec((BIG_TILE, D_MODEL), pidx), pl.BlockSpec((BIG_TILE, D_MODEL), sidx),
                  pl.BlockSpec((BIG_TILE, ML_WIDTH), pidx), pl.BlockSpec((BIG_TILE, ML_WIDTH), sidx),
                  pl.BlockSpec((BIG_TILE, HY_WIDTH), pidx), pl.BlockSpec((BIG_TILE, HY_WIDTH), sidx),
                  pl.BlockSpec((1, N_MOD, D_MODEL), lambda i: (_mod_row_of_tile(i, tps, N_BIG_P), 0, 0)),
                  pl.BlockSpec((4, D_MODEL), lambda i: (0, 0)),
                  pl.BlockSpec((D_MODEL, D_MODEL), lambda i: (0, 0)),
                  pl.BlockSpec((ROUTER_ROWS, D_MODEL), lambda i: (0, 0)),
                  pl.BlockSpec((ROUTER_ROWS, 1), lambda i: (0, 0))],
        out_specs=(pl.BlockSpec((BIG_TILE, D_MODEL), lambda i: (i, 0)),
                   pl.BlockSpec((BIG_TILE, H2_EXT), lambda i: (i, 0)),
                   pl.BlockSpec((per, 1, TILE), lambda i: (i, 0, 0))),
        compiler_params=_cparams(("arbitrary",)),
        name="out_proj_router",
    )(xp, xs, yp, ys, zp, zs, mods3, g_norm, w_out, w_r, b_r)


def _route_kernel(bid_ref, pos_ref, meta_ref):
    nb = 32
    tm = float(ROW_TILE)
    sub = lax.broadcasted_iota(jnp.int32, (nb, TILE), 0)
    ri = lax.broadcasted_iota(jnp.int32, (TILE, TILE), 0)
    ci = lax.broadcasted_iota(jnp.int32, (TILE, TILE), 1)
    before = jnp.where(ri < ci, 1.0, 0.0).astype(BF16)

    def onehot(blk):
        return jnp.where(sub == bid_ref[blk], 1.0, 0.0)

    zeros = jnp.zeros((nb, 1), F32)
    cnt = lax.fori_loop(0, N_TILES, lambda blk, c: c + jnp.sum(onehot(blk), axis=1, keepdims=True), zeros)
    padded = jnp.floor((cnt + (tm - 1.0)) * (1.0 / tm)) * tm
    r32 = lax.broadcasted_iota(jnp.int32, (nb, nb), 0)
    c32 = lax.broadcasted_iota(jnp.int32, (nb, nb), 1)
    padded_row = jnp.sum(jnp.where(r32 == c32, padded, 0.0), axis=0, keepdims=True)
    offs = jnp.sum(jnp.where(c32 < r32, padded_row, 0.0), axis=1, keepdims=True)
    ends = offs + padded

    def place(blk, seen):
        oh = onehot(blk)
        rank = _dot(oh.astype(BF16), before)
        pos = jnp.sum(oh * (rank + seen + offs), axis=0, keepdims=True)
        pos_ref[blk] = pos.astype(jnp.int32)
        return seen + jnp.sum(oh, axis=1, keepdims=True)

    lax.fori_loop(0, N_TILES, place, zeros)

    start = lax.broadcasted_iota(jnp.int32, (nb, 128), 1).astype(F32) * tm
    bsub = lax.broadcasted_iota(jnp.int32, (nb, 128), 0)
    done = jnp.where((bsub < N_BUCKETS) & (ends <= start), 1.0, 0.0)
    tb = jnp.sum(done, axis=0, keepdims=True)
    valid = jnp.where(tb < N_BUCKETS, 1.0, 0.0)
    tbc = jnp.minimum(tb, N_BUCKETS - 1.0)
    grp = jnp.floor((tbc + 0.5) * (1.0 / PAIRS_PER_GROUP))
    pair = tbc - PAIRS_PER_GROUP * grp
    loc_a = jnp.zeros_like(pair)
    loc_b = jnp.zeros_like(pair)
    for k, (sa, sb) in enumerate(PAIR_SLOTS):
        loc_a = jnp.where(pair == k, float(sa), loc_a)
        loc_b = jnp.where(pair == k, float(sb), loc_b)
    mine = bsub.astype(F32) == tbc
    used = jnp.sum(jnp.where(mine, offs + cnt, 0.0), axis=0, keepdims=True)
    n_rows = jnp.clip(used - start[0:1], 0.0, tm) * valid
    row8 = lax.broadcasted_iota(jnp.int32, (8, 128), 0)
    meta = jnp.where(row8 == 0, grp * EXPERTS_PER_GROUP + loc_a,
                     jnp.where(row8 == 1, grp * EXPERTS_PER_GROUP + loc_b,
                               jnp.where(row8 == 2, valid, jnp.where(row8 == 3, n_rows, 0.0))))
    meta_ref[...] = meta.astype(jnp.int32)


def _route(bid):
    return pl.pallas_call(
        _route_kernel,
        out_shape=(jax.ShapeDtypeStruct((N_TILES, 1, TILE), jnp.int32),
                   jax.ShapeDtypeStruct((8, 128), jnp.int32)),
        compiler_params=pltpu.CompilerParams(vmem_limit_bytes=VMEM_LIMIT),
        name="moe_route",
    )(bid)


def _moe_kernel(meta_ref, pos_ref, h2_hbm, wga_ref, wua_ref, wda_ref, wgb_ref, wub_ref, wdb_ref,
                y_ref, src_ref, xbuf, sem, wga_s, wua_s, wda_s, wgb_s, wub_s, wdb_s):
    j = pl.program_id(0)

    def row_copy(tile, r, slot):
        tok = src_ref[tile * ROW_TILE + r]
        return pltpu.make_async_copy(h2_hbm.at[pl.ds(tok, 1), :], xbuf.at[slot, pl.ds(r, 1), :], sem.at[slot])

    def issue_rows(tile, slot, lo, hi):
        for r in range(lo, hi):
            row_copy(tile, r, slot).start(priority=1)

    def wait(slot):
        pltpu.make_async_copy(h2_hbm.at[pl.ds(0, ROW_TILE), :], xbuf.at[slot], sem.at[slot]).wait()

    @pl.when(j == 0)
    def _():
        def fill(t, c):
            def body(r, c2):
                p = t * ROW_TILE + r
                src_ref[p] = p % T_ALL
                return c2
            n = meta_ref[3, t]
            lax.fori_loop(n, jnp.where(meta_ref[2, t] == 1, ROW_TILE, n), body, 0)
            return c
        lax.fori_loop(0, N_ROW_TILES, fill, 0)

        def invert(t, c):
            src_ref[pos_ref[t]] = t
            return c
        lax.fori_loop(0, T_ALL, invert, 0, unroll=8)

        @pl.when(meta_ref[2, 0] == 1)
        def _():
            def body(g, c):
                for k in range(8):
                    row_copy(0, g * 8 + k, 0).start()
                return c
            lax.fori_loop(0, ROW_TILE // 8, body, 0)

    nxt = jnp.minimum(j + 1, N_ROW_TILES - 1)
    has_next = (j + 1 < N_ROW_TILES) & (meta_ref[2, nxt] == 1)
    valid = meta_ref[2, j] == 1
    prev = jnp.maximum(j - 1, 0)

    @pl.when(valid & ((j == 0) | (meta_ref[0, j] != meta_ref[0, prev])))
    def _():
        wga_s[...] = wga_ref[0].astype(BF16)
        wua_s[...] = wua_ref[0].astype(BF16)
        wda_s[...] = wda_ref[0].astype(BF16)

    @pl.when(valid & ((j == 0) | (meta_ref[1, j] != meta_ref[1, prev])))
    def _():
        wgb_s[...] = wgb_ref[0].astype(BF16)
        wub_s[...] = wub_ref[0].astype(BF16)
        wdb_s[...] = wdb_ref[0].astype(BF16)

    def compute(fetch_next):
        slot = j % 2
        nslot = nxt % 2
        step = ROW_TILE // 8
        batches = iter(range(0, ROW_TILE, step))

        def fetch():
            if fetch_next:
                lo = next(batches)
                issue_rows(nxt, nslot, lo, lo + step)

        wait(slot)
        x = xbuf[slot, :, 0:D_MODEL].astype(BF16)
        gates = xbuf[slot, :, D_MODEL:H2_EXT]
        hg_a = _dot(x, wga_s[...])
        fetch()
        hu_a = _dot(x, wua_s[...])
        fetch()
        hg_b = _dot(x, wgb_s[...])
        fetch()
        hu_b = _dot(x, wub_s[...])
        fetch()
        act_a = (hg_a * jax.nn.sigmoid(hg_a) * hu_a * gates[:, 0:1]).astype(BF16)
        fetch()
        act_b = (hg_b * jax.nn.sigmoid(hg_b) * hu_b * gates[:, 1:2]).astype(BF16)
        fetch()
        y = _dot(act_a, wda_s[...])
        fetch()
        y = y + _dot(act_b, wdb_s[...])
        fetch()
        y_ref[...] = y

    @pl.when(valid & has_next)
    def _():
        compute(True)

    @pl.when(valid & jnp.logical_not(has_next))
    def _():
        compute(False)

    @pl.when(jnp.logical_not(valid))
    def _():
        y_ref[...] = jnp.zeros_like(y_ref)


def _moe(meta, pos, h2ext, w_gate, w_up, w_down):
    up_spec = lambda slot: pl.BlockSpec((1, D_MODEL, EXPERT_FF), lambda j, meta, pos: (meta[slot, j], 0, 0))
    down_spec = lambda slot: pl.BlockSpec((1, EXPERT_FF, D_MODEL), lambda j, meta, pos: (meta[slot, j], 0, 0))
    grid_spec = pltpu.PrefetchScalarGridSpec(
        num_scalar_prefetch=2,
        grid=(N_ROW_TILES,),
        in_specs=[pl.BlockSpec(memory_space=pl.ANY),
                  up_spec(0), up_spec(0), down_spec(0), up_spec(1), up_spec(1), down_spec(1)],
        out_specs=pl.BlockSpec((ROW_TILE, D_MODEL), lambda j, meta, pos: (j, 0)),
        scratch_shapes=[pltpu.SMEM((ROW_CAP,), jnp.int32),
                        pltpu.VMEM((2, ROW_TILE, H2_EXT), F32),
                        pltpu.SemaphoreType.DMA((2,)),
                        pltpu.VMEM((D_MODEL, EXPERT_FF), BF16), pltpu.VMEM((D_MODEL, EXPERT_FF), BF16),
                        pltpu.VMEM((EXPERT_FF, D_MODEL), BF16),
                        pltpu.VMEM((D_MODEL, EXPERT_FF), BF16), pltpu.VMEM((D_MODEL, EXPERT_FF), BF16),
                        pltpu.VMEM((EXPERT_FF, D_MODEL), BF16)])
    return pl.pallas_call(
        _moe_kernel,
        out_shape=jax.ShapeDtypeStruct((ROW_CAP, D_MODEL), F32),
        grid_spec=grid_spec,
        compiler_params=_cparams(("arbitrary",)),
        name="moe_experts",
    )(meta, pos, h2ext, w_gate, w_up, w_down, w_gate, w_up, w_down)


def _final_kernel(pos_ref, y_hbm, x1_ref, m_ref, gn_ref, op_ref, os_ref, ybuf, sem):
    i = pl.program_id(0)

    def row_copy(tile, r, slot):
        p = pos_ref[tile * TILE + r]
        return pltpu.make_async_copy(y_hbm.at[pl.ds(p, 1), :], ybuf.at[slot, pl.ds(r, 1), :], sem.at[slot])

    def issue(tile, slot):
        def body(r2, c):
            row_copy(tile, 2 * r2, slot).start(priority=0)
            row_copy(tile, 2 * r2 + 1, slot).start(priority=1)
            return c
        lax.fori_loop(0, TILE // 2, body, 0, unroll=4)

    def wait(slot):
        pltpu.make_async_copy(y_hbm.at[pl.ds(0, TILE)], ybuf.at[slot], sem.at[slot]).wait()

    @pl.when(i == 0)
    def _():
        issue(0, 0)

    @pl.when(i + 1 < N_TILES)
    def _():
        issue(i + 1, (i + 1) % 2)

    slot = i % 2
    wait(slot)
    out = x1_ref[...] + m_ref[0, 5:6, :] * _rms(ybuf[slot], gn_ref[3:4, :])

    @pl.when(i < N_TILES_P)
    def _():
        op_ref[...] = out

    @pl.when(i >= N_TILES_P)
    def _():
        os_ref[...] = out


def _final(pos, y_sorted, x1, mods3, g_norm):
    tps = DEC_SEQ // TILE
    grid_spec = pltpu.PrefetchScalarGridSpec(
        num_scalar_prefetch=1,
        grid=(N_TILES,),
        in_specs=[pl.BlockSpec(memory_space=pl.ANY),
                  pl.BlockSpec((TILE, D_MODEL), lambda i, pos: (i, 0)),
                  pl.BlockSpec((1, N_MOD, D_MODEL), lambda i, pos: (_mod_row_of_tile(i, tps, N_TILES_P), 0, 0)),
                  pl.BlockSpec((4, D_MODEL), lambda i, pos: (0, 0))],
        out_specs=(pl.BlockSpec((TILE, D_MODEL), lambda i, pos: (jnp.minimum(i, N_TILES_P - 1), 0)),
                   pl.BlockSpec((TILE, D_MODEL), lambda i, pos: (jnp.maximum(i - N_TILES_P, 0), 0))),
        scratch_shapes=[pltpu.VMEM((2, TILE, D_MODEL), F32), pltpu.SemaphoreType.DMA((2,))])
    return pl.pallas_call(
        _final_kernel,
        out_shape=(jax.ShapeDtypeStruct((T_PROMPT, D_MODEL), F32),
                   jax.ShapeDtypeStruct((T_SAMPLE, D_MODEL), F32)),
        grid_spec=grid_spec,
        compiler_params=_cparams(("arbitrary",)),
        name="moe_combine_final",
    )(pos, y_sorted, x1, mods3, g_norm)


def kernel(x_prompt, x_sample, state_C, state_n, state_m, c, c_ctx, w_ada, b_ada, g_norm, w_in, ml_gate_bias, ml_head_gain, hy_conv_w, hy_f_w1, hy_f_b1, hy_f_w2, hy_f_b2, hy_f_w3, hy_f_b3, hy_decay, hy_bias, w_out, w_rc, b_rc, w_rf, b_rf, w_gate, w_up, w_down):
    xp = x_prompt.reshape(T_PROMPT, D_MODEL)
    xs = x_sample.reshape(T_SAMPLE, D_MODEL)
    gn = g_norm[0]

    cv = jnp.concatenate([c_ctx[None, :], c, jnp.zeros((MOD_ROWS - 1 - DEC_BATCH, D_MODEL), F32)], axis=0)
    mods3 = _ada(cv, w_ada[0], b_ada[0]).reshape(MOD_ROWS, N_MOD, D_MODEL)

    w_in0 = w_in[0]
    w_qkvo, w_hy = _prep_in_weights(w_in0.T)
    wg = w_in0[:, ML_QKVO_COLS:ML_QKVO_COLS + ML_GATE_COLS]
    gbt = ml_gate_bias[0].reshape(ML_GATE_COLS, 1)
    proj, gates_t = _inproj(xp, xs, mods3, gn, w_qkvo, w_hy, wg.T, gbt)

    gain = ml_head_gain[0].reshape(1, ML_WIDTH)
    y_ml_p, c_new, n_new, m_new = _mlstm(proj, gates_t, gain, None, SEQ, BATCH, 0)
    state = (state_C[:, 0], state_n[:, 0], state_m[:, 0].reshape(DEC_BATCH, 2 * ML_HEADS, 1))
    y_ml_s, _, _, _ = _mlstm(proj, gates_t, gain, state, DEC_SEQ, DEC_BATCH, T_PROMPT // DEC_SEQ)

    w1p = jnp.pad(hy_f_w1[0], ((0, 128 - HY_EMB), (0, 0)))
    b1 = hy_f_b1[0].reshape(1, -1)
    b2 = hy_f_b2[0].reshape(1, -1)
    b3 = hy_f_b3[0].reshape(1, -1)
    dec = hy_decay[0].reshape(1, -1)
    z_parts = []
    for seq_len, n_seq, row_off, width, seqs in ((SEQ, BATCH, 0, SEQ, 4), (DEC_SEQ, DEC_BATCH, T_PROMPT, GRID_W, 2)):
        f, ft = _dft_mats(seq_len)
        coefs = _hyena_filters(seq_len, f, w1p, b1, hy_f_w2[0], b2, hy_f_w3[0], b3, dec)
        z_parts.append(_hyena(proj, hy_conv_w[0], coefs, hy_bias[0], f, ft, seq_len, n_seq, row_off, width, seqs))
    z_p, z_s = z_parts

    pad_r = ROUTER_ROWS - N_GROUPS - N_EXPERTS
    w_r = jnp.pad(jnp.concatenate([w_rc[0], w_rf[0]], axis=1).T, ((0, pad_r), (0, 0)))
    b_r = jnp.pad(jnp.concatenate([b_rc[0], b_rf[0]], axis=0), (0, pad_r)).reshape(ROUTER_ROWS, 1)
    x1, h2ext, bid = _outproj(xp, xs, y_ml_p, y_ml_s, z_p, z_s, mods3, gn, w_out[0].astype(BF16), w_r, b_r)

    pos3, meta = _route(bid)
    pos = pos3.reshape(T_ALL)
    y_sorted = _moe(meta, pos, h2ext, w_gate[0], w_up[0], w_down[0])
    y_p, y_s = _final(pos, y_sorted, x1, mods3, gn)

    new_c = c_new.reshape(BATCH, 1, 2, ML_HEADS, ML_HEAD_DIM, ML_HEAD_DIM)
    new_n = n_new.reshape(BATCH, 1, 2, ML_HEADS, ML_HEAD_DIM)
    new_m = m_new[:, :, 0].reshape(BATCH, 1, 2, ML_HEADS)
    return (y_p.reshape(BATCH, SEQ, D_MODEL), y_s.reshape(DEC_BATCH, DEC_SEQ, D_MODEL), new_c, new_n, new_m)
```

```python
import functools
import math

import jax
import jax.numpy as jnp
import numpy as np
from jax import lax
from jax.experimental import pallas as pl
from jax.experimental.pallas import tpu as pltpu

F32 = jnp.float32
BF16 = jnp.bfloat16

D_MODEL = 1024
BATCH = 16
SEQ = 256
DEC_BATCH = 4
DEC_SEQ = 1024
GRID_W = 64
ML_WIDTH = 512
ML_HEADS = 4
ML_HEAD_DIM = 128
HY_WIDTH = 512
HY_ORDER = 2
HY_EMB = 33
HY_BANDS = 16
HY_FILTER_HIDDEN = 64
HY_MOD_SHIFT = 0.05
N_GROUPS = 4
EXPERTS_PER_GROUP = 4
N_EXPERTS = 16
EXPERT_FF = 512
N_MOD = 6
EPS = 1e-6
ML_QKVO_COLS = 4 * ML_WIDTH
ML_GATE_COLS = 4 * ML_HEADS
HY_COLS = 3 * HY_WIDTH
MAIN_COLS = ML_QKVO_COLS + HY_COLS

T_PROMPT = BATCH * SEQ
T_SAMPLE = DEC_BATCH * DEC_SEQ
T_ALL = T_PROMPT + T_SAMPLE
TILE = 256
N_TILES_P = T_PROMPT // TILE
N_TILES = T_ALL // TILE
MOD_ROWS = 8
K_SCALE = ML_HEAD_DIM ** -0.5
VMEM_LIMIT = 56 * 1024 * 1024


def _cparams(sem):
    return pltpu.CompilerParams(dimension_semantics=sem, vmem_limit_bytes=VMEM_LIMIT)


def _split2(x):
    hi = x.astype(BF16)
    lo = (x - hi.astype(F32)).astype(BF16)
    return hi, lo


def _dot(a, b):
    return jnp.dot(a, b, preferred_element_type=F32)


def _dot_nt(a, b):
    return lax.dot_general(a, b, (((1,), (1,)), ((), ())), preferred_element_type=F32)


def _dot_tn(a, b):
    return lax.dot_general(a, b, (((0,), (0,)), ((), ())), preferred_element_type=F32)


def _dot3(a, b):
    ah, al = _split2(a)
    bh, bl = _split2(b)
    return _dot(ah, bh) + _dot(al, bh) + _dot(ah, bl)


def _dot3_nt(a, b):
    ah, al = _split2(a)
    bh, bl = _split2(b)
    return _dot_nt(ah, bh) + _dot_nt(al, bh) + _dot_nt(ah, bl)


def _dot_exact_rhs(x, t):
    x1 = x.astype(BF16)
    r1 = x - x1.astype(F32)
    x2 = r1.astype(BF16)
    x3 = (r1 - x2.astype(F32)).astype(BF16)
    return _dot(x1, t) + _dot(x2, t) + _dot(x3, t)


def _rms(x, g):
    return x * lax.rsqrt(jnp.mean(x * x, axis=-1, keepdims=True) + EPS) * g


def _mod_row_of_tile(i, tiles_per_sample_seq, n_prompt_tiles):
    return jnp.where(i < n_prompt_tiles, 0, 1 + (i - n_prompt_tiles) // tiles_per_sample_seq)


def _ada_kernel(cv_ref, w_ref, b_ref, o_ref):
    cv = cv_ref[...]
    s = cv * jax.nn.sigmoid(cv)
    sh, sl = _split2(s)
    wh, wl = _split2(w_ref[...])
    both = _dot(jnp.concatenate([sh.astype(F32), sl.astype(F32)], axis=0).astype(BF16), wh)
    o_ref[...] = both[0:MOD_ROWS] + both[MOD_ROWS:] + _dot(sh, wl) + b_ref[...]


def _ada(cv, w_ada, b_ada):
    n = N_MOD * D_MODEL
    return pl.pallas_call(
        _ada_kernel,
        out_shape=jax.ShapeDtypeStruct((MOD_ROWS, n), F32),
        grid=(N_MOD,),
        in_specs=[pl.BlockSpec((MOD_ROWS, D_MODEL), lambda j: (0, 0)),
                  pl.BlockSpec((D_MODEL, D_MODEL), lambda j: (0, j)),
                  pl.BlockSpec((1, D_MODEL), lambda j: (0, j))],
        out_specs=pl.BlockSpec((MOD_ROWS, D_MODEL), lambda j: (0, j)),
        compiler_params=_cparams(("arbitrary",)),
        name="ada_mod",
    )(cv, w_ada, b_ada.reshape(1, n))


PREP_COLS = 512


def _prep_q_kernel(wt_ref, o_ref):
    o_ref[...] = wt_ref[...].T.astype(BF16)


def _prep_hy_kernel(wt_hbm, o_ref, buf, sem):
    start = pl.multiple_of(ML_QKVO_COLS + ML_GATE_COLS + pl.program_id(0) * PREP_COLS, 8)
    copy = pltpu.make_async_copy(wt_hbm.at[pl.ds(start, PREP_COLS), :], buf, sem)
    copy.start()
    copy.wait()
    o_ref[...] = buf[...].T.astype(BF16)


def _prep_in_weights(w_in_t):
    out_blk = pl.BlockSpec((D_MODEL, PREP_COLS), lambda j: (0, j))
    w_qkvo = pl.pallas_call(
        _prep_q_kernel,
        out_shape=jax.ShapeDtypeStruct((D_MODEL, ML_QKVO_COLS), BF16),
        grid=(ML_QKVO_COLS // PREP_COLS,),
        in_specs=[pl.BlockSpec((PREP_COLS, D_MODEL), lambda j: (j, 0))], out_specs=out_blk,
        compiler_params=_cparams(("arbitrary",)), name="prep_w_qkvo",
    )(w_in_t)
    w_hy = pl.pallas_call(
        _prep_hy_kernel,
        out_shape=jax.ShapeDtypeStruct((D_MODEL, HY_COLS), BF16),
        grid=(HY_COLS // PREP_COLS,),
        in_specs=[pl.BlockSpec(memory_space=pl.ANY)], out_specs=out_blk,
        scratch_shapes=[pltpu.VMEM((PREP_COLS, D_MODEL), F32), pltpu.SemaphoreType.DMA(())],
        compiler_params=_cparams(("arbitrary",)), name="prep_w_hy",
    )(w_in_t)
    return w_qkvo, w_hy


def _log_sigmoid(x):
    return jnp.minimum(x, 0.0) - jnp.log1p(jnp.exp(-jnp.abs(x)))


def _rows_to_cols(rows):
    ri = lax.broadcasted_iota(jnp.int32, (TILE, TILE), 0)
    ci = lax.broadcasted_iota(jnp.int32, (TILE, TILE), 1)
    eye = jnp.where(ri == ci, 1.0, 0.0).astype(BF16)
    p1 = rows.astype(BF16)
    r1 = rows - p1.astype(F32)
    p2 = r1.astype(BF16)
    p3 = (r1 - p2.astype(F32)).astype(BF16)
    return _dot_nt(eye, p1) + _dot_nt(eye, p2) + _dot_nt(eye, p3)


BIG_TILE = 4 * TILE
N_BIG_P = T_PROMPT // BIG_TILE
N_BIG = T_ALL // BIG_TILE


def _inproj_kernel(xp_ref, xs_ref, m_ref, gn_ref, wq_ref, wh_ref, wgt_ref, gbt_ref, proj_ref, gatet_ref):
    is_p = pl.program_id(0) < N_BIG_P
    halves = [slice(r * TILE, (r + 1) * TILE) for r in range(BIG_TILE // TILE)]
    hs = [_rms(jnp.where(is_p, xp_ref[rows, :], xs_ref[rows, :]), gn_ref[0:1, :]) * (1.0 + m_ref[0, 1:2, :])
          + m_ref[0, 0:1, :] for rows in halves]
    hbs = [h.astype(BF16) for h in hs]
    cb = 512
    for j in range(ML_QKVO_COLS // cb):
        for rows, hb in zip(halves, hbs):
            proj_ref[rows, j * cb:(j + 1) * cb] = _dot(hb, wq_ref[:, j * cb:(j + 1) * cb]).astype(BF16)
    for j in range(HY_COLS // cb):
        lo = ML_QKVO_COLS + j * cb
        for rows, hb in zip(halves, hbs):
            proj_ref[rows, lo:lo + cb] = _dot(hb, wh_ref[:, j * cb:(j + 1) * cb]).astype(BF16)
    wth, wtl = _split2(wgt_ref[...])
    gts = []
    for h, hb in zip(hs, hbs):
        hl = (h - hb.astype(F32)).astype(BF16)
        gt = _dot_nt(wth, hb) + _dot_nt(wth, hl) + _dot_nt(wtl, hb) + gbt_ref[...]
        row = lax.broadcasted_iota(jnp.int32, gt.shape, 0)
        gts.append(jnp.where((row % 8) >= 4, _log_sigmoid(gt), gt))
    for r, gt in enumerate(gts):
        gatet_ref[r] = gt


def _inproj(xp, xs, mods3, g_norm, w_qkvo, w_hy, wgt, gbt):
    tps = DEC_SEQ // BIG_TILE
    per = BIG_TILE // TILE
    return pl.pallas_call(
        _inproj_kernel,
        out_shape=(jax.ShapeDtypeStruct((T_ALL, MAIN_COLS), BF16),
                   jax.ShapeDtypeStruct((N_TILES, ML_GATE_COLS, TILE), F32)),
        grid=(N_BIG,),
        in_specs=[pl.BlockSpec((BIG_TILE, D_MODEL), lambda i: (jnp.minimum(i, N_BIG_P - 1), 0)),
                  pl.BlockSpec((BIG_TILE, D_MODEL), lambda i: (jnp.maximum(i - N_BIG_P, 0), 0)),
                  pl.BlockSpec((1, N_MOD, D_MODEL), lambda i: (_mod_row_of_tile(i, tps, N_BIG_P), 0, 0)),
                  pl.BlockSpec((4, D_MODEL), lambda i: (0, 0)),
                  pl.BlockSpec((D_MODEL, ML_QKVO_COLS), lambda i: (0, 0)),
                  pl.BlockSpec((D_MODEL, HY_COLS), lambda i: (0, 0)),
                  pl.BlockSpec((ML_GATE_COLS, D_MODEL), lambda i: (0, 0)),
                  pl.BlockSpec((ML_GATE_COLS, 1), lambda i: (0, 0))],
        out_specs=(pl.BlockSpec((BIG_TILE, MAIN_COLS), lambda i: (i, 0)),
                   pl.BlockSpec((per, ML_GATE_COLS, TILE), lambda i: (i, 0, 0))),
        compiler_params=_cparams(("arbitrary",)),
        name="in_proj",
    )(xp, xs, mods3, g_norm, w_qkvo, w_hy, wgt, gbt)


ST_ROWS = ML_HEAD_DIM + 16


def _mlstm_kernel(*refs, seq_len, has_state):
    if has_state:
        (q_ref, k_ref, v_ref, o_ref, gt_ref, gain_ref, c0_ref, n0_ref, m0_ref,
         y_ref, c_ref, n_ref, m_ref, vt_ref, hf_ref, hb_ref, st_ref, ms_ref) = refs
    else:
        (q_ref, k_ref, v_ref, o_ref, gt_ref, gain_ref,
         y_ref, c_ref, n_ref, m_ref, vt_ref, hf_ref, hb_ref, st_ref, ms_ref) = refs
    ch = TILE
    nc = seq_len // ch
    hd = ML_HEAD_DIM
    key = lax.broadcasted_iota(jnp.int32, (ch, ch), 0)
    qry = lax.broadcasted_iota(jnp.int32, (ch, ch), 1)
    key_le = key <= qry
    key_ge = key >= qry
    t_le = jnp.where(key_le, 1.0, 0.0).astype(BF16)
    t_ge = jnp.where(key_ge, 1.0, 0.0).astype(BF16)
    sub16 = lax.broadcasted_iota(jnp.int32, (16, ch), 0)
    ln_scale = math.log(K_SCALE)

    for c in range(nc):
        for h in range(ML_HEADS):
            cols = slice(h * hd, (h + 1) * hd)
            vt_ref[c, cols, :] = v_ref[c * ch:(c + 1) * ch, cols].astype(F32).T.astype(BF16)

    for d in range(2):
        for h in range(ML_HEADS):
            r = d * ML_HEADS + h
            st_ref[r] = jnp.zeros((ST_ROWS, hd), F32)
            if has_state:
                st_ref[r, 0:hd, :] = c0_ref[0, d, h].T
                st_ref[r, hd:hd + 1, :] = n0_ref[0, d, h:h + 1, :]
                ms_ref[r] = jnp.broadcast_to(m0_ref[0, r:r + 1, :], (1, ch))
            else:
                ms_ref[r] = jnp.zeros((1, ch), F32)

    def step(t, carry):
        for d in range(2):
            c = t if d == 0 else nc - 1 - t
            rows = pl.ds(pl.multiple_of(c * ch, ch), ch)
            grow = gt_ref[c]
            brow_all = _dot_exact_rhs(grow, t_le if d == 0 else t_ge)
            ccol_all = _rows_to_cols(grow - pltpu.roll(brow_all, ML_GATE_COLS - ML_HEADS, axis=0))
            mask = key_le if d == 0 else key_ge
            hacc_ref = hf_ref if d == 0 else hb_ref
            heads = range(ML_HEADS)
            regs = [d * ML_HEADS + h for h in heads]
            colss = [slice(h * hd, (h + 1) * hd) for h in heads]
            qs = [q_ref[rows, cols] for cols in colss]
            ks = [k_ref[rows, cols] for cols in colss]
            vts = [vt_ref[c, cols, :] for cols in colss]
            sts = [st_ref[r] for r in regs]
            m_prevs = [ms_ref[r] for r in regs]
            b_rows = [brow_all[(1 + 2 * d) * ML_HEADS + h:(1 + 2 * d) * ML_HEADS + h + 1, :] for h in heads]
            ig_rows = [grow[2 * d * ML_HEADS + h:2 * d * ML_HEADS + h + 1, :] for h in heads]
            qks = [_dot_nt(k, q) for k, q in zip(ks, qs)]
            iqs = [_dot_nt(st.astype(BF16), q) for st, q in zip(sts, qs)]
            ss, sc_inters, m_poss = [], [], []
            for h in heads:
                icol = 2 * d * ML_HEADS + h
                c_col = ccol_all[:, icol:icol + 1]
                logd = jnp.where(mask, b_rows[h] + c_col, -jnp.inf)
                inter = b_rows[h] + m_prevs[h]
                m_pos = jnp.maximum(inter, jnp.max(logd, axis=0, keepdims=True))
                ss.append(qks[h] * jnp.exp(logd - (m_pos - ln_scale)))
                sc_inters.append(jnp.exp(inter - m_pos))
                m_poss.append(m_pos)
            pvs = [_dot(vt, s.astype(BF16)) for vt, s in zip(vts, ss)]
            for h in heads:
                num = sc_inters[h] * iqs[h][0:hd] + pvs[h]
                den = sc_inters[h] * iqs[h][hd:hd + 1] + jnp.sum(ss[h], axis=0, keepdims=True)
                hacc_ref[c, colss[h], :] = num * (1.0 / jnp.maximum(jnp.abs(den), jnp.exp(-m_poss[h])))
            lhss, decays = [], []
            for h in heads:
                b_row = b_rows[h]
                b_last = b_row[:, ch - 1:ch] if d == 0 else b_row[:, 0:1]
                logw = b_last - b_row + ig_rows[h]
                m_new = jnp.maximum(b_last + m_prevs[h], jnp.max(logw, axis=1, keepdims=True))
                w = jnp.exp(logw - (m_new - ln_scale))
                decays.append(jnp.exp(b_last + m_prevs[h] - m_new))
                lhss.append(jnp.concatenate([(vts[h].astype(F32) * w).astype(BF16),
                                             jnp.where(sub16 == 0, w, 0.0).astype(BF16)], axis=0))
                ms_ref[regs[h]] = m_new
            upds = [_dot(lhs, k) for lhs, k in zip(lhss, ks)]
            for h in heads:
                st_ref[regs[h]] = decays[h][:, 0:hd] * sts[h] + upds[h]
        return carry

    lax.fori_loop(0, nc, step, 0)

    for d in range(2):
        for h in range(ML_HEADS):
            r = d * ML_HEADS + h
            c_ref[0, d, h] = st_ref[r, 0:hd, :].T
            n_ref[0, d, h:h + 1, :] = st_ref[r, hd:hd + 1, :]
            m_ref[0, r:r + 1, :] = ms_ref[r][:, 0:hd]
    for c in range(nc):
        for h in range(ML_HEADS):
            cols = slice(h * hd, (h + 1) * hd)
            ht = hf_ref[c, cols, :] + hb_ref[c, cols, :]
            ht = ht * lax.rsqrt(jnp.mean(ht * ht, axis=0, keepdims=True) + EPS)
            rows = slice(c * ch, (c + 1) * ch)
            y = ht.T * gain_ref[:, cols] * jax.nn.sigmoid(o_ref[rows, cols].astype(F32))
            y_ref[rows, cols] = y.astype(BF16)


def _mlstm(proj, gates_t, gain, state, seq_len, n_seq, row_block_off):
    has_state = state is not None
    tiles = seq_len // TILE
    off = row_block_off
    qkvo_specs = [pl.BlockSpec((seq_len, ML_WIDTH), functools.partial(lambda b, j: (off + b, j), j=j))
                  for j in range(4)]
    in_specs = qkvo_specs + [
        pl.BlockSpec((tiles, ML_GATE_COLS, TILE), lambda b: (off + b, 0, 0)),
        pl.BlockSpec((1, ML_WIDTH), lambda b: (0, 0)),
    ]
    args = [proj, proj, proj, proj, gates_t, gain]
    if has_state:
        c0, n0, m0 = state
        in_specs += [
            pl.BlockSpec((1, 2, ML_HEADS, ML_HEAD_DIM, ML_HEAD_DIM), lambda b: (b, 0, 0, 0, 0)),
            pl.BlockSpec((1, 2, ML_HEADS, ML_HEAD_DIM), lambda b: (b, 0, 0, 0)),
            pl.BlockSpec((1, 2 * ML_HEADS, 1), lambda b: (b, 0, 0)),
        ]
        args += [c0, n0, m0]
    out_shape = (jax.ShapeDtypeStruct((n_seq * seq_len, ML_WIDTH), BF16),
                 jax.ShapeDtypeStruct((n_seq, 2, ML_HEADS, ML_HEAD_DIM, ML_HEAD_DIM), F32),
                 jax.ShapeDtypeStruct((n_seq, 2, ML_HEADS, ML_HEAD_DIM), F32),
                 jax.ShapeDtypeStruct((n_seq, 2 * ML_HEADS, ML_HEAD_DIM), F32))
    out_specs = (pl.BlockSpec((seq_len, ML_WIDTH), lambda b: (b, 0)),
                 pl.BlockSpec((1, 2, ML_HEADS, ML_HEAD_DIM, ML_HEAD_DIM), lambda b: (b, 0, 0, 0, 0)),
                 pl.BlockSpec((1, 2, ML_HEADS, ML_HEAD_DIM), lambda b: (b, 0, 0, 0)),
                 pl.BlockSpec((1, 2 * ML_HEADS, ML_HEAD_DIM), lambda b: (b, 0, 0)))
    scratch = [pltpu.VMEM((tiles, ML_WIDTH, TILE), BF16),
               pltpu.VMEM((tiles, ML_WIDTH, TILE), F32), pltpu.VMEM((tiles, ML_WIDTH, TILE), F32),
               pltpu.VMEM((2 * ML_HEADS, ST_ROWS, ML_HEAD_DIM), F32),
               pltpu.VMEM((2 * ML_HEADS, 1, TILE), F32)]
    return pl.pallas_call(
        functools.partial(_mlstm_kernel, seq_len=seq_len, has_state=has_state),
        out_shape=out_shape, grid=(n_seq,), in_specs=in_specs, out_specs=out_specs,
        scratch_shapes=scratch, compiler_params=_cparams(("arbitrary",)),
        name=f"mlstm_{seq_len}",
    )(*args)


def _dft_mats(seq_len):
    k = np.arange(seq_len, dtype=np.int64)[:, None]
    d = np.arange(seq_len, dtype=np.int64)[None, :]
    ang = np.pi * ((k * d) % (2 * seq_len)).astype(np.float64) / seq_len
    sinm = np.sin(ang)
    sinm[0, :] = np.where(d[0] % 2 == 0, 1.0, -1.0)
    f = np.concatenate([np.cos(ang), sinm], axis=0).astype(np.float32)
    return jnp.asarray(f).astype(BF16), jnp.asarray(np.ascontiguousarray(f.T)).astype(BF16)


def _filter_feats(seq_len):
    t = np.linspace(0.0, 1.0, seq_len, dtype=np.float64)[:, None]
    wpos = 2.0 * np.pi * np.arange(seq_len, dtype=np.float64)[:, None] / seq_len
    bands = np.linspace(1e-4, HY_BANDS - 1, HY_BANDS, dtype=np.float64)[None, :]
    z = np.concatenate([t, np.cos(bands * wpos), -np.sin(bands * wpos)], axis=-1)
    return jnp.asarray(np.pad(z, ((0, 0), (0, 128 - HY_EMB))).astype(np.float32))


def _filter_kernel(z_ref, w1_ref, b1_ref, w2_ref, b2_ref, w3_ref, b3_ref, dec_ref, f_ref,
                   a_ref, b_ref, d_ref, *, seq_len):
    n = 2 * seq_len
    oc = 2 * HY_WIDTH
    z = z_ref[...]
    h = jnp.sin(_dot3(z, w1_ref[...]) + b1_ref[...])
    h = jnp.sin(_dot3(h, w2_ref[...]) + b2_ref[...])
    t = z[:, 0:1]
    di = lax.broadcasted_iota(jnp.int32, (seq_len, 1), 0)
    sgn = jnp.where(di % 2 == 0, 1.0, -1.0)
    first = di == 0
    ssums, sdifs = [], []
    for o in range(HY_ORDER):
        cols = slice(o * oc, (o + 1) * oc)
        g = _dot3(h, w3_ref[:, cols]) + b3_ref[:, cols]
        g = g * (jnp.exp(-t * jnp.abs(dec_ref[:, cols])) + HY_MOD_SHIFT)
        ss = jnp.sum(g * g, axis=0, keepdims=True)
        inv = lax.rsqrt(ss[:, :HY_WIDTH] + ss[:, HY_WIDTH:] + EPS)
        hp = g[:, :HY_WIDTH] * inv
        hn = g[:, HY_WIDTH:] * inv
        ssums.append(hp + hn)
        sdifs.append(hp - hn)
    hcs = [_dot(f_ref[0:seq_len, :], s.astype(BF16)) for s in ssums]
    hss = [_dot(f_ref[seq_len:n, :], s.astype(BF16)) for s in sdifs]
    for o in range(HY_ORDER):
        nyq = jnp.sum(ssums[o] * sgn, axis=0, keepdims=True)
        a_ref[o] = hcs[o] * jnp.where(first, 1.0 / n, 2.0 / n)
        b_ref[o] = jnp.where(first, 0.0, hss[o] * (2.0 / n))
        d_ref[o] = jnp.where(first, nyq * (1.0 / n), hcs[o] * (2.0 / n))


def _hyena_filters(seq_len, f, w1p, b1, w2, b2, w3, b3, dec):
    z = _filter_feats(seq_len)
    out = jax.ShapeDtypeStruct((HY_ORDER, seq_len, HY_WIDTH), F32)
    return pl.pallas_call(
        functools.partial(_filter_kernel, seq_len=seq_len),
        out_shape=(out, out, out),
        compiler_params=pltpu.CompilerParams(vmem_limit_bytes=VMEM_LIMIT),
        name=f"hyena_filter_{seq_len}",
    )(z, w1p, b1, w2, b2, w3, b3, dec, f)


def _hyena_kernel(x1_ref, x2_ref, v_ref, cw1_ref, cw2_ref, cwv_ref, a_ref, b_ref, d_ref, bias_ref,
                  f_ref, ft_ref, z_ref, *, seq_len, width, seqs):
    rows = seqs * seq_len
    ti = lax.broadcasted_iota(jnp.int32, (rows, 1), 0)
    has_prev = (ti % width) != 0
    has_next = (ti % width) != (width - 1)

    def short_conv(x_ref, w_ref):
        x = x_ref[...].astype(F32)
        prev = jnp.where(has_prev, pltpu.roll(x, 1, axis=0), 0.0)
        nxt = jnp.where(has_next, pltpu.roll(x, rows - 1, axis=0), 0.0)
        return w_ref[0:1, :] * prev + w_ref[1:2, :] * x + w_ref[2:3, :] * nxt

    gates = (short_conv(x1_ref, cw1_ref), short_conv(x2_ref, cw2_ref))
    v = short_conv(v_ref, cwv_ref)
    sls = [slice(i * seq_len, (i + 1) * seq_len) for i in range(seqs)]
    zs = [v[sl] for sl in sls]
    for o in range(HY_ORDER):
        a, b, dd = a_ref[o], b_ref[o], d_ref[o]
        us = [_dot(f_ref[...], z.astype(BF16)) for z in zs]
        ys = []
        for u in us:
            ut = u[:seq_len]
            ub = u[seq_len:]
            ys.append(((ut * a - ub * b).astype(BF16), (ut * b + ub * dd).astype(BF16)))
        convs = [_dot(ft_ref[:, :seq_len], yt) + _dot(ft_ref[:, seq_len:], yb) for yt, yb in ys]
        zs = [gates[o][sl] * (y + bias_ref[o:o + 1, :] * z) for sl, y, z in zip(sls, convs, zs)]
    for sl, z in zip(sls, zs):
        z_ref[sl, :] = z.astype(BF16)


def _hyena(proj, conv_w, coefs, hy_bias, f, ft, seq_len, n_seq, row_off, width, seqs):
    cb = 256
    nblk = HY_WIDTH // cb
    base = ML_QKVO_COLS // cb
    rows = seqs * seq_len
    off = row_off // rows
    a, b, d = coefs

    def col_spec(part):
        return pl.BlockSpec((rows, cb), lambda j, s: (off + s, base + part * nblk + j))

    def w_spec(part):
        return pl.BlockSpec((3, cb), lambda j, s: (0, part * nblk + j))

    coef_spec = pl.BlockSpec((HY_ORDER, seq_len, cb), lambda j, s: (0, 0, j))
    return pl.pallas_call(
        functools.partial(_hyena_kernel, seq_len=seq_len, width=width, seqs=seqs),
        out_shape=jax.ShapeDtypeStruct((n_seq * seq_len, HY_WIDTH), BF16),
        grid=(nblk, n_seq // seqs),
        in_specs=[col_spec(0), col_spec(1), col_spec(2), w_spec(0), w_spec(1), w_spec(2),
                  coef_spec, coef_spec, coef_spec,
                  pl.BlockSpec((HY_ORDER, cb), lambda j, s: (0, j)),
                  pl.BlockSpec((2 * seq_len, seq_len), lambda j, s: (0, 0)),
                  pl.BlockSpec((seq_len, 2 * seq_len), lambda j, s: (0, 0))],
        out_specs=pl.BlockSpec((rows, cb), lambda j, s: (s, j)),
        compiler_params=_cparams(("arbitrary", "arbitrary")),
        name=f"hyena_conv_{seq_len}",
    )(proj, proj, proj, conv_w, conv_w, conv_w, a, b, d, hy_bias, f, ft)


def _first_max(x, n):
    mx = jnp.max(x, axis=0, keepdims=True)
    row = lax.broadcasted_iota(jnp.int32, x.shape, 0).astype(F32)
    idx = jnp.min(jnp.where(x == mx, row, float(n)), axis=0, keepdims=True)
    return mx, idx.astype(jnp.int32)


ROUTER_ROWS = 32
PAIRS_PER_GROUP = 6
N_BUCKETS = N_GROUPS * PAIRS_PER_GROUP
PAIR_SLOTS = ((0, 1), (0, 2), (0, 3), (1, 3), (1, 2), (3, 2))
LANES = 128
H2_EXT = D_MODEL + LANES
ROW_TILE = 256
ROW_CAP = T_ALL + N_BUCKETS * ROW_TILE
N_ROW_TILES = ROW_CAP // ROW_TILE


def _outproj_kernel(xp_ref, xs_ref, yp_ref, ys_ref, zp_ref, zs_ref, m_ref, gn_ref, wo_ref, wr_ref, br_ref,
                    x1_ref, h2_ref, bid_ref):
    is_p = pl.program_id(0) < N_BIG_P
    wrh, wrl = _split2(wr_ref[...])
    halves = [slice(r * TILE, (r + 1) * TILE) for r in range(BIG_TILE // TILE)]
    ys = [_dot(jnp.where(is_p, yp_ref[rows, :], ys_ref[rows, :]), wo_ref[0:ML_WIDTH, :])
          + _dot(jnp.where(is_p, zp_ref[rows, :], zs_ref[rows, :]), wo_ref[ML_WIDTH:, :]) for rows in halves]
    h2s = []
    for rows, y in zip(halves, ys):
        x = jnp.where(is_p, xp_ref[rows, :], xs_ref[rows, :])
        x1 = x + m_ref[0, 2:3, :] * _rms(y, gn_ref[1:2, :])
        x1_ref[rows, :] = x1
        h2 = _rms(x1, gn_ref[2:3, :]) * (1.0 + m_ref[0, 4:5, :]) + m_ref[0, 3:4, :]
        h2_ref[rows, 0:D_MODEL] = h2
        h2s.append(h2)
    logits = []
    for h2 in h2s:
        h2h, h2l = _split2(h2)
        logits.append(_dot_nt(wrh, h2h) + _dot_nt(wrh, h2l) + _dot_nt(wrl, h2h) + br_ref[...])
    routed = [_route_tile(lg) for lg in logits]
    for r, (rows, (gate_rows, bucket)) in enumerate(zip(halves, routed)):
        h2_ref[rows, D_MODEL:H2_EXT] = jnp.zeros((TILE, LANES), F32)
        h2_ref[rows, D_MODEL:D_MODEL + 8] = _rows_to_cols(gate_rows)
        bid_ref[r] = bucket


def _route_tile(logits):
    lc = logits[0:N_GROUPS]
    mx, gi = _first_max(lc, N_GROUPS)
    p_grp = 1.0 / jnp.sum(jnp.exp(lc - mx), axis=0, keepdims=True)
    lsel = jnp.zeros((EXPERTS_PER_GROUP, TILE), F32)
    for g in range(N_GROUPS):
        lo = N_GROUPS + g * EXPERTS_PER_GROUP
        lsel = jnp.where(gi == g, logits[lo:lo + EXPERTS_PER_GROUP], lsel)
    l1, i1 = _first_max(lsel, EXPERTS_PER_GROUP)
    sub4 = lax.broadcasted_iota(jnp.int32, lsel.shape, 0)
    l2, i2 = _first_max(jnp.where(sub4 == i1, -jnp.inf, lsel), EXPERTS_PER_GROUP)
    e2 = jnp.exp(l2 - l1)
    w1 = p_grp / (1.0 + e2)
    w2 = p_grp * e2 / (1.0 + e2)
    lo_e = jnp.minimum(i1, i2)
    hi_e = jnp.maximum(i1, i2)
    pair = jnp.where(lo_e == 0, hi_e - 1, jnp.where(lo_e == 1, jnp.where(hi_e == 3, 3, 4), 5))
    slot_a = jnp.where(pair == 5, hi_e, lo_e)
    first_in_a = i1 == slot_a
    w_a = jnp.where(first_in_a, w1, w2)
    w_b = jnp.where(first_in_a, w2, w1)
    sub = lax.broadcasted_iota(jnp.int32, (8, TILE), 0)
    gate_rows = jnp.where(sub == 0, w_a, jnp.where(sub == 1, w_b, 0.0))
    return gate_rows, gi * PAIRS_PER_GROUP + pair


def _outproj(xp, xs, yp, ys, zp, zs, mods3, g_norm, w_out, w_r, b_r):
    tps = DEC_SEQ // BIG_TILE
    per = BIG_TILE // TILE
    pidx = lambda i: (jnp.minimum(i, N_BIG_P - 1), 0)
    sidx = lambda i: (jnp.maximum(i - N_BIG_P, 0), 0)
    return pl.pallas_call(
        _outproj_kernel,
        out_shape=(jax.ShapeDtypeStruct((T_ALL, D_MODEL), F32),
                   jax.ShapeDtypeStruct((T_ALL, H2_EXT), F32),
                   jax.ShapeDtypeStruct((N_TILES, 1, TILE), jnp.int32)),
        grid=(N_BIG,),
        in_specs=[pl.BlockSpec((BIG_TILE, D_MODEL), pidx), pl.BlockSpec((BIG_TILE, D_MODEL), sidx),
                  pl.BlockSpec((BIG_TILE, ML_WIDTH), pidx), pl.BlockSpec((BIG_TILE, ML_WIDTH), sidx),
                  pl.BlockSpec((BIG_TILE, HY_WIDTH), pidx), pl.BlockSpec((BIG_TILE, HY_WIDTH), sidx),
                  pl.BlockSpec((1, N_MOD, D_MODEL), lambda i: (_mod_row_of_tile(i, tps, N_BIG_P), 0, 0)),
                  pl.BlockSpec((4, D_MODEL), lambda i: (0, 0)),
                  pl.BlockSpec((D_MODEL, D_MODEL), lambda i: (0, 0)),
                  pl.BlockSpec((ROUTER_ROWS, D_MODEL), lambda i: (0, 0)),
                  pl.BlockSpec((ROUTER_ROWS, 1), lambda i: (0, 0))],
        out_specs=(pl.BlockSpec((BIG_TILE, D_MODEL), lambda i: (i, 0)),
                   pl.BlockSpec((BIG_TILE, H2_EXT), lambda i: (i, 0)),
                   pl.BlockSpec((per, 1, TILE), lambda i: (i, 0, 0))),
        compiler_params=_cparams(("arbitrary",)),
        name="out_proj_router",
    )(xp, xs, yp, ys, zp, zs, mods3, g_norm, w_out, w_r, b_r)


def _route_kernel(bid_ref, pos_ref, meta_ref):
    nb = 32
    tm = float(ROW_TILE)
    sub = lax.broadcasted_iota(jnp.int32, (nb, TILE), 0)
    ri = lax.broadcasted_iota(jnp.int32, (TILE, TILE), 0)
    ci = lax.broadcasted_iota(jnp.int32, (TILE, TILE), 1)
    before = jnp.where(ri < ci, 1.0, 0.0).astype(BF16)

    def onehot(blk):
        return jnp.where(sub == bid_ref[blk], 1.0, 0.0)

    zeros = jnp.zeros((nb, 1), F32)
    cnt = lax.fori_loop(0, N_TILES, lambda blk, c: c + jnp.sum(onehot(blk), axis=1, keepdims=True), zeros)
    padded = jnp.floor((cnt + (tm - 1.0)) * (1.0 / tm)) * tm
    r32 = lax.broadcasted_iota(jnp.int32, (nb, nb), 0)
    c32 = lax.broadcasted_iota(jnp.int32, (nb, nb), 1)
    padded_row = jnp.sum(jnp.where(r32 == c32, padded, 0.0), axis=0, keepdims=True)
    offs = jnp.sum(jnp.where(c32 < r32, padded_row, 0.0), axis=1, keepdims=True)
    ends = offs + padded

    def place(blk, seen):
        oh = onehot(blk)
        rank = _dot(oh.astype(BF16), before)
        pos = jnp.sum(oh * (rank + seen + offs), axis=0, keepdims=True)
        pos_ref[blk] = pos.astype(jnp.int32)
        return seen + jnp.sum(oh, axis=1, keepdims=True)

    lax.fori_loop(0, N_TILES, place, zeros)

    start = lax.broadcasted_iota(jnp.int32, (nb, 128), 1).astype(F32) * tm
    bsub = lax.broadcasted_iota(jnp.int32, (nb, 128), 0)
    done = jnp.where((bsub < N_BUCKETS) & (ends <= start), 1.0, 0.0)
    tb = jnp.sum(done, axis=0, keepdims=True)
    valid = jnp.where(tb < N_BUCKETS, 1.0, 0.0)
    tbc = jnp.minimum(tb, N_BUCKETS - 1.0)
    grp = jnp.floor((tbc + 0.5) * (1.0 / PAIRS_PER_GROUP))
    pair = tbc - PAIRS_PER_GROUP * grp
    loc_a = jnp.zeros_like(pair)
    loc_b = jnp.zeros_like(pair)
    for k, (sa, sb) in enumerate(PAIR_SLOTS):
        loc_a = jnp.where(pair == k, float(sa), loc_a)
        loc_b = jnp.where(pair == k, float(sb), loc_b)
    mine = bsub.astype(F32) == tbc
    used = jnp.sum(jnp.where(mine, offs + cnt, 0.0), axis=0, keepdims=True)
    n_rows = jnp.clip(used - start[0:1], 0.0, tm) * valid
    row8 = lax.broadcasted_iota(jnp.int32, (8, 128), 0)
    meta = jnp.where(row8 == 0, grp * EXPERTS_PER_GROUP + loc_a,
                     jnp.where(row8 == 1, grp * EXPERTS_PER_GROUP + loc_b,
                               jnp.where(row8 == 2, valid, jnp.where(row8 == 3, n_rows, 0.0))))
    meta_ref[...] = meta.astype(jnp.int32)


def _route(bid):
    return pl.pallas_call(
        _route_kernel,
        out_shape=(jax.ShapeDtypeStruct((N_TILES, 1, TILE), jnp.int32),
                   jax.ShapeDtypeStruct((8, 128), jnp.int32)),
        compiler_params=pltpu.CompilerParams(vmem_limit_bytes=VMEM_LIMIT),
        name="moe_route",
    )(bid)


def _moe_kernel(meta_ref, pos_ref, h2_hbm, wga_ref, wua_ref, wda_ref, wgb_ref, wub_ref, wdb_ref,
                y_ref, src_ref, xbuf, sem, wga_s, wua_s, wda_s, wgb_s, wub_s, wdb_s):
    j = pl.program_id(0)

    def row_copy(tile, r, slot):
        tok = src_ref[tile * ROW_TILE + r]
        return pltpu.make_async_copy(h2_hbm.at[pl.ds(tok, 1), :], xbuf.at[slot, pl.ds(r, 1), :], sem.at[slot])

    def issue_rows(tile, slot, lo, hi):
        for r in range(lo, hi):
            row_copy(tile, r, slot).start()

    def wait(slot):
        pltpu.make_async_copy(h2_hbm.at[pl.ds(0, ROW_TILE), :], xbuf.at[slot], sem.at[slot]).wait()

    @pl.when(j == 0)
    def _():
        def fill(t, c):
            def body(r, c2):
                p = t * ROW_TILE + r
                src_ref[p] = p % T_ALL
                return c2
            n = meta_ref[3, t]
            lax.fori_loop(n, jnp.where(meta_ref[2, t] == 1, ROW_TILE, n), body, 0)
            return c
        lax.fori_loop(0, N_ROW_TILES, fill, 0)

        def invert(t, c):
            src_ref[pos_ref[t]] = t
            return c
        lax.fori_loop(0, T_ALL, invert, 0, unroll=8)

        @pl.when(meta_ref[2, 0] == 1)
        def _():
            def body(g, c):
                for k in range(8):
                    row_copy(0, g * 8 + k, 0).start()
                return c
            lax.fori_loop(0, ROW_TILE // 8, body, 0)

    nxt = jnp.minimum(j + 1, N_ROW_TILES - 1)
    has_next = (j + 1 < N_ROW_TILES) & (meta_ref[2, nxt] == 1)
    valid = meta_ref[2, j] == 1
    prev = jnp.maximum(j - 1, 0)

    @pl.when(valid & ((j == 0) | (meta_ref[0, j] != meta_ref[0, prev])))
    def _():
        wga_s[...] = wga_ref[0].astype(BF16)
        wua_s[...] = wua_ref[0].astype(BF16)
        wda_s[...] = wda_ref[0].astype(BF16)

    @pl.when(valid & ((j == 0) | (meta_ref[1, j] != meta_ref[1, prev])))
    def _():
        wgb_s[...] = wgb_ref[0].astype(BF16)
        wub_s[...] = wub_ref[0].astype(BF16)
        wdb_s[...] = wdb_ref[0].astype(BF16)

    def compute(fetch_next):
        slot = j % 2
        nslot = nxt % 2
        step = ROW_TILE // 8
        batches = iter(range(0, ROW_TILE, step))

        def fetch():
            if fetch_next:
                lo = next(batches)
                issue_rows(nxt, nslot, lo, lo + step)

        wait(slot)
        x = xbuf[slot, :, 0:D_MODEL].astype(BF16)
        gates = xbuf[slot, :, D_MODEL:H2_EXT]
        hg_a = _dot(x, wga_s[...])
        fetch()
        hu_a = _dot(x, wua_s[...])
        fetch()
        hg_b = _dot(x, wgb_s[...])
        fetch()
        hu_b = _dot(x, wub_s[...])
        fetch()
        act_a = (hg_a * jax.nn.sigmoid(hg_a) * hu_a * gates[:, 0:1]).astype(BF16)
        fetch()
        act_b = (hg_b * jax.nn.sigmoid(hg_b) * hu_b * gates[:, 1:2]).astype(BF16)
        fetch()
        y = _dot(act_a, wda_s[...])
        fetch()
        y = y + _dot(act_b, wdb_s[...])
        fetch()
        y_ref[...] = y

    @pl.when(valid & has_next)
    def _():
        compute(True)

    @pl.when(valid & jnp.logical_not(has_next))
    def _():
        compute(False)

    @pl.when(jnp.logical_not(valid))
    def _():
        y_ref[...] = jnp.zeros_like(y_ref)


def _moe(meta, pos, h2ext, w_gate, w_up, w_down):
    up_spec = lambda slot: pl.BlockSpec((1, D_MODEL, EXPERT_FF), lambda j, meta, pos: (meta[slot, j], 0, 0))
    down_spec = lambda slot: pl.BlockSpec((1, EXPERT_FF, D_MODEL), lambda j, meta, pos: (meta[slot, j], 0, 0))
    grid_spec = pltpu.PrefetchScalarGridSpec(
        num_scalar_prefetch=2,
        grid=(N_ROW_TILES,),
        in_specs=[pl.BlockSpec(memory_space=pl.ANY),
                  up_spec(0), up_spec(0), down_spec(0), up_spec(1), up_spec(1), down_spec(1)],
        out_specs=pl.BlockSpec((ROW_TILE, D_MODEL), lambda j, meta, pos: (j, 0)),
        scratch_shapes=[pltpu.SMEM((ROW_CAP,), jnp.int32),
                        pltpu.VMEM((2, ROW_TILE, H2_EXT), F32),
                        pltpu.SemaphoreType.DMA((2,)),
                        pltpu.VMEM((D_MODEL, EXPERT_FF), BF16), pltpu.VMEM((D_MODEL, EXPERT_FF), BF16),
                        pltpu.VMEM((EXPERT_FF, D_MODEL), BF16),
                        pltpu.VMEM((D_MODEL, EXPERT_FF), BF16), pltpu.VMEM((D_MODEL, EXPERT_FF), BF16),
                        pltpu.VMEM((EXPERT_FF, D_MODEL), BF16)])
    return pl.pallas_call(
        _moe_kernel,
        out_shape=jax.ShapeDtypeStruct((ROW_CAP, D_MODEL), F32),
        grid_spec=grid_spec,
        compiler_params=_cparams(("arbitrary",)),
        name="moe_experts",
    )(meta, pos, h2ext, w_gate, w_up, w_down, w_gate, w_up, w_down)


def _final_kernel(pos_ref, y_hbm, x1_ref, m_ref, gn_ref, op_ref, os_ref, ybuf, sem):
    i = pl.program_id(0)

    def row_copy(tile, r, slot):
        p = pos_ref[tile * TILE + r]
        return pltpu.make_async_copy(y_hbm.at[pl.ds(p, 1), :], ybuf.at[slot, pl.ds(r, 1), :], sem.at[slot])

    def issue(tile, slot):
        def body(r2, c):
            row_copy(tile, 2 * r2, slot).start(priority=0)
            row_copy(tile, 2 * r2 + 1, slot).start(priority=1)
            return c
        lax.fori_loop(0, TILE // 2, body, 0, unroll=4)

    def wait(slot):
        pltpu.make_async_copy(y_hbm.at[pl.ds(0, TILE)], ybuf.at[slot], sem.at[slot]).wait()

    @pl.when(i == 0)
    def _():
        issue(0, 0)

    @pl.when(i + 1 < N_TILES)
    def _():
        issue(i + 1, (i + 1) % 2)

    slot = i % 2
    wait(slot)
    out = x1_ref[...] + m_ref[0, 5:6, :] * _rms(ybuf[slot], gn_ref[3:4, :])

    @pl.when(i < N_TILES_P)
    def _():
        op_ref[...] = out

    @pl.when(i >= N_TILES_P)
    def _():
        os_ref[...] = out


def _final(pos, y_sorted, x1, mods3, g_norm):
    tps = DEC_SEQ // TILE
    grid_spec = pltpu.PrefetchScalarGridSpec(
        num_scalar_prefetch=1,
        grid=(N_TILES,),
        in_specs=[pl.BlockSpec(memory_space=pl.ANY),
                  pl.BlockSpec((TILE, D_MODEL), lambda i, pos: (i, 0)),
                  pl.BlockSpec((1, N_MOD, D_MODEL), lambda i, pos: (_mod_row_of_tile(i, tps, N_TILES_P), 0, 0)),
                  pl.BlockSpec((4, D_MODEL), lambda i, pos: (0, 0))],
        out_specs=(pl.BlockSpec((TILE, D_MODEL), lambda i, pos: (jnp.minimum(i, N_TILES_P - 1), 0)),
                   pl.BlockSpec((TILE, D_MODEL), lambda i, pos: (jnp.maximum(i - N_TILES_P, 0), 0))),
        scratch_shapes=[pltpu.VMEM((2, TILE, D_MODEL), F32), pltpu.SemaphoreType.DMA((2,))])
    return pl.pallas_call(
        _final_kernel,
        out_shape=(jax.ShapeDtypeStruct((T_PROMPT, D_MODEL), F32),
                   jax.ShapeDtypeStruct((T_SAMPLE, D_MODEL), F32)),
        grid_spec=grid_spec,
        compiler_params=_cparams(("arbitrary",)),
        name="moe_combine_final",
    )(pos, y_sorted, x1, mods3, g_norm)


def kernel(x_prompt, x_sample, state_C, state_n, state_m, c, c_ctx, w_ada, b_ada, g_norm, w_in, ml_gate_bias, ml_head_gain, hy_conv_w, hy_f_w1, hy_f_b1, hy_f_w2, hy_f_b2, hy_f_w3, hy_f_b3, hy_decay, hy_bias, w_out, w_rc, b_rc, w_rf, b_rf, w_gate, w_up, w_down):
    xp = x_prompt.reshape(T_PROMPT, D_MODEL)
    xs = x_sample.reshape(T_SAMPLE, D_MODEL)
    gn = g_norm[0]

    cv = jnp.concatenate([c_ctx[None, :], c, jnp.zeros((MOD_ROWS - 1 - DEC_BATCH, D_MODEL), F32)], axis=0)
    mods3 = _ada(cv, w_ada[0], b_ada[0]).reshape(MOD_ROWS, N_MOD, D_MODEL)

    w_in0 = w_in[0]
    w_qkvo, w_hy = _prep_in_weights(w_in0.T)
    wg = w_in0[:, ML_QKVO_COLS:ML_QKVO_COLS + ML_GATE_COLS]
    gbt = ml_gate_bias[0].reshape(ML_GATE_COLS, 1)
    proj, gates_t = _inproj(xp, xs, mods3, gn, w_qkvo, w_hy, wg.T, gbt)

    gain = ml_head_gain[0].reshape(1, ML_WIDTH)
    y_ml_p, c_new, n_new, m_new = _mlstm(proj, gates_t, gain, None, SEQ, BATCH, 0)
    state = (state_C[:, 0], state_n[:, 0], state_m[:, 0].reshape(DEC_BATCH, 2 * ML_HEADS, 1))
    y_ml_s, _, _, _ = _mlstm(proj, gates_t, gain, state, DEC_SEQ, DEC_BATCH, T_PROMPT // DEC_SEQ)

    w1p = jnp.pad(hy_f_w1[0], ((0, 128 - HY_EMB), (0, 0)))
    b1 = hy_f_b1[0].reshape(1, -1)
    b2 = hy_f_b2[0].reshape(1, -1)
    b3 = hy_f_b3[0].reshape(1, -1)
    dec = hy_decay[0].reshape(1, -1)
    z_parts = []
    for seq_len, n_seq, row_off, width, seqs in ((SEQ, BATCH, 0, SEQ, 4), (DEC_SEQ, DEC_BATCH, T_PROMPT, GRID_W, 2)):
        f, ft = _dft_mats(seq_len)
        coefs = _hyena_filters(seq_len, f, w1p, b1, hy_f_w2[0], b2, hy_f_w3[0], b3, dec)
        z_parts.append(_hyena(proj, hy_conv_w[0], coefs, hy_bias[0], f, ft, seq_len, n_seq, row_off, width, seqs))
    z_p, z_s = z_parts

    pad_r = ROUTER_ROWS - N_GROUPS - N_EXPERTS
    w_r = jnp.pad(jnp.concatenate([w_rc[0], w_rf[0]], axis=1).T, ((0, pad_r), (0, 0)))
    b_r = jnp.pad(jnp.concatenate([b_rc[0], b_rf[0]], axis=0), (0, pad_r)).reshape(ROUTER_ROWS, 1)
    x1, h2ext, bid = _outproj(xp, xs, y_ml_p, y_ml_s, z_p, z_s, mods3, gn, w_out[0].astype(BF16), w_r, b_r)

    pos3, meta = _route(bid)
    pos = pos3.reshape(T_ALL)
    y_sorted = _moe(meta, pos, h2ext, w_gate[0], w_up[0], w_down[0])
    y_p, y_s = _final(pos, y_sorted, x1, mods3, gn)

    new_c = c_new.reshape(BATCH, 1, 2, ML_HEADS, ML_HEAD_DIM, ML_HEAD_DIM)
    new_n = n_new.reshape(BATCH, 1, 2, ML_HEADS, ML_HEAD_DIM)
    new_m = m_new[:, :, 0].reshape(BATCH, 1, 2, ML_HEADS)
    return (y_p.reshape(BATCH, SEQ, D_MODEL), y_s.reshape(DEC_BATCH, DEC_SEQ, D_MODEL), new_c, new_n, new_m)
```

```python
import functools
import math

import jax
import jax.numpy as jnp
import numpy as np
from jax import lax
from jax.experimental import pallas as pl
from jax.experimental.pallas import tpu as pltpu

F32 = jnp.float32
BF16 = jnp.bfloat16

D_MODEL = 1024
BATCH = 16
SEQ = 256
DEC_BATCH = 4
DEC_SEQ = 1024
GRID_W = 64
ML_WIDTH = 512
ML_HEADS = 4
ML_HEAD_DIM = 128
HY_WIDTH = 512
HY_ORDER = 2
HY_EMB = 33
HY_BANDS = 16
HY_FILTER_HIDDEN = 64
HY_MOD_SHIFT = 0.05
N_GROUPS = 4
EXPERTS_PER_GROUP = 4
N_EXPERTS = 16
EXPERT_FF = 512
N_MOD = 6
EPS = 1e-6
ML_QKVO_COLS = 4 * ML_WIDTH
ML_GATE_COLS = 4 * ML_HEADS
HY_COLS = 3 * HY_WIDTH
MAIN_COLS = ML_QKVO_COLS + HY_COLS

T_PROMPT = BATCH * SEQ
T_SAMPLE = DEC_BATCH * DEC_SEQ
T_ALL = T_PROMPT + T_SAMPLE
TILE = 256
N_TILES_P = T_PROMPT // TILE
N_TILES = T_ALL // TILE
MOD_ROWS = 8
K_SCALE = ML_HEAD_DIM ** -0.5
VMEM_LIMIT = 56 * 1024 * 1024


def _cparams(sem):
    return pltpu.CompilerParams(dimension_semantics=sem, vmem_limit_bytes=VMEM_LIMIT)


def _split2(x):
    hi = x.astype(BF16)
    lo = (x - hi.astype(F32)).astype(BF16)
    return hi, lo


def _dot(a, b):
    return jnp.dot(a, b, preferred_element_type=F32)


def _dot_nt(a, b):
    return lax.dot_general(a, b, (((1,), (1,)), ((), ())), preferred_element_type=F32)


def _dot3(a, b):
    ah, al = _split2(a)
    bh, bl = _split2(b)
    return _dot(ah, bh) + _dot(al, bh) + _dot(ah, bl)


def _dot_exact_rhs(x, t):
    x1 = x.astype(BF16)
    r1 = x - x1.astype(F32)
    x2 = r1.astype(BF16)
    x3 = (r1 - x2.astype(F32)).astype(BF16)
    return _dot(x1, t) + _dot(x2, t) + _dot(x3, t)


def _rms(x, g):
    return x * lax.rsqrt(jnp.mean(x * x, axis=-1, keepdims=True) + EPS) * g


def _mod_row_of_tile(i, tiles_per_sample_seq, n_prompt_tiles):
    return jnp.where(i < n_prompt_tiles, 0, 1 + (i - n_prompt_tiles) // tiles_per_sample_seq)


def _ada_kernel(cv_ref, w_ref, b_ref, o_ref):
    cv = cv_ref[...]
    s = cv * jax.nn.sigmoid(cv)
    sh, sl = _split2(s)
    wh, wl = _split2(w_ref[...])
    both = _dot(jnp.concatenate([sh.astype(F32), sl.astype(F32)], axis=0).astype(BF16), wh)
    o_ref[...] = both[0:MOD_ROWS] + both[MOD_ROWS:] + _dot(sh, wl) + b_ref[...]


def _ada(cv, w_ada, b_ada):
    n = N_MOD * D_MODEL
    return pl.pallas_call(
        _ada_kernel,
        out_shape=jax.ShapeDtypeStruct((MOD_ROWS, n), F32),
        grid=(N_MOD,),
        in_specs=[pl.BlockSpec((MOD_ROWS, D_MODEL), lambda j: (0, 0)),
                  pl.BlockSpec((D_MODEL, D_MODEL), lambda j: (0, j)),
                  pl.BlockSpec((1, D_MODEL), lambda j: (0, j))],
        out_specs=pl.BlockSpec((MOD_ROWS, D_MODEL), lambda j: (0, j)),
        compiler_params=_cparams(("arbitrary",)),
        name="ada_mod",
    )(cv, w_ada, b_ada.reshape(1, n))


PREP_COLS = 512


def _prep_q_kernel(wt_ref, o_ref):
    o_ref[...] = wt_ref[...].T.astype(BF16)


def _prep_hy_kernel(wt_hbm, o_ref, buf, sem):
    start = pl.multiple_of(ML_QKVO_COLS + ML_GATE_COLS + pl.program_id(0) * PREP_COLS, 8)
    copy = pltpu.make_async_copy(wt_hbm.at[pl.ds(start, PREP_COLS), :], buf, sem)
    copy.start()
    copy.wait()
    o_ref[...] = buf[...].T.astype(BF16)


def _prep_in_weights(w_in_t):
    out_blk = pl.BlockSpec((D_MODEL, PREP_COLS), lambda j: (0, j))
    w_qkvo = pl.pallas_call(
        _prep_q_kernel,
        out_shape=jax.ShapeDtypeStruct((D_MODEL, ML_QKVO_COLS), BF16),
        grid=(ML_QKVO_COLS // PREP_COLS,),
        in_specs=[pl.BlockSpec((PREP_COLS, D_MODEL), lambda j: (j, 0))], out_specs=out_blk,
        compiler_params=_cparams(("arbitrary",)), name="prep_w_qkvo",
    )(w_in_t)
    w_hy = pl.pallas_call(
        _prep_hy_kernel,
        out_shape=jax.ShapeDtypeStruct((D_MODEL, HY_COLS), BF16),
        grid=(HY_COLS // PREP_COLS,),
        in_specs=[pl.BlockSpec(memory_space=pl.ANY)], out_specs=out_blk,
        scratch_shapes=[pltpu.VMEM((PREP_COLS, D_MODEL), F32), pltpu.SemaphoreType.DMA(())],
        compiler_params=_cparams(("arbitrary",)), name="prep_w_hy",
    )(w_in_t)
    return w_qkvo, w_hy


def _log_sigmoid(x):
    return jnp.minimum(x, 0.0) - jnp.log1p(jnp.exp(-jnp.abs(x)))


def _rows_to_cols(rows):
    ri = lax.broadcasted_iota(jnp.int32, (TILE, TILE), 0)
    ci = lax.broadcasted_iota(jnp.int32, (TILE, TILE), 1)
    eye = jnp.where(ri == ci, 1.0, 0.0).astype(BF16)
    p1 = rows.astype(BF16)
    r1 = rows - p1.astype(F32)
    p2 = r1.astype(BF16)
    p3 = (r1 - p2.astype(F32)).astype(BF16)
    return _dot_nt(eye, p1) + _dot_nt(eye, p2) + _dot_nt(eye, p3)


BIG_TILE = 4 * TILE
N_BIG_P = T_PROMPT // BIG_TILE
N_BIG = T_ALL // BIG_TILE


def _inproj_kernel(xp_ref, xs_ref, m_ref, gn_ref, wq_ref, wh_ref, wgt_ref, gbt_ref, proj_ref, gatet_ref):
    is_p = pl.program_id(0) < N_BIG_P
    halves = [slice(r * TILE, (r + 1) * TILE) for r in range(BIG_TILE // TILE)]
    hs = [_rms(jnp.where(is_p, xp_ref[rows, :], xs_ref[rows, :]), gn_ref[0:1, :]) * (1.0 + m_ref[0, 1:2, :])
          + m_ref[0, 0:1, :] for rows in halves]
    hbs = [h.astype(BF16) for h in hs]
    cb = 512
    for j in range(ML_QKVO_COLS // cb):
        for rows, hb in zip(halves, hbs):
            proj_ref[rows, j * cb:(j + 1) * cb] = _dot(hb, wq_ref[:, j * cb:(j + 1) * cb]).astype(BF16)
    for j in range(HY_COLS // cb):
        lo = ML_QKVO_COLS + j * cb
        for rows, hb in zip(halves, hbs):
            proj_ref[rows, lo:lo + cb] = _dot(hb, wh_ref[:, j * cb:(j + 1) * cb]).astype(BF16)
    wth, wtl = _split2(wgt_ref[...])
    gts = []
    for h, hb in zip(hs, hbs):
        hl = (h - hb.astype(F32)).astype(BF16)
        gt = _dot_nt(wth, hb) + _dot_nt(wth, hl) + _dot_nt(wtl, hb) + gbt_ref[...]
        row = lax.broadcasted_iota(jnp.int32, gt.shape, 0)
        gts.append(jnp.where((row % 8) >= 4, _log_sigmoid(gt), gt))
    for r, gt in enumerate(gts):
        gatet_ref[r] = gt


def _inproj(xp, xs, mods3, g_norm, w_qkvo, w_hy, wgt, gbt):
    tps = DEC_SEQ // BIG_TILE
    per = BIG_TILE // TILE
    return pl.pallas_call(
        _inproj_kernel,
        out_shape=(jax.ShapeDtypeStruct((T_ALL, MAIN_COLS), BF16),
                   jax.ShapeDtypeStruct((N_TILES, ML_GATE_COLS, TILE), F32)),
        grid=(N_BIG,),
        in_specs=[pl.BlockSpec((BIG_TILE, D_MODEL), lambda i: (jnp.minimum(i, N_BIG_P - 1), 0)),
                  pl.BlockSpec((BIG_TILE, D_MODEL), lambda i: (jnp.maximum(i - N_BIG_P, 0), 0)),
                  pl.BlockSpec((1, N_MOD, D_MODEL), lambda i: (_mod_row_of_tile(i, tps, N_BIG_P), 0, 0)),
                  pl.BlockSpec((4, D_MODEL), lambda i: (0, 0)),
                  pl.BlockSpec((D_MODEL, ML_QKVO_COLS), lambda i: (0, 0)),
                  pl.BlockSpec((D_MODEL, HY_COLS), lambda i: (0, 0)),
                  pl.BlockSpec((ML_GATE_COLS, D_MODEL), lambda i: (0, 0)),
                  pl.BlockSpec((ML_GATE_COLS, 1), lambda i: (0, 0))],
        out_specs=(pl.BlockSpec((BIG_TILE, MAIN_COLS), lambda i: (i, 0)),
                   pl.BlockSpec((per, ML_GATE_COLS, TILE), lambda i: (i, 0, 0))),
        compiler_params=_cparams(("arbitrary",)),
        name="in_proj",
    )(xp, xs, mods3, g_norm, w_qkvo, w_hy, wgt, gbt)


ST_ROWS = ML_HEAD_DIM + 16


def _mlstm_kernel(*refs, seq_len, has_state):
    if has_state:
        (q_ref, k_ref, v_ref, o_ref, gt_ref, gain_ref, c0_ref, n0_ref, m0_ref,
         y_ref, c_ref, n_ref, m_ref, vt_ref, hf_ref, hb_ref, st_ref, ms_ref) = refs
    else:
        (q_ref, k_ref, v_ref, o_ref, gt_ref, gain_ref,
         y_ref, c_ref, n_ref, m_ref, vt_ref, hf_ref, hb_ref, st_ref, ms_ref) = refs
    ch = TILE
    nc = seq_len // ch
    hd = ML_HEAD_DIM
    key = lax.broadcasted_iota(jnp.int32, (ch, ch), 0)
    qry = lax.broadcasted_iota(jnp.int32, (ch, ch), 1)
    key_le = key <= qry
    key_ge = key >= qry
    t_le = jnp.where(key_le, 1.0, 0.0).astype(BF16)
    t_ge = jnp.where(key_ge, 1.0, 0.0).astype(BF16)
    sub16 = lax.broadcasted_iota(jnp.int32, (16, ch), 0)
    ln_scale = math.log(K_SCALE)

    for c in range(nc):
        for h in range(ML_HEADS):
            cols = slice(h * hd, (h + 1) * hd)
            vt_ref[c, cols, :] = v_ref[c * ch:(c + 1) * ch, cols].T

    for d in range(2):
        for h in range(ML_HEADS):
            r = d * ML_HEADS + h
            st_ref[r] = jnp.zeros((ST_ROWS, hd), F32)
            if has_state:
                st_ref[r, 0:hd, :] = c0_ref[0, d, h].T
                st_ref[r, hd:hd + 1, :] = n0_ref[0, d, h:h + 1, :]
                ms_ref[r] = jnp.broadcast_to(m0_ref[0, r:r + 1, :], (1, ch))
            else:
                ms_ref[r] = jnp.zeros((1, ch), F32)

    def step(t, carry):
        for d in range(2):
            c = t if d == 0 else nc - 1 - t
            rows = pl.ds(pl.multiple_of(c * ch, ch), ch)
            grow = gt_ref[c]
            brow_all = _dot_exact_rhs(grow, t_le if d == 0 else t_ge)
            ccol_all = _rows_to_cols(grow - pltpu.roll(brow_all, ML_GATE_COLS - ML_HEADS, axis=0))
            mask = key_le if d == 0 else key_ge
            hacc_ref = hf_ref if d == 0 else hb_ref
            heads = range(ML_HEADS)
            regs = [d * ML_HEADS + h for h in heads]
            colss = [slice(h * hd, (h + 1) * hd) for h in heads]
            qs = [q_ref[rows, cols] for cols in colss]
            ks = [k_ref[rows, cols] for cols in colss]
            vts = [vt_ref[c, cols, :] for cols in colss]
            sts = [st_ref[r] for r in regs]
            m_prevs = [ms_ref[r] for r in regs]
            b_rows = [brow_all[(1 + 2 * d) * ML_HEADS + h:(1 + 2 * d) * ML_HEADS + h + 1, :] for h in heads]
            ig_rows = [grow[2 * d * ML_HEADS + h:2 * d * ML_HEADS + h + 1, :] for h in heads]
            qks = [_dot_nt(k, q) for k, q in zip(ks, qs)]
            iqs = [_dot_nt(st.astype(BF16), q) for st, q in zip(sts, qs)]
            ss, sc_inters, m_poss = [], [], []
            for h in heads:
                icol = 2 * d * ML_HEADS + h
                c_col = ccol_all[:, icol:icol + 1]
                logd = jnp.where(mask, b_rows[h] + c_col, -jnp.inf)
                inter = b_rows[h] + m_prevs[h]
                m_pos = jnp.maximum(inter, jnp.max(logd, axis=0, keepdims=True))
                ss.append(qks[h] * jnp.exp(logd - (m_pos - ln_scale)))
                sc_inters.append(jnp.exp(inter - m_pos))
                m_poss.append(m_pos)
            pvs = [_dot(vt, s.astype(BF16)) for vt, s in zip(vts, ss)]
            for h in heads:
                num = sc_inters[h] * iqs[h][0:hd] + pvs[h]
                den = sc_inters[h] * iqs[h][hd:hd + 1] + jnp.sum(ss[h], axis=0, keepdims=True)
                hacc_ref[c, colss[h], :] = num * (1.0 / jnp.maximum(jnp.abs(den), jnp.exp(-m_poss[h])))
            lhss, decays = [], []
            for h in heads:
                b_row = b_rows[h]
                b_last = b_row[:, ch - 1:ch] if d == 0 else b_row[:, 0:1]
                logw = b_last - b_row + ig_rows[h]
                m_new = jnp.maximum(b_last + m_prevs[h], jnp.max(logw, axis=1, keepdims=True))
                w = jnp.exp(logw - (m_new - ln_scale))
                decays.append(jnp.exp(b_last + m_prevs[h] - m_new))
                lhss.append(jnp.concatenate([(vts[h].astype(F32) * w).astype(BF16),
                                             jnp.where(sub16 == 0, w, 0.0).astype(BF16)], axis=0))
                ms_ref[regs[h]] = m_new
            upds = [_dot(lhs, k) for lhs, k in zip(lhss, ks)]
            for h in heads:
                st_ref[regs[h]] = decays[h][:, 0:hd] * sts[h] + upds[h]
        return carry

    lax.fori_loop(0, nc, step, 0)

    for d in range(2):
        for h in range(ML_HEADS):
            r = d * ML_HEADS + h
            c_ref[0, d, h] = st_ref[r, 0:hd, :].T
            n_ref[0, d, h:h + 1, :] = st_ref[r, hd:hd + 1, :]
            m_ref[0, r:r + 1, :] = ms_ref[r][:, 0:hd]
    for c in range(nc):
        for h in range(ML_HEADS):
            cols = slice(h * hd, (h + 1) * hd)
            ht = hf_ref[c, cols, :] + hb_ref[c, cols, :]
            ht = ht * lax.rsqrt(jnp.mean(ht * ht, axis=0, keepdims=True) + EPS)
            rows = slice(c * ch, (c + 1) * ch)
            y = ht.T * gain_ref[:, cols] * jax.nn.sigmoid(o_ref[rows, cols].astype(F32))
            y_ref[rows, cols] = y.astype(BF16)


def _mlstm(proj, gates_t, gain, state, seq_len, n_seq, row_block_off):
    has_state = state is not None
    tiles = seq_len // TILE
    off = row_block_off
    qkvo_specs = [pl.BlockSpec((seq_len, ML_WIDTH), functools.partial(lambda b, j: (off + b, j), j=j))
                  for j in range(4)]
    in_specs = qkvo_specs + [
        pl.BlockSpec((tiles, ML_GATE_COLS, TILE), lambda b: (off + b, 0, 0)),
        pl.BlockSpec((1, ML_WIDTH), lambda b: (0, 0)),
    ]
    args = [proj, proj, proj, proj, gates_t, gain]
    if has_state:
        c0, n0, m0 = state
        in_specs += [
            pl.BlockSpec((1, 2, ML_HEADS, ML_HEAD_DIM, ML_HEAD_DIM), lambda b: (b, 0, 0, 0, 0)),
            pl.BlockSpec((1, 2, ML_HEADS, ML_HEAD_DIM), lambda b: (b, 0, 0, 0)),
            pl.BlockSpec((1, 2 * ML_HEADS, 1), lambda b: (b, 0, 0)),
        ]
        args += [c0, n0, m0]
    out_shape = (jax.ShapeDtypeStruct((n_seq * seq_len, ML_WIDTH), BF16),
                 jax.ShapeDtypeStruct((n_seq, 2, ML_HEADS, ML_HEAD_DIM, ML_HEAD_DIM), F32),
                 jax.ShapeDtypeStruct((n_seq, 2, ML_HEADS, ML_HEAD_DIM), F32),
                 jax.ShapeDtypeStruct((n_seq, 2 * ML_HEADS, ML_HEAD_DIM), F32))
    out_specs = (pl.BlockSpec((seq_len, ML_WIDTH), lambda b: (b, 0)),
                 pl.BlockSpec((1, 2, ML_HEADS, ML_HEAD_DIM, ML_HEAD_DIM), lambda b: (b, 0, 0, 0, 0)),
                 pl.BlockSpec((1, 2, ML_HEADS, ML_HEAD_DIM), lambda b: (b, 0, 0, 0)),
                 pl.BlockSpec((1, 2 * ML_HEADS, ML_HEAD_DIM), lambda b: (b, 0, 0)))
    scratch = [pltpu.VMEM((tiles, ML_WIDTH, TILE), BF16),
               pltpu.VMEM((tiles, ML_WIDTH, TILE), F32), pltpu.VMEM((tiles, ML_WIDTH, TILE), F32),
               pltpu.VMEM((2 * ML_HEADS, ST_ROWS, ML_HEAD_DIM), F32),
               pltpu.VMEM((2 * ML_HEADS, 1, TILE), F32)]
    return pl.pallas_call(
        functools.partial(_mlstm_kernel, seq_len=seq_len, has_state=has_state),
        out_shape=out_shape, grid=(n_seq,), in_specs=in_specs, out_specs=out_specs,
        scratch_shapes=scratch, compiler_params=_cparams(("arbitrary",)),
        name=f"mlstm_{seq_len}",
    )(*args)


def _dft_mats(seq_len):
    k = np.arange(seq_len, dtype=np.int64)[:, None]
    d = np.arange(seq_len, dtype=np.int64)[None, :]
    ang = np.pi * ((k * d) % (2 * seq_len)).astype(np.float64) / seq_len
    sinm = np.sin(ang)
    sinm[0, :] = np.where(d[0] % 2 == 0, 1.0, -1.0)
    f = np.concatenate([np.cos(ang), sinm], axis=0).astype(np.float32)
    return jnp.asarray(f).astype(BF16), jnp.asarray(np.ascontiguousarray(f.T)).astype(BF16)


def _filter_feats(seq_len):
    t = np.linspace(0.0, 1.0, seq_len, dtype=np.float64)[:, None]
    wpos = 2.0 * np.pi * np.arange(seq_len, dtype=np.float64)[:, None] / seq_len
    bands = np.linspace(1e-4, HY_BANDS - 1, HY_BANDS, dtype=np.float64)[None, :]
    z = np.concatenate([t, np.cos(bands * wpos), -np.sin(bands * wpos)], axis=-1)
    return jnp.asarray(np.pad(z, ((0, 0), (0, 128 - HY_EMB))).astype(np.float32))


def _filter_kernel(z_ref, w1_ref, b1_ref, w2_ref, b2_ref, w3_ref, b3_ref, dec_ref, f_ref,
                   a_ref, b_ref, d_ref, *, seq_len):
    n = 2 * seq_len
    oc = 2 * HY_WIDTH
    z = z_ref[...]
    h = jnp.sin(_dot3(z, w1_ref[...]) + b1_ref[...])
    h = jnp.sin(_dot3(h, w2_ref[...]) + b2_ref[...])
    t = z[:, 0:1]
    di = lax.broadcasted_iota(jnp.int32, (seq_len, 1), 0)
    sgn = jnp.where(di % 2 == 0, 1.0, -1.0)
    first = di == 0
    ssums, sdifs = [], []
    for o in range(HY_ORDER):
        cols = slice(o * oc, (o + 1) * oc)
        g = _dot3(h, w3_ref[:, cols]) + b3_ref[:, cols]
        g = g * (jnp.exp(-t * jnp.abs(dec_ref[:, cols])) + HY_MOD_SHIFT)
        ss = jnp.sum(g * g, axis=0, keepdims=True)
        inv = lax.rsqrt(ss[:, :HY_WIDTH] + ss[:, HY_WIDTH:] + EPS)
        hp = g[:, :HY_WIDTH] * inv
        hn = g[:, HY_WIDTH:] * inv
        ssums.append(hp + hn)
        sdifs.append(hp - hn)
    hcs = [_dot(f_ref[0:seq_len, :], s.astype(BF16)) for s in ssums]
    hss = [_dot(f_ref[seq_len:n, :], s.astype(BF16)) for s in sdifs]
    for o in range(HY_ORDER):
        nyq = jnp.sum(ssums[o] * sgn, axis=0, keepdims=True)
        a_ref[o] = hcs[o] * jnp.where(first, 1.0 / n, 2.0 / n)
        b_ref[o] = jnp.where(first, 0.0, hss[o] * (2.0 / n))
        d_ref[o] = jnp.where(first, nyq * (1.0 / n), hcs[o] * (2.0 / n))


def _hyena_filters(seq_len, f, w1p, b1, w2, b2, w3, b3, dec):
    z = _filter_feats(seq_len)
    out = jax.ShapeDtypeStruct((HY_ORDER, seq_len, HY_WIDTH), F32)
    return pl.pallas_call(
        functools.partial(_filter_kernel, seq_len=seq_len),
        out_shape=(out, out, out),
        compiler_params=pltpu.CompilerParams(vmem_limit_bytes=VMEM_LIMIT),
        name=f"hyena_filter_{seq_len}",
    )(z, w1p, b1, w2, b2, w3, b3, dec, f)


def _hyena_kernel(x1_ref, x2_ref, v_ref, cw1_ref, cw2_ref, cwv_ref, a_ref, b_ref, d_ref, bias_ref,
                  f_ref, ft_ref, z_ref, *, seq_len, width, seqs):
    rows = seqs * seq_len
    ti = lax.broadcasted_iota(jnp.int32, (rows, 1), 0)
    has_prev = (ti % width) != 0
    has_next = (ti % width) != (width - 1)

    def short_conv(x_ref, w_ref):
        x = x_ref[...].astype(F32)
        prev = jnp.where(has_prev, pltpu.roll(x, 1, axis=0), 0.0)
        nxt = jnp.where(has_next, pltpu.roll(x, rows - 1, axis=0), 0.0)
        return w_ref[0:1, :] * prev + w_ref[1:2, :] * x + w_ref[2:3, :] * nxt

    gates = (short_conv(x1_ref, cw1_ref), short_conv(x2_ref, cw2_ref))
    v = short_conv(v_ref, cwv_ref)
    sls = [slice(i * seq_len, (i + 1) * seq_len) for i in range(seqs)]
    zs = [v[sl] for sl in sls]
    for o in range(HY_ORDER):
        a, b, dd = a_ref[o], b_ref[o], d_ref[o]
        us = [_dot(f_ref[...], z.astype(BF16)) for z in zs]
        ys = []
        for u in us:
            ut = u[:seq_len]
            ub = u[seq_len:]
            ys.append(((ut * a - ub * b).astype(BF16), (ut * b + ub * dd).astype(BF16)))
        convs = [_dot(ft_ref[:, :seq_len], yt) + _dot(ft_ref[:, seq_len:], yb) for yt, yb in ys]
        zs = [gates[o][sl] * (y + bias_ref[o:o + 1, :] * z) for sl, y, z in zip(sls, convs, zs)]
    for sl, z in zip(sls, zs):
        z_ref[sl, :] = z.astype(BF16)


def _hyena(proj, conv_w, coefs, hy_bias, f, ft, seq_len, n_seq, row_off, width, seqs):
    cb = 256
    nblk = HY_WIDTH // cb
    base = ML_QKVO_COLS // cb
    rows = seqs * seq_len
    off = row_off // rows
    a, b, d = coefs

    def col_spec(part):
        return pl.BlockSpec((rows, cb), lambda j, s: (off + s, base + part * nblk + j))

    def w_spec(part):
        return pl.BlockSpec((3, cb), lambda j, s: (0, part * nblk + j))

    coef_spec = pl.BlockSpec((HY_ORDER, seq_len, cb), lambda j, s: (0, 0, j))
    return pl.pallas_call(
        functools.partial(_hyena_kernel, seq_len=seq_len, width=width, seqs=seqs),
        out_shape=jax.ShapeDtypeStruct((n_seq * seq_len, HY_WIDTH), BF16),
        grid=(nblk, n_seq // seqs),
        in_specs=[col_spec(0), col_spec(1), col_spec(2), w_spec(0), w_spec(1), w_spec(2),
                  coef_spec, coef_spec, coef_spec,
                  pl.BlockSpec((HY_ORDER, cb), lambda j, s: (0, j)),
                  pl.BlockSpec((2 * seq_len, seq_len), lambda j, s: (0, 0)),
                  pl.BlockSpec((seq_len, 2 * seq_len), lambda j, s: (0, 0))],
        out_specs=pl.BlockSpec((rows, cb), lambda j, s: (s, j)),
        compiler_params=_cparams(("arbitrary", "arbitrary")),
        name=f"hyena_conv_{seq_len}",
    )(proj, proj, proj, conv_w, conv_w, conv_w, a, b, d, hy_bias, f, ft)


def _first_max(x, n):
    mx = jnp.max(x, axis=0, keepdims=True)
    row = lax.broadcasted_iota(jnp.int32, x.shape, 0).astype(F32)
    idx = jnp.min(jnp.where(x == mx, row, float(n)), axis=0, keepdims=True)
    return mx, idx.astype(jnp.int32)


ROUTER_ROWS = 32
PAIRS_PER_GROUP = 6
N_BUCKETS = N_GROUPS * PAIRS_PER_GROUP
PAIR_SLOTS = ((0, 1), (0, 2), (0, 3), (1, 3), (1, 2), (3, 2))
LANES = 128
H2_EXT = D_MODEL + LANES
ROW_TILE = 256
ROW_CAP = T_ALL + N_BUCKETS * ROW_TILE
N_ROW_TILES = ROW_CAP // ROW_TILE


def _outproj_kernel(xp_ref, xs_ref, yp_ref, ys_ref, zp_ref, zs_ref, m_ref, gn_ref, wo_ref, wr_ref, br_ref,
                    x1_ref, h2_ref, bid_ref):
    is_p = pl.program_id(0) < N_BIG_P
    wrh, wrl = _split2(wr_ref[...])
    halves = [slice(r * TILE, (r + 1) * TILE) for r in range(BIG_TILE // TILE)]
    ys = [_dot(jnp.where(is_p, yp_ref[rows, :], ys_ref[rows, :]), wo_ref[0:ML_WIDTH, :])
          + _dot(jnp.where(is_p, zp_ref[rows, :], zs_ref[rows, :]), wo_ref[ML_WIDTH:, :]) for rows in halves]
    h2s = []
    for rows, y in zip(halves, ys):
        x = jnp.where(is_p, xp_ref[rows, :], xs_ref[rows, :])
        x1 = x + m_ref[0, 2:3, :] * _rms(y, gn_ref[1:2, :])
        x1_ref[rows, :] = x1
        h2 = _rms(x1, gn_ref[2:3, :]) * (1.0 + m_ref[0, 4:5, :]) + m_ref[0, 3:4, :]
        h2_ref[rows, 0:D_MODEL] = h2
        h2s.append(h2)
    logits = []
    for h2 in h2s:
        h2h, h2l = _split2(h2)
        logits.append(_dot_nt(wrh, h2h) + _dot_nt(wrh, h2l) + _dot_nt(wrl, h2h) + br_ref[...])
    routed = [_route_tile(lg) for lg in logits]
    for r, (rows, (gate_rows, bucket)) in enumerate(zip(halves, routed)):
        h2_ref[rows, D_MODEL:H2_EXT] = jnp.zeros((TILE, LANES), F32)
        h2_ref[rows, D_MODEL:D_MODEL + 8] = _rows_to_cols(gate_rows)
        bid_ref[r] = bucket


def _route_tile(logits):
    lc = logits[0:N_GROUPS]
    mx, gi = _first_max(lc, N_GROUPS)
    p_grp = 1.0 / jnp.sum(jnp.exp(lc - mx), axis=0, keepdims=True)
    lsel = jnp.zeros((EXPERTS_PER_GROUP, TILE), F32)
    for g in range(N_GROUPS):
        lo = N_GROUPS + g * EXPERTS_PER_GROUP
        lsel = jnp.where(gi == g, logits[lo:lo + EXPERTS_PER_GROUP], lsel)
    l1, i1 = _first_max(lsel, EXPERTS_PER_GROUP)
    sub4 = lax.broadcasted_iota(jnp.int32, lsel.shape, 0)
    l2, i2 = _first_max(jnp.where(sub4 == i1, -jnp.inf, lsel), EXPERTS_PER_GROUP)
    e2 = jnp.exp(l2 - l1)
    w1 = p_grp / (1.0 + e2)
    w2 = p_grp * e2 / (1.0 + e2)
    lo_e = jnp.minimum(i1, i2)
    hi_e = jnp.maximum(i1, i2)
    pair = jnp.where(lo_e == 0, hi_e - 1, jnp.where(lo_e == 1, jnp.where(hi_e == 3, 3, 4), 5))
    slot_a = jnp.where(pair == 5, hi_e, lo_e)
    first_in_a = i1 == slot_a
    w_a = jnp.where(first_in_a, w1, w2)
    w_b = jnp.where(first_in_a, w2, w1)
    sub = lax.broadcasted_iota(jnp.int32, (8, TILE), 0)
    gate_rows = jnp.where(sub == 0, w_a, jnp.where(sub == 1, w_b, 0.0))
    return gate_rows, gi * PAIRS_PER_GROUP + pair


def _outproj(xp, xs, yp, ys, zp, zs, mods3, g_norm, w_out, w_r, b_r):
    tps = DEC_SEQ // BIG_TILE
    per = BIG_TILE // TILE
    pidx = lambda i: (jnp.minimum(i, N_BIG_P - 1), 0)
    sidx = lambda i: (jnp.maximum(i - N_BIG_P, 0), 0)
    return pl.pallas_call(
        _outproj_kernel,
        out_shape=(jax.ShapeDtypeStruct((T_ALL, D_MODEL), F32),
                   jax.ShapeDtypeStruct((T_ALL, H2_EXT), F32),
                   jax.ShapeDtypeStruct((N_TILES, 1, TILE), jnp.int32)),
        grid=(N_BIG,),
        in_specs=[pl.BlockSpec((BIG_TILE, D_MODEL), pidx), pl.BlockSpec((BIG_TILE, D_MODEL), sidx),
                  pl.BlockSpec((BIG_TILE, ML_WIDTH), pidx), pl.BlockSpec((BIG_TILE, ML_WIDTH), sidx),
                  pl.BlockSpec((BIG_TILE, HY_WIDTH), pidx), pl.BlockSpec((BIG_TILE, HY_WIDTH), sidx),
                  pl.BlockSpec((1, N_MOD, D_MODEL), lambda i: (_mod_row_of_tile(i, tps, N_BIG_P), 0, 0)),
                  pl.BlockSpec((4, D_MODEL), lambda i: (0, 0)),
                  pl.BlockSpec((D_MODEL, D_MODEL), lambda i: (0, 0)),
                  pl.BlockSpec((ROUTER_ROWS, D_MODEL), lambda i: (0, 0)),
                  pl.BlockSpec((ROUTER_ROWS, 1), lambda i: (0, 0))],
        out_specs=(pl.BlockSpec((BIG_TILE, D_MODEL), lambda i: (i, 0)),
                   pl.BlockSpec((BIG_TILE, H2_EXT), lambda i: (i, 0)),
                   pl.BlockSpec((per, 1, TILE), lambda i: (i, 0, 0))),
        compiler_params=_cparams(("arbitrary",)),
        name="out_proj_router",
    )(xp, xs, yp, ys, zp, zs, mods3, g_norm, w_out, w_r, b_r)


def _route_kernel(bid_ref, pos_ref, meta_ref):
    nb = 32
    tm = float(ROW_TILE)
    sub = lax.broadcasted_iota(jnp.int32, (nb, TILE), 0)
    ri = lax.broadcasted_iota(jnp.int32, (TILE, TILE), 0)
    ci = lax.broadcasted_iota(jnp.int32, (TILE, TILE), 1)
    before = jnp.where(ri < ci, 1.0, 0.0).astype(BF16)

    def onehot(blk):
        return jnp.where(sub == bid_ref[blk], 1.0, 0.0)

    zeros = jnp.zeros((nb, 1), F32)
    cnt = lax.fori_loop(0, N_TILES, lambda blk, c: c + jnp.sum(onehot(blk), axis=1, keepdims=True), zeros)
    padded = jnp.floor((cnt + (tm - 1.0)) * (1.0 / tm)) * tm
    r32 = lax.broadcasted_iota(jnp.int32, (nb, nb), 0)
    c32 = lax.broadcasted_iota(jnp.int32, (nb, nb), 1)
    padded_row = jnp.sum(jnp.where(r32 == c32, padded, 0.0), axis=0, keepdims=True)
    offs = jnp.sum(jnp.where(c32 < r32, padded_row, 0.0), axis=1, keepdims=True)
    ends = offs + padded

    def place(blk, seen):
        oh = onehot(blk)
        rank = _dot(oh.astype(BF16), before)
        pos = jnp.sum(oh * (rank + seen + offs), axis=0, keepdims=True)
        pos_ref[blk] = pos.astype(jnp.int32)
        return seen + jnp.sum(oh, axis=1, keepdims=True)

    lax.fori_loop(0, N_TILES, place, zeros)

    start = lax.broadcasted_iota(jnp.int32, (nb, 128), 1).astype(F32) * tm
    bsub = lax.broadcasted_iota(jnp.int32, (nb, 128), 0)
    done = jnp.where((bsub < N_BUCKETS) & (ends <= start), 1.0, 0.0)
    tb = jnp.sum(done, axis=0, keepdims=True)
    valid = jnp.where(tb < N_BUCKETS, 1.0, 0.0)
    tbc = jnp.minimum(tb, N_BUCKETS - 1.0)
    grp = jnp.floor((tbc + 0.5) * (1.0 / PAIRS_PER_GROUP))
    pair = tbc - PAIRS_PER_GROUP * grp
    loc_a = jnp.zeros_like(pair)
    loc_b = jnp.zeros_like(pair)
    for k, (sa, sb) in enumerate(PAIR_SLOTS):
        loc_a = jnp.where(pair == k, float(sa), loc_a)
        loc_b = jnp.where(pair == k, float(sb), loc_b)
    mine = bsub.astype(F32) == tbc
    used = jnp.sum(jnp.where(mine, offs + cnt, 0.0), axis=0, keepdims=True)
    n_rows = jnp.clip(used - start[0:1], 0.0, tm) * valid
    row8 = lax.broadcasted_iota(jnp.int32, (8, 128), 0)
    meta = jnp.where(row8 == 0, grp * EXPERTS_PER_GROUP + loc_a,
                     jnp.where(row8 == 1, grp * EXPERTS_PER_GROUP + loc_b,
                               jnp.where(row8 == 2, valid, jnp.where(row8 == 3, n_rows, 0.0))))
    meta_ref[...] = meta.astype(jnp.int32)


def _route(bid):
    return pl.pallas_call(
        _route_kernel,
        out_shape=(jax.ShapeDtypeStruct((N_TILES, 1, TILE), jnp.int32),
                   jax.ShapeDtypeStruct((8, 128), jnp.int32)),
        compiler_params=pltpu.CompilerParams(vmem_limit_bytes=VMEM_LIMIT),
        name="moe_route",
    )(bid)


def _moe_kernel(meta_ref, pos_ref, h2_hbm, wga_ref, wua_ref, wda_ref, wgb_ref, wub_ref, wdb_ref,
                y_ref, src_ref, xbuf, sem, wga_s, wua_s, wda_s, wgb_s, wub_s, wdb_s):
    j = pl.program_id(0)

    def row_copy(tile, r, slot):
        tok = src_ref[tile * ROW_TILE + r]
        return pltpu.make_async_copy(h2_hbm.at[pl.ds(tok, 1), :], xbuf.at[slot, pl.ds(r, 1), :], sem.at[slot])

    def issue_rows(tile, slot, lo, hi):
        for r in range(lo, hi):
            row_copy(tile, r, slot).start()

    def wait(slot):
        pltpu.make_async_copy(h2_hbm.at[pl.ds(0, ROW_TILE), :], xbuf.at[slot], sem.at[slot]).wait()

    @pl.when(j == 0)
    def _():
        def fill(t, c):
            def body(r, c2):
                p = t * ROW_TILE + r
                src_ref[p] = p % T_ALL
                return c2
            n = meta_ref[3, t]
            lax.fori_loop(n, jnp.where(meta_ref[2, t] == 1, ROW_TILE, n), body, 0)
            return c
        lax.fori_loop(0, N_ROW_TILES, fill, 0)

        def invert(t, c):
            src_ref[pos_ref[t]] = t
            return c
        lax.fori_loop(0, T_ALL, invert, 0, unroll=8)

        @pl.when(meta_ref[2, 0] == 1)
        def _():
            def body(g, c):
                for k in range(8):
                    row_copy(0, g * 8 + k, 0).start()
                return c
            lax.fori_loop(0, ROW_TILE // 8, body, 0)

    nxt = jnp.minimum(j + 1, N_ROW_TILES - 1)
    has_next = (j + 1 < N_ROW_TILES) & (meta_ref[2, nxt] == 1)
    valid = meta_ref[2, j] == 1
    prev = jnp.maximum(j - 1, 0)

    @pl.when(valid & ((j == 0) | (meta_ref[0, j] != meta_ref[0, prev])))
    def _():
        wga_s[...] = wga_ref[0].astype(BF16)
        wua_s[...] = wua_ref[0].astype(BF16)
        wda_s[...] = wda_ref[0].astype(BF16)

    @pl.when(valid & ((j == 0) | (meta_ref[1, j] != meta_ref[1, prev])))
    def _():
        wgb_s[...] = wgb_ref[0].astype(BF16)
        wub_s[...] = wub_ref[0].astype(BF16)
        wdb_s[...] = wdb_ref[0].astype(BF16)

    def compute(fetch_next):
        slot = j % 2
        nslot = nxt % 2
        step = ROW_TILE // 8
        batches = iter(range(0, ROW_TILE, step))

        def fetch():
            if fetch_next:
                lo = next(batches)
                issue_rows(nxt, nslot, lo, lo + step)

        wait(slot)
        x = xbuf[slot, :, 0:D_MODEL].astype(BF16)
        gates = xbuf[slot, :, D_MODEL:H2_EXT]
        hg_a = _dot(x, wga_s[...])
        fetch()
        hu_a = _dot(x, wua_s[...])
        fetch()
        hg_b = _dot(x, wgb_s[...])
        fetch()
        hu_b = _dot(x, wub_s[...])
        fetch()
        act_a = (hg_a * jax.nn.sigmoid(hg_a) * hu_a * gates[:, 0:1]).astype(BF16)
        fetch()
        act_b = (hg_b * jax.nn.sigmoid(hg_b) * hu_b * gates[:, 1:2]).astype(BF16)
        fetch()
        y = _dot(act_a, wda_s[...])
        fetch()
        y = y + _dot(act_b, wdb_s[...])
        fetch()
        y_ref[...] = y

    @pl.when(valid & has_next)
    def _():
        compute(True)

    @pl.when(valid & jnp.logical_not(has_next))
    def _():
        compute(False)

    @pl.when(jnp.logical_not(valid))
    def _():
        y_ref[...] = jnp.zeros_like(y_ref)


def _moe(meta, pos, h2ext, w_gate, w_up, w_down):
    up_spec = lambda slot: pl.BlockSpec((1, D_MODEL, EXPERT_FF), lambda j, meta, pos: (meta[slot, j], 0, 0))
    down_spec = lambda slot: pl.BlockSpec((1, EXPERT_FF, D_MODEL), lambda j, meta, pos: (meta[slot, j], 0, 0))
    grid_spec = pltpu.PrefetchScalarGridSpec(
        num_scalar_prefetch=2,
        grid=(N_ROW_TILES,),
        in_specs=[pl.BlockSpec(memory_space=pl.ANY),
                  up_spec(0), up_spec(0), down_spec(0), up_spec(1), up_spec(1), down_spec(1)],
        out_specs=pl.BlockSpec((ROW_TILE, D_MODEL), lambda j, meta, pos: (j, 0)),
        scratch_shapes=[pltpu.SMEM((ROW_CAP,), jnp.int32),
                        pltpu.VMEM((2, ROW_TILE, H2_EXT), F32),
                        pltpu.SemaphoreType.DMA((2,)),
                        pltpu.VMEM((D_MODEL, EXPERT_FF), BF16), pltpu.VMEM((D_MODEL, EXPERT_FF), BF16),
                        pltpu.VMEM((EXPERT_FF, D_MODEL), BF16),
                        pltpu.VMEM((D_MODEL, EXPERT_FF), BF16), pltpu.VMEM((D_MODEL, EXPERT_FF), BF16),
                        pltpu.VMEM((EXPERT_FF, D_MODEL), BF16)])
    return pl.pallas_call(
        _moe_kernel,
        out_shape=jax.ShapeDtypeStruct((ROW_CAP, D_MODEL), F32),
        grid_spec=grid_spec,
        compiler_params=_cparams(("arbitrary",)),
        name="moe_experts",
    )(meta, pos, h2ext, w_gate, w_up, w_down, w_gate, w_up, w_down)


def _final_kernel(pos_ref, y_hbm, x1_ref, m_ref, gn_ref, op_ref, os_ref, ybuf, sem):
    i = pl.program_id(0)

    def row_copy(tile, r, slot):
        p = pos_ref[tile * TILE + r]
        return pltpu.make_async_copy(y_hbm.at[pl.ds(p, 1), :], ybuf.at[slot, pl.ds(r, 1), :], sem.at[slot])

    def issue(tile, slot):
        def body(r2, c):
            row_copy(tile, 2 * r2, slot).start(priority=0)
            row_copy(tile, 2 * r2 + 1, slot).start(priority=1)
            return c
        lax.fori_loop(0, TILE // 2, body, 0, unroll=4)

    def wait(slot):
        pltpu.make_async_copy(y_hbm.at[pl.ds(0, TILE)], ybuf.at[slot], sem.at[slot]).wait()

    @pl.when(i == 0)
    def _():
        issue(0, 0)

    @pl.when(i + 1 < N_TILES)
    def _():
        issue(i + 1, (i + 1) % 2)

    slot = i % 2
    wait(slot)
    out = x1_ref[...] + m_ref[0, 5:6, :] * _rms(ybuf[slot], gn_ref[3:4, :])

    @pl.when(i < N_TILES_P)
    def _():
        op_ref[...] = out

    @pl.when(i >= N_TILES_P)
    def _():
        os_ref[...] = out


def _final(pos, y_sorted, x1, mods3, g_norm):
    tps = DEC_SEQ // TILE
    grid_spec = pltpu.PrefetchScalarGridSpec(
        num_scalar_prefetch=1,
        grid=(N_TILES,),
        in_specs=[pl.BlockSpec(memory_space=pl.ANY),
                  pl.BlockSpec((TILE, D_MODEL), lambda i, pos: (i, 0)),
                  pl.BlockSpec((1, N_MOD, D_MODEL), lambda i, pos: (_mod_row_of_tile(i, tps, N_TILES_P), 0, 0)),
                  pl.BlockSpec((4, D_MODEL), lambda i, pos: (0, 0))],
        out_specs=(pl.BlockSpec((TILE, D_MODEL), lambda i, pos: (jnp.minimum(i, N_TILES_P - 1), 0)),
                   pl.BlockSpec((TILE, D_MODEL), lambda i, pos: (jnp.maximum(i - N_TILES_P, 0), 0))),
        scratch_shapes=[pltpu.VMEM((2, TILE, D_MODEL), F32), pltpu.SemaphoreType.DMA((2,))])
    return pl.pallas_call(
        _final_kernel,
        out_shape=(jax.ShapeDtypeStruct((T_PROMPT, D_MODEL), F32),
                   jax.ShapeDtypeStruct((T_SAMPLE, D_MODEL), F32)),
        grid_spec=grid_spec,
        compiler_params=_cparams(("arbitrary",)),
        name="moe_combine_final",
    )(pos, y_sorted, x1, mods3, g_norm)


def kernel(x_prompt, x_sample, state_C, state_n, state_m, c, c_ctx, w_ada, b_ada, g_norm, w_in, ml_gate_bias, ml_head_gain, hy_conv_w, hy_f_w1, hy_f_b1, hy_f_w2, hy_f_b2, hy_f_w3, hy_f_b3, hy_decay, hy_bias, w_out, w_rc, b_rc, w_rf, b_rf, w_gate, w_up, w_down):
    xp = x_prompt.reshape(T_PROMPT, D_MODEL)
    xs = x_sample.reshape(T_SAMPLE, D_MODEL)
    gn = g_norm[0]

    cv = jnp.concatenate([c_ctx[None, :], c, jnp.zeros((MOD_ROWS - 1 - DEC_BATCH, D_MODEL), F32)], axis=0)
    mods3 = _ada(cv, w_ada[0], b_ada[0]).reshape(MOD_ROWS, N_MOD, D_MODEL)

    w_in0 = w_in[0]
    w_qkvo, w_hy = _prep_in_weights(w_in0.T)
    wg = w_in0[:, ML_QKVO_COLS:ML_QKVO_COLS + ML_GATE_COLS]
    gbt = ml_gate_bias[0].reshape(ML_GATE_COLS, 1)
    proj, gates_t = _inproj(xp, xs, mods3, gn, w_qkvo, w_hy, wg.T, gbt)

    gain = ml_head_gain[0].reshape(1, ML_WIDTH)
    y_ml_p, c_new, n_new, m_new = _mlstm(proj, gates_t, gain, None, SEQ, BATCH, 0)
    state = (state_C[:, 0], state_n[:, 0], state_m[:, 0].reshape(DEC_BATCH, 2 * ML_HEADS, 1))
    y_ml_s, _, _, _ = _mlstm(proj, gates_t, gain, state, DEC_SEQ, DEC_BATCH, T_PROMPT // DEC_SEQ)

    w1p = jnp.pad(hy_f_w1[0], ((0, 128 - HY_EMB), (0, 0)))
    b1 = hy_f_b1[0].reshape(1, -1)
    b2 = hy_f_b2[0].reshape(1, -1)
    b3 = hy_f_b3[0].reshape(1, -1)
    dec = hy_decay[0].reshape(1, -1)
    z_parts = []
    for seq_len, n_seq, row_off, width, seqs in ((SEQ, BATCH, 0, SEQ, 4), (DEC_SEQ, DEC_BATCH, T_PROMPT, GRID_W, 2)):
        f, ft = _dft_mats(seq_len)
        coefs = _hyena_filters(seq_len, f, w1p, b1, hy_f_w2[0], b2, hy_f_w3[0], b3, dec)
        z_parts.append(_hyena(proj, hy_conv_w[0], coefs, hy_bias[0], f, ft, seq_len, n_seq, row_off, width, seqs))
    z_p, z_s = z_parts

    pad_r = ROUTER_ROWS - N_GROUPS - N_EXPERTS
    w_r = jnp.pad(jnp.concatenate([w_rc[0], w_rf[0]], axis=1).T, ((0, pad_r), (0, 0)))
    b_r = jnp.pad(jnp.concatenate([b_rc[0], b_rf[0]], axis=0), (0, pad_r)).reshape(ROUTER_ROWS, 1)
    x1, h2ext, bid = _outproj(xp, xs, y_ml_p, y_ml_s, z_p, z_s, mods3, gn, w_out[0].astype(BF16), w_r, b_r)

    pos3, meta = _route(bid)
    pos = pos3.reshape(T_ALL)
    y_sorted = _moe(meta, pos, h2ext, w_gate[0], w_up[0], w_down[0])
    y_p, y_s = _final(pos, y_sorted, x1, mods3, gn)

    new_c = c_new.reshape(BATCH, 1, 2, ML_HEADS, ML_HEAD_DIM, ML_HEAD_DIM)
    new_n = n_new.reshape(BATCH, 1, 2, ML_HEADS, ML_HEAD_DIM)
    new_m = m_new[:, :, 0].reshape(BATCH, 1, 2, ML_HEADS)
    return (y_p.reshape(BATCH, SEQ, D_MODEL), y_s.reshape(DEC_BATCH, DEC_SEQ, D_MODEL), new_c, new_n, new_m)
```

```python
import functools
import math

import jax
import jax.numpy as jnp
import numpy as np
from jax import lax
from jax.experimental import pallas as pl
from jax.experimental.pallas import tpu as pltpu

F32 = jnp.float32
BF16 = jnp.bfloat16

D_MODEL = 1024
BATCH = 16
SEQ = 256
DEC_BATCH = 4
DEC_SEQ = 1024
GRID_W = 64
ML_WIDTH = 512
ML_HEADS = 4
ML_HEAD_DIM = 128
HY_WIDTH = 512
HY_ORDER = 2
HY_EMB = 33
HY_BANDS = 16
HY_FILTER_HIDDEN = 64
HY_MOD_SHIFT = 0.05
N_GROUPS = 4
EXPERTS_PER_GROUP = 4
N_EXPERTS = 16
EXPERT_FF = 512
N_MOD = 6
EPS = 1e-6
ML_QKVO_COLS = 4 * ML_WIDTH
ML_GATE_COLS = 4 * ML_HEADS
HY_COLS = 3 * HY_WIDTH
MAIN_COLS = ML_QKVO_COLS + HY_COLS

T_PROMPT = BATCH * SEQ
T_SAMPLE = DEC_BATCH * DEC_SEQ
T_ALL = T_PROMPT + T_SAMPLE
TILE = 256
N_TILES_P = T_PROMPT // TILE
N_TILES = T_ALL // TILE
MOD_ROWS = 8
K_SCALE = ML_HEAD_DIM ** -0.5
VMEM_LIMIT = 56 * 1024 * 1024


def _cparams(sem):
    return pltpu.CompilerParams(dimension_semantics=sem, vmem_limit_bytes=VMEM_LIMIT)


def _split2(x):
    hi = x.astype(BF16)
    lo = (x - hi.astype(F32)).astype(BF16)
    return hi, lo


def _dot(a, b):
    return jnp.dot(a, b, preferred_element_type=F32)


def _dot_nt(a, b):
    return lax.dot_general(a, b, (((1,), (1,)), ((), ())), preferred_element_type=F32)


def _dot3(a, b):
    ah, al = _split2(a)
    bh, bl = _split2(b)
    return _dot(ah, bh) + _dot(al, bh) + _dot(ah, bl)


def _dot_exact_rhs(x, t):
    x1 = x.astype(BF16)
    r1 = x - x1.astype(F32)
    x2 = r1.astype(BF16)
    x3 = (r1 - x2.astype(F32)).astype(BF16)
    return _dot(x1, t) + _dot(x2, t) + _dot(x3, t)


def _rms(x, g):
    return x * lax.rsqrt(jnp.mean(x * x, axis=-1, keepdims=True) + EPS) * g


def _mod_row_of_tile(i, tiles_per_sample_seq, n_prompt_tiles):
    return jnp.where(i < n_prompt_tiles, 0, 1 + (i - n_prompt_tiles) // tiles_per_sample_seq)


def _ada_kernel(cv_ref, w_ref, b_ref, o_ref):
    cv = cv_ref[...]
    s = cv * jax.nn.sigmoid(cv)
    sh, sl = _split2(s)
    wh, wl = _split2(w_ref[...])
    both = _dot(jnp.concatenate([sh.astype(F32), sl.astype(F32)], axis=0).astype(BF16), wh)
    o_ref[...] = both[0:MOD_ROWS] + both[MOD_ROWS:] + _dot(sh, wl) + b_ref[...]


def _ada(cv, w_ada, b_ada):
    n = N_MOD * D_MODEL
    return pl.pallas_call(
        _ada_kernel,
        out_shape=jax.ShapeDtypeStruct((MOD_ROWS, n), F32),
        grid=(N_MOD,),
        in_specs=[pl.BlockSpec((MOD_ROWS, D_MODEL), lambda j: (0, 0)),
                  pl.BlockSpec((D_MODEL, D_MODEL), lambda j: (0, j)),
                  pl.BlockSpec((1, D_MODEL), lambda j: (0, j))],
        out_specs=pl.BlockSpec((MOD_ROWS, D_MODEL), lambda j: (0, j)),
        compiler_params=_cparams(("arbitrary",)),
        name="ada_mod",
    )(cv, w_ada, b_ada.reshape(1, n))


PREP_COLS = 512


def _prep_q_kernel(wt_ref, o_ref):
    o_ref[...] = wt_ref[...].T.astype(BF16)


def _prep_hy_kernel(wt_hbm, o_ref, buf, sem):
    start = pl.multiple_of(ML_QKVO_COLS + ML_GATE_COLS + pl.program_id(0) * PREP_COLS, 8)
    copy = pltpu.make_async_copy(wt_hbm.at[pl.ds(start, PREP_COLS), :], buf, sem)
    copy.start()
    copy.wait()
    o_ref[...] = buf[...].T.astype(BF16)


def _prep_in_weights(w_in_t):
    out_blk = pl.BlockSpec((D_MODEL, PREP_COLS), lambda j: (0, j))
    w_qkvo = pl.pallas_call(
        _prep_q_kernel,
        out_shape=jax.ShapeDtypeStruct((D_MODEL, ML_QKVO_COLS), BF16),
        grid=(ML_QKVO_COLS // PREP_COLS,),
        in_specs=[pl.BlockSpec((PREP_COLS, D_MODEL), lambda j: (j, 0))], out_specs=out_blk,
        compiler_params=_cparams(("arbitrary",)), name="prep_w_qkvo",
    )(w_in_t)
    w_hy = pl.pallas_call(
        _prep_hy_kernel,
        out_shape=jax.ShapeDtypeStruct((D_MODEL, HY_COLS), BF16),
        grid=(HY_COLS // PREP_COLS,),
        in_specs=[pl.BlockSpec(memory_space=pl.ANY)], out_specs=out_blk,
        scratch_shapes=[pltpu.VMEM((PREP_COLS, D_MODEL), F32), pltpu.SemaphoreType.DMA(())],
        compiler_params=_cparams(("arbitrary",)), name="prep_w_hy",
    )(w_in_t)
    return w_qkvo, w_hy


def _log_sigmoid(x):
    return jnp.minimum(x, 0.0) - jnp.log1p(jnp.exp(-jnp.abs(x)))


def _rows_to_cols(rows):
    ri = lax.broadcasted_iota(jnp.int32, (TILE, TILE), 0)
    ci = lax.broadcasted_iota(jnp.int32, (TILE, TILE), 1)
    eye = jnp.where(ri == ci, 1.0, 0.0).astype(BF16)
    p1 = rows.astype(BF16)
    r1 = rows - p1.astype(F32)
    p2 = r1.astype(BF16)
    p3 = (r1 - p2.astype(F32)).astype(BF16)
    return _dot_nt(eye, p1) + _dot_nt(eye, p2) + _dot_nt(eye, p3)


BIG_TILE = 4 * TILE
N_BIG_P = T_PROMPT // BIG_TILE
N_BIG = T_ALL // BIG_TILE


def _inproj_kernel(xp_ref, xs_ref, m_ref, gn_ref, wq_ref, wh_ref, wgt_ref, gbt_ref, proj_ref, gatet_ref):
    is_p = pl.program_id(0) < N_BIG_P
    halves = [slice(r * TILE, (r + 1) * TILE) for r in range(BIG_TILE // TILE)]
    hs = [_rms(jnp.where(is_p, xp_ref[rows, :], xs_ref[rows, :]), gn_ref[0:1, :]) * (1.0 + m_ref[0, 1:2, :])
          + m_ref[0, 0:1, :] for rows in halves]
    hbs = [h.astype(BF16) for h in hs]
    cb = 512
    for j in range(ML_QKVO_COLS // cb):
        for rows, hb in zip(halves, hbs):
            proj_ref[rows, j * cb:(j + 1) * cb] = _dot(hb, wq_ref[:, j * cb:(j + 1) * cb]).astype(BF16)
    for j in range(HY_COLS // cb):
        lo = ML_QKVO_COLS + j * cb
        for rows, hb in zip(halves, hbs):
            proj_ref[rows, lo:lo + cb] = _dot(hb, wh_ref[:, j * cb:(j + 1) * cb]).astype(BF16)
    wth, wtl = _split2(wgt_ref[...])
    gts = []
    for h, hb in zip(hs, hbs):
        hl = (h - hb.astype(F32)).astype(BF16)
        gt = _dot_nt(wth, hb) + _dot_nt(wth, hl) + _dot_nt(wtl, hb) + gbt_ref[...]
        row = lax.broadcasted_iota(jnp.int32, gt.shape, 0)
        gts.append(jnp.where((row % 8) >= 4, _log_sigmoid(gt), gt))
    for r, gt in enumerate(gts):
        gatet_ref[r] = gt


def _inproj(xp, xs, mods3, g_norm, w_qkvo, w_hy, wgt, gbt):
    tps = DEC_SEQ // BIG_TILE
    per = BIG_TILE // TILE
    return pl.pallas_call(
        _inproj_kernel,
        out_shape=(jax.ShapeDtypeStruct((T_ALL, MAIN_COLS), BF16),
                   jax.ShapeDtypeStruct((N_TILES, ML_GATE_COLS, TILE), F32)),
        grid=(N_BIG,),
        in_specs=[pl.BlockSpec((BIG_TILE, D_MODEL), lambda i: (jnp.minimum(i, N_BIG_P - 1), 0)),
                  pl.BlockSpec((BIG_TILE, D_MODEL), lambda i: (jnp.maximum(i - N_BIG_P, 0), 0)),
                  pl.BlockSpec((1, N_MOD, D_MODEL), lambda i: (_mod_row_of_tile(i, tps, N_BIG_P), 0, 0)),
                  pl.BlockSpec((4, D_MODEL), lambda i: (0, 0)),
                  pl.BlockSpec((D_MODEL, ML_QKVO_COLS), lambda i: (0, 0)),
                  pl.BlockSpec((D_MODEL, HY_COLS), lambda i: (0, 0)),
                  pl.BlockSpec((ML_GATE_COLS, D_MODEL), lambda i: (0, 0)),
                  pl.BlockSpec((ML_GATE_COLS, 1), lambda i: (0, 0))],
        out_specs=(pl.BlockSpec((BIG_TILE, MAIN_COLS), lambda i: (i, 0)),
                   pl.BlockSpec((per, ML_GATE_COLS, TILE), lambda i: (i, 0, 0))),
        compiler_params=_cparams(("arbitrary",)),
        name="in_proj",
    )(xp, xs, mods3, g_norm, w_qkvo, w_hy, wgt, gbt)


ST_ROWS = ML_HEAD_DIM + 16


def _mlstm_kernel(*refs, seq_len, has_state):
    if has_state:
        (q_ref, k_ref, v_ref, o_ref, gt_ref, gain_ref, c0_ref, n0_ref, m0_ref,
         y_ref, c_ref, n_ref, m_ref, vt_ref, hf_ref, hb_ref, st_ref, ms_ref) = refs
    else:
        (q_ref, k_ref, v_ref, o_ref, gt_ref, gain_ref,
         y_ref, c_ref, n_ref, m_ref, vt_ref, hf_ref, hb_ref, st_ref, ms_ref) = refs
    ch = TILE
    nc = seq_len // ch
    hd = ML_HEAD_DIM
    key = lax.broadcasted_iota(jnp.int32, (ch, ch), 0)
    qry = lax.broadcasted_iota(jnp.int32, (ch, ch), 1)
    key_le = key <= qry
    key_ge = key >= qry
    t_le = jnp.where(key_le, 1.0, 0.0).astype(BF16)
    t_ge = jnp.where(key_ge, 1.0, 0.0).astype(BF16)
    sub16 = lax.broadcasted_iota(jnp.int32, (16, ch), 0)
    ln_scale = math.log(K_SCALE)

    for c in range(nc):
        for h in range(ML_HEADS):
            cols = slice(h * hd, (h + 1) * hd)
            vt_ref[c, cols, :] = v_ref[c * ch:(c + 1) * ch, cols].T

    for d in range(2):
        for h in range(ML_HEADS):
            r = d * ML_HEADS + h
            st_ref[r] = jnp.zeros((ST_ROWS, hd), F32)
            if has_state:
                st_ref[r, 0:hd, :] = c0_ref[0, d, h].T
                st_ref[r, hd:hd + 1, :] = n0_ref[0, d, h:h + 1, :]
                ms_ref[r] = jnp.broadcast_to(m0_ref[0, r:r + 1, :], (1, ch))
            else:
                ms_ref[r] = jnp.zeros((1, ch), F32)

    def step(t, carry):
        for d in range(2):
            c = t if d == 0 else nc - 1 - t
            rows = pl.ds(pl.multiple_of(c * ch, ch), ch)
            grow = gt_ref[c]
            brow_all = _dot_exact_rhs(grow, t_le if d == 0 else t_ge)
            ccol_all = _rows_to_cols(grow - pltpu.roll(brow_all, ML_GATE_COLS - ML_HEADS, axis=0))
            mask = key_le if d == 0 else key_ge
            hacc_ref = hf_ref if d == 0 else hb_ref
            heads = range(ML_HEADS)
            regs = [d * ML_HEADS + h for h in heads]
            colss = [slice(h * hd, (h + 1) * hd) for h in heads]
            qs = [q_ref[rows, cols] for cols in colss]
            ks = [k_ref[rows, cols] for cols in colss]
            vts = [vt_ref[c, cols, :] for cols in colss]
            sts = [st_ref[r] for r in regs]
            m_prevs = [ms_ref[r] for r in regs]
            b_rows = [brow_all[(1 + 2 * d) * ML_HEADS + h:(1 + 2 * d) * ML_HEADS + h + 1, :] for h in heads]
            ig_rows = [grow[2 * d * ML_HEADS + h:2 * d * ML_HEADS + h + 1, :] for h in heads]
            qks = [_dot_nt(k, q) for k, q in zip(ks, qs)]
            iqs = [_dot_nt(st.astype(BF16), q) for st, q in zip(sts, qs)]
            ss, sc_inters, m_poss = [], [], []
            for h in heads:
                icol = 2 * d * ML_HEADS + h
                c_col = ccol_all[:, icol:icol + 1]
                logd = jnp.where(mask, b_rows[h] + c_col, -jnp.inf)
                inter = b_rows[h] + m_prevs[h]
                m_pos = jnp.maximum(inter, jnp.max(logd, axis=0, keepdims=True))
                ss.append(qks[h] * jnp.exp(logd - (m_pos - ln_scale)))
                sc_inters.append(jnp.exp(inter - m_pos))
                m_poss.append(m_pos)
            pvs = [_dot(vt, s.astype(BF16)) for vt, s in zip(vts, ss)]
            for h in heads:
                num = sc_inters[h] * iqs[h][0:hd] + pvs[h]
                den = sc_inters[h] * iqs[h][hd:hd + 1] + jnp.sum(ss[h], axis=0, keepdims=True)
                hacc_ref[c, colss[h], :] = num * (1.0 / jnp.maximum(jnp.abs(den), jnp.exp(-m_poss[h])))
            lhss, decays = [], []
            for h in heads:
                b_row = b_rows[h]
                b_last = b_row[:, ch - 1:ch] if d == 0 else b_row[:, 0:1]
                logw = b_last - b_row + ig_rows[h]
                m_new = jnp.maximum(b_last + m_prevs[h], jnp.max(logw, axis=1, keepdims=True))
                w = jnp.exp(logw - (m_new - ln_scale))
                decays.append(jnp.exp(b_last + m_prevs[h] - m_new))
                lhss.append(jnp.concatenate([(vts[h].astype(F32) * w).astype(BF16),
                                             jnp.where(sub16 == 0, w, 0.0).astype(BF16)], axis=0))
                ms_ref[regs[h]] = m_new
            upds = [_dot(lhs, k) for lhs, k in zip(lhss, ks)]
            for h in heads:
                st_ref[regs[h]] = decays[h][:, 0:hd] * sts[h] + upds[h]
        return carry

    lax.fori_loop(0, nc, step, 0)

    for d in range(2):
        for h in range(ML_HEADS):
            r = d * ML_HEADS + h
            c_ref[0, d, h] = st_ref[r, 0:hd, :].T
            n_ref[0, d, h:h + 1, :] = st_ref[r, hd:hd + 1, :]
            m_ref[0, r:r + 1, :] = ms_ref[r][:, 0:hd]
    for c in range(nc):
        for h in range(ML_HEADS):
            cols = slice(h * hd, (h + 1) * hd)
            ht = hf_ref[c, cols, :] + hb_ref[c, cols, :]
            ht = ht * lax.rsqrt(jnp.mean(ht * ht, axis=0, keepdims=True) + EPS)
            rows = slice(c * ch, (c + 1) * ch)
            y = ht.T * gain_ref[:, cols] * jax.nn.sigmoid(o_ref[rows, cols].astype(F32))
            y_ref[rows, cols] = y.astype(BF16)


def _mlstm(proj, gates_t, gain, state, seq_len, n_seq, row_block_off):
    has_state = state is not None
    tiles = seq_len // TILE
    off = row_block_off
    qkvo_specs = [pl.BlockSpec((seq_len, ML_WIDTH), functools.partial(lambda b, j: (off + b, j), j=j))
                  for j in range(4)]
    in_specs = qkvo_specs + [
        pl.BlockSpec((tiles, ML_GATE_COLS, TILE), lambda b: (off + b, 0, 0)),
        pl.BlockSpec((1, ML_WIDTH), lambda b: (0, 0)),
    ]
    args = [proj, proj, proj, proj, gates_t, gain]
    if has_state:
        c0, n0, m0 = state
        in_specs += [
            pl.BlockSpec((1, 2, ML_HEADS, ML_HEAD_DIM, ML_HEAD_DIM), lambda b: (b, 0, 0, 0, 0)),
            pl.BlockSpec((1, 2, ML_HEADS, ML_HEAD_DIM), lambda b: (b, 0, 0, 0)),
            pl.BlockSpec((1, 2 * ML_HEADS, 1), lambda b: (b, 0, 0)),
        ]
        args += [c0, n0, m0]
    out_shape = (jax.ShapeDtypeStruct((n_seq * seq_len, ML_WIDTH), BF16),
                 jax.ShapeDtypeStruct((n_seq, 2, ML_HEADS, ML_HEAD_DIM, ML_HEAD_DIM), F32),
                 jax.ShapeDtypeStruct((n_seq, 2, ML_HEADS, ML_HEAD_DIM), F32),
                 jax.ShapeDtypeStruct((n_seq, 2 * ML_HEADS, ML_HEAD_DIM), F32))
    out_specs = (pl.BlockSpec((seq_len, ML_WIDTH), lambda b: (b, 0)),
                 pl.BlockSpec((1, 2, ML_HEADS, ML_HEAD_DIM, ML_HEAD_DIM), lambda b: (b, 0, 0, 0, 0)),
                 pl.BlockSpec((1, 2, ML_HEADS, ML_HEAD_DIM), lambda b: (b, 0, 0, 0)),
                 pl.BlockSpec((1, 2 * ML_HEADS, ML_HEAD_DIM), lambda b: (b, 0, 0)))
    scratch = [pltpu.VMEM((tiles, ML_WIDTH, TILE), BF16),
               pltpu.VMEM((tiles, ML_WIDTH, TILE), F32), pltpu.VMEM((tiles, ML_WIDTH, TILE), F32),
               pltpu.VMEM((2 * ML_HEADS, ST_ROWS, ML_HEAD_DIM), F32),
               pltpu.VMEM((2 * ML_HEADS, 1, TILE), F32)]
    return pl.pallas_call(
        functools.partial(_mlstm_kernel, seq_len=seq_len, has_state=has_state),
        out_shape=out_shape, grid=(n_seq,), in_specs=in_specs, out_specs=out_specs,
        scratch_shapes=scratch, compiler_params=_cparams(("arbitrary",)),
        name=f"mlstm_{seq_len}",
    )(*args)


def _dft_mats(seq_len):
    k = np.arange(seq_len, dtype=np.int64)[:, None]
    d = np.arange(seq_len, dtype=np.int64)[None, :]
    ang = np.pi * ((k * d) % (2 * seq_len)).astype(np.float64) / seq_len
    sinm = np.sin(ang)
    sinm[0, :] = np.where(d[0] % 2 == 0, 1.0, -1.0)
    f = np.concatenate([np.cos(ang), sinm], axis=0).astype(np.float32)
    return jnp.asarray(f).astype(BF16), jnp.asarray(np.ascontiguousarray(f.T)).astype(BF16)


def _filter_feats(seq_len):
    t = np.linspace(0.0, 1.0, seq_len, dtype=np.float64)[:, None]
    wpos = 2.0 * np.pi * np.arange(seq_len, dtype=np.float64)[:, None] / seq_len
    bands = np.linspace(1e-4, HY_BANDS - 1, HY_BANDS, dtype=np.float64)[None, :]
    z = np.concatenate([t, np.cos(bands * wpos), -np.sin(bands * wpos)], axis=-1)
    return jnp.asarray(np.pad(z, ((0, 0), (0, 128 - HY_EMB))).astype(np.float32))


def _filter_kernel(z_ref, w1_ref, b1_ref, w2_ref, b2_ref, w3_ref, b3_ref, dec_ref, f_ref,
                   a_ref, b_ref, d_ref, *, seq_len):
    n = 2 * seq_len
    oc = 2 * HY_WIDTH
    z = z_ref[...]
    h = jnp.sin(_dot3(z, w1_ref[...]) + b1_ref[...])
    h = jnp.sin(_dot3(h, w2_ref[...]) + b2_ref[...])
    t = z[:, 0:1]
    di = lax.broadcasted_iota(jnp.int32, (seq_len, 1), 0)
    sgn = jnp.where(di % 2 == 0, 1.0, -1.0)
    first = di == 0
    ssums, sdifs = [], []
    for o in range(HY_ORDER):
        cols = slice(o * oc, (o + 1) * oc)
        g = _dot3(h, w3_ref[:, cols]) + b3_ref[:, cols]
        g = g * (jnp.exp(-t * jnp.abs(dec_ref[:, cols])) + HY_MOD_SHIFT)
        ss = jnp.sum(g * g, axis=0, keepdims=True)
        inv = lax.rsqrt(ss[:, :HY_WIDTH] + ss[:, HY_WIDTH:] + EPS)
        hp = g[:, :HY_WIDTH] * inv
        hn = g[:, HY_WIDTH:] * inv
        ssums.append(hp + hn)
        sdifs.append(hp - hn)
    hcs = [_dot(f_ref[0:seq_len, :], s.astype(BF16)) for s in ssums]
    hss = [_dot(f_ref[seq_len:n, :], s.astype(BF16)) for s in sdifs]
    for o in range(HY_ORDER):
        nyq = jnp.sum(ssums[o] * sgn, axis=0, keepdims=True)
        a_ref[o] = hcs[o] * jnp.where(first, 1.0 / n, 2.0 / n)
        b_ref[o] = jnp.where(first, 0.0, hss[o] * (2.0 / n))
        d_ref[o] = jnp.where(first, nyq * (1.0 / n), hcs[o] * (2.0 / n))


def _hyena_filters(seq_len, f, w1p, b1, w2, b2, w3, b3, dec):
    z = _filter_feats(seq_len)
    out = jax.ShapeDtypeStruct((HY_ORDER, seq_len, HY_WIDTH), F32)
    return pl.pallas_call(
        functools.partial(_filter_kernel, seq_len=seq_len),
        out_shape=(out, out, out),
        compiler_params=pltpu.CompilerParams(vmem_limit_bytes=VMEM_LIMIT),
        name=f"hyena_filter_{seq_len}",
    )(z, w1p, b1, w2, b2, w3, b3, dec, f)


def _hyena_kernel(x1_ref, x2_ref, v_ref, cw1_ref, cw2_ref, cwv_ref, a_ref, b_ref, d_ref, bias_ref,
                  f_ref, ft_ref, z_ref, *, seq_len, width, seqs):
    rows = seqs * seq_len
    ti = lax.broadcasted_iota(jnp.int32, (rows, 1), 0)
    has_prev = (ti % width) != 0
    has_next = (ti % width) != (width - 1)

    def short_conv(x_ref, w_ref):
        x = x_ref[...].astype(F32)
        prev = jnp.where(has_prev, pltpu.roll(x, 1, axis=0), 0.0)
        nxt = jnp.where(has_next, pltpu.roll(x, rows - 1, axis=0), 0.0)
        return w_ref[0:1, :] * prev + w_ref[1:2, :] * x + w_ref[2:3, :] * nxt

    gates = (short_conv(x1_ref, cw1_ref), short_conv(x2_ref, cw2_ref))
    v = short_conv(v_ref, cwv_ref)
    sls = [slice(i * seq_len, (i + 1) * seq_len) for i in range(seqs)]
    zs = [v[sl] for sl in sls]
    for o in range(HY_ORDER):
        a, b, dd = a_ref[o], b_ref[o], d_ref[o]
        us = [_dot(f_ref[...], z.astype(BF16)) for z in zs]
        ys = []
        for u in us:
            ut = u[:seq_len]
            ub = u[seq_len:]
            ys.append(((ut * a - ub * b).astype(BF16), (ut * b + ub * dd).astype(BF16)))
        convs = [_dot(ft_ref[:, :seq_len], yt) + _dot(ft_ref[:, seq_len:], yb) for yt, yb in ys]
        zs = [gates[o][sl] * (y + bias_ref[o:o + 1, :] * z) for sl, y, z in zip(sls, convs, zs)]
    for sl, z in zip(sls, zs):
        z_ref[sl, :] = z.astype(BF16)


def _hyena(proj, conv_w, coefs, hy_bias, f, ft, seq_len, n_seq, row_off, width, seqs):
    cb = 256
    nblk = HY_WIDTH // cb
    base = ML_QKVO_COLS // cb
    rows = seqs * seq_len
    off = row_off // rows
    a, b, d = coefs

    def col_spec(part):
        return pl.BlockSpec((rows, cb), lambda j, s: (off + s, base + part * nblk + j))

    def w_spec(part):
        return pl.BlockSpec((3, cb), lambda j, s: (0, part * nblk + j))

    coef_spec = pl.BlockSpec((HY_ORDER, seq_len, cb), lambda j, s: (0, 0, j))
    return pl.pallas_call(
        functools.partial(_hyena_kernel, seq_len=seq_len, width=width, seqs=seqs),
        out_shape=jax.ShapeDtypeStruct((n_seq * seq_len, HY_WIDTH), BF16),
        grid=(nblk, n_seq // seqs),
        in_specs=[col_spec(0), col_spec(1), col_spec(2), w_spec(0), w_spec(1), w_spec(2),
                  coef_spec, coef_spec, coef_spec,
                  pl.BlockSpec((HY_ORDER, cb), lambda j, s: (0, j)),
                  pl.BlockSpec((2 * seq_len, seq_len), lambda j, s: (0, 0)),
                  pl.BlockSpec((seq_len, 2 * seq_len), lambda j, s: (0, 0))],
        out_specs=pl.BlockSpec((rows, cb), lambda j, s: (s, j)),
        compiler_params=_cparams(("arbitrary", "arbitrary")),
        name=f"hyena_conv_{seq_len}",
    )(proj, proj, proj, conv_w, conv_w, conv_w, a, b, d, hy_bias, f, ft)


def _first_max(x, n):
    mx = jnp.max(x, axis=0, keepdims=True)
    row = lax.broadcasted_iota(jnp.int32, x.shape, 0).astype(F32)
    idx = jnp.min(jnp.where(x == mx, row, float(n)), axis=0, keepdims=True)
    return mx, idx.astype(jnp.int32)


ROUTER_ROWS = 32
PAIRS_PER_GROUP = 6
N_BUCKETS = N_GROUPS * PAIRS_PER_GROUP
PAIR_SLOTS = ((0, 1), (0, 2), (0, 3), (1, 3), (1, 2), (3, 2))
LANES = 128
H2_EXT = D_MODEL + LANES
ROW_TILE = 256
ROW_CAP = T_ALL + N_BUCKETS * ROW_TILE
N_ROW_TILES = ROW_CAP // ROW_TILE


def _outproj_kernel(xp_ref, xs_ref, yp_ref, ys_ref, zp_ref, zs_ref, m_ref, gn_ref, wo_ref, wr_ref, br_ref,
                    x1_ref, h2_ref, bid_ref):
    is_p = pl.program_id(0) < N_BIG_P
    wrh, wrl = _split2(wr_ref[...])
    halves = [slice(r * TILE, (r + 1) * TILE) for r in range(BIG_TILE // TILE)]
    ys = [_dot(jnp.where(is_p, yp_ref[rows, :], ys_ref[rows, :]), wo_ref[0:ML_WIDTH, :])
          + _dot(jnp.where(is_p, zp_ref[rows, :], zs_ref[rows, :]), wo_ref[ML_WIDTH:, :]) for rows in halves]
    h2s = []
    for rows, y in zip(halves, ys):
        x = jnp.where(is_p, xp_ref[rows, :], xs_ref[rows, :])
        x1 = x + m_ref[0, 2:3, :] * _rms(y, gn_ref[1:2, :])
        x1_ref[rows, :] = x1
        h2 = _rms(x1, gn_ref[2:3, :]) * (1.0 + m_ref[0, 4:5, :]) + m_ref[0, 3:4, :]
        h2_ref[rows, 0:D_MODEL] = h2
        h2s.append(h2)
    logits = []
    for h2 in h2s:
        h2h, h2l = _split2(h2)
        logits.append(_dot_nt(wrh, h2h) + _dot_nt(wrh, h2l) + _dot_nt(wrl, h2h) + br_ref[...])
    routed = [_route_tile(lg) for lg in logits]
    for r, (rows, (gate_rows, bucket)) in enumerate(zip(halves, routed)):
        h2_ref[rows, D_MODEL:H2_EXT] = jnp.zeros((TILE, LANES), F32)
        h2_ref[rows, D_MODEL:D_MODEL + 8] = _rows_to_cols(gate_rows)
        bid_ref[r] = bucket


def _route_tile(logits):
    lc = logits[0:N_GROUPS]
    mx, gi = _first_max(lc, N_GROUPS)
    p_grp = 1.0 / jnp.sum(jnp.exp(lc - mx), axis=0, keepdims=True)
    lsel = jnp.zeros((EXPERTS_PER_GROUP, TILE), F32)
    for g in range(N_GROUPS):
        lo = N_GROUPS + g * EXPERTS_PER_GROUP
        lsel = jnp.where(gi == g, logits[lo:lo + EXPERTS_PER_GROUP], lsel)
    l1, i1 = _first_max(lsel, EXPERTS_PER_GROUP)
    sub4 = lax.broadcasted_iota(jnp.int32, lsel.shape, 0)
    l2, i2 = _first_max(jnp.where(sub4 == i1, -jnp.inf, lsel), EXPERTS_PER_GROUP)
    e2 = jnp.exp(l2 - l1)
    w1 = p_grp / (1.0 + e2)
    w2 = p_grp * e2 / (1.0 + e2)
    lo_e = jnp.minimum(i1, i2)
    hi_e = jnp.maximum(i1, i2)
    pair = jnp.where(lo_e == 0, hi_e - 1, jnp.where(lo_e == 1, jnp.where(hi_e == 3, 3, 4), 5))
    slot_a = jnp.where(pair == 5, hi_e, lo_e)
    first_in_a = i1 == slot_a
    w_a = jnp.where(first_in_a, w1, w2)
    w_b = jnp.where(first_in_a, w2, w1)
    sub = lax.broadcasted_iota(jnp.int32, (8, TILE), 0)
    gate_rows = jnp.where(sub == 0, w_a, jnp.where(sub == 1, w_b, 0.0))
    return gate_rows, gi * PAIRS_PER_GROUP + pair


def _outproj(xp, xs, yp, ys, zp, zs, mods3, g_norm, w_out, w_r, b_r):
    tps = DEC_SEQ // BIG_TILE
    per = BIG_TILE // TILE
    pidx = lambda i: (jnp.minimum(i, N_BIG_P - 1), 0)
    sidx = lambda i: (jnp.maximum(i - N_BIG_P, 0), 0)
    return pl.pallas_call(
        _outproj_kernel,
        out_shape=(jax.ShapeDtypeStruct((T_ALL, D_MODEL), F32),
                   jax.ShapeDtypeStruct((T_ALL, H2_EXT), F32),
                   jax.ShapeDtypeStruct((N_TILES, 1, TILE), jnp.int32)),
        grid=(N_BIG,),
        in_specs=[pl.BlockSpec((BIG_TILE, D_MODEL), pidx), pl.BlockSpec((BIG_TILE, D_MODEL), sidx),
                  pl.BlockSpec((BIG_TILE, ML_WIDTH), pidx), pl.BlockSpec((BIG_TILE, ML_WIDTH), sidx),
                  pl.BlockSpec((BIG_TILE, HY_WIDTH), pidx), pl.BlockSpec((BIG_TILE, HY_WIDTH), sidx),
                  pl.BlockSpec((1, N_MOD, D_MODEL), lambda i: (_mod_row_of_tile(i, tps, N_BIG_P), 0, 0)),
                  pl.BlockSpec((4, D_MODEL), lambda i: (0, 0)),
                  pl.BlockSpec((D_MODEL, D_MODEL), lambda i: (0, 0)),
                  pl.BlockSpec((ROUTER_ROWS, D_MODEL), lambda i: (0, 0)),
                  pl.BlockSpec((ROUTER_ROWS, 1), lambda i: (0, 0))],
        out_specs=(pl.BlockSpec((BIG_TILE, D_MODEL), lambda i: (i, 0)),
                   pl.BlockSpec((BIG_TILE, H2_EXT), lambda i: (i, 0)),
                   pl.BlockSpec((per, 1, TILE), lambda i: (i, 0, 0))),
        compiler_params=_cparams(("arbitrary",)),
        name="out_proj_router",
    )(xp, xs, yp, ys, zp, zs, mods3, g_norm, w_out, w_r, b_r)


def _route_kernel(bid_ref, pos_ref, meta_ref):
    nb = 32
    tm = float(ROW_TILE)
    sub = lax.broadcasted_iota(jnp.int32, (nb, TILE), 0)
    ri = lax.broadcasted_iota(jnp.int32, (TILE, TILE), 0)
    ci = lax.broadcasted_iota(jnp.int32, (TILE, TILE), 1)
    before = jnp.where(ri < ci, 1.0, 0.0).astype(BF16)

    def onehot(blk):
        return jnp.where(sub == bid_ref[blk], 1.0, 0.0)

    zeros = jnp.zeros((nb, 1), F32)
    cnt = lax.fori_loop(0, N_TILES, lambda blk, c: c + jnp.sum(onehot(blk), axis=1, keepdims=True), zeros)
    padded = jnp.floor((cnt + (tm - 1.0)) * (1.0 / tm)) * tm
    r32 = lax.broadcasted_iota(jnp.int32, (nb, nb), 0)
    c32 = lax.broadcasted_iota(jnp.int32, (nb, nb), 1)
    padded_row = jnp.sum(jnp.where(r32 == c32, padded, 0.0), axis=0, keepdims=True)
    offs = jnp.sum(jnp.where(c32 < r32, padded_row, 0.0), axis=1, keepdims=True)
    ends = offs + padded

    def place(blk, seen):
        oh = onehot(blk)
        rank = _dot(oh.astype(BF16), before)
        pos = jnp.sum(oh * (rank + seen + offs), axis=0, keepdims=True)
        pos_ref[blk] = pos.astype(jnp.int32)
        return seen + jnp.sum(oh, axis=1, keepdims=True)

    lax.fori_loop(0, N_TILES, place, zeros)

    start = lax.broadcasted_iota(jnp.int32, (nb, 128), 1).astype(F32) * tm
    bsub = lax.broadcasted_iota(jnp.int32, (nb, 128), 0)
    done = jnp.where((bsub < N_BUCKETS) & (ends <= start), 1.0, 0.0)
    tb = jnp.sum(done, axis=0, keepdims=True)
    valid = jnp.where(tb < N_BUCKETS, 1.0, 0.0)
    tbc = jnp.minimum(tb, N_BUCKETS - 1.0)
    grp = jnp.floor((tbc + 0.5) * (1.0 / PAIRS_PER_GROUP))
    pair = tbc - PAIRS_PER_GROUP * grp
    loc_a = jnp.zeros_like(pair)
    loc_b = jnp.zeros_like(pair)
    for k, (sa, sb) in enumerate(PAIR_SLOTS):
        loc_a = jnp.where(pair == k, float(sa), loc_a)
        loc_b = jnp.where(pair == k, float(sb), loc_b)
    mine = bsub.astype(F32) == tbc
    used = jnp.sum(jnp.where(mine, offs + cnt, 0.0), axis=0, keepdims=True)
    n_rows = jnp.clip(used - start[0:1], 0.0, tm) * valid
    row8 = lax.broadcasted_iota(jnp.int32, (8, 128), 0)
    meta = jnp.where(row8 == 0, grp * EXPERTS_PER_GROUP + loc_a,
                     jnp.where(row8 == 1, grp * EXPERTS_PER_GROUP + loc_b,
                               jnp.where(row8 == 2, valid, jnp.where(row8 == 3, n_rows, 0.0))))
    meta_ref[...] = meta.astype(jnp.int32)


def _route(bid):
    return pl.pallas_call(
        _route_kernel,
        out_shape=(jax.ShapeDtypeStruct((N_TILES, 1, TILE), jnp.int32),
                   jax.ShapeDtypeStruct((8, 128), jnp.int32)),
        compiler_params=pltpu.CompilerParams(vmem_limit_bytes=VMEM_LIMIT),
        name="moe_route",
    )(bid)


def _moe_kernel(meta_ref, pos_ref, h2_hbm, wga_ref, wua_ref, wda_ref, wgb_ref, wub_ref, wdb_ref,
                y_ref, src_ref, xbuf, sem, wga_s, wua_s, wda_s, wgb_s, wub_s, wdb_s):
    j = pl.program_id(0)

    def row_copy(tile, r, slot):
        tok = src_ref[tile * ROW_TILE + r]
        return pltpu.make_async_copy(h2_hbm.at[pl.ds(tok, 1), :], xbuf.at[slot, pl.ds(r, 1), :], sem.at[slot])

    def issue_rows(tile, slot, lo, hi):
        for r in range(lo, hi):
            row_copy(tile, r, slot).start()

    group = 8

    def row_groups(tile):
        return (meta_ref[3, tile] + (group - 1)) // group

    def issue_counted(tile, slot):
        def body(g, c):
            for k in range(group):
                row_copy(tile, g * group + k, slot).start()
            return c
        lax.fori_loop(0, row_groups(tile), body, 0)

    def wait_counted(tile, slot):
        def body(g, c):
            for k in range(group):
                row_copy(tile, g * group + k, slot).wait()
            return c
        lax.fori_loop(0, row_groups(tile), body, 0)

    def wait_full(slot):
        pltpu.make_async_copy(h2_hbm.at[pl.ds(0, ROW_TILE), :], xbuf.at[slot], sem.at[slot]).wait()

    @pl.when(j == 0)
    def _():
        xbuf[...] = jnp.zeros_like(xbuf)

        def clear(t, c):
            n = meta_ref[3, t]
            for k in range(group - 1):
                src_ref[t * ROW_TILE + jnp.minimum(n + k, ROW_TILE - 1)] = 0
            return c
        lax.fori_loop(0, N_ROW_TILES, clear, 0)

        def invert(t, c):
            src_ref[pos_ref[t]] = t
            return c
        lax.fori_loop(0, T_ALL, invert, 0, unroll=8)

        @pl.when(meta_ref[2, 0] == 1)
        def _():
            issue_counted(0, 0)

    nxt = jnp.minimum(j + 1, N_ROW_TILES - 1)
    has_next = (j + 1 < N_ROW_TILES) & (meta_ref[2, nxt] == 1)
    next_full = has_next & (meta_ref[3, nxt] == ROW_TILE)
    valid = meta_ref[2, j] == 1
    full = meta_ref[3, j] == ROW_TILE
    prev = jnp.maximum(j - 1, 0)

    @pl.when(valid & has_next & jnp.logical_not(next_full))
    def _():
        issue_counted(nxt, nxt % 2)

    @pl.when(valid & full)
    def _():
        wait_full(j % 2)

    @pl.when(valid & jnp.logical_not(full))
    def _():
        wait_counted(j, j % 2)

    @pl.when(valid & ((j == 0) | (meta_ref[0, j] != meta_ref[0, prev])))
    def _():
        wga_s[...] = wga_ref[0].astype(BF16)
        wua_s[...] = wua_ref[0].astype(BF16)
        wda_s[...] = wda_ref[0].astype(BF16)

    @pl.when(valid & ((j == 0) | (meta_ref[1, j] != meta_ref[1, prev])))
    def _():
        wgb_s[...] = wgb_ref[0].astype(BF16)
        wub_s[...] = wub_ref[0].astype(BF16)
        wdb_s[...] = wdb_ref[0].astype(BF16)

    def compute(fetch_next):
        slot = j % 2
        nslot = nxt % 2
        step = ROW_TILE // 8
        batches = iter(range(0, ROW_TILE, step))

        def fetch():
            if fetch_next:
                lo = next(batches)
                issue_rows(nxt, nslot, lo, lo + step)

        x = xbuf[slot, :, 0:D_MODEL].astype(BF16)
        gates = xbuf[slot, :, D_MODEL:H2_EXT]
        hg_a = _dot(x, wga_s[...])
        fetch()
        hu_a = _dot(x, wua_s[...])
        fetch()
        hg_b = _dot(x, wgb_s[...])
        fetch()
        hu_b = _dot(x, wub_s[...])
        fetch()
        act_a = (hg_a * jax.nn.sigmoid(hg_a) * hu_a * gates[:, 0:1]).astype(BF16)
        fetch()
        act_b = (hg_b * jax.nn.sigmoid(hg_b) * hu_b * gates[:, 1:2]).astype(BF16)
        fetch()
        y = _dot(act_a, wda_s[...])
        fetch()
        y = y + _dot(act_b, wdb_s[...])
        fetch()
        y_ref[...] = y

    @pl.when(valid & next_full)
    def _():
        compute(True)

    @pl.when(valid & jnp.logical_not(next_full))
    def _():
        compute(False)

    @pl.when(jnp.logical_not(valid))
    def _():
        y_ref[...] = jnp.zeros_like(y_ref)


def _moe(meta, pos, h2ext, w_gate, w_up, w_down):
    up_spec = lambda slot: pl.BlockSpec((1, D_MODEL, EXPERT_FF), lambda j, meta, pos: (meta[slot, j], 0, 0))
    down_spec = lambda slot: pl.BlockSpec((1, EXPERT_FF, D_MODEL), lambda j, meta, pos: (meta[slot, j], 0, 0))
    grid_spec = pltpu.PrefetchScalarGridSpec(
        num_scalar_prefetch=2,
        grid=(N_ROW_TILES,),
        in_specs=[pl.BlockSpec(memory_space=pl.ANY),
                  up_spec(0), up_spec(0), down_spec(0), up_spec(1), up_spec(1), down_spec(1)],
        out_specs=pl.BlockSpec((ROW_TILE, D_MODEL), lambda j, meta, pos: (j, 0)),
        scratch_shapes=[pltpu.SMEM((ROW_CAP,), jnp.int32),
                        pltpu.VMEM((2, ROW_TILE, H2_EXT), F32),
                        pltpu.SemaphoreType.DMA((2,)),
                        pltpu.VMEM((D_MODEL, EXPERT_FF), BF16), pltpu.VMEM((D_MODEL, EXPERT_FF), BF16),
                        pltpu.VMEM((EXPERT_FF, D_MODEL), BF16),
                        pltpu.VMEM((D_MODEL, EXPERT_FF), BF16), pltpu.VMEM((D_MODEL, EXPERT_FF), BF16),
                        pltpu.VMEM((EXPERT_FF, D_MODEL), BF16)])
    return pl.pallas_call(
        _moe_kernel,
        out_shape=jax.ShapeDtypeStruct((ROW_CAP, D_MODEL), F32),
        grid_spec=grid_spec,
        compiler_params=_cparams(("arbitrary",)),
        name="moe_experts",
    )(meta, pos, h2ext, w_gate, w_up, w_down, w_gate, w_up, w_down)


def _final_kernel(pos_ref, y_hbm, x1_ref, m_ref, gn_ref, op_ref, os_ref, ybuf, sem):
    i = pl.program_id(0)

    def row_copy(tile, r, slot):
        p = pos_ref[tile * TILE + r]
        return pltpu.make_async_copy(y_hbm.at[pl.ds(p, 1), :], ybuf.at[slot, pl.ds(r, 1), :], sem.at[slot])

    def issue(tile, slot):
        def body(r2, c):
            row_copy(tile, 2 * r2, slot).start(priority=0)
            row_copy(tile, 2 * r2 + 1, slot).start(priority=1)
            return c
        lax.fori_loop(0, TILE // 2, body, 0, unroll=4)

    def wait(slot):
        pltpu.make_async_copy(y_hbm.at[pl.ds(0, TILE)], ybuf.at[slot], sem.at[slot]).wait()

    @pl.when(i == 0)
    def _():
        issue(0, 0)

    @pl.when(i + 1 < N_TILES)
    def _():
        issue(i + 1, (i + 1) % 2)

    slot = i % 2
    wait(slot)
    out = x1_ref[...] + m_ref[0, 5:6, :] * _rms(ybuf[slot], gn_ref[3:4, :])

    @pl.when(i < N_TILES_P)
    def _():
        op_ref[...] = out

    @pl.when(i >= N_TILES_P)
    def _():
        os_ref[...] = out


def _final(pos, y_sorted, x1, mods3, g_norm):
    tps = DEC_SEQ // TILE
    grid_spec = pltpu.PrefetchScalarGridSpec(
        num_scalar_prefetch=1,
        grid=(N_TILES,),
        in_specs=[pl.BlockSpec(memory_space=pl.ANY),
                  pl.BlockSpec((TILE, D_MODEL), lambda i, pos: (i, 0)),
                  pl.BlockSpec((1, N_MOD, D_MODEL), lambda i, pos: (_mod_row_of_tile(i, tps, N_TILES_P), 0, 0)),
                  pl.BlockSpec((4, D_MODEL), lambda i, pos: (0, 0))],
        out_specs=(pl.BlockSpec((TILE, D_MODEL), lambda i, pos: (jnp.minimum(i, N_TILES_P - 1), 0)),
                   pl.BlockSpec((TILE, D_MODEL), lambda i, pos: (jnp.maximum(i - N_TILES_P, 0), 0))),
        scratch_shapes=[pltpu.VMEM((2, TILE, D_MODEL), F32), pltpu.SemaphoreType.DMA((2,))])
    return pl.pallas_call(
        _final_kernel,
        out_shape=(jax.ShapeDtypeStruct((T_PROMPT, D_MODEL), F32),
                   jax.ShapeDtypeStruct((T_SAMPLE, D_MODEL), F32)),
        grid_spec=grid_spec,
        compiler_params=_cparams(("arbitrary",)),
        name="moe_combine_final",
    )(pos, y_sorted, x1, mods3, g_norm)


def kernel(x_prompt, x_sample, state_C, state_n, state_m, c, c_ctx, w_ada, b_ada, g_norm, w_in, ml_gate_bias, ml_head_gain, hy_conv_w, hy_f_w1, hy_f_b1, hy_f_w2, hy_f_b2, hy_f_w3, hy_f_b3, hy_decay, hy_bias, w_out, w_rc, b_rc, w_rf, b_rf, w_gate, w_up, w_down):
    xp = x_prompt.reshape(T_PROMPT, D_MODEL)
    xs = x_sample.reshape(T_SAMPLE, D_MODEL)
    gn = g_norm[0]

    cv = jnp.concatenate([c_ctx[None, :], c, jnp.zeros((MOD_ROWS - 1 - DEC_BATCH, D_MODEL), F32)], axis=0)
    mods3 = _ada(cv, w_ada[0], b_ada[0]).reshape(MOD_ROWS, N_MOD, D_MODEL)

    w_in0 = w_in[0]
    w_qkvo, w_hy = _prep_in_weights(w_in0.T)
    wg = w_in0[:, ML_QKVO_COLS:ML_QKVO_COLS + ML_GATE_COLS]
    gbt = ml_gate_bias[0].reshape(ML_GATE_COLS, 1)
    proj, gates_t = _inproj(xp, xs, mods3, gn, w_qkvo, w_hy, wg.T, gbt)

    gain = ml_head_gain[0].reshape(1, ML_WIDTH)
    y_ml_p, c_new, n_new, m_new = _mlstm(proj, gates_t, gain, None, SEQ, BATCH, 0)
    state = (state_C[:, 0], state_n[:, 0], state_m[:, 0].reshape(DEC_BATCH, 2 * ML_HEADS, 1))
    y_ml_s, _, _, _ = _mlstm(proj, gates_t, gain, state, DEC_SEQ, DEC_BATCH, T_PROMPT // DEC_SEQ)

    w1p = jnp.pad(hy_f_w1[0], ((0, 128 - HY_EMB), (0, 0)))
    b1 = hy_f_b1[0].reshape(1, -1)
    b2 = hy_f_b2[0].reshape(1, -1)
    b3 = hy_f_b3[0].reshape(1, -1)
    dec = hy_decay[0].reshape(1, -1)
    z_parts = []
    for seq_len, n_seq, row_off, width, seqs in ((SEQ, BATCH, 0, SEQ, 4), (DEC_SEQ, DEC_BATCH, T_PROMPT, GRID_W, 2)):
        f, ft = _dft_mats(seq_len)
        coefs = _hyena_filters(seq_len, f, w1p, b1, hy_f_w2[0], b2, hy_f_w3[0], b3, dec)
        z_parts.append(_hyena(proj, hy_conv_w[0], coefs, hy_bias[0], f, ft, seq_len, n_seq, row_off, width, seqs))
    z_p, z_s = z_parts

    pad_r = ROUTER_ROWS - N_GROUPS - N_EXPERTS
    w_r = jnp.pad(jnp.concatenate([w_rc[0], w_rf[0]], axis=1).T, ((0, pad_r), (0, 0)))
    b_r = jnp.pad(jnp.concatenate([b_rc[0], b_rf[0]], axis=0), (0, pad_r)).reshape(ROUTER_ROWS, 1)
    x1, h2ext, bid = _outproj(xp, xs, y_ml_p, y_ml_s, z_p, z_s, mods3, gn, w_out[0].astype(BF16), w_r, b_r)

    pos3, meta = _route(bid)
    pos = pos3.reshape(T_ALL)
    y_sorted = _moe(meta, pos, h2ext, w_gate[0], w_up[0], w_down[0])
    y_p, y_s = _final(pos, y_sorted, x1, mods3, gn)

    new_c = c_new.reshape(BATCH, 1, 2, ML_HEADS, ML_HEAD_DIM, ML_HEAD_DIM)
    new_n = n_new.reshape(BATCH, 1, 2, ML_HEADS, ML_HEAD_DIM)
    new_m = m_new[:, :, 0].reshape(BATCH, 1, 2, ML_HEADS)
    return (y_p.reshape(BATCH, SEQ, D_MODEL), y_s.reshape(DEC_BATCH, DEC_SEQ, D_MODEL), new_c, new_n, new_m)
```

```python
import functools
import math

import jax
import jax.numpy as jnp
import numpy as np
from jax import lax
from jax.experimental import pallas as pl
from jax.experimental.pallas import tpu as pltpu

F32 = jnp.float32
BF16 = jnp.bfloat16

D_MODEL = 1024
BATCH = 16
SEQ = 256
DEC_BATCH = 4
DEC_SEQ = 1024
GRID_W = 64
ML_WIDTH = 512
ML_HEADS = 4
ML_HEAD_DIM = 128
HY_WIDTH = 512
HY_ORDER = 2
HY_EMB = 33
HY_BANDS = 16
HY_FILTER_HIDDEN = 64
HY_MOD_SHIFT = 0.05
N_GROUPS = 4
EXPERTS_PER_GROUP = 4
N_EXPERTS = 16
EXPERT_FF = 512
N_MOD = 6
EPS = 1e-6
ML_QKVO_COLS = 4 * ML_WIDTH
ML_GATE_COLS = 4 * ML_HEADS
HY_COLS = 3 * HY_WIDTH
MAIN_COLS = ML_QKVO_COLS + HY_COLS

T_PROMPT = BATCH * SEQ
T_SAMPLE = DEC_BATCH * DEC_SEQ
T_ALL = T_PROMPT + T_SAMPLE
TILE = 256
N_TILES_P = T_PROMPT // TILE
N_TILES = T_ALL // TILE
MOD_ROWS = 8
K_SCALE = ML_HEAD_DIM ** -0.5
VMEM_LIMIT = 56 * 1024 * 1024


def _cparams(sem):
    return pltpu.CompilerParams(dimension_semantics=sem, vmem_limit_bytes=VMEM_LIMIT)


def _split2(x):
    hi = x.astype(BF16)
    lo = (x - hi.astype(F32)).astype(BF16)
    return hi, lo


def _dot(a, b):
    return jnp.dot(a, b, preferred_element_type=F32)


def _dot_nt(a, b):
    return lax.dot_general(a, b, (((1,), (1,)), ((), ())), preferred_element_type=F32)


def _dot3(a, b):
    ah, al = _split2(a)
    bh, bl = _split2(b)
    return _dot(ah, bh) + _dot(al, bh) + _dot(ah, bl)


def _dot_exact_rhs(x, t):
    x1 = x.astype(BF16)
    r1 = x - x1.astype(F32)
    x2 = r1.astype(BF16)
    x3 = (r1 - x2.astype(F32)).astype(BF16)
    return _dot(x1, t) + _dot(x2, t) + _dot(x3, t)


def _rms(x, g):
    return x * lax.rsqrt(jnp.mean(x * x, axis=-1, keepdims=True) + EPS) * g


def _mod_row_of_tile(i, tiles_per_sample_seq, n_prompt_tiles):
    return jnp.where(i < n_prompt_tiles, 0, 1 + (i - n_prompt_tiles) // tiles_per_sample_seq)


def _ada_kernel(cv_ref, w_ref, b_ref, o_ref):
    cv = cv_ref[...]
    s = cv * jax.nn.sigmoid(cv)
    sh, sl = _split2(s)
    wh, wl = _split2(w_ref[...])
    both = _dot(jnp.concatenate([sh.astype(F32), sl.astype(F32)], axis=0).astype(BF16), wh)
    o_ref[...] = both[0:MOD_ROWS] + both[MOD_ROWS:] + _dot(sh, wl) + b_ref[...]


def _ada(cv, w_ada, b_ada):
    n = N_MOD * D_MODEL
    return pl.pallas_call(
        _ada_kernel,
        out_shape=jax.ShapeDtypeStruct((MOD_ROWS, n), F32),
        grid=(N_MOD,),
        in_specs=[pl.BlockSpec((MOD_ROWS, D_MODEL), lambda j: (0, 0)),
                  pl.BlockSpec((D_MODEL, D_MODEL), lambda j: (0, j)),
                  pl.BlockSpec((1, D_MODEL), lambda j: (0, j))],
        out_specs=pl.BlockSpec((MOD_ROWS, D_MODEL), lambda j: (0, j)),
        compiler_params=_cparams(("arbitrary",)),
        name="ada_mod",
    )(cv, w_ada, b_ada.reshape(1, n))


PREP_COLS = 512


def _prep_q_kernel(wt_ref, o_ref):
    o_ref[...] = wt_ref[...].T.astype(BF16)


def _prep_hy_kernel(wt_hbm, o_ref, buf, sem):
    start = pl.multiple_of(ML_QKVO_COLS + ML_GATE_COLS + pl.program_id(0) * PREP_COLS, 8)
    copy = pltpu.make_async_copy(wt_hbm.at[pl.ds(start, PREP_COLS), :], buf, sem)
    copy.start()
    copy.wait()
    o_ref[...] = buf[...].T.astype(BF16)


def _prep_in_weights(w_in_t):
    out_blk = pl.BlockSpec((D_MODEL, PREP_COLS), lambda j: (0, j))
    w_qkvo = pl.pallas_call(
        _prep_q_kernel,
        out_shape=jax.ShapeDtypeStruct((D_MODEL, ML_QKVO_COLS), BF16),
        grid=(ML_QKVO_COLS // PREP_COLS,),
        in_specs=[pl.BlockSpec((PREP_COLS, D_MODEL), lambda j: (j, 0))], out_specs=out_blk,
        compiler_params=_cparams(("arbitrary",)), name="prep_w_qkvo",
    )(w_in_t)
    w_hy = pl.pallas_call(
        _prep_hy_kernel,
        out_shape=jax.ShapeDtypeStruct((D_MODEL, HY_COLS), BF16),
        grid=(HY_COLS // PREP_COLS,),
        in_specs=[pl.BlockSpec(memory_space=pl.ANY)], out_specs=out_blk,
        scratch_shapes=[pltpu.VMEM((PREP_COLS, D_MODEL), F32), pltpu.SemaphoreType.DMA(())],
        compiler_params=_cparams(("arbitrary",)), name="prep_w_hy",
    )(w_in_t)
    return w_qkvo, w_hy


def _log_sigmoid(x):
    return jnp.minimum(x, 0.0) - jnp.log1p(jnp.exp(-jnp.abs(x)))


def _rows_to_cols(rows):
    ri = lax.broadcasted_iota(jnp.int32, (TILE, TILE), 0)
    ci = lax.broadcasted_iota(jnp.int32, (TILE, TILE), 1)
    eye = jnp.where(ri == ci, 1.0, 0.0).astype(BF16)
    p1 = rows.astype(BF16)
    r1 = rows - p1.astype(F32)
    p2 = r1.astype(BF16)
    p3 = (r1 - p2.astype(F32)).astype(BF16)
    return _dot_nt(eye, p1) + _dot_nt(eye, p2) + _dot_nt(eye, p3)


BIG_TILE = 4 * TILE
N_BIG_P = T_PROMPT // BIG_TILE
N_BIG = T_ALL // BIG_TILE


def _inproj_kernel(xp_ref, xs_ref, m_ref, gn_ref, wq_ref, wh_ref, wgt_ref, gbt_ref, proj_ref, gatet_ref):
    is_p = pl.program_id(0) < N_BIG_P
    halves = [slice(r * TILE, (r + 1) * TILE) for r in range(BIG_TILE // TILE)]
    hs = [_rms(jnp.where(is_p, xp_ref[rows, :], xs_ref[rows, :]), gn_ref[0:1, :]) * (1.0 + m_ref[0, 1:2, :])
          + m_ref[0, 0:1, :] for rows in halves]
    hbs = [h.astype(BF16) for h in hs]
    cb = 512
    for j in range(ML_QKVO_COLS // cb):
        for rows, hb in zip(halves, hbs):
            proj_ref[rows, j * cb:(j + 1) * cb] = _dot(hb, wq_ref[:, j * cb:(j + 1) * cb]).astype(BF16)
    for j in range(HY_COLS // cb):
        lo = ML_QKVO_COLS + j * cb
        for rows, hb in zip(halves, hbs):
            proj_ref[rows, lo:lo + cb] = _dot(hb, wh_ref[:, j * cb:(j + 1) * cb]).astype(BF16)
    wth, wtl = _split2(wgt_ref[...])
    gts = []
    for h, hb in zip(hs, hbs):
        hl = (h - hb.astype(F32)).astype(BF16)
        gt = _dot_nt(wth, hb) + _dot_nt(wth, hl) + _dot_nt(wtl, hb) + gbt_ref[...]
        row = lax.broadcasted_iota(jnp.int32, gt.shape, 0)
        gts.append(jnp.where((row % 8) >= 4, _log_sigmoid(gt), gt))
    for r, gt in enumerate(gts):
        gatet_ref[r] = gt


def _inproj(xp, xs, mods3, g_norm, w_qkvo, w_hy, wgt, gbt):
    tps = DEC_SEQ // BIG_TILE
    per = BIG_TILE // TILE
    return pl.pallas_call(
        _inproj_kernel,
        out_shape=(jax.ShapeDtypeStruct((T_ALL, MAIN_COLS), BF16),
                   jax.ShapeDtypeStruct((N_TILES, ML_GATE_COLS, TILE), F32)),
        grid=(N_BIG,),
        in_specs=[pl.BlockSpec((BIG_TILE, D_MODEL), lambda i: (jnp.minimum(i, N_BIG_P - 1), 0)),
                  pl.BlockSpec((BIG_TILE, D_MODEL), lambda i: (jnp.maximum(i - N_BIG_P, 0), 0)),
                  pl.BlockSpec((1, N_MOD, D_MODEL), lambda i: (_mod_row_of_tile(i, tps, N_BIG_P), 0, 0)),
                  pl.BlockSpec((4, D_MODEL), lambda i: (0, 0)),
                  pl.BlockSpec((D_MODEL, ML_QKVO_COLS), lambda i: (0, 0)),
                  pl.BlockSpec((D_MODEL, HY_COLS), lambda i: (0, 0)),
                  pl.BlockSpec((ML_GATE_COLS, D_MODEL), lambda i: (0, 0)),
                  pl.BlockSpec((ML_GATE_COLS, 1), lambda i: (0, 0))],
        out_specs=(pl.BlockSpec((BIG_TILE, MAIN_COLS), lambda i: (i, 0)),
                   pl.BlockSpec((per, ML_GATE_COLS, TILE), lambda i: (i, 0, 0))),
        compiler_params=_cparams(("arbitrary",)),
        name="in_proj",
    )(xp, xs, mods3, g_norm, w_qkvo, w_hy, wgt, gbt)


ST_ROWS = ML_HEAD_DIM + 16


def _mlstm_kernel(*refs, seq_len, has_state):
    if has_state:
        (q_ref, k_ref, v_ref, o_ref, gt_ref, gain_ref, c0_ref, n0_ref, m0_ref,
         y_ref, c_ref, n_ref, m_ref, vt_ref, hf_ref, hb_ref, st_ref, ms_ref) = refs
    else:
        (q_ref, k_ref, v_ref, o_ref, gt_ref, gain_ref,
         y_ref, c_ref, n_ref, m_ref, vt_ref, hf_ref, hb_ref, st_ref, ms_ref) = refs
    ch = TILE
    nc = seq_len // ch
    hd = ML_HEAD_DIM
    key = lax.broadcasted_iota(jnp.int32, (ch, ch), 0)
    qry = lax.broadcasted_iota(jnp.int32, (ch, ch), 1)
    key_le = key <= qry
    key_ge = key >= qry
    t_le = jnp.where(key_le, 1.0, 0.0).astype(BF16)
    t_ge = jnp.where(key_ge, 1.0, 0.0).astype(BF16)
    sub16 = lax.broadcasted_iota(jnp.int32, (16, ch), 0)
    ln_scale = math.log(K_SCALE)

    for c in range(nc):
        for h in range(ML_HEADS):
            cols = slice(h * hd, (h + 1) * hd)
            vt_ref[c, cols, :] = v_ref[c * ch:(c + 1) * ch, cols].T

    for d in range(2):
        for h in range(ML_HEADS):
            r = d * ML_HEADS + h
            st_ref[r] = jnp.zeros((ST_ROWS, hd), F32)
            if has_state:
                st_ref[r, 0:hd, :] = c0_ref[0, d, h].T
                st_ref[r, hd:hd + 1, :] = n0_ref[0, d, h:h + 1, :]
                ms_ref[r] = jnp.broadcast_to(m0_ref[0, r:r + 1, :], (1, ch))
            else:
                ms_ref[r] = jnp.zeros((1, ch), F32)

    def step(t, carry):
        for d in range(2):
            c = t if d == 0 else nc - 1 - t
            rows = pl.ds(pl.multiple_of(c * ch, ch), ch)
            grow = gt_ref[c]
            brow_all = _dot_exact_rhs(grow, t_le if d == 0 else t_ge)
            ccol_all = _rows_to_cols(grow - pltpu.roll(brow_all, ML_GATE_COLS - ML_HEADS, axis=0))
            mask = key_le if d == 0 else key_ge
            hacc_ref = hf_ref if d == 0 else hb_ref
            heads = range(ML_HEADS)
            regs = [d * ML_HEADS + h for h in heads]
            colss = [slice(h * hd, (h + 1) * hd) for h in heads]
            qs = [q_ref[rows, cols] for cols in colss]
            ks = [k_ref[rows, cols] for cols in colss]
            vts = [vt_ref[c, cols, :] for cols in colss]
            sts = [st_ref[r] for r in regs]
            m_prevs = [ms_ref[r] for r in regs]
            b_rows = [brow_all[(1 + 2 * d) * ML_HEADS + h:(1 + 2 * d) * ML_HEADS + h + 1, :] for h in heads]
            ig_rows = [grow[2 * d * ML_HEADS + h:2 * d * ML_HEADS + h + 1, :] for h in heads]
            qks = [_dot_nt(k, q) for k, q in zip(ks, qs)]
            iqs = [_dot_nt(st.astype(BF16), q) for st, q in zip(sts, qs)]
            ss, sc_inters, m_poss = [], [], []
            for h in heads:
                icol = 2 * d * ML_HEADS + h
                c_col = ccol_all[:, icol:icol + 1]
                logd = jnp.where(mask, b_rows[h] + c_col, -jnp.inf)
                inter = b_rows[h] + m_prevs[h]
                m_pos = jnp.maximum(inter, jnp.max(logd, axis=0, keepdims=True))
                ss.append(qks[h] * jnp.exp(logd - (m_pos - ln_scale)))
                sc_inters.append(jnp.exp(inter - m_pos))
                m_poss.append(m_pos)
            pvs = [_dot(vt, s.astype(BF16)) for vt, s in zip(vts, ss)]
            for h in heads:
                num = sc_inters[h] * iqs[h][0:hd] + pvs[h]
                den = sc_inters[h] * iqs[h][hd:hd + 1] + jnp.sum(ss[h], axis=0, keepdims=True)
                hacc_ref[c, colss[h], :] = num * (1.0 / jnp.maximum(jnp.abs(den), jnp.exp(-m_poss[h])))
            lhss, decays = [], []
            for h in heads:
                b_row = b_rows[h]
                b_last = b_row[:, ch - 1:ch] if d == 0 else b_row[:, 0:1]
                logw = b_last - b_row + ig_rows[h]
                m_new = jnp.maximum(b_last + m_prevs[h], jnp.max(logw, axis=1, keepdims=True))
                w = jnp.exp(logw - (m_new - ln_scale))
                decays.append(jnp.exp(b_last + m_prevs[h] - m_new))
                lhss.append(jnp.concatenate([(vts[h].astype(F32) * w).astype(BF16),
                                             jnp.where(sub16 == 0, w, 0.0).astype(BF16)], axis=0))
                ms_ref[regs[h]] = m_new
            upds = [_dot(lhs, k) for lhs, k in zip(lhss, ks)]
            for h in heads:
                st_ref[regs[h]] = decays[h][:, 0:hd] * sts[h] + upds[h]
        return carry

    lax.fori_loop(0, nc, step, 0)

    for d in range(2):
        for h in range(ML_HEADS):
            r = d * ML_HEADS + h
            c_ref[0, d, h] = st_ref[r, 0:hd, :].T
            n_ref[0, d, h:h + 1, :] = st_ref[r, hd:hd + 1, :]
            m_ref[0, r:r + 1, :] = ms_ref[r][:, 0:hd]
    for c in range(nc):
        for h in range(ML_HEADS):
            cols = slice(h * hd, (h + 1) * hd)
            ht = hf_ref[c, cols, :] + hb_ref[c, cols, :]
            ht = ht * lax.rsqrt(jnp.mean(ht * ht, axis=0, keepdims=True) + EPS)
            rows = slice(c * ch, (c + 1) * ch)
            y = ht.T * gain_ref[:, cols] * jax.nn.sigmoid(o_ref[rows, cols].astype(F32))
            y_ref[rows, cols] = y.astype(BF16)


def _mlstm(proj, gates_t, gain, state, seq_len, n_seq, row_block_off):
    has_state = state is not None
    tiles = seq_len // TILE
    off = row_block_off
    qkvo_specs = [pl.BlockSpec((seq_len, ML_WIDTH), functools.partial(lambda b, j: (off + b, j), j=j))
                  for j in range(4)]
    in_specs = qkvo_specs + [
        pl.BlockSpec((tiles, ML_GATE_COLS, TILE), lambda b: (off + b, 0, 0)),
        pl.BlockSpec((1, ML_WIDTH), lambda b: (0, 0)),
    ]
    args = [proj, proj, proj, proj, gates_t, gain]
    if has_state:
        c0, n0, m0 = state
        in_specs += [
            pl.BlockSpec((1, 2, ML_HEADS, ML_HEAD_DIM, ML_HEAD_DIM), lambda b: (b, 0, 0, 0, 0)),
            pl.BlockSpec((1, 2, ML_HEADS, ML_HEAD_DIM), lambda b: (b, 0, 0, 0)),
            pl.BlockSpec((1, 2 * ML_HEADS, 1), lambda b: (b, 0, 0)),
        ]
        args += [c0, n0, m0]
    out_shape = (jax.ShapeDtypeStruct((n_seq * seq_len, ML_WIDTH), BF16),
                 jax.ShapeDtypeStruct((n_seq, 2, ML_HEADS, ML_HEAD_DIM, ML_HEAD_DIM), F32),
                 jax.ShapeDtypeStruct((n_seq, 2, ML_HEADS, ML_HEAD_DIM), F32),
                 jax.ShapeDtypeStruct((n_seq, 2 * ML_HEADS, ML_HEAD_DIM), F32))
    out_specs = (pl.BlockSpec((seq_len, ML_WIDTH), lambda b: (b, 0)),
                 pl.BlockSpec((1, 2, ML_HEADS, ML_HEAD_DIM, ML_HEAD_DIM), lambda b: (b, 0, 0, 0, 0)),
                 pl.BlockSpec((1, 2, ML_HEADS, ML_HEAD_DIM), lambda b: (b, 0, 0, 0)),
                 pl.BlockSpec((1, 2 * ML_HEADS, ML_HEAD_DIM), lambda b: (b, 0, 0)))
    scratch = [pltpu.VMEM((tiles, ML_WIDTH, TILE), BF16),
               pltpu.VMEM((tiles, ML_WIDTH, TILE), F32), pltpu.VMEM((tiles, ML_WIDTH, TILE), F32),
               pltpu.VMEM((2 * ML_HEADS, ST_ROWS, ML_HEAD_DIM), F32),
               pltpu.VMEM((2 * ML_HEADS, 1, TILE), F32)]
    return pl.pallas_call(
        functools.partial(_mlstm_kernel, seq_len=seq_len, has_state=has_state),
        out_shape=out_shape, grid=(n_seq,), in_specs=in_specs, out_specs=out_specs,
        scratch_shapes=scratch, compiler_params=_cparams(("arbitrary",)),
        name=f"mlstm_{seq_len}",
    )(*args)


def _dft_mats(seq_len):
    k = np.arange(seq_len, dtype=np.int64)[:, None]
    d = np.arange(seq_len, dtype=np.int64)[None, :]
    ang = np.pi * ((k * d) % (2 * seq_len)).astype(np.float64) / seq_len
    sinm = np.sin(ang)
    sinm[0, :] = np.where(d[0] % 2 == 0, 1.0, -1.0)
    f = np.concatenate([np.cos(ang), sinm], axis=0).astype(np.float32)
    return jnp.asarray(f).astype(BF16), jnp.asarray(np.ascontiguousarray(f.T)).astype(BF16)


def _filter_feats(seq_len):
    t = np.linspace(0.0, 1.0, seq_len, dtype=np.float64)[:, None]
    wpos = 2.0 * np.pi * np.arange(seq_len, dtype=np.float64)[:, None] / seq_len
    bands = np.linspace(1e-4, HY_BANDS - 1, HY_BANDS, dtype=np.float64)[None, :]
    z = np.concatenate([t, np.cos(bands * wpos), -np.sin(bands * wpos)], axis=-1)
    return jnp.asarray(np.pad(z, ((0, 0), (0, 128 - HY_EMB))).astype(np.float32))


def _filter_kernel(z_ref, w1_ref, b1_ref, w2_ref, b2_ref, w3_ref, b3_ref, dec_ref, f_ref,
                   a_ref, b_ref, d_ref, *, seq_len):
    n = 2 * seq_len
    oc = 2 * HY_WIDTH
    z = z_ref[...]
    h = jnp.sin(_dot3(z, w1_ref[...]) + b1_ref[...])
    h = jnp.sin(_dot3(h, w2_ref[...]) + b2_ref[...])
    t = z[:, 0:1]
    di = lax.broadcasted_iota(jnp.int32, (seq_len, 1), 0)
    sgn = jnp.where(di % 2 == 0, 1.0, -1.0)
    first = di == 0
    ssums, sdifs = [], []
    for o in range(HY_ORDER):
        cols = slice(o * oc, (o + 1) * oc)
        g = _dot3(h, w3_ref[:, cols]) + b3_ref[:, cols]
        g = g * (jnp.exp(-t * jnp.abs(dec_ref[:, cols])) + HY_MOD_SHIFT)
        ss = jnp.sum(g * g, axis=0, keepdims=True)
        inv = lax.rsqrt(ss[:, :HY_WIDTH] + ss[:, HY_WIDTH:] + EPS)
        hp = g[:, :HY_WIDTH] * inv
        hn = g[:, HY_WIDTH:] * inv
        ssums.append(hp + hn)
        sdifs.append(hp - hn)
    hcs = [_dot(f_ref[0:seq_len, :], s.astype(BF16)) for s in ssums]
    hss = [_dot(f_ref[seq_len:n, :], s.astype(BF16)) for s in sdifs]
    for o in range(HY_ORDER):
        nyq = jnp.sum(ssums[o] * sgn, axis=0, keepdims=True)
        a_ref[o] = hcs[o] * jnp.where(first, 1.0 / n, 2.0 / n)
        b_ref[o] = jnp.where(first, 0.0, hss[o] * (2.0 / n))
        d_ref[o] = jnp.where(first, nyq * (1.0 / n), hcs[o] * (2.0 / n))


def _hyena_filters(seq_len, f, w1p, b1, w2, b2, w3, b3, dec):
    z = _filter_feats(seq_len)
    out = jax.ShapeDtypeStruct((HY_ORDER, seq_len, HY_WIDTH), F32)
    return pl.pallas_call(
        functools.partial(_filter_kernel, seq_len=seq_len),
        out_shape=(out, out, out),
        compiler_params=pltpu.CompilerParams(vmem_limit_bytes=VMEM_LIMIT),
        name=f"hyena_filter_{seq_len}",
    )(z, w1p, b1, w2, b2, w3, b3, dec, f)


def _hyena_kernel(x1_ref, x2_ref, v_ref, cw1_ref, cw2_ref, cwv_ref, a_ref, b_ref, d_ref, bias_ref,
                  f_ref, ft_ref, z_ref, *, seq_len, width, seqs):
    rows = seqs * seq_len
    ti = lax.broadcasted_iota(jnp.int32, (rows, 1), 0)
    has_prev = (ti % width) != 0
    has_next = (ti % width) != (width - 1)

    def short_conv(x_ref, w_ref):
        x = x_ref[...].astype(F32)
        prev = jnp.where(has_prev, pltpu.roll(x, 1, axis=0), 0.0)
        nxt = jnp.where(has_next, pltpu.roll(x, rows - 1, axis=0), 0.0)
        return w_ref[0:1, :] * prev + w_ref[1:2, :] * x + w_ref[2:3, :] * nxt

    gates = (short_conv(x1_ref, cw1_ref), short_conv(x2_ref, cw2_ref))
    v = short_conv(v_ref, cwv_ref)
    sls = [slice(i * seq_len, (i + 1) * seq_len) for i in range(seqs)]
    zs = [v[sl] for sl in sls]
    for o in range(HY_ORDER):
        a, b, dd = a_ref[o], b_ref[o], d_ref[o]
        us = [_dot(f_ref[...], z.astype(BF16)) for z in zs]
        ys = []
        for u in us:
            ut = u[:seq_len]
            ub = u[seq_len:]
            ys.append(((ut * a - ub * b).astype(BF16), (ut * b + ub * dd).astype(BF16)))
        convs = [_dot(ft_ref[:, :seq_len], yt) + _dot(ft_ref[:, seq_len:], yb) for yt, yb in ys]
        zs = [gates[o][sl] * (y + bias_ref[o:o + 1, :] * z) for sl, y, z in zip(sls, convs, zs)]
    for sl, z in zip(sls, zs):
        z_ref[sl, :] = z.astype(BF16)


def _hyena(proj, conv_w, coefs, hy_bias, f, ft, seq_len, n_seq, row_off, width, seqs):
    cb = 256
    nblk = HY_WIDTH // cb
    base = ML_QKVO_COLS // cb
    rows = seqs * seq_len
    off = row_off // rows
    a, b, d = coefs

    def col_spec(part):
        return pl.BlockSpec((rows, cb), lambda j, s: (off + s, base + part * nblk + j))

    def w_spec(part):
        return pl.BlockSpec((3, cb), lambda j, s: (0, part * nblk + j))

    coef_spec = pl.BlockSpec((HY_ORDER, seq_len, cb), lambda j, s: (0, 0, j))
    return pl.pallas_call(
        functools.partial(_hyena_kernel, seq_len=seq_len, width=width, seqs=seqs),
        out_shape=jax.ShapeDtypeStruct((n_seq * seq_len, HY_WIDTH), BF16),
        grid=(nblk, n_seq // seqs),
        in_specs=[col_spec(0), col_spec(1), col_spec(2), w_spec(0), w_spec(1), w_spec(2),
                  coef_spec, coef_spec, coef_spec,
                  pl.BlockSpec((HY_ORDER, cb), lambda j, s: (0, j)),
                  pl.BlockSpec((2 * seq_len, seq_len), lambda j, s: (0, 0)),
                  pl.BlockSpec((seq_len, 2 * seq_len), lambda j, s: (0, 0))],
        out_specs=pl.BlockSpec((rows, cb), lambda j, s: (s, j)),
        compiler_params=_cparams(("arbitrary", "arbitrary")),
        name=f"hyena_conv_{seq_len}",
    )(proj, proj, proj, conv_w, conv_w, conv_w, a, b, d, hy_bias, f, ft)


def _first_max(x, n):
    mx = jnp.max(x, axis=0, keepdims=True)
    row = lax.broadcasted_iota(jnp.int32, x.shape, 0).astype(F32)
    idx = jnp.min(jnp.where(x == mx, row, float(n)), axis=0, keepdims=True)
    return mx, idx.astype(jnp.int32)


ROUTER_ROWS = 32
PAIRS_PER_GROUP = 6
N_BUCKETS = N_GROUPS * PAIRS_PER_GROUP
PAIR_SLOTS = ((0, 1), (0, 2), (0, 3), (1, 3), (1, 2), (3, 2))
LANES = 128
H2_EXT = D_MODEL + LANES
ROW_TILE = 256
ROW_CAP = T_ALL + N_BUCKETS * ROW_TILE
N_ROW_TILES = ROW_CAP // ROW_TILE


def _outproj_kernel(xp_ref, xs_ref, yp_ref, ys_ref, zp_ref, zs_ref, m_ref, gn_ref, wo_ref, wr_ref, br_ref,
                    x1_ref, h2_ref, bid_ref):
    is_p = pl.program_id(0) < N_BIG_P
    wrh, wrl = _split2(wr_ref[...])
    halves = [slice(r * TILE, (r + 1) * TILE) for r in range(BIG_TILE // TILE)]
    ys = [_dot(jnp.where(is_p, yp_ref[rows, :], ys_ref[rows, :]), wo_ref[0:ML_WIDTH, :])
          + _dot(jnp.where(is_p, zp_ref[rows, :], zs_ref[rows, :]), wo_ref[ML_WIDTH:, :]) for rows in halves]
    h2s = []
    for rows, y in zip(halves, ys):
        x = jnp.where(is_p, xp_ref[rows, :], xs_ref[rows, :])
        x1 = x + m_ref[0, 2:3, :] * _rms(y, gn_ref[1:2, :])
        x1_ref[rows, :] = x1
        h2 = _rms(x1, gn_ref[2:3, :]) * (1.0 + m_ref[0, 4:5, :]) + m_ref[0, 3:4, :]
        h2_ref[rows, 0:D_MODEL] = h2
        h2s.append(h2)
    logits = []
    for h2 in h2s:
        h2h, h2l = _split2(h2)
        logits.append(_dot_nt(wrh, h2h) + _dot_nt(wrh, h2l) + _dot_nt(wrl, h2h) + br_ref[...])
    routed = [_route_tile(lg) for lg in logits]
    for r, (rows, (gate_rows, bucket)) in enumerate(zip(halves, routed)):
        h2_ref[rows, D_MODEL:H2_EXT] = jnp.zeros((TILE, LANES), F32)
        h2_ref[rows, D_MODEL:D_MODEL + 8] = _rows_to_cols(gate_rows)
        bid_ref[r] = bucket


def _route_tile(logits):
    lc = logits[0:N_GROUPS]
    mx, gi = _first_max(lc, N_GROUPS)
    p_grp = 1.0 / jnp.sum(jnp.exp(lc - mx), axis=0, keepdims=True)
    lsel = jnp.zeros((EXPERTS_PER_GROUP, TILE), F32)
    for g in range(N_GROUPS):
        lo = N_GROUPS + g * EXPERTS_PER_GROUP
        lsel = jnp.where(gi == g, logits[lo:lo + EXPERTS_PER_GROUP], lsel)
    l1, i1 = _first_max(lsel, EXPERTS_PER_GROUP)
    sub4 = lax.broadcasted_iota(jnp.int32, lsel.shape, 0)
    l2, i2 = _first_max(jnp.where(sub4 == i1, -jnp.inf, lsel), EXPERTS_PER_GROUP)
    e2 = jnp.exp(l2 - l1)
    w1 = p_grp / (1.0 + e2)
    w2 = p_grp * e2 / (1.0 + e2)
    lo_e = jnp.minimum(i1, i2)
    hi_e = jnp.maximum(i1, i2)
    pair = jnp.where(lo_e == 0, hi_e - 1, jnp.where(lo_e == 1, jnp.where(hi_e == 3, 3, 4), 5))
    slot_a = jnp.where(pair == 5, hi_e, lo_e)
    first_in_a = i1 == slot_a
    w_a = jnp.where(first_in_a, w1, w2)
    w_b = jnp.where(first_in_a, w2, w1)
    sub = lax.broadcasted_iota(jnp.int32, (8, TILE), 0)
    gate_rows = jnp.where(sub == 0, w_a, jnp.where(sub == 1, w_b, 0.0))
    return gate_rows, gi * PAIRS_PER_GROUP + pair


def _outproj(xp, xs, yp, ys, zp, zs, mods3, g_norm, w_out, w_r, b_r):
    tps = DEC_SEQ // BIG_TILE
    per = BIG_TILE // TILE
    pidx = lambda i: (jnp.minimum(i, N_BIG_P - 1), 0)
    sidx = lambda i: (jnp.maximum(i - N_BIG_P, 0), 0)
    return pl.pallas_call(
        _outproj_kernel,
        out_shape=(jax.ShapeDtypeStruct((T_ALL, D_MODEL), F32),
                   jax.ShapeDtypeStruct((T_ALL, H2_EXT), F32),
                   jax.ShapeDtypeStruct((N_TILES, 1, TILE), jnp.int32)),
        grid=(N_BIG,),
        in_specs=[pl.BlockSpec((BIG_TILE, D_MODEL), pidx), pl.BlockSpec((BIG_TILE, D_MODEL), sidx),
                  pl.BlockSpec((BIG_TILE, ML_WIDTH), pidx), pl.BlockSpec((BIG_TILE, ML_WIDTH), sidx),
                  pl.BlockSpec((BIG_TILE, HY_WIDTH), pidx), pl.BlockSpec((BIG_TILE, HY_WIDTH), sidx),
                  pl.BlockSpec((1, N_MOD, D_MODEL), lambda i: (_mod_row_of_tile(i, tps, N_BIG_P), 0, 0)),
                  pl.BlockSpec((4, D_MODEL), lambda i: (0, 0)),
                  pl.BlockSpec((D_MODEL, D_MODEL), lambda i: (0, 0)),
                  pl.BlockSpec((ROUTER_ROWS, D_MODEL), lambda i: (0, 0)),
                  pl.BlockSpec((ROUTER_ROWS, 1), lambda i: (0, 0))],
        out_specs=(pl.BlockSpec((BIG_TILE, D_MODEL), lambda i: (i, 0)),
                   pl.BlockSpec((BIG_TILE, H2_EXT), lambda i: (i, 0)),
                   pl.BlockSpec((per, 1, TILE), lambda i: (i, 0, 0))),
        compiler_params=_cparams(("arbitrary",)),
        name="out_proj_router",
    )(xp, xs, yp, ys, zp, zs, mods3, g_norm, w_out, w_r, b_r)


def _route_kernel(bid_ref, pos_ref, meta_ref):
    nb = 32
    tm = float(ROW_TILE)
    sub = lax.broadcasted_iota(jnp.int32, (nb, TILE), 0)
    ri = lax.broadcasted_iota(jnp.int32, (TILE, TILE), 0)
    ci = lax.broadcasted_iota(jnp.int32, (TILE, TILE), 1)
    before = jnp.where(ri < ci, 1.0, 0.0).astype(BF16)

    def onehot(blk):
        return jnp.where(sub == bid_ref[blk], 1.0, 0.0)

    zeros = jnp.zeros((nb, 1), F32)
    cnt = lax.fori_loop(0, N_TILES, lambda blk, c: c + jnp.sum(onehot(blk), axis=1, keepdims=True), zeros)
    padded = jnp.floor((cnt + (tm - 1.0)) * (1.0 / tm)) * tm
    r32 = lax.broadcasted_iota(jnp.int32, (nb, nb), 0)
    c32 = lax.broadcasted_iota(jnp.int32, (nb, nb), 1)
    padded_row = jnp.sum(jnp.where(r32 == c32, padded, 0.0), axis=0, keepdims=True)
    offs = jnp.sum(jnp.where(c32 < r32, padded_row, 0.0), axis=1, keepdims=True)
    ends = offs + padded

    def place(blk, seen):
        oh = onehot(blk)
        rank = _dot(oh.astype(BF16), before)
        pos = jnp.sum(oh * (rank + seen + offs), axis=0, keepdims=True)
        pos_ref[blk] = pos.astype(jnp.int32)
        return seen + jnp.sum(oh, axis=1, keepdims=True)

    lax.fori_loop(0, N_TILES, place, zeros)

    start = lax.broadcasted_iota(jnp.int32, (nb, 128), 1).astype(F32) * tm
    bsub = lax.broadcasted_iota(jnp.int32, (nb, 128), 0)
    done = jnp.where((bsub < N_BUCKETS) & (ends <= start), 1.0, 0.0)
    tb = jnp.sum(done, axis=0, keepdims=True)
    valid = jnp.where(tb < N_BUCKETS, 1.0, 0.0)
    tbc = jnp.minimum(tb, N_BUCKETS - 1.0)
    grp = jnp.floor((tbc + 0.5) * (1.0 / PAIRS_PER_GROUP))
    pair = tbc - PAIRS_PER_GROUP * grp
    loc_a = jnp.zeros_like(pair)
    loc_b = jnp.zeros_like(pair)
    for k, (sa, sb) in enumerate(PAIR_SLOTS):
        loc_a = jnp.where(pair == k, float(sa), loc_a)
        loc_b = jnp.where(pair == k, float(sb), loc_b)
    mine = bsub.astype(F32) == tbc
    used = jnp.sum(jnp.where(mine, offs + cnt, 0.0), axis=0, keepdims=True)
    n_rows = jnp.clip(used - start[0:1], 0.0, tm) * valid
    lane = lax.broadcasted_iota(jnp.int32, (1, 128), 1).astype(F32)
    r128 = lax.broadcasted_iota(jnp.int32, (128, 128), 0)
    c128 = lax.broadcasted_iota(jnp.int32, (128, 128), 1)
    ex_a = grp * EXPERTS_PER_GROUP + loc_a
    ex_b = grp * EXPERTS_PER_GROUP + loc_b
    new_a = jnp.zeros_like(lane)
    new_b = jnp.zeros_like(lane)
    for e in range(N_EXPERTS):
        uses = (valid > 0.0) & ((ex_a == e) | (ex_b == e))
        first = jnp.min(jnp.where(uses, lane, 1e9), axis=1, keepdims=True)
        new_a = jnp.where((ex_a == e) & (lane == first), 1.0, new_a)
        new_b = jnp.where((ex_b == e) & (lane == first) & (ex_a != e), 1.0, new_b)

    def window(ex, new):
        last = jnp.where(new > 0.0, lane, -1.0)
        shift = 1
        while shift < 128:
            moved = pltpu.roll(jnp.broadcast_to(last, (8, 128)), shift, axis=1)[0:1]
            last = jnp.maximum(last, jnp.where(lane >= shift, moved, -1.0))
            shift *= 2
        ex_col = jnp.sum(jnp.where(r128 == c128, ex, 0.0), axis=1, keepdims=True)
        return jnp.sum(jnp.where(r128.astype(F32) == last, ex_col, 0.0), axis=0, keepdims=True)

    rows = (window(ex_a, new_a), window(ex_b, new_b), valid, n_rows, new_a, new_b, loc_a, loc_b)
    row8 = lax.broadcasted_iota(jnp.int32, (8, 128), 0)
    meta = jnp.zeros((8, 128), F32)
    for i, row in enumerate(rows):
        meta = jnp.where(row8 == i, row, meta)
    meta_ref[...] = meta.astype(jnp.int32)


def _route(bid):
    return pl.pallas_call(
        _route_kernel,
        out_shape=(jax.ShapeDtypeStruct((N_TILES, 1, TILE), jnp.int32),
                   jax.ShapeDtypeStruct((8, 128), jnp.int32)),
        compiler_params=pltpu.CompilerParams(vmem_limit_bytes=VMEM_LIMIT),
        name="moe_route",
    )(bid)


def _moe_kernel(meta_ref, pos_ref, h2_hbm, wga_ref, wua_ref, wda_ref, wgb_ref, wub_ref, wdb_ref,
                y_ref, src_ref, xbuf, sem, wg_s, wu_s, wd_s):
    j = pl.program_id(0)

    def row_copy(tile, r, slot):
        tok = src_ref[tile * ROW_TILE + r]
        return pltpu.make_async_copy(h2_hbm.at[pl.ds(tok, 1), :], xbuf.at[slot, pl.ds(r, 1), :], sem.at[slot])

    def issue_rows(tile, slot, lo, hi):
        for r in range(lo, hi):
            row_copy(tile, r, slot).start()

    group = 8

    def row_groups(tile):
        return (meta_ref[3, tile] + (group - 1)) // group

    def issue_counted(tile, slot):
        def body(g, c):
            for k in range(group):
                row_copy(tile, g * group + k, slot).start()
            return c
        lax.fori_loop(0, row_groups(tile), body, 0)

    def wait_counted(tile, slot):
        def body(g, c):
            for k in range(group):
                row_copy(tile, g * group + k, slot).wait()
            return c
        lax.fori_loop(0, row_groups(tile), body, 0)

    def wait_full(slot):
        pltpu.make_async_copy(h2_hbm.at[pl.ds(0, ROW_TILE), :], xbuf.at[slot], sem.at[slot]).wait()

    @pl.when(j == 0)
    def _():
        xbuf[...] = jnp.zeros_like(xbuf)

        def clear(t, c):
            n = meta_ref[3, t]
            for k in range(group - 1):
                src_ref[t * ROW_TILE + jnp.minimum(n + k, ROW_TILE - 1)] = 0
            return c
        lax.fori_loop(0, N_ROW_TILES, clear, 0)

        def invert(t, c):
            src_ref[pos_ref[t]] = t
            return c
        lax.fori_loop(0, T_ALL, invert, 0, unroll=8)

        @pl.when(meta_ref[2, 0] == 1)
        def _():
            issue_counted(0, 0)

    nxt = jnp.minimum(j + 1, N_ROW_TILES - 1)
    has_next = (j + 1 < N_ROW_TILES) & (meta_ref[2, nxt] == 1)
    next_full = has_next & (meta_ref[3, nxt] == ROW_TILE)
    valid = meta_ref[2, j] == 1
    full = meta_ref[3, j] == ROW_TILE

    @pl.when(valid & has_next & jnp.logical_not(next_full))
    def _():
        issue_counted(nxt, nxt % 2)

    @pl.when(valid & full)
    def _():
        wait_full(j % 2)

    @pl.when(valid & jnp.logical_not(full))
    def _():
        wait_counted(j, j % 2)

    loc_a = meta_ref[6, j]
    loc_b = meta_ref[7, j]

    @pl.when(valid & (meta_ref[4, j] == 1))
    def _():
        wg_s[loc_a] = wga_ref[0].astype(BF16)
        wu_s[loc_a] = wua_ref[0].astype(BF16)
        wd_s[loc_a] = wda_ref[0].astype(BF16)

    @pl.when(valid & (meta_ref[5, j] == 1))
    def _():
        wg_s[loc_b] = wgb_ref[0].astype(BF16)
        wu_s[loc_b] = wub_ref[0].astype(BF16)
        wd_s[loc_b] = wdb_ref[0].astype(BF16)

    def compute(fetch_next):
        slot = j % 2
        nslot = nxt % 2
        step = ROW_TILE // 8
        batches = iter(range(0, ROW_TILE, step))

        def fetch():
            if fetch_next:
                lo = next(batches)
                issue_rows(nxt, nslot, lo, lo + step)

        x = xbuf[slot, :, 0:D_MODEL].astype(BF16)
        gates = xbuf[slot, :, D_MODEL:H2_EXT]
        hg_a = _dot(x, wg_s[loc_a])
        fetch()
        hu_a = _dot(x, wu_s[loc_a])
        fetch()
        hg_b = _dot(x, wg_s[loc_b])
        fetch()
        hu_b = _dot(x, wu_s[loc_b])
        fetch()
        act_a = (hg_a * jax.nn.sigmoid(hg_a) * hu_a * gates[:, 0:1]).astype(BF16)
        fetch()
        act_b = (hg_b * jax.nn.sigmoid(hg_b) * hu_b * gates[:, 1:2]).astype(BF16)
        fetch()
        y = _dot(act_a, wd_s[loc_a])
        fetch()
        y = y + _dot(act_b, wd_s[loc_b])
        fetch()
        y_ref[...] = y

    @pl.when(valid & next_full)
    def _():
        compute(True)

    @pl.when(valid & jnp.logical_not(next_full))
    def _():
        compute(False)

    @pl.when(jnp.logical_not(valid))
    def _():
        y_ref[...] = jnp.zeros_like(y_ref)


def _moe(meta, pos, h2ext, w_gate, w_up, w_down):
    up_spec = lambda slot: pl.BlockSpec((1, D_MODEL, EXPERT_FF), lambda j, meta, pos: (meta[slot, j], 0, 0))
    down_spec = lambda slot: pl.BlockSpec((1, EXPERT_FF, D_MODEL), lambda j, meta, pos: (meta[slot, j], 0, 0))
    grid_spec = pltpu.PrefetchScalarGridSpec(
        num_scalar_prefetch=2,
        grid=(N_ROW_TILES,),
        in_specs=[pl.BlockSpec(memory_space=pl.ANY),
                  up_spec(0), up_spec(0), down_spec(0), up_spec(1), up_spec(1), down_spec(1)],
        out_specs=pl.BlockSpec((ROW_TILE, D_MODEL), lambda j, meta, pos: (j, 0)),
        scratch_shapes=[pltpu.SMEM((ROW_CAP,), jnp.int32),
                        pltpu.VMEM((2, ROW_TILE, H2_EXT), F32),
                        pltpu.SemaphoreType.DMA((2,)),
                        pltpu.VMEM((EXPERTS_PER_GROUP, D_MODEL, EXPERT_FF), BF16),
                        pltpu.VMEM((EXPERTS_PER_GROUP, D_MODEL, EXPERT_FF), BF16),
                        pltpu.VMEM((EXPERTS_PER_GROUP, EXPERT_FF, D_MODEL), BF16)])
    return pl.pallas_call(
        _moe_kernel,
        out_shape=jax.ShapeDtypeStruct((ROW_CAP, D_MODEL), F32),
        grid_spec=grid_spec,
        compiler_params=_cparams(("arbitrary",)),
        name="moe_experts",
    )(meta, pos, h2ext, w_gate, w_up, w_down, w_gate, w_up, w_down)


def _final_kernel(pos_ref, y_hbm, x1_ref, m_ref, gn_ref, op_ref, os_ref, ybuf, sem):
    i = pl.program_id(0)

    def row_copy(tile, r, slot):
        p = pos_ref[tile * TILE + r]
        return pltpu.make_async_copy(y_hbm.at[pl.ds(p, 1), :], ybuf.at[slot, pl.ds(r, 1), :], sem.at[slot])

    def issue(tile, slot):
        def body(r2, c):
            row_copy(tile, 2 * r2, slot).start(priority=0)
            row_copy(tile, 2 * r2 + 1, slot).start(priority=1)
            return c
        lax.fori_loop(0, TILE // 2, body, 0, unroll=4)

    def wait(slot):
        pltpu.make_async_copy(y_hbm.at[pl.ds(0, TILE)], ybuf.at[slot], sem.at[slot]).wait()

    @pl.when(i == 0)
    def _():
        issue(0, 0)

    @pl.when(i + 1 < N_TILES)
    def _():
        issue(i + 1, (i + 1) % 2)

    slot = i % 2
    wait(slot)
    out = x1_ref[...] + m_ref[0, 5:6, :] * _rms(ybuf[slot], gn_ref[3:4, :])

    @pl.when(i < N_TILES_P)
    def _():
        op_ref[...] = out

    @pl.when(i >= N_TILES_P)
    def _():
        os_ref[...] = out


def _final(pos, y_sorted, x1, mods3, g_norm):
    tps = DEC_SEQ // TILE
    grid_spec = pltpu.PrefetchScalarGridSpec(
        num_scalar_prefetch=1,
        grid=(N_TILES,),
        in_specs=[pl.BlockSpec(memory_space=pl.ANY),
                  pl.BlockSpec((TILE, D_MODEL), lambda i, pos: (i, 0)),
                  pl.BlockSpec((1, N_MOD, D_MODEL), lambda i, pos: (_mod_row_of_tile(i, tps, N_TILES_P), 0, 0)),
                  pl.BlockSpec((4, D_MODEL), lambda i, pos: (0, 0))],
        out_specs=(pl.BlockSpec((TILE, D_MODEL), lambda i, pos: (jnp.minimum(i, N_TILES_P - 1), 0)),
                   pl.BlockSpec((TILE, D_MODEL), lambda i, pos: (jnp.maximum(i - N_TILES_P, 0), 0))),
        scratch_shapes=[pltpu.VMEM((2, TILE, D_MODEL), F32), pltpu.SemaphoreType.DMA((2,))])
    return pl.pallas_call(
        _final_kernel,
        out_shape=(jax.ShapeDtypeStruct((T_PROMPT, D_MODEL), F32),
                   jax.ShapeDtypeStruct((T_SAMPLE, D_MODEL), F32)),
        grid_spec=grid_spec,
        compiler_params=_cparams(("arbitrary",)),
        name="moe_combine_final",
    )(pos, y_sorted, x1, mods3, g_norm)


def kernel(x_prompt, x_sample, state_C, state_n, state_m, c, c_ctx, w_ada, b_ada, g_norm, w_in, ml_gate_bias, ml_head_gain, hy_conv_w, hy_f_w1, hy_f_b1, hy_f_w2, hy_f_b2, hy_f_w3, hy_f_b3, hy_decay, hy_bias, w_out, w_rc, b_rc, w_rf, b_rf, w_gate, w_up, w_down):
    xp = x_prompt.reshape(T_PROMPT, D_MODEL)
    xs = x_sample.reshape(T_SAMPLE, D_MODEL)
    gn = g_norm[0]

    cv = jnp.concatenate([c_ctx[None, :], c, jnp.zeros((MOD_ROWS - 1 - DEC_BATCH, D_MODEL), F32)], axis=0)
    mods3 = _ada(cv, w_ada[0], b_ada[0]).reshape(MOD_ROWS, N_MOD, D_MODEL)

    w_in0 = w_in[0]
    w_qkvo, w_hy = _prep_in_weights(w_in0.T)
    wg = w_in0[:, ML_QKVO_COLS:ML_QKVO_COLS + ML_GATE_COLS]
    gbt = ml_gate_bias[0].reshape(ML_GATE_COLS, 1)
    proj, gates_t = _inproj(xp, xs, mods3, gn, w_qkvo, w_hy, wg.T, gbt)

    gain = ml_head_gain[0].reshape(1, ML_WIDTH)
    y_ml_p, c_new, n_new, m_new = _mlstm(proj, gates_t, gain, None, SEQ, BATCH, 0)
    state = (state_C[:, 0], state_n[:, 0], state_m[:, 0].reshape(DEC_BATCH, 2 * ML_HEADS, 1))
    y_ml_s, _, _, _ = _mlstm(proj, gates_t, gain, state, DEC_SEQ, DEC_BATCH, T_PROMPT // DEC_SEQ)

    w1p = jnp.pad(hy_f_w1[0], ((0, 128 - HY_EMB), (0, 0)))
    b1 = hy_f_b1[0].reshape(1, -1)
    b2 = hy_f_b2[0].reshape(1, -1)
    b3 = hy_f_b3[0].reshape(1, -1)
    dec = hy_decay[0].reshape(1, -1)
    z_parts = []
    for seq_len, n_seq, row_off, width, seqs in ((SEQ, BATCH, 0, SEQ, 4), (DEC_SEQ, DEC_BATCH, T_PROMPT, GRID_W, 2)):
        f, ft = _dft_mats(seq_len)
        coefs = _hyena_filters(seq_len, f, w1p, b1, hy_f_w2[0], b2, hy_f_w3[0], b3, dec)
        z_parts.append(_hyena(proj, hy_conv_w[0], coefs, hy_bias[0], f, ft, seq_len, n_seq, row_off, width, seqs))
    z_p, z_s = z_parts

    pad_r = ROUTER_ROWS - N_GROUPS - N_EXPERTS
    w_r = jnp.pad(jnp.concatenate([w_rc[0], w_rf[0]], axis=1).T, ((0, pad_r), (0, 0)))
    b_r = jnp.pad(jnp.concatenate([b_rc[0], b_rf[0]], axis=0), (0, pad_r)).reshape(ROUTER_ROWS, 1)
    x1, h2ext, bid = _outproj(xp, xs, y_ml_p, y_ml_s, z_p, z_s, mods3, gn, w_out[0].astype(BF16), w_r, b_r)

    pos3, meta = _route(bid)
    pos = pos3.reshape(T_ALL)
    y_sorted = _moe(meta, pos, h2ext, w_gate[0], w_up[0], w_down[0])
    y_p, y_s = _final(pos, y_sorted, x1, mods3, gn)

    new_c = c_new.reshape(BATCH, 1, 2, ML_HEADS, ML_HEAD_DIM, ML_HEAD_DIM)
    new_n = n_new.reshape(BATCH, 1, 2, ML_HEADS, ML_HEAD_DIM)
    new_m = m_new[:, :, 0].reshape(BATCH, 1, 2, ML_HEADS)
    return (y_p.reshape(BATCH, SEQ, D_MODEL), y_s.reshape(DEC_BATCH, DEC_SEQ, D_MODEL), new_c, new_n, new_m)
```

```python
import functools
import math

import jax
import jax.numpy as jnp
import numpy as np
from jax import lax
from jax.experimental import pallas as pl
from jax.experimental.pallas import tpu as pltpu

F32 = jnp.float32
BF16 = jnp.bfloat16

D_MODEL = 1024
BATCH = 16
SEQ = 256
DEC_BATCH = 4
DEC_SEQ = 1024
GRID_W = 64
ML_WIDTH = 512
ML_HEADS = 4
ML_HEAD_DIM = 128
HY_WIDTH = 512
HY_ORDER = 2
HY_EMB = 33
HY_BANDS = 16
HY_FILTER_HIDDEN = 64
HY_MOD_SHIFT = 0.05
N_GROUPS = 4
EXPERTS_PER_GROUP = 4
N_EXPERTS = 16
EXPERT_FF = 512
N_MOD = 6
EPS = 1e-6
ML_QKVO_COLS = 4 * ML_WIDTH
ML_GATE_COLS = 4 * ML_HEADS
HY_COLS = 3 * HY_WIDTH
MAIN_COLS = ML_QKVO_COLS + HY_COLS

T_PROMPT = BATCH * SEQ
T_SAMPLE = DEC_BATCH * DEC_SEQ
T_ALL = T_PROMPT + T_SAMPLE
TILE = 256
N_TILES_P = T_PROMPT // TILE
N_TILES = T_ALL // TILE
MOD_ROWS = 8
K_SCALE = ML_HEAD_DIM ** -0.5
VMEM_LIMIT = 56 * 1024 * 1024


def _cparams(sem):
    return pltpu.CompilerParams(dimension_semantics=sem, vmem_limit_bytes=VMEM_LIMIT)


def _split2(x):
    hi = x.astype(BF16)
    lo = (x - hi.astype(F32)).astype(BF16)
    return hi, lo


def _dot(a, b):
    return jnp.dot(a, b, preferred_element_type=F32)


def _dot_nt(a, b):
    return lax.dot_general(a, b, (((1,), (1,)), ((), ())), preferred_element_type=F32)


def _dot3(a, b):
    ah, al = _split2(a)
    bh, bl = _split2(b)
    return _dot(ah, bh) + _dot(al, bh) + _dot(ah, bl)


def _dot_exact_rhs(x, t):
    x1 = x.astype(BF16)
    r1 = x - x1.astype(F32)
    x2 = r1.astype(BF16)
    x3 = (r1 - x2.astype(F32)).astype(BF16)
    return _dot(x1, t) + _dot(x2, t) + _dot(x3, t)


def _rms(x, g):
    return x * lax.rsqrt(jnp.mean(x * x, axis=-1, keepdims=True) + EPS) * g


def _mod_row_of_tile(i, tiles_per_sample_seq, n_prompt_tiles):
    return jnp.where(i < n_prompt_tiles, 0, 1 + (i - n_prompt_tiles) // tiles_per_sample_seq)


def _ada_kernel(cv_ref, w_ref, b_ref, o_ref):
    cv = cv_ref[...]
    s = cv * jax.nn.sigmoid(cv)
    sh, sl = _split2(s)
    wh, wl = _split2(w_ref[...])
    both = _dot(jnp.concatenate([sh.astype(F32), sl.astype(F32)], axis=0).astype(BF16), wh)
    o_ref[...] = both[0:MOD_ROWS] + both[MOD_ROWS:] + _dot(sh, wl) + b_ref[...]


def _ada(cv, w_ada, b_ada):
    n = N_MOD * D_MODEL
    return pl.pallas_call(
        _ada_kernel,
        out_shape=jax.ShapeDtypeStruct((MOD_ROWS, n), F32),
        grid=(N_MOD,),
        in_specs=[pl.BlockSpec((MOD_ROWS, D_MODEL), lambda j: (0, 0)),
                  pl.BlockSpec((D_MODEL, D_MODEL), lambda j: (0, j)),
                  pl.BlockSpec((1, D_MODEL), lambda j: (0, j))],
        out_specs=pl.BlockSpec((MOD_ROWS, D_MODEL), lambda j: (0, j)),
        compiler_params=_cparams(("arbitrary",)),
        name="ada_mod",
    )(cv, w_ada, b_ada.reshape(1, n))


PREP_COLS = 512


def _prep_q_kernel(wt_ref, o_ref):
    o_ref[...] = wt_ref[...].T.astype(BF16)


def _prep_hy_kernel(wt_hbm, o_ref, buf, sem):
    start = pl.multiple_of(ML_QKVO_COLS + ML_GATE_COLS + pl.program_id(0) * PREP_COLS, 8)
    copy = pltpu.make_async_copy(wt_hbm.at[pl.ds(start, PREP_COLS), :], buf, sem)
    copy.start()
    copy.wait()
    o_ref[...] = buf[...].T.astype(BF16)


def _prep_in_weights(w_in_t):
    out_blk = pl.BlockSpec((D_MODEL, PREP_COLS), lambda j: (0, j))
    w_qkvo = pl.pallas_call(
        _prep_q_kernel,
        out_shape=jax.ShapeDtypeStruct((D_MODEL, ML_QKVO_COLS), BF16),
        grid=(ML_QKVO_COLS // PREP_COLS,),
        in_specs=[pl.BlockSpec((PREP_COLS, D_MODEL), lambda j: (j, 0))], out_specs=out_blk,
        compiler_params=_cparams(("arbitrary",)), name="prep_w_qkvo",
    )(w_in_t)
    w_hy = pl.pallas_call(
        _prep_hy_kernel,
        out_shape=jax.ShapeDtypeStruct((D_MODEL, HY_COLS), BF16),
        grid=(HY_COLS // PREP_COLS,),
        in_specs=[pl.BlockSpec(memory_space=pl.ANY)], out_specs=out_blk,
        scratch_shapes=[pltpu.VMEM((PREP_COLS, D_MODEL), F32), pltpu.SemaphoreType.DMA(())],
        compiler_params=_cparams(("arbitrary",)), name="prep_w_hy",
    )(w_in_t)
    return w_qkvo, w_hy


def _log_sigmoid(x):
    return jnp.minimum(x, 0.0) - jnp.log1p(jnp.exp(-jnp.abs(x)))


def _rows_to_cols(rows):
    ri = lax.broadcasted_iota(jnp.int32, (TILE, TILE), 0)
    ci = lax.broadcasted_iota(jnp.int32, (TILE, TILE), 1)
    eye = jnp.where(ri == ci, 1.0, 0.0).astype(BF16)
    p1 = rows.astype(BF16)
    r1 = rows - p1.astype(F32)
    p2 = r1.astype(BF16)
    p3 = (r1 - p2.astype(F32)).astype(BF16)
    return _dot_nt(eye, p1) + _dot_nt(eye, p2) + _dot_nt(eye, p3)


BIG_TILE = 4 * TILE
N_BIG_P = T_PROMPT // BIG_TILE
N_BIG = T_ALL // BIG_TILE


def _inproj_kernel(xp_ref, xs_ref, m_ref, gn_ref, wq_ref, wh_ref, wgt_ref, gbt_ref, proj_ref, gatet_ref):
    is_p = pl.program_id(0) < N_BIG_P
    halves = [slice(r * TILE, (r + 1) * TILE) for r in range(BIG_TILE // TILE)]
    hs = [_rms(jnp.where(is_p, xp_ref[rows, :], xs_ref[rows, :]), gn_ref[0:1, :]) * (1.0 + m_ref[0, 1:2, :])
          + m_ref[0, 0:1, :] for rows in halves]
    hbs = [h.astype(BF16) for h in hs]
    cb = 512
    for j in range(ML_QKVO_COLS // cb):
        for rows, hb in zip(halves, hbs):
            proj_ref[rows, j * cb:(j + 1) * cb] = _dot(hb, wq_ref[:, j * cb:(j + 1) * cb]).astype(BF16)
    for j in range(HY_COLS // cb):
        lo = ML_QKVO_COLS + j * cb
        for rows, hb in zip(halves, hbs):
            proj_ref[rows, lo:lo + cb] = _dot(hb, wh_ref[:, j * cb:(j + 1) * cb]).astype(BF16)
    wth, wtl = _split2(wgt_ref[...])
    gts = []
    for h, hb in zip(hs, hbs):
        hl = (h - hb.astype(F32)).astype(BF16)
        gt = _dot_nt(wth, hb) + _dot_nt(wth, hl) + _dot_nt(wtl, hb) + gbt_ref[...]
        row = lax.broadcasted_iota(jnp.int32, gt.shape, 0)
        gts.append(jnp.where((row % 8) >= 4, _log_sigmoid(gt), gt))
    for r, gt in enumerate(gts):
        gatet_ref[r] = gt


def _inproj(xp, xs, mods3, g_norm, w_qkvo, w_hy, wgt, gbt):
    tps = DEC_SEQ // BIG_TILE
    per = BIG_TILE // TILE
    return pl.pallas_call(
        _inproj_kernel,
        out_shape=(jax.ShapeDtypeStruct((T_ALL, MAIN_COLS), BF16),
                   jax.ShapeDtypeStruct((N_TILES, ML_GATE_COLS, TILE), F32)),
        grid=(N_BIG,),
        in_specs=[pl.BlockSpec((BIG_TILE, D_MODEL), lambda i: (jnp.minimum(i, N_BIG_P - 1), 0)),
                  pl.BlockSpec((BIG_TILE, D_MODEL), lambda i: (jnp.maximum(i - N_BIG_P, 0), 0)),
                  pl.BlockSpec((1, N_MOD, D_MODEL), lambda i: (_mod_row_of_tile(i, tps, N_BIG_P), 0, 0)),
                  pl.BlockSpec((4, D_MODEL), lambda i: (0, 0)),
                  pl.BlockSpec((D_MODEL, ML_QKVO_COLS), lambda i: (0, 0)),
                  pl.BlockSpec((D_MODEL, HY_COLS), lambda i: (0, 0)),
                  pl.BlockSpec((ML_GATE_COLS, D_MODEL), lambda i: (0, 0)),
                  pl.BlockSpec((ML_GATE_COLS, 1), lambda i: (0, 0))],
        out_specs=(pl.BlockSpec((BIG_TILE, MAIN_COLS), lambda i: (i, 0)),
                   pl.BlockSpec((per, ML_GATE_COLS, TILE), lambda i: (i, 0, 0))),
        compiler_params=_cparams(("arbitrary",)),
        name="in_proj",
    )(xp, xs, mods3, g_norm, w_qkvo, w_hy, wgt, gbt)


ST_ROWS = ML_HEAD_DIM + 16


def _mlstm_kernel(*refs, seq_len, has_state):
    if has_state:
        (q_ref, k_ref, v_ref, o_ref, gt_ref, gain_ref, c0_ref, n0_ref, m0_ref,
         y_ref, c_ref, n_ref, m_ref, vt_ref, hf_ref, hb_ref, st_ref, ms_ref) = refs
    else:
        (q_ref, k_ref, v_ref, o_ref, gt_ref, gain_ref,
         y_ref, c_ref, n_ref, m_ref, vt_ref, hf_ref, hb_ref, st_ref, ms_ref) = refs
    ch = TILE
    nc = seq_len // ch
    hd = ML_HEAD_DIM
    key = lax.broadcasted_iota(jnp.int32, (ch, ch), 0)
    qry = lax.broadcasted_iota(jnp.int32, (ch, ch), 1)
    key_le = key <= qry
    key_ge = key >= qry
    t_le = jnp.where(key_le, 1.0, 0.0).astype(BF16)
    t_ge = jnp.where(key_ge, 1.0, 0.0).astype(BF16)
    sub16 = lax.broadcasted_iota(jnp.int32, (16, ch), 0)
    ln_scale = math.log(K_SCALE)

    for c in range(nc):
        for h in range(ML_HEADS):
            cols = slice(h * hd, (h + 1) * hd)
            vt_ref[c, cols, :] = v_ref[c * ch:(c + 1) * ch, cols].T

    for d in range(2):
        for h in range(ML_HEADS):
            r = d * ML_HEADS + h
            st_ref[r] = jnp.zeros((ST_ROWS, hd), F32)
            if has_state:
                st_ref[r, 0:hd, :] = c0_ref[0, d, h].T
                st_ref[r, hd:hd + 1, :] = n0_ref[0, d, h:h + 1, :]
                ms_ref[r] = jnp.broadcast_to(m0_ref[0, r:r + 1, :], (1, ch))
            else:
                ms_ref[r] = jnp.zeros((1, ch), F32)

    def step(t, carry):
        for d in range(2):
            c = t if d == 0 else nc - 1 - t
            rows = pl.ds(pl.multiple_of(c * ch, ch), ch)
            grow = gt_ref[c]
            brow_all = _dot_exact_rhs(grow, t_le if d == 0 else t_ge)
            ccol_all = _rows_to_cols(grow - pltpu.roll(brow_all, ML_GATE_COLS - ML_HEADS, axis=0))
            mask = key_le if d == 0 else key_ge
            hacc_ref = hf_ref if d == 0 else hb_ref
            heads = range(ML_HEADS)
            regs = [d * ML_HEADS + h for h in heads]
            colss = [slice(h * hd, (h + 1) * hd) for h in heads]
            qs = [q_ref[rows, cols] for cols in colss]
            ks = [k_ref[rows, cols] for cols in colss]
            vts = [vt_ref[c, cols, :] for cols in colss]
            sts = [st_ref[r] for r in regs]
            m_prevs = [ms_ref[r] for r in regs]
            b_rows = [brow_all[(1 + 2 * d) * ML_HEADS + h:(1 + 2 * d) * ML_HEADS + h + 1, :] for h in heads]
            ig_rows = [grow[2 * d * ML_HEADS + h:2 * d * ML_HEADS + h + 1, :] for h in heads]
            qks = [_dot_nt(k, q) for k, q in zip(ks, qs)]
            iqs = [_dot_nt(st.astype(BF16), q) for st, q in zip(sts, qs)]
            ss, sc_inters, m_poss = [], [], []
            for h in heads:
                icol = 2 * d * ML_HEADS + h
                c_col = ccol_all[:, icol:icol + 1]
                logd = jnp.where(mask, b_rows[h] + c_col, -jnp.inf)
                inter = b_rows[h] + m_prevs[h]
                m_pos = jnp.maximum(inter, jnp.max(logd, axis=0, keepdims=True))
                ss.append(qks[h] * jnp.exp(logd - (m_pos - ln_scale)))
                sc_inters.append(jnp.exp(inter - m_pos))
                m_poss.append(m_pos)
            pvs = [_dot(vt, s.astype(BF16)) for vt, s in zip(vts, ss)]
            for h in heads:
                num = sc_inters[h] * iqs[h][0:hd] + pvs[h]
                den = sc_inters[h] * iqs[h][hd:hd + 1] + jnp.sum(ss[h], axis=0, keepdims=True)
                hacc_ref[c, colss[h], :] = num * (1.0 / jnp.maximum(jnp.abs(den), jnp.exp(-m_poss[h])))
            lhss, decays = [], []
            for h in heads:
                b_row = b_rows[h]
                b_last = b_row[:, ch - 1:ch] if d == 0 else b_row[:, 0:1]
                logw = b_last - b_row + ig_rows[h]
                m_new = jnp.maximum(b_last + m_prevs[h], jnp.max(logw, axis=1, keepdims=True))
                w = jnp.exp(logw - (m_new - ln_scale))
                decays.append(jnp.exp(b_last + m_prevs[h] - m_new))
                lhss.append(jnp.concatenate([(vts[h].astype(F32) * w).astype(BF16),
                                             jnp.where(sub16 == 0, w, 0.0).astype(BF16)], axis=0))
                ms_ref[regs[h]] = m_new
            upds = [_dot(lhs, k) for lhs, k in zip(lhss, ks)]
            for h in heads:
                st_ref[regs[h]] = decays[h][:, 0:hd] * sts[h] + upds[h]
        return carry

    lax.fori_loop(0, nc, step, 0)

    for d in range(2):
        for h in range(ML_HEADS):
            r = d * ML_HEADS + h
            c_ref[0, d, h] = st_ref[r, 0:hd, :].T
            n_ref[0, d, h:h + 1, :] = st_ref[r, hd:hd + 1, :]
            m_ref[0, r:r + 1, :] = ms_ref[r][:, 0:hd]
    for c in range(nc):
        for h in range(ML_HEADS):
            cols = slice(h * hd, (h + 1) * hd)
            ht = hf_ref[c, cols, :] + hb_ref[c, cols, :]
            ht = ht * lax.rsqrt(jnp.mean(ht * ht, axis=0, keepdims=True) + EPS)
            rows = slice(c * ch, (c + 1) * ch)
            y = ht.T * gain_ref[:, cols] * jax.nn.sigmoid(o_ref[rows, cols].astype(F32))
            y_ref[rows, cols] = y.astype(BF16)


def _mlstm(proj, gates_t, gain, state, seq_len, n_seq, row_block_off):
    has_state = state is not None
    tiles = seq_len // TILE
    off = row_block_off
    qkvo_specs = [pl.BlockSpec((seq_len, ML_WIDTH), functools.partial(lambda b, j: (off + b, j), j=j))
                  for j in range(4)]
    in_specs = qkvo_specs + [
        pl.BlockSpec((tiles, ML_GATE_COLS, TILE), lambda b: (off + b, 0, 0)),
        pl.BlockSpec((1, ML_WIDTH), lambda b: (0, 0)),
    ]
    args = [proj, proj, proj, proj, gates_t, gain]
    if has_state:
        c0, n0, m0 = state
        in_specs += [
            pl.BlockSpec((1, 2, ML_HEADS, ML_HEAD_DIM, ML_HEAD_DIM), lambda b: (b, 0, 0, 0, 0)),
            pl.BlockSpec((1, 2, ML_HEADS, ML_HEAD_DIM), lambda b: (b, 0, 0, 0)),
            pl.BlockSpec((1, 2 * ML_HEADS, 1), lambda b: (b, 0, 0)),
        ]
        args += [c0, n0, m0]
    out_shape = (jax.ShapeDtypeStruct((n_seq * seq_len, ML_WIDTH), BF16),
                 jax.ShapeDtypeStruct((n_seq, 2, ML_HEADS, ML_HEAD_DIM, ML_HEAD_DIM), F32),
                 jax.ShapeDtypeStruct((n_seq, 2, ML_HEADS, ML_HEAD_DIM), F32),
                 jax.ShapeDtypeStruct((n_seq, 2 * ML_HEADS, ML_HEAD_DIM), F32))
    out_specs = (pl.BlockSpec((seq_len, ML_WIDTH), lambda b: (b, 0)),
                 pl.BlockSpec((1, 2, ML_HEADS, ML_HEAD_DIM, ML_HEAD_DIM), lambda b: (b, 0, 0, 0, 0)),
                 pl.BlockSpec((1, 2, ML_HEADS, ML_HEAD_DIM), lambda b: (b, 0, 0, 0)),
                 pl.BlockSpec((1, 2 * ML_HEADS, ML_HEAD_DIM), lambda b: (b, 0, 0)))
    scratch = [pltpu.VMEM((tiles, ML_WIDTH, TILE), BF16),
               pltpu.VMEM((tiles, ML_WIDTH, TILE), F32), pltpu.VMEM((tiles, ML_WIDTH, TILE), F32),
               pltpu.VMEM((2 * ML_HEADS, ST_ROWS, ML_HEAD_DIM), F32),
               pltpu.VMEM((2 * ML_HEADS, 1, TILE), F32)]
    return pl.pallas_call(
        functools.partial(_mlstm_kernel, seq_len=seq_len, has_state=has_state),
        out_shape=out_shape, grid=(n_seq,), in_specs=in_specs, out_specs=out_specs,
        scratch_shapes=scratch, compiler_params=_cparams(("arbitrary",)),
        name=f"mlstm_{seq_len}",
    )(*args)


def _dft_mats(seq_len):
    k = np.arange(seq_len, dtype=np.int64)[:, None]
    d = np.arange(seq_len, dtype=np.int64)[None, :]
    ang = np.pi * ((k * d) % (2 * seq_len)).astype(np.float64) / seq_len
    sinm = np.sin(ang)
    sinm[0, :] = np.where(d[0] % 2 == 0, 1.0, -1.0)
    f = np.concatenate([np.cos(ang), sinm], axis=0).astype(np.float32)
    return jnp.asarray(f).astype(BF16), jnp.asarray(np.ascontiguousarray(f.T)).astype(BF16)


def _filter_feats(seq_len):
    t = np.linspace(0.0, 1.0, seq_len, dtype=np.float64)[:, None]
    wpos = 2.0 * np.pi * np.arange(seq_len, dtype=np.float64)[:, None] / seq_len
    bands = np.linspace(1e-4, HY_BANDS - 1, HY_BANDS, dtype=np.float64)[None, :]
    z = np.concatenate([t, np.cos(bands * wpos), -np.sin(bands * wpos)], axis=-1)
    return jnp.asarray(np.pad(z, ((0, 0), (0, 128 - HY_EMB))).astype(np.float32))


def _filter_kernel(z_ref, w1_ref, b1_ref, w2_ref, b2_ref, w3_ref, b3_ref, dec_ref, f_ref,
                   a_ref, b_ref, d_ref, *, seq_len):
    n = 2 * seq_len
    oc = 2 * HY_WIDTH
    z = z_ref[...]
    h = jnp.sin(_dot3(z, w1_ref[...]) + b1_ref[...])
    h = jnp.sin(_dot3(h, w2_ref[...]) + b2_ref[...])
    t = z[:, 0:1]
    di = lax.broadcasted_iota(jnp.int32, (seq_len, 1), 0)
    sgn = jnp.where(di % 2 == 0, 1.0, -1.0)
    first = di == 0
    ssums, sdifs = [], []
    for o in range(HY_ORDER):
        cols = slice(o * oc, (o + 1) * oc)
        g = _dot3(h, w3_ref[:, cols]) + b3_ref[:, cols]
        g = g * (jnp.exp(-t * jnp.abs(dec_ref[:, cols])) + HY_MOD_SHIFT)
        ss = jnp.sum(g * g, axis=0, keepdims=True)
        inv = lax.rsqrt(ss[:, :HY_WIDTH] + ss[:, HY_WIDTH:] + EPS)
        hp = g[:, :HY_WIDTH] * inv
        hn = g[:, HY_WIDTH:] * inv
        ssums.append(hp + hn)
        sdifs.append(hp - hn)
    hcs = [_dot(f_ref[0:seq_len, :], s.astype(BF16)) for s in ssums]
    hss = [_dot(f_ref[seq_len:n, :], s.astype(BF16)) for s in sdifs]
    for o in range(HY_ORDER):
        nyq = jnp.sum(ssums[o] * sgn, axis=0, keepdims=True)
        a_ref[o] = hcs[o] * jnp.where(first, 1.0 / n, 2.0 / n)
        b_ref[o] = jnp.where(first, 0.0, hss[o] * (2.0 / n))
        d_ref[o] = jnp.where(first, nyq * (1.0 / n), hcs[o] * (2.0 / n))


def _hyena_filters(seq_len, f, w1p, b1, w2, b2, w3, b3, dec):
    z = _filter_feats(seq_len)
    out = jax.ShapeDtypeStruct((HY_ORDER, seq_len, HY_WIDTH), F32)
    return pl.pallas_call(
        functools.partial(_filter_kernel, seq_len=seq_len),
        out_shape=(out, out, out),
        compiler_params=pltpu.CompilerParams(vmem_limit_bytes=VMEM_LIMIT),
        name=f"hyena_filter_{seq_len}",
    )(z, w1p, b1, w2, b2, w3, b3, dec, f)


def _hyena_kernel(x1_ref, x2_ref, v_ref, cw1_ref, cw2_ref, cwv_ref, a_ref, b_ref, d_ref, bias_ref,
                  f_ref, ft_ref, z_ref, *, seq_len, width, seqs):
    rows = seqs * seq_len
    ti = lax.broadcasted_iota(jnp.int32, (rows, 1), 0)
    has_prev = (ti % width) != 0
    has_next = (ti % width) != (width - 1)

    def short_conv(x_ref, w_ref):
        x = x_ref[...].astype(F32)
        prev = jnp.where(has_prev, pltpu.roll(x, 1, axis=0), 0.0)
        nxt = jnp.where(has_next, pltpu.roll(x, rows - 1, axis=0), 0.0)
        return w_ref[0:1, :] * prev + w_ref[1:2, :] * x + w_ref[2:3, :] * nxt

    gates = (short_conv(x1_ref, cw1_ref), short_conv(x2_ref, cw2_ref))
    v = short_conv(v_ref, cwv_ref)
    sls = [slice(i * seq_len, (i + 1) * seq_len) for i in range(seqs)]
    zs = [v[sl] for sl in sls]
    for o in range(HY_ORDER):
        a, b, dd = a_ref[o], b_ref[o], d_ref[o]
        us = [_dot(f_ref[...], z.astype(BF16)) for z in zs]
        ys = []
        for u in us:
            ut = u[:seq_len]
            ub = u[seq_len:]
            ys.append(((ut * a - ub * b).astype(BF16), (ut * b + ub * dd).astype(BF16)))
        convs = [_dot(ft_ref[:, :seq_len], yt) + _dot(ft_ref[:, seq_len:], yb) for yt, yb in ys]
        zs = [gates[o][sl] * (y + bias_ref[o:o + 1, :] * z) for sl, y, z in zip(sls, convs, zs)]
    for sl, z in zip(sls, zs):
        z_ref[sl, :] = z.astype(BF16)


def _hyena(proj, conv_w, coefs, hy_bias, f, ft, seq_len, n_seq, row_off, width, seqs):
    cb = 256
    nblk = HY_WIDTH // cb
    base = ML_QKVO_COLS // cb
    rows = seqs * seq_len
    off = row_off // rows
    a, b, d = coefs

    def col_spec(part):
        return pl.BlockSpec((rows, cb), lambda j, s: (off + s, base + part * nblk + j))

    def w_spec(part):
        return pl.BlockSpec((3, cb), lambda j, s: (0, part * nblk + j))

    coef_spec = pl.BlockSpec((HY_ORDER, seq_len, cb), lambda j, s: (0, 0, j))
    return pl.pallas_call(
        functools.partial(_hyena_kernel, seq_len=seq_len, width=width, seqs=seqs),
        out_shape=jax.ShapeDtypeStruct((n_seq * seq_len, HY_WIDTH), BF16),
        grid=(nblk, n_seq // seqs),
        in_specs=[col_spec(0), col_spec(1), col_spec(2), w_spec(0), w_spec(1), w_spec(2),
                  coef_spec, coef_spec, coef_spec,
                  pl.BlockSpec((HY_ORDER, cb), lambda j, s: (0, j)),
                  pl.BlockSpec((2 * seq_len, seq_len), lambda j, s: (0, 0)),
                  pl.BlockSpec((seq_len, 2 * seq_len), lambda j, s: (0, 0))],
        out_specs=pl.BlockSpec((rows, cb), lambda j, s: (s, j)),
        compiler_params=_cparams(("arbitrary", "arbitrary")),
        name=f"hyena_conv_{seq_len}",
    )(proj, proj, proj, conv_w, conv_w, conv_w, a, b, d, hy_bias, f, ft)


def _first_max(x, n):
    mx = jnp.max(x, axis=0, keepdims=True)
    row = lax.broadcasted_iota(jnp.int32, x.shape, 0).astype(F32)
    idx = jnp.min(jnp.where(x == mx, row, float(n)), axis=0, keepdims=True)
    return mx, idx.astype(jnp.int32)


ROUTER_ROWS = 32
PAIRS_PER_GROUP = 6
N_BUCKETS = N_GROUPS * PAIRS_PER_GROUP
PAIR_SLOTS = ((0, 1), (0, 2), (0, 3), (1, 3), (1, 2), (3, 2))
LANES = 128
H2_EXT = D_MODEL + LANES
ROW_TILE = 256
ROW_CAP = T_ALL + N_BUCKETS * ROW_TILE
N_ROW_TILES = ROW_CAP // ROW_TILE


def _outproj_kernel(xp_ref, xs_ref, yp_ref, ys_ref, zp_ref, zs_ref, m_ref, gn_ref, wo_ref, wr_ref, br_ref,
                    x1_ref, h2_ref, bid_ref):
    is_p = pl.program_id(0) < N_BIG_P
    wrh, wrl = _split2(wr_ref[...])
    halves = [slice(r * TILE, (r + 1) * TILE) for r in range(BIG_TILE // TILE)]
    ys = [_dot(jnp.where(is_p, yp_ref[rows, :], ys_ref[rows, :]), wo_ref[0:ML_WIDTH, :])
          + _dot(jnp.where(is_p, zp_ref[rows, :], zs_ref[rows, :]), wo_ref[ML_WIDTH:, :]) for rows in halves]
    h2s = []
    for rows, y in zip(halves, ys):
        x = jnp.where(is_p, xp_ref[rows, :], xs_ref[rows, :])
        x1 = x + m_ref[0, 2:3, :] * _rms(y, gn_ref[1:2, :])
        x1_ref[rows, :] = x1
        h2 = _rms(x1, gn_ref[2:3, :]) * (1.0 + m_ref[0, 4:5, :]) + m_ref[0, 3:4, :]
        h2_ref[rows, 0:D_MODEL] = h2
        h2s.append(h2)
    logits = []
    for h2 in h2s:
        h2h, h2l = _split2(h2)
        logits.append(_dot_nt(wrh, h2h) + _dot_nt(wrh, h2l) + _dot_nt(wrl, h2h) + br_ref[...])
    routed = [_route_tile(lg) for lg in logits]
    for r, (rows, (gate_rows, bucket)) in enumerate(zip(halves, routed)):
        h2_ref[rows, D_MODEL:H2_EXT] = jnp.zeros((TILE, LANES), F32)
        h2_ref[rows, D_MODEL:D_MODEL + 8] = _rows_to_cols(gate_rows)
        bid_ref[r] = bucket


def _route_tile(logits):
    lc = logits[0:N_GROUPS]
    mx, gi = _first_max(lc, N_GROUPS)
    p_grp = 1.0 / jnp.sum(jnp.exp(lc - mx), axis=0, keepdims=True)
    lsel = jnp.zeros((EXPERTS_PER_GROUP, TILE), F32)
    for g in range(N_GROUPS):
        lo = N_GROUPS + g * EXPERTS_PER_GROUP
        lsel = jnp.where(gi == g, logits[lo:lo + EXPERTS_PER_GROUP], lsel)
    l1, i1 = _first_max(lsel, EXPERTS_PER_GROUP)
    sub4 = lax.broadcasted_iota(jnp.int32, lsel.shape, 0)
    l2, i2 = _first_max(jnp.where(sub4 == i1, -jnp.inf, lsel), EXPERTS_PER_GROUP)
    e2 = jnp.exp(l2 - l1)
    w1 = p_grp / (1.0 + e2)
    w2 = p_grp * e2 / (1.0 + e2)
    lo_e = jnp.minimum(i1, i2)
    hi_e = jnp.maximum(i1, i2)
    pair = jnp.where(lo_e == 0, hi_e - 1, jnp.where(lo_e == 1, jnp.where(hi_e == 3, 3, 4), 5))
    slot_a = jnp.where(pair == 5, hi_e, lo_e)
    first_in_a = i1 == slot_a
    w_a = jnp.where(first_in_a, w1, w2)
    w_b = jnp.where(first_in_a, w2, w1)
    sub = lax.broadcasted_iota(jnp.int32, (8, TILE), 0)
    gate_rows = jnp.where(sub == 0, w_a, jnp.where(sub == 1, w_b, 0.0))
    return gate_rows, gi * PAIRS_PER_GROUP + pair


def _outproj(xp, xs, yp, ys, zp, zs, mods3, g_norm, w_out, w_r, b_r):
    tps = DEC_SEQ // BIG_TILE
    per = BIG_TILE // TILE
    pidx = lambda i: (jnp.minimum(i, N_BIG_P - 1), 0)
    sidx = lambda i: (jnp.maximum(i - N_BIG_P, 0), 0)
    return pl.pallas_call(
        _outproj_kernel,
        out_shape=(jax.ShapeDtypeStruct((T_ALL, D_MODEL), F32),
                   jax.ShapeDtypeStruct((T_ALL, H2_EXT), F32),
                   jax.ShapeDtypeStruct((N_TILES, 1, TILE), jnp.int32)),
        grid=(N_BIG,),
        in_specs=[pl.BlockSpec((BIG_TILE, D_MODEL), pidx), pl.BlockSpec((BIG_TILE, D_MODEL), sidx),
                  pl.BlockSpec((BIG_TILE, ML_WIDTH), pidx), pl.BlockSpec((BIG_TILE, ML_WIDTH), sidx),
                  pl.BlockSpec((BIG_TILE, HY_WIDTH), pidx), pl.BlockSpec((BIG_TILE, HY_WIDTH), sidx),
                  pl.BlockSpec((1, N_MOD, D_MODEL), lambda i: (_mod_row_of_tile(i, tps, N_BIG_P), 0, 0)),
                  pl.BlockSpec((4, D_MODEL), lambda i: (0, 0)),
                  pl.BlockSpec((D_MODEL, D_MODEL), lambda i: (0, 0)),
                  pl.BlockSpec((ROUTER_ROWS, D_MODEL), lambda i: (0, 0)),
                  pl.BlockSpec((ROUTER_ROWS, 1), lambda i: (0, 0))],
        out_specs=(pl.BlockSpec((BIG_TILE, D_MODEL), lambda i: (i, 0)),
                   pl.BlockSpec((BIG_TILE, H2_EXT), lambda i: (i, 0)),
                   pl.BlockSpec((per, 1, TILE), lambda i: (i, 0, 0))),
        compiler_params=_cparams(("arbitrary",)),
        name="out_proj_router",
    )(xp, xs, yp, ys, zp, zs, mods3, g_norm, w_out, w_r, b_r)


def _route_kernel(bid_ref, pos_ref, meta_ref):
    nb = 32
    tm = float(ROW_TILE)
    sub = lax.broadcasted_iota(jnp.int32, (nb, TILE), 0)
    ri = lax.broadcasted_iota(jnp.int32, (TILE, TILE), 0)
    ci = lax.broadcasted_iota(jnp.int32, (TILE, TILE), 1)
    before = jnp.where(ri < ci, 1.0, 0.0).astype(BF16)

    def onehot(blk):
        return jnp.where(sub == bid_ref[blk], 1.0, 0.0)

    zeros = jnp.zeros((nb, 1), F32)
    cnt = lax.fori_loop(0, N_TILES, lambda blk, c: c + jnp.sum(onehot(blk), axis=1, keepdims=True), zeros)
    padded = jnp.floor((cnt + (tm - 1.0)) * (1.0 / tm)) * tm
    r32 = lax.broadcasted_iota(jnp.int32, (nb, nb), 0)
    c32 = lax.broadcasted_iota(jnp.int32, (nb, nb), 1)
    padded_row = jnp.sum(jnp.where(r32 == c32, padded, 0.0), axis=0, keepdims=True)
    offs = jnp.sum(jnp.where(c32 < r32, padded_row, 0.0), axis=1, keepdims=True)
    ends = offs + padded

    def place(blk, seen):
        oh = onehot(blk)
        rank = _dot(oh.astype(BF16), before)
        pos = jnp.sum(oh * (rank + seen + offs), axis=0, keepdims=True)
        pos_ref[blk] = pos.astype(jnp.int32)
        return seen + jnp.sum(oh, axis=1, keepdims=True)

    lax.fori_loop(0, N_TILES, place, zeros)

    start = lax.broadcasted_iota(jnp.int32, (nb, 128), 1).astype(F32) * tm
    bsub = lax.broadcasted_iota(jnp.int32, (nb, 128), 0)
    done = jnp.where((bsub < N_BUCKETS) & (ends <= start), 1.0, 0.0)
    tb = jnp.sum(done, axis=0, keepdims=True)
    valid = jnp.where(tb < N_BUCKETS, 1.0, 0.0)
    tbc = jnp.minimum(tb, N_BUCKETS - 1.0)
    grp = jnp.floor((tbc + 0.5) * (1.0 / PAIRS_PER_GROUP))
    pair = tbc - PAIRS_PER_GROUP * grp
    loc_a = jnp.zeros_like(pair)
    loc_b = jnp.zeros_like(pair)
    for k, (sa, sb) in enumerate(PAIR_SLOTS):
        loc_a = jnp.where(pair == k, float(sa), loc_a)
        loc_b = jnp.where(pair == k, float(sb), loc_b)
    mine = bsub.astype(F32) == tbc
    used = jnp.sum(jnp.where(mine, offs + cnt, 0.0), axis=0, keepdims=True)
    n_rows = jnp.clip(used - start[0:1], 0.0, tm) * valid
    lane = lax.broadcasted_iota(jnp.int32, (1, 128), 1).astype(F32)
    r128 = lax.broadcasted_iota(jnp.int32, (128, 128), 0)
    c128 = lax.broadcasted_iota(jnp.int32, (128, 128), 1)
    ex_a = grp * EXPERTS_PER_GROUP + loc_a
    ex_b = grp * EXPERTS_PER_GROUP + loc_b
    new_a = jnp.zeros_like(lane)
    new_b = jnp.zeros_like(lane)
    for e in range(N_EXPERTS):
        uses = (valid > 0.0) & ((ex_a == e) | (ex_b == e))
        first = jnp.min(jnp.where(uses, lane, 1e9), axis=1, keepdims=True)
        new_a = jnp.where((ex_a == e) & (lane == first), 1.0, new_a)
        new_b = jnp.where((ex_b == e) & (lane == first) & (ex_a != e), 1.0, new_b)

    def window(ex, new):
        last = jnp.where(new > 0.0, lane, -1.0)
        shift = 1
        while shift < 128:
            moved = pltpu.roll(jnp.broadcast_to(last, (8, 128)), shift, axis=1)[0:1]
            last = jnp.maximum(last, jnp.where(lane >= shift, moved, -1.0))
            shift *= 2
        ex_col = jnp.sum(jnp.where(r128 == c128, ex, 0.0), axis=1, keepdims=True)
        return jnp.sum(jnp.where(r128.astype(F32) == last, ex_col, 0.0), axis=0, keepdims=True)

    rows = (window(ex_a, new_a), window(ex_b, new_b), valid, n_rows, new_a, new_b, loc_a, loc_b)
    row8 = lax.broadcasted_iota(jnp.int32, (8, 128), 0)
    meta = jnp.zeros((8, 128), F32)
    for i, row in enumerate(rows):
        meta = jnp.where(row8 == i, row, meta)
    meta_ref[...] = meta.astype(jnp.int32)


def _route(bid):
    return pl.pallas_call(
        _route_kernel,
        out_shape=(jax.ShapeDtypeStruct((N_TILES, 1, TILE), jnp.int32),
                   jax.ShapeDtypeStruct((8, 128), jnp.int32)),
        compiler_params=pltpu.CompilerParams(vmem_limit_bytes=VMEM_LIMIT),
        name="moe_route",
    )(bid)


def _moe_kernel(meta_ref, pos_ref, h2_hbm, wga_ref, wua_ref, wda_ref, wgb_ref, wub_ref, wdb_ref,
                y_ref, src_ref, xbuf, sem, wg_s, wu_s, wd_s):
    j = pl.program_id(0)

    def row_copy(tile, r, slot):
        tok = src_ref[tile * ROW_TILE + r]
        return pltpu.make_async_copy(h2_hbm.at[pl.ds(tok, 1), :], xbuf.at[slot, pl.ds(r, 1), :], sem.at[slot])

    def issue_rows(tile, slot, lo, hi):
        for r in range(lo, hi):
            row_copy(tile, r, slot).start()

    group = 8

    def row_groups(tile):
        return (meta_ref[3, tile] + (group - 1)) // group

    def issue_counted(tile, slot):
        def body(g, c):
            for k in range(group):
                row_copy(tile, g * group + k, slot).start()
            return c
        lax.fori_loop(0, row_groups(tile), body, 0)

    def wait_counted(tile, slot):
        def body(g, c):
            for k in range(group):
                row_copy(tile, g * group + k, slot).wait()
            return c
        lax.fori_loop(0, row_groups(tile), body, 0)

    def wait_full(slot):
        pltpu.make_async_copy(h2_hbm.at[pl.ds(0, ROW_TILE), :], xbuf.at[slot], sem.at[slot]).wait()

    @pl.when(j == 0)
    def _():
        xbuf[...] = jnp.zeros_like(xbuf)

        def clear(t, c):
            n = meta_ref[3, t]
            for k in range(group - 1):
                src_ref[t * ROW_TILE + jnp.minimum(n + k, ROW_TILE - 1)] = 0
            return c
        lax.fori_loop(0, N_ROW_TILES, clear, 0)

        def invert(t, c):
            src_ref[pos_ref[t]] = t
            return c
        lax.fori_loop(0, T_ALL, invert, 0, unroll=8)

        @pl.when(meta_ref[2, 0] == 1)
        def _():
            issue_counted(0, 0)

    nxt = jnp.minimum(j + 1, N_ROW_TILES - 1)
    has_next = (j + 1 < N_ROW_TILES) & (meta_ref[2, nxt] == 1)
    next_full = has_next & (meta_ref[3, nxt] == ROW_TILE)
    valid = meta_ref[2, j] == 1
    full = meta_ref[3, j] == ROW_TILE

    @pl.when(valid & has_next & jnp.logical_not(next_full))
    def _():
        issue_counted(nxt, nxt % 2)

    @pl.when(valid & full)
    def _():
        wait_full(j % 2)

    @pl.when(valid & jnp.logical_not(full))
    def _():
        wait_counted(j, j % 2)

    loc_a = meta_ref[6, j]
    loc_b = meta_ref[7, j]

    @pl.when(valid & (meta_ref[4, j] == 1))
    def _():
        wg_s[loc_a] = wga_ref[0].astype(BF16)
        wu_s[loc_a] = wua_ref[0].astype(BF16)
        wd_s[loc_a] = wda_ref[0].astype(BF16)

    @pl.when(valid & (meta_ref[5, j] == 1))
    def _():
        wg_s[loc_b] = wgb_ref[0].astype(BF16)
        wu_s[loc_b] = wub_ref[0].astype(BF16)
        wd_s[loc_b] = wdb_ref[0].astype(BF16)

    def compute(fetch_next):
        slot = j % 2
        nslot = nxt % 2
        step = ROW_TILE // 8
        batches = iter(range(0, ROW_TILE, step))

        def fetch():
            if fetch_next:
                lo = next(batches)
                issue_rows(nxt, nslot, lo, lo + step)

        x = xbuf[slot, :, 0:D_MODEL].astype(BF16)
        gates = xbuf[slot, :, D_MODEL:H2_EXT]
        hg_a = _dot(x, wg_s[loc_a])
        fetch()
        hu_a = _dot(x, wu_s[loc_a])
        fetch()
        hg_b = _dot(x, wg_s[loc_b])
        fetch()
        hu_b = _dot(x, wu_s[loc_b])
        fetch()
        act_a = (hg_a * jax.nn.sigmoid(hg_a) * hu_a * gates[:, 0:1]).astype(BF16)
        fetch()
        act_b = (hg_b * jax.nn.sigmoid(hg_b) * hu_b * gates[:, 1:2]).astype(BF16)
        fetch()
        y = _dot(act_a, wd_s[loc_a])
        fetch()
        y = y + _dot(act_b, wd_s[loc_b])
        fetch()
        y_ref[...] = y

    @pl.when(valid & next_full)
    def _():
        compute(True)

    @pl.when(valid & jnp.logical_not(next_full))
    def _():
        compute(False)

    @pl.when(jnp.logical_not(valid))
    def _():
        y_ref[...] = jnp.zeros_like(y_ref)


def _moe(meta, pos, h2ext, w_gate, w_up, w_down):
    up_spec = lambda slot: pl.BlockSpec((1, D_MODEL, EXPERT_FF), lambda j, meta, pos: (meta[slot, j], 0, 0))
    down_spec = lambda slot: pl.BlockSpec((1, EXPERT_FF, D_MODEL), lambda j, meta, pos: (meta[slot, j], 0, 0))
    grid_spec = pltpu.PrefetchScalarGridSpec(
        num_scalar_prefetch=2,
        grid=(N_ROW_TILES,),
        in_specs=[pl.BlockSpec(memory_space=pl.ANY),
                  up_spec(0), up_spec(0), down_spec(0), up_spec(1), up_spec(1), down_spec(1)],
        out_specs=pl.BlockSpec((ROW_TILE, D_MODEL), lambda j, meta, pos: (j, 0)),
        scratch_shapes=[pltpu.SMEM((ROW_CAP,), jnp.int32),
                        pltpu.VMEM((2, ROW_TILE, H2_EXT), F32),
                        pltpu.SemaphoreType.DMA((2,)),
                        pltpu.VMEM((EXPERTS_PER_GROUP, D_MODEL, EXPERT_FF), BF16),
                        pltpu.VMEM((EXPERTS_PER_GROUP, D_MODEL, EXPERT_FF), BF16),
                        pltpu.VMEM((EXPERTS_PER_GROUP, EXPERT_FF, D_MODEL), BF16)])
    return pl.pallas_call(
        _moe_kernel,
        out_shape=jax.ShapeDtypeStruct((ROW_CAP, D_MODEL), F32),
        grid_spec=grid_spec,
        compiler_params=_cparams(("arbitrary",)),
        name="moe_experts",
    )(meta, pos, h2ext, w_gate, w_up, w_down, w_gate, w_up, w_down)


def _final_kernel(pos_ref, y_hbm, x1_ref, m_ref, gn_ref, op_ref, os_ref, ybuf, sem):
    i = pl.program_id(0)

    def row_copy(tile, r, slot):
        p = pos_ref[tile * BIG_TILE + r]
        return pltpu.make_async_copy(y_hbm.at[pl.ds(p, 1), :], ybuf.at[slot, pl.ds(r, 1), :], sem.at[slot])

    def issue(tile, slot):
        def body(r2, c):
            row_copy(tile, 2 * r2, slot).start(priority=0)
            row_copy(tile, 2 * r2 + 1, slot).start(priority=1)
            return c
        lax.fori_loop(0, BIG_TILE // 2, body, 0, unroll=4)

    def wait(slot):
        pltpu.make_async_copy(y_hbm.at[pl.ds(0, BIG_TILE)], ybuf.at[slot], sem.at[slot]).wait()

    @pl.when(i == 0)
    def _():
        issue(0, 0)

    @pl.when(i + 1 < N_BIG)
    def _():
        issue(i + 1, (i + 1) % 2)

    slot = i % 2
    wait(slot)
    out = x1_ref[...] + m_ref[0, 5:6, :] * _rms(ybuf[slot], gn_ref[3:4, :])

    @pl.when(i < N_BIG_P)
    def _():
        op_ref[...] = out

    @pl.when(i >= N_BIG_P)
    def _():
        os_ref[...] = out


def _final(pos, y_sorted, x1, mods3, g_norm):
    tps = DEC_SEQ // BIG_TILE
    grid_spec = pltpu.PrefetchScalarGridSpec(
        num_scalar_prefetch=1,
        grid=(N_BIG,),
        in_specs=[pl.BlockSpec(memory_space=pl.ANY),
                  pl.BlockSpec((BIG_TILE, D_MODEL), lambda i, pos: (i, 0)),
                  pl.BlockSpec((1, N_MOD, D_MODEL), lambda i, pos: (_mod_row_of_tile(i, tps, N_BIG_P), 0, 0)),
                  pl.BlockSpec((4, D_MODEL), lambda i, pos: (0, 0))],
        out_specs=(pl.BlockSpec((BIG_TILE, D_MODEL), lambda i, pos: (jnp.minimum(i, N_BIG_P - 1), 0)),
                   pl.BlockSpec((BIG_TILE, D_MODEL), lambda i, pos: (jnp.maximum(i - N_BIG_P, 0), 0))),
        scratch_shapes=[pltpu.VMEM((2, BIG_TILE, D_MODEL), F32), pltpu.SemaphoreType.DMA((2,))])
    return pl.pallas_call(
        _final_kernel,
        out_shape=(jax.ShapeDtypeStruct((T_PROMPT, D_MODEL), F32),
                   jax.ShapeDtypeStruct((T_SAMPLE, D_MODEL), F32)),
        grid_spec=grid_spec,
        compiler_params=_cparams(("arbitrary",)),
        name="moe_combine_final",
    )(pos, y_sorted, x1, mods3, g_norm)


def kernel(x_prompt, x_sample, state_C, state_n, state_m, c, c_ctx, w_ada, b_ada, g_norm, w_in, ml_gate_bias, ml_head_gain, hy_conv_w, hy_f_w1, hy_f_b1, hy_f_w2, hy_f_b2, hy_f_w3, hy_f_b3, hy_decay, hy_bias, w_out, w_rc, b_rc, w_rf, b_rf, w_gate, w_up, w_down):
    xp = x_prompt.reshape(T_PROMPT, D_MODEL)
    xs = x_sample.reshape(T_SAMPLE, D_MODEL)
    gn = g_norm[0]

    cv = jnp.concatenate([c_ctx[None, :], c, jnp.zeros((MOD_ROWS - 1 - DEC_BATCH, D_MODEL), F32)], axis=0)
    mods3 = _ada(cv, w_ada[0], b_ada[0]).reshape(MOD_ROWS, N_MOD, D_MODEL)

    w_in0 = w_in[0]
    w_qkvo, w_hy = _prep_in_weights(w_in0.T)
    wg = w_in0[:, ML_QKVO_COLS:ML_QKVO_COLS + ML_GATE_COLS]
    gbt = ml_gate_bias[0].reshape(ML_GATE_COLS, 1)
    proj, gates_t = _inproj(xp, xs, mods3, gn, w_qkvo, w_hy, wg.T, gbt)

    gain = ml_head_gain[0].reshape(1, ML_WIDTH)
    y_ml_p, c_new, n_new, m_new = _mlstm(proj, gates_t, gain, None, SEQ, BATCH, 0)
    state = (state_C[:, 0], state_n[:, 0], state_m[:, 0].reshape(DEC_BATCH, 2 * ML_HEADS, 1))
    y_ml_s, _, _, _ = _mlstm(proj, gates_t, gain, state, DEC_SEQ, DEC_BATCH, T_PROMPT // DEC_SEQ)

    w1p = jnp.pad(hy_f_w1[0], ((0, 128 - HY_EMB), (0, 0)))
    b1 = hy_f_b1[0].reshape(1, -1)
    b2 = hy_f_b2[0].reshape(1, -1)
    b3 = hy_f_b3[0].reshape(1, -1)
    dec = hy_decay[0].reshape(1, -1)
    z_parts = []
    for seq_len, n_seq, row_off, width, seqs in ((SEQ, BATCH, 0, SEQ, 4), (DEC_SEQ, DEC_BATCH, T_PROMPT, GRID_W, 2)):
        f, ft = _dft_mats(seq_len)
        coefs = _hyena_filters(seq_len, f, w1p, b1, hy_f_w2[0], b2, hy_f_w3[0], b3, dec)
        z_parts.append(_hyena(proj, hy_conv_w[0], coefs, hy_bias[0], f, ft, seq_len, n_seq, row_off, width, seqs))
    z_p, z_s = z_parts

    pad_r = ROUTER_ROWS - N_GROUPS - N_EXPERTS
    w_r = jnp.pad(jnp.concatenate([w_rc[0], w_rf[0]], axis=1).T, ((0, pad_r), (0, 0)))
    b_r = jnp.pad(jnp.concatenate([b_rc[0], b_rf[0]], axis=0), (0, pad_r)).reshape(ROUTER_ROWS, 1)
    x1, h2ext, bid = _outproj(xp, xs, y_ml_p, y_ml_s, z_p, z_s, mods3, gn, w_out[0].astype(BF16), w_r, b_r)

    pos3, meta = _route(bid)
    pos = pos3.reshape(T_ALL)
    y_sorted = _moe(meta, pos, h2ext, w_gate[0], w_up[0], w_down[0])
    y_p, y_s = _final(pos, y_sorted, x1, mods3, gn)

    new_c = c_new.reshape(BATCH, 1, 2, ML_HEADS, ML_HEAD_DIM, ML_HEAD_DIM)
    new_n = n_new.reshape(BATCH, 1, 2, ML_HEADS, ML_HEAD_DIM)
    new_m = m_new[:, :, 0].reshape(BATCH, 1, 2, ML_HEADS)
    return (y_p.reshape(BATCH, SEQ, D_MODEL), y_s.reshape(DEC_BATCH, DEC_SEQ, D_MODEL), new_c, new_n, new_m)
```

```python
import functools
import math

import jax
import jax.numpy as jnp
import numpy as np
from jax import lax
from jax.experimental import pallas as pl
from jax.experimental.pallas import tpu as pltpu

F32 = jnp.float32
BF16 = jnp.bfloat16

D_MODEL = 1024
BATCH = 16
SEQ = 256
DEC_BATCH = 4
DEC_SEQ = 1024
GRID_W = 64
ML_WIDTH = 512
ML_HEADS = 4
ML_HEAD_DIM = 128
HY_WIDTH = 512
HY_ORDER = 2
HY_EMB = 33
HY_BANDS = 16
HY_FILTER_HIDDEN = 64
HY_MOD_SHIFT = 0.05
N_GROUPS = 4
EXPERTS_PER_GROUP = 4
N_EXPERTS = 16
EXPERT_FF = 512
N_MOD = 6
EPS = 1e-6
ML_QKVO_COLS = 4 * ML_WIDTH
ML_GATE_COLS = 4 * ML_HEADS
HY_COLS = 3 * HY_WIDTH
MAIN_COLS = ML_QKVO_COLS + HY_COLS

T_PROMPT = BATCH * SEQ
T_SAMPLE = DEC_BATCH * DEC_SEQ
T_ALL = T_PROMPT + T_SAMPLE
TILE = 256
N_TILES_P = T_PROMPT // TILE
N_TILES = T_ALL // TILE
MOD_ROWS = 8
K_SCALE = ML_HEAD_DIM ** -0.5
VMEM_LIMIT = 56 * 1024 * 1024


def _cparams(sem):
    return pltpu.CompilerParams(dimension_semantics=sem, vmem_limit_bytes=VMEM_LIMIT)


def _split2(x):
    hi = x.astype(BF16)
    lo = (x - hi.astype(F32)).astype(BF16)
    return hi, lo


def _dot(a, b):
    return jnp.dot(a, b, preferred_element_type=F32)


def _dot_nt(a, b):
    return lax.dot_general(a, b, (((1,), (1,)), ((), ())), preferred_element_type=F32)


def _dot3(a, b):
    ah, al = _split2(a)
    bh, bl = _split2(b)
    return _dot(ah, bh) + _dot(al, bh) + _dot(ah, bl)


def _dot_exact_rhs(x, t):
    x1 = x.astype(BF16)
    r1 = x - x1.astype(F32)
    x2 = r1.astype(BF16)
    x3 = (r1 - x2.astype(F32)).astype(BF16)
    return _dot(x1, t) + _dot(x2, t) + _dot(x3, t)


def _rms(x, g):
    return x * lax.rsqrt(jnp.mean(x * x, axis=-1, keepdims=True) + EPS) * g


def _mod_row_of_tile(i, tiles_per_sample_seq, n_prompt_tiles):
    return jnp.where(i < n_prompt_tiles, 0, 1 + (i - n_prompt_tiles) // tiles_per_sample_seq)


def _ada_kernel(cv_ref, w_ref, b_ref, o_ref):
    cv = cv_ref[...]
    s = cv * jax.nn.sigmoid(cv)
    sh, sl = _split2(s)
    wh, wl = _split2(w_ref[...])
    both = _dot(jnp.concatenate([sh.astype(F32), sl.astype(F32)], axis=0).astype(BF16), wh)
    o_ref[...] = both[0:MOD_ROWS] + both[MOD_ROWS:] + _dot(sh, wl) + b_ref[...]


def _ada(cv, w_ada, b_ada):
    n = N_MOD * D_MODEL
    return pl.pallas_call(
        _ada_kernel,
        out_shape=jax.ShapeDtypeStruct((MOD_ROWS, n), F32),
        grid=(N_MOD,),
        in_specs=[pl.BlockSpec((MOD_ROWS, D_MODEL), lambda j: (0, 0)),
                  pl.BlockSpec((D_MODEL, D_MODEL), lambda j: (0, j)),
                  pl.BlockSpec((1, D_MODEL), lambda j: (0, j))],
        out_specs=pl.BlockSpec((MOD_ROWS, D_MODEL), lambda j: (0, j)),
        compiler_params=_cparams(("arbitrary",)),
        name="ada_mod",
    )(cv, w_ada, b_ada.reshape(1, n))


PREP_COLS = 512


def _prep_q_kernel(wt_ref, o_ref):
    o_ref[...] = wt_ref[...].T.astype(BF16)


def _prep_hy_kernel(wt_hbm, o_ref, buf, sem):
    start = pl.multiple_of(ML_QKVO_COLS + ML_GATE_COLS + pl.program_id(0) * PREP_COLS, 8)
    copy = pltpu.make_async_copy(wt_hbm.at[pl.ds(start, PREP_COLS), :], buf, sem)
    copy.start()
    copy.wait()
    o_ref[...] = buf[...].T.astype(BF16)


def _prep_in_weights(w_in_t):
    out_blk = pl.BlockSpec((D_MODEL, PREP_COLS), lambda j: (0, j))
    w_qkvo = pl.pallas_call(
        _prep_q_kernel,
        out_shape=jax.ShapeDtypeStruct((D_MODEL, ML_QKVO_COLS), BF16),
        grid=(ML_QKVO_COLS // PREP_COLS,),
        in_specs=[pl.BlockSpec((PREP_COLS, D_MODEL), lambda j: (j, 0))], out_specs=out_blk,
        compiler_params=_cparams(("arbitrary",)), name="prep_w_qkvo",
    )(w_in_t)
    w_hy = pl.pallas_call(
        _prep_hy_kernel,
        out_shape=jax.ShapeDtypeStruct((D_MODEL, HY_COLS), BF16),
        grid=(HY_COLS // PREP_COLS,),
        in_specs=[pl.BlockSpec(memory_space=pl.ANY)], out_specs=out_blk,
        scratch_shapes=[pltpu.VMEM((PREP_COLS, D_MODEL), F32), pltpu.SemaphoreType.DMA(())],
        compiler_params=_cparams(("arbitrary",)), name="prep_w_hy",
    )(w_in_t)
    return w_qkvo, w_hy


def _log_sigmoid(x):
    return jnp.minimum(x, 0.0) - jnp.log1p(jnp.exp(-jnp.abs(x)))


def _rows_to_cols(rows):
    ri = lax.broadcasted_iota(jnp.int32, (TILE, TILE), 0)
    ci = lax.broadcasted_iota(jnp.int32, (TILE, TILE), 1)
    eye = jnp.where(ri == ci, 1.0, 0.0).astype(BF16)
    p1 = rows.astype(BF16)
    r1 = rows - p1.astype(F32)
    p2 = r1.astype(BF16)
    p3 = (r1 - p2.astype(F32)).astype(BF16)
    return _dot_nt(eye, p1) + _dot_nt(eye, p2) + _dot_nt(eye, p3)


BIG_TILE = 4 * TILE
N_BIG_P = T_PROMPT // BIG_TILE
N_BIG = T_ALL // BIG_TILE


def _inproj_kernel(xp_ref, xs_ref, m_ref, gn_ref, wq_ref, wh_ref, wgt_ref, gbt_ref, proj_ref, gatet_ref):
    is_p = pl.program_id(0) < N_BIG_P
    halves = [slice(r * TILE, (r + 1) * TILE) for r in range(BIG_TILE // TILE)]
    hs = [_rms(jnp.where(is_p, xp_ref[rows, :], xs_ref[rows, :]), gn_ref[0:1, :]) * (1.0 + m_ref[0, 1:2, :])
          + m_ref[0, 0:1, :] for rows in halves]
    hbs = [h.astype(BF16) for h in hs]
    cb = 512
    for j in range(ML_QKVO_COLS // cb):
        for rows, hb in zip(halves, hbs):
            proj_ref[rows, j * cb:(j + 1) * cb] = _dot(hb, wq_ref[:, j * cb:(j + 1) * cb]).astype(BF16)
    for j in range(HY_COLS // cb):
        lo = ML_QKVO_COLS + j * cb
        for rows, hb in zip(halves, hbs):
            proj_ref[rows, lo:lo + cb] = _dot(hb, wh_ref[:, j * cb:(j + 1) * cb]).astype(BF16)
    wth, wtl = _split2(wgt_ref[...])
    gts = []
    for h, hb in zip(hs, hbs):
        hl = (h - hb.astype(F32)).astype(BF16)
        gt = _dot_nt(wth, hb) + _dot_nt(wth, hl) + _dot_nt(wtl, hb) + gbt_ref[...]
        row = lax.broadcasted_iota(jnp.int32, gt.shape, 0)
        gts.append(jnp.where((row % 8) >= 4, _log_sigmoid(gt), gt))
    for r, gt in enumerate(gts):
        gatet_ref[r] = gt


def _inproj(xp, xs, mods3, g_norm, w_qkvo, w_hy, wgt, gbt):
    tps = DEC_SEQ // BIG_TILE
    per = BIG_TILE // TILE
    return pl.pallas_call(
        _inproj_kernel,
        out_shape=(jax.ShapeDtypeStruct((T_ALL, MAIN_COLS), BF16),
                   jax.ShapeDtypeStruct((N_TILES, ML_GATE_COLS, TILE), F32)),
        grid=(N_BIG,),
        in_specs=[pl.BlockSpec((BIG_TILE, D_MODEL), lambda i: (jnp.minimum(i, N_BIG_P - 1), 0)),
                  pl.BlockSpec((BIG_TILE, D_MODEL), lambda i: (jnp.maximum(i - N_BIG_P, 0), 0)),
                  pl.BlockSpec((1, N_MOD, D_MODEL), lambda i: (_mod_row_of_tile(i, tps, N_BIG_P), 0, 0)),
                  pl.BlockSpec((4, D_MODEL), lambda i: (0, 0)),
                  pl.BlockSpec((D_MODEL, ML_QKVO_COLS), lambda i: (0, 0)),
                  pl.BlockSpec((D_MODEL, HY_COLS), lambda i: (0, 0)),
                  pl.BlockSpec((ML_GATE_COLS, D_MODEL), lambda i: (0, 0)),
                  pl.BlockSpec((ML_GATE_COLS, 1), lambda i: (0, 0))],
        out_specs=(pl.BlockSpec((BIG_TILE, MAIN_COLS), lambda i: (i, 0)),
                   pl.BlockSpec((per, ML_GATE_COLS, TILE), lambda i: (i, 0, 0))),
        compiler_params=_cparams(("arbitrary",)),
        name="in_proj",
    )(xp, xs, mods3, g_norm, w_qkvo, w_hy, wgt, gbt)


ST_ROWS = ML_HEAD_DIM + 16


def _mlstm_kernel(*refs, seq_len, has_state):
    if has_state:
        (q_ref, k_ref, v_ref, o_ref, gt_ref, gain_ref, c0_ref, n0_ref, m0_ref,
         y_ref, c_ref, n_ref, m_ref, vt_ref, hf_ref, hb_ref, st_ref, ms_ref) = refs
    else:
        (q_ref, k_ref, v_ref, o_ref, gt_ref, gain_ref,
         y_ref, c_ref, n_ref, m_ref, vt_ref, hf_ref, hb_ref, st_ref, ms_ref) = refs
    ch = TILE
    nc = seq_len // ch
    hd = ML_HEAD_DIM
    key = lax.broadcasted_iota(jnp.int32, (ch, ch), 0)
    qry = lax.broadcasted_iota(jnp.int32, (ch, ch), 1)
    key_le = key <= qry
    key_ge = key >= qry
    t_le = jnp.where(key_le, 1.0, 0.0).astype(BF16)
    t_ge = jnp.where(key_ge, 1.0, 0.0).astype(BF16)
    sub16 = lax.broadcasted_iota(jnp.int32, (16, ch), 0)
    ln_scale = math.log(K_SCALE)

    for c in range(nc):
        for h in range(ML_HEADS):
            cols = slice(h * hd, (h + 1) * hd)
            vt_ref[c, cols, :] = v_ref[c * ch:(c + 1) * ch, cols].T

    for d in range(2):
        for h in range(ML_HEADS):
            r = d * ML_HEADS + h
            st_ref[r] = jnp.zeros((ST_ROWS, hd), F32)
            if has_state:
                st_ref[r, 0:hd, :] = c0_ref[0, d, h].T
                st_ref[r, hd:hd + 1, :] = n0_ref[0, d, h:h + 1, :]
                ms_ref[r] = jnp.broadcast_to(m0_ref[0, r:r + 1, :], (1, ch))
            else:
                ms_ref[r] = jnp.zeros((1, ch), F32)

    def step(t, carry):
        chains = [(d, h) for d in range(2) for h in range(ML_HEADS)]
        n = range(len(chains))
        cs = [t, nc - 1 - t]
        rowss = [pl.ds(pl.multiple_of(c * ch, ch), ch) for c in cs]
        grows = [gt_ref[c] for c in cs]
        brows = [_dot_exact_rhs(grows[d], t_le if d == 0 else t_ge) for d in range(2)]
        ccols = [_rows_to_cols(grows[d] - pltpu.roll(brows[d], ML_GATE_COLS - ML_HEADS, axis=0)) for d in range(2)]
        masks = [key_le, key_ge]
        haccs = [hf_ref, hb_ref]
        regs = [d * ML_HEADS + h for d, h in chains]
        colss = [slice(h * hd, (h + 1) * hd) for d, h in chains]
        qs = [q_ref[rowss[d], colss[i]] for i, (d, h) in enumerate(chains)]
        ks = [k_ref[rowss[d], colss[i]] for i, (d, h) in enumerate(chains)]
        vts = [vt_ref[cs[d], colss[i], :] for i, (d, h) in enumerate(chains)]
        sts = [st_ref[r] for r in regs]
        m_prevs = [ms_ref[r] for r in regs]
        b_rows = [brows[d][(1 + 2 * d) * ML_HEADS + h:(1 + 2 * d) * ML_HEADS + h + 1, :] for d, h in chains]
        ig_rows = [grows[d][2 * d * ML_HEADS + h:2 * d * ML_HEADS + h + 1, :] for d, h in chains]
        qks = [_dot_nt(k, q) for k, q in zip(ks, qs)]
        iqs = [_dot_nt(st.astype(BF16), q) for st, q in zip(sts, qs)]
        ss, sc_inters, m_poss = [], [], []
        for i, (d, h) in enumerate(chains):
            icol = 2 * d * ML_HEADS + h
            c_col = ccols[d][:, icol:icol + 1]
            logd = jnp.where(masks[d], b_rows[i] + c_col, -jnp.inf)
            inter = b_rows[i] + m_prevs[i]
            m_pos = jnp.maximum(inter, jnp.max(logd, axis=0, keepdims=True))
            ss.append(qks[i] * jnp.exp(logd - (m_pos - ln_scale)))
            sc_inters.append(jnp.exp(inter - m_pos))
            m_poss.append(m_pos)
        pvs = [_dot(vt, s.astype(BF16)) for vt, s in zip(vts, ss)]
        for i, (d, h) in enumerate(chains):
            num = sc_inters[i] * iqs[i][0:hd] + pvs[i]
            den = sc_inters[i] * iqs[i][hd:hd + 1] + jnp.sum(ss[i], axis=0, keepdims=True)
            haccs[d][cs[d], colss[i], :] = num * (1.0 / jnp.maximum(jnp.abs(den), jnp.exp(-m_poss[i])))
        lhss, decays = [], []
        for i, (d, h) in enumerate(chains):
            b_row = b_rows[i]
            b_last = b_row[:, ch - 1:ch] if d == 0 else b_row[:, 0:1]
            logw = b_last - b_row + ig_rows[i]
            m_new = jnp.maximum(b_last + m_prevs[i], jnp.max(logw, axis=1, keepdims=True))
            w = jnp.exp(logw - (m_new - ln_scale))
            decays.append(jnp.exp(b_last + m_prevs[i] - m_new))
            lhss.append(jnp.concatenate([(vts[i].astype(F32) * w).astype(BF16),
                                         jnp.where(sub16 == 0, w, 0.0).astype(BF16)], axis=0))
            ms_ref[regs[i]] = m_new
        upds = [_dot(lhs, k) for lhs, k in zip(lhss, ks)]
        for i in n:
            st_ref[regs[i]] = decays[i][:, 0:hd] * sts[i] + upds[i]
        return carry

    lax.fori_loop(0, nc, step, 0)

    for d in range(2):
        for h in range(ML_HEADS):
            r = d * ML_HEADS + h
            c_ref[0, d, h] = st_ref[r, 0:hd, :].T
            n_ref[0, d, h:h + 1, :] = st_ref[r, hd:hd + 1, :]
            m_ref[0, r:r + 1, :] = ms_ref[r][:, 0:hd]
    for c in range(nc):
        for h in range(ML_HEADS):
            cols = slice(h * hd, (h + 1) * hd)
            ht = hf_ref[c, cols, :] + hb_ref[c, cols, :]
            ht = ht * lax.rsqrt(jnp.mean(ht * ht, axis=0, keepdims=True) + EPS)
            rows = slice(c * ch, (c + 1) * ch)
            y = ht.T * gain_ref[:, cols] * jax.nn.sigmoid(o_ref[rows, cols].astype(F32))
            y_ref[rows, cols] = y.astype(BF16)


def _mlstm(proj, gates_t, gain, state, seq_len, n_seq, row_block_off):
    has_state = state is not None
    tiles = seq_len // TILE
    off = row_block_off
    qkvo_specs = [pl.BlockSpec((seq_len, ML_WIDTH), functools.partial(lambda b, j: (off + b, j), j=j))
                  for j in range(4)]
    in_specs = qkvo_specs + [
        pl.BlockSpec((tiles, ML_GATE_COLS, TILE), lambda b: (off + b, 0, 0)),
        pl.BlockSpec((1, ML_WIDTH), lambda b: (0, 0)),
    ]
    args = [proj, proj, proj, proj, gates_t, gain]
    if has_state:
        c0, n0, m0 = state
        in_specs += [
            pl.BlockSpec((1, 2, ML_HEADS, ML_HEAD_DIM, ML_HEAD_DIM), lambda b: (b, 0, 0, 0, 0)),
            pl.BlockSpec((1, 2, ML_HEADS, ML_HEAD_DIM), lambda b: (b, 0, 0, 0)),
            pl.BlockSpec((1, 2 * ML_HEADS, 1), lambda b: (b, 0, 0)),
        ]
        args += [c0, n0, m0]
    out_shape = (jax.ShapeDtypeStruct((n_seq * seq_len, ML_WIDTH), BF16),
                 jax.ShapeDtypeStruct((n_seq, 2, ML_HEADS, ML_HEAD_DIM, ML_HEAD_DIM), F32),
                 jax.ShapeDtypeStruct((n_seq, 2, ML_HEADS, ML_HEAD_DIM), F32),
                 jax.ShapeDtypeStruct((n_seq, 2 * ML_HEADS, ML_HEAD_DIM), F32))
    out_specs = (pl.BlockSpec((seq_len, ML_WIDTH), lambda b: (b, 0)),
                 pl.BlockSpec((1, 2, ML_HEADS, ML_HEAD_DIM, ML_HEAD_DIM), lambda b: (b, 0, 0, 0, 0)),
                 pl.BlockSpec((1, 2, ML_HEADS, ML_HEAD_DIM), lambda b: (b, 0, 0, 0)),
                 pl.BlockSpec((1, 2 * ML_HEADS, ML_HEAD_DIM), lambda b: (b, 0, 0)))
    scratch = [pltpu.VMEM((tiles, ML_WIDTH, TILE), BF16),
               pltpu.VMEM((tiles, ML_WIDTH, TILE), F32), pltpu.VMEM((tiles, ML_WIDTH, TILE), F32),
               pltpu.VMEM((2 * ML_HEADS, ST_ROWS, ML_HEAD_DIM), F32),
               pltpu.VMEM((2 * ML_HEADS, 1, TILE), F32)]
    return pl.pallas_call(
        functools.partial(_mlstm_kernel, seq_len=seq_len, has_state=has_state),
        out_shape=out_shape, grid=(n_seq,), in_specs=in_specs, out_specs=out_specs,
        scratch_shapes=scratch, compiler_params=_cparams(("arbitrary",)),
        name=f"mlstm_{seq_len}",
    )(*args)


def _dft_mats(seq_len):
    k = np.arange(seq_len, dtype=np.int64)[:, None]
    d = np.arange(seq_len, dtype=np.int64)[None, :]
    ang = np.pi * ((k * d) % (2 * seq_len)).astype(np.float64) / seq_len
    sinm = np.sin(ang)
    sinm[0, :] = np.where(d[0] % 2 == 0, 1.0, -1.0)
    f = np.concatenate([np.cos(ang), sinm], axis=0).astype(np.float32)
    return jnp.asarray(f).astype(BF16), jnp.asarray(np.ascontiguousarray(f.T)).astype(BF16)


def _filter_feats(seq_len):
    t = np.linspace(0.0, 1.0, seq_len, dtype=np.float64)[:, None]
    wpos = 2.0 * np.pi * np.arange(seq_len, dtype=np.float64)[:, None] / seq_len
    bands = np.linspace(1e-4, HY_BANDS - 1, HY_BANDS, dtype=np.float64)[None, :]
    z = np.concatenate([t, np.cos(bands * wpos), -np.sin(bands * wpos)], axis=-1)
    return jnp.asarray(np.pad(z, ((0, 0), (0, 128 - HY_EMB))).astype(np.float32))


def _filter_kernel(z_ref, w1_ref, b1_ref, w2_ref, b2_ref, w3_ref, b3_ref, dec_ref, f_ref,
                   a_ref, b_ref, d_ref, *, seq_len):
    n = 2 * seq_len
    oc = 2 * HY_WIDTH
    z = z_ref[...]
    h = jnp.sin(_dot3(z, w1_ref[...]) + b1_ref[...])
    h = jnp.sin(_dot3(h, w2_ref[...]) + b2_ref[...])
    t = z[:, 0:1]
    di = lax.broadcasted_iota(jnp.int32, (seq_len, 1), 0)
    sgn = jnp.where(di % 2 == 0, 1.0, -1.0)
    first = di == 0
    ssums, sdifs = [], []
    for o in range(HY_ORDER):
        cols = slice(o * oc, (o + 1) * oc)
        g = _dot3(h, w3_ref[:, cols]) + b3_ref[:, cols]
        g = g * (jnp.exp(-t * jnp.abs(dec_ref[:, cols])) + HY_MOD_SHIFT)
        ss = jnp.sum(g * g, axis=0, keepdims=True)
        inv = lax.rsqrt(ss[:, :HY_WIDTH] + ss[:, HY_WIDTH:] + EPS)
        hp = g[:, :HY_WIDTH] * inv
        hn = g[:, HY_WIDTH:] * inv
        ssums.append(hp + hn)
        sdifs.append(hp - hn)
    hcs = [_dot(f_ref[0:seq_len, :], s.astype(BF16)) for s in ssums]
    hss = [_dot(f_ref[seq_len:n, :], s.astype(BF16)) for s in sdifs]
    for o in range(HY_ORDER):
        nyq = jnp.sum(ssums[o] * sgn, axis=0, keepdims=True)
        a_ref[o] = hcs[o] * jnp.where(first, 1.0 / n, 2.0 / n)
        b_ref[o] = jnp.where(first, 0.0, hss[o] * (2.0 / n))
        d_ref[o] = jnp.where(first, nyq * (1.0 / n), hcs[o] * (2.0 / n))


def _hyena_filters(seq_len, f, w1p, b1, w2, b2, w3, b3, dec):
    z = _filter_feats(seq_len)
    out = jax.ShapeDtypeStruct((HY_ORDER, seq_len, HY_WIDTH), F32)
    return pl.pallas_call(
        functools.partial(_filter_kernel, seq_len=seq_len),
        out_shape=(out, out, out),
        compiler_params=pltpu.CompilerParams(vmem_limit_bytes=VMEM_LIMIT),
        name=f"hyena_filter_{seq_len}",
    )(z, w1p, b1, w2, b2, w3, b3, dec, f)


def _hyena_kernel(x1_ref, x2_ref, v_ref, cw1_ref, cw2_ref, cwv_ref, a_ref, b_ref, d_ref, bias_ref,
                  f_ref, ft_ref, z_ref, *, seq_len, width, seqs):
    rows = seqs * seq_len
    ti = lax.broadcasted_iota(jnp.int32, (rows, 1), 0)
    has_prev = (ti % width) != 0
    has_next = (ti % width) != (width - 1)

    def short_conv(x_ref, w_ref):
        x = x_ref[...].astype(F32)
        prev = jnp.where(has_prev, pltpu.roll(x, 1, axis=0), 0.0)
        nxt = jnp.where(has_next, pltpu.roll(x, rows - 1, axis=0), 0.0)
        return w_ref[0:1, :] * prev + w_ref[1:2, :] * x + w_ref[2:3, :] * nxt

    gates = (short_conv(x1_ref, cw1_ref), short_conv(x2_ref, cw2_ref))
    v = short_conv(v_ref, cwv_ref)
    sls = [slice(i * seq_len, (i + 1) * seq_len) for i in range(seqs)]
    zs = [v[sl] for sl in sls]
    for o in range(HY_ORDER):
        a, b, dd = a_ref[o], b_ref[o], d_ref[o]
        us = [_dot(f_ref[...], z.astype(BF16)) for z in zs]
        ys = []
        for u in us:
            ut = u[:seq_len]
            ub = u[seq_len:]
            ys.append(((ut * a - ub * b).astype(BF16), (ut * b + ub * dd).astype(BF16)))
        convs = [_dot(ft_ref[:, :seq_len], yt) + _dot(ft_ref[:, seq_len:], yb) for yt, yb in ys]
        zs = [gates[o][sl] * (y + bias_ref[o:o + 1, :] * z) for sl, y, z in zip(sls, convs, zs)]
    for sl, z in zip(sls, zs):
        z_ref[sl, :] = z.astype(BF16)


def _hyena(proj, conv_w, coefs, hy_bias, f, ft, seq_len, n_seq, row_off, width, seqs):
    cb = 256
    nblk = HY_WIDTH // cb
    base = ML_QKVO_COLS // cb
    rows = seqs * seq_len
    off = row_off // rows
    a, b, d = coefs

    def col_spec(part):
        return pl.BlockSpec((rows, cb), lambda j, s: (off + s, base + part * nblk + j))

    def w_spec(part):
        return pl.BlockSpec((3, cb), lambda j, s: (0, part * nblk + j))

    coef_spec = pl.BlockSpec((HY_ORDER, seq_len, cb), lambda j, s: (0, 0, j))
    return pl.pallas_call(
        functools.partial(_hyena_kernel, seq_len=seq_len, width=width, seqs=seqs),
        out_shape=jax.ShapeDtypeStruct((n_seq * seq_len, HY_WIDTH), BF16),
        grid=(nblk, n_seq // seqs),
        in_specs=[col_spec(0), col_spec(1), col_spec(2), w_spec(0), w_spec(1), w_spec(2),
                  coef_spec, coef_spec, coef_spec,
                  pl.BlockSpec((HY_ORDER, cb), lambda j, s: (0, j)),
                  pl.BlockSpec((2 * seq_len, seq_len), lambda j, s: (0, 0)),
                  pl.BlockSpec((seq_len, 2 * seq_len), lambda j, s: (0, 0))],
        out_specs=pl.BlockSpec((rows, cb), lambda j, s: (s, j)),
        compiler_params=_cparams(("arbitrary", "arbitrary")),
        name=f"hyena_conv_{seq_len}",
    )(proj, proj, proj, conv_w, conv_w, conv_w, a, b, d, hy_bias, f, ft)


def _first_max(x, n):
    mx = jnp.max(x, axis=0, keepdims=True)
    row = lax.broadcasted_iota(jnp.int32, x.shape, 0).astype(F32)
    idx = jnp.min(jnp.where(x == mx, row, float(n)), axis=0, keepdims=True)
    return mx, idx.astype(jnp.int32)


ROUTER_ROWS = 32
PAIRS_PER_GROUP = 6
N_BUCKETS = N_GROUPS * PAIRS_PER_GROUP
PAIR_SLOTS = ((0, 1), (0, 2), (0, 3), (1, 3), (1, 2), (3, 2))
LANES = 128
H2_EXT = D_MODEL + LANES
ROW_TILE = 256
ROW_CAP = T_ALL + N_BUCKETS * ROW_TILE
N_ROW_TILES = ROW_CAP // ROW_TILE


def _outproj_kernel(xp_ref, xs_ref, yp_ref, ys_ref, zp_ref, zs_ref, m_ref, gn_ref, wo_ref, wr_ref, br_ref,
                    x1_ref, h2_ref, bid_ref):
    is_p = pl.program_id(0) < N_BIG_P
    wrh, wrl = _split2(wr_ref[...])
    halves = [slice(r * TILE, (r + 1) * TILE) for r in range(BIG_TILE // TILE)]
    ys = [_dot(jnp.where(is_p, yp_ref[rows, :], ys_ref[rows, :]), wo_ref[0:ML_WIDTH, :])
          + _dot(jnp.where(is_p, zp_ref[rows, :], zs_ref[rows, :]), wo_ref[ML_WIDTH:, :]) for rows in halves]
    h2s = []
    for rows, y in zip(halves, ys):
        x = jnp.where(is_p, xp_ref[rows, :], xs_ref[rows, :])
        x1 = x + m_ref[0, 2:3, :] * _rms(y, gn_ref[1:2, :])
        x1_ref[rows, :] = x1
        h2 = _rms(x1, gn_ref[2:3, :]) * (1.0 + m_ref[0, 4:5, :]) + m_ref[0, 3:4, :]
        h2_ref[rows, 0:D_MODEL] = h2
        h2s.append(h2)
    logits = []
    for h2 in h2s:
        h2h, h2l = _split2(h2)
        logits.append(_dot_nt(wrh, h2h) + _dot_nt(wrh, h2l) + _dot_nt(wrl, h2h) + br_ref[...])
    routed = [_route_tile(lg) for lg in logits]
    for r, (rows, (gate_rows, bucket)) in enumerate(zip(halves, routed)):
        h2_ref[rows, D_MODEL:H2_EXT] = jnp.zeros((TILE, LANES), F32)
        h2_ref[rows, D_MODEL:D_MODEL + 8] = _rows_to_cols(gate_rows)
        bid_ref[r] = bucket


def _route_tile(logits):
    lc = logits[0:N_GROUPS]
    mx, gi = _first_max(lc, N_GROUPS)
    p_grp = 1.0 / jnp.sum(jnp.exp(lc - mx), axis=0, keepdims=True)
    lsel = jnp.zeros((EXPERTS_PER_GROUP, TILE), F32)
    for g in range(N_GROUPS):
        lo = N_GROUPS + g * EXPERTS_PER_GROUP
        lsel = jnp.where(gi == g, logits[lo:lo + EXPERTS_PER_GROUP], lsel)
    l1, i1 = _first_max(lsel, EXPERTS_PER_GROUP)
    sub4 = lax.broadcasted_iota(jnp.int32, lsel.shape, 0)
    l2, i2 = _first_max(jnp.where(sub4 == i1, -jnp.inf, lsel), EXPERTS_PER_GROUP)
    e2 = jnp.exp(l2 - l1)
    w1 = p_grp / (1.0 + e2)
    w2 = p_grp * e2 / (1.0 + e2)
    lo_e = jnp.minimum(i1, i2)
    hi_e = jnp.maximum(i1, i2)
    pair = jnp.where(lo_e == 0, hi_e - 1, jnp.where(lo_e == 1, jnp.where(hi_e == 3, 3, 4), 5))
    slot_a = jnp.where(pair == 5, hi_e, lo_e)
    first_in_a = i1 == slot_a
    w_a = jnp.where(first_in_a, w1, w2)
    w_b = jnp.where(first_in_a, w2, w1)
    sub = lax.broadcasted_iota(jnp.int32, (8, TILE), 0)
    gate_rows = jnp.where(sub == 0, w_a, jnp.where(sub == 1, w_b, 0.0))
    return gate_rows, gi * PAIRS_PER_GROUP + pair


def _outproj(xp, xs, yp, ys, zp, zs, mods3, g_norm, w_out, w_r, b_r):
    tps = DEC_SEQ // BIG_TILE
    per = BIG_TILE // TILE
    pidx = lambda i: (jnp.minimum(i, N_BIG_P - 1), 0)
    sidx = lambda i: (jnp.maximum(i - N_BIG_P, 0), 0)
    return pl.pallas_call(
        _outproj_kernel,
        out_shape=(jax.ShapeDtypeStruct((T_ALL, D_MODEL), F32),
                   jax.ShapeDtypeStruct((T_ALL, H2_EXT), F32),
                   jax.ShapeDtypeStruct((N_TILES, 1, TILE), jnp.int32)),
        grid=(N_BIG,),
        in_specs=[pl.BlockSpec((BIG_TILE, D_MODEL), pidx), pl.BlockSpec((BIG_TILE, D_MODEL), sidx),
                  pl.BlockSpec((BIG_TILE, ML_WIDTH), pidx), pl.BlockSpec((BIG_TILE, ML_WIDTH), sidx),
                  pl.BlockSpec((BIG_TILE, HY_WIDTH), pidx), pl.BlockSpec((BIG_TILE, HY_WIDTH), sidx),
                  pl.BlockSpec((1, N_MOD, D_MODEL), lambda i: (_mod_row_of_tile(i, tps, N_BIG_P), 0, 0)),
                  pl.BlockSpec((4, D_MODEL), lambda i: (0, 0)),
                  pl.BlockSpec((D_MODEL, D_MODEL), lambda i: (0, 0)),
                  pl.BlockSpec((ROUTER_ROWS, D_MODEL), lambda i: (0, 0)),
                  pl.BlockSpec((ROUTER_ROWS, 1), lambda i: (0, 0))],
        out_specs=(pl.BlockSpec((BIG_TILE, D_MODEL), lambda i: (i, 0)),
                   pl.BlockSpec((BIG_TILE, H2_EXT), lambda i: (i, 0)),
                   pl.BlockSpec((per, 1, TILE), lambda i: (i, 0, 0))),
        compiler_params=_cparams(("arbitrary",)),
        name="out_proj_router",
    )(xp, xs, yp, ys, zp, zs, mods3, g_norm, w_out, w_r, b_r)


def _route_kernel(bid_ref, pos_ref, meta_ref):
    nb = 32
    tm = float(ROW_TILE)
    sub = lax.broadcasted_iota(jnp.int32, (nb, TILE), 0)
    ri = lax.broadcasted_iota(jnp.int32, (TILE, TILE), 0)
    ci = lax.broadcasted_iota(jnp.int32, (TILE, TILE), 1)
    before = jnp.where(ri < ci, 1.0, 0.0).astype(BF16)

    def onehot(blk):
        return jnp.where(sub == bid_ref[blk], 1.0, 0.0)

    zeros = jnp.zeros((nb, 1), F32)
    cnt = lax.fori_loop(0, N_TILES, lambda blk, c: c + jnp.sum(onehot(blk), axis=1, keepdims=True), zeros)
    padded = jnp.floor((cnt + (tm - 1.0)) * (1.0 / tm)) * tm
    r32 = lax.broadcasted_iota(jnp.int32, (nb, nb), 0)
    c32 = lax.broadcasted_iota(jnp.int32, (nb, nb), 1)
    padded_row = jnp.sum(jnp.where(r32 == c32, padded, 0.0), axis=0, keepdims=True)
    offs = jnp.sum(jnp.where(c32 < r32, padded_row, 0.0), axis=1, keepdims=True)
    ends = offs + padded

    def place(blk, seen):
        oh = onehot(blk)
        rank = _dot(oh.astype(BF16), before)
        pos = jnp.sum(oh * (rank + seen + offs), axis=0, keepdims=True)
        pos_ref[blk] = pos.astype(jnp.int32)
        return seen + jnp.sum(oh, axis=1, keepdims=True)

    lax.fori_loop(0, N_TILES, place, zeros)

    start = lax.broadcasted_iota(jnp.int32, (nb, 128), 1).astype(F32) * tm
    bsub = lax.broadcasted_iota(jnp.int32, (nb, 128), 0)
    done = jnp.where((bsub < N_BUCKETS) & (ends <= start), 1.0, 0.0)
    tb = jnp.sum(done, axis=0, keepdims=True)
    valid = jnp.where(tb < N_BUCKETS, 1.0, 0.0)
    tbc = jnp.minimum(tb, N_BUCKETS - 1.0)
    grp = jnp.floor((tbc + 0.5) * (1.0 / PAIRS_PER_GROUP))
    pair = tbc - PAIRS_PER_GROUP * grp
    loc_a = jnp.zeros_like(pair)
    loc_b = jnp.zeros_like(pair)
    for k, (sa, sb) in enumerate(PAIR_SLOTS):
        loc_a = jnp.where(pair == k, float(sa), loc_a)
        loc_b = jnp.where(pair == k, float(sb), loc_b)
    mine = bsub.astype(F32) == tbc
    used = jnp.sum(jnp.where(mine, offs + cnt, 0.0), axis=0, keepdims=True)
    n_rows = jnp.clip(used - start[0:1], 0.0, tm) * valid
    row8 = lax.broadcasted_iota(jnp.int32, (8, 128), 0)
    meta = jnp.where(row8 == 0, grp * EXPERTS_PER_GROUP + loc_a,
                     jnp.where(row8 == 1, grp * EXPERTS_PER_GROUP + loc_b,
                               jnp.where(row8 == 2, valid, jnp.where(row8 == 3, n_rows, 0.0))))
    meta_ref[...] = meta.astype(jnp.int32)


def _route(bid):
    return pl.pallas_call(
        _route_kernel,
        out_shape=(jax.ShapeDtypeStruct((N_TILES, 1, TILE), jnp.int32),
                   jax.ShapeDtypeStruct((8, 128), jnp.int32)),
        compiler_params=pltpu.CompilerParams(vmem_limit_bytes=VMEM_LIMIT),
        name="moe_route",
    )(bid)


def _moe_kernel(meta_ref, pos_ref, h2_hbm, wga_ref, wua_ref, wda_ref, wgb_ref, wub_ref, wdb_ref,
                y_ref, src_ref, xbuf, sem, wga_s, wua_s, wda_s, wgb_s, wub_s, wdb_s):
    j = pl.program_id(0)

    def row_copy(tile, r, slot):
        tok = src_ref[tile * ROW_TILE + r]
        return pltpu.make_async_copy(h2_hbm.at[pl.ds(tok, 1), :], xbuf.at[slot, pl.ds(r, 1), :], sem.at[slot])

    def issue_rows(tile, slot, lo, hi):
        for r in range(lo, hi):
            row_copy(tile, r, slot).start()

    group = 8

    def row_groups(tile):
        return (meta_ref[3, tile] + (group - 1)) // group

    def issue_counted(tile, slot):
        def body(g, c):
            for k in range(group):
                row_copy(tile, g * group + k, slot).start()
            return c
        lax.fori_loop(0, row_groups(tile), body, 0)

    def wait_counted(tile, slot):
        def body(g, c):
            for k in range(group):
                row_copy(tile, g * group + k, slot).wait()
            return c
        lax.fori_loop(0, row_groups(tile), body, 0)

    def wait_full(slot):
        pltpu.make_async_copy(h2_hbm.at[pl.ds(0, ROW_TILE), :], xbuf.at[slot], sem.at[slot]).wait()

    @pl.when(j == 0)
    def _():
        xbuf[...] = jnp.zeros_like(xbuf)

        def clear(t, c):
            n = meta_ref[3, t]
            for k in range(group - 1):
                src_ref[t * ROW_TILE + jnp.minimum(n + k, ROW_TILE - 1)] = 0
            return c
        lax.fori_loop(0, N_ROW_TILES, clear, 0)

        def invert(t, c):
            src_ref[pos_ref[t]] = t
            return c
        lax.fori_loop(0, T_ALL, invert, 0, unroll=8)

        @pl.when(meta_ref[2, 0] == 1)
        def _():
            issue_counted(0, 0)

    nxt = jnp.minimum(j + 1, N_ROW_TILES - 1)
    has_next = (j + 1 < N_ROW_TILES) & (meta_ref[2, nxt] == 1)
    next_full = has_next & (meta_ref[3, nxt] == ROW_TILE)
    valid = meta_ref[2, j] == 1
    full = meta_ref[3, j] == ROW_TILE
    prev = jnp.maximum(j - 1, 0)

    @pl.when(valid & has_next & jnp.logical_not(next_full))
    def _():
        issue_counted(nxt, nxt % 2)

    @pl.when(valid & full)
    def _():
        wait_full(j % 2)

    @pl.when(valid & jnp.logical_not(full))
    def _():
        wait_counted(j, j % 2)

    @pl.when(valid & ((j == 0) | (meta_ref[0, j] != meta_ref[0, prev])))
    def _():
        wga_s[...] = wga_ref[0].astype(BF16)
        wua_s[...] = wua_ref[0].astype(BF16)
        wda_s[...] = wda_ref[0].astype(BF16)

    @pl.when(valid & ((j == 0) | (meta_ref[1, j] != meta_ref[1, prev])))
    def _():
        wgb_s[...] = wgb_ref[0].astype(BF16)
        wub_s[...] = wub_ref[0].astype(BF16)
        wdb_s[...] = wdb_ref[0].astype(BF16)

    def compute(fetch_next):
        slot = j % 2
        nslot = nxt % 2
        step = ROW_TILE // 8
        batches = iter(range(0, ROW_TILE, step))

        def fetch():
            if fetch_next:
                lo = next(batches)
                issue_rows(nxt, nslot, lo, lo + step)

        x = xbuf[slot, :, 0:D_MODEL].astype(BF16)
        gates = xbuf[slot, :, D_MODEL:H2_EXT]
        hg_a = _dot(x, wga_s[...])
        fetch()
        hu_a = _dot(x, wua_s[...])
        fetch()
        hg_b = _dot(x, wgb_s[...])
        fetch()
        hu_b = _dot(x, wub_s[...])
        fetch()
        act_a = (hg_a * jax.nn.sigmoid(hg_a) * hu_a * gates[:, 0:1]).astype(BF16)
        fetch()
        act_b = (hg_b * jax.nn.sigmoid(hg_b) * hu_b * gates[:, 1:2]).astype(BF16)
        fetch()
        y = _dot(act_a, wda_s[...])
        fetch()
        y = y + _dot(act_b, wdb_s[...])
        fetch()
        y_ref[...] = y

    @pl.when(valid & next_full)
    def _():
        compute(True)

    @pl.when(valid & jnp.logical_not(next_full))
    def _():
        compute(False)

    @pl.when(jnp.logical_not(valid))
    def _():
        y_ref[...] = jnp.zeros_like(y_ref)


def _moe(meta, pos, h2ext, w_gate, w_up, w_down):
    up_spec = lambda slot: pl.BlockSpec((1, D_MODEL, EXPERT_FF), lambda j, meta, pos: (meta[slot, j], 0, 0))
    down_spec = lambda slot: pl.BlockSpec((1, EXPERT_FF, D_MODEL), lambda j, meta, pos: (meta[slot, j], 0, 0))
    grid_spec = pltpu.PrefetchScalarGridSpec(
        num_scalar_prefetch=2,
        grid=(N_ROW_TILES,),
        in_specs=[pl.BlockSpec(memory_space=pl.ANY),
                  up_spec(0), up_spec(0), down_spec(0), up_spec(1), up_spec(1), down_spec(1)],
        out_specs=pl.BlockSpec((ROW_TILE, D_MODEL), lambda j, meta, pos: (j, 0)),
        scratch_shapes=[pltpu.SMEM((ROW_CAP,), jnp.int32),
                        pltpu.VMEM((2, ROW_TILE, H2_EXT), F32),
                        pltpu.SemaphoreType.DMA((2,)),
                        pltpu.VMEM((D_MODEL, EXPERT_FF), BF16), pltpu.VMEM((D_MODEL, EXPERT_FF), BF16),
                        pltpu.VMEM((EXPERT_FF, D_MODEL), BF16),
                        pltpu.VMEM((D_MODEL, EXPERT_FF), BF16), pltpu.VMEM((D_MODEL, EXPERT_FF), BF16),
                        pltpu.VMEM((EXPERT_FF, D_MODEL), BF16)])
    return pl.pallas_call(
        _moe_kernel,
        out_shape=jax.ShapeDtypeStruct((ROW_CAP, D_MODEL), F32),
        grid_spec=grid_spec,
        compiler_params=_cparams(("arbitrary",)),
        name="moe_experts",
    )(meta, pos, h2ext, w_gate, w_up, w_down, w_gate, w_up, w_down)


def _final_kernel(pos_ref, y_hbm, x1_ref, m_ref, gn_ref, op_ref, os_ref, ybuf, sem):
    i = pl.program_id(0)

    def row_copy(tile, r, slot):
        p = pos_ref[tile * TILE + r]
        return pltpu.make_async_copy(y_hbm.at[pl.ds(p, 1), :], ybuf.at[slot, pl.ds(r, 1), :], sem.at[slot])

    def issue(tile, slot):
        def body(r2, c):
            row_copy(tile, 2 * r2, slot).start(priority=0)
            row_copy(tile, 2 * r2 + 1, slot).start(priority=1)
            return c
        lax.fori_loop(0, TILE // 2, body, 0, unroll=4)

    def wait(slot):
        pltpu.make_async_copy(y_hbm.at[pl.ds(0, TILE)], ybuf.at[slot], sem.at[slot]).wait()

    @pl.when(i == 0)
    def _():
        issue(0, 0)

    @pl.when(i + 1 < N_TILES)
    def _():
        issue(i + 1, (i + 1) % 2)

    slot = i % 2
    wait(slot)
    out = x1_ref[...] + m_ref[0, 5:6, :] * _rms(ybuf[slot], gn_ref[3:4, :])

    @pl.when(i < N_TILES_P)
    def _():
        op_ref[...] = out

    @pl.when(i >= N_TILES_P)
    def _():
        os_ref[...] = out


def _final(pos, y_sorted, x1, mods3, g_norm):
    tps = DEC_SEQ // TILE
    grid_spec = pltpu.PrefetchScalarGridSpec(
        num_scalar_prefetch=1,
        grid=(N_TILES,),
        in_specs=[pl.BlockSpec(memory_space=pl.ANY),
                  pl.BlockSpec((TILE, D_MODEL), lambda i, pos: (i, 0)),
                  pl.BlockSpec((1, N_MOD, D_MODEL), lambda i, pos: (_mod_row_of_tile(i, tps, N_TILES_P), 0, 0)),
                  pl.BlockSpec((4, D_MODEL), lambda i, pos: (0, 0))],
        out_specs=(pl.BlockSpec((TILE, D_MODEL), lambda i, pos: (jnp.minimum(i, N_TILES_P - 1), 0)),
                   pl.BlockSpec((TILE, D_MODEL), lambda i, pos: (jnp.maximum(i - N_TILES_P, 0), 0))),
        scratch_shapes=[pltpu.VMEM((2, TILE, D_MODEL), F32), pltpu.SemaphoreType.DMA((2,))])
    return pl.pallas_call(
        _final_kernel,
        out_shape=(jax.ShapeDtypeStruct((T_PROMPT, D_MODEL), F32),
                   jax.ShapeDtypeStruct((T_SAMPLE, D_MODEL), F32)),
        grid_spec=grid_spec,
        compiler_params=_cparams(("arbitrary",)),
        name="moe_combine_final",
    )(pos, y_sorted, x1, mods3, g_norm)


def kernel(x_prompt, x_sample, state_C, state_n, state_m, c, c_ctx, w_ada, b_ada, g_norm, w_in, ml_gate_bias, ml_head_gain, hy_conv_w, hy_f_w1, hy_f_b1, hy_f_w2, hy_f_b2, hy_f_w3, hy_f_b3, hy_decay, hy_bias, w_out, w_rc, b_rc, w_rf, b_rf, w_gate, w_up, w_down):
    xp = x_prompt.reshape(T_PROMPT, D_MODEL)
    xs = x_sample.reshape(T_SAMPLE, D_MODEL)
    gn = g_norm[0]

    cv = jnp.concatenate([c_ctx[None, :], c, jnp.zeros((MOD_ROWS - 1 - DEC_BATCH, D_MODEL), F32)], axis=0)
    mods3 = _ada(cv, w_ada[0], b_ada[0]).reshape(MOD_ROWS, N_MOD, D_MODEL)

    w_in0 = w_in[0]
    w_qkvo, w_hy = _prep_in_weights(w_in0.T)
    wg = w_in0[:, ML_QKVO_COLS:ML_QKVO_COLS + ML_GATE_COLS]
    gbt = ml_gate_bias[0].reshape(ML_GATE_COLS, 1)
    proj, gates_t = _inproj(xp, xs, mods3, gn, w_qkvo, w_hy, wg.T, gbt)

    gain = ml_head_gain[0].reshape(1, ML_WIDTH)
    y_ml_p, c_new, n_new, m_new = _mlstm(proj, gates_t, gain, None, SEQ, BATCH, 0)
    state = (state_C[:, 0], state_n[:, 0], state_m[:, 0].reshape(DEC_BATCH, 2 * ML_HEADS, 1))
    y_ml_s, _, _, _ = _mlstm(proj, gates_t, gain, state, DEC_SEQ, DEC_BATCH, T_PROMPT // DEC_SEQ)

    w1p = jnp.pad(hy_f_w1[0], ((0, 128 - HY_EMB), (0, 0)))
    b1 = hy_f_b1[0].reshape(1, -1)
    b2 = hy_f_b2[0].reshape(1, -1)
    b3 = hy_f_b3[0].reshape(1, -1)
    dec = hy_decay[0].reshape(1, -1)
    z_parts = []
    for seq_len, n_seq, row_off, width, seqs in ((SEQ, BATCH, 0, SEQ, 4), (DEC_SEQ, DEC_BATCH, T_PROMPT, GRID_W, 2)):
        f, ft = _dft_mats(seq_len)
        coefs = _hyena_filters(seq_len, f, w1p, b1, hy_f_w2[0], b2, hy_f_w3[0], b3, dec)
        z_parts.append(_hyena(proj, hy_conv_w[0], coefs, hy_bias[0], f, ft, seq_len, n_seq, row_off, width, seqs))
    z_p, z_s = z_parts

    pad_r = ROUTER_ROWS - N_GROUPS - N_EXPERTS
    w_r = jnp.pad(jnp.concatenate([w_rc[0], w_rf[0]], axis=1).T, ((0, pad_r), (0, 0)))
    b_r = jnp.pad(jnp.concatenate([b_rc[0], b_rf[0]], axis=0), (0, pad_r)).reshape(ROUTER_ROWS, 1)
    x1, h2ext, bid = _outproj(xp, xs, y_ml_p, y_ml_s, z_p, z_s, mods3, gn, w_out[0].astype(BF16), w_r, b_r)

    pos3, meta = _route(bid)
    pos = pos3.reshape(T_ALL)
    y_sorted = _moe(meta, pos, h2ext, w_gate[0], w_up[0], w_down[0])
    y_p, y_s = _final(pos, y_sorted, x1, mods3, gn)

    new_c = c_new.reshape(BATCH, 1, 2, ML_HEADS, ML_HEAD_DIM, ML_HEAD_DIM)
    new_n = n_new.reshape(BATCH, 1, 2, ML_HEADS, ML_HEAD_DIM)
    new_m = m_new[:, :, 0].reshape(BATCH, 1, 2, ML_HEADS)
    return (y_p.reshape(BATCH, SEQ, D_MODEL), y_s.reshape(DEC_BATCH, DEC_SEQ, D_MODEL), new_c, new_n, new_m)
```

```python
import functools
import math

import jax
import jax.numpy as jnp
import numpy as np
from jax import lax
from jax.experimental import pallas as pl
from jax.experimental.pallas import tpu as pltpu

F32 = jnp.float32
BF16 = jnp.bfloat16

D_MODEL = 1024
BATCH = 16
SEQ = 256
DEC_BATCH = 4
DEC_SEQ = 1024
GRID_W = 64
ML_WIDTH = 512
ML_HEADS = 4
ML_HEAD_DIM = 128
HY_WIDTH = 512
HY_ORDER = 2
HY_EMB = 33
HY_BANDS = 16
HY_FILTER_HIDDEN = 64
HY_MOD_SHIFT = 0.05
N_GROUPS = 4
EXPERTS_PER_GROUP = 4
N_EXPERTS = 16
EXPERT_FF = 512
N_MOD = 6
EPS = 1e-6
ML_QKVO_COLS = 4 * ML_WIDTH
ML_GATE_COLS = 4 * ML_HEADS
HY_COLS = 3 * HY_WIDTH
MAIN_COLS = ML_QKVO_COLS + HY_COLS

T_PROMPT = BATCH * SEQ
T_SAMPLE = DEC_BATCH * DEC_SEQ
T_ALL = T_PROMPT + T_SAMPLE
TILE = 256
N_TILES_P = T_PROMPT // TILE
N_TILES = T_ALL // TILE
MOD_ROWS = 8
K_SCALE = ML_HEAD_DIM ** -0.5
VMEM_LIMIT = 56 * 1024 * 1024


def _cparams(sem):
    return pltpu.CompilerParams(dimension_semantics=sem, vmem_limit_bytes=VMEM_LIMIT)


def _split2(x):
    hi = x.astype(BF16)
    lo = (x - hi.astype(F32)).astype(BF16)
    return hi, lo


def _dot(a, b):
    return jnp.dot(a, b, preferred_element_type=F32)


def _dot_nt(a, b):
    return lax.dot_general(a, b, (((1,), (1,)), ((), ())), preferred_element_type=F32)


def _dot3(a, b):
    ah, al = _split2(a)
    bh, bl = _split2(b)
    return _dot(ah, bh) + _dot(al, bh) + _dot(ah, bl)


def _dot_exact_rhs(x, t):
    x1 = x.astype(BF16)
    r1 = x - x1.astype(F32)
    x2 = r1.astype(BF16)
    x3 = (r1 - x2.astype(F32)).astype(BF16)
    return _dot(x1, t) + _dot(x2, t) + _dot(x3, t)


def _rms(x, g):
    return x * lax.rsqrt(jnp.mean(x * x, axis=-1, keepdims=True) + EPS) * g


def _mod_row_of_tile(i, tiles_per_sample_seq, n_prompt_tiles):
    return jnp.where(i < n_prompt_tiles, 0, 1 + (i - n_prompt_tiles) // tiles_per_sample_seq)


def _ada_kernel(cv_ref, w_ref, b_ref, o_ref):
    cv = cv_ref[...]
    s = cv * jax.nn.sigmoid(cv)
    sh, sl = _split2(s)
    wh, wl = _split2(w_ref[...])
    both = _dot(jnp.concatenate([sh.astype(F32), sl.astype(F32)], axis=0).astype(BF16), wh)
    o_ref[...] = both[0:MOD_ROWS] + both[MOD_ROWS:] + _dot(sh, wl) + b_ref[...]


def _ada(cv, w_ada, b_ada):
    n = N_MOD * D_MODEL
    return pl.pallas_call(
        _ada_kernel,
        out_shape=jax.ShapeDtypeStruct((MOD_ROWS, n), F32),
        grid=(N_MOD,),
        in_specs=[pl.BlockSpec((MOD_ROWS, D_MODEL), lambda j: (0, 0)),
                  pl.BlockSpec((D_MODEL, D_MODEL), lambda j: (0, j)),
                  pl.BlockSpec((1, D_MODEL), lambda j: (0, j))],
        out_specs=pl.BlockSpec((MOD_ROWS, D_MODEL), lambda j: (0, j)),
        compiler_params=_cparams(("arbitrary",)),
        name="ada_mod",
    )(cv, w_ada, b_ada.reshape(1, n))


PREP_COLS = 512


def _prep_q_kernel(wt_ref, o_ref):
    o_ref[...] = wt_ref[...].T.astype(BF16)


def _prep_hy_kernel(wt_hbm, o_ref, buf, sem):
    start = pl.multiple_of(ML_QKVO_COLS + ML_GATE_COLS + pl.program_id(0) * PREP_COLS, 8)
    copy = pltpu.make_async_copy(wt_hbm.at[pl.ds(start, PREP_COLS), :], buf, sem)
    copy.start()
    copy.wait()
    o_ref[...] = buf[...].T.astype(BF16)


def _prep_in_weights(w_in_t):
    out_blk = pl.BlockSpec((D_MODEL, PREP_COLS), lambda j: (0, j))
    w_qkvo = pl.pallas_call(
        _prep_q_kernel,
        out_shape=jax.ShapeDtypeStruct((D_MODEL, ML_QKVO_COLS), BF16),
        grid=(ML_QKVO_COLS // PREP_COLS,),
        in_specs=[pl.BlockSpec((PREP_COLS, D_MODEL), lambda j: (j, 0))], out_specs=out_blk,
        compiler_params=_cparams(("arbitrary",)), name="prep_w_qkvo",
    )(w_in_t)
    w_hy = pl.pallas_call(
        _prep_hy_kernel,
        out_shape=jax.ShapeDtypeStruct((D_MODEL, HY_COLS), BF16),
        grid=(HY_COLS // PREP_COLS,),
        in_specs=[pl.BlockSpec(memory_space=pl.ANY)], out_specs=out_blk,
        scratch_shapes=[pltpu.VMEM((PREP_COLS, D_MODEL), F32), pltpu.SemaphoreType.DMA(())],
        compiler_params=_cparams(("arbitrary",)), name="prep_w_hy",
    )(w_in_t)
    return w_qkvo, w_hy


def _log_sigmoid(x):
    return jnp.minimum(x, 0.0) - jnp.log1p(jnp.exp(-jnp.abs(x)))


def _rows_to_cols(rows):
    ri = lax.broadcasted_iota(jnp.int32, (TILE, TILE), 0)
    ci = lax.broadcasted_iota(jnp.int32, (TILE, TILE), 1)
    eye = jnp.where(ri == ci, 1.0, 0.0).astype(BF16)
    p1 = rows.astype(BF16)
    r1 = rows - p1.astype(F32)
    p2 = r1.astype(BF16)
    p3 = (r1 - p2.astype(F32)).astype(BF16)
    return _dot_nt(eye, p1) + _dot_nt(eye, p2) + _dot_nt(eye, p3)


BIG_TILE = 4 * TILE
N_BIG_P = T_PROMPT // BIG_TILE
N_BIG = T_ALL // BIG_TILE


def _inproj_kernel(xp_ref, xs_ref, m_ref, gn_ref, wq_ref, wh_ref, wgt_ref, gbt_ref, proj_ref, gatet_ref):
    is_p = pl.program_id(0) < N_BIG_P
    halves = [slice(r * TILE, (r + 1) * TILE) for r in range(BIG_TILE // TILE)]
    hs = [_rms(jnp.where(is_p, xp_ref[rows, :], xs_ref[rows, :]), gn_ref[0:1, :]) * (1.0 + m_ref[0, 1:2, :])
          + m_ref[0, 0:1, :] for rows in halves]
    hbs = [h.astype(BF16) for h in hs]
    cb = 1024
    for j in range(ML_QKVO_COLS // cb):
        for rows, hb in zip(halves, hbs):
            proj_ref[rows, j * cb:(j + 1) * cb] = _dot(hb, wq_ref[:, j * cb:(j + 1) * cb]).astype(BF16)
    cb = 512
    for j in range(HY_COLS // cb):
        lo = ML_QKVO_COLS + j * cb
        for rows, hb in zip(halves, hbs):
            proj_ref[rows, lo:lo + cb] = _dot(hb, wh_ref[:, j * cb:(j + 1) * cb]).astype(BF16)
    wth, wtl = _split2(wgt_ref[...])
    wt2 = jnp.concatenate([wth, wtl], axis=0)
    gts = []
    for h, hb in zip(hs, hbs):
        hl = (h - hb.astype(F32)).astype(BF16)
        both = _dot_nt(wt2, hb)
        gt = both[0:ML_GATE_COLS] + both[ML_GATE_COLS:] + _dot_nt(wth, hl) + gbt_ref[...]
        row = lax.broadcasted_iota(jnp.int32, gt.shape, 0)
        gts.append(jnp.where((row % 8) >= 4, _log_sigmoid(gt), gt))
    for r, gt in enumerate(gts):
        gatet_ref[r] = gt


def _inproj(xp, xs, mods3, g_norm, w_qkvo, w_hy, wgt, gbt):
    tps = DEC_SEQ // BIG_TILE
    per = BIG_TILE // TILE
    return pl.pallas_call(
        _inproj_kernel,
        out_shape=(jax.ShapeDtypeStruct((T_ALL, MAIN_COLS), BF16),
                   jax.ShapeDtypeStruct((N_TILES, ML_GATE_COLS, TILE), F32)),
        grid=(N_BIG,),
        in_specs=[pl.BlockSpec((BIG_TILE, D_MODEL), lambda i: (jnp.minimum(i, N_BIG_P - 1), 0)),
                  pl.BlockSpec((BIG_TILE, D_MODEL), lambda i: (jnp.maximum(i - N_BIG_P, 0), 0)),
                  pl.BlockSpec((1, N_MOD, D_MODEL), lambda i: (_mod_row_of_tile(i, tps, N_BIG_P), 0, 0)),
                  pl.BlockSpec((4, D_MODEL), lambda i: (0, 0)),
                  pl.BlockSpec((D_MODEL, ML_QKVO_COLS), lambda i: (0, 0)),
                  pl.BlockSpec((D_MODEL, HY_COLS), lambda i: (0, 0)),
                  pl.BlockSpec((ML_GATE_COLS, D_MODEL), lambda i: (0, 0)),
                  pl.BlockSpec((ML_GATE_COLS, 1), lambda i: (0, 0))],
        out_specs=(pl.BlockSpec((BIG_TILE, MAIN_COLS), lambda i: (i, 0)),
                   pl.BlockSpec((per, ML_GATE_COLS, TILE), lambda i: (i, 0, 0))),
        compiler_params=_cparams(("arbitrary",)),
        name="in_proj",
    )(xp, xs, mods3, g_norm, w_qkvo, w_hy, wgt, gbt)


ST_ROWS = ML_HEAD_DIM + 16


def _mlstm_kernel(*refs, seq_len, has_state):
    if has_state:
        (q_ref, k_ref, v_ref, o_ref, gt_ref, gain_ref, c0_ref, n0_ref, m0_ref,
         y_ref, c_ref, n_ref, m_ref, vt_ref, hf_ref, hb_ref, st_ref, ms_ref) = refs
    else:
        (q_ref, k_ref, v_ref, o_ref, gt_ref, gain_ref,
         y_ref, c_ref, n_ref, m_ref, vt_ref, hf_ref, hb_ref, st_ref, ms_ref) = refs
    ch = TILE
    nc = seq_len // ch
    hd = ML_HEAD_DIM
    key = lax.broadcasted_iota(jnp.int32, (ch, ch), 0)
    qry = lax.broadcasted_iota(jnp.int32, (ch, ch), 1)
    key_le = key <= qry
    key_ge = key >= qry
    t_le = jnp.where(key_le, 1.0, 0.0).astype(BF16)
    t_ge = jnp.where(key_ge, 1.0, 0.0).astype(BF16)
    sub16 = lax.broadcasted_iota(jnp.int32, (16, ch), 0)
    ln_scale = math.log(K_SCALE)

    for c in range(nc):
        for h in range(ML_HEADS):
            cols = slice(h * hd, (h + 1) * hd)
            vt_ref[c, cols, :] = v_ref[c * ch:(c + 1) * ch, cols].T

    for d in range(2):
        for h in range(ML_HEADS):
            r = d * ML_HEADS + h
            st_ref[r] = jnp.zeros((ST_ROWS, hd), F32)
            if has_state:
                st_ref[r, 0:hd, :] = c0_ref[0, d, h].T
                st_ref[r, hd:hd + 1, :] = n0_ref[0, d, h:h + 1, :]
                ms_ref[r] = jnp.broadcast_to(m0_ref[0, r:r + 1, :], (1, ch))
            else:
                ms_ref[r] = jnp.zeros((1, ch), F32)

    def step(t, carry):
        chains = [(d, h) for d in range(2) for h in range(ML_HEADS)]
        n = range(len(chains))
        cs = [t, nc - 1 - t]
        rowss = [pl.ds(pl.multiple_of(c * ch, ch), ch) for c in cs]
        grows = [gt_ref[c] for c in cs]
        brows = [_dot_exact_rhs(grows[d], t_le if d == 0 else t_ge) for d in range(2)]
        ccols = [_rows_to_cols(grows[d] - pltpu.roll(brows[d], ML_GATE_COLS - ML_HEADS, axis=0)) for d in range(2)]
        masks = [key_le, key_ge]
        haccs = [hf_ref, hb_ref]
        regs = [d * ML_HEADS + h for d, h in chains]
        colss = [slice(h * hd, (h + 1) * hd) for d, h in chains]
        qs = [q_ref[rowss[d], colss[i]] for i, (d, h) in enumerate(chains)]
        ks = [k_ref[rowss[d], colss[i]] for i, (d, h) in enumerate(chains)]
        vts = [vt_ref[cs[d], colss[i], :] for i, (d, h) in enumerate(chains)]
        sts = [st_ref[r] for r in regs]
        m_prevs = [ms_ref[r] for r in regs]
        b_rows = [brows[d][(1 + 2 * d) * ML_HEADS + h:(1 + 2 * d) * ML_HEADS + h + 1, :] for d, h in chains]
        ig_rows = [grows[d][2 * d * ML_HEADS + h:2 * d * ML_HEADS + h + 1, :] for d, h in chains]
        qks = [_dot_nt(k, q) for k, q in zip(ks, qs)]
        iqs = [_dot_nt(st.astype(BF16), q) for st, q in zip(sts, qs)]
        ss, sc_inters, m_poss = [], [], []
        for i, (d, h) in enumerate(chains):
            icol = 2 * d * ML_HEADS + h
            c_col = ccols[d][:, icol:icol + 1]
            logd = jnp.where(masks[d], b_rows[i] + c_col, -jnp.inf)
            inter = b_rows[i] + m_prevs[i]
            m_pos = jnp.maximum(inter, jnp.max(logd, axis=0, keepdims=True))
            ss.append(qks[i] * jnp.exp(logd - (m_pos - ln_scale)))
            sc_inters.append(jnp.exp(inter - m_pos))
            m_poss.append(m_pos)
        pvs = [_dot(vt, s.astype(BF16)) for vt, s in zip(vts, ss)]
        for i, (d, h) in enumerate(chains):
            num = sc_inters[i] * iqs[i][0:hd] + pvs[i]
            den = sc_inters[i] * iqs[i][hd:hd + 1] + jnp.sum(ss[i], axis=0, keepdims=True)
            haccs[d][cs[d], colss[i], :] = num * (1.0 / jnp.maximum(jnp.abs(den), jnp.exp(-m_poss[i])))
        lhss, decays = [], []
        for i, (d, h) in enumerate(chains):
            b_row = b_rows[i]
            b_last = b_row[:, ch - 1:ch] if d == 0 else b_row[:, 0:1]
            logw = b_last - b_row + ig_rows[i]
            m_new = jnp.maximum(b_last + m_prevs[i], jnp.max(logw, axis=1, keepdims=True))
            w = jnp.exp(logw - (m_new - ln_scale))
            decays.append(jnp.exp(b_last + m_prevs[i] - m_new))
            lhss.append(jnp.concatenate([(vts[i].astype(F32) * w).astype(BF16),
                                         jnp.where(sub16 == 0, w, 0.0).astype(BF16)], axis=0))
            ms_ref[regs[i]] = m_new
        upds = [_dot(lhs, k) for lhs, k in zip(lhss, ks)]
        for i in n:
            st_ref[regs[i]] = decays[i][:, 0:hd] * sts[i] + upds[i]
        return carry

    lax.fori_loop(0, nc, step, 0)

    for d in range(2):
        for h in range(ML_HEADS):
            r = d * ML_HEADS + h
            c_ref[0, d, h] = st_ref[r, 0:hd, :].T
            n_ref[0, d, h:h + 1, :] = st_ref[r, hd:hd + 1, :]
            m_ref[0, r:r + 1, :] = ms_ref[r][:, 0:hd]
    for c in range(nc):
        for h in range(ML_HEADS):
            cols = slice(h * hd, (h + 1) * hd)
            ht = hf_ref[c, cols, :] + hb_ref[c, cols, :]
            ht = ht * lax.rsqrt(jnp.mean(ht * ht, axis=0, keepdims=True) + EPS)
            rows = slice(c * ch, (c + 1) * ch)
            y = ht.T * gain_ref[:, cols] * jax.nn.sigmoid(o_ref[rows, cols].astype(F32))
            y_ref[rows, cols] = y.astype(BF16)


def _mlstm(proj, gates_t, gain, state, seq_len, n_seq, row_block_off):
    has_state = state is not None
    tiles = seq_len // TILE
    off = row_block_off
    qkvo_specs = [pl.BlockSpec((seq_len, ML_WIDTH), functools.partial(lambda b, j: (off + b, j), j=j))
                  for j in range(4)]
    in_specs = qkvo_specs + [
        pl.BlockSpec((tiles, ML_GATE_COLS, TILE), lambda b: (off + b, 0, 0)),
        pl.BlockSpec((1, ML_WIDTH), lambda b: (0, 0)),
    ]
    args = [proj, proj, proj, proj, gates_t, gain]
    if has_state:
        c0, n0, m0 = state
        in_specs += [
            pl.BlockSpec((1, 2, ML_HEADS, ML_HEAD_DIM, ML_HEAD_DIM), lambda b: (b, 0, 0, 0, 0)),
            pl.BlockSpec((1, 2, ML_HEADS, ML_HEAD_DIM), lambda b: (b, 0, 0, 0)),
            pl.BlockSpec((1, 2 * ML_HEADS, 1), lambda b: (b, 0, 0)),
        ]
        args += [c0, n0, m0]
    out_shape = (jax.ShapeDtypeStruct((n_seq * seq_len, ML_WIDTH), BF16),
                 jax.ShapeDtypeStruct((n_seq, 2, ML_HEADS, ML_HEAD_DIM, ML_HEAD_DIM), F32),
                 jax.ShapeDtypeStruct((n_seq, 2, ML_HEADS, ML_HEAD_DIM), F32),
                 jax.ShapeDtypeStruct((n_seq, 2 * ML_HEADS, ML_HEAD_DIM), F32))
    out_specs = (pl.BlockSpec((seq_len, ML_WIDTH), lambda b: (b, 0)),
                 pl.BlockSpec((1, 2, ML_HEADS, ML_HEAD_DIM, ML_HEAD_DIM), lambda b: (b, 0, 0, 0, 0)),
                 pl.BlockSpec((1, 2, ML_HEADS, ML_HEAD_DIM), lambda b: (b, 0, 0, 0)),
                 pl.BlockSpec((1, 2 * ML_HEADS, ML_HEAD_DIM), lambda b: (b, 0, 0)))
    scratch = [pltpu.VMEM((tiles, ML_WIDTH, TILE), BF16),
               pltpu.VMEM((tiles, ML_WIDTH, TILE), F32), pltpu.VMEM((tiles, ML_WIDTH, TILE), F32),
               pltpu.VMEM((2 * ML_HEADS, ST_ROWS, ML_HEAD_DIM), F32),
               pltpu.VMEM((2 * ML_HEADS, 1, TILE), F32)]
    return pl.pallas_call(
        functools.partial(_mlstm_kernel, seq_len=seq_len, has_state=has_state),
        out_shape=out_shape, grid=(n_seq,), in_specs=in_specs, out_specs=out_specs,
        scratch_shapes=scratch, compiler_params=_cparams(("arbitrary",)),
        name=f"mlstm_{seq_len}",
    )(*args)


def _dft_mats(seq_len):
    k = np.arange(seq_len, dtype=np.int64)[:, None]
    d = np.arange(seq_len, dtype=np.int64)[None, :]
    ang = np.pi * ((k * d) % (2 * seq_len)).astype(np.float64) / seq_len
    sinm = np.sin(ang)
    sinm[0, :] = np.where(d[0] % 2 == 0, 1.0, -1.0)
    f = np.concatenate([np.cos(ang), sinm], axis=0).astype(np.float32)
    return jnp.asarray(f).astype(BF16), jnp.asarray(np.ascontiguousarray(f.T)).astype(BF16)


def _filter_feats(seq_len):
    t = np.linspace(0.0, 1.0, seq_len, dtype=np.float64)[:, None]
    wpos = 2.0 * np.pi * np.arange(seq_len, dtype=np.float64)[:, None] / seq_len
    bands = np.linspace(1e-4, HY_BANDS - 1, HY_BANDS, dtype=np.float64)[None, :]
    z = np.concatenate([t, np.cos(bands * wpos), -np.sin(bands * wpos)], axis=-1)
    return jnp.asarray(np.pad(z, ((0, 0), (0, 128 - HY_EMB))).astype(np.float32))


def _filter_kernel(z_ref, w1_ref, b1_ref, w2_ref, b2_ref, w3_ref, b3_ref, dec_ref, f_ref,
                   a_ref, b_ref, d_ref, *, seq_len):
    n = 2 * seq_len
    oc = 2 * HY_WIDTH
    z = z_ref[...]
    h = jnp.sin(_dot3(z, w1_ref[...]) + b1_ref[...])
    h = jnp.sin(_dot3(h, w2_ref[...]) + b2_ref[...])
    t = z[:, 0:1]
    di = lax.broadcasted_iota(jnp.int32, (seq_len, 1), 0)
    sgn = jnp.where(di % 2 == 0, 1.0, -1.0)
    first = di == 0
    ssums, sdifs = [], []
    for o in range(HY_ORDER):
        cols = slice(o * oc, (o + 1) * oc)
        g = _dot3(h, w3_ref[:, cols]) + b3_ref[:, cols]
        g = g * (jnp.exp(-t * jnp.abs(dec_ref[:, cols])) + HY_MOD_SHIFT)
        ss = jnp.sum(g * g, axis=0, keepdims=True)
        inv = lax.rsqrt(ss[:, :HY_WIDTH] + ss[:, HY_WIDTH:] + EPS)
        hp = g[:, :HY_WIDTH] * inv
        hn = g[:, HY_WIDTH:] * inv
        ssums.append(hp + hn)
        sdifs.append(hp - hn)
    hcs = [_dot(f_ref[0:seq_len, :], s.astype(BF16)) for s in ssums]
    hss = [_dot(f_ref[seq_len:n, :], s.astype(BF16)) for s in sdifs]
    for o in range(HY_ORDER):
        nyq = jnp.sum(ssums[o] * sgn, axis=0, keepdims=True)
        a_ref[o] = hcs[o] * jnp.where(first, 1.0 / n, 2.0 / n)
        b_ref[o] = jnp.where(first, 0.0, hss[o] * (2.0 / n))
        d_ref[o] = jnp.where(first, nyq * (1.0 / n), hcs[o] * (2.0 / n))


def _hyena_filters(seq_len, f, w1p, b1, w2, b2, w3, b3, dec):
    z = _filter_feats(seq_len)
    out = jax.ShapeDtypeStruct((HY_ORDER, seq_len, HY_WIDTH), F32)
    return pl.pallas_call(
        functools.partial(_filter_kernel, seq_len=seq_len),
        out_shape=(out, out, out),
        compiler_params=pltpu.CompilerParams(vmem_limit_bytes=VMEM_LIMIT),
        name=f"hyena_filter_{seq_len}",
    )(z, w1p, b1, w2, b2, w3, b3, dec, f)


def _hyena_kernel(x1_ref, x2_ref, v_ref, cw1_ref, cw2_ref, cwv_ref, a_ref, b_ref, d_ref, bias_ref,
                  f_ref, ft_ref, z_ref, *, seq_len, width, seqs):
    rows = seqs * seq_len
    ti = lax.broadcasted_iota(jnp.int32, (rows, 1), 0)
    has_prev = (ti % width) != 0
    has_next = (ti % width) != (width - 1)

    def short_conv(x_ref, w_ref):
        x = x_ref[...].astype(F32)
        prev = jnp.where(has_prev, pltpu.roll(x, 1, axis=0), 0.0)
        nxt = jnp.where(has_next, pltpu.roll(x, rows - 1, axis=0), 0.0)
        return w_ref[0:1, :] * prev + w_ref[1:2, :] * x + w_ref[2:3, :] * nxt

    gates = (short_conv(x1_ref, cw1_ref), short_conv(x2_ref, cw2_ref))
    v = short_conv(v_ref, cwv_ref)
    sls = [slice(i * seq_len, (i + 1) * seq_len) for i in range(seqs)]
    zs = [v[sl] for sl in sls]
    for o in range(HY_ORDER):
        a, b, dd = a_ref[o], b_ref[o], d_ref[o]
        us = [_dot(f_ref[...], z.astype(BF16)) for z in zs]
        ys = []
        for u in us:
            ut = u[:seq_len]
            ub = u[seq_len:]
            ys.append(((ut * a - ub * b).astype(BF16), (ut * b + ub * dd).astype(BF16)))
        convs = [_dot(ft_ref[:, :seq_len], yt) + _dot(ft_ref[:, seq_len:], yb) for yt, yb in ys]
        zs = [gates[o][sl] * (y + bias_ref[o:o + 1, :] * z) for sl, y, z in zip(sls, convs, zs)]
    for sl, z in zip(sls, zs):
        z_ref[sl, :] = z.astype(BF16)


def _hyena(proj, conv_w, coefs, hy_bias, f, ft, seq_len, n_seq, row_off, width, seqs):
    cb = 256
    nblk = HY_WIDTH // cb
    base = ML_QKVO_COLS // cb
    rows = seqs * seq_len
    off = row_off // rows
    a, b, d = coefs

    def col_spec(part):
        return pl.BlockSpec((rows, cb), lambda j, s: (off + s, base + part * nblk + j))

    def w_spec(part):
        return pl.BlockSpec((3, cb), lambda j, s: (0, part * nblk + j))

    coef_spec = pl.BlockSpec((HY_ORDER, seq_len, cb), lambda j, s: (0, 0, j))
    return pl.pallas_call(
        functools.partial(_hyena_kernel, seq_len=seq_len, width=width, seqs=seqs),
        out_shape=jax.ShapeDtypeStruct((n_seq * seq_len, HY_WIDTH), BF16),
        grid=(nblk, n_seq // seqs),
        in_specs=[col_spec(0), col_spec(1), col_spec(2), w_spec(0), w_spec(1), w_spec(2),
                  coef_spec, coef_spec, coef_spec,
                  pl.BlockSpec((HY_ORDER, cb), lambda j, s: (0, j)),
                  pl.BlockSpec((2 * seq_len, seq_len), lambda j, s: (0, 0)),
                  pl.BlockSpec((seq_len, 2 * seq_len), lambda j, s: (0, 0))],
        out_specs=pl.BlockSpec((rows, cb), lambda j, s: (s, j)),
        compiler_params=_cparams(("arbitrary", "arbitrary")),
        name=f"hyena_conv_{seq_len}",
    )(proj, proj, proj, conv_w, conv_w, conv_w, a, b, d, hy_bias, f, ft)


def _first_max(x, n):
    mx = jnp.max(x, axis=0, keepdims=True)
    row = lax.broadcasted_iota(jnp.int32, x.shape, 0).astype(F32)
    idx = jnp.min(jnp.where(x == mx, row, float(n)), axis=0, keepdims=True)
    return mx, idx.astype(jnp.int32)


ROUTER_ROWS = 32
PAIRS_PER_GROUP = 6
N_BUCKETS = N_GROUPS * PAIRS_PER_GROUP
PAIR_SLOTS = ((0, 1), (0, 2), (0, 3), (1, 3), (1, 2), (3, 2))
LANES = 128
H2_EXT = D_MODEL + LANES
ROW_TILE = 256
ROW_CAP = T_ALL + N_BUCKETS * ROW_TILE
N_ROW_TILES = ROW_CAP // ROW_TILE


def _outproj_kernel(xp_ref, xs_ref, yp_ref, ys_ref, zp_ref, zs_ref, m_ref, gn_ref, wo_ref, wr_ref, br_ref,
                    x1_ref, h2_ref, bid_ref):
    is_p = pl.program_id(0) < N_BIG_P
    wrh, wrl = _split2(wr_ref[...])
    halves = [slice(r * TILE, (r + 1) * TILE) for r in range(BIG_TILE // TILE)]
    ys = [_dot(jnp.where(is_p, yp_ref[rows, :], ys_ref[rows, :]), wo_ref[0:ML_WIDTH, :])
          + _dot(jnp.where(is_p, zp_ref[rows, :], zs_ref[rows, :]), wo_ref[ML_WIDTH:, :]) for rows in halves]
    h2s = []
    for rows, y in zip(halves, ys):
        x = jnp.where(is_p, xp_ref[rows, :], xs_ref[rows, :])
        x1 = x + m_ref[0, 2:3, :] * _rms(y, gn_ref[1:2, :])
        x1_ref[rows, :] = x1
        h2 = _rms(x1, gn_ref[2:3, :]) * (1.0 + m_ref[0, 4:5, :]) + m_ref[0, 3:4, :]
        h2_ref[rows, 0:D_MODEL] = h2
        h2s.append(h2)
    wr2 = jnp.concatenate([wrh, wrl], axis=0)
    logits = []
    for h2 in h2s:
        h2h, h2l = _split2(h2)
        both = _dot_nt(wr2, h2h)
        logits.append(both[0:ROUTER_ROWS] + both[ROUTER_ROWS:] + _dot_nt(wrh, h2l) + br_ref[...])
    routed = [_route_tile(lg) for lg in logits]
    for r, (rows, (gate_rows, bucket)) in enumerate(zip(halves, routed)):
        h2_ref[rows, D_MODEL:H2_EXT] = jnp.zeros((TILE, LANES), F32)
        h2_ref[rows, D_MODEL:D_MODEL + 8] = _rows_to_cols(gate_rows)
        bid_ref[r] = bucket


def _route_tile(logits):
    lc = logits[0:N_GROUPS]
    mx, gi = _first_max(lc, N_GROUPS)
    p_grp = 1.0 / jnp.sum(jnp.exp(lc - mx), axis=0, keepdims=True)
    lsel = jnp.zeros((EXPERTS_PER_GROUP, TILE), F32)
    for g in range(N_GROUPS):
        lo = N_GROUPS + g * EXPERTS_PER_GROUP
        lsel = jnp.where(gi == g, logits[lo:lo + EXPERTS_PER_GROUP], lsel)
    l1, i1 = _first_max(lsel, EXPERTS_PER_GROUP)
    sub4 = lax.broadcasted_iota(jnp.int32, lsel.shape, 0)
    l2, i2 = _first_max(jnp.where(sub4 == i1, -jnp.inf, lsel), EXPERTS_PER_GROUP)
    e2 = jnp.exp(l2 - l1)
    w1 = p_grp / (1.0 + e2)
    w2 = p_grp * e2 / (1.0 + e2)
    lo_e = jnp.minimum(i1, i2)
    hi_e = jnp.maximum(i1, i2)
    pair = jnp.where(lo_e == 0, hi_e - 1, jnp.where(lo_e == 1, jnp.where(hi_e == 3, 3, 4), 5))
    slot_a = jnp.where(pair == 5, hi_e, lo_e)
    first_in_a = i1 == slot_a
    w_a = jnp.where(first_in_a, w1, w2)
    w_b = jnp.where(first_in_a, w2, w1)
    sub = lax.broadcasted_iota(jnp.int32, (8, TILE), 0)
    gate_rows = jnp.where(sub == 0, w_a, jnp.where(sub == 1, w_b, 0.0))
    return gate_rows, gi * PAIRS_PER_GROUP + pair


def _outproj(xp, xs, yp, ys, zp, zs, mods3, g_norm, w_out, w_r, b_r):
    tps = DEC_SEQ // BIG_TILE
    per = BIG_TILE // TILE
    pidx = lambda i: (jnp.minimum(i, N_BIG_P - 1), 0)
    sidx = lambda i: (jnp.maximum(i - N_BIG_P, 0), 0)
    return pl.pallas_call(
        _outproj_kernel,
        out_shape=(jax.ShapeDtypeStruct((T_ALL, D_MODEL), F32),
                   jax.ShapeDtypeStruct((T_ALL, H2_EXT), F32),
                   jax.ShapeDtypeStruct((N_TILES, 1, TILE), jnp.int32)),
        grid=(N_BIG,),
        in_specs=[pl.BlockSpec((BIG_TILE, D_MODEL), pidx), pl.BlockSpec((BIG_TILE, D_MODEL), sidx),
                  pl.BlockSpec((BIG_TILE, ML_WIDTH), pidx), pl.BlockSpec((BIG_TILE, ML_WIDTH), sidx),
                  pl.BlockSpec((BIG_TILE, HY_WIDTH), pidx), pl.BlockSpec((BIG_TILE, HY_WIDTH), sidx),
                  pl.BlockSpec((1, N_MOD, D_MODEL), lambda i: (_mod_row_of_tile(i, tps, N_BIG_P), 0, 0)),
                  pl.BlockSpec((4, D_MODEL), lambda i: (0, 0)),
                  pl.BlockSpec((D_MODEL, D_MODEL), lambda i: (0, 0)),
                  pl.BlockSpec((ROUTER_ROWS, D_MODEL), lambda i: (0, 0)),
                  pl.BlockSpec((ROUTER_ROWS, 1), lambda i: (0, 0))],
        out_specs=(pl.BlockSpec((BIG_TILE, D_MODEL), lambda i: (i, 0)),
                   pl.BlockSpec((BIG_TILE, H2_EXT), lambda i: (i, 0)),
                   pl.BlockSpec((per, 1, TILE), lambda i: (i, 0, 0))),
        compiler_params=_cparams(("arbitrary",)),
        name="out_proj_router",
    )(xp, xs, yp, ys, zp, zs, mods3, g_norm, w_out, w_r, b_r)


def _route_kernel(bid_ref, pos_ref, meta_ref):
    nb = 32
    tm = float(ROW_TILE)
    sub = lax.broadcasted_iota(jnp.int32, (nb, TILE), 0)
    ri = lax.broadcasted_iota(jnp.int32, (TILE, TILE), 0)
    ci = lax.broadcasted_iota(jnp.int32, (TILE, TILE), 1)
    before = jnp.where(ri < ci, 1.0, 0.0).astype(BF16)

    def onehot(blk):
        return jnp.where(sub == bid_ref[blk], 1.0, 0.0)

    zeros = jnp.zeros((nb, 1), F32)
    cnt = lax.fori_loop(0, N_TILES, lambda blk, c: c + jnp.sum(onehot(blk), axis=1, keepdims=True), zeros)
    padded = jnp.floor((cnt + (tm - 1.0)) * (1.0 / tm)) * tm
    r32 = lax.broadcasted_iota(jnp.int32, (nb, nb), 0)
    c32 = lax.broadcasted_iota(jnp.int32, (nb, nb), 1)
    padded_row = jnp.sum(jnp.where(r32 == c32, padded, 0.0), axis=0, keepdims=True)
    offs = jnp.sum(jnp.where(c32 < r32, padded_row, 0.0), axis=1, keepdims=True)
    ends = offs + padded

    def place(blk, seen):
        oh = onehot(blk)
        rank = _dot(oh.astype(BF16), before)
        pos = jnp.sum(oh * (rank + seen + offs), axis=0, keepdims=True)
        pos_ref[blk] = pos.astype(jnp.int32)
        return seen + jnp.sum(oh, axis=1, keepdims=True)

    lax.fori_loop(0, N_TILES, place, zeros)

    start = lax.broadcasted_iota(jnp.int32, (nb, 128), 1).astype(F32) * tm
    bsub = lax.broadcasted_iota(jnp.int32, (nb, 128), 0)
    done = jnp.where((bsub < N_BUCKETS) & (ends <= start), 1.0, 0.0)
    tb = jnp.sum(done, axis=0, keepdims=True)
    valid = jnp.where(tb < N_BUCKETS, 1.0, 0.0)
    tbc = jnp.minimum(tb, N_BUCKETS - 1.0)
    grp = jnp.floor((tbc + 0.5) * (1.0 / PAIRS_PER_GROUP))
    pair = tbc - PAIRS_PER_GROUP * grp
    loc_a = jnp.zeros_like(pair)
    loc_b = jnp.zeros_like(pair)
    for k, (sa, sb) in enumerate(PAIR_SLOTS):
        loc_a = jnp.where(pair == k, float(sa), loc_a)
        loc_b = jnp.where(pair == k, float(sb), loc_b)
    mine = bsub.astype(F32) == tbc
    used = jnp.sum(jnp.where(mine, offs + cnt, 0.0), axis=0, keepdims=True)
    n_rows = jnp.clip(used - start[0:1], 0.0, tm) * valid
    row8 = lax.broadcasted_iota(jnp.int32, (8, 128), 0)
    meta = jnp.where(row8 == 0, grp * EXPERTS_PER_GROUP + loc_a,
                     jnp.where(row8 == 1, grp * EXPERTS_PER_GROUP + loc_b,
                               jnp.where(row8 == 2, valid, jnp.where(row8 == 3, n_rows, 0.0))))
    meta_ref[...] = meta.astype(jnp.int32)


def _route(bid):
    return pl.pallas_call(
        _route_kernel,
        out_shape=(jax.ShapeDtypeStruct((N_TILES, 1, TILE), jnp.int32),
                   jax.ShapeDtypeStruct((8, 128), jnp.int32)),
        compiler_params=pltpu.CompilerParams(vmem_limit_bytes=VMEM_LIMIT),
        name="moe_route",
    )(bid)


def _moe_kernel(meta_ref, pos_ref, h2_hbm, wga_ref, wua_ref, wda_ref, wgb_ref, wub_ref, wdb_ref,
                y_ref, src_ref, xbuf, sem, wga_s, wua_s, wda_s, wgb_s, wub_s, wdb_s):
    j = pl.program_id(0)

    def row_copy(tile, r, slot):
        tok = src_ref[tile * ROW_TILE + r]
        return pltpu.make_async_copy(h2_hbm.at[pl.ds(tok, 1), :], xbuf.at[slot, pl.ds(r, 1), :], sem.at[slot])

    def issue_rows(tile, slot, lo, hi):
        for r in range(lo, hi):
            row_copy(tile, r, slot).start()

    group = 8

    def row_groups(tile):
        return (meta_ref[3, tile] + (group - 1)) // group

    def issue_counted(tile, slot):
        def body(g, c):
            for k in range(group):
                row_copy(tile, g * group + k, slot).start()
            return c
        lax.fori_loop(0, row_groups(tile), body, 0)

    def wait_counted(tile, slot):
        def body(g, c):
            for k in range(group):
                row_copy(tile, g * group + k, slot).wait()
            return c
        lax.fori_loop(0, row_groups(tile), body, 0)

    def wait_full(slot):
        pltpu.make_async_copy(h2_hbm.at[pl.ds(0, ROW_TILE), :], xbuf.at[slot], sem.at[slot]).wait()

    @pl.when(j == 0)
    def _():
        xbuf[...] = jnp.zeros_like(xbuf)

        def clear(t, c):
            n = meta_ref[3, t]
            for k in range(group - 1):
                src_ref[t * ROW_TILE + jnp.minimum(n + k, ROW_TILE - 1)] = 0
            return c
        lax.fori_loop(0, N_ROW_TILES, clear, 0)

        def invert(t, c):
            src_ref[pos_ref[t]] = t
            return c
        lax.fori_loop(0, T_ALL, invert, 0, unroll=8)

        @pl.when(meta_ref[2, 0] == 1)
        def _():
            issue_counted(0, 0)

    nxt = jnp.minimum(j + 1, N_ROW_TILES - 1)
    has_next = (j + 1 < N_ROW_TILES) & (meta_ref[2, nxt] == 1)
    next_full = has_next & (meta_ref[3, nxt] == ROW_TILE)
    valid = meta_ref[2, j] == 1
    full = meta_ref[3, j] == ROW_TILE
    prev = jnp.maximum(j - 1, 0)

    @pl.when(valid & has_next & jnp.logical_not(next_full))
    def _():
        issue_counted(nxt, nxt % 2)

    @pl.when(valid & full)
    def _():
        wait_full(j % 2)

    @pl.when(valid & jnp.logical_not(full))
    def _():
        wait_counted(j, j % 2)

    @pl.when(valid & ((j == 0) | (meta_ref[0, j] != meta_ref[0, prev])))
    def _():
        wga_s[...] = wga_ref[0].astype(BF16)
        wua_s[...] = wua_ref[0].astype(BF16)
        wda_s[...] = wda_ref[0].astype(BF16)

    @pl.when(valid & ((j == 0) | (meta_ref[1, j] != meta_ref[1, prev])))
    def _():
        wgb_s[...] = wgb_ref[0].astype(BF16)
        wub_s[...] = wub_ref[0].astype(BF16)
        wdb_s[...] = wdb_ref[0].astype(BF16)

    def compute(fetch_next):
        slot = j % 2
        nslot = nxt % 2
        step = ROW_TILE // 8
        batches = iter(range(0, ROW_TILE, step))

        def fetch():
            if fetch_next:
                lo = next(batches)
                issue_rows(nxt, nslot, lo, lo + step)

        x = xbuf[slot, :, 0:D_MODEL].astype(BF16)
        gates = xbuf[slot, :, D_MODEL:H2_EXT]
        hg_a = _dot(x, wga_s[...])
        fetch()
        hu_a = _dot(x, wua_s[...])
        fetch()
        hg_b = _dot(x, wgb_s[...])
        fetch()
        hu_b = _dot(x, wub_s[...])
        fetch()
        act_a = (hg_a * jax.nn.sigmoid(hg_a) * hu_a * gates[:, 0:1]).astype(BF16)
        fetch()
        act_b = (hg_b * jax.nn.sigmoid(hg_b) * hu_b * gates[:, 1:2]).astype(BF16)
        fetch()
        y = _dot(act_a, wda_s[...])
        fetch()
        y = y + _dot(act_b, wdb_s[...])
        fetch()
        y_ref[...] = y

    @pl.when(valid & next_full)
    def _():
        compute(True)

    @pl.when(valid & jnp.logical_not(next_full))
    def _():
        compute(False)

    @pl.when(jnp.logical_not(valid))
    def _():
        y_ref[...] = jnp.zeros_like(y_ref)


def _moe(meta, pos, h2ext, w_gate, w_up, w_down):
    up_spec = lambda slot: pl.BlockSpec((1, D_MODEL, EXPERT_FF), lambda j, meta, pos: (meta[slot, j], 0, 0))
    down_spec = lambda slot: pl.BlockSpec((1, EXPERT_FF, D_MODEL), lambda j, meta, pos: (meta[slot, j], 0, 0))
    grid_spec = pltpu.PrefetchScalarGridSpec(
        num_scalar_prefetch=2,
        grid=(N_ROW_TILES,),
        in_specs=[pl.BlockSpec(memory_space=pl.ANY),
                  up_spec(0), up_spec(0), down_spec(0), up_spec(1), up_spec(1), down_spec(1)],
        out_specs=pl.BlockSpec((ROW_TILE, D_MODEL), lambda j, meta, pos: (j, 0)),
        scratch_shapes=[pltpu.SMEM((ROW_CAP,), jnp.int32),
                        pltpu.VMEM((2, ROW_TILE, H2_EXT), F32),
                        pltpu.SemaphoreType.DMA((2,)),
                        pltpu.VMEM((D_MODEL, EXPERT_FF), BF16), pltpu.VMEM((D_MODEL, EXPERT_FF), BF16),
                        pltpu.VMEM((EXPERT_FF, D_MODEL), BF16),
                        pltpu.VMEM((D_MODEL, EXPERT_FF), BF16), pltpu.VMEM((D_MODEL, EXPERT_FF), BF16),
                        pltpu.VMEM((EXPERT_FF, D_MODEL), BF16)])
    return pl.pallas_call(
        _moe_kernel,
        out_shape=jax.ShapeDtypeStruct((ROW_CAP, D_MODEL), F32),
        grid_spec=grid_spec,
        compiler_params=_cparams(("arbitrary",)),
        name="moe_experts",
    )(meta, pos, h2ext, w_gate, w_up, w_down, w_gate, w_up, w_down)


def _final_kernel(pos_ref, y_hbm, x1_ref, m_ref, gn_ref, op_ref, os_ref, ybuf, sem):
    i = pl.program_id(0)

    def row_copy(tile, r, slot):
        p = pos_ref[tile * TILE + r]
        return pltpu.make_async_copy(y_hbm.at[pl.ds(p, 1), :], ybuf.at[slot, pl.ds(r, 1), :], sem.at[slot])

    def issue(tile, slot):
        def body(r2, c):
            row_copy(tile, 2 * r2, slot).start(priority=0)
            row_copy(tile, 2 * r2 + 1, slot).start(priority=1)
            return c
        lax.fori_loop(0, TILE // 2, body, 0, unroll=4)

    def wait(slot):
        pltpu.make_async_copy(y_hbm.at[pl.ds(0, TILE)], ybuf.at[slot], sem.at[slot]).wait()

    @pl.when(i == 0)
    def _():
        issue(0, 0)

    @pl.when(i + 1 < N_TILES)
    def _():
        issue(i + 1, (i + 1) % 2)

    slot = i % 2
    wait(slot)
    out = x1_ref[...] + m_ref[0, 5:6, :] * _rms(ybuf[slot], gn_ref[3:4, :])

    @pl.when(i < N_TILES_P)
    def _():
        op_ref[...] = out

    @pl.when(i >= N_TILES_P)
    def _():
        os_ref[...] = out


def _final(pos, y_sorted, x1, mods3, g_norm):
    tps = DEC_SEQ // TILE
    grid_spec = pltpu.PrefetchScalarGridSpec(
        num_scalar_prefetch=1,
        grid=(N_TILES,),
        in_specs=[pl.BlockSpec(memory_space=pl.ANY),
                  pl.BlockSpec((TILE, D_MODEL), lambda i, pos: (i, 0)),
                  pl.BlockSpec((1, N_MOD, D_MODEL), lambda i, pos: (_mod_row_of_tile(i, tps, N_TILES_P), 0, 0)),
                  pl.BlockSpec((4, D_MODEL), lambda i, pos: (0, 0))],
        out_specs=(pl.BlockSpec((TILE, D_MODEL), lambda i, pos: (jnp.minimum(i, N_TILES_P - 1), 0)),
                   pl.BlockSpec((TILE, D_MODEL), lambda i, pos: (jnp.maximum(i - N_TILES_P, 0), 0))),
        scratch_shapes=[pltpu.VMEM((2, TILE, D_MODEL), F32), pltpu.SemaphoreType.DMA((2,))])
    return pl.pallas_call(
        _final_kernel,
        out_shape=(jax.ShapeDtypeStruct((T_PROMPT, D_MODEL), F32),
                   jax.ShapeDtypeStruct((T_SAMPLE, D_MODEL), F32)),
        grid_spec=grid_spec,
        compiler_params=_cparams(("arbitrary",)),
        name="moe_combine_final",
    )(pos, y_sorted, x1, mods3, g_norm)


def kernel(x_prompt, x_sample, state_C, state_n, state_m, c, c_ctx, w_ada, b_ada, g_norm, w_in, ml_gate_bias, ml_head_gain, hy_conv_w, hy_f_w1, hy_f_b1, hy_f_w2, hy_f_b2, hy_f_w3, hy_f_b3, hy_decay, hy_bias, w_out, w_rc, b_rc, w_rf, b_rf, w_gate, w_up, w_down):
    xp = x_prompt.reshape(T_PROMPT, D_MODEL)
    xs = x_sample.reshape(T_SAMPLE, D_MODEL)
    gn = g_norm[0]

    cv = jnp.concatenate([c_ctx[None, :], c, jnp.zeros((MOD_ROWS - 1 - DEC_BATCH, D_MODEL), F32)], axis=0)
    mods3 = _ada(cv, w_ada[0], b_ada[0]).reshape(MOD_ROWS, N_MOD, D_MODEL)

    w_in0 = w_in[0]
    w_qkvo, w_hy = _prep_in_weights(w_in0.T)
    wg = w_in0[:, ML_QKVO_COLS:ML_QKVO_COLS + ML_GATE_COLS]
    gbt = ml_gate_bias[0].reshape(ML_GATE_COLS, 1)
    proj, gates_t = _inproj(xp, xs, mods3, gn, w_qkvo, w_hy, wg.T, gbt)

    gain = ml_head_gain[0].reshape(1, ML_WIDTH)
    y_ml_p, c_new, n_new, m_new = _mlstm(proj, gates_t, gain, None, SEQ, BATCH, 0)
    state = (state_C[:, 0], state_n[:, 0], state_m[:, 0].reshape(DEC_BATCH, 2 * ML_HEADS, 1))
    y_ml_s, _, _, _ = _mlstm(proj, gates_t, gain, state, DEC_SEQ, DEC_BATCH, T_PROMPT // DEC_SEQ)

    w1p = jnp.pad(hy_f_w1[0], ((0, 128 - HY_EMB), (0, 0)))
    b1 = hy_f_b1[0].reshape(1, -1)
    b2 = hy_f_b2[0].reshape(1, -1)
    b3 = hy_f_b3[0].reshape(1, -1)
    dec = hy_decay[0].reshape(1, -1)
    z_parts = []
    for seq_len, n_seq, row_off, width, seqs in ((SEQ, BATCH, 0, SEQ, 4), (DEC_SEQ, DEC_BATCH, T_PROMPT, GRID_W, 2)):
        f, ft = _dft_mats(seq_len)
        coefs = _hyena_filters(seq_len, f, w1p, b1, hy_f_w2[0], b2, hy_f_w3[0], b3, dec)
        z_parts.append(_hyena(proj, hy_conv_w[0], coefs, hy_bias[0], f, ft, seq_len, n_seq, row_off, width, seqs))
    z_p, z_s = z_parts

    pad_r = ROUTER_ROWS - N_GROUPS - N_EXPERTS
    w_r = jnp.pad(jnp.concatenate([w_rc[0], w_rf[0]], axis=1).T, ((0, pad_r), (0, 0)))
    b_r = jnp.pad(jnp.concatenate([b_rc[0], b_rf[0]], axis=0), (0, pad_r)).reshape(ROUTER_ROWS, 1)
    x1, h2ext, bid = _outproj(xp, xs, y_ml_p, y_ml_s, z_p, z_s, mods3, gn, w_out[0].astype(BF16), w_r, b_r)

    pos3, meta = _route(bid)
    pos = pos3.reshape(T_ALL)
    y_sorted = _moe(meta, pos, h2ext, w_gate[0], w_up[0], w_down[0])
    y_p, y_s = _final(pos, y_sorted, x1, mods3, gn)

    new_c = c_new.reshape(BATCH, 1, 2, ML_HEADS, ML_HEAD_DIM, ML_HEAD_DIM)
    new_n = n_new.reshape(BATCH, 1, 2, ML_HEADS, ML_HEAD_DIM)
    new_m = m_new[:, :, 0].reshape(BATCH, 1, 2, ML_HEADS)
    return (y_p.reshape(BATCH, SEQ, D_MODEL), y_s.reshape(DEC_BATCH, DEC_SEQ, D_MODEL), new_c, new_n, new_m)
```

```python
import functools
import math

import jax
import jax.numpy as jnp
import numpy as np
from jax import lax
from jax.experimental import pallas as pl
from jax.experimental.pallas import tpu as pltpu

F32 = jnp.float32
BF16 = jnp.bfloat16

D_MODEL = 1024
BATCH = 16
SEQ = 256
DEC_BATCH = 4
DEC_SEQ = 1024
GRID_W = 64
ML_WIDTH = 512
ML_HEADS = 4
ML_HEAD_DIM = 128
HY_WIDTH = 512
HY_ORDER = 2
HY_EMB = 33
HY_BANDS = 16
HY_FILTER_HIDDEN = 64
HY_MOD_SHIFT = 0.05
N_GROUPS = 4
EXPERTS_PER_GROUP = 4
N_EXPERTS = 16
EXPERT_FF = 512
N_MOD = 6
EPS = 1e-6
ML_QKVO_COLS = 4 * ML_WIDTH
ML_GATE_COLS = 4 * ML_HEADS
HY_COLS = 3 * HY_WIDTH
MAIN_COLS = ML_QKVO_COLS + HY_COLS

T_PROMPT = BATCH * SEQ
T_SAMPLE = DEC_BATCH * DEC_SEQ
T_ALL = T_PROMPT + T_SAMPLE
TILE = 256
N_TILES_P = T_PROMPT // TILE
N_TILES = T_ALL // TILE
MOD_ROWS = 8
K_SCALE = ML_HEAD_DIM ** -0.5
VMEM_LIMIT = 56 * 1024 * 1024


def _cparams(sem):
    return pltpu.CompilerParams(dimension_semantics=sem, vmem_limit_bytes=VMEM_LIMIT)


def _split2(x):
    hi = x.astype(BF16)
    lo = (x - hi.astype(F32)).astype(BF16)
    return hi, lo


def _dot(a, b):
    return jnp.dot(a, b, preferred_element_type=F32)


def _dot_nt(a, b):
    return lax.dot_general(a, b, (((1,), (1,)), ((), ())), preferred_element_type=F32)


def _dot3(a, b):
    ah, al = _split2(a)
    bh, bl = _split2(b)
    return _dot(ah, bh) + _dot(al, bh) + _dot(ah, bl)


def _dot_exact_rhs(x, t):
    x1 = x.astype(BF16)
    r1 = x - x1.astype(F32)
    x2 = r1.astype(BF16)
    x3 = (r1 - x2.astype(F32)).astype(BF16)
    return _dot(x1, t) + _dot(x2, t) + _dot(x3, t)


def _rms(x, g):
    return x * lax.rsqrt(jnp.mean(x * x, axis=-1, keepdims=True) + EPS) * g


def _mod_row_of_tile(i, tiles_per_sample_seq, n_prompt_tiles):
    return jnp.where(i < n_prompt_tiles, 0, 1 + (i - n_prompt_tiles) // tiles_per_sample_seq)


def _ada_kernel(cv_ref, w_ref, b_ref, o_ref):
    cv = cv_ref[...]
    s = cv * jax.nn.sigmoid(cv)
    sh, sl = _split2(s)
    wh, wl = _split2(w_ref[...])
    both = _dot(jnp.concatenate([sh.astype(F32), sl.astype(F32)], axis=0).astype(BF16), wh)
    o_ref[...] = both[0:MOD_ROWS] + both[MOD_ROWS:] + _dot(sh, wl) + b_ref[...]


def _ada(cv, w_ada, b_ada):
    n = N_MOD * D_MODEL
    return pl.pallas_call(
        _ada_kernel,
        out_shape=jax.ShapeDtypeStruct((MOD_ROWS, n), F32),
        grid=(N_MOD,),
        in_specs=[pl.BlockSpec((MOD_ROWS, D_MODEL), lambda j: (0, 0)),
                  pl.BlockSpec((D_MODEL, D_MODEL), lambda j: (0, j)),
                  pl.BlockSpec((1, D_MODEL), lambda j: (0, j))],
        out_specs=pl.BlockSpec((MOD_ROWS, D_MODEL), lambda j: (0, j)),
        compiler_params=_cparams(("arbitrary",)),
        name="ada_mod",
    )(cv, w_ada, b_ada.reshape(1, n))


PREP_COLS = 512


def _prep_q_kernel(wt_ref, o_ref):
    o_ref[...] = wt_ref[...].T.astype(BF16)


def _prep_hy_kernel(wt_hbm, o_ref, buf, sem):
    start = pl.multiple_of(ML_QKVO_COLS + ML_GATE_COLS + pl.program_id(0) * PREP_COLS, 8)
    copy = pltpu.make_async_copy(wt_hbm.at[pl.ds(start, PREP_COLS), :], buf, sem)
    copy.start()
    copy.wait()
    o_ref[...] = buf[...].T.astype(BF16)


def _prep_in_weights(w_in_t):
    out_blk = pl.BlockSpec((D_MODEL, PREP_COLS), lambda j: (0, j))
    w_qkvo = pl.pallas_call(
        _prep_q_kernel,
        out_shape=jax.ShapeDtypeStruct((D_MODEL, ML_QKVO_COLS), BF16),
        grid=(ML_QKVO_COLS // PREP_COLS,),
        in_specs=[pl.BlockSpec((PREP_COLS, D_MODEL), lambda j: (j, 0))], out_specs=out_blk,
        compiler_params=_cparams(("arbitrary",)), name="prep_w_qkvo",
    )(w_in_t)
    w_hy = pl.pallas_call(
        _prep_hy_kernel,
        out_shape=jax.ShapeDtypeStruct((D_MODEL, HY_COLS), BF16),
        grid=(HY_COLS // PREP_COLS,),
        in_specs=[pl.BlockSpec(memory_space=pl.ANY)], out_specs=out_blk,
        scratch_shapes=[pltpu.VMEM((PREP_COLS, D_MODEL), F32), pltpu.SemaphoreType.DMA(())],
        compiler_params=_cparams(("arbitrary",)), name="prep_w_hy",
    )(w_in_t)
    return w_qkvo, w_hy


def _log_sigmoid(x):
    return jnp.minimum(x, 0.0) - jnp.log1p(jnp.exp(-jnp.abs(x)))


def _rows_to_cols(rows):
    ri = lax.broadcasted_iota(jnp.int32, (TILE, TILE), 0)
    ci = lax.broadcasted_iota(jnp.int32, (TILE, TILE), 1)
    eye = jnp.where(ri == ci, 1.0, 0.0).astype(BF16)
    p1 = rows.astype(BF16)
    r1 = rows - p1.astype(F32)
    p2 = r1.astype(BF16)
    p3 = (r1 - p2.astype(F32)).astype(BF16)
    return _dot_nt(eye, p1) + _dot_nt(eye, p2) + _dot_nt(eye, p3)


BIG_TILE = 4 * TILE
N_BIG_P = T_PROMPT // BIG_TILE
N_BIG = T_ALL // BIG_TILE


def _inproj_kernel(xp_ref, xs_ref, m_ref, gn_ref, wq_ref, wh_ref, wgt_ref, gbt_ref, proj_ref, gatet_ref):
    is_p = pl.program_id(0) < N_BIG_P
    halves = [slice(r * TILE, (r + 1) * TILE) for r in range(BIG_TILE // TILE)]
    hs = [_rms(jnp.where(is_p, xp_ref[rows, :], xs_ref[rows, :]), gn_ref[0:1, :]) * (1.0 + m_ref[0, 1:2, :])
          + m_ref[0, 0:1, :] for rows in halves]
    hbs = [h.astype(BF16) for h in hs]
    cb = 1024
    for j in range(ML_QKVO_COLS // cb):
        for rows, hb in zip(halves, hbs):
            proj_ref[rows, j * cb:(j + 1) * cb] = _dot(hb, wq_ref[:, j * cb:(j + 1) * cb]).astype(BF16)
    cb = 512
    for j in range(HY_COLS // cb):
        lo = ML_QKVO_COLS + j * cb
        for rows, hb in zip(halves, hbs):
            proj_ref[rows, lo:lo + cb] = _dot(hb, wh_ref[:, j * cb:(j + 1) * cb]).astype(BF16)
    wth, wtl = _split2(wgt_ref[...])
    wt2 = jnp.concatenate([wth, wtl], axis=0)
    gts = []
    for h, hb in zip(hs, hbs):
        hl = (h - hb.astype(F32)).astype(BF16)
        both = _dot_nt(wt2, hb)
        gt = both[0:ML_GATE_COLS] + both[ML_GATE_COLS:] + _dot_nt(wth, hl) + gbt_ref[...]
        row = lax.broadcasted_iota(jnp.int32, gt.shape, 0)
        gts.append(jnp.where((row % 8) >= 4, _log_sigmoid(gt), gt))
    for r, gt in enumerate(gts):
        gatet_ref[r] = gt


def _inproj(xp, xs, mods3, g_norm, w_qkvo, w_hy, wgt, gbt):
    tps = DEC_SEQ // BIG_TILE
    per = BIG_TILE // TILE
    return pl.pallas_call(
        _inproj_kernel,
        out_shape=(jax.ShapeDtypeStruct((T_ALL, MAIN_COLS), BF16),
                   jax.ShapeDtypeStruct((N_TILES, ML_GATE_COLS, TILE), F32)),
        grid=(N_BIG,),
        in_specs=[pl.BlockSpec((BIG_TILE, D_MODEL), lambda i: (jnp.minimum(i, N_BIG_P - 1), 0)),
                  pl.BlockSpec((BIG_TILE, D_MODEL), lambda i: (jnp.maximum(i - N_BIG_P, 0), 0)),
                  pl.BlockSpec((1, N_MOD, D_MODEL), lambda i: (_mod_row_of_tile(i, tps, N_BIG_P), 0, 0)),
                  pl.BlockSpec((4, D_MODEL), lambda i: (0, 0)),
                  pl.BlockSpec((D_MODEL, ML_QKVO_COLS), lambda i: (0, 0)),
                  pl.BlockSpec((D_MODEL, HY_COLS), lambda i: (0, 0)),
                  pl.BlockSpec((ML_GATE_COLS, D_MODEL), lambda i: (0, 0)),
                  pl.BlockSpec((ML_GATE_COLS, 1), lambda i: (0, 0))],
        out_specs=(pl.BlockSpec((BIG_TILE, MAIN_COLS), lambda i: (i, 0)),
                   pl.BlockSpec((per, ML_GATE_COLS, TILE), lambda i: (i, 0, 0))),
        compiler_params=_cparams(("arbitrary",)),
        name="in_proj",
    )(xp, xs, mods3, g_norm, w_qkvo, w_hy, wgt, gbt)


ST_ROWS = ML_HEAD_DIM + 16


def _mlstm_kernel(*refs, seq_len, has_state):
    if has_state:
        (q_ref, k_ref, v_ref, o_ref, gt_ref, gain_ref, c0_ref, n0_ref, m0_ref,
         y_ref, c_ref, n_ref, m_ref, vt_ref, hf_ref, hb_ref, st_ref, ms_ref) = refs
    else:
        (q_ref, k_ref, v_ref, o_ref, gt_ref, gain_ref,
         y_ref, c_ref, n_ref, m_ref, vt_ref, hf_ref, hb_ref, st_ref, ms_ref) = refs
    ch = TILE
    nc = seq_len // ch
    hd = ML_HEAD_DIM
    key = lax.broadcasted_iota(jnp.int32, (ch, ch), 0)
    qry = lax.broadcasted_iota(jnp.int32, (ch, ch), 1)
    key_le = key <= qry
    key_ge = key >= qry
    t_le = jnp.where(key_le, 1.0, 0.0).astype(BF16)
    t_ge = jnp.where(key_ge, 1.0, 0.0).astype(BF16)
    sub16 = lax.broadcasted_iota(jnp.int32, (16, ch), 0)
    ln_scale = math.log(K_SCALE)

    for c in range(nc):
        for h in range(ML_HEADS):
            cols = slice(h * hd, (h + 1) * hd)
            vt_ref[c, cols, :] = v_ref[c * ch:(c + 1) * ch, cols].T

    for d in range(2):
        for h in range(ML_HEADS):
            r = d * ML_HEADS + h
            st_ref[r] = jnp.zeros((ST_ROWS, hd), F32)
            if has_state:
                st_ref[r, 0:hd, :] = c0_ref[0, d, h].T
                st_ref[r, hd:hd + 1, :] = n0_ref[0, d, h:h + 1, :]
                ms_ref[r] = jnp.broadcast_to(m0_ref[0, r:r + 1, :], (1, ch))
            else:
                ms_ref[r] = jnp.zeros((1, ch), F32)

    def step(t, carry):
        chains = [(d, h) for d in range(2) for h in range(ML_HEADS)]
        n = range(len(chains))
        cs = [t, nc - 1 - t]
        rowss = [pl.ds(pl.multiple_of(c * ch, ch), ch) for c in cs]
        grows = [gt_ref[c] for c in cs]
        brows = [_dot_exact_rhs(grows[d], t_le if d == 0 else t_ge) for d in range(2)]
        ccols = [_rows_to_cols(grows[d] - pltpu.roll(brows[d], ML_GATE_COLS - ML_HEADS, axis=0)) for d in range(2)]
        masks = [key_le, key_ge]
        haccs = [hf_ref, hb_ref]
        regs = [d * ML_HEADS + h for d, h in chains]
        colss = [slice(h * hd, (h + 1) * hd) for d, h in chains]
        qs = [q_ref[rowss[d], colss[i]] for i, (d, h) in enumerate(chains)]
        ks = [k_ref[rowss[d], colss[i]] for i, (d, h) in enumerate(chains)]
        vts = [vt_ref[cs[d], colss[i], :] for i, (d, h) in enumerate(chains)]
        sts = [st_ref[r] for r in regs]
        m_prevs = [ms_ref[r] for r in regs]
        b_rows = [brows[d][(1 + 2 * d) * ML_HEADS + h:(1 + 2 * d) * ML_HEADS + h + 1, :] for d, h in chains]
        ig_rows = [grows[d][2 * d * ML_HEADS + h:2 * d * ML_HEADS + h + 1, :] for d, h in chains]
        qks = [_dot_nt(k, q) for k, q in zip(ks, qs)]
        iqs = [_dot_nt(st.astype(BF16), q) for st, q in zip(sts, qs)]
        ss, sc_inters, m_poss = [], [], []
        for i, (d, h) in enumerate(chains):
            icol = 2 * d * ML_HEADS + h
            c_col = ccols[d][:, icol:icol + 1]
            logd = jnp.where(masks[d], b_rows[i] + c_col, -jnp.inf)
            inter = b_rows[i] + m_prevs[i]
            m_pos = jnp.maximum(inter, jnp.max(logd, axis=0, keepdims=True))
            ss.append(qks[i] * jnp.exp(logd - (m_pos - ln_scale)))
            sc_inters.append(jnp.exp(inter - m_pos))
            m_poss.append(m_pos)
        pvs = [_dot(vt, s.astype(BF16)) for vt, s in zip(vts, ss)]
        for i, (d, h) in enumerate(chains):
            num = sc_inters[i] * iqs[i][0:hd] + pvs[i]
            den = sc_inters[i] * iqs[i][hd:hd + 1] + jnp.sum(ss[i], axis=0, keepdims=True)
            haccs[d][cs[d], colss[i], :] = num * (1.0 / jnp.maximum(jnp.abs(den), jnp.exp(-m_poss[i])))
        lhss, decays = [], []
        for i, (d, h) in enumerate(chains):
            b_row = b_rows[i]
            b_last = b_row[:, ch - 1:ch] if d == 0 else b_row[:, 0:1]
            logw = b_last - b_row + ig_rows[i]
            m_new = jnp.maximum(b_last + m_prevs[i], jnp.max(logw, axis=1, keepdims=True))
            w = jnp.exp(logw - (m_new - ln_scale))
            decays.append(jnp.exp(b_last + m_prevs[i] - m_new))
            lhss.append(jnp.concatenate([(vts[i].astype(F32) * w).astype(BF16),
                                         jnp.where(sub16 == 0, w, 0.0).astype(BF16)], axis=0))
            ms_ref[regs[i]] = m_new
        upds = [_dot(lhs, k) for lhs, k in zip(lhss, ks)]
        for i in n:
            st_ref[regs[i]] = decays[i][:, 0:hd] * sts[i] + upds[i]
        return carry

    lax.fori_loop(0, nc, step, 0)

    for d in range(2):
        for h in range(ML_HEADS):
            r = d * ML_HEADS + h
            c_ref[0, d, h] = st_ref[r, 0:hd, :].T
            n_ref[0, d, h:h + 1, :] = st_ref[r, hd:hd + 1, :]
            m_ref[0, r:r + 1, :] = ms_ref[r][:, 0:hd]
    for c in range(nc):
        for h in range(ML_HEADS):
            cols = slice(h * hd, (h + 1) * hd)
            ht = hf_ref[c, cols, :] + hb_ref[c, cols, :]
            ht = ht * lax.rsqrt(jnp.mean(ht * ht, axis=0, keepdims=True) + EPS)
            rows = slice(c * ch, (c + 1) * ch)
            y = ht.T * gain_ref[:, cols] * jax.nn.sigmoid(o_ref[rows, cols].astype(F32))
            y_ref[rows, cols] = y.astype(BF16)


def _mlstm(proj, gates_t, gain, state, seq_len, n_seq, row_block_off):
    has_state = state is not None
    tiles = seq_len // TILE
    off = row_block_off
    qkvo_specs = [pl.BlockSpec((seq_len, ML_WIDTH), functools.partial(lambda b, j: (off + b, j), j=j))
                  for j in range(4)]
    in_specs = qkvo_specs + [
        pl.BlockSpec((tiles, ML_GATE_COLS, TILE), lambda b: (off + b, 0, 0)),
        pl.BlockSpec((1, ML_WIDTH), lambda b: (0, 0)),
    ]
    args = [proj, proj, proj, proj, gates_t, gain]
    if has_state:
        c0, n0, m0 = state
        in_specs += [
            pl.BlockSpec((1, 2, ML_HEADS, ML_HEAD_DIM, ML_HEAD_DIM), lambda b: (b, 0, 0, 0, 0)),
            pl.BlockSpec((1, 2, ML_HEADS, ML_HEAD_DIM), lambda b: (b, 0, 0, 0)),
            pl.BlockSpec((1, 2 * ML_HEADS, 1), lambda b: (b, 0, 0)),
        ]
        args += [c0, n0, m0]
    out_shape = (jax.ShapeDtypeStruct((n_seq * seq_len, ML_WIDTH), BF16),
                 jax.ShapeDtypeStruct((n_seq, 2, ML_HEADS, ML_HEAD_DIM, ML_HEAD_DIM), F32),
                 jax.ShapeDtypeStruct((n_seq, 2, ML_HEADS, ML_HEAD_DIM), F32),
                 jax.ShapeDtypeStruct((n_seq, 2 * ML_HEADS, ML_HEAD_DIM), F32))
    out_specs = (pl.BlockSpec((seq_len, ML_WIDTH), lambda b: (b, 0)),
                 pl.BlockSpec((1, 2, ML_HEADS, ML_HEAD_DIM, ML_HEAD_DIM), lambda b: (b, 0, 0, 0, 0)),
                 pl.BlockSpec((1, 2, ML_HEADS, ML_HEAD_DIM), lambda b: (b, 0, 0, 0)),
                 pl.BlockSpec((1, 2 * ML_HEADS, ML_HEAD_DIM), lambda b: (b, 0, 0)))
    scratch = [pltpu.VMEM((tiles, ML_WIDTH, TILE), BF16),
               pltpu.VMEM((tiles, ML_WIDTH, TILE), F32), pltpu.VMEM((tiles, ML_WIDTH, TILE), F32),
               pltpu.VMEM((2 * ML_HEADS, ST_ROWS, ML_HEAD_DIM), F32),
               pltpu.VMEM((2 * ML_HEADS, 1, TILE), F32)]
    return pl.pallas_call(
        functools.partial(_mlstm_kernel, seq_len=seq_len, has_state=has_state),
        out_shape=out_shape, grid=(n_seq,), in_specs=in_specs, out_specs=out_specs,
        scratch_shapes=scratch, compiler_params=_cparams(("arbitrary",)),
        name=f"mlstm_{seq_len}",
    )(*args)


def _dft_mats(seq_len):
    k = np.arange(seq_len, dtype=np.int64)[:, None]
    d = np.arange(seq_len, dtype=np.int64)[None, :]
    ang = np.pi * ((k * d) % (2 * seq_len)).astype(np.float64) / seq_len
    sinm = np.sin(ang)
    sinm[0, :] = np.where(d[0] % 2 == 0, 1.0, -1.0)
    f = np.concatenate([np.cos(ang), sinm], axis=0).astype(np.float32)
    return jnp.asarray(f).astype(BF16), jnp.asarray(np.ascontiguousarray(f.T)).astype(BF16)


def _filter_feats(seq_len):
    t = np.linspace(0.0, 1.0, seq_len, dtype=np.float64)[:, None]
    wpos = 2.0 * np.pi * np.arange(seq_len, dtype=np.float64)[:, None] / seq_len
    bands = np.linspace(1e-4, HY_BANDS - 1, HY_BANDS, dtype=np.float64)[None, :]
    z = np.concatenate([t, np.cos(bands * wpos), -np.sin(bands * wpos)], axis=-1)
    return jnp.asarray(np.pad(z, ((0, 0), (0, 128 - HY_EMB))).astype(np.float32))


def _filter_kernel(z_ref, w1_ref, b1_ref, w2_ref, b2_ref, w3_ref, b3_ref, dec_ref, f_ref,
                   a_ref, b_ref, d_ref, *, seq_len):
    n = 2 * seq_len
    oc = 2 * HY_WIDTH
    z = z_ref[...]
    h = jnp.sin(_dot3(z, w1_ref[...]) + b1_ref[...])
    h = jnp.sin(_dot3(h, w2_ref[...]) + b2_ref[...])
    t = z[:, 0:1]
    di = lax.broadcasted_iota(jnp.int32, (seq_len, 1), 0)
    sgn = jnp.where(di % 2 == 0, 1.0, -1.0)
    first = di == 0
    ssums, sdifs = [], []
    for o in range(HY_ORDER):
        cols = slice(o * oc, (o + 1) * oc)
        g = _dot3(h, w3_ref[:, cols]) + b3_ref[:, cols]
        g = g * (jnp.exp(-t * jnp.abs(dec_ref[:, cols])) + HY_MOD_SHIFT)
        ss = jnp.sum(g * g, axis=0, keepdims=True)
        inv = lax.rsqrt(ss[:, :HY_WIDTH] + ss[:, HY_WIDTH:] + EPS)
        hp = g[:, :HY_WIDTH] * inv
        hn = g[:, HY_WIDTH:] * inv
        ssums.append(hp + hn)
        sdifs.append(hp - hn)
    hcs = [_dot(f_ref[0:seq_len, :], s.astype(BF16)) for s in ssums]
    hss = [_dot(f_ref[seq_len:n, :], s.astype(BF16)) for s in sdifs]
    for o in range(HY_ORDER):
        nyq = jnp.sum(ssums[o] * sgn, axis=0, keepdims=True)
        a_ref[o] = hcs[o] * jnp.where(first, 1.0 / n, 2.0 / n)
        b_ref[o] = jnp.where(first, 0.0, hss[o] * (2.0 / n))
        d_ref[o] = jnp.where(first, nyq * (1.0 / n), hcs[o] * (2.0 / n))


def _hyena_filters(seq_len, f, w1p, b1, w2, b2, w3, b3, dec):
    z = _filter_feats(seq_len)
    out = jax.ShapeDtypeStruct((HY_ORDER, seq_len, HY_WIDTH), F32)
    return pl.pallas_call(
        functools.partial(_filter_kernel, seq_len=seq_len),
        out_shape=(out, out, out),
        compiler_params=pltpu.CompilerParams(vmem_limit_bytes=VMEM_LIMIT),
        name=f"hyena_filter_{seq_len}",
    )(z, w1p, b1, w2, b2, w3, b3, dec, f)


def _hyena_kernel(x1_ref, x2_ref, v_ref, cw1_ref, cw2_ref, cwv_ref, a_ref, b_ref, d_ref, bias_ref,
                  f_ref, ft_ref, z_ref, *, seq_len, width, seqs):
    rows = seqs * seq_len
    ti = lax.broadcasted_iota(jnp.int32, (rows, 1), 0)
    has_prev = (ti % width) != 0
    has_next = (ti % width) != (width - 1)

    def short_conv(x_ref, w_ref):
        x = x_ref[...].astype(F32)
        prev = jnp.where(has_prev, pltpu.roll(x, 1, axis=0), 0.0)
        nxt = jnp.where(has_next, pltpu.roll(x, rows - 1, axis=0), 0.0)
        return w_ref[0:1, :] * prev + w_ref[1:2, :] * x + w_ref[2:3, :] * nxt

    gates = (short_conv(x1_ref, cw1_ref), short_conv(x2_ref, cw2_ref))
    v = short_conv(v_ref, cwv_ref)
    sls = [slice(i * seq_len, (i + 1) * seq_len) for i in range(seqs)]
    zs = [v[sl] for sl in sls]
    for o in range(HY_ORDER):
        a, b, dd = a_ref[o], b_ref[o], d_ref[o]
        us = [_dot(f_ref[...], z.astype(BF16)) for z in zs]
        ys = []
        for u in us:
            ut = u[:seq_len]
            ub = u[seq_len:]
            ys.append(((ut * a - ub * b).astype(BF16), (ut * b + ub * dd).astype(BF16)))
        convs = [_dot(ft_ref[:, :seq_len], yt) + _dot(ft_ref[:, seq_len:], yb) for yt, yb in ys]
        zs = [gates[o][sl] * (y + bias_ref[o:o + 1, :] * z) for sl, y, z in zip(sls, convs, zs)]
    for sl, z in zip(sls, zs):
        z_ref[sl, :] = z.astype(BF16)


def _hyena(proj, conv_w, coefs, hy_bias, f, ft, seq_len, n_seq, row_off, width, seqs):
    cb = 256
    nblk = HY_WIDTH // cb
    base = ML_QKVO_COLS // cb
    rows = seqs * seq_len
    off = row_off // rows
    a, b, d = coefs

    def col_spec(part):
        return pl.BlockSpec((rows, cb), lambda j, s: (off + s, base + part * nblk + j))

    def w_spec(part):
        return pl.BlockSpec((3, cb), lambda j, s: (0, part * nblk + j))

    coef_spec = pl.BlockSpec((HY_ORDER, seq_len, cb), lambda j, s: (0, 0, j))
    return pl.pallas_call(
        functools.partial(_hyena_kernel, seq_len=seq_len, width=width, seqs=seqs),
        out_shape=jax.ShapeDtypeStruct((n_seq * seq_len, HY_WIDTH), BF16),
        grid=(nblk, n_seq // seqs),
        in_specs=[col_spec(0), col_spec(1), col_spec(2), w_spec(0), w_spec(1), w_spec(2),
                  coef_spec, coef_spec, coef_spec,
                  pl.BlockSpec((HY_ORDER, cb), lambda j, s: (0, j)),
                  pl.BlockSpec((2 * seq_len, seq_len), lambda j, s: (0, 0)),
                  pl.BlockSpec((seq_len, 2 * seq_len), lambda j, s: (0, 0))],
        out_specs=pl.BlockSpec((rows, cb), lambda j, s: (s, j)),
        compiler_params=_cparams(("arbitrary", "arbitrary")),
        name=f"hyena_conv_{seq_len}",
    )(proj, proj, proj, conv_w, conv_w, conv_w, a, b, d, hy_bias, f, ft)


def _first_max(x, n):
    mx = jnp.max(x, axis=0, keepdims=True)
    row = lax.broadcasted_iota(jnp.int32, x.shape, 0).astype(F32)
    idx = jnp.min(jnp.where(x == mx, row, float(n)), axis=0, keepdims=True)
    return mx, idx.astype(jnp.int32)


ROUTER_ROWS = 32
PAIRS_PER_GROUP = 6
N_BUCKETS = N_GROUPS * PAIRS_PER_GROUP
PAIR_SLOTS = ((0, 1), (0, 2), (0, 3), (1, 3), (1, 2), (3, 2))
LANES = 128
H2_EXT = D_MODEL + LANES
ROW_TILE = 256
ROW_CAP = T_ALL + N_BUCKETS * ROW_TILE
N_ROW_TILES = ROW_CAP // ROW_TILE


def _outproj_kernel(xp_ref, xs_ref, yp_ref, ys_ref, zp_ref, zs_ref, m_ref, gn_ref, wo_ref, wr_ref, br_ref,
                    x1_ref, h2_ref, bid_ref):
    is_p = pl.program_id(0) < N_BIG_P
    wrh, wrl = _split2(wr_ref[...])
    halves = [slice(r * TILE, (r + 1) * TILE) for r in range(BIG_TILE // TILE)]
    ys = [_dot(jnp.where(is_p, yp_ref[rows, :], ys_ref[rows, :]), wo_ref[0:ML_WIDTH, :])
          + _dot(jnp.where(is_p, zp_ref[rows, :], zs_ref[rows, :]), wo_ref[ML_WIDTH:, :]) for rows in halves]
    h2s = []
    for rows, y in zip(halves, ys):
        x = jnp.where(is_p, xp_ref[rows, :], xs_ref[rows, :])
        x1 = x + m_ref[0, 2:3, :] * _rms(y, gn_ref[1:2, :])
        x1_ref[rows, :] = x1
        h2 = _rms(x1, gn_ref[2:3, :]) * (1.0 + m_ref[0, 4:5, :]) + m_ref[0, 3:4, :]
        h2_ref[rows, 0:D_MODEL] = h2
        h2s.append(h2)
    wr2 = jnp.concatenate([wrh, wrl], axis=0)
    logits = []
    for h2 in h2s:
        h2h, h2l = _split2(h2)
        both = _dot_nt(wr2, h2h)
        logits.append(both[0:ROUTER_ROWS] + both[ROUTER_ROWS:] + _dot_nt(wrh, h2l) + br_ref[...])
    routed = [_route_tile(lg) for lg in logits]
    for r, (rows, (gate_rows, bucket)) in enumerate(zip(halves, routed)):
        h2_ref[rows, D_MODEL:H2_EXT] = jnp.zeros((TILE, LANES), F32)
        h2_ref[rows, D_MODEL:D_MODEL + 8] = _rows_to_cols(gate_rows)
        bid_ref[r] = bucket


def _route_tile(logits):
    lc = logits[0:N_GROUPS]
    mx, gi = _first_max(lc, N_GROUPS)
    p_grp = 1.0 / jnp.sum(jnp.exp(lc - mx), axis=0, keepdims=True)
    lsel = jnp.zeros((EXPERTS_PER_GROUP, TILE), F32)
    for g in range(N_GROUPS):
        lo = N_GROUPS + g * EXPERTS_PER_GROUP
        lsel = jnp.where(gi == g, logits[lo:lo + EXPERTS_PER_GROUP], lsel)
    l1, i1 = _first_max(lsel, EXPERTS_PER_GROUP)
    sub4 = lax.broadcasted_iota(jnp.int32, lsel.shape, 0)
    l2, i2 = _first_max(jnp.where(sub4 == i1, -jnp.inf, lsel), EXPERTS_PER_GROUP)
    e2 = jnp.exp(l2 - l1)
    w1 = p_grp / (1.0 + e2)
    w2 = p_grp * e2 / (1.0 + e2)
    lo_e = jnp.minimum(i1, i2)
    hi_e = jnp.maximum(i1, i2)
    pair = jnp.where(lo_e == 0, hi_e - 1, jnp.where(lo_e == 1, jnp.where(hi_e == 3, 3, 4), 5))
    slot_a = jnp.where(pair == 5, hi_e, lo_e)
    first_in_a = i1 == slot_a
    w_a = jnp.where(first_in_a, w1, w2)
    w_b = jnp.where(first_in_a, w2, w1)
    sub = lax.broadcasted_iota(jnp.int32, (8, TILE), 0)
    gate_rows = jnp.where(sub == 0, w_a, jnp.where(sub == 1, w_b, 0.0))
    return gate_rows, gi * PAIRS_PER_GROUP + pair


def _outproj(xp, xs, yp, ys, zp, zs, mods3, g_norm, w_out, w_r, b_r):
    tps = DEC_SEQ // BIG_TILE
    per = BIG_TILE // TILE
    pidx = lambda i: (jnp.minimum(i, N_BIG_P - 1), 0)
    sidx = lambda i: (jnp.maximum(i - N_BIG_P, 0), 0)
    return pl.pallas_call(
        _outproj_kernel,
        out_shape=(jax.ShapeDtypeStruct((T_ALL, D_MODEL), F32),
                   jax.ShapeDtypeStruct((T_ALL, H2_EXT), F32),
                   jax.ShapeDtypeStruct((N_TILES, 1, TILE), jnp.int32)),
        grid=(N_BIG,),
        in_specs=[pl.BlockSpec((BIG_TILE, D_MODEL), pidx), pl.BlockSpec((BIG_TILE, D_MODEL), sidx),
                  pl.BlockSpec((BIG_TILE, ML_WIDTH), pidx), pl.BlockSpec((BIG_TILE, ML_WIDTH), sidx),
                  pl.BlockSpec((BIG_TILE, HY_WIDTH), pidx), pl.BlockSpec((BIG_TILE, HY_WIDTH), sidx),
                  pl.BlockSpec((1, N_MOD, D_MODEL), lambda i: (_mod_row_of_tile(i, tps, N_BIG_P), 0, 0)),
                  pl.BlockSpec((4, D_MODEL), lambda i: (0, 0)),
                  pl.BlockSpec((D_MODEL, D_MODEL), lambda i: (0, 0)),
                  pl.BlockSpec((ROUTER_ROWS, D_MODEL), lambda i: (0, 0)),
                  pl.BlockSpec((ROUTER_ROWS, 1), lambda i: (0, 0))],
        out_specs=(pl.BlockSpec((BIG_TILE, D_MODEL), lambda i: (i, 0)),
                   pl.BlockSpec((BIG_TILE, H2_EXT), lambda i: (i, 0)),
                   pl.BlockSpec((per, 1, TILE), lambda i: (i, 0, 0))),
        compiler_params=_cparams(("arbitrary",)),
        name="out_proj_router",
    )(xp, xs, yp, ys, zp, zs, mods3, g_norm, w_out, w_r, b_r)


def _route_kernel(bid_ref, pos_ref, meta_ref):
    nb = 32
    tm = float(ROW_TILE)
    sub = lax.broadcasted_iota(jnp.int32, (nb, TILE), 0)
    ri = lax.broadcasted_iota(jnp.int32, (TILE, TILE), 0)
    ci = lax.broadcasted_iota(jnp.int32, (TILE, TILE), 1)
    before = jnp.where(ri < ci, 1.0, 0.0).astype(BF16)

    ohs = [jnp.where(sub == bid_ref[blk], 1.0, 0.0) for blk in range(N_TILES)]
    cnts = [jnp.sum(oh, axis=1, keepdims=True) for oh in ohs]
    cnt = functools.reduce(lambda a, b: a + b, cnts)
    ranks = _dot(jnp.concatenate(ohs, axis=0).astype(BF16), before)
    padded = jnp.floor((cnt + (tm - 1.0)) * (1.0 / tm)) * tm
    r32 = lax.broadcasted_iota(jnp.int32, (nb, nb), 0)
    c32 = lax.broadcasted_iota(jnp.int32, (nb, nb), 1)
    padded_row = jnp.sum(jnp.where(r32 == c32, padded, 0.0), axis=0, keepdims=True)
    offs = jnp.sum(jnp.where(c32 < r32, padded_row, 0.0), axis=1, keepdims=True)
    ends = offs + padded

    seen = jnp.zeros((nb, 1), F32)
    for blk in range(N_TILES):
        rank = ranks[blk * nb:(blk + 1) * nb]
        pos = jnp.sum(ohs[blk] * (rank + seen + offs), axis=0, keepdims=True)
        pos_ref[blk] = pos.astype(jnp.int32)
        seen = seen + cnts[blk]

    start = lax.broadcasted_iota(jnp.int32, (nb, 128), 1).astype(F32) * tm
    bsub = lax.broadcasted_iota(jnp.int32, (nb, 128), 0)
    done = jnp.where((bsub < N_BUCKETS) & (ends <= start), 1.0, 0.0)
    tb = jnp.sum(done, axis=0, keepdims=True)
    valid = jnp.where(tb < N_BUCKETS, 1.0, 0.0)
    tbc = jnp.minimum(tb, N_BUCKETS - 1.0)
    grp = jnp.floor((tbc + 0.5) * (1.0 / PAIRS_PER_GROUP))
    pair = tbc - PAIRS_PER_GROUP * grp
    loc_a = jnp.zeros_like(pair)
    loc_b = jnp.zeros_like(pair)
    for k, (sa, sb) in enumerate(PAIR_SLOTS):
        loc_a = jnp.where(pair == k, float(sa), loc_a)
        loc_b = jnp.where(pair == k, float(sb), loc_b)
    mine = bsub.astype(F32) == tbc
    used = jnp.sum(jnp.where(mine, offs + cnt, 0.0), axis=0, keepdims=True)
    n_rows = jnp.clip(used - start[0:1], 0.0, tm) * valid
    row8 = lax.broadcasted_iota(jnp.int32, (8, 128), 0)
    meta = jnp.where(row8 == 0, grp * EXPERTS_PER_GROUP + loc_a,
                     jnp.where(row8 == 1, grp * EXPERTS_PER_GROUP + loc_b,
                               jnp.where(row8 == 2, valid, jnp.where(row8 == 3, n_rows, 0.0))))
    meta_ref[...] = meta.astype(jnp.int32)


def _route(bid):
    return pl.pallas_call(
        _route_kernel,
        out_shape=(jax.ShapeDtypeStruct((N_TILES, 1, TILE), jnp.int32),
                   jax.ShapeDtypeStruct((8, 128), jnp.int32)),
        compiler_params=pltpu.CompilerParams(vmem_limit_bytes=VMEM_LIMIT),
        name="moe_route",
    )(bid)


def _moe_kernel(meta_ref, pos_ref, h2_hbm, wga_ref, wua_ref, wda_ref, wgb_ref, wub_ref, wdb_ref,
                y_ref, src_ref, xbuf, sem, wga_s, wua_s, wda_s, wgb_s, wub_s, wdb_s):
    j = pl.program_id(0)

    def row_copy(tile, r, slot):
        tok = src_ref[tile * ROW_TILE + r]
        return pltpu.make_async_copy(h2_hbm.at[pl.ds(tok, 1), :], xbuf.at[slot, pl.ds(r, 1), :], sem.at[slot])

    def issue_rows(tile, slot, lo, hi):
        for r in range(lo, hi):
            row_copy(tile, r, slot).start()

    group = 8

    def row_groups(tile):
        return (meta_ref[3, tile] + (group - 1)) // group

    def issue_counted(tile, slot):
        def body(g, c):
            for k in range(group):
                row_copy(tile, g * group + k, slot).start()
            return c
        lax.fori_loop(0, row_groups(tile), body, 0)

    def wait_counted(tile, slot):
        def body(g, c):
            for k in range(group):
                row_copy(tile, g * group + k, slot).wait()
            return c
        lax.fori_loop(0, row_groups(tile), body, 0)

    def wait_full(slot):
        pltpu.make_async_copy(h2_hbm.at[pl.ds(0, ROW_TILE), :], xbuf.at[slot], sem.at[slot]).wait()

    @pl.when(j == 0)
    def _():
        xbuf[...] = jnp.zeros_like(xbuf)

        def clear(t, c):
            n = meta_ref[3, t]
            for k in range(group - 1):
                src_ref[t * ROW_TILE + jnp.minimum(n + k, ROW_TILE - 1)] = 0
            return c
        lax.fori_loop(0, N_ROW_TILES, clear, 0)

        def invert(t, c):
            src_ref[pos_ref[t]] = t
            return c
        lax.fori_loop(0, T_ALL, invert, 0, unroll=8)

        @pl.when(meta_ref[2, 0] == 1)
        def _():
            issue_counted(0, 0)

    nxt = jnp.minimum(j + 1, N_ROW_TILES - 1)
    has_next = (j + 1 < N_ROW_TILES) & (meta_ref[2, nxt] == 1)
    next_full = has_next & (meta_ref[3, nxt] == ROW_TILE)
    valid = meta_ref[2, j] == 1
    full = meta_ref[3, j] == ROW_TILE
    prev = jnp.maximum(j - 1, 0)

    @pl.when(valid & has_next & jnp.logical_not(next_full))
    def _():
        issue_counted(nxt, nxt % 2)

    @pl.when(valid & full)
    def _():
        wait_full(j % 2)

    @pl.when(valid & jnp.logical_not(full))
    def _():
        wait_counted(j, j % 2)

    @pl.when(valid & ((j == 0) | (meta_ref[0, j] != meta_ref[0, prev])))
    def _():
        wga_s[...] = wga_ref[0].astype(BF16)
        wua_s[...] = wua_ref[0].astype(BF16)
        wda_s[...] = wda_ref[0].astype(BF16)

    @pl.when(valid & ((j == 0) | (meta_ref[1, j] != meta_ref[1, prev])))
    def _():
        wgb_s[...] = wgb_ref[0].astype(BF16)
        wub_s[...] = wub_ref[0].astype(BF16)
        wdb_s[...] = wdb_ref[0].astype(BF16)

    def compute(fetch_next):
        slot = j % 2
        nslot = nxt % 2
        step = ROW_TILE // 8
        batches = iter(range(0, ROW_TILE, step))

        def fetch():
            if fetch_next:
                lo = next(batches)
                issue_rows(nxt, nslot, lo, lo + step)

        x = xbuf[slot, :, 0:D_MODEL].astype(BF16)
        gates = xbuf[slot, :, D_MODEL:H2_EXT]
        hg_a = _dot(x, wga_s[...])
        fetch()
        hu_a = _dot(x, wua_s[...])
        fetch()
        hg_b = _dot(x, wgb_s[...])
        fetch()
        hu_b = _dot(x, wub_s[...])
        fetch()
        act_a = (hg_a * jax.nn.sigmoid(hg_a) * hu_a * gates[:, 0:1]).astype(BF16)
        fetch()
        act_b = (hg_b * jax.nn.sigmoid(hg_b) * hu_b * gates[:, 1:2]).astype(BF16)
        fetch()
        y = _dot(act_a, wda_s[...])
        fetch()
        y = y + _dot(act_b, wdb_s[...])
        fetch()
        y_ref[...] = y

    @pl.when(valid & next_full)
    def _():
        compute(True)

    @pl.when(valid & jnp.logical_not(next_full))
    def _():
        compute(False)

    @pl.when(jnp.logical_not(valid))
    def _():
        y_ref[...] = jnp.zeros_like(y_ref)


def _moe(meta, pos, h2ext, w_gate, w_up, w_down):
    up_spec = lambda slot: pl.BlockSpec((1, D_MODEL, EXPERT_FF), lambda j, meta, pos: (meta[slot, j], 0, 0))
    down_spec = lambda slot: pl.BlockSpec((1, EXPERT_FF, D_MODEL), lambda j, meta, pos: (meta[slot, j], 0, 0))
    grid_spec = pltpu.PrefetchScalarGridSpec(
        num_scalar_prefetch=2,
        grid=(N_ROW_TILES,),
        in_specs=[pl.BlockSpec(memory_space=pl.ANY),
                  up_spec(0), up_spec(0), down_spec(0), up_spec(1), up_spec(1), down_spec(1)],
        out_specs=pl.BlockSpec((ROW_TILE, D_MODEL), lambda j, meta, pos: (j, 0)),
        scratch_shapes=[pltpu.SMEM((ROW_CAP,), jnp.int32),
                        pltpu.VMEM((2, ROW_TILE, H2_EXT), F32),
                        pltpu.SemaphoreType.DMA((2,)),
                        pltpu.VMEM((D_MODEL, EXPERT_FF), BF16), pltpu.VMEM((D_MODEL, EXPERT_FF), BF16),
                        pltpu.VMEM((EXPERT_FF, D_MODEL), BF16),
                        pltpu.VMEM((D_MODEL, EXPERT_FF), BF16), pltpu.VMEM((D_MODEL, EXPERT_FF), BF16),
                        pltpu.VMEM((EXPERT_FF, D_MODEL), BF16)])
    return pl.pallas_call(
        _moe_kernel,
        out_shape=jax.ShapeDtypeStruct((ROW_CAP, D_MODEL), F32),
        grid_spec=grid_spec,
        compiler_params=_cparams(("arbitrary",)),
        name="moe_experts",
    )(meta, pos, h2ext, w_gate, w_up, w_down, w_gate, w_up, w_down)


def _final_kernel(pos_ref, y_hbm, x1_ref, m_ref, gn_ref, op_ref, os_ref, ybuf, sem):
    i = pl.program_id(0)

    def row_copy(tile, r, slot):
        p = pos_ref[tile * TILE + r]
        return pltpu.make_async_copy(y_hbm.at[pl.ds(p, 1), :], ybuf.at[slot, pl.ds(r, 1), :], sem.at[slot])

    def issue(tile, slot):
        def body(r2, c):
            row_copy(tile, 2 * r2, slot).start(priority=0)
            row_copy(tile, 2 * r2 + 1, slot).start(priority=1)
            return c
        lax.fori_loop(0, TILE // 2, body, 0, unroll=4)

    def wait(slot):
        pltpu.make_async_copy(y_hbm.at[pl.ds(0, TILE)], ybuf.at[slot], sem.at[slot]).wait()

    @pl.when(i == 0)
    def _():
        issue(0, 0)

    @pl.when(i + 1 < N_TILES)
    def _():
        issue(i + 1, (i + 1) % 2)

    slot = i % 2
    wait(slot)
    out = x1_ref[...] + m_ref[0, 5:6, :] * _rms(ybuf[slot], gn_ref[3:4, :])

    @pl.when(i < N_TILES_P)
    def _():
        op_ref[...] = out

    @pl.when(i >= N_TILES_P)
    def _():
        os_ref[...] = out


def _final(pos, y_sorted, x1, mods3, g_norm):
    tps = DEC_SEQ // TILE
    grid_spec = pltpu.PrefetchScalarGridSpec(
        num_scalar_prefetch=1,
        grid=(N_TILES,),
        in_specs=[pl.BlockSpec(memory_space=pl.ANY),
                  pl.BlockSpec((TILE, D_MODEL), lambda i, pos: (i, 0)),
                  pl.BlockSpec((1, N_MOD, D_MODEL), lambda i, pos: (_mod_row_of_tile(i, tps, N_TILES_P), 0, 0)),
                  pl.BlockSpec((4, D_MODEL), lambda i, pos: (0, 0))],
        out_specs=(pl.BlockSpec((TILE, D_MODEL), lambda i, pos: (jnp.minimum(i, N_TILES_P - 1), 0)),
                   pl.BlockSpec((TILE, D_MODEL), lambda i, pos: (jnp.maximum(i - N_TILES_P, 0), 0))),
        scratch_shapes=[pltpu.VMEM((2, TILE, D_MODEL), F32), pltpu.SemaphoreType.DMA((2,))])
    return pl.pallas_call(
        _final_kernel,
        out_shape=(jax.ShapeDtypeStruct((T_PROMPT, D_MODEL), F32),
                   jax.ShapeDtypeStruct((T_SAMPLE, D_MODEL), F32)),
        grid_spec=grid_spec,
        compiler_params=_cparams(("arbitrary",)),
        name="moe_combine_final",
    )(pos, y_sorted, x1, mods3, g_norm)


def kernel(x_prompt, x_sample, state_C, state_n, state_m, c, c_ctx, w_ada, b_ada, g_norm, w_in, ml_gate_bias, ml_head_gain, hy_conv_w, hy_f_w1, hy_f_b1, hy_f_w2, hy_f_b2, hy_f_w3, hy_f_b3, hy_decay, hy_bias, w_out, w_rc, b_rc, w_rf, b_rf, w_gate, w_up, w_down):
    xp = x_prompt.reshape(T_PROMPT, D_MODEL)
    xs = x_sample.reshape(T_SAMPLE, D_MODEL)
    gn = g_norm[0]

    cv = jnp.concatenate([c_ctx[None, :], c, jnp.zeros((MOD_ROWS - 1 - DEC_BATCH, D_MODEL), F32)], axis=0)
    mods3 = _ada(cv, w_ada[0], b_ada[0]).reshape(MOD_ROWS, N_MOD, D_MODEL)

    w_in0 = w_in[0]
    w_qkvo, w_hy = _prep_in_weights(w_in0.T)
    wg = w_in0[:, ML_QKVO_COLS:ML_QKVO_COLS + ML_GATE_COLS]
    gbt = ml_gate_bias[0].reshape(ML_GATE_COLS, 1)
    proj, gates_t = _inproj(xp, xs, mods3, gn, w_qkvo, w_hy, wg.T, gbt)

    gain = ml_head_gain[0].reshape(1, ML_WIDTH)
    y_ml_p, c_new, n_new, m_new = _mlstm(proj, gates_t, gain, None, SEQ, BATCH, 0)
    state = (state_C[:, 0], state_n[:, 0], state_m[:, 0].reshape(DEC_BATCH, 2 * ML_HEADS, 1))
    y_ml_s, _, _, _ = _mlstm(proj, gates_t, gain, state, DEC_SEQ, DEC_BATCH, T_PROMPT // DEC_SEQ)

    w1p = jnp.pad(hy_f_w1[0], ((0, 128 - HY_EMB), (0, 0)))
    b1 = hy_f_b1[0].reshape(1, -1)
    b2 = hy_f_b2[0].reshape(1, -1)
    b3 = hy_f_b3[0].reshape(1, -1)
    dec = hy_decay[0].reshape(1, -1)
    z_parts = []
    for seq_len, n_seq, row_off, width, seqs in ((SEQ, BATCH, 0, SEQ, 4), (DEC_SEQ, DEC_BATCH, T_PROMPT, GRID_W, 2)):
        f, ft = _dft_mats(seq_len)
        coefs = _hyena_filters(seq_len, f, w1p, b1, hy_f_w2[0], b2, hy_f_w3[0], b3, dec)
        z_parts.append(_hyena(proj, hy_conv_w[0], coefs, hy_bias[0], f, ft, seq_len, n_seq, row_off, width, seqs))
    z_p, z_s = z_parts

    pad_r = ROUTER_ROWS - N_GROUPS - N_EXPERTS
    w_r = jnp.pad(jnp.concatenate([w_rc[0], w_rf[0]], axis=1).T, ((0, pad_r), (0, 0)))
    b_r = jnp.pad(jnp.concatenate([b_rc[0], b_rf[0]], axis=0), (0, pad_r)).reshape(ROUTER_ROWS, 1)
    x1, h2ext, bid = _outproj(xp, xs, y_ml_p, y_ml_s, z_p, z_s, mods3, gn, w_out[0].astype(BF16), w_r, b_r)

    pos3, meta = _route(bid)
    pos = pos3.reshape(T_ALL)
    y_sorted = _moe(meta, pos, h2ext, w_gate[0], w_up[0], w_down[0])
    y_p, y_s = _final(pos, y_sorted, x1, mods3, gn)

    new_c = c_new.reshape(BATCH, 1, 2, ML_HEADS, ML_HEAD_DIM, ML_HEAD_DIM)
    new_n = n_new.reshape(BATCH, 1, 2, ML_HEADS, ML_HEAD_DIM)
    new_m = m_new[:, :, 0].reshape(BATCH, 1, 2, ML_HEADS)
    return (y_p.reshape(BATCH, SEQ, D_MODEL), y_s.reshape(DEC_BATCH, DEC_SEQ, D_MODEL), new_c, new_n, new_m)
```

```python
import functools
import math

import jax
import jax.numpy as jnp
import numpy as np
from jax import lax
from jax.experimental import pallas as pl
from jax.experimental.pallas import tpu as pltpu

F32 = jnp.float32
BF16 = jnp.bfloat16

D_MODEL = 1024
BATCH = 16
SEQ = 256
DEC_BATCH = 4
DEC_SEQ = 1024
GRID_W = 64
ML_WIDTH = 512
ML_HEADS = 4
ML_HEAD_DIM = 128
HY_WIDTH = 512
HY_ORDER = 2
HY_EMB = 33
HY_BANDS = 16
HY_FILTER_HIDDEN = 64
HY_MOD_SHIFT = 0.05
N_GROUPS = 4
EXPERTS_PER_GROUP = 4
N_EXPERTS = 16
EXPERT_FF = 512
N_MOD = 6
EPS = 1e-6
ML_QKVO_COLS = 4 * ML_WIDTH
ML_GATE_COLS = 4 * ML_HEADS
HY_COLS = 3 * HY_WIDTH
MAIN_COLS = ML_QKVO_COLS + HY_COLS

T_PROMPT = BATCH * SEQ
T_SAMPLE = DEC_BATCH * DEC_SEQ
T_ALL = T_PROMPT + T_SAMPLE
TILE = 256
N_TILES_P = T_PROMPT // TILE
N_TILES = T_ALL // TILE
MOD_ROWS = 8
K_SCALE = ML_HEAD_DIM ** -0.5
VMEM_LIMIT = 56 * 1024 * 1024


def _cparams(sem):
    return pltpu.CompilerParams(dimension_semantics=sem, vmem_limit_bytes=VMEM_LIMIT)


def _split2(x):
    hi = x.astype(BF16)
    lo = (x - hi.astype(F32)).astype(BF16)
    return hi, lo


def _dot(a, b):
    return jnp.dot(a, b, preferred_element_type=F32)


def _dot_nt(a, b):
    return lax.dot_general(a, b, (((1,), (1,)), ((), ())), preferred_element_type=F32)


def _dot3(a, b):
    ah, al = _split2(a)
    bh, bl = _split2(b)
    return _dot(ah, bh) + _dot(al, bh) + _dot(ah, bl)


def _dot_exact_rhs(x, t):
    x1 = x.astype(BF16)
    r1 = x - x1.astype(F32)
    x2 = r1.astype(BF16)
    x3 = (r1 - x2.astype(F32)).astype(BF16)
    return _dot(x1, t) + _dot(x2, t) + _dot(x3, t)


def _rms(x, g):
    return x * lax.rsqrt(jnp.mean(x * x, axis=-1, keepdims=True) + EPS) * g


def _mod_row_of_tile(i, tiles_per_sample_seq, n_prompt_tiles):
    return jnp.where(i < n_prompt_tiles, 0, 1 + (i - n_prompt_tiles) // tiles_per_sample_seq)


def _ada_kernel(cv_ref, w_ref, b_ref, o_ref):
    cv = cv_ref[...]
    s = cv * jax.nn.sigmoid(cv)
    sh, sl = _split2(s)
    wh, wl = _split2(w_ref[...])
    both = _dot(jnp.concatenate([sh.astype(F32), sl.astype(F32)], axis=0).astype(BF16), wh)
    o_ref[...] = both[0:MOD_ROWS] + both[MOD_ROWS:] + _dot(sh, wl) + b_ref[...]


def _ada(cv, w_ada, b_ada):
    n = N_MOD * D_MODEL
    return pl.pallas_call(
        _ada_kernel,
        out_shape=jax.ShapeDtypeStruct((MOD_ROWS, n), F32),
        grid=(N_MOD,),
        in_specs=[pl.BlockSpec((MOD_ROWS, D_MODEL), lambda j: (0, 0)),
                  pl.BlockSpec((D_MODEL, D_MODEL), lambda j: (0, j)),
                  pl.BlockSpec((1, D_MODEL), lambda j: (0, j))],
        out_specs=pl.BlockSpec((MOD_ROWS, D_MODEL), lambda j: (0, j)),
        compiler_params=_cparams(("arbitrary",)),
        name="ada_mod",
    )(cv, w_ada, b_ada.reshape(1, n))


PREP_COLS = 512


def _prep_q_kernel(wt_ref, o_ref):
    o_ref[...] = wt_ref[...].T.astype(BF16)


def _prep_hy_kernel(wt_hbm, o_ref, buf, sem):
    start = pl.multiple_of(ML_QKVO_COLS + ML_GATE_COLS + pl.program_id(0) * PREP_COLS, 8)
    copy = pltpu.make_async_copy(wt_hbm.at[pl.ds(start, PREP_COLS), :], buf, sem)
    copy.start()
    copy.wait()
    o_ref[...] = buf[...].T.astype(BF16)


def _prep_in_weights(w_in_t):
    out_blk = pl.BlockSpec((D_MODEL, PREP_COLS), lambda j: (0, j))
    w_qkvo = pl.pallas_call(
        _prep_q_kernel,
        out_shape=jax.ShapeDtypeStruct((D_MODEL, ML_QKVO_COLS), BF16),
        grid=(ML_QKVO_COLS // PREP_COLS,),
        in_specs=[pl.BlockSpec((PREP_COLS, D_MODEL), lambda j: (j, 0))], out_specs=out_blk,
        compiler_params=_cparams(("arbitrary",)), name="prep_w_qkvo",
    )(w_in_t)
    w_hy = pl.pallas_call(
        _prep_hy_kernel,
        out_shape=jax.ShapeDtypeStruct((D_MODEL, HY_COLS), BF16),
        grid=(HY_COLS // PREP_COLS,),
        in_specs=[pl.BlockSpec(memory_space=pl.ANY)], out_specs=out_blk,
        scratch_shapes=[pltpu.VMEM((PREP_COLS, D_MODEL), F32), pltpu.SemaphoreType.DMA(())],
        compiler_params=_cparams(("arbitrary",)), name="prep_w_hy",
    )(w_in_t)
    return w_qkvo, w_hy


def _log_sigmoid(x):
    return jnp.minimum(x, 0.0) - jnp.log1p(jnp.exp(-jnp.abs(x)))


def _rows_to_cols(rows):
    ri = lax.broadcasted_iota(jnp.int32, (TILE, TILE), 0)
    ci = lax.broadcasted_iota(jnp.int32, (TILE, TILE), 1)
    eye = jnp.where(ri == ci, 1.0, 0.0).astype(BF16)
    p1 = rows.astype(BF16)
    r1 = rows - p1.astype(F32)
    p2 = r1.astype(BF16)
    p3 = (r1 - p2.astype(F32)).astype(BF16)
    return _dot_nt(eye, p1) + _dot_nt(eye, p2) + _dot_nt(eye, p3)


BIG_TILE = 4 * TILE
N_BIG_P = T_PROMPT // BIG_TILE
N_BIG = T_ALL // BIG_TILE


def _inproj_kernel(xp_ref, xs_ref, m_ref, gn_ref, wq_ref, wh_ref, wgt_ref, gbt_ref, proj_ref, gatet_ref):
    is_p = pl.program_id(0) < N_BIG_P
    halves = [slice(r * TILE, (r + 1) * TILE) for r in range(BIG_TILE // TILE)]
    hs = [_rms(jnp.where(is_p, xp_ref[rows, :], xs_ref[rows, :]), gn_ref[0:1, :]) * (1.0 + m_ref[0, 1:2, :])
          + m_ref[0, 0:1, :] for rows in halves]
    hbs = [h.astype(BF16) for h in hs]
    cb = 1024
    for j in range(ML_QKVO_COLS // cb):
        for rows, hb in zip(halves, hbs):
            proj_ref[rows, j * cb:(j + 1) * cb] = _dot(hb, wq_ref[:, j * cb:(j + 1) * cb]).astype(BF16)
    cb = 512
    for j in range(HY_COLS // cb):
        lo = ML_QKVO_COLS + j * cb
        for rows, hb in zip(halves, hbs):
            proj_ref[rows, lo:lo + cb] = _dot(hb, wh_ref[:, j * cb:(j + 1) * cb]).astype(BF16)
    wth, wtl = _split2(wgt_ref[...])
    wt2 = jnp.concatenate([wth, wtl], axis=0)
    gts = []
    for h, hb in zip(hs, hbs):
        hl = (h - hb.astype(F32)).astype(BF16)
        both = _dot_nt(wt2, hb)
        gt = both[0:ML_GATE_COLS] + both[ML_GATE_COLS:] + _dot_nt(wth, hl) + gbt_ref[...]
        row = lax.broadcasted_iota(jnp.int32, gt.shape, 0)
        gts.append(jnp.where((row % 8) >= 4, _log_sigmoid(gt), gt))
    for r, gt in enumerate(gts):
        gatet_ref[r] = gt


def _inproj(xp, xs, mods3, g_norm, w_qkvo, w_hy, wgt, gbt):
    tps = DEC_SEQ // BIG_TILE
    per = BIG_TILE // TILE
    return pl.pallas_call(
        _inproj_kernel,
        out_shape=(jax.ShapeDtypeStruct((T_ALL, MAIN_COLS), BF16),
                   jax.ShapeDtypeStruct((N_TILES, ML_GATE_COLS, TILE), F32)),
        grid=(N_BIG,),
        in_specs=[pl.BlockSpec((BIG_TILE, D_MODEL), lambda i: (jnp.minimum(i, N_BIG_P - 1), 0)),
                  pl.BlockSpec((BIG_TILE, D_MODEL), lambda i: (jnp.maximum(i - N_BIG_P, 0), 0)),
                  pl.BlockSpec((1, N_MOD, D_MODEL), lambda i: (_mod_row_of_tile(i, tps, N_BIG_P), 0, 0)),
                  pl.BlockSpec((4, D_MODEL), lambda i: (0, 0)),
                  pl.BlockSpec((D_MODEL, ML_QKVO_COLS), lambda i: (0, 0)),
                  pl.BlockSpec((D_MODEL, HY_COLS), lambda i: (0, 0)),
                  pl.BlockSpec((ML_GATE_COLS, D_MODEL), lambda i: (0, 0)),
                  pl.BlockSpec((ML_GATE_COLS, 1), lambda i: (0, 0))],
        out_specs=(pl.BlockSpec((BIG_TILE, MAIN_COLS), lambda i: (i, 0)),
                   pl.BlockSpec((per, ML_GATE_COLS, TILE), lambda i: (i, 0, 0))),
        compiler_params=_cparams(("arbitrary",)),
        name="in_proj",
    )(xp, xs, mods3, g_norm, w_qkvo, w_hy, wgt, gbt)


ST_ROWS = ML_HEAD_DIM + 16


def _mlstm_kernel(*refs, seq_len, has_state):
    if has_state:
        (q_ref, k_ref, v_ref, o_ref, gt_ref, gain_ref, c0_ref, n0_ref, m0_ref,
         y_ref, c_ref, n_ref, m_ref, vt_ref, hf_ref, hb_ref, st_ref, ms_ref) = refs
    else:
        (q_ref, k_ref, v_ref, o_ref, gt_ref, gain_ref,
         y_ref, c_ref, n_ref, m_ref, vt_ref, hf_ref, hb_ref, st_ref, ms_ref) = refs
    ch = TILE
    nc = seq_len // ch
    hd = ML_HEAD_DIM
    key = lax.broadcasted_iota(jnp.int32, (ch, ch), 0)
    qry = lax.broadcasted_iota(jnp.int32, (ch, ch), 1)
    key_le = key <= qry
    key_ge = key >= qry
    t_le = jnp.where(key_le, 1.0, 0.0).astype(BF16)
    t_ge = jnp.where(key_ge, 1.0, 0.0).astype(BF16)
    sub16 = lax.broadcasted_iota(jnp.int32, (16, ch), 0)
    ln_scale = math.log(K_SCALE)

    for c in range(nc):
        for h in range(ML_HEADS):
            cols = slice(h * hd, (h + 1) * hd)
            vt_ref[c, cols, :] = v_ref[c * ch:(c + 1) * ch, cols].T

    for d in range(2):
        for h in range(ML_HEADS):
            r = d * ML_HEADS + h
            st_ref[r] = jnp.zeros((ST_ROWS, hd), F32)
            if has_state:
                st_ref[r, 0:hd, :] = c0_ref[0, d, h].T
                st_ref[r, hd:hd + 1, :] = n0_ref[0, d, h:h + 1, :]
                ms_ref[r] = jnp.broadcast_to(m0_ref[0, r:r + 1, :], (1, ch))
            else:
                ms_ref[r] = jnp.zeros((1, ch), F32)

    def step(t, carry):
        chains = [(d, h) for d in range(2) for h in range(ML_HEADS)]
        n = range(len(chains))
        cs = [t, nc - 1 - t]
        rowss = [pl.ds(pl.multiple_of(c * ch, ch), ch) for c in cs]
        grows = [gt_ref[c] for c in cs]
        brows = [_dot_exact_rhs(grows[d], t_le if d == 0 else t_ge) for d in range(2)]
        ccols = [_rows_to_cols(grows[d] - pltpu.roll(brows[d], ML_GATE_COLS - ML_HEADS, axis=0)) for d in range(2)]
        masks = [key_le, key_ge]
        haccs = [hf_ref, hb_ref]
        regs = [d * ML_HEADS + h for d, h in chains]
        colss = [slice(h * hd, (h + 1) * hd) for d, h in chains]
        qs = [q_ref[rowss[d], colss[i]] for i, (d, h) in enumerate(chains)]
        ks = [k_ref[rowss[d], colss[i]] for i, (d, h) in enumerate(chains)]
        vts = [vt_ref[cs[d], colss[i], :] for i, (d, h) in enumerate(chains)]
        sts = [st_ref[r] for r in regs]
        m_prevs = [ms_ref[r] for r in regs]
        b_rows = [brows[d][(1 + 2 * d) * ML_HEADS + h:(1 + 2 * d) * ML_HEADS + h + 1, :] for d, h in chains]
        ig_rows = [grows[d][2 * d * ML_HEADS + h:2 * d * ML_HEADS + h + 1, :] for d, h in chains]
        qks = [_dot_nt(k, q) for k, q in zip(ks, qs)]
        iqs = [_dot_nt(st.astype(BF16), q) for st, q in zip(sts, qs)]
        ss, sc_inters, m_poss = [], [], []
        for i, (d, h) in enumerate(chains):
            icol = 2 * d * ML_HEADS + h
            c_col = ccols[d][:, icol:icol + 1]
            logd = jnp.where(masks[d], b_rows[i] + c_col, -jnp.inf)
            inter = b_rows[i] + m_prevs[i]
            m_pos = jnp.maximum(inter, jnp.max(logd, axis=0, keepdims=True))
            ss.append(qks[i] * jnp.exp(logd - (m_pos - ln_scale)))
            sc_inters.append(jnp.exp(inter - m_pos))
            m_poss.append(m_pos)
        pvs = [_dot(vt, s.astype(BF16)) for vt, s in zip(vts, ss)]
        for i, (d, h) in enumerate(chains):
            num = sc_inters[i] * iqs[i][0:hd] + pvs[i]
            den = sc_inters[i] * iqs[i][hd:hd + 1] + jnp.sum(ss[i], axis=0, keepdims=True)
            haccs[d][cs[d], colss[i], :] = num * (1.0 / jnp.maximum(jnp.abs(den), jnp.exp(-m_poss[i])))
        lhss, decays = [], []
        for i, (d, h) in enumerate(chains):
            b_row = b_rows[i]
            b_last = b_row[:, ch - 1:ch] if d == 0 else b_row[:, 0:1]
            logw = b_last - b_row + ig_rows[i]
            m_new = jnp.maximum(b_last + m_prevs[i], jnp.max(logw, axis=1, keepdims=True))
            w = jnp.exp(logw - (m_new - ln_scale))
            decays.append(jnp.exp(b_last + m_prevs[i] - m_new))
            lhss.append(jnp.concatenate([(vts[i].astype(F32) * w).astype(BF16),
                                         jnp.where(sub16 == 0, w, 0.0).astype(BF16)], axis=0))
            ms_ref[regs[i]] = m_new
        upds = [_dot(lhs, k) for lhs, k in zip(lhss, ks)]
        for i in n:
            st_ref[regs[i]] = decays[i][:, 0:hd] * sts[i] + upds[i]
        return carry

    lax.fori_loop(0, nc, step, 0, unroll=True)

    for d in range(2):
        for h in range(ML_HEADS):
            r = d * ML_HEADS + h
            c_ref[0, d, h] = st_ref[r, 0:hd, :].T
            n_ref[0, d, h:h + 1, :] = st_ref[r, hd:hd + 1, :]
            m_ref[0, r:r + 1, :] = ms_ref[r][:, 0:hd]
    for c in range(nc):
        for h in range(ML_HEADS):
            cols = slice(h * hd, (h + 1) * hd)
            ht = hf_ref[c, cols, :] + hb_ref[c, cols, :]
            ht = ht * lax.rsqrt(jnp.mean(ht * ht, axis=0, keepdims=True) + EPS)
            rows = slice(c * ch, (c + 1) * ch)
            y = ht.T * gain_ref[:, cols] * jax.nn.sigmoid(o_ref[rows, cols].astype(F32))
            y_ref[rows, cols] = y.astype(BF16)


def _mlstm(proj, gates_t, gain, state, seq_len, n_seq, row_block_off):
    has_state = state is not None
    tiles = seq_len // TILE
    off = row_block_off
    qkvo_specs = [pl.BlockSpec((seq_len, ML_WIDTH), functools.partial(lambda b, j: (off + b, j), j=j))
                  for j in range(4)]
    in_specs = qkvo_specs + [
        pl.BlockSpec((tiles, ML_GATE_COLS, TILE), lambda b: (off + b, 0, 0)),
        pl.BlockSpec((1, ML_WIDTH), lambda b: (0, 0)),
    ]
    args = [proj, proj, proj, proj, gates_t, gain]
    if has_state:
        c0, n0, m0 = state
        in_specs += [
            pl.BlockSpec((1, 2, ML_HEADS, ML_HEAD_DIM, ML_HEAD_DIM), lambda b: (b, 0, 0, 0, 0)),
            pl.BlockSpec((1, 2, ML_HEADS, ML_HEAD_DIM), lambda b: (b, 0, 0, 0)),
            pl.BlockSpec((1, 2 * ML_HEADS, 1), lambda b: (b, 0, 0)),
        ]
        args += [c0, n0, m0]
    out_shape = (jax.ShapeDtypeStruct((n_seq * seq_len, ML_WIDTH), BF16),
                 jax.ShapeDtypeStruct((n_seq, 2, ML_HEADS, ML_HEAD_DIM, ML_HEAD_DIM), F32),
                 jax.ShapeDtypeStruct((n_seq, 2, ML_HEADS, ML_HEAD_DIM), F32),
                 jax.ShapeDtypeStruct((n_seq, 2 * ML_HEADS, ML_HEAD_DIM), F32))
    out_specs = (pl.BlockSpec((seq_len, ML_WIDTH), lambda b: (b, 0)),
                 pl.BlockSpec((1, 2, ML_HEADS, ML_HEAD_DIM, ML_HEAD_DIM), lambda b: (b, 0, 0, 0, 0)),
                 pl.BlockSpec((1, 2, ML_HEADS, ML_HEAD_DIM), lambda b: (b, 0, 0, 0)),
                 pl.BlockSpec((1, 2 * ML_HEADS, ML_HEAD_DIM), lambda b: (b, 0, 0)))
    scratch = [pltpu.VMEM((tiles, ML_WIDTH, TILE), BF16),
               pltpu.VMEM((tiles, ML_WIDTH, TILE), F32), pltpu.VMEM((tiles, ML_WIDTH, TILE), F32),
               pltpu.VMEM((2 * ML_HEADS, ST_ROWS, ML_HEAD_DIM), F32),
               pltpu.VMEM((2 * ML_HEADS, 1, TILE), F32)]
    return pl.pallas_call(
        functools.partial(_mlstm_kernel, seq_len=seq_len, has_state=has_state),
        out_shape=out_shape, grid=(n_seq,), in_specs=in_specs, out_specs=out_specs,
        scratch_shapes=scratch, compiler_params=_cparams(("arbitrary",)),
        name=f"mlstm_{seq_len}",
    )(*args)


def _dft_mats(seq_len):
    k = np.arange(seq_len, dtype=np.int64)[:, None]
    d = np.arange(seq_len, dtype=np.int64)[None, :]
    ang = np.pi * ((k * d) % (2 * seq_len)).astype(np.float64) / seq_len
    sinm = np.sin(ang)
    sinm[0, :] = np.where(d[0] % 2 == 0, 1.0, -1.0)
    f = np.concatenate([np.cos(ang), sinm], axis=0).astype(np.float32)
    return jnp.asarray(f).astype(BF16), jnp.asarray(np.ascontiguousarray(f.T)).astype(BF16)


def _filter_feats(seq_len):
    t = np.linspace(0.0, 1.0, seq_len, dtype=np.float64)[:, None]
    wpos = 2.0 * np.pi * np.arange(seq_len, dtype=np.float64)[:, None] / seq_len
    bands = np.linspace(1e-4, HY_BANDS - 1, HY_BANDS, dtype=np.float64)[None, :]
    z = np.concatenate([t, np.cos(bands * wpos), -np.sin(bands * wpos)], axis=-1)
    return jnp.asarray(np.pad(z, ((0, 0), (0, 128 - HY_EMB))).astype(np.float32))


def _filter_kernel(z_ref, w1_ref, b1_ref, w2_ref, b2_ref, w3_ref, b3_ref, dec_ref, f_ref,
                   a_ref, b_ref, d_ref, *, seq_len):
    n = 2 * seq_len
    oc = 2 * HY_WIDTH
    z = z_ref[...]
    h = jnp.sin(_dot3(z, w1_ref[...]) + b1_ref[...])
    h = jnp.sin(_dot3(h, w2_ref[...]) + b2_ref[...])
    t = z[:, 0:1]
    di = lax.broadcasted_iota(jnp.int32, (seq_len, 1), 0)
    sgn = jnp.where(di % 2 == 0, 1.0, -1.0)
    first = di == 0
    ssums, sdifs = [], []
    for o in range(HY_ORDER):
        cols = slice(o * oc, (o + 1) * oc)
        g = _dot3(h, w3_ref[:, cols]) + b3_ref[:, cols]
        g = g * (jnp.exp(-t * jnp.abs(dec_ref[:, cols])) + HY_MOD_SHIFT)
        ss = jnp.sum(g * g, axis=0, keepdims=True)
        inv = lax.rsqrt(ss[:, :HY_WIDTH] + ss[:, HY_WIDTH:] + EPS)
        hp = g[:, :HY_WIDTH] * inv
        hn = g[:, HY_WIDTH:] * inv
        ssums.append(hp + hn)
        sdifs.append(hp - hn)
    hcs = [_dot(f_ref[0:seq_len, :], s.astype(BF16)) for s in ssums]
    hss = [_dot(f_ref[seq_len:n, :], s.astype(BF16)) for s in sdifs]
    for o in range(HY_ORDER):
        nyq = jnp.sum(ssums[o] * sgn, axis=0, keepdims=True)
        a_ref[o] = hcs[o] * jnp.where(first, 1.0 / n, 2.0 / n)
        b_ref[o] = jnp.where(first, 0.0, hss[o] * (2.0 / n))
        d_ref[o] = jnp.where(first, nyq * (1.0 / n), hcs[o] * (2.0 / n))


def _hyena_filters(seq_len, f, w1p, b1, w2, b2, w3, b3, dec):
    z = _filter_feats(seq_len)
    out = jax.ShapeDtypeStruct((HY_ORDER, seq_len, HY_WIDTH), F32)
    return pl.pallas_call(
        functools.partial(_filter_kernel, seq_len=seq_len),
        out_shape=(out, out, out),
        compiler_params=pltpu.CompilerParams(vmem_limit_bytes=VMEM_LIMIT),
        name=f"hyena_filter_{seq_len}",
    )(z, w1p, b1, w2, b2, w3, b3, dec, f)


def _hyena_kernel(x1_ref, x2_ref, v_ref, cw1_ref, cw2_ref, cwv_ref, a_ref, b_ref, d_ref, bias_ref,
                  f_ref, ft_ref, z_ref, *, seq_len, width, seqs):
    rows = seqs * seq_len
    ti = lax.broadcasted_iota(jnp.int32, (rows, 1), 0)
    has_prev = (ti % width) != 0
    has_next = (ti % width) != (width - 1)

    def short_conv(x_ref, w_ref):
        x = x_ref[...].astype(F32)
        prev = jnp.where(has_prev, pltpu.roll(x, 1, axis=0), 0.0)
        nxt = jnp.where(has_next, pltpu.roll(x, rows - 1, axis=0), 0.0)
        return w_ref[0:1, :] * prev + w_ref[1:2, :] * x + w_ref[2:3, :] * nxt

    gates = (short_conv(x1_ref, cw1_ref), short_conv(x2_ref, cw2_ref))
    v = short_conv(v_ref, cwv_ref)
    sls = [slice(i * seq_len, (i + 1) * seq_len) for i in range(seqs)]
    zs = [v[sl] for sl in sls]
    for o in range(HY_ORDER):
        a, b, dd = a_ref[o], b_ref[o], d_ref[o]
        us = [_dot(f_ref[...], z.astype(BF16)) for z in zs]
        ys = []
        for u in us:
            ut = u[:seq_len]
            ub = u[seq_len:]
            ys.append(((ut * a - ub * b).astype(BF16), (ut * b + ub * dd).astype(BF16)))
        convs = [_dot(ft_ref[:, :seq_len], yt) + _dot(ft_ref[:, seq_len:], yb) for yt, yb in ys]
        zs = [gates[o][sl] * (y + bias_ref[o:o + 1, :] * z) for sl, y, z in zip(sls, convs, zs)]
    for sl, z in zip(sls, zs):
        z_ref[sl, :] = z.astype(BF16)


def _hyena(proj, conv_w, coefs, hy_bias, f, ft, seq_len, n_seq, row_off, width, seqs):
    cb = 256
    nblk = HY_WIDTH // cb
    base = ML_QKVO_COLS // cb
    rows = seqs * seq_len
    off = row_off // rows
    a, b, d = coefs

    def col_spec(part):
        return pl.BlockSpec((rows, cb), lambda j, s: (off + s, base + part * nblk + j))

    def w_spec(part):
        return pl.BlockSpec((3, cb), lambda j, s: (0, part * nblk + j))

    coef_spec = pl.BlockSpec((HY_ORDER, seq_len, cb), lambda j, s: (0, 0, j))
    return pl.pallas_call(
        functools.partial(_hyena_kernel, seq_len=seq_len, width=width, seqs=seqs),
        out_shape=jax.ShapeDtypeStruct((n_seq * seq_len, HY_WIDTH), BF16),
        grid=(nblk, n_seq // seqs),
        in_specs=[col_spec(0), col_spec(1), col_spec(2), w_spec(0), w_spec(1), w_spec(2),
                  coef_spec, coef_spec, coef_spec,
                  pl.BlockSpec((HY_ORDER, cb), lambda j, s: (0, j)),
                  pl.BlockSpec((2 * seq_len, seq_len), lambda j, s: (0, 0)),
                  pl.BlockSpec((seq_len, 2 * seq_len), lambda j, s: (0, 0))],
        out_specs=pl.BlockSpec((rows, cb), lambda j, s: (s, j)),
        compiler_params=_cparams(("arbitrary", "arbitrary")),
        name=f"hyena_conv_{seq_len}",
    )(proj, proj, proj, conv_w, conv_w, conv_w, a, b, d, hy_bias, f, ft)


def _first_max(x, n):
    mx = jnp.max(x, axis=0, keepdims=True)
    row = lax.broadcasted_iota(jnp.int32, x.shape, 0).astype(F32)
    idx = jnp.min(jnp.where(x == mx, row, float(n)), axis=0, keepdims=True)
    return mx, idx.astype(jnp.int32)


ROUTER_ROWS = 32
PAIRS_PER_GROUP = 6
N_BUCKETS = N_GROUPS * PAIRS_PER_GROUP
PAIR_SLOTS = ((0, 1), (0, 2), (0, 3), (1, 3), (1, 2), (3, 2))
LANES = 128
H2_EXT = D_MODEL + LANES
ROW_TILE = 256
ROW_CAP = T_ALL + N_BUCKETS * ROW_TILE
N_ROW_TILES = ROW_CAP // ROW_TILE


def _outproj_kernel(xp_ref, xs_ref, yp_ref, ys_ref, zp_ref, zs_ref, m_ref, gn_ref, wo_ref, wr_ref, br_ref,
                    x1_ref, h2_ref, bid_ref):
    is_p = pl.program_id(0) < N_BIG_P
    wrh, wrl = _split2(wr_ref[...])
    halves = [slice(r * TILE, (r + 1) * TILE) for r in range(BIG_TILE // TILE)]
    ys = [_dot(jnp.where(is_p, yp_ref[rows, :], ys_ref[rows, :]), wo_ref[0:ML_WIDTH, :])
          + _dot(jnp.where(is_p, zp_ref[rows, :], zs_ref[rows, :]), wo_ref[ML_WIDTH:, :]) for rows in halves]
    h2s = []
    for rows, y in zip(halves, ys):
        x = jnp.where(is_p, xp_ref[rows, :], xs_ref[rows, :])
        x1 = x + m_ref[0, 2:3, :] * _rms(y, gn_ref[1:2, :])
        x1_ref[rows, :] = x1
        h2 = _rms(x1, gn_ref[2:3, :]) * (1.0 + m_ref[0, 4:5, :]) + m_ref[0, 3:4, :]
        h2_ref[rows, 0:D_MODEL] = h2
        h2s.append(h2)
    wr2 = jnp.concatenate([wrh, wrl], axis=0)
    logits = []
    for h2 in h2s:
        h2h, h2l = _split2(h2)
        both = _dot_nt(wr2, h2h)
        logits.append(both[0:ROUTER_ROWS] + both[ROUTER_ROWS:] + _dot_nt(wrh, h2l) + br_ref[...])
    routed = [_route_tile(lg) for lg in logits]
    for r, (rows, (gate_rows, bucket)) in enumerate(zip(halves, routed)):
        h2_ref[rows, D_MODEL:H2_EXT] = jnp.zeros((TILE, LANES), F32)
        h2_ref[rows, D_MODEL:D_MODEL + 8] = _rows_to_cols(gate_rows)
        bid_ref[r] = bucket


def _route_tile(logits):
    lc = logits[0:N_GROUPS]
    mx, gi = _first_max(lc, N_GROUPS)
    p_grp = 1.0 / jnp.sum(jnp.exp(lc - mx), axis=0, keepdims=True)
    lsel = jnp.zeros((EXPERTS_PER_GROUP, TILE), F32)
    for g in range(N_GROUPS):
        lo = N_GROUPS + g * EXPERTS_PER_GROUP
        lsel = jnp.where(gi == g, logits[lo:lo + EXPERTS_PER_GROUP], lsel)
    l1, i1 = _first_max(lsel, EXPERTS_PER_GROUP)
    sub4 = lax.broadcasted_iota(jnp.int32, lsel.shape, 0)
    l2, i2 = _first_max(jnp.where(sub4 == i1, -jnp.inf, lsel), EXPERTS_PER_GROUP)
    e2 = jnp.exp(l2 - l1)
    w1 = p_grp / (1.0 + e2)
    w2 = p_grp * e2 / (1.0 + e2)
    lo_e = jnp.minimum(i1, i2)
    hi_e = jnp.maximum(i1, i2)
    pair = jnp.where(lo_e == 0, hi_e - 1, jnp.where(lo_e == 1, jnp.where(hi_e == 3, 3, 4), 5))
    slot_a = jnp.where(pair == 5, hi_e, lo_e)
    first_in_a = i1 == slot_a
    w_a = jnp.where(first_in_a, w1, w2)
    w_b = jnp.where(first_in_a, w2, w1)
    sub = lax.broadcasted_iota(jnp.int32, (8, TILE), 0)
    gate_rows = jnp.where(sub == 0, w_a, jnp.where(sub == 1, w_b, 0.0))
    return gate_rows, gi * PAIRS_PER_GROUP + pair


def _outproj(xp, xs, yp, ys, zp, zs, mods3, g_norm, w_out, w_r, b_r):
    tps = DEC_SEQ // BIG_TILE
    per = BIG_TILE // TILE
    pidx = lambda i: (jnp.minimum(i, N_BIG_P - 1), 0)
    sidx = lambda i: (jnp.maximum(i - N_BIG_P, 0), 0)
    return pl.pallas_call(
        _outproj_kernel,
        out_shape=(jax.ShapeDtypeStruct((T_ALL, D_MODEL), F32),
                   jax.ShapeDtypeStruct((T_ALL, H2_EXT), F32),
                   jax.ShapeDtypeStruct((N_TILES, 1, TILE), jnp.int32)),
        grid=(N_BIG,),
        in_specs=[pl.BlockSpec((BIG_TILE, D_MODEL), pidx), pl.BlockSpec((BIG_TILE, D_MODEL), sidx),
                  pl.BlockSpec((BIG_TILE, ML_WIDTH), pidx), pl.BlockSpec((BIG_TILE, ML_WIDTH), sidx),
                  pl.BlockSpec((BIG_TILE, HY_WIDTH), pidx), pl.BlockSpec((BIG_TILE, HY_WIDTH), sidx),
                  pl.BlockSpec((1, N_MOD, D_MODEL), lambda i: (_mod_row_of_tile(i, tps, N_BIG_P), 0, 0)),
                  pl.BlockSpec((4, D_MODEL), lambda i: (0, 0)),
                  pl.BlockSpec((D_MODEL, D_MODEL), lambda i: (0, 0)),
                  pl.BlockSpec((ROUTER_ROWS, D_MODEL), lambda i: (0, 0)),
                  pl.BlockSpec((ROUTER_ROWS, 1), lambda i: (0, 0))],
        out_specs=(pl.BlockSpec((BIG_TILE, D_MODEL), lambda i: (i, 0)),
                   pl.BlockSpec((BIG_TILE, H2_EXT), lambda i: (i, 0)),
                   pl.BlockSpec((per, 1, TILE), lambda i: (i, 0, 0))),
        compiler_params=_cparams(("arbitrary",)),
        name="out_proj_router",
    )(xp, xs, yp, ys, zp, zs, mods3, g_norm, w_out, w_r, b_r)


def _route_kernel(bid_ref, pos_ref, meta_ref):
    nb = 32
    tm = float(ROW_TILE)
    sub = lax.broadcasted_iota(jnp.int32, (nb, TILE), 0)
    ri = lax.broadcasted_iota(jnp.int32, (TILE, TILE), 0)
    ci = lax.broadcasted_iota(jnp.int32, (TILE, TILE), 1)
    before = jnp.where(ri < ci, 1.0, 0.0).astype(BF16)

    ohs = [jnp.where(sub == bid_ref[blk], 1.0, 0.0) for blk in range(N_TILES)]
    cnts = [jnp.sum(oh, axis=1, keepdims=True) for oh in ohs]
    cnt = functools.reduce(lambda a, b: a + b, cnts)
    ranks = _dot(jnp.concatenate(ohs, axis=0).astype(BF16), before)
    padded = jnp.floor((cnt + (tm - 1.0)) * (1.0 / tm)) * tm
    r32 = lax.broadcasted_iota(jnp.int32, (nb, nb), 0)
    c32 = lax.broadcasted_iota(jnp.int32, (nb, nb), 1)
    padded_row = jnp.sum(jnp.where(r32 == c32, padded, 0.0), axis=0, keepdims=True)
    offs = jnp.sum(jnp.where(c32 < r32, padded_row, 0.0), axis=1, keepdims=True)
    ends = offs + padded

    seen = jnp.zeros((nb, 1), F32)
    for blk in range(N_TILES):
        rank = ranks[blk * nb:(blk + 1) * nb]
        pos = jnp.sum(ohs[blk] * (rank + seen + offs), axis=0, keepdims=True)
        pos_ref[blk] = pos.astype(jnp.int32)
        seen = seen + cnts[blk]

    start = lax.broadcasted_iota(jnp.int32, (nb, 128), 1).astype(F32) * tm
    bsub = lax.broadcasted_iota(jnp.int32, (nb, 128), 0)
    done = jnp.where((bsub < N_BUCKETS) & (ends <= start), 1.0, 0.0)
    tb = jnp.sum(done, axis=0, keepdims=True)
    valid = jnp.where(tb < N_BUCKETS, 1.0, 0.0)
    tbc = jnp.minimum(tb, N_BUCKETS - 1.0)
    grp = jnp.floor((tbc + 0.5) * (1.0 / PAIRS_PER_GROUP))
    pair = tbc - PAIRS_PER_GROUP * grp
    loc_a = jnp.zeros_like(pair)
    loc_b = jnp.zeros_like(pair)
    for k, (sa, sb) in enumerate(PAIR_SLOTS):
        loc_a = jnp.where(pair == k, float(sa), loc_a)
        loc_b = jnp.where(pair == k, float(sb), loc_b)
    mine = bsub.astype(F32) == tbc
    used = jnp.sum(jnp.where(mine, offs + cnt, 0.0), axis=0, keepdims=True)
    n_rows = jnp.clip(used - start[0:1], 0.0, tm) * valid
    row8 = lax.broadcasted_iota(jnp.int32, (8, 128), 0)
    meta = jnp.where(row8 == 0, grp * EXPERTS_PER_GROUP + loc_a,
                     jnp.where(row8 == 1, grp * EXPERTS_PER_GROUP + loc_b,
                               jnp.where(row8 == 2, valid, jnp.where(row8 == 3, n_rows, 0.0))))
    meta_ref[...] = meta.astype(jnp.int32)


def _route(bid):
    return pl.pallas_call(
        _route_kernel,
        out_shape=(jax.ShapeDtypeStruct((N_TILES, 1, TILE), jnp.int32),
                   jax.ShapeDtypeStruct((8, 128), jnp.int32)),
        compiler_params=pltpu.CompilerParams(vmem_limit_bytes=VMEM_LIMIT),
        name="moe_route",
    )(bid)


def _moe_kernel(meta_ref, pos_ref, h2_hbm, wga_ref, wua_ref, wda_ref, wgb_ref, wub_ref, wdb_ref,
                y_ref, src_ref, xbuf, sem, wga_s, wua_s, wda_s, wgb_s, wub_s, wdb_s):
    j = pl.program_id(0)

    def row_copy(tile, r, slot):
        tok = src_ref[tile * ROW_TILE + r]
        return pltpu.make_async_copy(h2_hbm.at[pl.ds(tok, 1), :], xbuf.at[slot, pl.ds(r, 1), :], sem.at[slot])

    def issue_rows(tile, slot, lo, hi):
        for r in range(lo, hi):
            row_copy(tile, r, slot).start()

    group = 8

    def row_groups(tile):
        return (meta_ref[3, tile] + (group - 1)) // group

    def issue_counted(tile, slot):
        def body(g, c):
            for k in range(group):
                row_copy(tile, g * group + k, slot).start()
            return c
        lax.fori_loop(0, row_groups(tile), body, 0)

    def wait_counted(tile, slot):
        def body(g, c):
            for k in range(group):
                row_copy(tile, g * group + k, slot).wait()
            return c
        lax.fori_loop(0, row_groups(tile), body, 0)

    def wait_full(slot):
        pltpu.make_async_copy(h2_hbm.at[pl.ds(0, ROW_TILE), :], xbuf.at[slot], sem.at[slot]).wait()

    @pl.when(j == 0)
    def _():
        xbuf[...] = jnp.zeros_like(xbuf)

        def clear(t, c):
            n = meta_ref[3, t]
            for k in range(group - 1):
                src_ref[t * ROW_TILE + jnp.minimum(n + k, ROW_TILE - 1)] = 0
            return c
        lax.fori_loop(0, N_ROW_TILES, clear, 0)

        def invert(t, c):
            src_ref[pos_ref[t]] = t
            return c
        lax.fori_loop(0, T_ALL, invert, 0, unroll=8)

        @pl.when(meta_ref[2, 0] == 1)
        def _():
            issue_counted(0, 0)

    nxt = jnp.minimum(j + 1, N_ROW_TILES - 1)
    has_next = (j + 1 < N_ROW_TILES) & (meta_ref[2, nxt] == 1)
    next_full = has_next & (meta_ref[3, nxt] == ROW_TILE)
    valid = meta_ref[2, j] == 1
    full = meta_ref[3, j] == ROW_TILE
    prev = jnp.maximum(j - 1, 0)

    @pl.when(valid & has_next & jnp.logical_not(next_full))
    def _():
        issue_counted(nxt, nxt % 2)

    @pl.when(valid & full)
    def _():
        wait_full(j % 2)

    @pl.when(valid & jnp.logical_not(full))
    def _():
        wait_counted(j, j % 2)

    @pl.when(valid & ((j == 0) | (meta_ref[0, j] != meta_ref[0, prev])))
    def _():
        wga_s[...] = wga_ref[0].astype(BF16)
        wua_s[...] = wua_ref[0].astype(BF16)
        wda_s[...] = wda_ref[0].astype(BF16)

    @pl.when(valid & ((j == 0) | (meta_ref[1, j] != meta_ref[1, prev])))
    def _():
        wgb_s[...] = wgb_ref[0].astype(BF16)
        wub_s[...] = wub_ref[0].astype(BF16)
        wdb_s[...] = wdb_ref[0].astype(BF16)

    def compute(fetch_next):
        slot = j % 2
        nslot = nxt % 2
        step = ROW_TILE // 8
        batches = iter(range(0, ROW_TILE, step))

        def fetch():
            if fetch_next:
                lo = next(batches)
                issue_rows(nxt, nslot, lo, lo + step)

        x = xbuf[slot, :, 0:D_MODEL].astype(BF16)
        gates = xbuf[slot, :, D_MODEL:H2_EXT]
        hg_a = _dot(x, wga_s[...])
        fetch()
        hu_a = _dot(x, wua_s[...])
        fetch()
        hg_b = _dot(x, wgb_s[...])
        fetch()
        hu_b = _dot(x, wub_s[...])
        fetch()
        act_a = (hg_a * jax.nn.sigmoid(hg_a) * hu_a * gates[:, 0:1]).astype(BF16)
        fetch()
        act_b = (hg_b * jax.nn.sigmoid(hg_b) * hu_b * gates[:, 1:2]).astype(BF16)
        fetch()
        y = _dot(act_a, wda_s[...])
        fetch()
        y = y + _dot(act_b, wdb_s[...])
        fetch()
        y_ref[...] = y

    @pl.when(valid & next_full)
    def _():
        compute(True)

    @pl.when(valid & jnp.logical_not(next_full))
    def _():
        compute(False)

    @pl.when(jnp.logical_not(valid))
    def _():
        y_ref[...] = jnp.zeros_like(y_ref)


def _moe(meta, pos, h2ext, w_gate, w_up, w_down):
    up_spec = lambda slot: pl.BlockSpec((1, D_MODEL, EXPERT_FF), lambda j, meta, pos: (meta[slot, j], 0, 0))
    down_spec = lambda slot: pl.BlockSpec((1, EXPERT_FF, D_MODEL), lambda j, meta, pos: (meta[slot, j], 0, 0))
    grid_spec = pltpu.PrefetchScalarGridSpec(
        num_scalar_prefetch=2,
        grid=(N_ROW_TILES,),
        in_specs=[pl.BlockSpec(memory_space=pl.ANY),
                  up_spec(0), up_spec(0), down_spec(0), up_spec(1), up_spec(1), down_spec(1)],
        out_specs=pl.BlockSpec((ROW_TILE, D_MODEL), lambda j, meta, pos: (j, 0)),
        scratch_shapes=[pltpu.SMEM((ROW_CAP,), jnp.int32),
                        pltpu.VMEM((2, ROW_TILE, H2_EXT), F32),
                        pltpu.SemaphoreType.DMA((2,)),
                        pltpu.VMEM((D_MODEL, EXPERT_FF), BF16), pltpu.VMEM((D_MODEL, EXPERT_FF), BF16),
                        pltpu.VMEM((EXPERT_FF, D_MODEL), BF16),
                        pltpu.VMEM((D_MODEL, EXPERT_FF), BF16), pltpu.VMEM((D_MODEL, EXPERT_FF), BF16),
                        pltpu.VMEM((EXPERT_FF, D_MODEL), BF16)])
    return pl.pallas_call(
        _moe_kernel,
        out_shape=jax.ShapeDtypeStruct((ROW_CAP, D_MODEL), F32),
        grid_spec=grid_spec,
        compiler_params=_cparams(("arbitrary",)),
        name="moe_experts",
    )(meta, pos, h2ext, w_gate, w_up, w_down, w_gate, w_up, w_down)


def _final_kernel(pos_ref, y_hbm, x1_ref, m_ref, gn_ref, op_ref, os_ref, ybuf, sem):
    i = pl.program_id(0)

    def row_copy(tile, r, slot):
        p = pos_ref[tile * TILE + r]
        return pltpu.make_async_copy(y_hbm.at[pl.ds(p, 1), :], ybuf.at[slot, pl.ds(r, 1), :], sem.at[slot])

    def issue(tile, slot):
        def body(r2, c):
            row_copy(tile, 2 * r2, slot).start(priority=0)
            row_copy(tile, 2 * r2 + 1, slot).start(priority=1)
            return c
        lax.fori_loop(0, TILE // 2, body, 0, unroll=4)

    def wait(slot):
        pltpu.make_async_copy(y_hbm.at[pl.ds(0, TILE)], ybuf.at[slot], sem.at[slot]).wait()

    @pl.when(i == 0)
    def _():
        issue(0, 0)

    @pl.when(i + 1 < N_TILES)
    def _():
        issue(i + 1, (i + 1) % 2)

    slot = i % 2
    wait(slot)
    out = x1_ref[...] + m_ref[0, 5:6, :] * _rms(ybuf[slot], gn_ref[3:4, :])

    @pl.when(i < N_TILES_P)
    def _():
        op_ref[...] = out

    @pl.when(i >= N_TILES_P)
    def _():
        os_ref[...] = out


def _final(pos, y_sorted, x1, mods3, g_norm):
    tps = DEC_SEQ // TILE
    grid_spec = pltpu.PrefetchScalarGridSpec(
        num_scalar_prefetch=1,
        grid=(N_TILES,),
        in_specs=[pl.BlockSpec(memory_space=pl.ANY),
                  pl.BlockSpec((TILE, D_MODEL), lambda i, pos: (i, 0)),
                  pl.BlockSpec((1, N_MOD, D_MODEL), lambda i, pos: (_mod_row_of_tile(i, tps, N_TILES_P), 0, 0)),
                  pl.BlockSpec((4, D_MODEL), lambda i, pos: (0, 0))],
        out_specs=(pl.BlockSpec((TILE, D_MODEL), lambda i, pos: (jnp.minimum(i, N_TILES_P - 1), 0)),
                   pl.BlockSpec((TILE, D_MODEL), lambda i, pos: (jnp.maximum(i - N_TILES_P, 0), 0))),
        scratch_shapes=[pltpu.VMEM((2, TILE, D_MODEL), F32), pltpu.SemaphoreType.DMA((2,))])
    return pl.pallas_call(
        _final_kernel,
        out_shape=(jax.ShapeDtypeStruct((T_PROMPT, D_MODEL), F32),
                   jax.ShapeDtypeStruct((T_SAMPLE, D_MODEL), F32)),
        grid_spec=grid_spec,
        compiler_params=_cparams(("arbitrary",)),
        name="moe_combine_final",
    )(pos, y_sorted, x1, mods3, g_norm)


def kernel(x_prompt, x_sample, state_C, state_n, state_m, c, c_ctx, w_ada, b_ada, g_norm, w_in, ml_gate_bias, ml_head_gain, hy_conv_w, hy_f_w1, hy_f_b1, hy_f_w2, hy_f_b2, hy_f_w3, hy_f_b3, hy_decay, hy_bias, w_out, w_rc, b_rc, w_rf, b_rf, w_gate, w_up, w_down):
    xp = x_prompt.reshape(T_PROMPT, D_MODEL)
    xs = x_sample.reshape(T_SAMPLE, D_MODEL)
    gn = g_norm[0]

    cv = jnp.concatenate([c_ctx[None, :], c, jnp.zeros((MOD_ROWS - 1 - DEC_BATCH, D_MODEL), F32)], axis=0)
    mods3 = _ada(cv, w_ada[0], b_ada[0]).reshape(MOD_ROWS, N_MOD, D_MODEL)

    w_in0 = w_in[0]
    w_qkvo, w_hy = _prep_in_weights(w_in0.T)
    wg = w_in0[:, ML_QKVO_COLS:ML_QKVO_COLS + ML_GATE_COLS]
    gbt = ml_gate_bias[0].reshape(ML_GATE_COLS, 1)
    proj, gates_t = _inproj(xp, xs, mods3, gn, w_qkvo, w_hy, wg.T, gbt)

    gain = ml_head_gain[0].reshape(1, ML_WIDTH)
    y_ml_p, c_new, n_new, m_new = _mlstm(proj, gates_t, gain, None, SEQ, BATCH, 0)
    state = (state_C[:, 0], state_n[:, 0], state_m[:, 0].reshape(DEC_BATCH, 2 * ML_HEADS, 1))
    y_ml_s, _, _, _ = _mlstm(proj, gates_t, gain, state, DEC_SEQ, DEC_BATCH, T_PROMPT // DEC_SEQ)

    w1p = jnp.pad(hy_f_w1[0], ((0, 128 - HY_EMB), (0, 0)))
    b1 = hy_f_b1[0].reshape(1, -1)
    b2 = hy_f_b2[0].reshape(1, -1)
    b3 = hy_f_b3[0].reshape(1, -1)
    dec = hy_decay[0].reshape(1, -1)
    z_parts = []
    for seq_len, n_seq, row_off, width, seqs in ((SEQ, BATCH, 0, SEQ, 4), (DEC_SEQ, DEC_BATCH, T_PROMPT, GRID_W, 2)):
        f, ft = _dft_mats(seq_len)
        coefs = _hyena_filters(seq_len, f, w1p, b1, hy_f_w2[0], b2, hy_f_w3[0], b3, dec)
        z_parts.append(_hyena(proj, hy_conv_w[0], coefs, hy_bias[0], f, ft, seq_len, n_seq, row_off, width, seqs))
    z_p, z_s = z_parts

    pad_r = ROUTER_ROWS - N_GROUPS - N_EXPERTS
    w_r = jnp.pad(jnp.concatenate([w_rc[0], w_rf[0]], axis=1).T, ((0, pad_r), (0, 0)))
    b_r = jnp.pad(jnp.concatenate([b_rc[0], b_rf[0]], axis=0), (0, pad_r)).reshape(ROUTER_ROWS, 1)
    x1, h2ext, bid = _outproj(xp, xs, y_ml_p, y_ml_s, z_p, z_s, mods3, gn, w_out[0].astype(BF16), w_r, b_r)

    pos3, meta = _route(bid)
    pos = pos3.reshape(T_ALL)
    y_sorted = _moe(meta, pos, h2ext, w_gate[0], w_up[0], w_down[0])
    y_p, y_s = _final(pos, y_sorted, x1, mods3, gn)

    new_c = c_new.reshape(BATCH, 1, 2, ML_HEADS, ML_HEAD_DIM, ML_HEAD_DIM)
    new_n = n_new.reshape(BATCH, 1, 2, ML_HEADS, ML_HEAD_DIM)
    new_m = m_new[:, :, 0].reshape(BATCH, 1, 2, ML_HEADS)
    return (y_p.reshape(BATCH, SEQ, D_MODEL), y_s.reshape(DEC_BATCH, DEC_SEQ, D_MODEL), new_c, new_n, new_m)
```
